```python
import jax, jax.numpy as jnp
from jax import lax
import numpy as np

D_MODEL = 1024
BATCH = 8
SEQ = 4096
DEPTH = 2

N_META = 16
N_MIXERS = 2
SB_HEADS = 16
SB_HEAD_DIM = D_MODEL // SB_HEADS
Q_BLOCK = 128
LRU_WIDTH = D_MODEL
LRU_BLOCKS = 8
LRU_BLOCK_DIM = LRU_WIDTH // LRU_BLOCKS
CONV_WIDTH = 4
LRU_C = 8.0
D_FF = 4 * D_MODEL
EPS = 1e-6
N_SB_LAYERS = (DEPTH + 1) // 2
N_LRU_LAYERS = DEPTH // 2

kernel_name = "hybrid_stickbreak_rglru_meta"


def rms_norm(x, g):
    xf = x.astype(jnp.float32)
    y = xf * lax.rsqrt(jnp.mean(xf * xf, axis=-1, keepdims=True) + EPS)
    return (y * g.astype(jnp.float32)).astype(x.dtype)


def _sb_block(q_blk, k, v, q_pos):
    t_len = k.shape[2]
    k_pos = jnp.arange(t_len)
    z = jnp.einsum('bhqd,bhkd->bhqk', q_blk, k).astype(jnp.float32) * (SB_HEAD_DIM ** -0.5)
    causal = k_pos[None, :] < q_pos[:, None]
    log_keep = jnp.where(causal, -jax.nn.softplus(z), 0.0)
    after = lax.cumsum(log_keep, axis=3, reverse=True) - log_keep
    w = jnp.where(causal, jnp.exp(jax.nn.log_sigmoid(z) + after), 0.0)
    return jnp.einsum('bhqk,bhkd->bhqd', w.astype(v.dtype), v)


def stick_breaking_mixer(x, w_qkv, w_o):
    b, t_len, _ = x.shape
    qkv = (x @ w_qkv).reshape(b, t_len, 3, SB_HEADS, SB_HEAD_DIM).transpose(2, 0, 3, 1, 4)
    q, k, v = qkv[0], qkv[1], qkv[2]
    meta_out = _sb_block(q[:, :, :N_META], k, v, jnp.arange(N_META))
    n_blk = (t_len - N_META) // Q_BLOCK
    q_real = q[:, :, N_META:].reshape(b, SB_HEADS, n_blk, Q_BLOCK, SB_HEAD_DIM).transpose(2, 0, 1, 3, 4)
    pos = N_META + jnp.arange(n_blk * Q_BLOCK).reshape(n_blk, Q_BLOCK)
    real_out = lax.map(lambda a: _sb_block(a[0], k, v, a[1]), (q_real, pos))
    real_out = real_out.transpose(1, 2, 0, 3, 4).reshape(b, SB_HEADS, n_blk * Q_BLOCK, SB_HEAD_DIM)
    o = jnp.concatenate([meta_out, real_out], axis=2)
    o = o.transpose(0, 2, 1, 3).reshape(b, t_len, D_MODEL)
    return o @ w_o


def _lru_combine(left, right):
    a1, b1 = left
    a2, b2 = right
    return a1 * a2, a2 * b1 + b2


def rglru_mixer(x, w_in, conv_w, conv_b, w_rg, b_rg, w_ig, b_ig, lam, w_out):
    b, t_len, _ = x.shape
    gate_in, rec_in = jnp.split(x @ w_in, 2, axis=-1)
    gate = jax.nn.gelu(gate_in)
    xp = jnp.pad(rec_in, ((0, 0), (CONV_WIDTH - 1, 0), (0, 0)))
    u = conv_b + sum(xp[:, j:j + t_len] * conv_w[j] for j in range(CONV_WIDTH))
    ub = u.reshape(b, t_len, LRU_BLOCKS, LRU_BLOCK_DIM)
    r = jax.nn.sigmoid(jnp.einsum('btni,nij->btnj', ub, w_rg).reshape(b, t_len, LRU_WIDTH) + b_rg)
    i = jax.nn.sigmoid(jnp.einsum('btni,nij->btnj', ub, w_ig).reshape(b, t_len, LRU_WIDTH) + b_ig)
    log_a = (-LRU_C * jax.nn.softplus(-lam.astype(jnp.float32))) * r.astype(jnp.float32)
    a = jnp.exp(log_a)
    mult = jnp.sqrt(-jnp.expm1(2.0 * log_a))
    bt = mult * (i * u).astype(jnp.float32)
    _, h = lax.associative_scan(_lru_combine, (a, bt), axis=1)
    y = h.astype(x.dtype) * gate
    return y @ w_out


def sq_relu_mlp(x, w_up, w_down):
    hdn = jax.nn.relu(x @ w_up)
    return (hdn * hdn) @ w_down


def _fwd_setup_inputs(seed: int = 0) -> dict:
    key = jax.random.key(seed)
    ks = jax.random.split(key, 20)
    f32 = jnp.float32
    D = D_MODEL

    def nrm(k, shape, scale):
        return jax.random.normal(k, shape, f32) * scale

    a0 = jax.random.uniform(ks[11], (N_LRU_LAYERS, LRU_WIDTH), f32, 0.9, 0.999)
    return {
        "x": nrm(ks[0], (BATCH, SEQ, D), 1.0),
        "meta_tokens": nrm(ks[1], (N_META, D), 1.0),
        "norm_mix": 1.0 + nrm(ks[2], (DEPTH, D), 0.02),
        "norm_mlp": 1.0 + nrm(ks[3], (DEPTH, D), 0.02),
        "sb_w_qkv": nrm(ks[4], (N_SB_LAYERS, D, 3 * D), D ** -0.5),
        "sb_w_o": nrm(ks[5], (N_SB_LAYERS, D, D), D ** -0.5),
        "lru_w_in": nrm(ks[6], (N_LRU_LAYERS, D, 2 * LRU_WIDTH), D ** -0.5),
        "lru_conv_w": nrm(ks[7], (N_LRU_LAYERS, CONV_WIDTH, LRU_WIDTH), CONV_WIDTH ** -0.5),
        "lru_conv_b": nrm(ks[8], (N_LRU_LAYERS, LRU_WIDTH), 0.01),
        "lru_w_rg": nrm(ks[9], (N_LRU_LAYERS, LRU_BLOCKS, LRU_BLOCK_DIM, LRU_BLOCK_DIM), LRU_BLOCK_DIM ** -0.5),
        "lru_b_rg": nrm(ks[10], (N_LRU_LAYERS, LRU_WIDTH), 0.01),
        "lru_w_ig": nrm(ks[12], (N_LRU_LAYERS, LRU_BLOCKS, LRU_BLOCK_DIM, LRU_BLOCK_DIM), LRU_BLOCK_DIM ** -0.5),
        "lru_b_ig": nrm(ks[13], (N_LRU_LAYERS, LRU_WIDTH), 0.01),
        "lru_lambda": jnp.log(a0) - jnp.log1p(-a0),
        "lru_w_out": nrm(ks[14], (N_LRU_LAYERS, LRU_WIDTH, D), LRU_WIDTH ** -0.5),
        "mlp_w_up": nrm(ks[15], (DEPTH, D, D_FF), D ** -0.5),
        "mlp_w_down": nrm(ks[16], (DEPTH, D_FF, D), D_FF ** -0.5),
        "norm_final": 1.0 + nrm(ks[17], (D,), 0.02),
    }


def _fwd_reference(x, meta_tokens, norm_mix, norm_mlp, sb_w_qkv, sb_w_o, lru_w_in, lru_conv_w,
              lru_conv_b, lru_w_rg, lru_b_rg, lru_w_ig, lru_b_ig, lru_lambda, lru_w_out,
              mlp_w_up, mlp_w_down, norm_final):
    b = x.shape[0]
    meta = jnp.broadcast_to(meta_tokens[None].astype(x.dtype), (b, N_META, D_MODEL))
    h = jnp.concatenate([meta, x], axis=1)
    for i in range(DEPTH):
        hn = rms_norm(h, norm_mix[i])
        j = i // N_MIXERS
        if i % N_MIXERS == 0:
            h = h + stick_breaking_mixer(hn, sb_w_qkv[j], sb_w_o[j])
        else:
            h = h + rglru_mixer(hn, lru_w_in[j], lru_conv_w[j], lru_conv_b[j], lru_w_rg[j],
                                lru_b_rg[j], lru_w_ig[j], lru_b_ig[j], lru_lambda[j], lru_w_out[j])
        hn = rms_norm(h, norm_mlp[i])
        h = h + sq_relu_mlp(hn, mlp_w_up[i], mlp_w_down[i])
    h = rms_norm(h, norm_final)
    return h[:, N_META:]


import jax as _jax
import jax.numpy as _jnp

TWIN_FORMAT = 'train_step'
FWD_PARAMS = ['x', 'meta_tokens', 'norm_mix', 'norm_mlp', 'sb_w_qkv', 'sb_w_o', 'lru_w_in', 'lru_conv_w', 'lru_conv_b', 'lru_w_rg', 'lru_b_rg', 'lru_w_ig', 'lru_b_ig', 'lru_lambda', 'lru_w_out', 'mlp_w_up', 'mlp_w_down', 'norm_final']
TWIN_WEIGHTS = ['meta_tokens', 'norm_mix', 'norm_mlp', 'sb_w_qkv', 'sb_w_o', 'lru_w_in', 'lru_conv_w', 'lru_conv_b', 'lru_w_rg', 'lru_b_rg', 'lru_w_ig', 'lru_b_ig', 'lru_lambda', 'lru_w_out', 'mlp_w_up', 'mlp_w_down', 'norm_final']
TWIN_DIFF_INPUT = 'x'
TWIN_INPUTS = ['x', 'meta_tokens', 'norm_mix', 'norm_mlp', 'sb_w_qkv', 'sb_w_o', 'lru_w_in', 'lru_conv_w', 'lru_conv_b', 'lru_w_rg', 'lru_b_rg', 'lru_w_ig', 'lru_b_ig', 'lru_lambda', 'lru_w_out', 'mlp_w_up', 'mlp_w_down', 'norm_final', 'loss_target', 'm_meta_tokens', 'm_norm_mix', 'm_norm_mlp', 'm_sb_w_qkv', 'm_sb_w_o', 'm_lru_w_in', 'm_lru_conv_w', 'm_lru_conv_b', 'm_lru_w_rg', 'm_lru_b_rg', 'm_lru_w_ig', 'm_lru_b_ig', 'm_lru_lambda', 'm_lru_w_out', 'm_mlp_w_up', 'm_mlp_w_down', 'm_norm_final', 'v_meta_tokens', 'v_norm_mix', 'v_norm_mlp', 'v_sb_w_qkv', 'v_sb_w_o', 'v_lru_w_in', 'v_lru_conv_w', 'v_lru_conv_b', 'v_lru_w_rg', 'v_lru_b_rg', 'v_lru_w_ig', 'v_lru_b_ig', 'v_lru_lambda', 'v_lru_w_out', 'v_mlp_w_up', 'v_mlp_w_down', 'v_norm_final']
TWIN_OUTPUTS = ['loss', 'grad_x', 'grad_meta_tokens', 'grad_norm_mix', 'grad_norm_mlp', 'grad_sb_w_qkv', 'grad_sb_w_o', 'grad_lru_w_in', 'grad_lru_conv_w', 'grad_lru_conv_b', 'grad_lru_w_rg', 'grad_lru_b_rg', 'grad_lru_w_ig', 'grad_lru_b_ig', 'grad_lru_lambda', 'grad_lru_w_out', 'grad_mlp_w_up', 'grad_mlp_w_down', 'grad_norm_final', 'delta_meta_tokens', 'delta_norm_mix', 'delta_norm_mlp', 'delta_sb_w_qkv', 'delta_sb_w_o', 'delta_lru_w_in', 'delta_lru_conv_w', 'delta_lru_conv_b', 'delta_lru_w_rg', 'delta_lru_b_rg', 'delta_lru_w_ig', 'delta_lru_b_ig', 'delta_lru_lambda', 'delta_lru_w_out', 'delta_mlp_w_up', 'delta_mlp_w_down', 'delta_norm_final', 'new_m_meta_tokens', 'new_m_norm_mix', 'new_m_norm_mlp', 'new_m_sb_w_qkv', 'new_m_sb_w_o', 'new_m_lru_w_in', 'new_m_lru_conv_w', 'new_m_lru_conv_b', 'new_m_lru_w_rg', 'new_m_lru_b_rg', 'new_m_lru_w_ig', 'new_m_lru_b_ig', 'new_m_lru_lambda', 'new_m_lru_w_out', 'new_m_mlp_w_up', 'new_m_mlp_w_down', 'new_m_norm_final', 'new_v_meta_tokens', 'new_v_norm_mix', 'new_v_norm_mlp', 'new_v_sb_w_qkv', 'new_v_sb_w_o', 'new_v_lru_w_in', 'new_v_lru_conv_w', 'new_v_lru_conv_b', 'new_v_lru_w_rg', 'new_v_lru_b_rg', 'new_v_lru_w_ig', 'new_v_lru_b_ig', 'new_v_lru_lambda', 'new_v_lru_w_out', 'new_v_mlp_w_up', 'new_v_mlp_w_down', 'new_v_norm_final']
TWIN_LEAF_KINDS = {'loss': 'loss', 'grad_x': 'grad_x', 'grad_meta_tokens': 'grad_w', 'grad_norm_mix': 'grad_w', 'grad_norm_mlp': 'grad_w', 'grad_sb_w_qkv': 'grad_w', 'grad_sb_w_o': 'grad_w', 'grad_lru_w_in': 'grad_w', 'grad_lru_conv_w': 'grad_w', 'grad_lru_conv_b': 'grad_w', 'grad_lru_w_rg': 'grad_w', 'grad_lru_b_rg': 'grad_w', 'grad_lru_w_ig': 'grad_w', 'grad_lru_b_ig': 'grad_w', 'grad_lru_lambda': 'grad_w', 'grad_lru_w_out': 'grad_w', 'grad_mlp_w_up': 'grad_w', 'grad_mlp_w_down': 'grad_w', 'grad_norm_final': 'grad_w', 'delta_meta_tokens': 'delta_w', 'delta_norm_mix': 'delta_w', 'delta_norm_mlp': 'delta_w', 'delta_sb_w_qkv': 'delta_w', 'delta_sb_w_o': 'delta_w', 'delta_lru_w_in': 'delta_w', 'delta_lru_conv_w': 'delta_w', 'delta_lru_conv_b': 'delta_w', 'delta_lru_w_rg': 'delta_w', 'delta_lru_b_rg': 'delta_w', 'delta_lru_w_ig': 'delta_w', 'delta_lru_b_ig': 'delta_w', 'delta_lru_lambda': 'delta_w', 'delta_lru_w_out': 'delta_w', 'delta_mlp_w_up': 'delta_w', 'delta_mlp_w_down': 'delta_w', 'delta_norm_final': 'delta_w', 'new_m_meta_tokens': 'new_m', 'new_m_norm_mix': 'new_m', 'new_m_norm_mlp': 'new_m', 'new_m_sb_w_qkv': 'new_m', 'new_m_sb_w_o': 'new_m', 'new_m_lru_w_in': 'new_m', 'new_m_lru_conv_w': 'new_m', 'new_m_lru_conv_b': 'new_m', 'new_m_lru_w_rg': 'new_m', 'new_m_lru_b_rg': 'new_m', 'new_m_lru_w_ig': 'new_m', 'new_m_lru_b_ig': 'new_m', 'new_m_lru_lambda': 'new_m', 'new_m_lru_w_out': 'new_m', 'new_m_mlp_w_up': 'new_m', 'new_m_mlp_w_down': 'new_m', 'new_m_norm_final': 'new_m', 'new_v_meta_tokens': 'new_v', 'new_v_norm_mix': 'new_v', 'new_v_norm_mlp': 'new_v', 'new_v_sb_w_qkv': 'new_v', 'new_v_sb_w_o': 'new_v', 'new_v_lru_w_in': 'new_v', 'new_v_lru_conv_w': 'new_v', 'new_v_lru_conv_b': 'new_v', 'new_v_lru_w_rg': 'new_v', 'new_v_lru_b_rg': 'new_v', 'new_v_lru_w_ig': 'new_v', 'new_v_lru_b_ig': 'new_v', 'new_v_lru_lambda': 'new_v', 'new_v_lru_w_out': 'new_v', 'new_v_mlp_w_up': 'new_v', 'new_v_mlp_w_down': 'new_v', 'new_v_norm_final': 'new_v'}


def _forward(args):
    return _fwd_reference(*[args[k] for k in FWD_PARAMS])


def _output_shape():
    def fwd():
        inp = _fwd_setup_inputs(0)
        return _fwd_reference(*[inp[k] for k in FWD_PARAMS])
    out = _jax.eval_shape(fwd)
    return out.shape, out.dtype

N_MICROBATCH = 1
ADAM_LR = 0.001
ADAM_B1 = 0.9
ADAM_B2 = 0.999
ADAM_EPS = 1e-08
ADAM_WD = 0.01
ADAM_STEP = 10
PER_EXAMPLE_BATCH_AXIS = {'x': 0, 'loss_target': 0}
SHARED_INPUTS = []
_WEIGHT_DTYPES = {'meta_tokens': _jnp.float32, 'norm_mix': _jnp.float32, 'norm_mlp': _jnp.float32, 'sb_w_qkv': _jnp.float32, 'sb_w_o': _jnp.float32, 'lru_w_in': _jnp.float32, 'lru_conv_w': _jnp.float32, 'lru_conv_b': _jnp.float32, 'lru_w_rg': _jnp.float32, 'lru_b_rg': _jnp.float32, 'lru_w_ig': _jnp.float32, 'lru_b_ig': _jnp.float32, 'lru_lambda': _jnp.float32, 'lru_w_out': _jnp.float32, 'mlp_w_up': _jnp.float32, 'mlp_w_down': _jnp.float32, 'norm_final': _jnp.float32}
MOMENT_SCALE = {'meta_tokens': 2.803214e-03, 'norm_mix': 1.120803e-01, 'norm_mlp': 1.501723e-01, 'sb_w_qkv': 7.825729e-02, 'sb_w_o': 1.139334e-01, 'lru_w_in': 6.839698e-02, 'lru_conv_w': 6.453024e-02, 'lru_conv_b': 1.975708e-01, 'lru_w_rg': 9.784697e-03, 'lru_b_rg': 1.412402e-02, 'lru_w_ig': 1.782528e-02, 'lru_b_ig': 2.396021e-02, 'lru_lambda': 3.132320e-02, 'lru_w_out': 6.840883e-02, 'mlp_w_up': 7.093093e-02, 'mlp_w_down': 1.521515e-01, 'norm_final': 3.247379e+01}


def _to_microbatches(a, axis):
    t = _jnp.moveaxis(a, axis, 0)
    t = t.reshape((N_MICROBATCH, t.shape[0] // N_MICROBATCH) + t.shape[1:])
    return _jnp.moveaxis(t, 1, axis + 1)


def setup_inputs(seed: int = 0) -> dict:
    inp = _fwd_setup_inputs(seed)
    key = _jax.random.fold_in(_jax.random.key(seed), 7919)
    shape, _ = _output_shape()
    out = dict(inp)
    out["loss_target"] = _jax.random.normal(_jax.random.fold_in(key, 0), shape, _jnp.float32)
    for i, name in enumerate(TWIN_WEIGHTS):
        w = inp[name].astype(_jnp.float32)
        if MOMENT_SCALE is None:
            s = _jnp.sqrt(_jnp.mean(_jnp.square(w)) + 1e-30)
        else:
            s = MOMENT_SCALE[name]
        km, kv = _jax.random.split(_jax.random.fold_in(key, i + 1))
        out[name] = w
        out["m_" + name] = s * _jax.random.normal(km, w.shape, _jnp.float32)
        out["v_" + name] = (s * s) * _jax.random.uniform(kv, w.shape, _jnp.float32, 0.5, 1.5)
    if N_MICROBATCH > 1:
        for name, axis in PER_EXAMPLE_BATCH_AXIS.items():
            out[name] = _to_microbatches(out[name], axis)
    return {'x': out['x'], 'meta_tokens': out['meta_tokens'], 'norm_mix': out['norm_mix'], 'norm_mlp': out['norm_mlp'], 'sb_w_qkv': out['sb_w_qkv'], 'sb_w_o': out['sb_w_o'], 'lru_w_in': out['lru_w_in'], 'lru_conv_w': out['lru_conv_w'], 'lru_conv_b': out['lru_conv_b'], 'lru_w_rg': out['lru_w_rg'], 'lru_b_rg': out['lru_b_rg'], 'lru_w_ig': out['lru_w_ig'], 'lru_b_ig': out['lru_b_ig'], 'lru_lambda': out['lru_lambda'], 'lru_w_out': out['lru_w_out'], 'mlp_w_up': out['mlp_w_up'], 'mlp_w_down': out['mlp_w_down'], 'norm_final': out['norm_final'], 'loss_target': out['loss_target'], 'm_meta_tokens': out['m_meta_tokens'], 'm_norm_mix': out['m_norm_mix'], 'm_norm_mlp': out['m_norm_mlp'], 'm_sb_w_qkv': out['m_sb_w_qkv'], 'm_sb_w_o': out['m_sb_w_o'], 'm_lru_w_in': out['m_lru_w_in'], 'm_lru_conv_w': out['m_lru_conv_w'], 'm_lru_conv_b': out['m_lru_conv_b'], 'm_lru_w_rg': out['m_lru_w_rg'], 'm_lru_b_rg': out['m_lru_b_rg'], 'm_lru_w_ig': out['m_lru_w_ig'], 'm_lru_b_ig': out['m_lru_b_ig'], 'm_lru_lambda': out['m_lru_lambda'], 'm_lru_w_out': out['m_lru_w_out'], 'm_mlp_w_up': out['m_mlp_w_up'], 'm_mlp_w_down': out['m_mlp_w_down'], 'm_norm_final': out['m_norm_final'], 'v_meta_tokens': out['v_meta_tokens'], 'v_norm_mix': out['v_norm_mix'], 'v_norm_mlp': out['v_norm_mlp'], 'v_sb_w_qkv': out['v_sb_w_qkv'], 'v_sb_w_o': out['v_sb_w_o'], 'v_lru_w_in': out['v_lru_w_in'], 'v_lru_conv_w': out['v_lru_conv_w'], 'v_lru_conv_b': out['v_lru_conv_b'], 'v_lru_w_rg': out['v_lru_w_rg'], 'v_lru_b_rg': out['v_lru_b_rg'], 'v_lru_w_ig': out['v_lru_w_ig'], 'v_lru_b_ig': out['v_lru_b_ig'], 'v_lru_lambda': out['v_lru_lambda'], 'v_lru_w_out': out['v_lru_w_out'], 'v_mlp_w_up': out['v_mlp_w_up'], 'v_mlp_w_down': out['v_mlp_w_down'], 'v_norm_final': out['v_norm_final']}


def _loss(weights, diff, rest, loss_target):
    with _jax.named_scope("forward"):
        args = {**rest, TWIN_DIFF_INPUT: diff, **{k: w.astype(_WEIGHT_DTYPES[k]) for k, w in weights.items()}}
        y = _forward(args)
    with _jax.named_scope("loss_head"):
        err = _jnp.square(y.astype(_jnp.float32) - loss_target)
        return 0.5 * _jnp.sum(_jnp.mean(err, axis=-1)) if err.ndim else 0.5 * err


def _adamw(w, g, m, v):
    m = ADAM_B1 * m + (1.0 - ADAM_B1) * g
    v = ADAM_B2 * v + (1.0 - ADAM_B2) * _jnp.square(g)
    m_hat = m / (1.0 - ADAM_B1 ** ADAM_STEP)
    v_hat = v / (1.0 - ADAM_B2 ** ADAM_STEP)
    delta = -ADAM_LR * (m_hat / (_jnp.sqrt(v_hat) + ADAM_EPS) + ADAM_WD * w)
    return delta, m, v


def reference(x, meta_tokens, norm_mix, norm_mlp, sb_w_qkv, sb_w_o, lru_w_in, lru_conv_w, lru_conv_b, lru_w_rg, lru_b_rg, lru_w_ig, lru_b_ig, lru_lambda, lru_w_out, mlp_w_up, mlp_w_down, norm_final, loss_target, m_meta_tokens, m_norm_mix, m_norm_mlp, m_sb_w_qkv, m_sb_w_o, m_lru_w_in, m_lru_conv_w, m_lru_conv_b, m_lru_w_rg, m_lru_b_rg, m_lru_w_ig, m_lru_b_ig, m_lru_lambda, m_lru_w_out, m_mlp_w_up, m_mlp_w_down, m_norm_final, v_meta_tokens, v_norm_mix, v_norm_mlp, v_sb_w_qkv, v_sb_w_o, v_lru_w_in, v_lru_conv_w, v_lru_conv_b, v_lru_w_rg, v_lru_b_rg, v_lru_w_ig, v_lru_b_ig, v_lru_lambda, v_lru_w_out, v_mlp_w_up, v_mlp_w_down, v_norm_final):
    given = dict(x=x, meta_tokens=meta_tokens, norm_mix=norm_mix, norm_mlp=norm_mlp, sb_w_qkv=sb_w_qkv, sb_w_o=sb_w_o, lru_w_in=lru_w_in, lru_conv_w=lru_conv_w, lru_conv_b=lru_conv_b, lru_w_rg=lru_w_rg, lru_b_rg=lru_b_rg, lru_w_ig=lru_w_ig, lru_b_ig=lru_b_ig, lru_lambda=lru_lambda, lru_w_out=lru_w_out, mlp_w_up=mlp_w_up, mlp_w_down=mlp_w_down, norm_final=norm_final, loss_target=loss_target, m_meta_tokens=m_meta_tokens, m_norm_mix=m_norm_mix, m_norm_mlp=m_norm_mlp, m_sb_w_qkv=m_sb_w_qkv, m_sb_w_o=m_sb_w_o, m_lru_w_in=m_lru_w_in, m_lru_conv_w=m_lru_conv_w, m_lru_conv_b=m_lru_conv_b, m_lru_w_rg=m_lru_w_rg, m_lru_b_rg=m_lru_b_rg, m_lru_w_ig=m_lru_w_ig, m_lru_b_ig=m_lru_b_ig, m_lru_lambda=m_lru_lambda, m_lru_w_out=m_lru_w_out, m_mlp_w_up=m_mlp_w_up, m_mlp_w_down=m_mlp_w_down, m_norm_final=m_norm_final, v_meta_tokens=v_meta_tokens, v_norm_mix=v_norm_mix, v_norm_mlp=v_norm_mlp, v_sb_w_qkv=v_sb_w_qkv, v_sb_w_o=v_sb_w_o, v_lru_w_in=v_lru_w_in, v_lru_conv_w=v_lru_conv_w, v_lru_conv_b=v_lru_conv_b, v_lru_w_rg=v_lru_w_rg, v_lru_b_rg=v_lru_b_rg, v_lru_w_ig=v_lru_w_ig, v_lru_b_ig=v_lru_b_ig, v_lru_lambda=v_lru_lambda, v_lru_w_out=v_lru_w_out, v_mlp_w_up=v_mlp_w_up, v_mlp_w_down=v_mlp_w_down, v_norm_final=v_norm_final)
    weights = {n: given[n] for n in TWIN_WEIGHTS}
    shared = {n: given[n] for n in SHARED_INPUTS}
    per_example = {n: given[n] for n in ['x']}
    grad_fn = _jax.value_and_grad(_loss, argnums=(0, 1))

    def one_microbatch(ex, loss_target):
        ex = dict(ex)
        diff = ex.pop(TWIN_DIFF_INPUT)
        return grad_fn(weights, diff, {**shared, **ex}, loss_target)

    if N_MICROBATCH == 1:
        loss, (grad_w, grad_x) = one_microbatch(per_example, given["loss_target"])
    else:
        def body(carry, xs):
            loss_sum, grad_sum = carry
            l_k, (gw_k, gx_k) = one_microbatch(xs[0], xs[1])
            with _jax.named_scope("update"):
                return (loss_sum + l_k, _jax.tree.map(_jnp.add, grad_sum, gw_k)), gx_k

        init = (_jnp.zeros((), _jnp.float32), _jax.tree.map(_jnp.zeros_like, weights))
        (loss, grad_w), grad_x = _jax.lax.scan(body, init, (per_example, given["loss_target"]))
    with _jax.named_scope("update"):
        delta_w, new_m, new_v = {}, {}, {}
        for n in TWIN_WEIGHTS:
            delta_w[n], new_m[n], new_v[n] = _adamw(weights[n], grad_w[n], given["m_" + n], given["v_" + n])
    return (loss, grad_x, *[grad_w[n] for n in TWIN_WEIGHTS], *[delta_w[n] for n in TWIN_WEIGHTS],
            *[new_m[n] for n in TWIN_WEIGHTS], *[new_v[n] for n in TWIN_WEIGHTS])
```

```python
import functools

import jax
import jax.numpy as jnp
from jax import lax
from jax.experimental import pallas as pl
from jax.experimental.pallas import tpu as pltpu

F32 = jnp.float32
BF16 = jnp.bfloat16
MESH = pl.DeviceIdType.MESH

EPS = 1e-6
HEAD_DIM = 64
LANE = 128
SUBLANE = 8
LRU_C = 8.0
VMEM_LIMIT = 56 * 1024 * 1024

ADAM_LR = 0.001
ADAM_B1 = 0.9
ADAM_B2 = 0.999
ADAM_EPS = 1e-08
ADAM_WD = 0.01
ADAM_STEP = 10


def _pcall(body, **kw):
    return pl.pallas_call(body, **kw)


def _params(*sem):
    return pltpu.CompilerParams(dimension_semantics=sem, vmem_limit_bytes=VMEM_LIMIT)


def _tile(n, pref, align):
    best = None
    for t in range(align, min(n, pref) + 1, align):
        if n % t == 0:
            best = t
    return n if best is None else best


def _sds(shape, dtype):
    return jax.ShapeDtypeStruct(shape, dtype)


def _rstd(x):
    return lax.rsqrt(jnp.mean(x * x, axis=-1, keepdims=True) + EPS)


def _norm_bwd(x, g, dy):
    rstd = _rstd(x)
    n = x * rstd
    dn = dy * g
    dx = rstd * (dn - n * jnp.mean(dn * n, axis=-1, keepdims=True))
    dg = jnp.sum(dy * n, axis=0, keepdims=True)
    return dx, dg


def _softplus_parts(z):
    l1p = jnp.log(1.0 + jnp.exp(-jnp.abs(z)))
    return jnp.maximum(z, 0.0) + l1p, jnp.minimum(z, 0.0) - l1p


def _sigmoid(x):
    return 1.0 / (1.0 + jnp.exp(-x))


def _gelu_parts(x):
    k = 0.7978845608028654
    inner = k * (x + 0.044715 * (x * x * x))
    t = jnp.tanh(inner)
    gelu = 0.5 * x * (1.0 + t)
    dgelu = 0.5 * (1.0 + t) + 0.5 * x * (1.0 - t * t) * (k * (1.0 + 3.0 * 0.044715 * (x * x)))
    return gelu, dgelu


def _norm_mm(h, g, w, *, out_dtype, name):
    T, D = h.shape
    S, _, n = w.shape
    tm = _tile(T, 1056, 16)
    tn = _tile(n, 768, LANE)
    nj = n // tn

    def body(h_ref, g_ref, w_ref, hn_ref, o_ref):
        @pl.when(pl.program_id(1) == 0)
        def _():
            x = h_ref[...]
            hn_ref[...] = (x * _rstd(x) * g_ref[...]).astype(BF16)

        o_ref[...] = jnp.dot(hn_ref[...], w_ref[...], preferred_element_type=F32).astype(out_dtype)

    return _pcall(
        body, name=name, grid=(T // tm, S * nj),
        in_specs=[pl.BlockSpec((tm, D), lambda i, j: (i, 0)),
                  pl.BlockSpec((1, D), lambda i, j: (0, 0)),
                  pl.BlockSpec((None, D, tn), lambda i, j: (j // nj, 0, j % nj))],
        out_specs=[pl.BlockSpec((tm, D), lambda i, j: (i, 0)),
                   pl.BlockSpec((tm, tn), lambda i, j: (i, j))],
        out_shape=[_sds((T, D), BF16), _sds((T, S * n), out_dtype)],
        compiler_params=_params("parallel", "arbitrary"),
    )(h, g, w)


def _mm_res(a, w, res, *, name):
    T, K = a.shape
    N = w.shape[1]
    tm = _tile(T, 1056, 16)

    def body(a_ref, w_ref, r_ref, o_ref):
        o_ref[...] = r_ref[...] + jnp.dot(a_ref[...], w_ref[...], preferred_element_type=F32)

    return _pcall(
        body, name=name, grid=(T // tm,),
        in_specs=[pl.BlockSpec((tm, K), lambda i: (i, 0)),
                  pl.BlockSpec((K, N), lambda i: (0, 0)),
                  pl.BlockSpec((tm, N), lambda i: (i, 0))],
        out_specs=pl.BlockSpec((tm, N), lambda i: (i, 0)),
        out_shape=_sds((T, N), F32),
        compiler_params=_params("parallel"),
    )(a, w, res)


def _mm_nt(a, w, *, out_dtype, name):
    T, N = a.shape
    K = w.shape[0]
    tm = _tile(T, 1056, 16)

    def body(a_ref, w_ref, o_ref):
        o_ref[...] = lax.dot_general(a_ref[...], w_ref[...], (((1,), (1,)), ((), ())),
                                     preferred_element_type=F32).astype(out_dtype)

    return _pcall(
        body, name=name, grid=(T // tm,),
        in_specs=[pl.BlockSpec((tm, N), lambda i: (i, 0)),
                  pl.BlockSpec((K, N), lambda i: (0, 0))],
        out_specs=pl.BlockSpec((tm, K), lambda i: (i, 0)),
        out_shape=_sds((T, K), out_dtype),
        compiler_params=_params("parallel"),
    )(a, w)


def _mm_tn(a, b, *, shards, relu2, name):
    T, Ka = a.shape
    Nb = b.shape[1]
    n = Nb // shards
    tka = _tile(Ka, 512, LANE)
    tnb = _tile(n, 512, LANE)
    nj = n // tnb

    def body(a_ref, b_ref, o_ref):
        av = a_ref[...]
        if relu2:
            r = jnp.maximum(av, 0)
            av = r * r
        o_ref[...] = lax.dot_general(av, b_ref[...], (((0,), (0,)), ((), ())),
                                     preferred_element_type=F32)

    return _pcall(
        body, name=name, grid=(Ka // tka, shards * nj),
        in_specs=[pl.BlockSpec((T, tka), lambda i, j: (0, i)),
                  pl.BlockSpec((T, tnb), lambda i, j: (0, j))],
        out_specs=pl.BlockSpec((None, tka, tnb), lambda i, j: (j // nj, i, j % nj)),
        out_shape=_sds((shards, Ka, n), F32),
        compiler_params=_params("parallel", "parallel"),
    )(a, b)


def _mm_nt_normbwd(dy, w, h, g, dres, *, name):
    T, D = h.shape
    S, _, n = w.shape
    tm = _tile(T, 528, 16)

    def body(dy_ref, w_ref, h_ref, g_ref, dr_ref, dh_ref, dhb_ref, dg_ref, acc_ref):
        i, s = pl.program_id(0), pl.program_id(1)
        part = lax.dot_general(dy_ref[...], w_ref[...], (((1,), (1,)), ((), ())),
                               preferred_element_type=F32)

        @pl.when(s == 0)
        def _():
            acc_ref[...] = part

        @pl.when(s > 0)
        def _():
            acc_ref[...] += part

        @pl.when(s == S - 1)
        def _():
            dx, dg = _norm_bwd(h_ref[...], g_ref[...], acc_ref[...])
            dh = dr_ref[...] + dx
            dh_ref[...] = dh
            dhb_ref[...] = dh.astype(BF16)

            @pl.when(i == 0)
            def _():
                dg_ref[...] = dg

            @pl.when(i > 0)
            def _():
                dg_ref[...] += dg

    return _pcall(
        body, name=name, grid=(T // tm, S),
        in_specs=[pl.BlockSpec((tm, n), lambda i, s: (i, s)),
                  pl.BlockSpec((None, D, n), lambda i, s: (s, 0, 0)),
                  pl.BlockSpec((tm, D), lambda i, s: (i, 0)),
                  pl.BlockSpec((1, D), lambda i, s: (0, 0)),
                  pl.BlockSpec((tm, D), lambda i, s: (i, 0))],
        out_specs=[pl.BlockSpec((tm, D), lambda i, s: (i, 0)),
                   pl.BlockSpec((tm, D), lambda i, s: (i, 0)),
                   pl.BlockSpec((1, D), lambda i, s: (0, 0))],
        out_shape=[_sds((T, D), F32), _sds((T, D), BF16), _sds((1, D), F32)],
        scratch_shapes=[pltpu.VMEM((tm, D), F32)],
        compiler_params=_params("arbitrary", "arbitrary"),
    )(dy, w, h, g, dres)


def _mlp_fwd(h, g, w_up, w_down, *, name):
    T, D = h.shape
    S, _, n = w_up.shape
    tm = _tile(T, 528, 16)
    tf = _tile(n, 512, LANE)
    nj = n // tf
    nf = S * nj

    def body(h_ref, g_ref, wu_ref, wd_ref, o_ref, hn_ref, up_ref, acc_ref):
        f = pl.program_id(1)

        @pl.when(f == 0)
        def _():
            x = h_ref[...]
            hn_ref[...] = (x * _rstd(x) * g_ref[...]).astype(BF16)

        up = jnp.dot(hn_ref[...], wu_ref[...], preferred_element_type=F32)
        up_ref[...] = up.astype(BF16)
        r = jnp.maximum(up, 0.0)
        part = jnp.dot((r * r).astype(BF16), wd_ref[...], preferred_element_type=F32)

        @pl.when(f == 0)
        def _():
            acc_ref[...] = part

        @pl.when(f > 0)
        def _():
            acc_ref[...] += part

        @pl.when(f == nf - 1)
        def _():
            o_ref[...] = h_ref[...] + acc_ref[...]

    return _pcall(
        body, name=name, grid=(T // tm, nf),
        in_specs=[pl.BlockSpec((tm, D), lambda i, f: (i, 0)),
                  pl.BlockSpec((1, D), lambda i, f: (0, 0)),
                  pl.BlockSpec((None, D, tf), lambda i, f: (f // nj, 0, f % nj)),
                  pl.BlockSpec((tf, D), lambda i, f: (f, 0))],
        out_specs=[pl.BlockSpec((tm, D), lambda i, f: (i, 0)),
                   pl.BlockSpec((tm, D), lambda i, f: (i, 0)),
                   pl.BlockSpec((tm, tf), lambda i, f: (i, f))],
        out_shape=[_sds((T, D), F32), _sds((T, D), BF16), _sds((T, S * n), BF16)],
        scratch_shapes=[pltpu.VMEM((tm, D), F32)],
        compiler_params=_params("parallel", "arbitrary"),
    )(h, g, w_up, w_down)


def _mlp_bwd(dy, h, g, up, w_up, w_down, *, name):
    T, D = h.shape
    S, _, n = w_up.shape
    tm = _tile(T, 528, 16)
    tf = _tile(n, 512, LANE)
    nj = n // tf
    nf = S * nj

    def body(dy_ref, h_ref, g_ref, up_ref, wu_ref, wd_ref, dup_ref, dh_ref, dhb_ref, dg_ref,
             dyb_ref, acc_ref):
        i, f = pl.program_id(0), pl.program_id(1)

        @pl.when(f == 0)
        def _():
            dyb_ref[...] = dy_ref[...].astype(BF16)

        dact = lax.dot_general(dyb_ref[...], wd_ref[...], (((1,), (1,)), ((), ())),
                               preferred_element_type=F32)
        r = jnp.maximum(up_ref[...].astype(F32), 0.0)
        dup = (dact * (2.0 * r)).astype(BF16)
        dup_ref[...] = dup
        part = lax.dot_general(dup, wu_ref[...], (((1,), (1,)), ((), ())),
                               preferred_element_type=F32)

        @pl.when(f == 0)
        def _():
            acc_ref[...] = part

        @pl.when(f > 0)
        def _():
            acc_ref[...] += part

        @pl.when(f == nf - 1)
        def _():
            dx, dg = _norm_bwd(h_ref[...], g_ref[...], acc_ref[...])
            dh = dy_ref[...] + dx
            dh_ref[...] = dh
            dhb_ref[...] = dh.astype(BF16)

            @pl.when(i == 0)
            def _():
                dg_ref[...] = dg

            @pl.when(i > 0)
            def _():
                dg_ref[...] += dg

    return _pcall(
        body, name=name, grid=(T // tm, nf),
        in_specs=[pl.BlockSpec((tm, D), lambda i, f: (i, 0)),
                  pl.BlockSpec((tm, D), lambda i, f: (i, 0)),
                  pl.BlockSpec((1, D), lambda i, f: (0, 0)),
                  pl.BlockSpec((tm, tf), lambda i, f: (i, f)),
                  pl.BlockSpec((None, D, tf), lambda i, f: (f // nj, 0, f % nj)),
                  pl.BlockSpec((tf, D), lambda i, f: (f, 0))],
        out_specs=[pl.BlockSpec((tm, tf), lambda i, f: (i, f)),
                   pl.BlockSpec((tm, D), lambda i, f: (i, 0)),
                   pl.BlockSpec((tm, D), lambda i, f: (i, 0)),
                   pl.BlockSpec((1, D), lambda i, f: (0, 0))],
        out_shape=[_sds((T, S * n), BF16), _sds((T, D), F32), _sds((T, D), BF16), _sds((1, D), F32)],
        scratch_shapes=[pltpu.VMEM((tm, D), BF16), pltpu.VMEM((tm, D), F32)],
        compiler_params=_params("arbitrary", "arbitrary"),
    )(dy, h, g, up, w_up, w_down)


ATT_BLOCK = 128


def _tri(strict_lower):
    B = ATT_BLOCK
    r = lax.broadcasted_iota(jnp.int32, (2 * B, B), 0)
    r = jnp.where(r >= B, r - B, r)
    c = lax.broadcasted_iota(jnp.int32, (2 * B, B), 1)
    m = (r > c) if strict_lower else (r < c)
    return jnp.where(m, 1.0, 0.0).astype(BF16)


def _split_dot(x, tri):
    hi = x.astype(BF16)
    lo = (x - hi.astype(F32)).astype(BF16)
    return jnp.dot(jnp.concatenate([hi, lo], axis=1), tri, preferred_element_type=F32)


def _causal_mask():
    B = ATT_BLOCK
    r = lax.broadcasted_iota(jnp.int32, (B, B), 0)
    c = lax.broadcasted_iota(jnp.int32, (B, B), 1)
    return c < r


def _attn_scores(q, ks, scale, carry, tri, masked):
    z = lax.dot_general(q, ks, (((1,), (1,)), ((), ())), preferred_element_type=F32) * scale
    sp, logsig = _softplus_parts(z)
    lk = -sp
    if masked:
        causal = _causal_mask()
        lk = jnp.where(causal, lk, 0.0)
    after = _split_dot(lk, tri)
    w = jnp.exp(logsig + after + carry)
    if masked:
        w = jnp.where(causal, w, 0.0)
    carry = carry + (after[:, 0:1] + lk[:, 0:1])
    return z, w, carry


def _attn_fwd(qkv, *, name):
    T = qkv.shape[0]
    D = qkv.shape[1] // 3
    B = ATT_BLOCK
    npairs = D // LANE
    nq = T // B
    scale = HEAD_DIM ** -0.5

    def body(q_ref, k_ref, v_ref, o_ref):
        i = pl.program_id(1)
        tri = _tri(True)
        for hh in range(2):
            cols = slice(hh * HEAD_DIM, (hh + 1) * HEAD_DIM)
            q = q_ref[:, cols]

            def block(j, carry, acc, masked):
                rows = pl.ds(pl.multiple_of(j * B, B), B)
                _, w, carry = _attn_scores(q, k_ref[rows, cols], scale, carry, tri, masked)
                acc = acc + jnp.dot(w.astype(BF16), v_ref[rows, cols], preferred_element_type=F32)
                return carry, acc

            carry, acc = block(i, jnp.zeros((B, 1), F32), jnp.zeros((B, HEAD_DIM), F32), True)
            carry, acc = lax.fori_loop(
                0, i, lambda jj, ca: block(i - 1 - jj, ca[0], ca[1], False), (carry, acc))
            o_ref[:, cols] = acc.astype(BF16)

    return _pcall(
        body, name=name, grid=(npairs, nq),
        in_specs=[pl.BlockSpec((B, LANE), lambda p, i: (i, p)),
                  pl.BlockSpec((T, LANE), lambda p, i: (0, npairs + p)),
                  pl.BlockSpec((T, LANE), lambda p, i: (0, 2 * npairs + p))],
        out_specs=pl.BlockSpec((B, LANE), lambda p, i: (i, p)),
        out_shape=_sds((T, D), BF16),
        compiler_params=_params("parallel", "arbitrary"),
    )(qkv, qkv, qkv)


def _attn_bwd(qkv, do, *, name):
    T = qkv.shape[0]
    D = qkv.shape[1] // 3
    B = ATT_BLOCK
    npairs = D // LANE
    nq = T // B
    scale = HEAD_DIM ** -0.5

    def body(q_ref, k_ref, v_ref, do_ref, dq_ref, dk_ref, dv_ref, z_s, w_s, dk_acc, dv_acc):
        i = pl.program_id(1)

        @pl.when(i == 0)
        def _():
            dk_acc[...] = jnp.zeros_like(dk_acc)
            dv_acc[...] = jnp.zeros_like(dv_acc)

        tri_after = _tri(True)
        tri_before = _tri(False)
        for hh in range(2):
            cols = slice(hh * HEAD_DIM, (hh + 1) * HEAD_DIM)
            q = q_ref[:, cols]
            dout = do_ref[:, cols]

            def score(j, carry, masked):
                rows = pl.ds(pl.multiple_of(j * B, B), B)
                z, w, carry = _attn_scores(q, k_ref[rows, cols], scale, carry, tri_after, masked)
                z_s[j] = z
                w_s[j] = w
                return carry

            carry = score(i, jnp.zeros((B, 1), F32), True)
            lax.fori_loop(0, i, lambda jj, c: score(i - 1 - jj, c, False), carry)

            def grad(j, gsum, dq, masked):
                rows = pl.ds(pl.multiple_of(j * B, B), B)
                ks = k_ref[rows, cols]
                z = z_s[j]
                w = w_s[j]
                dw = lax.dot_general(dout, v_ref[rows, cols], (((1,), (1,)), ((), ())),
                                     preferred_element_type=F32)
                g = dw * w
                before = _split_dot(g, tri_before)
                sig = _sigmoid(z)
                dz = (g * (1.0 - sig) - sig * (before + gsum)) * scale
                if masked:
                    dz = jnp.where(_causal_mask(), dz, 0.0)
                gsum = gsum + (before[:, B - 1:B] + g[:, B - 1:B])
                dzb = dz.astype(BF16)
                dq = dq + jnp.dot(dzb, ks, preferred_element_type=F32)
                dk_acc[rows, cols] += lax.dot_general(dzb, q, (((0,), (0,)), ((), ())),
                                                      preferred_element_type=F32)
                dv_acc[rows, cols] += lax.dot_general(w.astype(BF16), dout, (((0,), (0,)), ((), ())),
                                                      preferred_element_type=F32)
                return gsum, dq

            gsum, dq = lax.fori_loop(
                0, i, lambda j, c: grad(j, c[0], c[1], False),
                (jnp.zeros((B, 1), F32), jnp.zeros((B, HEAD_DIM), F32)))
            _, dq = grad(i, gsum, dq, True)
            dq_ref[:, cols] = dq.astype(BF16)

        @pl.when(i == nq - 1)
        def _():
            dk_ref[...] = dk_acc[...].astype(BF16)
            dv_ref[...] = dv_acc[...].astype(BF16)

    return _pcall(
        body, name=name, grid=(npairs, nq),
        in_specs=[pl.BlockSpec((B, LANE), lambda p, i: (i, p)),
                  pl.BlockSpec((T, LANE), lambda p, i: (0, npairs + p)),
                  pl.BlockSpec((T, LANE), lambda p, i: (0, 2 * npairs + p)),
                  pl.BlockSpec((B, LANE), lambda p, i: (i, p))],
        out_specs=[pl.BlockSpec((B, LANE), lambda p, i: (i, p)),
                   pl.BlockSpec((T, LANE), lambda p, i: (0, p)),
                   pl.BlockSpec((T, LANE), lambda p, i: (0, p))],
        out_shape=[_sds((T, D), BF16)] * 3,
        scratch_shapes=[pltpu.VMEM((nq, B, B), F32), pltpu.VMEM((nq, B, B), F32),
                        pltpu.VMEM((T, LANE), F32), pltpu.VMEM((T, LANE), F32)],
        compiler_params=_params("arbitrary", "arbitrary"),
    )(qkv, qkv, qkv, do)


HALO = SUBLANE


def _lru_gates(u, w_rg, b_rg, w_ig, b_ig, lam):
    nb = w_rg.shape[0]
    pre_r, pre_i = [], []
    for n in range(nb):
        ub = u[:, n * LANE:(n + 1) * LANE].astype(BF16)
        pre_r.append(jnp.dot(ub, w_rg[n], preferred_element_type=F32))
        pre_i.append(jnp.dot(ub, w_ig[n], preferred_element_type=F32))
    r = _sigmoid(jnp.concatenate(pre_r, axis=1) + b_rg)
    i = _sigmoid(jnp.concatenate(pre_i, axis=1) + b_ig)
    c = -LRU_C * _softplus_parts(-lam)[0]
    log_a = c * r
    a = jnp.exp(log_a)
    x2 = 2.0 * log_a
    em1 = jnp.where(jnp.abs(x2) < 1e-2, x2 * (1.0 + x2 * (0.5 + x2 * (1.0 / 6.0))), jnp.exp(x2) - 1.0)
    mult = jnp.sqrt(-em1)
    return r, i, a, mult, c


def _conv_rows(buf_ref, tt, conv_w, conv_b):
    u = conv_b
    for j in range(4):
        u = u + buf_ref[pl.ds(HALO - 3 + j, tt), :] * conv_w[j:j + 1, :]
    return u


def _fill_with_halo(buf_ref, prev_ref, cur_ref, first):
    tt = cur_ref.shape[0]
    buf_ref[pl.ds(0, HALO), :] = jnp.where(first, 0.0, prev_ref[...])
    buf_ref[pl.ds(HALO, tt), :] = cur_ref[...]


def _lru_time_tile(T):
    return _tile(T, 256, SUBLANE)


def _lru_pre(gr, conv_w, conv_b, w_rg, b_rg, w_ig, b_ig, lam, *, name):
    T = gr.shape[0]
    D = gr.shape[1] // 2
    tt = _lru_time_tile(T)
    hb = tt // HALO

    def body(x_ref, xp_ref, cw_ref, cb_ref, wr_ref, br_ref, wi_ref, bi_ref, lam_ref, a_ref, b_ref, buf):
        _fill_with_halo(buf, xp_ref, x_ref, pl.program_id(0) == 0)
        u = _conv_rows(buf, tt, cw_ref[...], cb_ref[...])
        _, i, a, mult, _ = _lru_gates(u, wr_ref, br_ref[...], wi_ref, bi_ref[...], lam_ref[...])
        a_ref[...] = a
        b_ref[...] = mult * (i * u)

    vec = pl.BlockSpec((1, D), lambda t: (0, 0))
    mat = pl.BlockSpec(w_rg.shape, lambda t: (0, 0, 0))
    return _pcall(
        body, name=name, grid=(T // tt,),
        in_specs=[pl.BlockSpec((tt, D), lambda t: (t, 1)),
                  pl.BlockSpec((HALO, D), lambda t: (jnp.maximum(t * hb - 1, 0), 1)),
                  pl.BlockSpec((4, D), lambda t: (0, 0)), vec, mat, vec, mat, vec, vec],
        out_specs=[pl.BlockSpec((tt, D), lambda t: (t, 0))] * 2,
        out_shape=[_sds((T, D), F32)] * 2,
        scratch_shapes=[pltpu.VMEM((tt + HALO, D), F32)],
        compiler_params=_params("parallel"),
    )(gr, gr, conv_w, conv_b, w_rg, b_rg, w_ig, b_ig, lam)


def _lru_scan(a, b, *, reverse, name):
    T, D = a.shape
    nb = D // LANE
    ts = _tile(T, 1056, SUBLANE)
    nt = T // ts
    a3 = a.reshape(T, nb, LANE)
    b3 = b.reshape(T, nb, LANE)

    def body(a_ref, b_ref, o_ref, carry):
        @pl.when(pl.program_id(0) == 0)
        def _():
            carry[...] = jnp.zeros_like(carry)

        if reverse:
            def step(k, c):
                t = ts - 1 - k
                l = b_ref[t] + c
                o_ref[t] = l
                return a_ref[t] * l
        else:
            def step(k, h):
                h = a_ref[k] * h + b_ref[k]
                o_ref[k] = h
                return h

        carry[...] = lax.fori_loop(0, ts, step, carry[...], unroll=8)

    if reverse:
        spec = pl.BlockSpec((ts, nb, LANE), lambda t: (nt - 1 - t, 0, 0))
    else:
        spec = pl.BlockSpec((ts, nb, LANE), lambda t: (t, 0, 0))
    out = _pcall(
        body, name=name, grid=(nt,),
        in_specs=[spec, spec], out_specs=spec,
        out_shape=_sds((T, nb, LANE), F32),
        scratch_shapes=[pltpu.VMEM((nb, LANE), F32)],
        compiler_params=_params("arbitrary"),
    )(a3, b3)
    return out.reshape(T, D)


def _lru_out(gr, hs, *, name):
    T, D = hs.shape
    tt = _tile(T, 1056, 16)

    def body(g_ref, h_ref, y_ref):
        y_ref[...] = (h_ref[...] * _gelu_parts(g_ref[...])[0]).astype(BF16)

    return _pcall(
        body, name=name, grid=(T // tt,),
        in_specs=[pl.BlockSpec((tt, D), lambda t: (t, 0)), pl.BlockSpec((tt, D), lambda t: (t, 0))],
        out_specs=pl.BlockSpec((tt, D), lambda t: (t, 0)),
        out_shape=_sds((T, D), BF16),
        compiler_params=_params("parallel"),
    )(gr, hs)


def _lru_out_bwd(gr, hs, dy, *, name):
    T, D = hs.shape
    tt = _tile(T, 1056, 16)

    def body(g_ref, h_ref, dy_ref, dg_ref, dh_ref):
        gelu, dgelu = _gelu_parts(g_ref[...])
        dy = dy_ref[...]
        dg_ref[...] = (dy * h_ref[...] * dgelu).astype(BF16)
        dh_ref[...] = dy * gelu

    spec = pl.BlockSpec((tt, D), lambda t: (t, 0))
    return _pcall(
        body, name=name, grid=(T // tt,),
        in_specs=[spec, spec, spec], out_specs=[spec, spec],
        out_shape=[_sds((T, D), BF16), _sds((T, D), F32)],
        compiler_params=_params("parallel"),
    )(gr, hs, dy)


def _lru_gate_bwd(gr, hs, lmb, conv_w, conv_b, w_rg, b_rg, w_ig, b_ig, lam, *, name):
    T, D = hs.shape
    nb = w_rg.shape[0]
    tt = _lru_time_tile(T)
    hb = tt // HALO
    nt = T // tt

    def body(x_ref, xp_ref, h_ref, hp_ref, l_ref, cw_ref, cb_ref, wr_ref, br_ref, wi_ref, bi_ref, lam_ref,
             du_ref, dwr_ref, dbr_ref, dwi_ref, dbi_ref, dlam_ref, xbuf, hbuf):
        t = pl.program_id(0)
        first = t == 0
        _fill_with_halo(xbuf, xp_ref, x_ref, first)
        _fill_with_halo(hbuf, hp_ref, h_ref, first)
        u = _conv_rows(xbuf, tt, cw_ref[...], cb_ref[...])
        lam_v = lam_ref[...]
        r, i, a, mult, c = _lru_gates(u, wr_ref, br_ref[...], wi_ref, bi_ref[...], lam_v)
        l = l_ref[...]
        h_prev = hbuf[pl.ds(HALO - 1, tt), :]
        dlog_a = l * h_prev * a - l * (i * u) * (a * a) / mult
        d_iu = l * mult
        dpre_r = (dlog_a * c) * (r * (1.0 - r))
        dpre_i = (d_iu * u) * (i * (1.0 - i))
        dpr_b = dpre_r.astype(BF16)
        dpi_b = dpre_i.astype(BF16)
        du_parts, dwr, dwi = [], [], []
        for n in range(nb):
            cs = slice(n * LANE, (n + 1) * LANE)
            ub = u[:, cs].astype(BF16)
            du_parts.append(
                lax.dot_general(dpr_b[:, cs], wr_ref[n], (((1,), (1,)), ((), ())), preferred_element_type=F32)
                + lax.dot_general(dpi_b[:, cs], wi_ref[n], (((1,), (1,)), ((), ())), preferred_element_type=F32))
            dwr.append(lax.dot_general(ub, dpr_b[:, cs], (((0,), (0,)), ((), ())), preferred_element_type=F32))
            dwi.append(lax.dot_general(ub, dpi_b[:, cs], (((0,), (0,)), ((), ())), preferred_element_type=F32))
        du_ref[...] = d_iu * i + jnp.concatenate(du_parts, axis=1)
        dbr = jnp.sum(dpre_r, axis=0, keepdims=True)
        dbi = jnp.sum(dpre_i, axis=0, keepdims=True)
        dc = jnp.sum(dlog_a * r, axis=0, keepdims=True)

        @pl.when(first)
        def _():
            for n in range(nb):
                dwr_ref[n] = dwr[n]
                dwi_ref[n] = dwi[n]
            dbr_ref[...] = dbr
            dbi_ref[...] = dbi
            dlam_ref[...] = dc

        @pl.when(t > 0)
        def _():
            for n in range(nb):
                dwr_ref[n] += dwr[n]
                dwi_ref[n] += dwi[n]
            dbr_ref[...] += dbr
            dbi_ref[...] += dbi
            dlam_ref[...] += dc

        @pl.when(t == nt - 1)
        def _():
            dlam_ref[...] = dlam_ref[...] * (LRU_C * _sigmoid(-lam_v))

    vec = pl.BlockSpec((1, D), lambda t: (0, 0))
    mat = pl.BlockSpec(w_rg.shape, lambda t: (0, 0, 0))
    blk = pl.BlockSpec((tt, D), lambda t: (t, 0))
    prev = pl.BlockSpec((HALO, D), lambda t: (jnp.maximum(t * hb - 1, 0), 0))
    return _pcall(
        body, name=name, grid=(nt,),
        in_specs=[pl.BlockSpec((tt, D), lambda t: (t, 1)),
                  pl.BlockSpec((HALO, D), lambda t: (jnp.maximum(t * hb - 1, 0), 1)),
                  blk, prev, blk,
                  pl.BlockSpec((4, D), lambda t: (0, 0)), vec, mat, vec, mat, vec, vec],
        out_specs=[blk, mat, vec, mat, vec, vec],
        out_shape=[_sds((T, D), F32), _sds(w_rg.shape, F32), _sds((1, D), F32),
                   _sds(w_rg.shape, F32), _sds((1, D), F32), _sds((1, D), F32)],
        scratch_shapes=[pltpu.VMEM((tt + HALO, D), F32), pltpu.VMEM((tt + HALO, D), F32)],
        compiler_params=_params("arbitrary"),
    )(gr, gr, hs, hs, lmb, conv_w, conv_b, w_rg, b_rg, w_ig, b_ig, lam)


def _lru_conv_bwd(gr, du, conv_w, *, name):
    T, D = du.shape
    tt = _lru_time_tile(T)
    hb = tt // HALO
    nt = T // tt

    def body(x_ref, xp_ref, du_ref, dun_ref, cw_ref, dx_ref, dcw_ref, dcb_ref, xbuf, dbuf):
        t = pl.program_id(0)
        _fill_with_halo(xbuf, xp_ref, x_ref, t == 0)
        du = du_ref[...]
        dbuf[pl.ds(0, tt), :] = du
        dbuf[pl.ds(tt, HALO), :] = jnp.where(t == nt - 1, 0.0, dun_ref[...])
        cw = cw_ref[...]
        dx = jnp.zeros((tt, D), F32)
        dcw = []
        for j in range(4):
            dx = dx + dbuf[pl.ds(3 - j, tt), :] * cw[j:j + 1, :]
            dcw.append(jnp.sum(du * xbuf[pl.ds(HALO - 3 + j, tt), :], axis=0, keepdims=True))
        dx_ref[...] = dx.astype(BF16)
        dcw = jnp.concatenate(dcw, axis=0)
        dcb = jnp.sum(du, axis=0, keepdims=True)

        @pl.when(t == 0)
        def _():
            dcw_ref[...] = dcw
            dcb_ref[...] = dcb

        @pl.when(t > 0)
        def _():
            dcw_ref[...] += dcw
            dcb_ref[...] += dcb

    blk = pl.BlockSpec((tt, D), lambda t: (t, 0))
    return _pcall(
        body, name=name, grid=(nt,),
        in_specs=[pl.BlockSpec((tt, D), lambda t: (t, 1)),
                  pl.BlockSpec((HALO, D), lambda t: (jnp.maximum(t * hb - 1, 0), 1)),
                  blk,
                  pl.BlockSpec((HALO, D), lambda t: (jnp.minimum((t + 1) * hb, T // HALO - 1), 0)),
                  pl.BlockSpec((4, D), lambda t: (0, 0))],
        out_specs=[blk, pl.BlockSpec((4, D), lambda t: (0, 0)), pl.BlockSpec((1, D), lambda t: (0, 0))],
        out_shape=[_sds((T, D), BF16), _sds((4, D), F32), _sds((1, D), F32)],
        scratch_shapes=[pltpu.VMEM((tt + HALO, D), F32), pltpu.VMEM((tt + HALO, D), F32)],
        compiler_params=_params("arbitrary"),
    )(gr, gr, du, du, conv_w)


def _loss_head(h, g, target, *, row_lo, row_hi, name):
    T, D = h.shape
    tm = _tile(T, 1056, 16)

    def body(h_ref, g_ref, t_ref, loss_ref, dh_ref, dhb_ref, dg_ref):
        i = pl.program_id(0)
        x = h_ref[...]
        g = g_ref[...]
        row = i * tm + lax.broadcasted_iota(jnp.int32, (tm, 1), 0)
        valid = jnp.logical_and(row >= row_lo, row < row_hi)
        rstd = _rstd(x)
        n = x * rstd
        err = jnp.where(valid, n * g - t_ref[...], 0.0)
        part = (0.5 / D) * jnp.sum(jnp.sum(err * err, axis=1, keepdims=True), axis=0, keepdims=True)
        dy = err * (1.0 / D)
        dn = dy * g
        dh = rstd * (dn - n * jnp.mean(dn * n, axis=-1, keepdims=True))
        dh_ref[...] = dh
        dhb_ref[...] = dh.astype(BF16)
        dg = jnp.sum(dy * n, axis=0, keepdims=True)

        @pl.when(i == 0)
        def _():
            loss_ref[...] = part
            dg_ref[...] = dg

        @pl.when(i > 0)
        def _():
            loss_ref[...] += part
            dg_ref[...] += dg

    blk = pl.BlockSpec((tm, D), lambda i: (i, 0))
    vec = pl.BlockSpec((1, D), lambda i: (0, 0))
    return _pcall(
        body, name=name, grid=(T // tm,),
        in_specs=[blk, vec, blk],
        out_specs=[pl.BlockSpec((1, 1), lambda i: (0, 0)), blk, blk, vec],
        out_shape=[_sds((1, 1), F32), _sds((T, D), F32), _sds((T, D), BF16), _sds((1, D), F32)],
        compiler_params=_params("arbitrary"),
    )(h, g, target)


def _adamw(w, g, m, v, *, name):
    R, C = w.shape
    tr = _tile(R, 512, SUBLANE)
    c1 = 1.0 / (1.0 - ADAM_B1 ** ADAM_STEP)
    c2 = 1.0 / (1.0 - ADAM_B2 ** ADAM_STEP)

    def body(w_ref, g_ref, m_ref, v_ref, d_ref, nm_ref, nv_ref):
        g = g_ref[...]
        m = ADAM_B1 * m_ref[...] + (1.0 - ADAM_B1) * g
        v = ADAM_B2 * v_ref[...] + (1.0 - ADAM_B2) * (g * g)
        nm_ref[...] = m
        nv_ref[...] = v
        d_ref[...] = -ADAM_LR * ((m * c1) / (jnp.sqrt(v * c2) + ADAM_EPS) + ADAM_WD * w_ref[...])

    blk = pl.BlockSpec((tr, C), lambda i: (i, 0))
    return _pcall(
        body, name=name, grid=(R // tr,),
        in_specs=[blk] * 4, out_specs=[blk] * 3,
        out_shape=[_sds((R, C), F32)] * 3,
        compiler_params=_params("parallel"),
    )(w, g, m, v)


ANY = pl.BlockSpec(memory_space=pl.ANY)


def _place():
    x, y, c = lax.axis_index("x"), lax.axis_index("y"), lax.axis_index("c")
    chips = [(1 - x, y), (x, 1 - y), (1 - x, 1 - y)]
    return x, y, c, chips


def _all_gather_chips(v, *, name):
    def body(v_ref, o_ref, send_sems, recv_sems, local_sem):
        x, y, c, chips = _place()
        me = 2 * x + y

        def copy(k, block, half, to, src=None):
            dst = o_ref.at[block, half]
            return pltpu.make_async_remote_copy(
                src_ref=dst if src is None else src, dst_ref=dst,
                send_sem=send_sems.at[k], recv_sem=recv_sems.at[k],
                device_id=to, device_id_type=MESH)

        mine = pltpu.make_async_copy(v_ref, o_ref.at[me], local_sem)
        mine.start()
        first = [copy(k, me, c, (cx, cy, c), src=v_ref.at[c]) for k, (cx, cy) in enumerate(chips)]
        for cp in first:
            cp.start()
        passed = [copy(3 + k, 2 * cx + cy, c, (x, y, 1 - c)) for k, (cx, cy) in enumerate(chips)]
        for k, (cx, cy) in enumerate(chips):
            copy(k, 2 * cx + cy, c, (x, y, c)).wait_recv()
            passed[k].start()
        for k, (cx, cy) in enumerate(chips):
            copy(3 + k, 2 * cx + cy, 1 - c, (x, y, c)).wait_recv()
        for cp in first + passed:
            cp.wait_send()
        mine.wait()

    return _pcall(
        body, name=name,
        in_specs=[ANY], out_specs=ANY,
        out_shape=_sds((4,) + v.shape, v.dtype),
        scratch_shapes=[pltpu.SemaphoreType.DMA((6,)), pltpu.SemaphoreType.DMA((6,)),
                        pltpu.SemaphoreType.DMA],
    )(v)


def _swap_halves(g, *, name):
    def body(g_ref, o_ref, send_sem, recv_sem):
        x, y, c, _ = _place()
        cp = pltpu.make_async_remote_copy(
            src_ref=g_ref.at[:, 1 - c], dst_ref=o_ref, send_sem=send_sem, recv_sem=recv_sem,
            device_id=(x, y, 1 - c), device_id_type=MESH)
        cp.start()
        cp.wait()

    return _pcall(
        body, name=name, in_specs=[ANY], out_specs=ANY,
        out_shape=_sds((4,) + g.shape[2:], g.dtype),
        scratch_shapes=[pltpu.SemaphoreType.DMA, pltpu.SemaphoreType.DMA],
    )(g)


def _scatter_chips(a, *, name):
    def body(a_ref, o_ref, send_sems, recv_sems):
        x, y, c, chips = _place()
        me = 2 * x + y
        sends = [pltpu.make_async_remote_copy(
            src_ref=a_ref.at[2 * cx + cy], dst_ref=o_ref.at[me],
            send_sem=send_sems.at[k], recv_sem=recv_sems.at[k],
            device_id=(cx, cy, c), device_id_type=MESH) for k, (cx, cy) in enumerate(chips)]
        for cp in sends:
            cp.start()
        for k, (cx, cy) in enumerate(chips):
            pltpu.make_async_remote_copy(
                src_ref=a_ref.at[me], dst_ref=o_ref.at[2 * cx + cy],
                send_sem=send_sems.at[k], recv_sem=recv_sems.at[k],
                device_id=(x, y, c), device_id_type=MESH).wait_recv()
        for cp in sends:
            cp.wait_send()

    return _pcall(
        body, name=name, in_specs=[ANY], out_specs=ANY,
        out_shape=_sds(a.shape, a.dtype),
        scratch_shapes=[pltpu.SemaphoreType.DMA((3,)), pltpu.SemaphoreType.DMA((3,))],
    )(a)


def _join_halves(r, *, name):
    def body(r_ref, o_ref, send_sem, recv_sem, local_sem):
        x, y, c, _ = _place()
        mine = pltpu.make_async_copy(r_ref, o_ref.at[c], local_sem)
        mine.start()
        cp = pltpu.make_async_remote_copy(
            src_ref=r_ref, dst_ref=o_ref.at[c], send_sem=send_sem, recv_sem=recv_sem,
            device_id=(x, y, 1 - c), device_id_type=MESH)
        cp.start()
        pltpu.make_async_remote_copy(
            src_ref=r_ref, dst_ref=o_ref.at[1 - c], send_sem=send_sem, recv_sem=recv_sem,
            device_id=(x, y, c), device_id_type=MESH).wait_recv()
        cp.wait_send()
        mine.wait()

    return _pcall(
        body, name=name, in_specs=[ANY], out_specs=ANY,
        out_shape=_sds((2,) + r.shape, r.dtype),
        scratch_shapes=[pltpu.SemaphoreType.DMA, pltpu.SemaphoreType.DMA, pltpu.SemaphoreType.DMA],
    )(r)


RED_COLS = 1024


def _add_halves(g, got, c, *, name):
    _, _, R, C = g.shape
    tr = _tile(R, 512, SUBLANE)

    def body(c_ref, g_ref, o_ref, out_ref):
        out_ref[...] = g_ref[...] + o_ref[...]

    return _pcall(
        body, name=name,
        grid_spec=pltpu.PrefetchScalarGridSpec(
            num_scalar_prefetch=1, grid=(4, R // tr),
            in_specs=[pl.BlockSpec((None, None, tr, C), lambda s, i, c_ref: (s, c_ref[0], i, 0)),
                      pl.BlockSpec((None, tr, C), lambda s, i, c_ref: (s, i, 0))],
            out_specs=pl.BlockSpec((None, tr, C), lambda s, i, c_ref: (s, i, 0))),
        out_shape=_sds((4, R, C), g.dtype),
        compiler_params=_params("parallel", "parallel"),
    )(c, g, got)


def _add_chips(a, got, me, *, name):
    _, R, C = a.shape
    tr = _tile(R, 256, SUBLANE)

    def body(me_ref, a_ref, o_ref, out_ref):
        me_v = me_ref[0]
        own = a_ref[...]
        acc = jnp.where(me_v == 0, own, o_ref[0])
        for s in range(1, 4):
            acc = acc + jnp.where(me_v == s, own, o_ref[s])
        out_ref[...] = acc

    return _pcall(
        body, name=name,
        grid_spec=pltpu.PrefetchScalarGridSpec(
            num_scalar_prefetch=1, grid=(R // tr,),
            in_specs=[pl.BlockSpec((None, tr, C), lambda i, me_ref: (me_ref[0], i, 0)),
                      pl.BlockSpec((4, tr, C), lambda i, me_ref: (0, i, 0))],
            out_specs=pl.BlockSpec((tr, C), lambda i, me_ref: (i, 0))),
        out_shape=_sds((R, C), a.dtype),
        compiler_params=_params("parallel"),
    )(me, a, got)


def _round_up(n, m):
    return (n + m - 1) // m * m


def _f32_as_bf16(a):
    return lax.bitcast_convert_type(a.astype(F32), BF16).reshape(-1)


def _bf16_as_f32(a):
    return lax.bitcast_convert_type(a.reshape(-1, 2), F32)


def _by_chip_cols(a, cols):
    lead = a.shape[:-1]
    a = a.reshape(lead + (4, cols))
    return jnp.moveaxis(a, -2, 0).reshape(4, -1)


def kernel(x, meta_tokens, norm_mix, norm_mlp, sb_w_qkv, sb_w_o, lru_w_in, lru_conv_w, lru_conv_b, lru_w_rg, lru_b_rg, lru_w_ig, lru_b_ig, lru_lambda, lru_w_out, mlp_w_up, mlp_w_down, norm_final, loss_target, m_meta_tokens, m_norm_mix, m_norm_mlp, m_sb_w_qkv, m_sb_w_o, m_lru_w_in, m_lru_conv_w, m_lru_conv_b, m_lru_w_rg, m_lru_b_rg, m_lru_w_ig, m_lru_b_ig, m_lru_lambda, m_lru_w_out, m_mlp_w_up, m_mlp_w_down, m_norm_final, v_meta_tokens, v_norm_mix, v_norm_mlp, v_sb_w_qkv, v_sb_w_o, v_lru_w_in, v_lru_conv_w, v_lru_conv_b, v_lru_w_rg, v_lru_b_rg, v_lru_w_ig, v_lru_b_ig, v_lru_lambda, v_lru_w_out, v_mlp_w_up, v_mlp_w_down, v_norm_final):
    weights = dict(meta_tokens=meta_tokens, norm_mix=norm_mix, norm_mlp=norm_mlp, sb_w_qkv=sb_w_qkv,
                   sb_w_o=sb_w_o, lru_w_in=lru_w_in, lru_conv_w=lru_conv_w, lru_conv_b=lru_conv_b,
                   lru_w_rg=lru_w_rg, lru_b_rg=lru_b_rg, lru_w_ig=lru_w_ig, lru_b_ig=lru_b_ig,
                   lru_lambda=lru_lambda, lru_w_out=lru_w_out, mlp_w_up=mlp_w_up, mlp_w_down=mlp_w_down,
                   norm_final=norm_final)
    m_in = dict(meta_tokens=m_meta_tokens, norm_mix=m_norm_mix, norm_mlp=m_norm_mlp, sb_w_qkv=m_sb_w_qkv,
                sb_w_o=m_sb_w_o, lru_w_in=m_lru_w_in, lru_conv_w=m_lru_conv_w, lru_conv_b=m_lru_conv_b,
                lru_w_rg=m_lru_w_rg, lru_b_rg=m_lru_b_rg, lru_w_ig=m_lru_w_ig, lru_b_ig=m_lru_b_ig,
                lru_lambda=m_lru_lambda, lru_w_out=m_lru_w_out, mlp_w_up=m_mlp_w_up,
                mlp_w_down=m_mlp_w_down, norm_final=m_norm_final)
    v_in = dict(meta_tokens=v_meta_tokens, norm_mix=v_norm_mix, norm_mlp=v_norm_mlp, sb_w_qkv=v_sb_w_qkv,
                sb_w_o=v_sb_w_o, lru_w_in=v_lru_w_in, lru_conv_w=v_lru_conv_w, lru_conv_b=v_lru_conv_b,
                lru_w_rg=v_lru_w_rg, lru_b_rg=v_lru_b_rg, lru_w_ig=v_lru_w_ig, lru_b_ig=v_lru_b_ig,
                lru_lambda=v_lru_lambda, lru_w_out=v_lru_w_out, mlp_w_up=v_mlp_w_up,
                mlp_w_down=v_mlp_w_down, norm_final=v_norm_final)
    names = list(weights)

    seq, D = x.shape[1], x.shape[2]
    n_meta = meta_tokens.shape[0]
    Dq = D // 4
    T = _round_up(n_meta + seq, ATT_BLOCK)
    nb = lru_w_rg.shape[1]
    F = mlp_w_up.shape[2]
    depth = mlp_w_up.shape[0]
    my_x, my_y, my_c = lax.axis_index("x"), lax.axis_index("y"), lax.axis_index("c")
    c_arr = jnp.reshape(my_c, (1,)).astype(jnp.int32)
    me_arr = jnp.reshape(2 * my_x + my_y, (1,)).astype(jnp.int32)

    big = [sb_w_qkv[0], sb_w_o[0], lru_w_in[0], lru_w_out[0]]
    big += [mlp_w_up[l] for l in range(depth)] + [mlp_w_down[l] for l in range(depth)]
    small = [meta_tokens, lru_conv_w[0], lru_conv_b, lru_b_rg, lru_b_ig, lru_lambda]
    parts = [b.astype(BF16).reshape(-1) for b in big] + [_f32_as_bf16(s) for s in small]
    sizes = [p.shape[0] for p in parts]
    total = _round_up(sum(sizes), 2 * 16 * RED_COLS)
    flat = jnp.concatenate(parts + [jnp.zeros((total - sum(sizes),), BF16)])
    gathered = _all_gather_chips(flat.reshape(2, -1, RED_COLS), name="gather_weights").reshape(4, total)
    offs = [sum(sizes[:k]) for k in range(len(sizes))]
    pieces = [gathered[:, o:o + s] for o, s in zip(offs, sizes)]
    w_qkv = pieces[0].reshape(4, D, 3 * Dq)
    w_o = pieces[1].reshape(D, D)
    w_in = pieces[2].reshape(4, D, 2 * Dq)
    w_out = pieces[3].reshape(D, D)
    w_up = [pieces[4 + l].reshape(4, D, F) for l in range(depth)]
    w_down = [pieces[4 + depth + l].reshape(4 * F, D) for l in range(depth)]
    sm = [_bf16_as_f32(p) for p in pieces[4 + 2 * depth:]]
    meta_full = jnp.moveaxis(sm[0].reshape(4, n_meta, Dq), 0, 1).reshape(n_meta, D)
    conv_w = jnp.moveaxis(sm[1].reshape(4, 4, Dq), 0, 1).reshape(4, D)
    conv_b, b_rg, b_ig, lam = [s.reshape(1, D) for s in sm[2:6]]
    w_rg = lru_w_rg[0].astype(BF16)
    w_ig = lru_w_ig[0].astype(BF16)
    g_mix = [norm_mix[l].reshape(1, D) for l in range(depth)]
    g_mlp = [norm_mlp[l].reshape(1, D) for l in range(depth)]
    g_fin = norm_final.reshape(1, D)

    pad_rows = T - n_meta - seq
    h0 = jnp.concatenate([meta_full, x[0], jnp.zeros((pad_rows, D), F32)], axis=0)
    target = jnp.concatenate([jnp.zeros((n_meta, D), F32), loss_target[0], jnp.zeros((pad_rows, D), F32)], axis=0)

    hn0, qkv = _norm_mm(h0, g_mix[0], w_qkv, out_dtype=BF16, name="qkv_proj")
    att = _attn_fwd(qkv, name="attn_fwd")
    h1 = _mm_res(att, w_o, h0, name="attn_out")
    h2, hnm0, up0 = _mlp_fwd(h1, g_mlp[0], w_up[0], w_down[0], name="mlp0_fwd")
    hn1, gr = _norm_mm(h2, g_mix[1], w_in, out_dtype=F32, name="lru_in")
    a_t, b_t = _lru_pre(gr, conv_w, conv_b, w_rg, b_rg, w_ig, b_ig, lam, name="lru_pre")
    hs = _lru_scan(a_t, b_t, reverse=False, name="lru_scan")
    y = _lru_out(gr, hs, name="lru_gate_out")
    h3 = _mm_res(y, w_out, h2, name="lru_out")
    h4, hnm1, up1 = _mlp_fwd(h3, g_mlp[1], w_up[1], w_down[1], name="mlp1_fwd")
    loss, dh4, dh4b, dg_fin = _loss_head(h4, g_fin, target, row_lo=n_meta, row_hi=n_meta + seq, name="loss_head")

    dup1, dh3, dh3b, dg_mlp1 = _mlp_bwd(dh4, h3, g_mlp[1], up1, w_up[1], w_down[1], name="mlp1_bwd")
    dw_up1 = _mm_tn(hnm1, dup1, shards=4, relu2=False, name="mlp1_dwup")
    dw_down1 = _mm_tn(up1, dh4b, shards=1, relu2=True, name="mlp1_dwdown")
    dy = _mm_nt(dh3b, w_out, out_dtype=F32, name="lru_out_bwd")
    dw_out = _mm_tn(y, dh3b, shards=1, relu2=False, name="lru_dwout")
    dgate, dhy = _lru_out_bwd(gr, hs, dy, name="lru_gate_out_bwd")
    lmb = _lru_scan(a_t, dhy, reverse=True, name="lru_scan_bwd")
    du, dw_rg, db_rg, dw_ig, db_ig, dlam = _lru_gate_bwd(
        gr, hs, lmb, conv_w, conv_b, w_rg, b_rg, w_ig, b_ig, lam, name="lru_gate_bwd")
    drec, dconv_w, dconv_b = _lru_conv_bwd(gr, du, conv_w, name="lru_conv_bwd")
    dgr = jnp.concatenate([dgate, drec], axis=1)
    dh2, dh2b, dg_mix1 = _mm_nt_normbwd(dgr, w_in, h2, g_mix[1], dh3, name="lru_in_bwd")
    dw_in = _mm_tn(hn1, dgr, shards=4, relu2=False, name="lru_dwin")
    dup0, dh1, dh1b, dg_mlp0 = _mlp_bwd(dh2, h1, g_mlp[0], up0, w_up[0], w_down[0], name="mlp0_bwd")
    dw_up0 = _mm_tn(hnm0, dup0, shards=4, relu2=False, name="mlp0_dwup")
    dw_down0 = _mm_tn(up0, dh2b, shards=1, relu2=True, name="mlp0_dwdown")
    datt = _mm_nt(dh1b, w_o, out_dtype=BF16, name="attn_out_bwd")
    dw_o = _mm_tn(att, dh1b, shards=1, relu2=False, name="attn_dwo")
    dq, dk, dv = _attn_bwd(qkv, datt, name="attn_bwd")
    dqkv = jnp.concatenate([dq, dk, dv], axis=1)
    dh0, _, dg_mix0 = _mm_nt_normbwd(dqkv, w_qkv, h0, g_mix[0], dh1, name="qkv_bwd")
    dw_qkv = _mm_tn(hn0, dqkv, shards=4, relu2=False, name="attn_dwqkv")
    grad_x = dh0[n_meta:n_meta + seq][None]
    dmeta = dh0[:n_meta]

    dw_up = [dw_up0, dw_up1]
    dw_down = [dw_down0, dw_down1]
    sharded = [dw_qkv.reshape(4, -1), dw_o.reshape(4, -1), dw_in.reshape(4, -1), dw_out.reshape(4, -1)]
    sharded += [d.reshape(4, -1) for d in dw_up] + [d.reshape(4, -1) for d in dw_down]
    sharded += [_by_chip_cols(dmeta, Dq), _by_chip_cols(dconv_w, Dq), dconv_b.reshape(4, Dq),
                db_rg.reshape(4, Dq), db_ig.reshape(4, Dq), dlam.reshape(4, Dq)]
    repl = [jnp.concatenate([dg_mix0, dg_mix1], axis=0).reshape(-1),
            jnp.concatenate([dg_mlp0, dg_mlp1], axis=0).reshape(-1),
            dg_fin.reshape(-1), dw_rg.reshape(-1), dw_ig.reshape(-1)]
    rsizes = [r.shape[0] for r in repl]
    rtotal = _round_up(sum(rsizes), 4 * 2 * SUBLANE * LANE)
    rflat = jnp.concatenate(repl + [jnp.zeros((rtotal - sum(rsizes),), F32)]).reshape(4, rtotal // 4)
    gsizes = [s.shape[1] for s in sharded] + [rtotal // 4]
    gtotal = _round_up(sum(gsizes), 2 * SUBLANE * RED_COLS)
    gflat = jnp.concatenate(sharded + [rflat, jnp.zeros((4, gtotal - sum(gsizes)), F32)], axis=1)
    half = gtotal // 2
    rows = half // RED_COLS
    gflat = gflat.reshape(4, 2, rows, RED_COLS)
    got = _swap_halves(gflat, name="reduce_cores")
    chip_sum = _add_halves(gflat, got, c_arr, name="reduce_cores_add")
    got = _scatter_chips(chip_sum, name="reduce_chips")
    mine = _add_chips(chip_sum, got, me_arr, name="reduce_chips_add")
    gshard = _join_halves(mine, name="reduce_join").reshape(gtotal)
    goffs = [sum(gsizes[:k]) for k in range(len(gsizes))]
    gp = [gshard[o:o + s] for o, s in zip(goffs, gsizes)]
    rchunk = gp[-1]
    rfull = _all_gather_chips(rchunk.reshape(2, -1, LANE), name="gather_replicated").reshape(rtotal)
    roffs = [sum(rsizes[:k]) for k in range(len(rsizes))]
    rp = [rfull[o:o + s] for o, s in zip(roffs, rsizes)]

    nbig = 4
    grads = dict(
        sb_w_qkv=gp[0], sb_w_o=gp[1], lru_w_in=gp[2], lru_w_out=gp[3],
        mlp_w_up=jnp.stack(gp[nbig:nbig + depth]), mlp_w_down=jnp.stack(gp[nbig + depth:nbig + 2 * depth]),
        meta_tokens=gp[nbig + 2 * depth], lru_conv_w=gp[nbig + 2 * depth + 1], lru_conv_b=gp[nbig + 2 * depth + 2],
        lru_b_rg=gp[nbig + 2 * depth + 3], lru_b_ig=gp[nbig + 2 * depth + 4], lru_lambda=gp[nbig + 2 * depth + 5],
        norm_mix=rp[0], norm_mlp=rp[1], norm_final=rp[2], lru_w_rg=rp[3], lru_w_ig=rp[4])
    grads = {n: grads[n].reshape(weights[n].shape) for n in names}

    large = ["sb_w_qkv", "sb_w_o", "lru_w_in", "lru_w_out", "mlp_w_up", "mlp_w_down"]
    delta, new_m, new_v = {}, {}, {}
    for n in large:
        shp = weights[n].shape
        view = (-1, shp[-1])
        d, nm, nv = _adamw(weights[n].reshape(view), grads[n].reshape(view), m_in[n].reshape(view),
                           v_in[n].reshape(view), name="adamw_" + n)
        delta[n], new_m[n], new_v[n] = d.reshape(shp), nm.reshape(shp), nv.reshape(shp)
    rest = [n for n in names if n not in large]
    ssz = [weights[n].size for n in rest]
    stotal = _round_up(sum(ssz), SUBLANE * RED_COLS)

    def pack(src):
        return jnp.concatenate([src[n].reshape(-1) for n in rest]
                               + [jnp.ones((stotal - sum(ssz),), F32)]).reshape(-1, RED_COLS)

    d, nm, nv = _adamw(pack(weights), pack(grads), pack(m_in), pack(v_in), name="adamw_small")
    soffs = [sum(ssz[:k]) for k in range(len(ssz))]
    for n, o, s in zip(rest, soffs, ssz):
        shp = weights[n].shape
        delta[n] = d.reshape(-1)[o:o + s].reshape(shp)
        new_m[n] = nm.reshape(-1)[o:o + s].reshape(shp)
        new_v[n] = nv.reshape(-1)[o:o + s].reshape(shp)

    loss = lax.psum(loss[0, 0], ("x", "y", "c"))
    return (loss, grad_x, *[grads[n] for n in names], *[delta[n] for n in names],
            *[new_m[n] for n in names], *[new_v[n] for n in names])
```

```python
import functools

import jax
import jax.numpy as jnp
from jax import lax
from jax.experimental import pallas as pl
from jax.experimental.pallas import tpu as pltpu

F32 = jnp.float32
BF16 = jnp.bfloat16
MESH = pl.DeviceIdType.MESH

EPS = 1e-6
HEAD_DIM = 64
LANE = 128
SUBLANE = 8
LRU_C = 8.0
VMEM_LIMIT = 56 * 1024 * 1024

ADAM_LR = 0.001
ADAM_B1 = 0.9
ADAM_B2 = 0.999
ADAM_EPS = 1e-08
ADAM_WD = 0.01
ADAM_STEP = 10


def _pcall(body, **kw):
    return pl.pallas_call(body, **kw)


def _params(*sem):
    return pltpu.CompilerParams(dimension_semantics=sem, vmem_limit_bytes=VMEM_LIMIT)


def _tile(n, pref, align):
    best = None
    for t in range(align, min(n, pref) + 1, align):
        if n % t == 0:
            best = t
    return n if best is None else best


def _sds(shape, dtype):
    return jax.ShapeDtypeStruct(shape, dtype)


def _rstd(x):
    return lax.rsqrt(jnp.mean(x * x, axis=-1, keepdims=True) + EPS)


def _norm_bwd(x, g, dy):
    rstd = _rstd(x)
    n = x * rstd
    dn = dy * g
    dx = rstd * (dn - n * jnp.mean(dn * n, axis=-1, keepdims=True))
    dg = jnp.sum(dy * n, axis=0, keepdims=True)
    return dx, dg


def _softplus_parts(z):
    l1p = jnp.log(1.0 + jnp.exp(-jnp.abs(z)))
    return jnp.maximum(z, 0.0) + l1p, jnp.minimum(z, 0.0) - l1p


def _sigmoid(x):
    return 1.0 / (1.0 + jnp.exp(-x))


def _gelu_parts(x):
    k = 0.7978845608028654
    inner = k * (x + 0.044715 * (x * x * x))
    t = jnp.tanh(inner)
    gelu = 0.5 * x * (1.0 + t)
    dgelu = 0.5 * (1.0 + t) + 0.5 * x * (1.0 - t * t) * (k * (1.0 + 3.0 * 0.044715 * (x * x)))
    return gelu, dgelu


def _norm_mm(h, g, w, *, out_dtype, name):
    T, D = h.shape
    S, _, n = w.shape
    tm = _tile(T, 1056, 16)
    tn = _tile(n, 768, LANE)
    nj = n // tn

    def body(h_ref, g_ref, w_ref, hn_ref, o_ref):
        @pl.when(pl.program_id(1) == 0)
        def _():
            x = h_ref[...]
            hn_ref[...] = (x * _rstd(x) * g_ref[...]).astype(BF16)

        o_ref[...] = jnp.dot(hn_ref[...], w_ref[...], preferred_element_type=F32).astype(out_dtype)

    return _pcall(
        body, name=name, grid=(T // tm, S * nj),
        in_specs=[pl.BlockSpec((tm, D), lambda i, j: (i, 0)),
                  pl.BlockSpec((1, D), lambda i, j: (0, 0)),
                  pl.BlockSpec((None, D, tn), lambda i, j: (j // nj, 0, j % nj))],
        out_specs=[pl.BlockSpec((tm, D), lambda i, j: (i, 0)),
                   pl.BlockSpec((tm, tn), lambda i, j: (i, j))],
        out_shape=[_sds((T, D), BF16), _sds((T, S * n), out_dtype)],
        compiler_params=_params("parallel", "arbitrary"),
    )(h, g, w)


def _mm_res(a, w, res, *, name):
    T, K = a.shape
    N = w.shape[1]
    tm = _tile(T, 1056, 16)

    def body(a_ref, w_ref, r_ref, o_ref):
        o_ref[...] = r_ref[...] + jnp.dot(a_ref[...], w_ref[...], preferred_element_type=F32)

    return _pcall(
        body, name=name, grid=(T // tm,),
        in_specs=[pl.BlockSpec((tm, K), lambda i: (i, 0)),
                  pl.BlockSpec((K, N), lambda i: (0, 0)),
                  pl.BlockSpec((tm, N), lambda i: (i, 0))],
        out_specs=pl.BlockSpec((tm, N), lambda i: (i, 0)),
        out_shape=_sds((T, N), F32),
        compiler_params=_params("parallel"),
    )(a, w, res)


def _mm_nt(a, w, *, out_dtype, name):
    T, N = a.shape
    K = w.shape[0]
    tm = _tile(T, 1056, 16)

    def body(a_ref, w_ref, o_ref):
        o_ref[...] = lax.dot_general(a_ref[...], w_ref[...], (((1,), (1,)), ((), ())),
                                     preferred_element_type=F32).astype(out_dtype)

    return _pcall(
        body, name=name, grid=(T // tm,),
        in_specs=[pl.BlockSpec((tm, N), lambda i: (i, 0)),
                  pl.BlockSpec((K, N), lambda i: (0, 0))],
        out_specs=pl.BlockSpec((tm, K), lambda i: (i, 0)),
        out_shape=_sds((T, K), out_dtype),
        compiler_params=_params("parallel"),
    )(a, w)


def _mm_tn(a, b, *, shards, relu2, name):
    T, Ka = a.shape
    Nb = b.shape[1]
    n = Nb // shards
    tka = _tile(Ka, 512, LANE)
    tnb = _tile(n, 512, LANE)
    nj = n // tnb

    def body(a_ref, b_ref, o_ref):
        av = a_ref[...]
        if relu2:
            r = jnp.maximum(av, 0)
            av = r * r
        o_ref[...] = lax.dot_general(av, b_ref[...], (((0,), (0,)), ((), ())),
                                     preferred_element_type=F32)

    return _pcall(
        body, name=name, grid=(Ka // tka, shards * nj),
        in_specs=[pl.BlockSpec((T, tka), lambda i, j: (0, i)),
                  pl.BlockSpec((T, tnb), lambda i, j: (0, j))],
        out_specs=pl.BlockSpec((None, tka, tnb), lambda i, j: (j // nj, i, j % nj)),
        out_shape=_sds((shards, Ka, n), F32),
        compiler_params=_params("parallel", "parallel"),
    )(a, b)


def _mm_nt_normbwd(dy, w, h, g, dres, *, name):
    T, D = h.shape
    S, _, n = w.shape
    tm = _tile(T, 528, 16)

    def body(dy_ref, w_ref, h_ref, g_ref, dr_ref, dh_ref, dhb_ref, dg_ref, acc_ref):
        i, s = pl.program_id(0), pl.program_id(1)
        part = lax.dot_general(dy_ref[...], w_ref[...], (((1,), (1,)), ((), ())),
                               preferred_element_type=F32)

        @pl.when(s == 0)
        def _():
            acc_ref[...] = part

        @pl.when(s > 0)
        def _():
            acc_ref[...] += part

        @pl.when(s == S - 1)
        def _():
            dx, dg = _norm_bwd(h_ref[...], g_ref[...], acc_ref[...])
            dh = dr_ref[...] + dx
            dh_ref[...] = dh
            dhb_ref[...] = dh.astype(BF16)

            @pl.when(i == 0)
            def _():
                dg_ref[...] = dg

            @pl.when(i > 0)
            def _():
                dg_ref[...] += dg

    return _pcall(
        body, name=name, grid=(T // tm, S),
        in_specs=[pl.BlockSpec((tm, n), lambda i, s: (i, s)),
                  pl.BlockSpec((None, D, n), lambda i, s: (s, 0, 0)),
                  pl.BlockSpec((tm, D), lambda i, s: (i, 0)),
                  pl.BlockSpec((1, D), lambda i, s: (0, 0)),
                  pl.BlockSpec((tm, D), lambda i, s: (i, 0))],
        out_specs=[pl.BlockSpec((tm, D), lambda i, s: (i, 0)),
                   pl.BlockSpec((tm, D), lambda i, s: (i, 0)),
                   pl.BlockSpec((1, D), lambda i, s: (0, 0))],
        out_shape=[_sds((T, D), F32), _sds((T, D), BF16), _sds((1, D), F32)],
        scratch_shapes=[pltpu.VMEM((tm, D), F32)],
        compiler_params=_params("arbitrary", "arbitrary"),
    )(dy, w, h, g, dres)


def _mlp_fwd(h, g, w_up, w_down, *, name):
    T, D = h.shape
    S, _, n = w_up.shape
    tm = _tile(T, 528, 16)
    tf = _tile(n, 512, LANE)
    nj = n // tf
    nf = S * nj

    def body(h_ref, g_ref, wu_ref, wd_ref, o_ref, hn_ref, up_ref, acc_ref):
        f = pl.program_id(1)

        @pl.when(f == 0)
        def _():
            x = h_ref[...]
            hn_ref[...] = (x * _rstd(x) * g_ref[...]).astype(BF16)

        up = jnp.dot(hn_ref[...], wu_ref[...], preferred_element_type=F32)
        up_ref[...] = up.astype(BF16)
        r = jnp.maximum(up, 0.0)
        part = jnp.dot((r * r).astype(BF16), wd_ref[...], preferred_element_type=F32)

        @pl.when(f == 0)
        def _():
            acc_ref[...] = part

        @pl.when(f > 0)
        def _():
            acc_ref[...] += part

        @pl.when(f == nf - 1)
        def _():
            o_ref[...] = h_ref[...] + acc_ref[...]

    return _pcall(
        body, name=name, grid=(T // tm, nf),
        in_specs=[pl.BlockSpec((tm, D), lambda i, f: (i, 0)),
                  pl.BlockSpec((1, D), lambda i, f: (0, 0)),
                  pl.BlockSpec((None, D, tf), lambda i, f: (f // nj, 0, f % nj)),
                  pl.BlockSpec((tf, D), lambda i, f: (f, 0))],
        out_specs=[pl.BlockSpec((tm, D), lambda i, f: (i, 0)),
                   pl.BlockSpec((tm, D), lambda i, f: (i, 0)),
                   pl.BlockSpec((tm, tf), lambda i, f: (i, f))],
        out_shape=[_sds((T, D), F32), _sds((T, D), BF16), _sds((T, S * n), BF16)],
        scratch_shapes=[pltpu.VMEM((tm, D), F32)],
        compiler_params=_params("parallel", "arbitrary"),
    )(h, g, w_up, w_down)


def _mlp_bwd(dy, h, g, up, w_up, w_down, *, name):
    T, D = h.shape
    S, _, n = w_up.shape
    tm = _tile(T, 528, 16)
    tf = _tile(n, 512, LANE)
    nj = n // tf
    nf = S * nj

    def body(dy_ref, h_ref, g_ref, up_ref, wu_ref, wd_ref, dup_ref, dh_ref, dhb_ref, dg_ref,
             dyb_ref, acc_ref):
        i, f = pl.program_id(0), pl.program_id(1)

        @pl.when(f == 0)
        def _():
            dyb_ref[...] = dy_ref[...].astype(BF16)

        dact = lax.dot_general(dyb_ref[...], wd_ref[...], (((1,), (1,)), ((), ())),
                               preferred_element_type=F32)
        r = jnp.maximum(up_ref[...].astype(F32), 0.0)
        dup = (dact * (2.0 * r)).astype(BF16)
        dup_ref[...] = dup
        part = lax.dot_general(dup, wu_ref[...], (((1,), (1,)), ((), ())),
                               preferred_element_type=F32)

        @pl.when(f == 0)
        def _():
            acc_ref[...] = part

        @pl.when(f > 0)
        def _():
            acc_ref[...] += part

        @pl.when(f == nf - 1)
        def _():
            dx, dg = _norm_bwd(h_ref[...], g_ref[...], acc_ref[...])
            dh = dy_ref[...] + dx
            dh_ref[...] = dh
            dhb_ref[...] = dh.astype(BF16)

            @pl.when(i == 0)
            def _():
                dg_ref[...] = dg

            @pl.when(i > 0)
            def _():
                dg_ref[...] += dg

    return _pcall(
        body, name=name, grid=(T // tm, nf),
        in_specs=[pl.BlockSpec((tm, D), lambda i, f: (i, 0)),
                  pl.BlockSpec((tm, D), lambda i, f: (i, 0)),
                  pl.BlockSpec((1, D), lambda i, f: (0, 0)),
                  pl.BlockSpec((tm, tf), lambda i, f: (i, f)),
                  pl.BlockSpec((None, D, tf), lambda i, f: (f // nj, 0, f % nj)),
                  pl.BlockSpec((tf, D), lambda i, f: (f, 0))],
        out_specs=[pl.BlockSpec((tm, tf), lambda i, f: (i, f)),
                   pl.BlockSpec((tm, D), lambda i, f: (i, 0)),
                   pl.BlockSpec((tm, D), lambda i, f: (i, 0)),
                   pl.BlockSpec((1, D), lambda i, f: (0, 0))],
        out_shape=[_sds((T, S * n), BF16), _sds((T, D), F32), _sds((T, D), BF16), _sds((1, D), F32)],
        scratch_shapes=[pltpu.VMEM((tm, D), BF16), pltpu.VMEM((tm, D), F32)],
        compiler_params=_params("arbitrary", "arbitrary"),
    )(dy, h, g, up, w_up, w_down)


ATT_BLOCK = 128


def _attn_tile(T):
    for w in (3 * ATT_BLOCK, 2 * ATT_BLOCK):
        if T % w == 0:
            return w
    return ATT_BLOCK


def _tri(strict_lower):
    B = ATT_BLOCK
    r = lax.broadcasted_iota(jnp.int32, (2 * B, B), 0)
    r = jnp.where(r >= B, r - B, r)
    c = lax.broadcasted_iota(jnp.int32, (2 * B, B), 1)
    m = (r > c) if strict_lower else (r < c)
    return jnp.where(m, 1.0, 0.0).astype(BF16)


def _split_dot(x, tri):
    hi = x.astype(BF16)
    lo = (x - hi.astype(F32)).astype(BF16)
    return jnp.dot(jnp.concatenate([hi, lo], axis=1), tri, preferred_element_type=F32)


def _causal_mask(W):
    r = lax.broadcasted_iota(jnp.int32, (W, W), 0)
    c = lax.broadcasted_iota(jnp.int32, (W, W), 1)
    return c < r


def _attn_scores(q, ks, scale, carry, tri, masked):
    W = q.shape[0]
    B = ATT_BLOCK
    z = lax.dot_general(q, ks, (((1,), (1,)), ((), ())), preferred_element_type=F32) * scale
    sp, logsig = _softplus_parts(z)
    lk = -sp
    if masked:
        causal = _causal_mask(W)
        lk = jnp.where(causal, lk, 0.0)
    afters = []
    for b in reversed(range(W // B)):
        blk = lk[:, b * B:(b + 1) * B]
        within = _split_dot(blk, tri)
        afters.append(within + carry)
        carry = carry + (within[:, 0:1] + blk[:, 0:1])
    after = jnp.concatenate(afters[::-1], axis=1)
    w = jnp.exp(logsig + after)
    if masked:
        w = jnp.where(causal, w, 0.0)
    return w, jnp.exp(logsig), carry


def _attn_fwd(qkv, *, name):
    T = qkv.shape[0]
    D = qkv.shape[1] // 3
    W = _attn_tile(T)
    npairs = D // LANE
    nq = T // W
    scale = HEAD_DIM ** -0.5

    def body(q_ref, k_ref, v_ref, o_ref):
        i = pl.program_id(1)
        tri = _tri(True)
        for hh in range(2):
            cols = slice(hh * HEAD_DIM, (hh + 1) * HEAD_DIM)
            q = q_ref[:, cols]

            def tile(j, carry, acc, masked):
                rows = pl.ds(pl.multiple_of(j * W, W), W)
                w, _, carry = _attn_scores(q, k_ref[rows, cols], scale, carry, tri, masked)
                acc = acc + jnp.dot(w.astype(BF16), v_ref[rows, cols], preferred_element_type=F32)
                return carry, acc

            carry, acc = tile(i, jnp.zeros((W, 1), F32), jnp.zeros((W, HEAD_DIM), F32), True)
            carry, acc = lax.fori_loop(
                0, i, lambda jj, ca: tile(i - 1 - jj, ca[0], ca[1], False), (carry, acc))
            o_ref[:, cols] = acc.astype(BF16)

    return _pcall(
        body, name=name, grid=(npairs, nq),
        in_specs=[pl.BlockSpec((W, LANE), lambda p, i: (i, p)),
                  pl.BlockSpec((T, LANE), lambda p, i: (0, npairs + p)),
                  pl.BlockSpec((T, LANE), lambda p, i: (0, 2 * npairs + p))],
        out_specs=pl.BlockSpec((W, LANE), lambda p, i: (i, p)),
        out_shape=_sds((T, D), BF16),
        compiler_params=_params("parallel", "arbitrary"),
    )(qkv, qkv, qkv)


def _attn_bwd(qkv, do, *, name):
    T = qkv.shape[0]
    D = qkv.shape[1] // 3
    B = ATT_BLOCK
    W = _attn_tile(T)
    npairs = D // LANE
    nq = T // W
    scale = HEAD_DIM ** -0.5

    def body(q_ref, k_ref, v_ref, do_ref, dq_ref, dk_ref, dv_ref, s_s, w_s, dk_acc, dv_acc):
        i = pl.program_id(1)

        @pl.when(i == 0)
        def _():
            dk_acc[...] = jnp.zeros_like(dk_acc)
            dv_acc[...] = jnp.zeros_like(dv_acc)

        tri_after = _tri(True)
        tri_before = _tri(False)
        for hh in range(2):
            cols = slice(hh * HEAD_DIM, (hh + 1) * HEAD_DIM)
            q = q_ref[:, cols]
            dout = do_ref[:, cols]

            def score(j, carry, masked):
                rows = pl.ds(pl.multiple_of(j * W, W), W)
                w, sig, carry = _attn_scores(q, k_ref[rows, cols], scale, carry, tri_after, masked)
                s_s[j] = sig
                w_s[j] = w
                return carry

            carry = score(i, jnp.zeros((W, 1), F32), True)
            lax.fori_loop(0, i, lambda jj, c: score(i - 1 - jj, c, False), carry)

            def grad(j, gsum, dq, masked):
                rows = pl.ds(pl.multiple_of(j * W, W), W)
                ks = k_ref[rows, cols]
                sig = s_s[j]
                w = w_s[j]
                dw = lax.dot_general(dout, v_ref[rows, cols], (((1,), (1,)), ((), ())),
                                     preferred_element_type=F32)
                g = dw * w
                befores = []
                for b in range(W // B):
                    blk = g[:, b * B:(b + 1) * B]
                    within = _split_dot(blk, tri_before)
                    befores.append(within + gsum)
                    gsum = gsum + (within[:, B - 1:B] + blk[:, B - 1:B])
                before = jnp.concatenate(befores, axis=1)
                dz = (g * (1.0 - sig) - sig * before) * scale
                if masked:
                    dz = jnp.where(_causal_mask(W), dz, 0.0)
                dzb = dz.astype(BF16)
                dq = dq + jnp.dot(dzb, ks, preferred_element_type=F32)
                dk_acc[rows, cols] += lax.dot_general(dzb, q, (((0,), (0,)), ((), ())),
                                                      preferred_element_type=F32)
                dv_acc[rows, cols] += lax.dot_general(w.astype(BF16), dout, (((0,), (0,)), ((), ())),
                                                      preferred_element_type=F32)
                return gsum, dq

            gsum, dq = lax.fori_loop(
                0, i, lambda j, c: grad(j, c[0], c[1], False),
                (jnp.zeros((W, 1), F32), jnp.zeros((W, HEAD_DIM), F32)))
            _, dq = grad(i, gsum, dq, True)
            dq_ref[:, cols] = dq.astype(BF16)

        @pl.when(i == nq - 1)
        def _():
            dk_ref[...] = dk_acc[...].astype(BF16)
            dv_ref[...] = dv_acc[...].astype(BF16)

    return _pcall(
        body, name=name, grid=(npairs, nq),
        in_specs=[pl.BlockSpec((W, LANE), lambda p, i: (i, p)),
                  pl.BlockSpec((T, LANE), lambda p, i: (0, npairs + p)),
                  pl.BlockSpec((T, LANE), lambda p, i: (0, 2 * npairs + p)),
                  pl.BlockSpec((W, LANE), lambda p, i: (i, p))],
        out_specs=[pl.BlockSpec((W, LANE), lambda p, i: (i, p)),
                   pl.BlockSpec((T, LANE), lambda p, i: (0, p)),
                   pl.BlockSpec((T, LANE), lambda p, i: (0, p))],
        out_shape=[_sds((T, D), BF16)] * 3,
        scratch_shapes=[pltpu.VMEM((nq, W, W), F32), pltpu.VMEM((nq, W, W), F32),
                        pltpu.VMEM((T, LANE), F32), pltpu.VMEM((T, LANE), F32)],
        compiler_params=_params("arbitrary", "arbitrary"),
    )(qkv, qkv, qkv, do)


HALO = SUBLANE


def _lru_gates(u, w_rg, b_rg, w_ig, b_ig, lam):
    nb = w_rg.shape[0]
    pre_r, pre_i = [], []
    for n in range(nb):
        ub = u[:, n * LANE:(n + 1) * LANE].astype(BF16)
        pre_r.append(jnp.dot(ub, w_rg[n], preferred_element_type=F32))
        pre_i.append(jnp.dot(ub, w_ig[n], preferred_element_type=F32))
    r = _sigmoid(jnp.concatenate(pre_r, axis=1) + b_rg)
    i = _sigmoid(jnp.concatenate(pre_i, axis=1) + b_ig)
    c = -LRU_C * _softplus_parts(-lam)[0]
    log_a = c * r
    a = jnp.exp(log_a)
    x2 = 2.0 * log_a
    em1 = jnp.where(jnp.abs(x2) < 1e-2, x2 * (1.0 + x2 * (0.5 + x2 * (1.0 / 6.0))), jnp.exp(x2) - 1.0)
    mult = jnp.sqrt(-em1)
    return r, i, a, mult, c


def _conv_rows(buf_ref, tt, conv_w, conv_b):
    u = conv_b
    for j in range(4):
        u = u + buf_ref[pl.ds(HALO - 3 + j, tt), :] * conv_w[j:j + 1, :]
    return u


def _fill_with_halo(buf_ref, prev_ref, cur_ref, first):
    tt = cur_ref.shape[0]
    buf_ref[pl.ds(0, HALO), :] = jnp.where(first, 0.0, prev_ref[...])
    buf_ref[pl.ds(HALO, tt), :] = cur_ref[...]


def _lru_time_tile(T):
    return _tile(T, 256, SUBLANE)


def _lru_pre(gr, conv_w, conv_b, w_rg, b_rg, w_ig, b_ig, lam, *, name):
    T = gr.shape[0]
    D = gr.shape[1] // 2
    tt = _lru_time_tile(T)
    hb = tt // HALO

    def body(x_ref, xp_ref, cw_ref, cb_ref, wr_ref, br_ref, wi_ref, bi_ref, lam_ref, a_ref, b_ref, buf):
        _fill_with_halo(buf, xp_ref, x_ref, pl.program_id(0) == 0)
        u = _conv_rows(buf, tt, cw_ref[...], cb_ref[...])
        _, i, a, mult, _ = _lru_gates(u, wr_ref, br_ref[...], wi_ref, bi_ref[...], lam_ref[...])
        a_ref[...] = a
        b_ref[...] = mult * (i * u)

    vec = pl.BlockSpec((1, D), lambda t: (0, 0))
    mat = pl.BlockSpec(w_rg.shape, lambda t: (0, 0, 0))
    return _pcall(
        body, name=name, grid=(T // tt,),
        in_specs=[pl.BlockSpec((tt, D), lambda t: (t, 1)),
                  pl.BlockSpec((HALO, D), lambda t: (jnp.maximum(t * hb - 1, 0), 1)),
                  pl.BlockSpec((4, D), lambda t: (0, 0)), vec, mat, vec, mat, vec, vec],
        out_specs=[pl.BlockSpec((tt, D), lambda t: (t, 0))] * 2,
        out_shape=[_sds((T, D), F32)] * 2,
        scratch_shapes=[pltpu.VMEM((tt + HALO, D), F32)],
        compiler_params=_params("parallel"),
    )(gr, gr, conv_w, conv_b, w_rg, b_rg, w_ig, b_ig, lam)


def _lru_scan(a, b, *, reverse, name):
    T, D = a.shape
    nb = D // LANE
    ts = _tile(T, 1056, SUBLANE)
    nt = T // ts
    a3 = a.reshape(T, nb, LANE)
    b3 = b.reshape(T, nb, LANE)

    def body(a_ref, b_ref, o_ref, carry):
        @pl.when(pl.program_id(0) == 0)
        def _():
            carry[...] = jnp.zeros_like(carry)

        if reverse:
            def step(k, c):
                t = ts - 1 - k
                l = b_ref[t] + c
                o_ref[t] = l
                return a_ref[t] * l
        else:
            def step(k, h):
                h = a_ref[k] * h + b_ref[k]
                o_ref[k] = h
                return h

        carry[...] = lax.fori_loop(0, ts, step, carry[...], unroll=8)

    if reverse:
        spec = pl.BlockSpec((ts, nb, LANE), lambda t: (nt - 1 - t, 0, 0))
    else:
        spec = pl.BlockSpec((ts, nb, LANE), lambda t: (t, 0, 0))
    out = _pcall(
        body, name=name, grid=(nt,),
        in_specs=[spec, spec], out_specs=spec,
        out_shape=_sds((T, nb, LANE), F32),
        scratch_shapes=[pltpu.VMEM((nb, LANE), F32)],
        compiler_params=_params("arbitrary"),
    )(a3, b3)
    return out.reshape(T, D)


def _lru_out(gr, hs, *, name):
    T, D = hs.shape
    tt = _tile(T, 1056, 16)

    def body(g_ref, h_ref, y_ref):
        y_ref[...] = (h_ref[...] * _gelu_parts(g_ref[...])[0]).astype(BF16)

    return _pcall(
        body, name=name, grid=(T // tt,),
        in_specs=[pl.BlockSpec((tt, D), lambda t: (t, 0)), pl.BlockSpec((tt, D), lambda t: (t, 0))],
        out_specs=pl.BlockSpec((tt, D), lambda t: (t, 0)),
        out_shape=_sds((T, D), BF16),
        compiler_params=_params("parallel"),
    )(gr, hs)


def _lru_out_bwd(gr, hs, dy, *, name):
    T, D = hs.shape
    tt = _tile(T, 1056, 16)

    def body(g_ref, h_ref, dy_ref, dg_ref, dh_ref):
        gelu, dgelu = _gelu_parts(g_ref[...])
        dy = dy_ref[...]
        dg_ref[...] = (dy * h_ref[...] * dgelu).astype(BF16)
        dh_ref[...] = dy * gelu

    spec = pl.BlockSpec((tt, D), lambda t: (t, 0))
    return _pcall(
        body, name=name, grid=(T // tt,),
        in_specs=[spec, spec, spec], out_specs=[spec, spec],
        out_shape=[_sds((T, D), BF16), _sds((T, D), F32)],
        compiler_params=_params("parallel"),
    )(gr, hs, dy)


def _lru_gate_bwd(gr, hs, lmb, conv_w, conv_b, w_rg, b_rg, w_ig, b_ig, lam, *, name):
    T, D = hs.shape
    nb = w_rg.shape[0]
    tt = _lru_time_tile(T)
    hb = tt // HALO
    nt = T // tt

    def body(x_ref, xp_ref, h_ref, hp_ref, l_ref, cw_ref, cb_ref, wr_ref, br_ref, wi_ref, bi_ref, lam_ref,
             du_ref, dwr_ref, dbr_ref, dwi_ref, dbi_ref, dlam_ref, xbuf, hbuf):
        t = pl.program_id(0)
        first = t == 0
        _fill_with_halo(xbuf, xp_ref, x_ref, first)
        _fill_with_halo(hbuf, hp_ref, h_ref, first)
        u = _conv_rows(xbuf, tt, cw_ref[...], cb_ref[...])
        lam_v = lam_ref[...]
        r, i, a, mult, c = _lru_gates(u, wr_ref, br_ref[...], wi_ref, bi_ref[...], lam_v)
        l = l_ref[...]
        h_prev = hbuf[pl.ds(HALO - 1, tt), :]
        dlog_a = l * h_prev * a - l * (i * u) * (a * a) / mult
        d_iu = l * mult
        dpre_r = (dlog_a * c) * (r * (1.0 - r))
        dpre_i = (d_iu * u) * (i * (1.0 - i))
        dpr_b = dpre_r.astype(BF16)
        dpi_b = dpre_i.astype(BF16)
        du_parts, dwr, dwi = [], [], []
        for n in range(nb):
            cs = slice(n * LANE, (n + 1) * LANE)
            ub = u[:, cs].astype(BF16)
            du_parts.append(
                lax.dot_general(dpr_b[:, cs], wr_ref[n], (((1,), (1,)), ((), ())), preferred_element_type=F32)
                + lax.dot_general(dpi_b[:, cs], wi_ref[n], (((1,), (1,)), ((), ())), preferred_element_type=F32))
            dwr.append(lax.dot_general(ub, dpr_b[:, cs], (((0,), (0,)), ((), ())), preferred_element_type=F32))
            dwi.append(lax.dot_general(ub, dpi_b[:, cs], (((0,), (0,)), ((), ())), preferred_element_type=F32))
        du_ref[...] = d_iu * i + jnp.concatenate(du_parts, axis=1)
        dbr = jnp.sum(dpre_r, axis=0, keepdims=True)
        dbi = jnp.sum(dpre_i, axis=0, keepdims=True)
        dc = jnp.sum(dlog_a * r, axis=0, keepdims=True)

        @pl.when(first)
        def _():
            for n in range(nb):
                dwr_ref[n] = dwr[n]
                dwi_ref[n] = dwi[n]
            dbr_ref[...] = dbr
            dbi_ref[...] = dbi
            dlam_ref[...] = dc

        @pl.when(t > 0)
        def _():
            for n in range(nb):
                dwr_ref[n] += dwr[n]
                dwi_ref[n] += dwi[n]
            dbr_ref[...] += dbr
            dbi_ref[...] += dbi
            dlam_ref[...] += dc

        @pl.when(t == nt - 1)
        def _():
            dlam_ref[...] = dlam_ref[...] * (LRU_C * _sigmoid(-lam_v))

    vec = pl.BlockSpec((1, D), lambda t: (0, 0))
    mat = pl.BlockSpec(w_rg.shape, lambda t: (0, 0, 0))
    blk = pl.BlockSpec((tt, D), lambda t: (t, 0))
    prev = pl.BlockSpec((HALO, D), lambda t: (jnp.maximum(t * hb - 1, 0), 0))
    return _pcall(
        body, name=name, grid=(nt,),
        in_specs=[pl.BlockSpec((tt, D), lambda t: (t, 1)),
                  pl.BlockSpec((HALO, D), lambda t: (jnp.maximum(t * hb - 1, 0), 1)),
                  blk, prev, blk,
                  pl.BlockSpec((4, D), lambda t: (0, 0)), vec, mat, vec, mat, vec, vec],
        out_specs=[blk, mat, vec, mat, vec, vec],
        out_shape=[_sds((T, D), F32), _sds(w_rg.shape, F32), _sds((1, D), F32),
                   _sds(w_rg.shape, F32), _sds((1, D), F32), _sds((1, D), F32)],
        scratch_shapes=[pltpu.VMEM((tt + HALO, D), F32), pltpu.VMEM((tt + HALO, D), F32)],
        compiler_params=_params("arbitrary"),
    )(gr, gr, hs, hs, lmb, conv_w, conv_b, w_rg, b_rg, w_ig, b_ig, lam)


def _lru_conv_bwd(gr, du, conv_w, *, name):
    T, D = du.shape
    tt = _lru_time_tile(T)
    hb = tt // HALO
    nt = T // tt

    def body(x_ref, xp_ref, du_ref, dun_ref, cw_ref, dx_ref, dcw_ref, dcb_ref, xbuf, dbuf):
        t = pl.program_id(0)
        _fill_with_halo(xbuf, xp_ref, x_ref, t == 0)
        du = du_ref[...]
        dbuf[pl.ds(0, tt), :] = du
        dbuf[pl.ds(tt, HALO), :] = jnp.where(t == nt - 1, 0.0, dun_ref[...])
        cw = cw_ref[...]
        dx = jnp.zeros((tt, D), F32)
        dcw = []
        for j in range(4):
            dx = dx + dbuf[pl.ds(3 - j, tt), :] * cw[j:j + 1, :]
            dcw.append(jnp.sum(du * xbuf[pl.ds(HALO - 3 + j, tt), :], axis=0, keepdims=True))
        dx_ref[...] = dx.astype(BF16)
        dcw = jnp.concatenate(dcw, axis=0)
        dcb = jnp.sum(du, axis=0, keepdims=True)

        @pl.when(t == 0)
        def _():
            dcw_ref[...] = dcw
            dcb_ref[...] = dcb

        @pl.when(t > 0)
        def _():
            dcw_ref[...] += dcw
            dcb_ref[...] += dcb

    blk = pl.BlockSpec((tt, D), lambda t: (t, 0))
    return _pcall(
        body, name=name, grid=(nt,),
        in_specs=[pl.BlockSpec((tt, D), lambda t: (t, 1)),
                  pl.BlockSpec((HALO, D), lambda t: (jnp.maximum(t * hb - 1, 0), 1)),
                  blk,
                  pl.BlockSpec((HALO, D), lambda t: (jnp.minimum((t + 1) * hb, T // HALO - 1), 0)),
                  pl.BlockSpec((4, D), lambda t: (0, 0))],
        out_specs=[blk, pl.BlockSpec((4, D), lambda t: (0, 0)), pl.BlockSpec((1, D), lambda t: (0, 0))],
        out_shape=[_sds((T, D), BF16), _sds((4, D), F32), _sds((1, D), F32)],
        scratch_shapes=[pltpu.VMEM((tt + HALO, D), F32), pltpu.VMEM((tt + HALO, D), F32)],
        compiler_params=_params("arbitrary"),
    )(gr, gr, du, du, conv_w)


def _loss_head(h, g, target, *, row_lo, row_hi, name):
    T, D = h.shape
    tm = _tile(T, 1056, 16)

    def body(h_ref, g_ref, t_ref, loss_ref, dh_ref, dhb_ref, dg_ref):
        i = pl.program_id(0)
        x = h_ref[...]
        g = g_ref[...]
        row = i * tm + lax.broadcasted_iota(jnp.int32, (tm, 1), 0)
        valid = jnp.logical_and(row >= row_lo, row < row_hi)
        rstd = _rstd(x)
        n = x * rstd
        err = jnp.where(valid, n * g - t_ref[...], 0.0)
        part = (0.5 / D) * jnp.sum(jnp.sum(err * err, axis=1, keepdims=True), axis=0, keepdims=True)
        dy = err * (1.0 / D)
        dn = dy * g
        dh = rstd * (dn - n * jnp.mean(dn * n, axis=-1, keepdims=True))
        dh_ref[...] = dh
        dhb_ref[...] = dh.astype(BF16)
        dg = jnp.sum(dy * n, axis=0, keepdims=True)

        @pl.when(i == 0)
        def _():
            loss_ref[...] = part
            dg_ref[...] = dg

        @pl.when(i > 0)
        def _():
            loss_ref[...] += part
            dg_ref[...] += dg

    blk = pl.BlockSpec((tm, D), lambda i: (i, 0))
    vec = pl.BlockSpec((1, D), lambda i: (0, 0))
    return _pcall(
        body, name=name, grid=(T // tm,),
        in_specs=[blk, vec, blk],
        out_specs=[pl.BlockSpec((1, 1), lambda i: (0, 0)), blk, blk, vec],
        out_shape=[_sds((1, 1), F32), _sds((T, D), F32), _sds((T, D), BF16), _sds((1, D), F32)],
        compiler_params=_params("arbitrary"),
    )(h, g, target)


def _adamw(w, g, m, v, *, name):
    R, C = w.shape
    tr = _tile(R, 512, SUBLANE)
    c1 = 1.0 / (1.0 - ADAM_B1 ** ADAM_STEP)
    c2 = 1.0 / (1.0 - ADAM_B2 ** ADAM_STEP)

    def body(w_ref, g_ref, m_ref, v_ref, d_ref, nm_ref, nv_ref):
        g = g_ref[...]
        m = ADAM_B1 * m_ref[...] + (1.0 - ADAM_B1) * g
        v = ADAM_B2 * v_ref[...] + (1.0 - ADAM_B2) * (g * g)
        nm_ref[...] = m
        nv_ref[...] = v
        d_ref[...] = -ADAM_LR * ((m * c1) / (jnp.sqrt(v * c2) + ADAM_EPS) + ADAM_WD * w_ref[...])

    blk = pl.BlockSpec((tr, C), lambda i: (i, 0))
    return _pcall(
        body, name=name, grid=(R // tr,),
        in_specs=[blk] * 4, out_specs=[blk] * 3,
        out_shape=[_sds((R, C), F32)] * 3,
        compiler_params=_params("parallel"),
    )(w, g, m, v)


ANY = pl.BlockSpec(memory_space=pl.ANY)


def _place():
    x, y, c = lax.axis_index("x"), lax.axis_index("y"), lax.axis_index("c")
    chips = [(1 - x, y), (x, 1 - y), (1 - x, 1 - y)]
    return x, y, c, chips


def _all_gather_chips(v, *, name):
    def body(v_ref, o_ref, send_sems, recv_sems, local_sem):
        x, y, c, chips = _place()
        me = 2 * x + y

        def copy(k, block, half, to, src=None):
            dst = o_ref.at[block, half]
            return pltpu.make_async_remote_copy(
                src_ref=dst if src is None else src, dst_ref=dst,
                send_sem=send_sems.at[k], recv_sem=recv_sems.at[k],
                device_id=to, device_id_type=MESH)

        mine = pltpu.make_async_copy(v_ref, o_ref.at[me], local_sem)
        mine.start()
        first = [copy(k, me, c, (cx, cy, c), src=v_ref.at[c]) for k, (cx, cy) in enumerate(chips)]
        for cp in first:
            cp.start()
        passed = [copy(3 + k, 2 * cx + cy, c, (x, y, 1 - c)) for k, (cx, cy) in enumerate(chips)]
        for k, (cx, cy) in enumerate(chips):
            copy(k, 2 * cx + cy, c, (x, y, c)).wait_recv()
            passed[k].start()
        for k, (cx, cy) in enumerate(chips):
            copy(3 + k, 2 * cx + cy, 1 - c, (x, y, c)).wait_recv()
        for cp in first + passed:
            cp.wait_send()
        mine.wait()

    return _pcall(
        body, name=name,
        in_specs=[ANY], out_specs=ANY,
        out_shape=_sds((4,) + v.shape, v.dtype),
        scratch_shapes=[pltpu.SemaphoreType.DMA((6,)), pltpu.SemaphoreType.DMA((6,)),
                        pltpu.SemaphoreType.DMA],
    )(v)


def _swap_halves(g, *, name):
    def body(g_ref, o_ref, send_sem, recv_sem):
        x, y, c, _ = _place()
        cp = pltpu.make_async_remote_copy(
            src_ref=g_ref.at[:, 1 - c], dst_ref=o_ref, send_sem=send_sem, recv_sem=recv_sem,
            device_id=(x, y, 1 - c), device_id_type=MESH)
        cp.start()
        cp.wait()

    return _pcall(
        body, name=name, in_specs=[ANY], out_specs=ANY,
        out_shape=_sds((4,) + g.shape[2:], g.dtype),
        scratch_shapes=[pltpu.SemaphoreType.DMA, pltpu.SemaphoreType.DMA],
    )(g)


def _scatter_chips(a, *, name):
    def body(a_ref, o_ref, send_sems, recv_sems):
        x, y, c, chips = _place()
        me = 2 * x + y
        sends = [pltpu.make_async_remote_copy(
            src_ref=a_ref.at[2 * cx + cy], dst_ref=o_ref.at[me],
            send_sem=send_sems.at[k], recv_sem=recv_sems.at[k],
            device_id=(cx, cy, c), device_id_type=MESH) for k, (cx, cy) in enumerate(chips)]
        for cp in sends:
            cp.start()
        for k, (cx, cy) in enumerate(chips):
            pltpu.make_async_remote_copy(
                src_ref=a_ref.at[me], dst_ref=o_ref.at[2 * cx + cy],
                send_sem=send_sems.at[k], recv_sem=recv_sems.at[k],
                device_id=(x, y, c), device_id_type=MESH).wait_recv()
        for cp in sends:
            cp.wait_send()

    return _pcall(
        body, name=name, in_specs=[ANY], out_specs=ANY,
        out_shape=_sds(a.shape, a.dtype),
        scratch_shapes=[pltpu.SemaphoreType.DMA((3,)), pltpu.SemaphoreType.DMA((3,))],
    )(a)


def _join_halves(r, *, name):
    def body(r_ref, o_ref, send_sem, recv_sem, local_sem):
        x, y, c, _ = _place()
        mine = pltpu.make_async_copy(r_ref, o_ref.at[c], local_sem)
        mine.start()
        cp = pltpu.make_async_remote_copy(
            src_ref=r_ref, dst_ref=o_ref.at[c], send_sem=send_sem, recv_sem=recv_sem,
            device_id=(x, y, 1 - c), device_id_type=MESH)
        cp.start()
        pltpu.make_async_remote_copy(
            src_ref=r_ref, dst_ref=o_ref.at[1 - c], send_sem=send_sem, recv_sem=recv_sem,
            device_id=(x, y, c), device_id_type=MESH).wait_recv()
        cp.wait_send()
        mine.wait()

    return _pcall(
        body, name=name, in_specs=[ANY], out_specs=ANY,
        out_shape=_sds((2,) + r.shape, r.dtype),
        scratch_shapes=[pltpu.SemaphoreType.DMA, pltpu.SemaphoreType.DMA, pltpu.SemaphoreType.DMA],
    )(r)


RED_COLS = 1024
RED_ROWS = 512


def _add_halves(g, got, c, *, name):
    _, _, R, C = g.shape
    tr = _tile(R, 512, SUBLANE)

    def body(c_ref, g_ref, o_ref, out_ref):
        out_ref[...] = g_ref[...] + o_ref[...]

    return _pcall(
        body, name=name,
        grid_spec=pltpu.PrefetchScalarGridSpec(
            num_scalar_prefetch=1, grid=(4, R // tr),
            in_specs=[pl.BlockSpec((None, None, tr, C), lambda s, i, c_ref: (s, c_ref[0], i, 0)),
                      pl.BlockSpec((None, tr, C), lambda s, i, c_ref: (s, i, 0))],
            out_specs=pl.BlockSpec((None, tr, C), lambda s, i, c_ref: (s, i, 0))),
        out_shape=_sds((4, R, C), g.dtype),
        compiler_params=_params("parallel", "parallel"),
    )(c, g, got)


def _add_chips(a, got, me, *, name):
    _, R, C = a.shape
    tr = _tile(R, 256, SUBLANE)

    def body(me_ref, a_ref, o_ref, out_ref):
        me_v = me_ref[0]
        own = a_ref[...]
        acc = jnp.where(me_v == 0, own, o_ref[0])
        for s in range(1, 4):
            acc = acc + jnp.where(me_v == s, own, o_ref[s])
        out_ref[...] = acc

    return _pcall(
        body, name=name,
        grid_spec=pltpu.PrefetchScalarGridSpec(
            num_scalar_prefetch=1, grid=(R // tr,),
            in_specs=[pl.BlockSpec((None, tr, C), lambda i, me_ref: (me_ref[0], i, 0)),
                      pl.BlockSpec((4, tr, C), lambda i, me_ref: (0, i, 0))],
            out_specs=pl.BlockSpec((tr, C), lambda i, me_ref: (i, 0))),
        out_shape=_sds((R, C), a.dtype),
        compiler_params=_params("parallel"),
    )(me, a, got)


def _round_up(n, m):
    return (n + m - 1) // m * m


def _f32_as_bf16(a):
    return lax.bitcast_convert_type(a.astype(F32), BF16).reshape(-1)


def _bf16_as_f32(a):
    return lax.bitcast_convert_type(a.reshape(-1, 2), F32)


def _by_chip_cols(a, cols):
    lead = a.shape[:-1]
    a = a.reshape(lead + (4, cols))
    return jnp.moveaxis(a, -2, 0).reshape(4, -1)


def kernel(x, meta_tokens, norm_mix, norm_mlp, sb_w_qkv, sb_w_o, lru_w_in, lru_conv_w, lru_conv_b, lru_w_rg, lru_b_rg, lru_w_ig, lru_b_ig, lru_lambda, lru_w_out, mlp_w_up, mlp_w_down, norm_final, loss_target, m_meta_tokens, m_norm_mix, m_norm_mlp, m_sb_w_qkv, m_sb_w_o, m_lru_w_in, m_lru_conv_w, m_lru_conv_b, m_lru_w_rg, m_lru_b_rg, m_lru_w_ig, m_lru_b_ig, m_lru_lambda, m_lru_w_out, m_mlp_w_up, m_mlp_w_down, m_norm_final, v_meta_tokens, v_norm_mix, v_norm_mlp, v_sb_w_qkv, v_sb_w_o, v_lru_w_in, v_lru_conv_w, v_lru_conv_b, v_lru_w_rg, v_lru_b_rg, v_lru_w_ig, v_lru_b_ig, v_lru_lambda, v_lru_w_out, v_mlp_w_up, v_mlp_w_down, v_norm_final):
    weights = dict(meta_tokens=meta_tokens, norm_mix=norm_mix, norm_mlp=norm_mlp, sb_w_qkv=sb_w_qkv,
                   sb_w_o=sb_w_o, lru_w_in=lru_w_in, lru_conv_w=lru_conv_w, lru_conv_b=lru_conv_b,
                   lru_w_rg=lru_w_rg, lru_b_rg=lru_b_rg, lru_w_ig=lru_w_ig, lru_b_ig=lru_b_ig,
                   lru_lambda=lru_lambda, lru_w_out=lru_w_out, mlp_w_up=mlp_w_up, mlp_w_down=mlp_w_down,
                   norm_final=norm_final)
    m_in = dict(meta_tokens=m_meta_tokens, norm_mix=m_norm_mix, norm_mlp=m_norm_mlp, sb_w_qkv=m_sb_w_qkv,
                sb_w_o=m_sb_w_o, lru_w_in=m_lru_w_in, lru_conv_w=m_lru_conv_w, lru_conv_b=m_lru_conv_b,
                lru_w_rg=m_lru_w_rg, lru_b_rg=m_lru_b_rg, lru_w_ig=m_lru_w_ig, lru_b_ig=m_lru_b_ig,
                lru_lambda=m_lru_lambda, lru_w_out=m_lru_w_out, mlp_w_up=m_mlp_w_up,
                mlp_w_down=m_mlp_w_down, norm_final=m_norm_final)
    v_in = dict(meta_tokens=v_meta_tokens, norm_mix=v_norm_mix, norm_mlp=v_norm_mlp, sb_w_qkv=v_sb_w_qkv,
                sb_w_o=v_sb_w_o, lru_w_in=v_lru_w_in, lru_conv_w=v_lru_conv_w, lru_conv_b=v_lru_conv_b,
                lru_w_rg=v_lru_w_rg, lru_b_rg=v_lru_b_rg, lru_w_ig=v_lru_w_ig, lru_b_ig=v_lru_b_ig,
                lru_lambda=v_lru_lambda, lru_w_out=v_lru_w_out, mlp_w_up=v_mlp_w_up,
                mlp_w_down=v_mlp_w_down, norm_final=v_norm_final)
    names = list(weights)

    seq, D = x.shape[1], x.shape[2]
    n_meta = meta_tokens.shape[0]
    Dq = D // 4
    T = _round_up(n_meta + seq, ATT_BLOCK)
    nb = lru_w_rg.shape[1]
    F = mlp_w_up.shape[2]
    depth = mlp_w_up.shape[0]
    my_x, my_y, my_c = lax.axis_index("x"), lax.axis_index("y"), lax.axis_index("c")
    c_arr = jnp.reshape(my_c, (1,)).astype(jnp.int32)
    me_arr = jnp.reshape(2 * my_x + my_y, (1,)).astype(jnp.int32)

    big = [sb_w_qkv[0], sb_w_o[0], lru_w_in[0], lru_w_out[0]]
    big += [mlp_w_up[l] for l in range(depth)] + [mlp_w_down[l] for l in range(depth)]
    small = [meta_tokens, lru_conv_w[0], lru_conv_b, lru_b_rg, lru_b_ig, lru_lambda]
    parts = [b.astype(BF16).reshape(-1) for b in big] + [_f32_as_bf16(s) for s in small]
    sizes = [p.shape[0] for p in parts]
    total = _round_up(sum(sizes), 2 * 16 * RED_COLS)
    flat = jnp.concatenate(parts + [jnp.zeros((total - sum(sizes),), BF16)])
    gathered = _all_gather_chips(flat.reshape(2, -1, RED_COLS), name="gather_weights").reshape(4, total)
    offs = [sum(sizes[:k]) for k in range(len(sizes))]
    pieces = [gathered[:, o:o + s] for o, s in zip(offs, sizes)]
    w_qkv = pieces[0].reshape(4, D, 3 * Dq)
    w_o = pieces[1].reshape(D, D)
    w_in = pieces[2].reshape(4, D, 2 * Dq)
    w_out = pieces[3].reshape(D, D)
    w_up = [pieces[4 + l].reshape(4, D, F) for l in range(depth)]
    w_down = [pieces[4 + depth + l].reshape(4 * F, D) for l in range(depth)]
    sm = [_bf16_as_f32(p) for p in pieces[4 + 2 * depth:]]
    meta_full = jnp.moveaxis(sm[0].reshape(4, n_meta, Dq), 0, 1).reshape(n_meta, D)
    conv_w = jnp.moveaxis(sm[1].reshape(4, 4, Dq), 0, 1).reshape(4, D)
    conv_b, b_rg, b_ig, lam = [s.reshape(1, D) for s in sm[2:6]]
    w_rg = lru_w_rg[0].astype(BF16)
    w_ig = lru_w_ig[0].astype(BF16)
    g_mix = [norm_mix[l].reshape(1, D) for l in range(depth)]
    g_mlp = [norm_mlp[l].reshape(1, D) for l in range(depth)]
    g_fin = norm_final.reshape(1, D)

    pad_rows = T - n_meta - seq
    h0 = jnp.concatenate([meta_full, x[0], jnp.zeros((pad_rows, D), F32)], axis=0)
    target = jnp.concatenate([jnp.zeros((n_meta, D), F32), loss_target[0], jnp.zeros((pad_rows, D), F32)], axis=0)

    hn0, qkv = _norm_mm(h0, g_mix[0], w_qkv, out_dtype=BF16, name="qkv_proj")
    att = _attn_fwd(qkv, name="attn_fwd")
    h1 = _mm_res(att, w_o, h0, name="attn_out")
    h2, hnm0, up0 = _mlp_fwd(h1, g_mlp[0], w_up[0], w_down[0], name="mlp0_fwd")
    hn1, gr = _norm_mm(h2, g_mix[1], w_in, out_dtype=F32, name="lru_in")
    a_t, b_t = _lru_pre(gr, conv_w, conv_b, w_rg, b_rg, w_ig, b_ig, lam, name="lru_pre")
    hs = _lru_scan(a_t, b_t, reverse=False, name="lru_scan")
    y = _lru_out(gr, hs, name="lru_gate_out")
    h3 = _mm_res(y, w_out, h2, name="lru_out")
    h4, hnm1, up1 = _mlp_fwd(h3, g_mlp[1], w_up[1], w_down[1], name="mlp1_fwd")
    loss, dh4, dh4b, dg_fin = _loss_head(h4, g_fin, target, row_lo=n_meta, row_hi=n_meta + seq, name="loss_head")

    dup1, dh3, dh3b, dg_mlp1 = _mlp_bwd(dh4, h3, g_mlp[1], up1, w_up[1], w_down[1], name="mlp1_bwd")
    dw_up1 = _mm_tn(hnm1, dup1, shards=4, relu2=False, name="mlp1_dwup")
    dw_down1 = _mm_tn(up1, dh4b, shards=1, relu2=True, name="mlp1_dwdown")
    dy = _mm_nt(dh3b, w_out, out_dtype=F32, name="lru_out_bwd")
    dw_out = _mm_tn(y, dh3b, shards=1, relu2=False, name="lru_dwout")
    dgate, dhy = _lru_out_bwd(gr, hs, dy, name="lru_gate_out_bwd")
    lmb = _lru_scan(a_t, dhy, reverse=True, name="lru_scan_bwd")
    du, dw_rg, db_rg, dw_ig, db_ig, dlam = _lru_gate_bwd(
        gr, hs, lmb, conv_w, conv_b, w_rg, b_rg, w_ig, b_ig, lam, name="lru_gate_bwd")
    drec, dconv_w, dconv_b = _lru_conv_bwd(gr, du, conv_w, name="lru_conv_bwd")
    dgr = jnp.concatenate([dgate, drec], axis=1)
    dh2, dh2b, dg_mix1 = _mm_nt_normbwd(dgr, w_in, h2, g_mix[1], dh3, name="lru_in_bwd")
    dw_in = _mm_tn(hn1, dgr, shards=4, relu2=False, name="lru_dwin")
    dup0, dh1, dh1b, dg_mlp0 = _mlp_bwd(dh2, h1, g_mlp[0], up0, w_up[0], w_down[0], name="mlp0_bwd")
    dw_up0 = _mm_tn(hnm0, dup0, shards=4, relu2=False, name="mlp0_dwup")
    dw_down0 = _mm_tn(up0, dh2b, shards=1, relu2=True, name="mlp0_dwdown")
    datt = _mm_nt(dh1b, w_o, out_dtype=BF16, name="attn_out_bwd")
    dw_o = _mm_tn(att, dh1b, shards=1, relu2=False, name="attn_dwo")
    dq, dk, dv = _attn_bwd(qkv, datt, name="attn_bwd")
    dqkv = jnp.concatenate([dq, dk, dv], axis=1)
    dh0, _, dg_mix0 = _mm_nt_normbwd(dqkv, w_qkv, h0, g_mix[0], dh1, name="qkv_bwd")
    dw_qkv = _mm_tn(hn0, dqkv, shards=4, relu2=False, name="attn_dwqkv")
    grad_x = dh0[n_meta:n_meta + seq][None]
    dmeta = dh0[:n_meta]

    dw_up = [dw_up0, dw_up1]
    dw_down = [dw_down0, dw_down1]
    sharded = [dw_qkv.reshape(4, -1), dw_o.reshape(4, -1), dw_in.reshape(4, -1), dw_out.reshape(4, -1)]
    sharded += [d.reshape(4, -1) for d in dw_up] + [d.reshape(4, -1) for d in dw_down]
    sharded += [_by_chip_cols(dmeta, Dq), _by_chip_cols(dconv_w, Dq), dconv_b.reshape(4, Dq),
                db_rg.reshape(4, Dq), db_ig.reshape(4, Dq), dlam.reshape(4, Dq)]
    repl = [jnp.concatenate([dg_mix0, dg_mix1], axis=0).reshape(-1),
            jnp.concatenate([dg_mlp0, dg_mlp1], axis=0).reshape(-1),
            dg_fin.reshape(-1), dw_rg.reshape(-1), dw_ig.reshape(-1)]
    rsizes = [r.shape[0] for r in repl]
    rtotal = _round_up(sum(rsizes), 4 * 2 * SUBLANE * LANE)
    rflat = jnp.concatenate(repl + [jnp.zeros((rtotal - sum(rsizes),), F32)]).reshape(4, rtotal // 4)
    gsizes = [s.shape[1] for s in sharded] + [rtotal // 4]
    gtotal = _round_up(sum(gsizes), 2 * RED_ROWS * RED_COLS)
    gflat = jnp.concatenate(sharded + [rflat, jnp.zeros((4, gtotal - sum(gsizes)), F32)], axis=1)
    half = gtotal // 2
    rows = half // RED_COLS
    gflat = gflat.reshape(4, 2, rows, RED_COLS)
    got = _swap_halves(gflat, name="reduce_cores")
    chip_sum = _add_halves(gflat, got, c_arr, name="reduce_cores_add")
    got = _scatter_chips(chip_sum, name="reduce_chips")
    mine = _add_chips(chip_sum, got, me_arr, name="reduce_chips_add")
    gshard = _join_halves(mine, name="reduce_join").reshape(gtotal)
    goffs = [sum(gsizes[:k]) for k in range(len(gsizes))]
    gp = [gshard[o:o + s] for o, s in zip(goffs, gsizes)]
    rchunk = gp[-1]
    rfull = _all_gather_chips(rchunk.reshape(2, -1, LANE), name="gather_replicated").reshape(rtotal)
    roffs = [sum(rsizes[:k]) for k in range(len(rsizes))]
    rp = [rfull[o:o + s] for o, s in zip(roffs, rsizes)]

    nbig = 4
    grads = dict(
        sb_w_qkv=gp[0], sb_w_o=gp[1], lru_w_in=gp[2], lru_w_out=gp[3],
        mlp_w_up=jnp.stack(gp[nbig:nbig + depth]), mlp_w_down=jnp.stack(gp[nbig + depth:nbig + 2 * depth]),
        meta_tokens=gp[nbig + 2 * depth], lru_conv_w=gp[nbig + 2 * depth + 1], lru_conv_b=gp[nbig + 2 * depth + 2],
        lru_b_rg=gp[nbig + 2 * depth + 3], lru_b_ig=gp[nbig + 2 * depth + 4], lru_lambda=gp[nbig + 2 * depth + 5],
        norm_mix=rp[0], norm_mlp=rp[1], norm_final=rp[2], lru_w_rg=rp[3], lru_w_ig=rp[4])
    grads = {n: grads[n].reshape(weights[n].shape) for n in names}

    large = ["sb_w_qkv", "sb_w_o", "lru_w_in", "lru_w_out", "mlp_w_up", "mlp_w_down"]
    delta, new_m, new_v = {}, {}, {}
    for n in large:
        shp = weights[n].shape
        view = (-1, shp[-1])
        d, nm, nv = _adamw(weights[n].reshape(view), grads[n].reshape(view), m_in[n].reshape(view),
                           v_in[n].reshape(view), name="adamw_" + n)
        delta[n], new_m[n], new_v[n] = d.reshape(shp), nm.reshape(shp), nv.reshape(shp)
    rest = [n for n in names if n not in large]
    ssz = [weights[n].size for n in rest]
    stotal = _round_up(sum(ssz), SUBLANE * RED_COLS)

    def pack(src):
        return jnp.concatenate([src[n].reshape(-1) for n in rest]
                               + [jnp.ones((stotal - sum(ssz),), F32)]).reshape(-1, RED_COLS)

    d, nm, nv = _adamw(pack(weights), pack(grads), pack(m_in), pack(v_in), name="adamw_small")
    soffs = [sum(ssz[:k]) for k in range(len(ssz))]
    for n, o, s in zip(rest, soffs, ssz):
        shp = weights[n].shape
        delta[n] = d.reshape(-1)[o:o + s].reshape(shp)
        new_m[n] = nm.reshape(-1)[o:o + s].reshape(shp)
        new_v[n] = nv.reshape(-1)[o:o + s].reshape(shp)

    loss = lax.psum(loss[0, 0], ("x", "y", "c"))
    return (loss, grad_x, *[grads[n] for n in names], *[delta[n] for n in names],
            *[new_m[n] for n in names], *[new_v[n] for n in names])
```

```python
import functools

import jax
import jax.numpy as jnp
from jax import lax
from jax.experimental import pallas as pl
from jax.experimental.pallas import tpu as pltpu

F32 = jnp.float32
BF16 = jnp.bfloat16
MESH = pl.DeviceIdType.MESH

EPS = 1e-6
HEAD_DIM = 64
LANE = 128
SUBLANE = 8
LRU_C = 8.0
VMEM_LIMIT = 56 * 1024 * 1024

ADAM_LR = 0.001
ADAM_B1 = 0.9
ADAM_B2 = 0.999
ADAM_EPS = 1e-08
ADAM_WD = 0.01
ADAM_STEP = 10


def _pcall(body, **kw):
    return pl.pallas_call(body, **kw)


def _params(*sem):
    return pltpu.CompilerParams(dimension_semantics=sem, vmem_limit_bytes=VMEM_LIMIT)


def _tile(n, pref, align):
    best = None
    for t in range(align, min(n, pref) + 1, align):
        if n % t == 0:
            best = t
    return n if best is None else best


def _sds(shape, dtype):
    return jax.ShapeDtypeStruct(shape, dtype)


def _rstd(x):
    return lax.rsqrt(jnp.mean(x * x, axis=-1, keepdims=True) + EPS)


def _norm_bwd(x, g, dy):
    rstd = _rstd(x)
    n = x * rstd
    dn = dy * g
    dx = rstd * (dn - n * jnp.mean(dn * n, axis=-1, keepdims=True))
    dg = jnp.sum(dy * n, axis=0, keepdims=True)
    return dx, dg


def _softplus_parts(z):
    l1p = jnp.log(1.0 + jnp.exp(-jnp.abs(z)))
    return jnp.maximum(z, 0.0) + l1p, jnp.minimum(z, 0.0) - l1p


def _sigmoid(x):
    return 1.0 / (1.0 + jnp.exp(-x))


def _gelu_parts(x):
    k = 0.7978845608028654
    inner = k * (x + 0.044715 * (x * x * x))
    t = jnp.tanh(inner)
    gelu = 0.5 * x * (1.0 + t)
    dgelu = 0.5 * (1.0 + t) + 0.5 * x * (1.0 - t * t) * (k * (1.0 + 3.0 * 0.044715 * (x * x)))
    return gelu, dgelu


def _norm_mm(h, g, w, *, out_dtype, name):
    T, D = h.shape
    S, _, n = w.shape
    tm = _tile(T, 1056, 16)
    tn = _tile(n, 768, LANE)
    nj = n // tn

    def body(h_ref, g_ref, w_ref, hn_ref, o_ref):
        @pl.when(pl.program_id(1) == 0)
        def _():
            x = h_ref[...]
            hn_ref[...] = (x * _rstd(x) * g_ref[...]).astype(BF16)

        o_ref[...] = jnp.dot(hn_ref[...], w_ref[...], preferred_element_type=F32).astype(out_dtype)

    return _pcall(
        body, name=name, grid=(T // tm, S * nj),
        in_specs=[pl.BlockSpec((tm, D), lambda i, j: (i, 0)),
                  pl.BlockSpec((1, D), lambda i, j: (0, 0)),
                  pl.BlockSpec((None, D, tn), lambda i, j: (j // nj, 0, j % nj))],
        out_specs=[pl.BlockSpec((tm, D), lambda i, j: (i, 0)),
                   pl.BlockSpec((tm, tn), lambda i, j: (i, j))],
        out_shape=[_sds((T, D), BF16), _sds((T, S * n), out_dtype)],
        compiler_params=_params("parallel", "arbitrary"),
    )(h, g, w)


def _mm_res(a, w, res, *, name):
    T, K = a.shape
    N = w.shape[1]
    tm = _tile(T, 1056, 16)

    def body(a_ref, w_ref, r_ref, o_ref):
        o_ref[...] = r_ref[...] + jnp.dot(a_ref[...], w_ref[...], preferred_element_type=F32)

    return _pcall(
        body, name=name, grid=(T // tm,),
        in_specs=[pl.BlockSpec((tm, K), lambda i: (i, 0)),
                  pl.BlockSpec((K, N), lambda i: (0, 0)),
                  pl.BlockSpec((tm, N), lambda i: (i, 0))],
        out_specs=pl.BlockSpec((tm, N), lambda i: (i, 0)),
        out_shape=_sds((T, N), F32),
        compiler_params=_params("parallel"),
    )(a, w, res)


def _mm_nt(a, w, *, out_dtype, name):
    T, N = a.shape
    K = w.shape[0]
    tm = _tile(T, 1056, 16)

    def body(a_ref, w_ref, o_ref):
        o_ref[...] = lax.dot_general(a_ref[...], w_ref[...], (((1,), (1,)), ((), ())),
                                     preferred_element_type=F32).astype(out_dtype)

    return _pcall(
        body, name=name, grid=(T // tm,),
        in_specs=[pl.BlockSpec((tm, N), lambda i: (i, 0)),
                  pl.BlockSpec((K, N), lambda i: (0, 0))],
        out_specs=pl.BlockSpec((tm, K), lambda i: (i, 0)),
        out_shape=_sds((T, K), out_dtype),
        compiler_params=_params("parallel"),
    )(a, w)


def _mm_tn(a, b, *, shards, relu2, name, slot=None, into=None, row_shards=0):
    T, Ka = a.shape
    Nb = b.shape[1]
    n = Nb // shards
    tka = _tile(Ka, 512, LANE)
    tnb = _tile(n, 512, LANE)
    nj = n // tnb

    def body(a_ref, b_ref, *rest):
        o_ref = rest[-1]
        av = a_ref[...]
        if relu2:
            r = jnp.maximum(av, 0)
            av = r * r
        o_ref[...] = lax.dot_general(av, b_ref[...], (((0,), (0,)), ((), ())),
                                     preferred_element_type=F32)

    in_specs = [pl.BlockSpec((T, tka), lambda i, j: (0, i)),
                pl.BlockSpec((T, tnb), lambda i, j: (0, j))]
    args = [a, b]
    aliases = {}
    if slot is None:
        out_spec = pl.BlockSpec((None, tka, tnb), lambda i, j: (j // nj, i, j % nj))
        out_shape = _sds((shards, Ka, n), F32)
    else:
        if row_shards:
            ni = Ka // row_shards // tka
            out_spec = pl.BlockSpec((None, None, tka, tnb), lambda i, j: (i // ni, slot, i % ni, j))
            out_shape = _sds((row_shards, 2, Ka // row_shards, n), F32)
        else:
            out_spec = pl.BlockSpec((None, None, tka, tnb), lambda i, j: (j // nj, slot, i, j % nj))
            out_shape = _sds((shards, 2, Ka, n), F32)
        if into is not None:
            in_specs.append(pl.BlockSpec(memory_space=pl.ANY))
            args.append(into)
            aliases = {2: 0}
    return _pcall(
        body, name=name, grid=(Ka // tka, shards * nj),
        in_specs=in_specs, out_specs=out_spec, out_shape=out_shape,
        input_output_aliases=aliases,
        compiler_params=_params("parallel", "parallel"),
    )(*args)


def _mm_nt_normbwd(dy, w, h, g, dres, *, name):
    T, D = h.shape
    S, _, n = w.shape
    tm = _tile(T, 528, 16)

    def body(dy_ref, w_ref, h_ref, g_ref, dr_ref, dh_ref, dhb_ref, dg_ref, acc_ref):
        i, s = pl.program_id(0), pl.program_id(1)
        part = lax.dot_general(dy_ref[...], w_ref[...], (((1,), (1,)), ((), ())),
                               preferred_element_type=F32)

        @pl.when(s == 0)
        def _():
            acc_ref[...] = part

        @pl.when(s > 0)
        def _():
            acc_ref[...] += part

        @pl.when(s == S - 1)
        def _():
            dx, dg = _norm_bwd(h_ref[...], g_ref[...], acc_ref[...])
            dh = dr_ref[...] + dx
            dh_ref[...] = dh
            dhb_ref[...] = dh.astype(BF16)

            @pl.when(i == 0)
            def _():
                dg_ref[...] = dg

            @pl.when(i > 0)
            def _():
                dg_ref[...] += dg

    return _pcall(
        body, name=name, grid=(T // tm, S),
        in_specs=[pl.BlockSpec((tm, n), lambda i, s: (i, s)),
                  pl.BlockSpec((None, D, n), lambda i, s: (s, 0, 0)),
                  pl.BlockSpec((tm, D), lambda i, s: (i, 0)),
                  pl.BlockSpec((1, D), lambda i, s: (0, 0)),
                  pl.BlockSpec((tm, D), lambda i, s: (i, 0))],
        out_specs=[pl.BlockSpec((tm, D), lambda i, s: (i, 0)),
                   pl.BlockSpec((tm, D), lambda i, s: (i, 0)),
                   pl.BlockSpec((1, D), lambda i, s: (0, 0))],
        out_shape=[_sds((T, D), F32), _sds((T, D), BF16), _sds((1, D), F32)],
        scratch_shapes=[pltpu.VMEM((tm, D), F32)],
        compiler_params=_params("arbitrary", "arbitrary"),
    )(dy, w, h, g, dres)


def _mlp_fwd(h, g, w_up, w_down, *, layer, name):
    T, D = h.shape
    S, _, _, n = w_up.shape
    tm = _tile(T, 528, 16)
    tf = _tile(n, 512, LANE)
    nj = n // tf
    nf = S * nj

    def body(h_ref, g_ref, wu_ref, wd_ref, o_ref, hn_ref, up_ref, acc_ref):
        f = pl.program_id(1)

        @pl.when(f == 0)
        def _():
            x = h_ref[...]
            hn_ref[...] = (x * _rstd(x) * g_ref[...]).astype(BF16)

        up = jnp.dot(hn_ref[...], wu_ref[...], preferred_element_type=F32)
        up_ref[...] = up.astype(BF16)
        r = jnp.maximum(up, 0.0)
        part = jnp.dot((r * r).astype(BF16), wd_ref[...], preferred_element_type=F32)

        @pl.when(f == 0)
        def _():
            acc_ref[...] = part

        @pl.when(f > 0)
        def _():
            acc_ref[...] += part

        @pl.when(f == nf - 1)
        def _():
            o_ref[...] = h_ref[...] + acc_ref[...]

    return _pcall(
        body, name=name, grid=(T // tm, nf),
        in_specs=[pl.BlockSpec((tm, D), lambda i, f: (i, 0)),
                  pl.BlockSpec((1, D), lambda i, f: (0, 0)),
                  pl.BlockSpec((None, None, D, tf), lambda i, f: (f // nj, layer, 0, f % nj)),
                  pl.BlockSpec((None, None, tf, D), lambda i, f: (f // nj, layer, f % nj, 0))],
        out_specs=[pl.BlockSpec((tm, D), lambda i, f: (i, 0)),
                   pl.BlockSpec((tm, D), lambda i, f: (i, 0)),
                   pl.BlockSpec((tm, tf), lambda i, f: (i, f))],
        out_shape=[_sds((T, D), F32), _sds((T, D), BF16), _sds((T, S * n), BF16)],
        scratch_shapes=[pltpu.VMEM((tm, D), F32)],
        compiler_params=_params("parallel", "arbitrary"),
    )(h, g, w_up, w_down)


def _mlp_bwd(dy, h, g, up, w_up, w_down, *, layer, name):
    T, D = h.shape
    S, _, _, n = w_up.shape
    tm = _tile(T, 528, 16)
    tf = _tile(n, 512, LANE)
    nj = n // tf
    nf = S * nj

    def body(dy_ref, h_ref, g_ref, up_ref, wu_ref, wd_ref, dup_ref, dh_ref, dhb_ref, dg_ref,
             dyb_ref, acc_ref):
        i, f = pl.program_id(0), pl.program_id(1)

        @pl.when(f == 0)
        def _():
            dyb_ref[...] = dy_ref[...].astype(BF16)

        dact = lax.dot_general(dyb_ref[...], wd_ref[...], (((1,), (1,)), ((), ())),
                               preferred_element_type=F32)
        r = jnp.maximum(up_ref[...].astype(F32), 0.0)
        dup = (dact * (2.0 * r)).astype(BF16)
        dup_ref[...] = dup
        part = lax.dot_general(dup, wu_ref[...], (((1,), (1,)), ((), ())),
                               preferred_element_type=F32)

        @pl.when(f == 0)
        def _():
            acc_ref[...] = part

        @pl.when(f > 0)
        def _():
            acc_ref[...] += part

        @pl.when(f == nf - 1)
        def _():
            dx, dg = _norm_bwd(h_ref[...], g_ref[...], acc_ref[...])
            dh = dy_ref[...] + dx
            dh_ref[...] = dh
            dhb_ref[...] = dh.astype(BF16)

            @pl.when(i == 0)
            def _():
                dg_ref[...] = dg

            @pl.when(i > 0)
            def _():
                dg_ref[...] += dg

    return _pcall(
        body, name=name, grid=(T // tm, nf),
        in_specs=[pl.BlockSpec((tm, D), lambda i, f: (i, 0)),
                  pl.BlockSpec((tm, D), lambda i, f: (i, 0)),
                  pl.BlockSpec((1, D), lambda i, f: (0, 0)),
                  pl.BlockSpec((tm, tf), lambda i, f: (i, f)),
                  pl.BlockSpec((None, None, D, tf), lambda i, f: (f // nj, layer, 0, f % nj)),
                  pl.BlockSpec((None, None, tf, D), lambda i, f: (f // nj, layer, f % nj, 0))],
        out_specs=[pl.BlockSpec((tm, tf), lambda i, f: (i, f)),
                   pl.BlockSpec((tm, D), lambda i, f: (i, 0)),
                   pl.BlockSpec((tm, D), lambda i, f: (i, 0)),
                   pl.BlockSpec((1, D), lambda i, f: (0, 0))],
        out_shape=[_sds((T, S * n), BF16), _sds((T, D), F32), _sds((T, D), BF16), _sds((1, D), F32)],
        scratch_shapes=[pltpu.VMEM((tm, D), BF16), pltpu.VMEM((tm, D), F32)],
        compiler_params=_params("arbitrary", "arbitrary"),
    )(dy, h, g, up, w_up, w_down)


ATT_BLOCK = 128


def _attn_tile(T):
    for w in (3 * ATT_BLOCK, 2 * ATT_BLOCK):
        if T % w == 0:
            return w
    return ATT_BLOCK


def _tri(strict_lower):
    B = ATT_BLOCK
    r = lax.broadcasted_iota(jnp.int32, (2 * B, B), 0)
    r = jnp.where(r >= B, r - B, r)
    c = lax.broadcasted_iota(jnp.int32, (2 * B, B), 1)
    m = (r > c) if strict_lower else (r < c)
    return jnp.where(m, 1.0, 0.0).astype(BF16)


def _split_dot(x, tri):
    hi = x.astype(BF16)
    lo = (x - hi.astype(F32)).astype(BF16)
    return jnp.dot(jnp.concatenate([hi, lo], axis=1), tri, preferred_element_type=F32)


def _causal_mask(W):
    r = lax.broadcasted_iota(jnp.int32, (W, W), 0)
    c = lax.broadcasted_iota(jnp.int32, (W, W), 1)
    return c < r


def _attn_scores(q, ks, scale, carry, tri, masked):
    W = q.shape[0]
    B = ATT_BLOCK
    z = lax.dot_general(q, ks, (((1,), (1,)), ((), ())), preferred_element_type=F32) * scale
    sp, logsig = _softplus_parts(z)
    lk = -sp
    if masked:
        causal = _causal_mask(W)
        lk = jnp.where(causal, lk, 0.0)
    afters = []
    for b in reversed(range(W // B)):
        blk = lk[:, b * B:(b + 1) * B]
        within = _split_dot(blk, tri)
        afters.append(within + carry)
        carry = carry + (within[:, 0:1] + blk[:, 0:1])
    after = jnp.concatenate(afters[::-1], axis=1)
    w = jnp.exp(logsig + after)
    if masked:
        w = jnp.where(causal, w, 0.0)
    return w, jnp.exp(logsig), carry


def _attn_fwd(qkv, *, name):
    T = qkv.shape[0]
    D = qkv.shape[1] // 3
    W = _attn_tile(T)
    npairs = D // LANE
    nq = T // W
    scale = HEAD_DIM ** -0.5

    def body(q_ref, k_ref, v_ref, o_ref):
        i = pl.program_id(1)
        tri = _tri(True)
        for hh in range(2):
            cols = slice(hh * HEAD_DIM, (hh + 1) * HEAD_DIM)
            q = q_ref[:, cols]

            def tile(j, carry, acc, masked):
                rows = pl.ds(pl.multiple_of(j * W, W), W)
                w, _, carry = _attn_scores(q, k_ref[rows, cols], scale, carry, tri, masked)
                acc = acc + jnp.dot(w.astype(BF16), v_ref[rows, cols], preferred_element_type=F32)
                return carry, acc

            carry, acc = tile(i, jnp.zeros((W, 1), F32), jnp.zeros((W, HEAD_DIM), F32), True)
            carry, acc = lax.fori_loop(
                0, i, lambda jj, ca: tile(i - 1 - jj, ca[0], ca[1], False), (carry, acc))
            o_ref[:, cols] = acc.astype(BF16)

    return _pcall(
        body, name=name, grid=(npairs, nq),
        in_specs=[pl.BlockSpec((W, LANE), lambda p, i: (i, p)),
                  pl.BlockSpec((T, LANE), lambda p, i: (0, npairs + p)),
                  pl.BlockSpec((T, LANE), lambda p, i: (0, 2 * npairs + p))],
        out_specs=pl.BlockSpec((W, LANE), lambda p, i: (i, p)),
        out_shape=_sds((T, D), BF16),
        compiler_params=_params("parallel", "arbitrary"),
    )(qkv, qkv, qkv)


def _attn_bwd(qkv, do, *, name):
    T = qkv.shape[0]
    D = qkv.shape[1] // 3
    B = ATT_BLOCK
    W = _attn_tile(T)
    npairs = D // LANE
    nq = T // W
    scale = HEAD_DIM ** -0.5

    def body(q_ref, k_ref, v_ref, do_ref, dq_ref, dk_ref, dv_ref, s_s, w_s, dk_acc, dv_acc):
        i = pl.program_id(1)

        @pl.when(i == 0)
        def _():
            dk_acc[...] = jnp.zeros_like(dk_acc)
            dv_acc[...] = jnp.zeros_like(dv_acc)

        tri_after = _tri(True)
        tri_before = _tri(False)
        for hh in range(2):
            cols = slice(hh * HEAD_DIM, (hh + 1) * HEAD_DIM)
            q = q_ref[:, cols]
            dout = do_ref[:, cols]

            def score(j, carry, masked):
                rows = pl.ds(pl.multiple_of(j * W, W), W)
                w, sig, carry = _attn_scores(q, k_ref[rows, cols], scale, carry, tri_after, masked)
                s_s[j] = sig
                w_s[j] = w
                return carry

            carry = score(i, jnp.zeros((W, 1), F32), True)
            lax.fori_loop(0, i, lambda jj, c: score(i - 1 - jj, c, False), carry)

            def grad(j, gsum, dq, masked):
                rows = pl.ds(pl.multiple_of(j * W, W), W)
                ks = k_ref[rows, cols]
                sig = s_s[j]
                w = w_s[j]
                dw = lax.dot_general(dout, v_ref[rows, cols], (((1,), (1,)), ((), ())),
                                     preferred_element_type=F32)
                g = dw * w
                befores = []
                for b in range(W // B):
                    blk = g[:, b * B:(b + 1) * B]
                    within = _split_dot(blk, tri_before)
                    befores.append(within + gsum)
                    gsum = gsum + (within[:, B - 1:B] + blk[:, B - 1:B])
                before = jnp.concatenate(befores, axis=1)
                dz = (g * (1.0 - sig) - sig * before) * scale
                if masked:
                    dz = jnp.where(_causal_mask(W), dz, 0.0)
                dzb = dz.astype(BF16)
                dq = dq + jnp.dot(dzb, ks, preferred_element_type=F32)
                dk_acc[rows, cols] += lax.dot_general(dzb, q, (((0,), (0,)), ((), ())),
                                                      preferred_element_type=F32)
                dv_acc[rows, cols] += lax.dot_general(w.astype(BF16), dout, (((0,), (0,)), ((), ())),
                                                      preferred_element_type=F32)
                return gsum, dq

            gsum, dq = lax.fori_loop(
                0, i, lambda j, c: grad(j, c[0], c[1], False),
                (jnp.zeros((W, 1), F32), jnp.zeros((W, HEAD_DIM), F32)))
            _, dq = grad(i, gsum, dq, True)
            dq_ref[:, cols] = dq.astype(BF16)

        @pl.when(i == nq - 1)
        def _():
            dk_ref[...] = dk_acc[...].astype(BF16)
            dv_ref[...] = dv_acc[...].astype(BF16)

    return _pcall(
        body, name=name, grid=(npairs, nq),
        in_specs=[pl.BlockSpec((W, LANE), lambda p, i: (i, p)),
                  pl.BlockSpec((T, LANE), lambda p, i: (0, npairs + p)),
                  pl.BlockSpec((T, LANE), lambda p, i: (0, 2 * npairs + p)),
                  pl.BlockSpec((W, LANE), lambda p, i: (i, p))],
        out_specs=[pl.BlockSpec((W, LANE), lambda p, i: (i, p)),
                   pl.BlockSpec((T, LANE), lambda p, i: (0, p)),
                   pl.BlockSpec((T, LANE), lambda p, i: (0, p))],
        out_shape=[_sds((T, D), BF16)] * 3,
        scratch_shapes=[pltpu.VMEM((nq, W, W), F32), pltpu.VMEM((nq, W, W), F32),
                        pltpu.VMEM((T, LANE), F32), pltpu.VMEM((T, LANE), F32)],
        compiler_params=_params("arbitrary", "arbitrary"),
    )(qkv, qkv, qkv, do)


HALO = SUBLANE


def _lru_gates(u, w_rg, b_rg, w_ig, b_ig, lam):
    nb = w_rg.shape[0]
    pre_r, pre_i = [], []
    for n in range(nb):
        ub = u[:, n * LANE:(n + 1) * LANE].astype(BF16)
        pre_r.append(jnp.dot(ub, w_rg[n], preferred_element_type=F32))
        pre_i.append(jnp.dot(ub, w_ig[n], preferred_element_type=F32))
    r = _sigmoid(jnp.concatenate(pre_r, axis=1) + b_rg)
    i = _sigmoid(jnp.concatenate(pre_i, axis=1) + b_ig)
    c = -LRU_C * _softplus_parts(-lam)[0]
    log_a = c * r
    a = jnp.exp(log_a)
    x2 = 2.0 * log_a
    em1 = jnp.where(jnp.abs(x2) < 1e-2, x2 * (1.0 + x2 * (0.5 + x2 * (1.0 / 6.0))), jnp.exp(x2) - 1.0)
    mult = jnp.sqrt(-em1)
    return r, i, a, mult, c


def _conv_rows(buf_ref, tt, conv_w, conv_b):
    u = conv_b
    for j in range(4):
        u = u + buf_ref[pl.ds(HALO - 3 + j, tt), :] * conv_w[j:j + 1, :]
    return u


def _fill_with_halo(buf_ref, prev_ref, cur_ref, first):
    tt = cur_ref.shape[0]
    buf_ref[pl.ds(0, HALO), :] = jnp.where(first, 0.0, prev_ref[...])
    buf_ref[pl.ds(HALO, tt), :] = cur_ref[...]


def _lru_time_tile(T):
    return _tile(T, 256, SUBLANE)


def _lru_pre(gr, conv_w, conv_b, w_rg, b_rg, w_ig, b_ig, lam, *, name):
    T = gr.shape[0]
    D = gr.shape[1] // 2
    tt = _lru_time_tile(T)
    hb = tt // HALO

    def body(x_ref, xp_ref, cw_ref, cb_ref, wr_ref, br_ref, wi_ref, bi_ref, lam_ref, a_ref, b_ref, buf):
        _fill_with_halo(buf, xp_ref, x_ref, pl.program_id(0) == 0)
        u = _conv_rows(buf, tt, cw_ref[...], cb_ref[...])
        _, i, a, mult, _ = _lru_gates(u, wr_ref, br_ref[...], wi_ref, bi_ref[...], lam_ref[...])
        a_ref[...] = a
        b_ref[...] = mult * (i * u)

    vec = pl.BlockSpec((1, D), lambda t: (0, 0))
    mat = pl.BlockSpec(w_rg.shape, lambda t: (0, 0, 0))
    return _pcall(
        body, name=name, grid=(T // tt,),
        in_specs=[pl.BlockSpec((tt, D), lambda t: (t, 1)),
                  pl.BlockSpec((HALO, D), lambda t: (jnp.maximum(t * hb - 1, 0), 1)),
                  pl.BlockSpec((4, D), lambda t: (0, 0)), vec, mat, vec, mat, vec, vec],
        out_specs=[pl.BlockSpec((tt, D), lambda t: (t, 0))] * 2,
        out_shape=[_sds((T, D), F32)] * 2,
        scratch_shapes=[pltpu.VMEM((tt + HALO, D), F32)],
        compiler_params=_params("parallel"),
    )(gr, gr, conv_w, conv_b, w_rg, b_rg, w_ig, b_ig, lam)


def _lru_scan(a, b, *, reverse, name):
    T, D = a.shape
    nb = D // LANE
    ts = _tile(T, 1056, SUBLANE)
    nt = T // ts
    a3 = a.reshape(T, nb, LANE)
    b3 = b.reshape(T, nb, LANE)

    def body(a_ref, b_ref, o_ref, carry):
        @pl.when(pl.program_id(0) == 0)
        def _():
            carry[...] = jnp.zeros_like(carry)

        if reverse:
            def step(k, c):
                t = ts - 1 - k
                l = b_ref[t] + c
                o_ref[t] = l
                return a_ref[t] * l
        else:
            def step(k, h):
                h = a_ref[k] * h + b_ref[k]
                o_ref[k] = h
                return h

        carry[...] = lax.fori_loop(0, ts, step, carry[...], unroll=8)

    if reverse:
        spec = pl.BlockSpec((ts, nb, LANE), lambda t: (nt - 1 - t, 0, 0))
    else:
        spec = pl.BlockSpec((ts, nb, LANE), lambda t: (t, 0, 0))
    out = _pcall(
        body, name=name, grid=(nt,),
        in_specs=[spec, spec], out_specs=spec,
        out_shape=_sds((T, nb, LANE), F32),
        scratch_shapes=[pltpu.VMEM((nb, LANE), F32)],
        compiler_params=_params("arbitrary"),
    )(a3, b3)
    return out.reshape(T, D)


def _lru_out(gr, hs, *, name):
    T, D = hs.shape
    tt = _tile(T, 1056, 16)

    def body(g_ref, h_ref, y_ref):
        y_ref[...] = (h_ref[...] * _gelu_parts(g_ref[...])[0]).astype(BF16)

    return _pcall(
        body, name=name, grid=(T // tt,),
        in_specs=[pl.BlockSpec((tt, D), lambda t: (t, 0)), pl.BlockSpec((tt, D), lambda t: (t, 0))],
        out_specs=pl.BlockSpec((tt, D), lambda t: (t, 0)),
        out_shape=_sds((T, D), BF16),
        compiler_params=_params("parallel"),
    )(gr, hs)


def _lru_out_bwd(gr, hs, dy, *, name):
    T, D = hs.shape
    tt = _tile(T, 1056, 16)

    def body(g_ref, h_ref, dy_ref, dg_ref, dh_ref):
        gelu, dgelu = _gelu_parts(g_ref[...])
        dy = dy_ref[...]
        dg_ref[...] = (dy * h_ref[...] * dgelu).astype(BF16)
        dh_ref[...] = dy * gelu

    spec = pl.BlockSpec((tt, D), lambda t: (t, 0))
    return _pcall(
        body, name=name, grid=(T // tt,),
        in_specs=[spec, spec, spec], out_specs=[spec, spec],
        out_shape=[_sds((T, D), BF16), _sds((T, D), F32)],
        compiler_params=_params("parallel"),
    )(gr, hs, dy)


def _lru_gate_bwd(gr, hs, lmb, conv_w, conv_b, w_rg, b_rg, w_ig, b_ig, lam, *, name):
    T, D = hs.shape
    nb = w_rg.shape[0]
    tt = _lru_time_tile(T)
    hb = tt // HALO
    nt = T // tt

    def body(x_ref, xp_ref, h_ref, hp_ref, l_ref, cw_ref, cb_ref, wr_ref, br_ref, wi_ref, bi_ref, lam_ref,
             du_ref, dwr_ref, dbr_ref, dwi_ref, dbi_ref, dlam_ref, xbuf, hbuf):
        t = pl.program_id(0)
        first = t == 0
        _fill_with_halo(xbuf, xp_ref, x_ref, first)
        _fill_with_halo(hbuf, hp_ref, h_ref, first)
        u = _conv_rows(xbuf, tt, cw_ref[...], cb_ref[...])
        lam_v = lam_ref[...]
        r, i, a, mult, c = _lru_gates(u, wr_ref, br_ref[...], wi_ref, bi_ref[...], lam_v)
        l = l_ref[...]
        h_prev = hbuf[pl.ds(HALO - 1, tt), :]
        dlog_a = l * h_prev * a - l * (i * u) * (a * a) / mult
        d_iu = l * mult
        dpre_r = (dlog_a * c) * (r * (1.0 - r))
        dpre_i = (d_iu * u) * (i * (1.0 - i))
        dpr_b = dpre_r.astype(BF16)
        dpi_b = dpre_i.astype(BF16)
        du_parts, dwr, dwi = [], [], []
        for n in range(nb):
            cs = slice(n * LANE, (n + 1) * LANE)
            ub = u[:, cs].astype(BF16)
            du_parts.append(
                lax.dot_general(dpr_b[:, cs], wr_ref[n], (((1,), (1,)), ((), ())), preferred_element_type=F32)
                + lax.dot_general(dpi_b[:, cs], wi_ref[n], (((1,), (1,)), ((), ())), preferred_element_type=F32))
            dwr.append(lax.dot_general(ub, dpr_b[:, cs], (((0,), (0,)), ((), ())), preferred_element_type=F32))
            dwi.append(lax.dot_general(ub, dpi_b[:, cs], (((0,), (0,)), ((), ())), preferred_element_type=F32))
        du_ref[...] = d_iu * i + jnp.concatenate(du_parts, axis=1)
        dbr = jnp.sum(dpre_r, axis=0, keepdims=True)
        dbi = jnp.sum(dpre_i, axis=0, keepdims=True)
        dc = jnp.sum(dlog_a * r, axis=0, keepdims=True)

        @pl.when(first)
        def _():
            for n in range(nb):
                dwr_ref[n] = dwr[n]
                dwi_ref[n] = dwi[n]
            dbr_ref[...] = dbr
            dbi_ref[...] = dbi
            dlam_ref[...] = dc

        @pl.when(t > 0)
        def _():
            for n in range(nb):
                dwr_ref[n] += dwr[n]
                dwi_ref[n] += dwi[n]
            dbr_ref[...] += dbr
            dbi_ref[...] += dbi
            dlam_ref[...] += dc

        @pl.when(t == nt - 1)
        def _():
            dlam_ref[...] = dlam_ref[...] * (LRU_C * _sigmoid(-lam_v))

    vec = pl.BlockSpec((1, D), lambda t: (0, 0))
    mat = pl.BlockSpec(w_rg.shape, lambda t: (0, 0, 0))
    blk = pl.BlockSpec((tt, D), lambda t: (t, 0))
    prev = pl.BlockSpec((HALO, D), lambda t: (jnp.maximum(t * hb - 1, 0), 0))
    return _pcall(
        body, name=name, grid=(nt,),
        in_specs=[pl.BlockSpec((tt, D), lambda t: (t, 1)),
                  pl.BlockSpec((HALO, D), lambda t: (jnp.maximum(t * hb - 1, 0), 1)),
                  blk, prev, blk,
                  pl.BlockSpec((4, D), lambda t: (0, 0)), vec, mat, vec, mat, vec, vec],
        out_specs=[blk, mat, vec, mat, vec, vec],
        out_shape=[_sds((T, D), F32), _sds(w_rg.shape, F32), _sds((1, D), F32),
                   _sds(w_rg.shape, F32), _sds((1, D), F32), _sds((1, D), F32)],
        scratch_shapes=[pltpu.VMEM((tt + HALO, D), F32), pltpu.VMEM((tt + HALO, D), F32)],
        compiler_params=_params("arbitrary"),
    )(gr, gr, hs, hs, lmb, conv_w, conv_b, w_rg, b_rg, w_ig, b_ig, lam)


def _lru_conv_bwd(gr, du, conv_w, *, name):
    T, D = du.shape
    tt = _lru_time_tile(T)
    hb = tt // HALO
    nt = T // tt

    def body(x_ref, xp_ref, du_ref, dun_ref, cw_ref, dx_ref, dcw_ref, dcb_ref, xbuf, dbuf):
        t = pl.program_id(0)
        _fill_with_halo(xbuf, xp_ref, x_ref, t == 0)
        du = du_ref[...]
        dbuf[pl.ds(0, tt), :] = du
        dbuf[pl.ds(tt, HALO), :] = jnp.where(t == nt - 1, 0.0, dun_ref[...])
        cw = cw_ref[...]
        dx = jnp.zeros((tt, D), F32)
        dcw = []
        for j in range(4):
            dx = dx + dbuf[pl.ds(3 - j, tt), :] * cw[j:j + 1, :]
            dcw.append(jnp.sum(du * xbuf[pl.ds(HALO - 3 + j, tt), :], axis=0, keepdims=True))
        dx_ref[...] = dx.astype(BF16)
        dcw = jnp.concatenate(dcw, axis=0)
        dcb = jnp.sum(du, axis=0, keepdims=True)

        @pl.when(t == 0)
        def _():
            dcw_ref[...] = dcw
            dcb_ref[...] = dcb

        @pl.when(t > 0)
        def _():
            dcw_ref[...] += dcw
            dcb_ref[...] += dcb

    blk = pl.BlockSpec((tt, D), lambda t: (t, 0))
    return _pcall(
        body, name=name, grid=(nt,),
        in_specs=[pl.BlockSpec((tt, D), lambda t: (t, 1)),
                  pl.BlockSpec((HALO, D), lambda t: (jnp.maximum(t * hb - 1, 0), 1)),
                  blk,
                  pl.BlockSpec((HALO, D), lambda t: (jnp.minimum((t + 1) * hb, T // HALO - 1), 0)),
                  pl.BlockSpec((4, D), lambda t: (0, 0))],
        out_specs=[blk, pl.BlockSpec((4, D), lambda t: (0, 0)), pl.BlockSpec((1, D), lambda t: (0, 0))],
        out_shape=[_sds((T, D), BF16), _sds((4, D), F32), _sds((1, D), F32)],
        scratch_shapes=[pltpu.VMEM((tt + HALO, D), F32), pltpu.VMEM((tt + HALO, D), F32)],
        compiler_params=_params("arbitrary"),
    )(gr, gr, du, du, conv_w)


def _loss_head(h, g, target, *, row_lo, row_hi, name):
    T, D = h.shape
    tm = _tile(T, 1056, 16)

    def body(h_ref, g_ref, t_ref, loss_ref, dh_ref, dhb_ref, dg_ref):
        i = pl.program_id(0)
        x = h_ref[...]
        g = g_ref[...]
        row = i * tm + lax.broadcasted_iota(jnp.int32, (tm, 1), 0)
        valid = jnp.logical_and(row >= row_lo, row < row_hi)
        rstd = _rstd(x)
        n = x * rstd
        err = jnp.where(valid, n * g - t_ref[...], 0.0)
        part = (0.5 / D) * jnp.sum(jnp.sum(err * err, axis=1, keepdims=True), axis=0, keepdims=True)
        dy = err * (1.0 / D)
        dn = dy * g
        dh = rstd * (dn - n * jnp.mean(dn * n, axis=-1, keepdims=True))
        dh_ref[...] = dh
        dhb_ref[...] = dh.astype(BF16)
        dg = jnp.sum(dy * n, axis=0, keepdims=True)

        @pl.when(i == 0)
        def _():
            loss_ref[...] = part
            dg_ref[...] = dg

        @pl.when(i > 0)
        def _():
            loss_ref[...] += part
            dg_ref[...] += dg

    blk = pl.BlockSpec((tm, D), lambda i: (i, 0))
    vec = pl.BlockSpec((1, D), lambda i: (0, 0))
    return _pcall(
        body, name=name, grid=(T // tm,),
        in_specs=[blk, vec, blk],
        out_specs=[pl.BlockSpec((1, 1), lambda i: (0, 0)), blk, blk, vec],
        out_shape=[_sds((1, 1), F32), _sds((T, D), F32), _sds((T, D), BF16), _sds((1, D), F32)],
        compiler_params=_params("arbitrary"),
    )(h, g, target)


def _adamw_math(w, g, m, v):
    c1 = 1.0 / (1.0 - ADAM_B1 ** ADAM_STEP)
    c2 = 1.0 / (1.0 - ADAM_B2 ** ADAM_STEP)
    m = ADAM_B1 * m + (1.0 - ADAM_B1) * g
    v = ADAM_B2 * v + (1.0 - ADAM_B2) * (g * g)
    delta = -ADAM_LR * ((m * c1) / (jnp.sqrt(v * c2) + ADAM_EPS) + ADAM_WD * w)
    return delta, m, v


def _adamw_halves(w, mine, theirs, m, v, c, *, name):
    _, R, C = w.shape
    tr = _tile(R, 256, SUBLANE)

    def body(c_ref, w_ref, a_ref, b_ref, m_ref, v_ref, g_ref, d_ref, nm_ref, nv_ref):
        g = jnp.where(pl.program_id(0) == c_ref[0], a_ref[...], b_ref[...])
        g_ref[...] = g
        d_ref[...], nm_ref[...], nv_ref[...] = _adamw_math(w_ref[...], g, m_ref[...], v_ref[...])

    full = pl.BlockSpec((None, tr, C), lambda h, i, c_ref: (h, i, 0))
    half = pl.BlockSpec((tr, C), lambda h, i, c_ref: (i, 0))
    return _pcall(
        body, name=name,
        grid_spec=pltpu.PrefetchScalarGridSpec(
            num_scalar_prefetch=1, grid=(2, R // tr),
            in_specs=[full, half, half, full, full], out_specs=[full] * 4),
        out_shape=[_sds((2, R, C), F32)] * 4,
        compiler_params=_params("parallel", "parallel"),
    )(c, w, mine, theirs, m, v)


def _adamw(w, g, m, v, *, name):
    R, C = w.shape
    tr = _tile(R, 512, SUBLANE)

    def body(w_ref, g_ref, m_ref, v_ref, d_ref, nm_ref, nv_ref):
        d_ref[...], nm_ref[...], nv_ref[...] = _adamw_math(w_ref[...], g_ref[...], m_ref[...], v_ref[...])

    blk = pl.BlockSpec((tr, C), lambda i: (i, 0))
    return _pcall(
        body, name=name, grid=(R // tr,),
        in_specs=[blk] * 4, out_specs=[blk] * 3,
        out_shape=[_sds((R, C), F32)] * 3,
        compiler_params=_params("parallel"),
    )(w, g, m, v)


ANY = pl.BlockSpec(memory_space=pl.ANY)


def _place():
    x, y, c = lax.axis_index("x"), lax.axis_index("y"), lax.axis_index("c")
    chips = [(1 - x, y), (x, 1 - y), (1 - x, 1 - y)]
    return x, y, c, chips


LOCAL_PIECES = 4


def _all_gather_chips(vs, *, name):
    n = len(vs)

    def body(*refs):
        v_refs, o_refs = refs[:n], refs[n:2 * n]
        send_sems, recv_sems, local_sems = refs[2 * n:]
        x, y, c, chips = _place()
        me = 2 * x + y

        def copy(a, k, block, half, to, src=None):
            dst = o_refs[a].at[block, half]
            return pltpu.make_async_remote_copy(
                src_ref=dst if src is None else src, dst_ref=dst,
                send_sem=send_sems.at[6 * a + k], recv_sem=recv_sems.at[6 * a + k],
                device_id=to, device_id_type=MESH)

        first = [copy(a, k, me, c, (cx, cy, c), src=v_refs[a].at[c])
                 for a in range(n) for k, (cx, cy) in enumerate(chips)]
        for cp in first:
            cp.start()
        local = []
        for a in range(n):
            rows = vs[a].shape[1] // (LOCAL_PIECES // 2)
            for p in range(LOCAL_PIECES):
                h, r0 = p % 2, (p // 2) * rows
                local.append(pltpu.make_async_copy(
                    v_refs[a].at[h, pl.ds(r0, rows)], o_refs[a].at[me, h, pl.ds(r0, rows)],
                    local_sems.at[LOCAL_PIECES * a + p]))
        for cp in local:
            cp.start()
        passed = []
        for a in range(n):
            for k, (cx, cy) in enumerate(chips):
                copy(a, k, 2 * cx + cy, c, (x, y, c)).wait_recv()
                passed.append(copy(a, 3 + k, 2 * cx + cy, c, (x, y, 1 - c)))
                passed[-1].start()
        for a in range(n):
            for k, (cx, cy) in enumerate(chips):
                copy(a, 3 + k, 2 * cx + cy, 1 - c, (x, y, c)).wait_recv()
        for cp in first + passed:
            cp.wait_send()
        for cp in local:
            cp.wait()

    return _pcall(
        body, name=name,
        in_specs=[ANY] * n, out_specs=[ANY] * n,
        out_shape=[_sds((4,) + v.shape, v.dtype) for v in vs],
        scratch_shapes=[pltpu.SemaphoreType.DMA((6 * n,)), pltpu.SemaphoreType.DMA((6 * n,)),
                        pltpu.SemaphoreType.DMA((LOCAL_PIECES * n,))],
    )(*vs)


def _swap_halves(gs, *, name):
    n = len(gs)

    def body(*refs):
        g_refs, o_refs = refs[:n], refs[n:2 * n]
        send_sems, recv_sems = refs[2 * n:]
        x, y, c, _ = _place()
        cps = [pltpu.make_async_remote_copy(
            src_ref=g_refs[a].at[:, 1 - c], dst_ref=o_refs[a], send_sem=send_sems.at[a],
            recv_sem=recv_sems.at[a], device_id=(x, y, 1 - c), device_id_type=MESH) for a in range(n)]
        for cp in cps:
            cp.start()
        for cp in cps:
            cp.wait()

    return _pcall(
        body, name=name, in_specs=[ANY] * n, out_specs=[ANY] * n,
        out_shape=[_sds((4,) + g.shape[2:], g.dtype) for g in gs],
        scratch_shapes=[pltpu.SemaphoreType.DMA((n,)), pltpu.SemaphoreType.DMA((n,))],
    )(*gs)


def _scatter_chips(as_, *, name):
    n = len(as_)

    def body(*refs):
        a_refs, o_refs = refs[:n], refs[n:2 * n]
        send_sems, recv_sems = refs[2 * n:]
        x, y, c, chips = _place()
        me = 2 * x + y
        sends = [pltpu.make_async_remote_copy(
            src_ref=a_refs[a].at[2 * cx + cy], dst_ref=o_refs[a].at[me],
            send_sem=send_sems.at[3 * a + k], recv_sem=recv_sems.at[3 * a + k],
            device_id=(cx, cy, c), device_id_type=MESH)
            for a in range(n) for k, (cx, cy) in enumerate(chips)]
        for cp in sends:
            cp.start()
        for a in range(n):
            for k, (cx, cy) in enumerate(chips):
                pltpu.make_async_remote_copy(
                    src_ref=a_refs[a].at[me], dst_ref=o_refs[a].at[2 * cx + cy],
                    send_sem=send_sems.at[3 * a + k], recv_sem=recv_sems.at[3 * a + k],
                    device_id=(x, y, c), device_id_type=MESH).wait_recv()
        for cp in sends:
            cp.wait_send()

    return _pcall(
        body, name=name, in_specs=[ANY] * n, out_specs=[ANY] * n,
        out_shape=[_sds(a.shape, a.dtype) for a in as_],
        scratch_shapes=[pltpu.SemaphoreType.DMA((3 * n,)), pltpu.SemaphoreType.DMA((3 * n,))],
    )(*as_)


def _send_sibling(rs, *, name):
    n = len(rs)

    def body(*refs):
        r_refs, o_refs = refs[:n], refs[n:2 * n]
        send_sems, recv_sems = refs[2 * n:]
        x, y, c, _ = _place()
        cps = [pltpu.make_async_remote_copy(
            src_ref=r_refs[a], dst_ref=o_refs[a], send_sem=send_sems.at[a], recv_sem=recv_sems.at[a],
            device_id=(x, y, 1 - c), device_id_type=MESH) for a in range(n)]
        for cp in cps:
            cp.start()
        for cp in cps:
            cp.wait()

    return _pcall(
        body, name=name, in_specs=[ANY] * n, out_specs=[ANY] * n,
        out_shape=[_sds(r.shape, r.dtype) for r in rs],
        scratch_shapes=[pltpu.SemaphoreType.DMA((n,)), pltpu.SemaphoreType.DMA((n,))],
    )(*rs)


def _add_halves(g, got, c, *, out_dtype, name):
    _, _, R, C = g.shape
    tr = _tile(R, 512, 16)

    def body(c_ref, g_ref, o_ref, out_ref):
        out_ref[...] = (g_ref[...] + o_ref[...]).astype(out_dtype)

    return _pcall(
        body, name=name,
        grid_spec=pltpu.PrefetchScalarGridSpec(
            num_scalar_prefetch=1, grid=(4, R // tr),
            in_specs=[pl.BlockSpec((None, None, tr, C), lambda s, i, c_ref: (s, c_ref[0], i, 0)),
                      pl.BlockSpec((None, tr, C), lambda s, i, c_ref: (s, i, 0))],
            out_specs=pl.BlockSpec((None, tr, C), lambda s, i, c_ref: (s, i, 0))),
        out_shape=_sds((4, R, C), out_dtype),
        compiler_params=_params("parallel", "parallel"),
    )(c, g, got)


def _add_chips(a, got, me, *, name):
    _, R, C = a.shape
    tr = _tile(R, 256, 16)

    def body(me_ref, a_ref, o_ref, out_ref):
        me_v = me_ref[0]
        own = a_ref[...].astype(F32)
        acc = jnp.where(me_v == 0, own, o_ref[0].astype(F32))
        for s in range(1, 4):
            acc = acc + jnp.where(me_v == s, own, o_ref[s].astype(F32))
        out_ref[...] = acc

    return _pcall(
        body, name=name,
        grid_spec=pltpu.PrefetchScalarGridSpec(
            num_scalar_prefetch=1, grid=(R // tr,),
            in_specs=[pl.BlockSpec((None, tr, C), lambda i, me_ref: (me_ref[0], i, 0)),
                      pl.BlockSpec((4, tr, C), lambda i, me_ref: (0, i, 0))],
            out_specs=pl.BlockSpec((tr, C), lambda i, me_ref: (i, 0))),
        out_shape=_sds((R, C), F32),
        compiler_params=_params("parallel"),
    )(me, a, got)


def _round_up(n, m):
    return (n + m - 1) // m * m


def _f32_as_bf16(a):
    return lax.bitcast_convert_type(a.astype(F32), BF16).reshape(-1)


def _bf16_as_f32(a):
    return lax.bitcast_convert_type(a.reshape(-1, 2), F32)


def _by_chip_cols(a, cols):
    lead = a.shape[:-1]
    a = a.reshape(lead + (4, cols))
    return jnp.moveaxis(a, -2, 0).reshape(4, -1)


def kernel(x, meta_tokens, norm_mix, norm_mlp, sb_w_qkv, sb_w_o, lru_w_in, lru_conv_w, lru_conv_b, lru_w_rg, lru_b_rg, lru_w_ig, lru_b_ig, lru_lambda, lru_w_out, mlp_w_up, mlp_w_down, norm_final, loss_target, m_meta_tokens, m_norm_mix, m_norm_mlp, m_sb_w_qkv, m_sb_w_o, m_lru_w_in, m_lru_conv_w, m_lru_conv_b, m_lru_w_rg, m_lru_b_rg, m_lru_w_ig, m_lru_b_ig, m_lru_lambda, m_lru_w_out, m_mlp_w_up, m_mlp_w_down, m_norm_final, v_meta_tokens, v_norm_mix, v_norm_mlp, v_sb_w_qkv, v_sb_w_o, v_lru_w_in, v_lru_conv_w, v_lru_conv_b, v_lru_w_rg, v_lru_b_rg, v_lru_w_ig, v_lru_b_ig, v_lru_lambda, v_lru_w_out, v_mlp_w_up, v_mlp_w_down, v_norm_final):
    weights = dict(meta_tokens=meta_tokens, norm_mix=norm_mix, norm_mlp=norm_mlp, sb_w_qkv=sb_w_qkv,
                   sb_w_o=sb_w_o, lru_w_in=lru_w_in, lru_conv_w=lru_conv_w, lru_conv_b=lru_conv_b,
                   lru_w_rg=lru_w_rg, lru_b_rg=lru_b_rg, lru_w_ig=lru_w_ig, lru_b_ig=lru_b_ig,
                   lru_lambda=lru_lambda, lru_w_out=lru_w_out, mlp_w_up=mlp_w_up, mlp_w_down=mlp_w_down,
                   norm_final=norm_final)
    m_in = dict(meta_tokens=m_meta_tokens, norm_mix=m_norm_mix, norm_mlp=m_norm_mlp, sb_w_qkv=m_sb_w_qkv,
                sb_w_o=m_sb_w_o, lru_w_in=m_lru_w_in, lru_conv_w=m_lru_conv_w, lru_conv_b=m_lru_conv_b,
                lru_w_rg=m_lru_w_rg, lru_b_rg=m_lru_b_rg, lru_w_ig=m_lru_w_ig, lru_b_ig=m_lru_b_ig,
                lru_lambda=m_lru_lambda, lru_w_out=m_lru_w_out, mlp_w_up=m_mlp_w_up,
                mlp_w_down=m_mlp_w_down, norm_final=m_norm_final)
    v_in = dict(meta_tokens=v_meta_tokens, norm_mix=v_norm_mix, norm_mlp=v_norm_mlp, sb_w_qkv=v_sb_w_qkv,
                sb_w_o=v_sb_w_o, lru_w_in=v_lru_w_in, lru_conv_w=v_lru_conv_w, lru_conv_b=v_lru_conv_b,
                lru_w_rg=v_lru_w_rg, lru_b_rg=v_lru_b_rg, lru_w_ig=v_lru_w_ig, lru_b_ig=v_lru_b_ig,
                lru_lambda=v_lru_lambda, lru_w_out=v_lru_w_out, mlp_w_up=v_mlp_w_up,
                mlp_w_down=v_mlp_w_down, norm_final=v_norm_final)
    names = list(weights)

    seq, D = x.shape[1], x.shape[2]
    n_meta = meta_tokens.shape[0]
    Dq = D // 4
    T = _round_up(n_meta + seq, ATT_BLOCK)
    nb = lru_w_rg.shape[1]
    F = mlp_w_up.shape[2]
    depth = mlp_w_up.shape[0]
    my_x, my_y, my_c = lax.axis_index("x"), lax.axis_index("y"), lax.axis_index("c")
    c_arr = jnp.reshape(my_c, (1,)).astype(jnp.int32)
    me_arr = jnp.reshape(2 * my_x + my_y, (1,)).astype(jnp.int32)

    assert depth == 2

    def halves(a):
        return a.astype(BF16).reshape(2, a.shape[0] // 2, a.shape[1])

    small = [meta_tokens, lru_conv_w[0], lru_conv_b, lru_b_rg, lru_b_ig, lru_lambda]
    sparts = [_f32_as_bf16(s) for s in small]
    sizes = [p.shape[0] for p in sparts]
    total = _round_up(sum(sizes), 2 * 32 * LANE)
    sflat =jnp.concatenate(sparts + [jnp.zeros((total - sum(sizes),), BF16)]).reshape(2, -1, LANE)
    gq, go, gi, gout, w_up, w_down, gsm = _all_gather_chips(
        [halves(sb_w_qkv[0]), halves(sb_w_o[0]), halves(lru_w_in[0]), halves(lru_w_out[0]),
         mlp_w_up.astype(BF16), mlp_w_down.astype(BF16), sflat], name="gather_weights")
    w_qkv = gq.reshape(4, D, 3 * Dq)
    w_o = go.reshape(D, D)
    w_in = gi.reshape(4, D, 2 * Dq)
    w_out = gout.reshape(D, D)
    gsm = gsm.reshape(4, total)
    offs = [sum(sizes[:k]) for k in range(len(sizes))]
    sm = [_bf16_as_f32(gsm[:, o:o + s]) for o, s in zip(offs, sizes)]
    meta_full = jnp.moveaxis(sm[0].reshape(4, n_meta, Dq), 0, 1).reshape(n_meta, D)
    conv_w = jnp.moveaxis(sm[1].reshape(4, 4, Dq), 0, 1).reshape(4, D)
    conv_b, b_rg, b_ig, lam = [s.reshape(1, D) for s in sm[2:6]]
    w_rg = lru_w_rg[0].astype(BF16)
    w_ig = lru_w_ig[0].astype(BF16)
    g_mix = [norm_mix[l].reshape(1, D) for l in range(depth)]
    g_mlp = [norm_mlp[l].reshape(1, D) for l in range(depth)]
    g_fin = norm_final.reshape(1, D)

    pad_rows = T - n_meta - seq
    h0 = jnp.concatenate([meta_full, x[0], jnp.zeros((pad_rows, D), F32)], axis=0)
    target = jnp.concatenate([jnp.zeros((n_meta, D), F32), loss_target[0], jnp.zeros((pad_rows, D), F32)], axis=0)

    hn0, qkv = _norm_mm(h0, g_mix[0], w_qkv, out_dtype=BF16, name="qkv_proj")
    att = _attn_fwd(qkv, name="attn_fwd")
    h1 = _mm_res(att, w_o, h0, name="attn_out")
    h2, hnm0, up0 = _mlp_fwd(h1, g_mlp[0], w_up, w_down, layer=0, name="mlp0_fwd")
    hn1, gr = _norm_mm(h2, g_mix[1], w_in, out_dtype=F32, name="lru_in")
    a_t, b_t = _lru_pre(gr, conv_w, conv_b, w_rg, b_rg, w_ig, b_ig, lam, name="lru_pre")
    hs = _lru_scan(a_t, b_t, reverse=False, name="lru_scan")
    y = _lru_out(gr, hs, name="lru_gate_out")
    h3 = _mm_res(y, w_out, h2, name="lru_out")
    h4, hnm1, up1 = _mlp_fwd(h3, g_mlp[1], w_up, w_down, layer=1, name="mlp1_fwd")
    loss, dh4, dh4b, dg_fin = _loss_head(h4, g_fin, target, row_lo=n_meta, row_hi=n_meta + seq, name="loss_head")

    dup1, dh3, dh3b, dg_mlp1 = _mlp_bwd(dh4, h3, g_mlp[1], up1, w_up, w_down, layer=1, name="mlp1_bwd")
    dw_up = _mm_tn(hnm1, dup1, shards=4, relu2=False, slot=1, name="mlp1_dwup")
    dw_down = _mm_tn(up1, dh4b, shards=1, relu2=True, slot=1, row_shards=4, name="mlp1_dwdown")
    dy = _mm_nt(dh3b, w_out, out_dtype=F32, name="lru_out_bwd")
    dw_out = _mm_tn(y, dh3b, shards=1, relu2=False, name="lru_dwout")
    dgate, dhy = _lru_out_bwd(gr, hs, dy, name="lru_gate_out_bwd")
    lmb = _lru_scan(a_t, dhy, reverse=True, name="lru_scan_bwd")
    du, dw_rg, db_rg, dw_ig, db_ig, dlam = _lru_gate_bwd(
        gr, hs, lmb, conv_w, conv_b, w_rg, b_rg, w_ig, b_ig, lam, name="lru_gate_bwd")
    drec, dconv_w, dconv_b = _lru_conv_bwd(gr, du, conv_w, name="lru_conv_bwd")
    dgr = jnp.concatenate([dgate, drec], axis=1)
    dh2, dh2b, dg_mix1 = _mm_nt_normbwd(dgr, w_in, h2, g_mix[1], dh3, name="lru_in_bwd")
    dw_in = _mm_tn(hn1, dgr, shards=4, relu2=False, name="lru_dwin")
    dup0, dh1, dh1b, dg_mlp0 = _mlp_bwd(dh2, h1, g_mlp[0], up0, w_up, w_down, layer=0, name="mlp0_bwd")
    dw_up = _mm_tn(hnm0, dup0, shards=4, relu2=False, slot=0, into=dw_up, name="mlp0_dwup")
    dw_down = _mm_tn(up0, dh2b, shards=1, relu2=True, slot=0, into=dw_down, row_shards=4, name="mlp0_dwdown")
    datt = _mm_nt(dh1b, w_o, out_dtype=BF16, name="attn_out_bwd")
    dw_o = _mm_tn(att, dh1b, shards=1, relu2=False, name="attn_dwo")
    dq, dk, dv = _attn_bwd(qkv, datt, name="attn_bwd")
    dqkv = jnp.concatenate([dq, dk, dv], axis=1)
    dh0, _, dg_mix0 = _mm_nt_normbwd(dqkv, w_qkv, h0, g_mix[0], dh1, name="qkv_bwd")
    dw_qkv = _mm_tn(hn0, dqkv, shards=4, relu2=False, name="attn_dwqkv")
    grad_x = dh0[n_meta:n_meta + seq][None]
    dmeta = dh0[:n_meta]

    def halves_of(d, rows):
        return d.reshape(4, 2, rows // 2, d.shape[-1])

    large = ["sb_w_qkv", "sb_w_o", "lru_w_in", "lru_w_out", "mlp_w_up", "mlp_w_down"]
    big = [halves_of(dw_qkv, D), halves_of(dw_o, Dq), halves_of(dw_in, D), halves_of(dw_out, Dq), dw_up, dw_down]
    sharded = [_by_chip_cols(dmeta, Dq), _by_chip_cols(dconv_w, Dq), dconv_b.reshape(4, Dq),
               db_rg.reshape(4, Dq), db_ig.reshape(4, Dq), dlam.reshape(4, Dq)]
    repl = [jnp.concatenate([dg_mix0, dg_mix1], axis=0).reshape(-1),
            jnp.concatenate([dg_mlp0, dg_mlp1], axis=0).reshape(-1),
            dg_fin.reshape(-1), dw_rg.reshape(-1), dw_ig.reshape(-1)]
    rsizes = [r.shape[0] for r in repl]
    rtotal = _round_up(sum(rsizes), 4 * 2 * 16 * LANE)
    rflat = jnp.concatenate(repl + [jnp.zeros((rtotal - sum(rsizes),), F32)]).reshape(4, rtotal // 4)
    gsizes = [s.shape[1] for s in sharded] + [rtotal // 4]
    gtotal = _round_up(sum(gsizes), 2 * 16 * LANE)
    tail = jnp.concatenate(sharded + [rflat, jnp.zeros((4, gtotal - sum(gsizes)), F32)], axis=1)
    parts = big + [tail.reshape(4, 2, -1, LANE)]
    tags = large + ["tail"]
    got = _swap_halves(parts, name="reduce_cores")
    chip_sums = [_add_halves(p, o, c_arr, out_dtype=F32 if t == "tail" else BF16, name="reduce_cores_add_" + t)
                 for p, o, t in zip(parts, got, tags)]
    got = _scatter_chips(chip_sums, name="reduce_chips")
    mine = [_add_chips(a, o, me_arr, name="reduce_chips_add_" + t) for a, o, t in zip(chip_sums, got, tags)]
    theirs = _send_sibling(mine, name="reduce_join")

    grads, delta, new_m, new_v = {}, {}, {}, {}
    for n, a, b in zip(large, mine, theirs):
        shp = weights[n].shape
        view = (2,) + a.shape
        g, d, nm, nv = _adamw_halves(weights[n].reshape(view), a, b, m_in[n].reshape(view),
                                     v_in[n].reshape(view), c_arr, name="adamw_" + n)
        grads[n], delta[n], new_m[n], new_v[n] = g.reshape(shp), d.reshape(shp), nm.reshape(shp), nv.reshape(shp)

    lo = jnp.where(my_c == 0, mine[-1], theirs[-1])
    hi = jnp.where(my_c == 0, theirs[-1], mine[-1])
    gshard = jnp.concatenate([lo, hi], axis=0).reshape(gtotal)
    goffs = [sum(gsizes[:k]) for k in range(len(gsizes))]
    gp = [gshard[o:o + s] for o, s in zip(goffs, gsizes)]
    rfull = _all_gather_chips([gp[-1].reshape(2, -1, LANE)], name="gather_replicated")[0].reshape(rtotal)
    roffs = [sum(rsizes[:k]) for k in range(len(rsizes))]
    rp = [rfull[o:o + s] for o, s in zip(roffs, rsizes)]
    grads.update(meta_tokens=gp[0], lru_conv_w=gp[1], lru_conv_b=gp[2], lru_b_rg=gp[3], lru_b_ig=gp[4],
                 lru_lambda=gp[5], norm_mix=rp[0], norm_mlp=rp[1], norm_final=rp[2], lru_w_rg=rp[3],
                 lru_w_ig=rp[4])
    grads = {n: grads[n].reshape(weights[n].shape) for n in names}
    rest = [n for n in names if n not in large]
    ssz = [weights[n].size for n in rest]
    small_cols = 8 * LANE
    stotal = _round_up(sum(ssz), SUBLANE * small_cols)

    def pack(src):
        return jnp.concatenate([src[n].reshape(-1) for n in rest]
                               + [jnp.ones((stotal - sum(ssz),), F32)]).reshape(-1, small_cols)

    d, nm, nv = _adamw(pack(weights), pack(grads), pack(m_in), pack(v_in), name="adamw_small")
    soffs = [sum(ssz[:k]) for k in range(len(ssz))]
    for n, o, s in zip(rest, soffs, ssz):
        shp = weights[n].shape
        delta[n] = d.reshape(-1)[o:o + s].reshape(shp)
        new_m[n] = nm.reshape(-1)[o:o + s].reshape(shp)
        new_v[n] = nv.reshape(-1)[o:o + s].reshape(shp)

    loss = lax.psum(loss[0, 0], ("x", "y", "c"))
    return (loss, grad_x, *[grads[n] for n in names], *[delta[n] for n in names],
            *[new_m[n] for n in names], *[new_v[n] for n in names])
```

```python
import functools

import jax
import jax.numpy as jnp
from jax import lax
from jax.experimental import pallas as pl
from jax.experimental.pallas import tpu as pltpu

F32 = jnp.float32
BF16 = jnp.bfloat16
MESH = pl.DeviceIdType.MESH

EPS = 1e-6
HEAD_DIM = 64
LANE = 128
SUBLANE = 8
LRU_C = 8.0
VMEM_LIMIT = 56 * 1024 * 1024

ADAM_LR = 0.001
ADAM_B1 = 0.9
ADAM_B2 = 0.999
ADAM_EPS = 1e-08
ADAM_WD = 0.01
ADAM_STEP = 10


def _pcall(body, **kw):
    return pl.pallas_call(body, **kw)


def _params(*sem):
    return pltpu.CompilerParams(dimension_semantics=sem, vmem_limit_bytes=VMEM_LIMIT)


def _tile(n, pref, align):
    best = None
    for t in range(align, min(n, pref) + 1, align):
        if n % t == 0:
            best = t
    return n if best is None else best


def _sds(shape, dtype):
    return jax.ShapeDtypeStruct(shape, dtype)


def _rstd(x):
    return lax.rsqrt(jnp.mean(x * x, axis=-1, keepdims=True) + EPS)


def _norm_bwd(x, g, dy):
    rstd = _rstd(x)
    n = x * rstd
    dn = dy * g
    dx = rstd * (dn - n * jnp.mean(dn * n, axis=-1, keepdims=True))
    dg = jnp.sum(dy * n, axis=0, keepdims=True)
    return dx, dg


def _softplus_parts(z):
    l1p = jnp.log(1.0 + jnp.exp(-jnp.abs(z)))
    return jnp.maximum(z, 0.0) + l1p, jnp.minimum(z, 0.0) - l1p


def _sigmoid(x):
    return 1.0 / (1.0 + jnp.exp(-x))


def _gelu_parts(x):
    k = 0.7978845608028654
    inner = k * (x + 0.044715 * (x * x * x))
    t = jnp.tanh(inner)
    gelu = 0.5 * x * (1.0 + t)
    dgelu = 0.5 * (1.0 + t) + 0.5 * x * (1.0 - t * t) * (k * (1.0 + 3.0 * 0.044715 * (x * x)))
    return gelu, dgelu


def _norm_mm(h, g, w, *, out_dtype, name):
    T, D = h.shape
    S, _, n = w.shape
    tm = _tile(T, 1056, 16)
    tn = _tile(n, 768, LANE)
    nj = n // tn

    def body(h_ref, g_ref, w_ref, hn_ref, o_ref):
        @pl.when(pl.program_id(1) == 0)
        def _():
            x = h_ref[...]
            hn_ref[...] = (x * _rstd(x) * g_ref[...]).astype(BF16)

        o_ref[...] = jnp.dot(hn_ref[...], w_ref[...], preferred_element_type=F32).astype(out_dtype)

    return _pcall(
        body, name=name, grid=(T // tm, S * nj),
        in_specs=[pl.BlockSpec((tm, D), lambda i, j: (i, 0)),
                  pl.BlockSpec((1, D), lambda i, j: (0, 0)),
                  pl.BlockSpec((None, D, tn), lambda i, j: (j // nj, 0, j % nj))],
        out_specs=[pl.BlockSpec((tm, D), lambda i, j: (i, 0)),
                   pl.BlockSpec((tm, tn), lambda i, j: (i, j))],
        out_shape=[_sds((T, D), BF16), _sds((T, S * n), out_dtype)],
        compiler_params=_params("parallel", "arbitrary"),
    )(h, g, w)


def _mm_res(a, w, res, *, name):
    T, K = a.shape
    N = w.shape[1]
    tm = _tile(T, 1056, 16)

    def body(a_ref, w_ref, r_ref, o_ref):
        o_ref[...] = r_ref[...] + jnp.dot(a_ref[...], w_ref[...], preferred_element_type=F32)

    return _pcall(
        body, name=name, grid=(T // tm,),
        in_specs=[pl.BlockSpec((tm, K), lambda i: (i, 0)),
                  pl.BlockSpec((K, N), lambda i: (0, 0)),
                  pl.BlockSpec((tm, N), lambda i: (i, 0))],
        out_specs=pl.BlockSpec((tm, N), lambda i: (i, 0)),
        out_shape=_sds((T, N), F32),
        compiler_params=_params("parallel"),
    )(a, w, res)


def _mm_nt(a, w, *, out_dtype, name):
    T, N = a.shape
    K = w.shape[0]
    tm = _tile(T, 1056, 16)

    def body(a_ref, w_ref, o_ref):
        o_ref[...] = lax.dot_general(a_ref[...], w_ref[...], (((1,), (1,)), ((), ())),
                                     preferred_element_type=F32).astype(out_dtype)

    return _pcall(
        body, name=name, grid=(T // tm,),
        in_specs=[pl.BlockSpec((tm, N), lambda i: (i, 0)),
                  pl.BlockSpec((K, N), lambda i: (0, 0))],
        out_specs=pl.BlockSpec((tm, K), lambda i: (i, 0)),
        out_shape=_sds((T, K), out_dtype),
        compiler_params=_params("parallel"),
    )(a, w)


def _mm_tn(a, b, *, shards, relu2, name, slot=None, into=None, row_shards=0, out_dtype=BF16):
    T, Ka = a.shape
    Nb = b.shape[1]
    n = Nb // shards
    tka = _tile(Ka, 512, LANE)
    tnb = _tile(n, 512, LANE)
    nj = n // tnb

    def body(a_ref, b_ref, *rest):
        o_ref = rest[-1]
        av = a_ref[...]
        if relu2:
            r = jnp.maximum(av, 0)
            av = r * r
        o_ref[...] = lax.dot_general(av, b_ref[...], (((0,), (0,)), ((), ())),
                                     preferred_element_type=F32).astype(out_dtype)

    in_specs = [pl.BlockSpec((T, tka), lambda i, j: (0, i)),
                pl.BlockSpec((T, tnb), lambda i, j: (0, j))]
    args = [a, b]
    aliases = {}
    if slot is None:
        out_spec = pl.BlockSpec((None, tka, tnb), lambda i, j: (j // nj, i, j % nj))
        out_shape = _sds((shards, Ka, n), out_dtype)
    else:
        if row_shards:
            ni = Ka // row_shards // tka
            out_spec = pl.BlockSpec((None, None, tka, tnb), lambda i, j: (i // ni, slot, i % ni, j))
            out_shape = _sds((row_shards, 2, Ka // row_shards, n), out_dtype)
        else:
            out_spec = pl.BlockSpec((None, None, tka, tnb), lambda i, j: (j // nj, slot, i, j % nj))
            out_shape = _sds((shards, 2, Ka, n), out_dtype)
        if into is not None:
            in_specs.append(pl.BlockSpec(memory_space=pl.ANY))
            args.append(into)
            aliases = {2: 0}
    return _pcall(
        body, name=name, grid=(Ka // tka, shards * nj),
        in_specs=in_specs, out_specs=out_spec, out_shape=out_shape,
        input_output_aliases=aliases,
        compiler_params=_params("parallel", "parallel"),
    )(*args)


def _mm_nt_normbwd(dy, w, h, g, dres, *, name):
    T, D = h.shape
    S, _, n = w.shape
    tm = _tile(T, 528, 16)

    def body(dy_ref, w_ref, h_ref, g_ref, dr_ref, dh_ref, dhb_ref, dg_ref, acc_ref):
        i, s = pl.program_id(0), pl.program_id(1)
        part = lax.dot_general(dy_ref[...], w_ref[...], (((1,), (1,)), ((), ())),
                               preferred_element_type=F32)

        @pl.when(s == 0)
        def _():
            acc_ref[...] = part

        @pl.when(s > 0)
        def _():
            acc_ref[...] += part

        @pl.when(s == S - 1)
        def _():
            dx, dg = _norm_bwd(h_ref[...], g_ref[...], acc_ref[...])
            dh = dr_ref[...] + dx
            dh_ref[...] = dh
            dhb_ref[...] = dh.astype(BF16)

            @pl.when(i == 0)
            def _():
                dg_ref[...] = dg

            @pl.when(i > 0)
            def _():
                dg_ref[...] += dg

    return _pcall(
        body, name=name, grid=(T // tm, S),
        in_specs=[pl.BlockSpec((tm, n), lambda i, s: (i, s)),
                  pl.BlockSpec((None, D, n), lambda i, s: (s, 0, 0)),
                  pl.BlockSpec((tm, D), lambda i, s: (i, 0)),
                  pl.BlockSpec((1, D), lambda i, s: (0, 0)),
                  pl.BlockSpec((tm, D), lambda i, s: (i, 0))],
        out_specs=[pl.BlockSpec((tm, D), lambda i, s: (i, 0)),
                   pl.BlockSpec((tm, D), lambda i, s: (i, 0)),
                   pl.BlockSpec((1, D), lambda i, s: (0, 0))],
        out_shape=[_sds((T, D), F32), _sds((T, D), BF16), _sds((1, D), F32)],
        scratch_shapes=[pltpu.VMEM((tm, D), F32)],
        compiler_params=_params("arbitrary", "arbitrary"),
    )(dy, w, h, g, dres)


def _mlp_fwd(h, g, w_up, w_down, *, layer, name):
    T, D = h.shape
    S, _, _, n = w_up.shape
    tm = _tile(T, 528, 16)
    tf = _tile(n, 512, LANE)
    nj = n // tf
    nf = S * nj

    def body(h_ref, g_ref, wu_ref, wd_ref, o_ref, hn_ref, up_ref, acc_ref):
        f = pl.program_id(1)

        @pl.when(f == 0)
        def _():
            x = h_ref[...]
            hn_ref[...] = (x * _rstd(x) * g_ref[...]).astype(BF16)

        up = jnp.dot(hn_ref[...], wu_ref[...], preferred_element_type=F32)
        up_ref[...] = up.astype(BF16)
        r = jnp.maximum(up, 0.0)
        part = jnp.dot((r * r).astype(BF16), wd_ref[...], preferred_element_type=F32)

        @pl.when(f == 0)
        def _():
            acc_ref[...] = part

        @pl.when(f > 0)
        def _():
            acc_ref[...] += part

        @pl.when(f == nf - 1)
        def _():
            o_ref[...] = h_ref[...] + acc_ref[...]

    return _pcall(
        body, name=name, grid=(T // tm, nf),
        in_specs=[pl.BlockSpec((tm, D), lambda i, f: (i, 0)),
                  pl.BlockSpec((1, D), lambda i, f: (0, 0)),
                  pl.BlockSpec((None, None, D, tf), lambda i, f: (f // nj, layer, 0, f % nj)),
                  pl.BlockSpec((None, None, tf, D), lambda i, f: (f // nj, layer, f % nj, 0))],
        out_specs=[pl.BlockSpec((tm, D), lambda i, f: (i, 0)),
                   pl.BlockSpec((tm, D), lambda i, f: (i, 0)),
                   pl.BlockSpec((tm, tf), lambda i, f: (i, f))],
        out_shape=[_sds((T, D), F32), _sds((T, D), BF16), _sds((T, S * n), BF16)],
        scratch_shapes=[pltpu.VMEM((tm, D), F32)],
        compiler_params=_params("parallel", "arbitrary"),
    )(h, g, w_up, w_down)


def _mlp_bwd(dy, h, g, up, w_up, w_down, *, layer, name):
    T, D = h.shape
    S, _, _, n = w_up.shape
    tm = _tile(T, 528, 16)
    tf = _tile(n, 512, LANE)
    nj = n // tf
    nf = S * nj

    def body(dy_ref, h_ref, g_ref, up_ref, wu_ref, wd_ref, dup_ref, dh_ref, dhb_ref, dg_ref,
             dyb_ref, acc_ref):
        i, f = pl.program_id(0), pl.program_id(1)

        @pl.when(f == 0)
        def _():
            dyb_ref[...] = dy_ref[...].astype(BF16)

        dact = lax.dot_general(dyb_ref[...], wd_ref[...], (((1,), (1,)), ((), ())),
                               preferred_element_type=F32)
        r = jnp.maximum(up_ref[...].astype(F32), 0.0)
        dup = (dact * (2.0 * r)).astype(BF16)
        dup_ref[...] = dup
        part = lax.dot_general(dup, wu_ref[...], (((1,), (1,)), ((), ())),
                               preferred_element_type=F32)

        @pl.when(f == 0)
        def _():
            acc_ref[...] = part

        @pl.when(f > 0)
        def _():
            acc_ref[...] += part

        @pl.when(f == nf - 1)
        def _():
            dx, dg = _norm_bwd(h_ref[...], g_ref[...], acc_ref[...])
            dh = dy_ref[...] + dx
            dh_ref[...] = dh
            dhb_ref[...] = dh.astype(BF16)

            @pl.when(i == 0)
            def _():
                dg_ref[...] = dg

            @pl.when(i > 0)
            def _():
                dg_ref[...] += dg

    return _pcall(
        body, name=name, grid=(T // tm, nf),
        in_specs=[pl.BlockSpec((tm, D), lambda i, f: (i, 0)),
                  pl.BlockSpec((tm, D), lambda i, f: (i, 0)),
                  pl.BlockSpec((1, D), lambda i, f: (0, 0)),
                  pl.BlockSpec((tm, tf), lambda i, f: (i, f)),
                  pl.BlockSpec((None, None, D, tf), lambda i, f: (f // nj, layer, 0, f % nj)),
                  pl.BlockSpec((None, None, tf, D), lambda i, f: (f // nj, layer, f % nj, 0))],
        out_specs=[pl.BlockSpec((tm, tf), lambda i, f: (i, f)),
                   pl.BlockSpec((tm, D), lambda i, f: (i, 0)),
                   pl.BlockSpec((tm, D), lambda i, f: (i, 0)),
                   pl.BlockSpec((1, D), lambda i, f: (0, 0))],
        out_shape=[_sds((T, S * n), BF16), _sds((T, D), F32), _sds((T, D), BF16), _sds((1, D), F32)],
        scratch_shapes=[pltpu.VMEM((tm, D), BF16), pltpu.VMEM((tm, D), F32)],
        compiler_params=_params("arbitrary", "arbitrary"),
    )(dy, h, g, up, w_up, w_down)


ATT_BLOCK = 128


def _attn_tile(T):
    for w in (3 * ATT_BLOCK, 2 * ATT_BLOCK):
        if T % w == 0:
            return w
    return ATT_BLOCK


def _tri(strict_lower, value):
    B = ATT_BLOCK
    r = lax.broadcasted_iota(jnp.int32, (2 * B, B), 0)
    r = jnp.where(r >= B, r - B, r)
    c = lax.broadcasted_iota(jnp.int32, (2 * B, B), 1)
    m = (r > c) if strict_lower else (r < c)
    return jnp.where(m, value, 0.0).astype(BF16)


def _split_dot(x, tri):
    hi = lax.bitcast_convert_type(lax.bitcast_convert_type(x, jnp.uint32) & jnp.uint32(0xFFFF0000), F32)
    lo = x - hi
    return jnp.dot(jnp.concatenate([hi.astype(BF16), lo.astype(BF16)], axis=1), tri,
                   preferred_element_type=F32)


def _causal_mask(W):
    r = lax.broadcasted_iota(jnp.int32, (W, W), 0)
    c = lax.broadcasted_iota(jnp.int32, (W, W), 1)
    return c < r


def _attn_scores(qs, ks, carry, tri_neg, masked, want_sig):
    W = qs.shape[0]
    B = ATT_BLOCK
    z = lax.dot_general(qs, ks, (((1,), (1,)), ((), ())), preferred_element_type=F32)
    minus_abs = lax.bitcast_convert_type(
        lax.bitcast_convert_type(z, jnp.uint32) | jnp.uint32(0x80000000), F32)
    sp = jnp.maximum(z, 0.0) + jnp.log(1.0 + jnp.exp(minus_abs))
    logsig = z - sp
    if masked:
        causal = _causal_mask(W)
        sp = jnp.where(causal, sp, 0.0)
    afters = []
    for b in reversed(range(W // B)):
        blk = sp[:, b * B:(b + 1) * B]
        within = _split_dot(blk, tri_neg)
        afters.append(within + carry)
        carry = carry + (within[:, 0:1] - blk[:, 0:1])
    after = jnp.concatenate(afters[::-1], axis=1)
    w = jnp.exp(logsig + after)
    if masked:
        w = jnp.where(causal, w, 0.0)
    return w, (jnp.exp(logsig) if want_sig else None), carry


def _ride_along(plan, npairs, nq):
    if plan is None:
        return 0, (lambda refs: None), (lambda refs: None)
    nx = plan.n

    def before(refs):
        p, i = pl.program_id(0), pl.program_id(1)

        @pl.when(jnp.logical_and(p == 0, i == 0))
        def _():
            plan.start(*refs)

        @pl.when(jnp.logical_and(p == npairs // 2, i == 0))
        def _():
            plan.mid(*refs)

    def after(refs):
        p, i = pl.program_id(0), pl.program_id(1)

        @pl.when(jnp.logical_and(p == npairs - 1, i == nq - 1))
        def _():
            plan.finish(*refs)

    return nx, before, after


def _attn_fwd(qkv, *, name, plan=None):
    T = qkv.shape[0]
    D = qkv.shape[1] // 3
    W = _attn_tile(T)
    npairs = D // LANE
    nq = T // W
    scale = HEAD_DIM ** -0.5
    nx, before, after = _ride_along(plan, npairs, nq)

    def body(*refs):
        q_ref, k_ref, v_ref = refs[:3]
        o_ref = refs[3 + nx]
        ride = (refs[3:3 + nx], refs[4 + nx:4 + 2 * nx], refs[4 + 2 * nx:])
        before(ride)
        i = pl.program_id(1)
        tri = _tri(True, -1.0)
        heads = [slice(hh * HEAD_DIM, (hh + 1) * HEAD_DIM) for hh in range(2)]
        qs = [q_ref[:, cols] * scale for cols in heads]

        def tile(j, state, masked):
            rows = pl.ds(pl.multiple_of(j * W, W), W)
            out = []
            for cols, q, (carry, acc) in zip(heads, qs, state):
                w, _, carry = _attn_scores(q, k_ref[rows, cols], carry, tri, masked, False)
                acc = acc + jnp.dot(w.astype(BF16), v_ref[rows, cols], preferred_element_type=F32)
                out.append((carry, acc))
            return tuple(out)

        zero = (jnp.zeros((W, 1), F32), jnp.zeros((W, HEAD_DIM), F32))
        state = tile(i, (zero, zero), True)
        state = lax.fori_loop(0, i, lambda jj, st: tile(i - 1 - jj, st, False), state)
        for cols, (_, acc) in zip(heads, state):
            o_ref[:, cols] = acc.astype(BF16)
        after(ride)

    extra = plan.inputs if plan else []
    return _pcall(
        body, name=name, grid=(npairs, nq),
        in_specs=[pl.BlockSpec((W, LANE), lambda p, i: (i, p)),
                  pl.BlockSpec((T, LANE), lambda p, i: (0, npairs + p)),
                  pl.BlockSpec((T, LANE), lambda p, i: (0, 2 * npairs + p))] + [ANY] * nx,
        out_specs=[pl.BlockSpec((W, LANE), lambda p, i: (i, p))] + [ANY] * nx,
        out_shape=[_sds((T, D), BF16)] + (plan.out_shape if plan else []),
        scratch_shapes=plan.scratch if plan else [],
        compiler_params=_params("arbitrary", "arbitrary"),
    )(qkv, qkv, qkv, *extra)


def _attn_bwd(qkv, do, *, name, plan=None):
    T = qkv.shape[0]
    D = qkv.shape[1] // 3
    B = ATT_BLOCK
    W = _attn_tile(T)
    npairs = D // LANE
    nq = T // W
    scale = HEAD_DIM ** -0.5
    nx, before, after = _ride_along(plan, npairs, nq)

    def body(*refs):
        q_ref, k_ref, v_ref, do_ref = refs[:4]
        dq_ref, dk_ref, dv_ref = refs[4 + nx:7 + nx]
        s_s, w_s, dk_acc, dv_acc = refs[7 + 2 * nx:11 + 2 * nx]
        ride = (refs[4:4 + nx], refs[7 + nx:7 + 2 * nx], refs[11 + 2 * nx:])
        before(ride)
        i = pl.program_id(1)

        @pl.when(i == 0)
        def _():
            dk_acc[...] = jnp.zeros_like(dk_acc)
            dv_acc[...] = jnp.zeros_like(dv_acc)

        tri_after = _tri(True, -1.0)
        tri_before = _tri(False, 1.0)
        heads = [slice(hh * HEAD_DIM, (hh + 1) * HEAD_DIM) for hh in range(2)]
        qs = [q_ref[:, cols] * scale for cols in heads]
        douts = [do_ref[:, cols] for cols in heads]

        def score(j, carries, masked):
            rows = pl.ds(pl.multiple_of(j * W, W), W)
            out = []
            for hh, (cols, carry) in enumerate(zip(heads, carries)):
                w, sig, carry = _attn_scores(qs[hh], k_ref[rows, cols], carry, tri_after, masked, True)
                s_s[hh, j] = sig
                w_s[hh, j] = w
                out.append(carry)
            return tuple(out)

        zero = jnp.zeros((W, 1), F32)
        carries = score(i, (zero, zero), True)
        lax.fori_loop(0, i, lambda jj, c: score(i - 1 - jj, c, False), carries)

        def grad(j, state, masked):
            rows = pl.ds(pl.multiple_of(j * W, W), W)
            out = []
            for hh, (cols, (gsum, dq)) in enumerate(zip(heads, state)):
                sig = s_s[hh, j]
                w = w_s[hh, j]
                dw = lax.dot_general(douts[hh], v_ref[rows, cols], (((1,), (1,)), ((), ())),
                                     preferred_element_type=F32)
                g = dw * w
                befores = []
                for b in range(W // B):
                    blk = g[:, b * B:(b + 1) * B]
                    within = _split_dot(blk, tri_before)
                    befores.append(within + gsum)
                    gsum = gsum + (within[:, B - 1:B] + blk[:, B - 1:B])
                dz = g - sig * (g + jnp.concatenate(befores, axis=1))
                if masked:
                    dz = jnp.where(_causal_mask(W), dz, 0.0)
                dzb = dz.astype(BF16)
                dq = dq + jnp.dot(dzb, k_ref[rows, cols] * scale, preferred_element_type=F32)
                dk_acc[rows, cols] += lax.dot_general(dzb, qs[hh], (((0,), (0,)), ((), ())),
                                                      preferred_element_type=F32)
                dv_acc[rows, cols] += lax.dot_general(w.astype(BF16), douts[hh], (((0,), (0,)), ((), ())),
                                                      preferred_element_type=F32)
                out.append((gsum, dq))
            return tuple(out)

        zero2 = (zero, jnp.zeros((W, HEAD_DIM), F32))
        state = lax.fori_loop(0, i, lambda j, st: grad(j, st, False), (zero2, zero2))
        state = grad(i, state, True)
        for cols, (_, dq) in zip(heads, state):
            dq_ref[:, cols] = dq.astype(BF16)

        @pl.when(i == nq - 1)
        def _():
            dk_ref[...] = dk_acc[...].astype(BF16)
            dv_ref[...] = dv_acc[...].astype(BF16)

        after(ride)

    extra = plan.inputs if plan else []
    return _pcall(
        body, name=name, grid=(npairs, nq),
        in_specs=[pl.BlockSpec((W, LANE), lambda p, i: (i, p)),
                  pl.BlockSpec((T, LANE), lambda p, i: (0, npairs + p)),
                  pl.BlockSpec((T, LANE), lambda p, i: (0, 2 * npairs + p)),
                  pl.BlockSpec((W, LANE), lambda p, i: (i, p))] + [ANY] * nx,
        out_specs=[pl.BlockSpec((W, LANE), lambda p, i: (i, p)),
                   pl.BlockSpec((T, LANE), lambda p, i: (0, p)),
                   pl.BlockSpec((T, LANE), lambda p, i: (0, p))] + [ANY] * nx,
        out_shape=[_sds((T, D), BF16)] * 3 + (plan.out_shape if plan else []),
        scratch_shapes=[pltpu.VMEM((2, nq, W, W), F32), pltpu.VMEM((2, nq, W, W), F32),
                        pltpu.VMEM((T, LANE), F32), pltpu.VMEM((T, LANE), F32)]
        + (plan.scratch if plan else []),
        compiler_params=_params("arbitrary", "arbitrary"),
    )(qkv, qkv, qkv, do, *extra)


HALO = SUBLANE


def _lru_gates(u, w_rg, b_rg, w_ig, b_ig, lam):
    nb = w_rg.shape[0]
    pre_r, pre_i = [], []
    for n in range(nb):
        ub = u[:, n * LANE:(n + 1) * LANE].astype(BF16)
        pre_r.append(jnp.dot(ub, w_rg[n], preferred_element_type=F32))
        pre_i.append(jnp.dot(ub, w_ig[n], preferred_element_type=F32))
    r = _sigmoid(jnp.concatenate(pre_r, axis=1) + b_rg)
    i = _sigmoid(jnp.concatenate(pre_i, axis=1) + b_ig)
    c = -LRU_C * _softplus_parts(-lam)[0]
    log_a = c * r
    a = jnp.exp(log_a)
    x2 = 2.0 * log_a
    em1 = jnp.where(jnp.abs(x2) < 1e-2, x2 * (1.0 + x2 * (0.5 + x2 * (1.0 / 6.0))), jnp.exp(x2) - 1.0)
    mult = jnp.sqrt(-em1)
    return r, i, a, mult, c


def _conv_rows(buf_ref, tt, conv_w, conv_b):
    u = conv_b
    for j in range(4):
        u = u + buf_ref[pl.ds(HALO - 3 + j, tt), :] * conv_w[j:j + 1, :]
    return u


def _fill_with_halo(buf_ref, prev_ref, cur_ref, first):
    tt = cur_ref.shape[0]
    buf_ref[pl.ds(0, HALO), :] = jnp.where(first, 0.0, prev_ref[...])
    buf_ref[pl.ds(HALO, tt), :] = cur_ref[...]


def _lru_time_tile(T):
    return _tile(T, 256, SUBLANE)


def _lru_pre(gr, conv_w, conv_b, w_rg, b_rg, w_ig, b_ig, lam, *, name):
    T = gr.shape[0]
    D = gr.shape[1] // 2
    tt = _lru_time_tile(T)
    hb = tt // HALO

    def body(x_ref, xp_ref, cw_ref, cb_ref, wr_ref, br_ref, wi_ref, bi_ref, lam_ref, a_ref, b_ref, buf):
        _fill_with_halo(buf, xp_ref, x_ref, pl.program_id(0) == 0)
        u = _conv_rows(buf, tt, cw_ref[...], cb_ref[...])
        _, i, a, mult, _ = _lru_gates(u, wr_ref, br_ref[...], wi_ref, bi_ref[...], lam_ref[...])
        a_ref[...] = a
        b_ref[...] = mult * (i * u)

    vec = pl.BlockSpec((1, D), lambda t: (0, 0))
    mat = pl.BlockSpec(w_rg.shape, lambda t: (0, 0, 0))
    return _pcall(
        body, name=name, grid=(T // tt,),
        in_specs=[pl.BlockSpec((tt, D), lambda t: (t, 1)),
                  pl.BlockSpec((HALO, D), lambda t: (jnp.maximum(t * hb - 1, 0), 1)),
                  pl.BlockSpec((4, D), lambda t: (0, 0)), vec, mat, vec, mat, vec, vec],
        out_specs=[pl.BlockSpec((tt, D), lambda t: (t, 0))] * 2,
        out_shape=[_sds((T, D), F32)] * 2,
        scratch_shapes=[pltpu.VMEM((tt + HALO, D), F32)],
        compiler_params=_params("parallel"),
    )(gr, gr, conv_w, conv_b, w_rg, b_rg, w_ig, b_ig, lam)


def _lru_scan(a, b, *, reverse, name):
    T, D = a.shape
    nb = D // LANE
    ts = _tile(T, 1056, SUBLANE)
    nt = T // ts
    a3 = a.reshape(T, nb, LANE)
    b3 = b.reshape(T, nb, LANE)

    def body(a_ref, b_ref, o_ref, carry):
        @pl.when(pl.program_id(0) == 0)
        def _():
            carry[...] = jnp.zeros_like(carry)

        if reverse:
            def step(k, c):
                t = ts - 1 - k
                l = b_ref[t] + c
                o_ref[t] = l
                return a_ref[t] * l
        else:
            def step(k, h):
                h = a_ref[k] * h + b_ref[k]
                o_ref[k] = h
                return h

        carry[...] = lax.fori_loop(0, ts, step, carry[...], unroll=8)

    if reverse:
        spec = pl.BlockSpec((ts, nb, LANE), lambda t: (nt - 1 - t, 0, 0))
    else:
        spec = pl.BlockSpec((ts, nb, LANE), lambda t: (t, 0, 0))
    out = _pcall(
        body, name=name, grid=(nt,),
        in_specs=[spec, spec], out_specs=spec,
        out_shape=_sds((T, nb, LANE), F32),
        scratch_shapes=[pltpu.VMEM((nb, LANE), F32)],
        compiler_params=_params("arbitrary"),
    )(a3, b3)
    return out.reshape(T, D)


def _lru_out(gr, hs, *, name):
    T, D = hs.shape
    tt = _tile(T, 1056, 16)

    def body(g_ref, h_ref, y_ref):
        y_ref[...] = (h_ref[...] * _gelu_parts(g_ref[...])[0]).astype(BF16)

    return _pcall(
        body, name=name, grid=(T // tt,),
        in_specs=[pl.BlockSpec((tt, D), lambda t: (t, 0)), pl.BlockSpec((tt, D), lambda t: (t, 0))],
        out_specs=pl.BlockSpec((tt, D), lambda t: (t, 0)),
        out_shape=_sds((T, D), BF16),
        compiler_params=_params("parallel"),
    )(gr, hs)


def _lru_out_bwd(gr, hs, dy, *, name):
    T, D = hs.shape
    tt = _tile(T, 1056, 16)

    def body(g_ref, h_ref, dy_ref, dg_ref, dh_ref):
        gelu, dgelu = _gelu_parts(g_ref[...])
        dy = dy_ref[...]
        dg_ref[...] = (dy * h_ref[...] * dgelu).astype(BF16)
        dh_ref[...] = dy * gelu

    spec = pl.BlockSpec((tt, D), lambda t: (t, 0))
    return _pcall(
        body, name=name, grid=(T // tt,),
        in_specs=[spec, spec, spec], out_specs=[spec, spec],
        out_shape=[_sds((T, D), BF16), _sds((T, D), F32)],
        compiler_params=_params("parallel"),
    )(gr, hs, dy)


def _lru_gate_bwd(gr, hs, lmb, conv_w, conv_b, w_rg, b_rg, w_ig, b_ig, lam, *, name):
    T, D = hs.shape
    nb = w_rg.shape[0]
    tt = _lru_time_tile(T)
    hb = tt // HALO
    nt = T // tt

    def body(x_ref, xp_ref, h_ref, hp_ref, l_ref, cw_ref, cb_ref, wr_ref, br_ref, wi_ref, bi_ref, lam_ref,
             du_ref, dwr_ref, dbr_ref, dwi_ref, dbi_ref, dlam_ref, xbuf, hbuf):
        t = pl.program_id(0)
        first = t == 0
        _fill_with_halo(xbuf, xp_ref, x_ref, first)
        _fill_with_halo(hbuf, hp_ref, h_ref, first)
        u = _conv_rows(xbuf, tt, cw_ref[...], cb_ref[...])
        lam_v = lam_ref[...]
        r, i, a, mult, c = _lru_gates(u, wr_ref, br_ref[...], wi_ref, bi_ref[...], lam_v)
        l = l_ref[...]
        h_prev = hbuf[pl.ds(HALO - 1, tt), :]
        dlog_a = l * h_prev * a - l * (i * u) * (a * a) / mult
        d_iu = l * mult
        dpre_r = (dlog_a * c) * (r * (1.0 - r))
        dpre_i = (d_iu * u) * (i * (1.0 - i))
        dpr_b = dpre_r.astype(BF16)
        dpi_b = dpre_i.astype(BF16)
        du_parts, dwr, dwi = [], [], []
        for n in range(nb):
            cs = slice(n * LANE, (n + 1) * LANE)
            ub = u[:, cs].astype(BF16)
            du_parts.append(
                lax.dot_general(dpr_b[:, cs], wr_ref[n], (((1,), (1,)), ((), ())), preferred_element_type=F32)
                + lax.dot_general(dpi_b[:, cs], wi_ref[n], (((1,), (1,)), ((), ())), preferred_element_type=F32))
            dwr.append(lax.dot_general(ub, dpr_b[:, cs], (((0,), (0,)), ((), ())), preferred_element_type=F32))
            dwi.append(lax.dot_general(ub, dpi_b[:, cs], (((0,), (0,)), ((), ())), preferred_element_type=F32))
        du_ref[...] = d_iu * i + jnp.concatenate(du_parts, axis=1)
        dbr = jnp.sum(dpre_r, axis=0, keepdims=True)
        dbi = jnp.sum(dpre_i, axis=0, keepdims=True)
        dc = jnp.sum(dlog_a * r, axis=0, keepdims=True)

        @pl.when(first)
        def _():
            for n in range(nb):
                dwr_ref[n] = dwr[n]
                dwi_ref[n] = dwi[n]
            dbr_ref[...] = dbr
            dbi_ref[...] = dbi
            dlam_ref[...] = dc

        @pl.when(t > 0)
        def _():
            for n in range(nb):
                dwr_ref[n] += dwr[n]
                dwi_ref[n] += dwi[n]
            dbr_ref[...] += dbr
            dbi_ref[...] += dbi
            dlam_ref[...] += dc

        @pl.when(t == nt - 1)
        def _():
            dlam_ref[...] = dlam_ref[...] * (LRU_C * _sigmoid(-lam_v))

    vec = pl.BlockSpec((1, D), lambda t: (0, 0))
    mat = pl.BlockSpec(w_rg.shape, lambda t: (0, 0, 0))
    blk = pl.BlockSpec((tt, D), lambda t: (t, 0))
    prev = pl.BlockSpec((HALO, D), lambda t: (jnp.maximum(t * hb - 1, 0), 0))
    return _pcall(
        body, name=name, grid=(nt,),
        in_specs=[pl.BlockSpec((tt, D), lambda t: (t, 1)),
                  pl.BlockSpec((HALO, D), lambda t: (jnp.maximum(t * hb - 1, 0), 1)),
                  blk, prev, blk,
                  pl.BlockSpec((4, D), lambda t: (0, 0)), vec, mat, vec, mat, vec, vec],
        out_specs=[blk, mat, vec, mat, vec, vec],
        out_shape=[_sds((T, D), F32), _sds(w_rg.shape, F32), _sds((1, D), F32),
                   _sds(w_rg.shape, F32), _sds((1, D), F32), _sds((1, D), F32)],
        scratch_shapes=[pltpu.VMEM((tt + HALO, D), F32), pltpu.VMEM((tt + HALO, D), F32)],
        compiler_params=_params("arbitrary"),
    )(gr, gr, hs, hs, lmb, conv_w, conv_b, w_rg, b_rg, w_ig, b_ig, lam)


def _lru_conv_bwd(gr, du, conv_w, *, name):
    T, D = du.shape
    tt = _lru_time_tile(T)
    hb = tt // HALO
    nt = T // tt

    def body(x_ref, xp_ref, du_ref, dun_ref, cw_ref, dx_ref, dcw_ref, dcb_ref, xbuf, dbuf):
        t = pl.program_id(0)
        _fill_with_halo(xbuf, xp_ref, x_ref, t == 0)
        du = du_ref[...]
        dbuf[pl.ds(0, tt), :] = du
        dbuf[pl.ds(tt, HALO), :] = jnp.where(t == nt - 1, 0.0, dun_ref[...])
        cw = cw_ref[...]
        dx = jnp.zeros((tt, D), F32)
        dcw = []
        for j in range(4):
            dx = dx + dbuf[pl.ds(3 - j, tt), :] * cw[j:j + 1, :]
            dcw.append(jnp.sum(du * xbuf[pl.ds(HALO - 3 + j, tt), :], axis=0, keepdims=True))
        dx_ref[...] = dx.astype(BF16)
        dcw = jnp.concatenate(dcw, axis=0)
        dcb = jnp.sum(du, axis=0, keepdims=True)

        @pl.when(t == 0)
        def _():
            dcw_ref[...] = dcw
            dcb_ref[...] = dcb

        @pl.when(t > 0)
        def _():
            dcw_ref[...] += dcw
            dcb_ref[...] += dcb

    blk = pl.BlockSpec((tt, D), lambda t: (t, 0))
    return _pcall(
        body, name=name, grid=(nt,),
        in_specs=[pl.BlockSpec((tt, D), lambda t: (t, 1)),
                  pl.BlockSpec((HALO, D), lambda t: (jnp.maximum(t * hb - 1, 0), 1)),
                  blk,
                  pl.BlockSpec((HALO, D), lambda t: (jnp.minimum((t + 1) * hb, T // HALO - 1), 0)),
                  pl.BlockSpec((4, D), lambda t: (0, 0))],
        out_specs=[blk, pl.BlockSpec((4, D), lambda t: (0, 0)), pl.BlockSpec((1, D), lambda t: (0, 0))],
        out_shape=[_sds((T, D), BF16), _sds((4, D), F32), _sds((1, D), F32)],
        scratch_shapes=[pltpu.VMEM((tt + HALO, D), F32), pltpu.VMEM((tt + HALO, D), F32)],
        compiler_params=_params("arbitrary"),
    )(gr, gr, du, du, conv_w)


def _loss_head(h, g, target, *, row_lo, row_hi, name):
    T, D = h.shape
    tm = _tile(T, 1056, 16)

    def body(h_ref, g_ref, t_ref, loss_ref, dh_ref, dhb_ref, dg_ref):
        i = pl.program_id(0)
        x = h_ref[...]
        g = g_ref[...]
        row = i * tm + lax.broadcasted_iota(jnp.int32, (tm, 1), 0)
        valid = jnp.logical_and(row >= row_lo, row < row_hi)
        rstd = _rstd(x)
        n = x * rstd
        err = jnp.where(valid, n * g - t_ref[...], 0.0)
        part = (0.5 / D) * jnp.sum(jnp.sum(err * err, axis=1, keepdims=True), axis=0, keepdims=True)
        dy = err * (1.0 / D)
        dn = dy * g
        dh = rstd * (dn - n * jnp.mean(dn * n, axis=-1, keepdims=True))
        dh_ref[...] = dh
        dhb_ref[...] = dh.astype(BF16)
        dg = jnp.sum(dy * n, axis=0, keepdims=True)

        @pl.when(i == 0)
        def _():
            loss_ref[...] = part
            dg_ref[...] = dg

        @pl.when(i > 0)
        def _():
            loss_ref[...] += part
            dg_ref[...] += dg

    blk = pl.BlockSpec((tm, D), lambda i: (i, 0))
    vec = pl.BlockSpec((1, D), lambda i: (0, 0))
    return _pcall(
        body, name=name, grid=(T // tm,),
        in_specs=[blk, vec, blk],
        out_specs=[pl.BlockSpec((1, 1), lambda i: (0, 0)), blk, blk, vec],
        out_shape=[_sds((1, 1), F32), _sds((T, D), F32), _sds((T, D), BF16), _sds((1, D), F32)],
        compiler_params=_params("arbitrary"),
    )(h, g, target)


def _adamw_math(w, g, m, v):
    c1 = 1.0 / (1.0 - ADAM_B1 ** ADAM_STEP)
    c2 = 1.0 / (1.0 - ADAM_B2 ** ADAM_STEP)
    m = ADAM_B1 * m + (1.0 - ADAM_B1) * g
    v = ADAM_B2 * v + (1.0 - ADAM_B2) * (g * g)
    delta = -ADAM_LR * ((m * c1) / (jnp.sqrt(v * c2) + ADAM_EPS) + ADAM_WD * w)
    return delta, m, v


def _adamw_halves(w, mine, theirs, m, v, c, *, name):
    _, R, C = w.shape
    tr = _tile(R, 256, SUBLANE)

    def body(c_ref, w_ref, a_ref, b_ref, m_ref, v_ref, g_ref, d_ref, nm_ref, nv_ref):
        g = jnp.where(pl.program_id(0) == c_ref[0], a_ref[...], b_ref[...])
        g_ref[...] = g
        d_ref[...], nm_ref[...], nv_ref[...] = _adamw_math(w_ref[...], g, m_ref[...], v_ref[...])

    full = pl.BlockSpec((None, tr, C), lambda h, i, c_ref: (h, i, 0))
    half = pl.BlockSpec((tr, C), lambda h, i, c_ref: (i, 0))
    return _pcall(
        body, name=name,
        grid_spec=pltpu.PrefetchScalarGridSpec(
            num_scalar_prefetch=1, grid=(2, R // tr),
            in_specs=[full, half, half, full, full], out_specs=[full] * 4),
        out_shape=[_sds((2, R, C), F32)] * 4,
        compiler_params=_params("parallel", "parallel"),
    )(c, w, mine, theirs, m, v)


def _adamw(w, g, m, v, *, name):
    R, C = w.shape
    tr = _tile(R, 512, SUBLANE)

    def body(w_ref, g_ref, m_ref, v_ref, d_ref, nm_ref, nv_ref):
        d_ref[...], nm_ref[...], nv_ref[...] = _adamw_math(w_ref[...], g_ref[...], m_ref[...], v_ref[...])

    blk = pl.BlockSpec((tr, C), lambda i: (i, 0))
    return _pcall(
        body, name=name, grid=(R // tr,),
        in_specs=[blk] * 4, out_specs=[blk] * 3,
        out_shape=[_sds((R, C), F32)] * 3,
        compiler_params=_params("parallel"),
    )(w, g, m, v)


ANY = pl.BlockSpec(memory_space=pl.ANY)


def _place():
    x, y, c = lax.axis_index("x"), lax.axis_index("y"), lax.axis_index("c")
    chips = [(1 - x, y), (x, 1 - y), (1 - x, 1 - y)]
    return x, y, c, chips


LOCAL_PIECES = 4


class _GatherChips:
    def __init__(self, vs):
        self.inputs = list(vs)
        n = self.n = len(vs)
        self.out_shape = [_sds((4,) + v.shape, v.dtype) for v in vs]
        self.scratch = [pltpu.SemaphoreType.DMA((6 * n,)), pltpu.SemaphoreType.DMA((6 * n,)),
                        pltpu.SemaphoreType.DMA((LOCAL_PIECES * n,))]

    def _copies(self, v_refs, o_refs, sems):
        send_sems, recv_sems, local_sems = sems
        x, y, c, chips = _place()
        me = 2 * x + y

        def copy(a, k, block, half, to, src=None):
            dst = o_refs[a].at[block, half]
            return pltpu.make_async_remote_copy(
                src_ref=dst if src is None else src, dst_ref=dst,
                send_sem=send_sems.at[6 * a + k], recv_sem=recv_sems.at[6 * a + k],
                device_id=to, device_id_type=MESH)

        ks = [(a, k, cx, cy) for a in range(self.n) for k, (cx, cy) in enumerate(chips)]

        def local():
            out = []
            for a in range(self.n):
                rows = self.inputs[a].shape[1] // (LOCAL_PIECES // 2)
                for p in range(LOCAL_PIECES):
                    h, r0 = p % 2, (p // 2) * rows
                    out.append(pltpu.make_async_copy(
                        v_refs[a].at[h, pl.ds(r0, rows)], o_refs[a].at[me, h, pl.ds(r0, rows)],
                        local_sems.at[LOCAL_PIECES * a + p]))
            return out

        return dict(
            first=lambda: [copy(a, k, me, c, (cx, cy, c), src=v_refs[a].at[c]) for a, k, cx, cy in ks],
            landed=lambda: [copy(a, k, 2 * cx + cy, c, (x, y, c)) for a, k, cx, cy in ks],
            passed=lambda: [copy(a, 3 + k, 2 * cx + cy, c, (x, y, 1 - c)) for a, k, cx, cy in ks],
            final=lambda: [copy(a, 3 + k, 2 * cx + cy, 1 - c, (x, y, c)) for a, k, cx, cy in ks],
            local=local)

    def start(self, v_refs, o_refs, sems):
        cps = self._copies(v_refs, o_refs, sems)
        for cp in cps["first"]() + cps["local"]():
            cp.start()

    def mid(self, v_refs, o_refs, sems):
        cps = self._copies(v_refs, o_refs, sems)
        for got, fwd in zip(cps["landed"](), cps["passed"]()):
            got.wait_recv()
            fwd.start()

    def finish(self, v_refs, o_refs, sems):
        cps = self._copies(v_refs, o_refs, sems)
        for cp in cps["final"]():
            cp.wait_recv()
        for cp in cps["first"]() + cps["passed"]():
            cp.wait_send()
        for cp in cps["local"]():
            cp.wait()


class _ExchangeBlocks:
    def __init__(self, ps):
        self.inputs = list(ps)
        n = self.n = len(ps)
        self.out_shape = [_sds((8,) + p.shape[2:], p.dtype) for p in ps]
        self.scratch = [pltpu.SemaphoreType.DMA((7 * n,)), pltpu.SemaphoreType.DMA((7 * n,))]

    def _copies(self, p_refs, o_refs, sems, incoming):
        send_sems, recv_sems = sems
        x, y, c, _ = _place()
        me = 4 * x + 2 * y + c
        out = []
        for a in range(self.n):
            for k in range(1, 8):
                px, py, pc = x ^ (k >> 2), y ^ ((k >> 1) & 1), c ^ (k & 1)
                out.append(pltpu.make_async_remote_copy(
                    src_ref=p_refs[a].at[2 * px + py, pc],
                    dst_ref=o_refs[a].at[4 * px + 2 * py + pc if incoming else me],
                    send_sem=send_sems.at[7 * a + k - 1], recv_sem=recv_sems.at[7 * a + k - 1],
                    device_id=(x, y, c) if incoming else (px, py, pc), device_id_type=MESH))
        return out

    def start(self, p_refs, o_refs, sems):
        for cp in self._copies(p_refs, o_refs, sems, False):
            cp.start()

    def mid(self, p_refs, o_refs, sems):
        pass

    def finish(self, p_refs, o_refs, sems):
        for cp in self._copies(p_refs, o_refs, sems, True):
            cp.wait_recv()
        for cp in self._copies(p_refs, o_refs, sems, False):
            cp.wait_send()


def _run_exchange(plan, *, name):
    n = plan.n

    def body(*refs):
        args = (refs[:n], refs[n:2 * n], refs[2 * n:])
        plan.start(*args)
        plan.mid(*args)
        plan.finish(*args)

    return _pcall(
        body, name=name, in_specs=[ANY] * n, out_specs=[ANY] * n,
        out_shape=plan.out_shape, scratch_shapes=plan.scratch,
    )(*plan.inputs)


def _send_sibling(rs, *, name):
    n = len(rs)

    def body(*refs):
        r_refs, o_refs = refs[:n], refs[n:2 * n]
        send_sems, recv_sems = refs[2 * n:]
        x, y, c, _ = _place()
        cps = [pltpu.make_async_remote_copy(
            src_ref=r_refs[a], dst_ref=o_refs[a], send_sem=send_sems.at[a], recv_sem=recv_sems.at[a],
            device_id=(x, y, 1 - c), device_id_type=MESH) for a in range(n)]
        for cp in cps:
            cp.start()
        for cp in cps:
            cp.wait()

    return _pcall(
        body, name=name, in_specs=[ANY] * n, out_specs=[ANY] * n,
        out_shape=[_sds(r.shape, r.dtype) for r in rs],
        scratch_shapes=[pltpu.SemaphoreType.DMA((n,)), pltpu.SemaphoreType.DMA((n,))],
    )(*rs)


def _add_devices(p, got, place, *, name):
    _, _, R, C = p.shape
    tr = _tile(R, 256, 16)

    def body(place_ref, p_ref, o_ref, out_ref):
        me = place_ref[2]
        own = p_ref[...].astype(F32)
        acc = jnp.where(me == 0, own, o_ref[0].astype(F32))
        for d in range(1, 8):
            acc = acc + jnp.where(me == d, own, o_ref[d].astype(F32))
        out_ref[...] = acc

    return _pcall(
        body, name=name,
        grid_spec=pltpu.PrefetchScalarGridSpec(
            num_scalar_prefetch=1, grid=(R // tr,),
            in_specs=[pl.BlockSpec((None, None, tr, C), lambda i, pr: (pr[0], pr[1], i, 0)),
                      pl.BlockSpec((8, tr, C), lambda i, pr: (0, i, 0))],
            out_specs=pl.BlockSpec((tr, C), lambda i, pr: (i, 0))),
        out_shape=_sds((R, C), F32),
        compiler_params=_params("parallel"),
    )(place, p, got)


def _round_up(n, m):
    return (n + m - 1) // m * m


def _f32_as_bf16(a):
    return lax.bitcast_convert_type(a.astype(F32), BF16).reshape(-1)


def _bf16_as_f32(a):
    return lax.bitcast_convert_type(a.reshape(-1, 2), F32)


def _by_chip_cols(a, cols):
    lead = a.shape[:-1]
    a = a.reshape(lead + (4, cols))
    return jnp.moveaxis(a, -2, 0).reshape(4, -1)


def kernel(x, meta_tokens, norm_mix, norm_mlp, sb_w_qkv, sb_w_o, lru_w_in, lru_conv_w, lru_conv_b, lru_w_rg, lru_b_rg, lru_w_ig, lru_b_ig, lru_lambda, lru_w_out, mlp_w_up, mlp_w_down, norm_final, loss_target, m_meta_tokens, m_norm_mix, m_norm_mlp, m_sb_w_qkv, m_sb_w_o, m_lru_w_in, m_lru_conv_w, m_lru_conv_b, m_lru_w_rg, m_lru_b_rg, m_lru_w_ig, m_lru_b_ig, m_lru_lambda, m_lru_w_out, m_mlp_w_up, m_mlp_w_down, m_norm_final, v_meta_tokens, v_norm_mix, v_norm_mlp, v_sb_w_qkv, v_sb_w_o, v_lru_w_in, v_lru_conv_w, v_lru_conv_b, v_lru_w_rg, v_lru_b_rg, v_lru_w_ig, v_lru_b_ig, v_lru_lambda, v_lru_w_out, v_mlp_w_up, v_mlp_w_down, v_norm_final):
    weights = dict(meta_tokens=meta_tokens, norm_mix=norm_mix, norm_mlp=norm_mlp, sb_w_qkv=sb_w_qkv,
                   sb_w_o=sb_w_o, lru_w_in=lru_w_in, lru_conv_w=lru_conv_w, lru_conv_b=lru_conv_b,
                   lru_w_rg=lru_w_rg, lru_b_rg=lru_b_rg, lru_w_ig=lru_w_ig, lru_b_ig=lru_b_ig,
                   lru_lambda=lru_lambda, lru_w_out=lru_w_out, mlp_w_up=mlp_w_up, mlp_w_down=mlp_w_down,
                   norm_final=norm_final)
    m_in = dict(meta_tokens=m_meta_tokens, norm_mix=m_norm_mix, norm_mlp=m_norm_mlp, sb_w_qkv=m_sb_w_qkv,
                sb_w_o=m_sb_w_o, lru_w_in=m_lru_w_in, lru_conv_w=m_lru_conv_w, lru_conv_b=m_lru_conv_b,
                lru_w_rg=m_lru_w_rg, lru_b_rg=m_lru_b_rg, lru_w_ig=m_lru_w_ig, lru_b_ig=m_lru_b_ig,
                lru_lambda=m_lru_lambda, lru_w_out=m_lru_w_out, mlp_w_up=m_mlp_w_up,
                mlp_w_down=m_mlp_w_down, norm_final=m_norm_final)
    v_in = dict(meta_tokens=v_meta_tokens, norm_mix=v_norm_mix, norm_mlp=v_norm_mlp, sb_w_qkv=v_sb_w_qkv,
                sb_w_o=v_sb_w_o, lru_w_in=v_lru_w_in, lru_conv_w=v_lru_conv_w, lru_conv_b=v_lru_conv_b,
                lru_w_rg=v_lru_w_rg, lru_b_rg=v_lru_b_rg, lru_w_ig=v_lru_w_ig, lru_b_ig=v_lru_b_ig,
                lru_lambda=v_lru_lambda, lru_w_out=v_lru_w_out, mlp_w_up=v_mlp_w_up,
                mlp_w_down=v_mlp_w_down, norm_final=v_norm_final)
    names = list(weights)

    seq, D = x.shape[1], x.shape[2]
    n_meta = meta_tokens.shape[0]
    Dq = D // 4
    T = _round_up(n_meta + seq, ATT_BLOCK)
    nb = lru_w_rg.shape[1]
    F = mlp_w_up.shape[2]
    depth = mlp_w_up.shape[0]
    my_x, my_y, my_c = lax.axis_index("x"), lax.axis_index("y"), lax.axis_index("c")
    c_arr = jnp.reshape(my_c, (1,)).astype(jnp.int32)

    assert depth == 2

    def halves(a):
        return a.astype(BF16).reshape(2, a.shape[0] // 2, a.shape[1])

    small = [meta_tokens, lru_conv_w[0], lru_conv_b, lru_b_rg, lru_b_ig, lru_lambda]
    sparts = [_f32_as_bf16(s) for s in small]
    sizes = [p.shape[0] for p in sparts]
    total = _round_up(sum(sizes), 2 * 32 * LANE)
    sflat =jnp.concatenate(sparts + [jnp.zeros((total - sum(sizes),), BF16)]).reshape(2, -1, LANE)
    gq, gsm = _run_exchange(_GatherChips([halves(sb_w_qkv[0]), sflat]), name="gather_first")
    gather_rest = _GatherChips([halves(sb_w_o[0]), halves(lru_w_in[0]), halves(lru_w_out[0]),
                                mlp_w_up.astype(BF16), mlp_w_down.astype(BF16)])
    w_qkv = gq.reshape(4, D, 3 * Dq)
    gsm = gsm.reshape(4, total)
    offs = [sum(sizes[:k]) for k in range(len(sizes))]
    sm = [_bf16_as_f32(gsm[:, o:o + s]) for o, s in zip(offs, sizes)]
    meta_full = jnp.moveaxis(sm[0].reshape(4, n_meta, Dq), 0, 1).reshape(n_meta, D)
    conv_w = jnp.moveaxis(sm[1].reshape(4, 4, Dq), 0, 1).reshape(4, D)
    conv_b, b_rg, b_ig, lam = [s.reshape(1, D) for s in sm[2:6]]
    w_rg = lru_w_rg[0].astype(BF16)
    w_ig = lru_w_ig[0].astype(BF16)
    g_mix = [norm_mix[l].reshape(1, D) for l in range(depth)]
    g_mlp = [norm_mlp[l].reshape(1, D) for l in range(depth)]
    g_fin = norm_final.reshape(1, D)

    pad_rows = T - n_meta - seq
    h0 = jnp.concatenate([meta_full, x[0], jnp.zeros((pad_rows, D), F32)], axis=0)
    target = jnp.concatenate([jnp.zeros((n_meta, D), F32), loss_target[0], jnp.zeros((pad_rows, D), F32)], axis=0)

    hn0, qkv = _norm_mm(h0, g_mix[0], w_qkv, out_dtype=BF16, name="qkv_proj")
    att, go, gi, gout, w_up, w_down = _attn_fwd(qkv, name="attn_fwd", plan=gather_rest)
    w_o = go.reshape(D, D)
    w_in = gi.reshape(4, D, 2 * Dq)
    w_out = gout.reshape(D, D)
    h1 = _mm_res(att, w_o, h0, name="attn_out")
    h2, hnm0, up0 = _mlp_fwd(h1, g_mlp[0], w_up, w_down, layer=0, name="mlp0_fwd")
    hn1, gr = _norm_mm(h2, g_mix[1], w_in, out_dtype=F32, name="lru_in")
    a_t, b_t = _lru_pre(gr, conv_w, conv_b, w_rg, b_rg, w_ig, b_ig, lam, name="lru_pre")
    hs = _lru_scan(a_t, b_t, reverse=False, name="lru_scan")
    y = _lru_out(gr, hs, name="lru_gate_out")
    h3 = _mm_res(y, w_out, h2, name="lru_out")
    h4, hnm1, up1 = _mlp_fwd(h3, g_mlp[1], w_up, w_down, layer=1, name="mlp1_fwd")
    loss, dh4, dh4b, dg_fin = _loss_head(h4, g_fin, target, row_lo=n_meta, row_hi=n_meta + seq, name="loss_head")

    dup1, dh3, dh3b, dg_mlp1 = _mlp_bwd(dh4, h3, g_mlp[1], up1, w_up, w_down, layer=1, name="mlp1_bwd")
    dw_up = _mm_tn(hnm1, dup1, shards=4, relu2=False, slot=1, name="mlp1_dwup")
    dw_down = _mm_tn(up1, dh4b, shards=1, relu2=True, slot=1, row_shards=4, name="mlp1_dwdown")
    dy = _mm_nt(dh3b, w_out, out_dtype=F32, name="lru_out_bwd")
    dw_out = _mm_tn(y, dh3b, shards=1, relu2=False, name="lru_dwout")
    dgate, dhy = _lru_out_bwd(gr, hs, dy, name="lru_gate_out_bwd")
    lmb = _lru_scan(a_t, dhy, reverse=True, name="lru_scan_bwd")
    du, dw_rg, db_rg, dw_ig, db_ig, dlam = _lru_gate_bwd(
        gr, hs, lmb, conv_w, conv_b, w_rg, b_rg, w_ig, b_ig, lam, name="lru_gate_bwd")
    drec, dconv_w, dconv_b = _lru_conv_bwd(gr, du, conv_w, name="lru_conv_bwd")
    dgr = jnp.concatenate([dgate, drec], axis=1)
    dh2, dh2b, dg_mix1 = _mm_nt_normbwd(dgr, w_in, h2, g_mix[1], dh3, name="lru_in_bwd")
    dw_in = _mm_tn(hn1, dgr, shards=4, relu2=False, name="lru_dwin")
    dup0, dh1, dh1b, dg_mlp0 = _mlp_bwd(dh2, h1, g_mlp[0], up0, w_up, w_down, layer=0, name="mlp0_bwd")
    dw_up = _mm_tn(hnm0, dup0, shards=4, relu2=False, slot=0, into=dw_up, name="mlp0_dwup")
    dw_down = _mm_tn(up0, dh2b, shards=1, relu2=True, slot=0, into=dw_down, row_shards=4, name="mlp0_dwdown")
    datt = _mm_nt(dh1b, w_o, out_dtype=BF16, name="attn_out_bwd")
    dw_o = _mm_tn(att, dh1b, shards=1, relu2=False, name="attn_dwo")
    def halves_of(d, rows):
        return d.reshape(4, 2, rows // 2, d.shape[-1])

    early = [halves_of(dw_o, Dq), halves_of(dw_in, D), halves_of(dw_out, Dq), dw_up, dw_down]
    dq, dk, dv, *got_early = _attn_bwd(qkv, datt, name="attn_bwd", plan=_ExchangeBlocks(early))
    dqkv = jnp.concatenate([dq, dk, dv], axis=1)
    dh0, _, dg_mix0 = _mm_nt_normbwd(dqkv, w_qkv, h0, g_mix[0], dh1, name="qkv_bwd")
    dw_qkv = _mm_tn(hn0, dqkv, shards=4, relu2=False, name="attn_dwqkv")
    grad_x = dh0[n_meta:n_meta + seq][None]
    dmeta = dh0[:n_meta]

    large = ["sb_w_o", "lru_w_in", "lru_w_out", "mlp_w_up", "mlp_w_down", "sb_w_qkv"]
    sharded =[_by_chip_cols(dmeta, Dq), _by_chip_cols(dconv_w, Dq), dconv_b.reshape(4, Dq),
               db_rg.reshape(4, Dq), db_ig.reshape(4, Dq), dlam.reshape(4, Dq)]
    repl = [jnp.concatenate([dg_mix0, dg_mix1], axis=0).reshape(-1),
            jnp.concatenate([dg_mlp0, dg_mlp1], axis=0).reshape(-1),
            dg_fin.reshape(-1), dw_rg.reshape(-1), dw_ig.reshape(-1)]
    rsizes = [r.shape[0] for r in repl]
    rtotal = _round_up(sum(rsizes), 4 * 2 * 16 * LANE)
    rflat = jnp.concatenate(repl + [jnp.zeros((rtotal - sum(rsizes),), F32)]).reshape(4, rtotal // 4)
    gsizes = [s.shape[1] for s in sharded] + [rtotal // 4]
    gtotal = _round_up(sum(gsizes), 2 * 16 * LANE)
    tail = jnp.concatenate(sharded + [rflat, jnp.zeros((4, gtotal - sum(gsizes)), F32)], axis=1)
    late = [halves_of(dw_qkv, D), tail.reshape(4, 2, -1, LANE)]
    got_late = _run_exchange(_ExchangeBlocks(late), name="reduce_late")
    place = jnp.stack([2 * my_x + my_y, my_c, 4 * my_x + 2 * my_y + my_c]).astype(jnp.int32)
    mine = [_add_devices(p, o, place, name="reduce_add_" + t)
            for p, o, t in zip(early + late, list(got_early) + list(got_late), large + ["tail"])]
    theirs = _send_sibling(mine, name="reduce_join")

    grads, delta, new_m, new_v = {}, {}, {}, {}
    for n, a, b in zip(large, mine, theirs):
        shp = weights[n].shape
        view = (2,) + a.shape
        g, d, nm, nv = _adamw_halves(weights[n].reshape(view), a, b, m_in[n].reshape(view),
                                     v_in[n].reshape(view), c_arr, name="adamw_" + n)
        grads[n], delta[n], new_m[n], new_v[n] = g.reshape(shp), d.reshape(shp), nm.reshape(shp), nv.reshape(shp)

    lo = jnp.where(my_c == 0, mine[-1], theirs[-1])
    hi = jnp.where(my_c == 0, theirs[-1], mine[-1])
    gshard = jnp.concatenate([lo, hi], axis=0).reshape(gtotal)
    goffs = [sum(gsizes[:k]) for k in range(len(gsizes))]
    gp = [gshard[o:o + s] for o, s in zip(goffs, gsizes)]
    rfull = _run_exchange(_GatherChips([gp[-1].reshape(2, -1, LANE)]), name="gather_replicated")[0].reshape(rtotal)
    roffs = [sum(rsizes[:k]) for k in range(len(rsizes))]
    rp = [rfull[o:o + s] for o, s in zip(roffs, rsizes)]
    grads.update(meta_tokens=gp[0], lru_conv_w=gp[1], lru_conv_b=gp[2], lru_b_rg=gp[3], lru_b_ig=gp[4],
                 lru_lambda=gp[5], norm_mix=rp[0], norm_mlp=rp[1], norm_final=rp[2], lru_w_rg=rp[3],
                 lru_w_ig=rp[4])
    grads = {n: grads[n].reshape(weights[n].shape) for n in names}
    rest = [n for n in names if n not in large]
    ssz = [weights[n].size for n in rest]
    small_cols = 8 * LANE
    stotal = _round_up(sum(ssz), SUBLANE * small_cols)

    def pack(src):
        return jnp.concatenate([src[n].reshape(-1) for n in rest]
                               + [jnp.ones((stotal - sum(ssz),), F32)]).reshape(-1, small_cols)

    d, nm, nv = _adamw(pack(weights), pack(grads), pack(m_in), pack(v_in), name="adamw_small")
    soffs = [sum(ssz[:k]) for k in range(len(ssz))]
    for n, o, s in zip(rest, soffs, ssz):
        shp = weights[n].shape
        delta[n] = d.reshape(-1)[o:o + s].reshape(shp)
        new_m[n] = nm.reshape(-1)[o:o + s].reshape(shp)
        new_v[n] = nv.reshape(-1)[o:o + s].reshape(shp)

    loss = lax.psum(loss[0, 0], ("x", "y", "c"))
    return (loss, grad_x, *[grads[n] for n in names], *[delta[n] for n in names],
            *[new_m[n] for n in names], *[new_v[n] for n in names])
```

```python
import functools

import jax
import jax.numpy as jnp
from jax import lax
from jax.experimental import pallas as pl
from jax.experimental.pallas import tpu as pltpu

F32 = jnp.float32
BF16 = jnp.bfloat16
MESH = pl.DeviceIdType.MESH

EPS = 1e-6
HEAD_DIM = 64
LANE = 128
SUBLANE = 8
LRU_C = 8.0
VMEM_LIMIT = 56 * 1024 * 1024

ADAM_LR = 0.001
ADAM_B1 = 0.9
ADAM_B2 = 0.999
ADAM_EPS = 1e-08
ADAM_WD = 0.01
ADAM_STEP = 10


def _pcall(body, **kw):
    return pl.pallas_call(body, **kw)


def _params(*sem):
    return pltpu.CompilerParams(dimension_semantics=sem, vmem_limit_bytes=VMEM_LIMIT)


def _tile(n, pref, align):
    best = None
    for t in range(align, min(n, pref) + 1, align):
        if n % t == 0:
            best = t
    return n if best is None else best


def _sds(shape, dtype):
    return jax.ShapeDtypeStruct(shape, dtype)


def _rstd(x):
    return lax.rsqrt(jnp.mean(x * x, axis=-1, keepdims=True) + EPS)


def _norm_bwd(x, g, dy):
    rstd = _rstd(x)
    n = x * rstd
    dn = dy * g
    dx = rstd * (dn - n * jnp.mean(dn * n, axis=-1, keepdims=True))
    dg = jnp.sum(dy * n, axis=0, keepdims=True)
    return dx, dg


def _softplus_parts(z):
    l1p = jnp.log(1.0 + jnp.exp(-jnp.abs(z)))
    return jnp.maximum(z, 0.0) + l1p, jnp.minimum(z, 0.0) - l1p


def _sigmoid(x):
    return 1.0 / (1.0 + jnp.exp(-x))


def _gelu_parts(x):
    k = 0.7978845608028654
    inner = k * (x + 0.044715 * (x * x * x))
    t = jnp.tanh(inner)
    gelu = 0.5 * x * (1.0 + t)
    dgelu = 0.5 * (1.0 + t) + 0.5 * x * (1.0 - t * t) * (k * (1.0 + 3.0 * 0.044715 * (x * x)))
    return gelu, dgelu


def _norm_mm(h, g, w, *, out_dtype, name):
    T, D = h.shape
    S, _, n = w.shape
    tm = _tile(T, 1056, 16)
    tn = _tile(n, 768, LANE)
    nj = n // tn

    def body(h_ref, g_ref, w_ref, hn_ref, o_ref):
        @pl.when(pl.program_id(1) == 0)
        def _():
            x = h_ref[...]
            hn_ref[...] = (x * _rstd(x) * g_ref[...]).astype(BF16)

        o_ref[...] = jnp.dot(hn_ref[...], w_ref[...], preferred_element_type=F32).astype(out_dtype)

    return _pcall(
        body, name=name, grid=(T // tm, S * nj),
        in_specs=[pl.BlockSpec((tm, D), lambda i, j: (i, 0)),
                  pl.BlockSpec((1, D), lambda i, j: (0, 0)),
                  pl.BlockSpec((None, D, tn), lambda i, j: (j // nj, 0, j % nj))],
        out_specs=[pl.BlockSpec((tm, D), lambda i, j: (i, 0)),
                   pl.BlockSpec((tm, tn), lambda i, j: (i, j))],
        out_shape=[_sds((T, D), BF16), _sds((T, S * n), out_dtype)],
        compiler_params=_params("parallel", "arbitrary"),
    )(h, g, w)


def _mm_res(a, w, res, *, name):
    T, K = a.shape
    N = w.shape[1]
    tm = _tile(T, 1056, 16)

    def body(a_ref, w_ref, r_ref, o_ref):
        o_ref[...] = r_ref[...] + jnp.dot(a_ref[...], w_ref[...], preferred_element_type=F32)

    return _pcall(
        body, name=name, grid=(T // tm,),
        in_specs=[pl.BlockSpec((tm, K), lambda i: (i, 0)),
                  pl.BlockSpec((K, N), lambda i: (0, 0)),
                  pl.BlockSpec((tm, N), lambda i: (i, 0))],
        out_specs=pl.BlockSpec((tm, N), lambda i: (i, 0)),
        out_shape=_sds((T, N), F32),
        compiler_params=_params("parallel"),
    )(a, w, res)


def _mm_nt(a, w, *, out_dtype, name):
    T, N = a.shape
    K = w.shape[0]
    tm = _tile(T, 1056, 16)

    def body(a_ref, w_ref, o_ref):
        o_ref[...] = lax.dot_general(a_ref[...], w_ref[...], (((1,), (1,)), ((), ())),
                                     preferred_element_type=F32).astype(out_dtype)

    return _pcall(
        body, name=name, grid=(T // tm,),
        in_specs=[pl.BlockSpec((tm, N), lambda i: (i, 0)),
                  pl.BlockSpec((K, N), lambda i: (0, 0))],
        out_specs=pl.BlockSpec((tm, K), lambda i: (i, 0)),
        out_shape=_sds((T, K), out_dtype),
        compiler_params=_params("parallel"),
    )(a, w)


def _mm_tn(a, b, *, shards, relu2, name, slot=None, into=None, row_shards=0, out_dtype=BF16):
    T, Ka = a.shape
    Nb = b.shape[1]
    n = Nb // shards
    tka = _tile(Ka, 512, LANE)
    tnb = _tile(n, 512, LANE)
    nj = n // tnb

    def body(a_ref, b_ref, *rest):
        o_ref = rest[-1]
        av = a_ref[...]
        if relu2:
            r = jnp.maximum(av, 0)
            av = r * r
        o_ref[...] = lax.dot_general(av, b_ref[...], (((0,), (0,)), ((), ())),
                                     preferred_element_type=F32).astype(out_dtype)

    in_specs = [pl.BlockSpec((T, tka), lambda i, j: (0, i)),
                pl.BlockSpec((T, tnb), lambda i, j: (0, j))]
    args = [a, b]
    aliases = {}
    if slot is None:
        out_spec = pl.BlockSpec((None, tka, tnb), lambda i, j: (j // nj, i, j % nj))
        out_shape = _sds((shards, Ka, n), out_dtype)
    else:
        if row_shards:
            ni = Ka // row_shards // tka
            out_spec = pl.BlockSpec((None, None, tka, tnb), lambda i, j: (i // ni, slot, i % ni, j))
            out_shape = _sds((row_shards, 2, Ka // row_shards, n), out_dtype)
        else:
            out_spec = pl.BlockSpec((None, None, tka, tnb), lambda i, j: (j // nj, slot, i, j % nj))
            out_shape = _sds((shards, 2, Ka, n), out_dtype)
        if into is not None:
            in_specs.append(pl.BlockSpec(memory_space=pl.ANY))
            args.append(into)
            aliases = {2: 0}
    return _pcall(
        body, name=name, grid=(Ka // tka, shards * nj),
        in_specs=in_specs, out_specs=out_spec, out_shape=out_shape,
        input_output_aliases=aliases,
        compiler_params=_params("parallel", "parallel"),
    )(*args)


def _mm_nt_normbwd(dy, w, h, g, dres, *, name):
    T, D = h.shape
    S, _, n = w.shape
    tm = _tile(T, 528, 16)

    def body(dy_ref, w_ref, h_ref, g_ref, dr_ref, dh_ref, dhb_ref, dg_ref, acc_ref):
        i, s = pl.program_id(0), pl.program_id(1)
        part = lax.dot_general(dy_ref[...], w_ref[...], (((1,), (1,)), ((), ())),
                               preferred_element_type=F32)

        @pl.when(s == 0)
        def _():
            acc_ref[...] = part

        @pl.when(s > 0)
        def _():
            acc_ref[...] += part

        @pl.when(s == S - 1)
        def _():
            dx, dg = _norm_bwd(h_ref[...], g_ref[...], acc_ref[...])
            dh = dr_ref[...] + dx
            dh_ref[...] = dh
            dhb_ref[...] = dh.astype(BF16)

            @pl.when(i == 0)
            def _():
                dg_ref[...] = dg

            @pl.when(i > 0)
            def _():
                dg_ref[...] += dg

    return _pcall(
        body, name=name, grid=(T // tm, S),
        in_specs=[pl.BlockSpec((tm, n), lambda i, s: (i, s)),
                  pl.BlockSpec((None, D, n), lambda i, s: (s, 0, 0)),
                  pl.BlockSpec((tm, D), lambda i, s: (i, 0)),
                  pl.BlockSpec((1, D), lambda i, s: (0, 0)),
                  pl.BlockSpec((tm, D), lambda i, s: (i, 0))],
        out_specs=[pl.BlockSpec((tm, D), lambda i, s: (i, 0)),
                   pl.BlockSpec((tm, D), lambda i, s: (i, 0)),
                   pl.BlockSpec((1, D), lambda i, s: (0, 0))],
        out_shape=[_sds((T, D), F32), _sds((T, D), BF16), _sds((1, D), F32)],
        scratch_shapes=[pltpu.VMEM((tm, D), F32)],
        compiler_params=_params("arbitrary", "arbitrary"),
    )(dy, w, h, g, dres)


def _row_chains(tm):
    first = (tm // 2 + 15) // 16 * 16
    return [slice(0, first), slice(first, tm)] if 0 < first < tm else [slice(0, tm)]


def _mlp_fwd(h, g, w_up, w_down, *, layer, name):
    T, D = h.shape
    S, _, _, n = w_up.shape
    tm = _tile(T, 528, 16)
    tf = _tile(n, 1024, LANE)
    nj = n // tf
    nf = S * nj
    chains = _row_chains(tm)

    def body(h_ref, g_ref, wu_ref, wd_ref, o_ref, hn_ref, up_ref, acc_ref):
        f = pl.program_id(1)

        @pl.when(f == 0)
        def _():
            x = h_ref[...]
            hn_ref[...] = (x * _rstd(x) * g_ref[...]).astype(BF16)

        parts = []
        for rows in chains:
            up = jnp.dot(hn_ref[rows, :], wu_ref[...], preferred_element_type=F32)
            up_ref[rows, :] = up.astype(BF16)
            r = jnp.maximum(up, 0.0)
            parts.append(jnp.dot((r * r).astype(BF16), wd_ref[...], preferred_element_type=F32))
        part = jnp.concatenate(parts, axis=0)

        @pl.when(f == 0)
        def _():
            acc_ref[...] = part

        @pl.when(f > 0)
        def _():
            acc_ref[...] += part

        @pl.when(f == nf - 1)
        def _():
            o_ref[...] = h_ref[...] + acc_ref[...]

    return _pcall(
        body, name=name, grid=(T // tm, nf),
        in_specs=[pl.BlockSpec((tm, D), lambda i, f: (i, 0)),
                  pl.BlockSpec((1, D), lambda i, f: (0, 0)),
                  pl.BlockSpec((None, None, D, tf), lambda i, f: (f // nj, layer, 0, f % nj)),
                  pl.BlockSpec((None, None, tf, D), lambda i, f: (f // nj, layer, f % nj, 0))],
        out_specs=[pl.BlockSpec((tm, D), lambda i, f: (i, 0)),
                   pl.BlockSpec((tm, D), lambda i, f: (i, 0)),
                   pl.BlockSpec((tm, tf), lambda i, f: (i, f))],
        out_shape=[_sds((T, D), F32), _sds((T, D), BF16), _sds((T, S * n), BF16)],
        scratch_shapes=[pltpu.VMEM((tm, D), F32)],
        compiler_params=_params("parallel", "arbitrary"),
    )(h, g, w_up, w_down)


def _mlp_bwd(dy, h, g, up, w_up, w_down, *, layer, name):
    T, D = h.shape
    S, _, _, n = w_up.shape
    tm = _tile(T, 528, 16)
    tf = _tile(n, 1024, LANE)
    nj = n // tf
    nf = S * nj
    chains = _row_chains(tm)

    def body(dy_ref, h_ref, g_ref, up_ref, wu_ref, wd_ref, dup_ref, dh_ref, dhb_ref, dg_ref,
             dyb_ref, acc_ref):
        i, f = pl.program_id(0), pl.program_id(1)

        @pl.when(f == 0)
        def _():
            dyb_ref[...] = dy_ref[...].astype(BF16)

        parts = []
        for rows in chains:
            dact = lax.dot_general(dyb_ref[rows, :], wd_ref[...], (((1,), (1,)), ((), ())),
                                   preferred_element_type=F32)
            r = jnp.maximum(up_ref[rows, :].astype(F32), 0.0)
            dup = (dact * (2.0 * r)).astype(BF16)
            dup_ref[rows, :] = dup
            parts.append(lax.dot_general(dup, wu_ref[...], (((1,), (1,)), ((), ())),
                                         preferred_element_type=F32))
        part = jnp.concatenate(parts, axis=0)

        @pl.when(f == 0)
        def _():
            acc_ref[...] = part

        @pl.when(f > 0)
        def _():
            acc_ref[...] += part

        @pl.when(f == nf - 1)
        def _():
            dx, dg = _norm_bwd(h_ref[...], g_ref[...], acc_ref[...])
            dh = dy_ref[...] + dx
            dh_ref[...] = dh
            dhb_ref[...] = dh.astype(BF16)

            @pl.when(i == 0)
            def _():
                dg_ref[...] = dg

            @pl.when(i > 0)
            def _():
                dg_ref[...] += dg

    return _pcall(
        body, name=name, grid=(T // tm, nf),
        in_specs=[pl.BlockSpec((tm, D), lambda i, f: (i, 0)),
                  pl.BlockSpec((tm, D), lambda i, f: (i, 0)),
                  pl.BlockSpec((1, D), lambda i, f: (0, 0)),
                  pl.BlockSpec((tm, tf), lambda i, f: (i, f)),
                  pl.BlockSpec((None, None, D, tf), lambda i, f: (f // nj, layer, 0, f % nj)),
                  pl.BlockSpec((None, None, tf, D), lambda i, f: (f // nj, layer, f % nj, 0))],
        out_specs=[pl.BlockSpec((tm, tf), lambda i, f: (i, f)),
                   pl.BlockSpec((tm, D), lambda i, f: (i, 0)),
                   pl.BlockSpec((tm, D), lambda i, f: (i, 0)),
                   pl.BlockSpec((1, D), lambda i, f: (0, 0))],
        out_shape=[_sds((T, S * n), BF16), _sds((T, D), F32), _sds((T, D), BF16), _sds((1, D), F32)],
        scratch_shapes=[pltpu.VMEM((tm, D), BF16), pltpu.VMEM((tm, D), F32)],
        compiler_params=_params("arbitrary", "arbitrary"),
    )(dy, h, g, up, w_up, w_down)


ATT_BLOCK = 128
FWD_HEADS = 4


def _attn_tile(T):
    for w in (3 * ATT_BLOCK, 2 * ATT_BLOCK):
        if T % w == 0:
            return w
    return ATT_BLOCK


def _tri(strict_lower, value):
    B = ATT_BLOCK
    r = lax.broadcasted_iota(jnp.int32, (2 * B, B), 0)
    r = jnp.where(r >= B, r - B, r)
    c = lax.broadcasted_iota(jnp.int32, (2 * B, B), 1)
    m = (r > c) if strict_lower else (r < c)
    return jnp.where(m, value, 0.0).astype(BF16)


def _split_dot(x, tri):
    hi = lax.bitcast_convert_type(lax.bitcast_convert_type(x, jnp.uint32) & jnp.uint32(0xFFFF0000), F32)
    lo = x - hi
    return jnp.dot(jnp.concatenate([hi.astype(BF16), lo.astype(BF16)], axis=1), tri,
                   preferred_element_type=F32)


def _causal_mask(W):
    r = lax.broadcasted_iota(jnp.int32, (W, W), 0)
    c = lax.broadcasted_iota(jnp.int32, (W, W), 1)
    return c < r


def _attn_scores(qs, ks, carry, tri_neg, masked, want_sig):
    W = qs.shape[0]
    B = ATT_BLOCK
    z = lax.dot_general(qs, ks, (((1,), (1,)), ((), ())), preferred_element_type=F32)
    minus_abs = lax.bitcast_convert_type(
        lax.bitcast_convert_type(z, jnp.uint32) | jnp.uint32(0x80000000), F32)
    sp = jnp.maximum(z, 0.0) + jnp.log(1.0 + jnp.exp(minus_abs))
    logsig = z - sp
    if masked:
        causal = _causal_mask(W)
        sp = jnp.where(causal, sp, 0.0)
    afters = []
    for b in reversed(range(W // B)):
        blk = sp[:, b * B:(b + 1) * B]
        within = _split_dot(blk, tri_neg)
        afters.append(within + carry)
        carry = carry + (within[:, 0:1] - blk[:, 0:1])
    after = jnp.concatenate(afters[::-1], axis=1)
    w = jnp.exp(logsig + after)
    if masked:
        w = jnp.where(causal, w, 0.0)
    return w, (jnp.exp(logsig) if want_sig else None), carry


def _ride_along(plan, npairs, nq):
    if plan is None:
        return 0, (lambda refs: None), (lambda refs: None)
    nx = plan.n

    def before(refs):
        p, i = pl.program_id(0), pl.program_id(1)

        @pl.when(jnp.logical_and(p == 0, i == 0))
        def _():
            plan.start(*refs)

        @pl.when(jnp.logical_and(p == npairs // 2, i == 0))
        def _():
            plan.mid(*refs)

    def after(refs):
        p, i = pl.program_id(0), pl.program_id(1)

        @pl.when(jnp.logical_and(p == npairs - 1, i == nq - 1))
        def _():
            plan.finish(*refs)

    return nx, before, after


def _attn_fwd(qkv, *, name, plan=None):
    T = qkv.shape[0]
    D = qkv.shape[1] // 3
    W = _attn_tile(T)
    lanes = FWD_HEADS * HEAD_DIM
    npairs = D // lanes
    nq = T // W
    scale = HEAD_DIM ** -0.5
    nx, before, after = _ride_along(plan, npairs, nq)

    def body(*refs):
        q_ref, k_ref, v_ref = refs[:3]
        o_ref = refs[3 + nx]
        ride = (refs[3:3 + nx], refs[4 + nx:4 + 2 * nx], refs[4 + 2 * nx:])
        before(ride)
        i = pl.program_id(1)
        tri = _tri(True, -1.0)
        heads = [slice(hh * HEAD_DIM, (hh + 1) * HEAD_DIM) for hh in range(FWD_HEADS)]
        qs = [q_ref[:, cols] * scale for cols in heads]

        def tile(j, state, masked):
            rows = pl.ds(pl.multiple_of(j * W, W), W)
            out = []
            for cols, q, (carry, acc) in zip(heads, qs, state):
                w, _, carry = _attn_scores(q, k_ref[rows, cols], carry, tri, masked, False)
                acc = acc + jnp.dot(w.astype(BF16), v_ref[rows, cols], preferred_element_type=F32)
                out.append((carry, acc))
            return tuple(out)

        zero = (jnp.zeros((W, 1), F32), jnp.zeros((W, HEAD_DIM), F32))
        state = tile(i, (zero,) * FWD_HEADS, True)
        state = lax.fori_loop(0, i, lambda jj, st: tile(i - 1 - jj, st, False), state)
        for cols, (_, acc) in zip(heads, state):
            o_ref[:, cols] = acc.astype(BF16)
        after(ride)

    extra = plan.inputs if plan else []
    return _pcall(
        body, name=name, grid=(npairs, nq),
        in_specs=[pl.BlockSpec((W, lanes), lambda p, i: (i, p)),
                  pl.BlockSpec((T, lanes), lambda p, i: (0, npairs + p)),
                  pl.BlockSpec((T, lanes), lambda p, i: (0, 2 * npairs + p))] + [ANY] * nx,
        out_specs=[pl.BlockSpec((W, lanes), lambda p, i: (i, p))] + [ANY] * nx,
        out_shape=[_sds((T, D), BF16)] + (plan.out_shape if plan else []),
        scratch_shapes=plan.scratch if plan else [],
        compiler_params=_params("arbitrary", "arbitrary"),
    )(qkv, qkv, qkv, *extra)


def _attn_bwd(qkv, do, *, name, plan=None):
    T = qkv.shape[0]
    D = qkv.shape[1] // 3
    B = ATT_BLOCK
    W = _attn_tile(T)
    npairs = D // LANE
    nq = T // W
    scale = HEAD_DIM ** -0.5
    nx, before, after = _ride_along(plan, npairs, nq)

    def body(*refs):
        q_ref, k_ref, v_ref, do_ref = refs[:4]
        dq_ref, dk_ref, dv_ref = refs[4 + nx:7 + nx]
        s_s, w_s, dk_acc, dv_acc = refs[7 + 2 * nx:11 + 2 * nx]
        ride = (refs[4:4 + nx], refs[7 + nx:7 + 2 * nx], refs[11 + 2 * nx:])
        before(ride)
        i = pl.program_id(1)

        @pl.when(i == 0)
        def _():
            dk_acc[...] = jnp.zeros_like(dk_acc)
            dv_acc[...] = jnp.zeros_like(dv_acc)

        tri_after = _tri(True, -1.0)
        tri_before = _tri(False, 1.0)
        heads = [slice(hh * HEAD_DIM, (hh + 1) * HEAD_DIM) for hh in range(2)]
        qs = [q_ref[:, cols] * scale for cols in heads]
        douts = [do_ref[:, cols] for cols in heads]

        def score(j, carries, masked):
            rows = pl.ds(pl.multiple_of(j * W, W), W)
            out = []
            for hh, (cols, carry) in enumerate(zip(heads, carries)):
                w, sig, carry = _attn_scores(qs[hh], k_ref[rows, cols], carry, tri_after, masked, True)
                s_s[hh, j] = sig
                w_s[hh, j] = w
                out.append(carry)
            return tuple(out)

        zero = jnp.zeros((W, 1), F32)
        carries = score(i, (zero, zero), True)
        lax.fori_loop(0, i, lambda jj, c: score(i - 1 - jj, c, False), carries)

        def grad(j, state, masked):
            rows = pl.ds(pl.multiple_of(j * W, W), W)
            out = []
            for hh, (cols, (gsum, dq)) in enumerate(zip(heads, state)):
                sig = s_s[hh, j]
                w = w_s[hh, j]
                dw = lax.dot_general(douts[hh], v_ref[rows, cols], (((1,), (1,)), ((), ())),
                                     preferred_element_type=F32)
                g = dw * w
                befores = []
                for b in range(W // B):
                    blk = g[:, b * B:(b + 1) * B]
                    within = _split_dot(blk, tri_before)
                    befores.append(within + gsum)
                    gsum = gsum + (within[:, B - 1:B] + blk[:, B - 1:B])
                dz = g - sig * (g + jnp.concatenate(befores, axis=1))
                if masked:
                    dz = jnp.where(_causal_mask(W), dz, 0.0)
                dzb = dz.astype(BF16)
                dq = dq + jnp.dot(dzb, k_ref[rows, cols] * scale, preferred_element_type=F32)
                dk_acc[rows, cols] += lax.dot_general(dzb, qs[hh], (((0,), (0,)), ((), ())),
                                                      preferred_element_type=F32)
                dv_acc[rows, cols] += lax.dot_general(w.astype(BF16), douts[hh], (((0,), (0,)), ((), ())),
                                                      preferred_element_type=F32)
                out.append((gsum, dq))
            return tuple(out)

        zero2 = (zero, jnp.zeros((W, HEAD_DIM), F32))
        state = lax.fori_loop(0, i, lambda j, st: grad(j, st, False), (zero2, zero2))
        state = grad(i, state, True)
        for cols, (_, dq) in zip(heads, state):
            dq_ref[:, cols] = dq.astype(BF16)

        @pl.when(i == nq - 1)
        def _():
            dk_ref[...] = dk_acc[...].astype(BF16)
            dv_ref[...] = dv_acc[...].astype(BF16)

        after(ride)

    extra = plan.inputs if plan else []
    return _pcall(
        body, name=name, grid=(npairs, nq),
        in_specs=[pl.BlockSpec((W, LANE), lambda p, i: (i, p)),
                  pl.BlockSpec((T, LANE), lambda p, i: (0, npairs + p)),
                  pl.BlockSpec((T, LANE), lambda p, i: (0, 2 * npairs + p)),
                  pl.BlockSpec((W, LANE), lambda p, i: (i, p))] + [ANY] * nx,
        out_specs=[pl.BlockSpec((W, LANE), lambda p, i: (i, p)),
                   pl.BlockSpec((T, LANE), lambda p, i: (0, p)),
                   pl.BlockSpec((T, LANE), lambda p, i: (0, p))] + [ANY] * nx,
        out_shape=[_sds((T, D), BF16)] * 3 + (plan.out_shape if plan else []),
        scratch_shapes=[pltpu.VMEM((2, nq, W, W), F32), pltpu.VMEM((2, nq, W, W), F32),
                        pltpu.VMEM((T, LANE), F32), pltpu.VMEM((T, LANE), F32)]
        + (plan.scratch if plan else []),
        compiler_params=_params("arbitrary", "arbitrary"),
    )(qkv, qkv, qkv, do, *extra)


HALO = SUBLANE


def _lru_gates(u, w_rg, b_rg, w_ig, b_ig, lam):
    nb = w_rg.shape[0]
    pre_r, pre_i = [], []
    for n in range(nb):
        ub = u[:, n * LANE:(n + 1) * LANE].astype(BF16)
        pre_r.append(jnp.dot(ub, w_rg[n], preferred_element_type=F32))
        pre_i.append(jnp.dot(ub, w_ig[n], preferred_element_type=F32))
    r = _sigmoid(jnp.concatenate(pre_r, axis=1) + b_rg)
    i = _sigmoid(jnp.concatenate(pre_i, axis=1) + b_ig)
    c = -LRU_C * _softplus_parts(-lam)[0]
    log_a = c * r
    a = jnp.exp(log_a)
    x2 = 2.0 * log_a
    em1 = jnp.where(jnp.abs(x2) < 1e-2, x2 * (1.0 + x2 * (0.5 + x2 * (1.0 / 6.0))), jnp.exp(x2) - 1.0)
    mult = jnp.sqrt(-em1)
    return r, i, a, mult, c


def _conv_rows(buf_ref, tt, conv_w, conv_b):
    u = conv_b
    for j in range(4):
        u = u + buf_ref[pl.ds(HALO - 3 + j, tt), :] * conv_w[j:j + 1, :]
    return u


def _fill_with_halo(buf_ref, prev_ref, cur_ref, first):
    tt = cur_ref.shape[0]
    buf_ref[pl.ds(0, HALO), :] = jnp.where(first, 0.0, prev_ref[...])
    buf_ref[pl.ds(HALO, tt), :] = cur_ref[...]


def _lru_time_tile(T):
    return _tile(T, 256, SUBLANE)


def _lru_pre(gr, conv_w, conv_b, w_rg, b_rg, w_ig, b_ig, lam, *, name):
    T = gr.shape[0]
    D = gr.shape[1] // 2
    tt = _lru_time_tile(T)
    hb = tt // HALO

    def body(x_ref, xp_ref, cw_ref, cb_ref, wr_ref, br_ref, wi_ref, bi_ref, lam_ref, a_ref, b_ref, buf):
        _fill_with_halo(buf, xp_ref, x_ref, pl.program_id(0) == 0)
        u = _conv_rows(buf, tt, cw_ref[...], cb_ref[...])
        _, i, a, mult, _ = _lru_gates(u, wr_ref, br_ref[...], wi_ref, bi_ref[...], lam_ref[...])
        a_ref[...] = a
        b_ref[...] = mult * (i * u)

    vec = pl.BlockSpec((1, D), lambda t: (0, 0))
    mat = pl.BlockSpec(w_rg.shape, lambda t: (0, 0, 0))
    return _pcall(
        body, name=name, grid=(T // tt,),
        in_specs=[pl.BlockSpec((tt, D), lambda t: (t, 1)),
                  pl.BlockSpec((HALO, D), lambda t: (jnp.maximum(t * hb - 1, 0), 1)),
                  pl.BlockSpec((4, D), lambda t: (0, 0)), vec, mat, vec, mat, vec, vec],
        out_specs=[pl.BlockSpec((tt, D), lambda t: (t, 0))] * 2,
        out_shape=[_sds((T, D), F32)] * 2,
        scratch_shapes=[pltpu.VMEM((tt + HALO, D), F32)],
        compiler_params=_params("parallel"),
    )(gr, gr, conv_w, conv_b, w_rg, b_rg, w_ig, b_ig, lam)


def _lru_scan(a, b, *, reverse, name):
    T, D = a.shape
    nb = D // LANE
    ts = _tile(T, 1056, SUBLANE)
    nt = T // ts
    a3 = a.reshape(T, nb, LANE)
    b3 = b.reshape(T, nb, LANE)

    def body(a_ref, b_ref, o_ref, carry):
        @pl.when(pl.program_id(0) == 0)
        def _():
            carry[...] = jnp.zeros_like(carry)

        if reverse:
            def step(k, c):
                t = ts - 1 - k
                l = b_ref[t] + c
                o_ref[t] = l
                return a_ref[t] * l
        else:
            def step(k, h):
                h = a_ref[k] * h + b_ref[k]
                o_ref[k] = h
                return h

        carry[...] = lax.fori_loop(0, ts, step, carry[...], unroll=8)

    if reverse:
        spec = pl.BlockSpec((ts, nb, LANE), lambda t: (nt - 1 - t, 0, 0))
    else:
        spec = pl.BlockSpec((ts, nb, LANE), lambda t: (t, 0, 0))
    out = _pcall(
        body, name=name, grid=(nt,),
        in_specs=[spec, spec], out_specs=spec,
        out_shape=_sds((T, nb, LANE), F32),
        scratch_shapes=[pltpu.VMEM((nb, LANE), F32)],
        compiler_params=_params("arbitrary"),
    )(a3, b3)
    return out.reshape(T, D)


def _lru_out(gr, hs, *, name):
    T, D = hs.shape
    tt = _tile(T, 1056, 16)

    def body(g_ref, h_ref, y_ref):
        y_ref[...] = (h_ref[...] * _gelu_parts(g_ref[...])[0]).astype(BF16)

    return _pcall(
        body, name=name, grid=(T // tt,),
        in_specs=[pl.BlockSpec((tt, D), lambda t: (t, 0)), pl.BlockSpec((tt, D), lambda t: (t, 0))],
        out_specs=pl.BlockSpec((tt, D), lambda t: (t, 0)),
        out_shape=_sds((T, D), BF16),
        compiler_params=_params("parallel"),
    )(gr, hs)


def _lru_out_bwd(gr, hs, dy, *, name):
    T, D = hs.shape
    tt = _tile(T, 1056, 16)

    def body(g_ref, h_ref, dy_ref, dg_ref, dh_ref):
        gelu, dgelu = _gelu_parts(g_ref[...])
        dy = dy_ref[...]
        dg_ref[...] = (dy * h_ref[...] * dgelu).astype(BF16)
        dh_ref[...] = dy * gelu

    spec = pl.BlockSpec((tt, D), lambda t: (t, 0))
    return _pcall(
        body, name=name, grid=(T // tt,),
        in_specs=[spec, spec, spec], out_specs=[spec, spec],
        out_shape=[_sds((T, D), BF16), _sds((T, D), F32)],
        compiler_params=_params("parallel"),
    )(gr, hs, dy)


def _lru_gate_bwd(gr, hs, lmb, conv_w, conv_b, w_rg, b_rg, w_ig, b_ig, lam, *, name):
    T, D = hs.shape
    nb = w_rg.shape[0]
    tt = _lru_time_tile(T)
    hb = tt // HALO
    nt = T // tt

    def body(x_ref, xp_ref, h_ref, hp_ref, l_ref, cw_ref, cb_ref, wr_ref, br_ref, wi_ref, bi_ref, lam_ref,
             du_ref, dwr_ref, dbr_ref, dwi_ref, dbi_ref, dlam_ref, xbuf, hbuf):
        t = pl.program_id(0)
        first = t == 0
        _fill_with_halo(xbuf, xp_ref, x_ref, first)
        _fill_with_halo(hbuf, hp_ref, h_ref, first)
        u = _conv_rows(xbuf, tt, cw_ref[...], cb_ref[...])
        lam_v = lam_ref[...]
        r, i, a, mult, c = _lru_gates(u, wr_ref, br_ref[...], wi_ref, bi_ref[...], lam_v)
        l = l_ref[...]
        h_prev = hbuf[pl.ds(HALO - 1, tt), :]
        dlog_a = l * h_prev * a - l * (i * u) * (a * a) / mult
        d_iu = l * mult
        dpre_r = (dlog_a * c) * (r * (1.0 - r))
        dpre_i = (d_iu * u) * (i * (1.0 - i))
        dpr_b = dpre_r.astype(BF16)
        dpi_b = dpre_i.astype(BF16)
        du_parts, dwr, dwi = [], [], []
        for n in range(nb):
            cs = slice(n * LANE, (n + 1) * LANE)
            ub = u[:, cs].astype(BF16)
            du_parts.append(
                lax.dot_general(dpr_b[:, cs], wr_ref[n], (((1,), (1,)), ((), ())), preferred_element_type=F32)
                + lax.dot_general(dpi_b[:, cs], wi_ref[n], (((1,), (1,)), ((), ())), preferred_element_type=F32))
            dwr.append(lax.dot_general(ub, dpr_b[:, cs], (((0,), (0,)), ((), ())), preferred_element_type=F32))
            dwi.append(lax.dot_general(ub, dpi_b[:, cs], (((0,), (0,)), ((), ())), preferred_element_type=F32))
        du_ref[...] = d_iu * i + jnp.concatenate(du_parts, axis=1)
        dbr = jnp.sum(dpre_r, axis=0, keepdims=True)
        dbi = jnp.sum(dpre_i, axis=0, keepdims=True)
        dc = jnp.sum(dlog_a * r, axis=0, keepdims=True)

        @pl.when(first)
        def _():
            for n in range(nb):
                dwr_ref[n] = dwr[n]
                dwi_ref[n] = dwi[n]
            dbr_ref[...] = dbr
            dbi_ref[...] = dbi
            dlam_ref[...] = dc

        @pl.when(t > 0)
        def _():
            for n in range(nb):
                dwr_ref[n] += dwr[n]
                dwi_ref[n] += dwi[n]
            dbr_ref[...] += dbr
            dbi_ref[...] += dbi
            dlam_ref[...] += dc

        @pl.when(t == nt - 1)
        def _():
            dlam_ref[...] = dlam_ref[...] * (LRU_C * _sigmoid(-lam_v))

    vec = pl.BlockSpec((1, D), lambda t: (0, 0))
    mat = pl.BlockSpec(w_rg.shape, lambda t: (0, 0, 0))
    blk = pl.BlockSpec((tt, D), lambda t: (t, 0))
    prev = pl.BlockSpec((HALO, D), lambda t: (jnp.maximum(t * hb - 1, 0), 0))
    return _pcall(
        body, name=name, grid=(nt,),
        in_specs=[pl.BlockSpec((tt, D), lambda t: (t, 1)),
                  pl.BlockSpec((HALO, D), lambda t: (jnp.maximum(t * hb - 1, 0), 1)),
                  blk, prev, blk,
                  pl.BlockSpec((4, D), lambda t: (0, 0)), vec, mat, vec, mat, vec, vec],
        out_specs=[blk, mat, vec, mat, vec, vec],
        out_shape=[_sds((T, D), F32), _sds(w_rg.shape, F32), _sds((1, D), F32),
                   _sds(w_rg.shape, F32), _sds((1, D), F32), _sds((1, D), F32)],
        scratch_shapes=[pltpu.VMEM((tt + HALO, D), F32), pltpu.VMEM((tt + HALO, D), F32)],
        compiler_params=_params("arbitrary"),
    )(gr, gr, hs, hs, lmb, conv_w, conv_b, w_rg, b_rg, w_ig, b_ig, lam)


def _lru_conv_bwd(gr, du, conv_w, *, name):
    T, D = du.shape
    tt = _lru_time_tile(T)
    hb = tt // HALO
    nt = T // tt

    def body(x_ref, xp_ref, du_ref, dun_ref, cw_ref, dx_ref, dcw_ref, dcb_ref, xbuf, dbuf):
        t = pl.program_id(0)
        _fill_with_halo(xbuf, xp_ref, x_ref, t == 0)
        du = du_ref[...]
        dbuf[pl.ds(0, tt), :] = du
        dbuf[pl.ds(tt, HALO), :] = jnp.where(t == nt - 1, 0.0, dun_ref[...])
        cw = cw_ref[...]
        dx = jnp.zeros((tt, D), F32)
        dcw = []
        for j in range(4):
            dx = dx + dbuf[pl.ds(3 - j, tt), :] * cw[j:j + 1, :]
            dcw.append(jnp.sum(du * xbuf[pl.ds(HALO - 3 + j, tt), :], axis=0, keepdims=True))
        dx_ref[...] = dx.astype(BF16)
        dcw = jnp.concatenate(dcw, axis=0)
        dcb = jnp.sum(du, axis=0, keepdims=True)

        @pl.when(t == 0)
        def _():
            dcw_ref[...] = dcw
            dcb_ref[...] = dcb

        @pl.when(t > 0)
        def _():
            dcw_ref[...] += dcw
            dcb_ref[...] += dcb

    blk = pl.BlockSpec((tt, D), lambda t: (t, 0))
    return _pcall(
        body, name=name, grid=(nt,),
        in_specs=[pl.BlockSpec((tt, D), lambda t: (t, 1)),
                  pl.BlockSpec((HALO, D), lambda t: (jnp.maximum(t * hb - 1, 0), 1)),
                  blk,
                  pl.BlockSpec((HALO, D), lambda t: (jnp.minimum((t + 1) * hb, T // HALO - 1), 0)),
                  pl.BlockSpec((4, D), lambda t: (0, 0))],
        out_specs=[blk, pl.BlockSpec((4, D), lambda t: (0, 0)), pl.BlockSpec((1, D), lambda t: (0, 0))],
        out_shape=[_sds((T, D), BF16), _sds((4, D), F32), _sds((1, D), F32)],
        scratch_shapes=[pltpu.VMEM((tt + HALO, D), F32), pltpu.VMEM((tt + HALO, D), F32)],
        compiler_params=_params("arbitrary"),
    )(gr, gr, du, du, conv_w)


def _loss_head(h, g, target, *, row_lo, row_hi, name):
    T, D = h.shape
    tm = _tile(T, 1056, 16)

    def body(h_ref, g_ref, t_ref, loss_ref, dh_ref, dhb_ref, dg_ref):
        i = pl.program_id(0)
        x = h_ref[...]
        g = g_ref[...]
        row = i * tm + lax.broadcasted_iota(jnp.int32, (tm, 1), 0)
        valid = jnp.logical_and(row >= row_lo, row < row_hi)
        rstd = _rstd(x)
        n = x * rstd
        err = jnp.where(valid, n * g - t_ref[...], 0.0)
        part = (0.5 / D) * jnp.sum(jnp.sum(err * err, axis=1, keepdims=True), axis=0, keepdims=True)
        dy = err * (1.0 / D)
        dn = dy * g
        dh = rstd * (dn - n * jnp.mean(dn * n, axis=-1, keepdims=True))
        dh_ref[...] = dh
        dhb_ref[...] = dh.astype(BF16)
        dg = jnp.sum(dy * n, axis=0, keepdims=True)

        @pl.when(i == 0)
        def _():
            loss_ref[...] = part
            dg_ref[...] = dg

        @pl.when(i > 0)
        def _():
            loss_ref[...] += part
            dg_ref[...] += dg

    blk = pl.BlockSpec((tm, D), lambda i: (i, 0))
    vec = pl.BlockSpec((1, D), lambda i: (0, 0))
    return _pcall(
        body, name=name, grid=(T // tm,),
        in_specs=[blk, vec, blk],
        out_specs=[pl.BlockSpec((1, 1), lambda i: (0, 0)), blk, blk, vec],
        out_shape=[_sds((1, 1), F32), _sds((T, D), F32), _sds((T, D), BF16), _sds((1, D), F32)],
        compiler_params=_params("arbitrary"),
    )(h, g, target)


def _adamw_math(w, g, m, v):
    c1 = 1.0 / (1.0 - ADAM_B1 ** ADAM_STEP)
    c2 = 1.0 / (1.0 - ADAM_B2 ** ADAM_STEP)
    m = ADAM_B1 * m + (1.0 - ADAM_B1) * g
    v = ADAM_B2 * v + (1.0 - ADAM_B2) * (g * g)
    delta = -ADAM_LR * ((m * c1) / (jnp.sqrt(v * c2) + ADAM_EPS) + ADAM_WD * w)
    return delta, m, v


def _adamw_halves(w, mine, theirs, m, v, c, *, name):
    _, R, C = w.shape
    tr = _tile(R, 256, SUBLANE)

    def body(c_ref, w_ref, a_ref, b_ref, m_ref, v_ref, g_ref, d_ref, nm_ref, nv_ref):
        g = jnp.where(pl.program_id(0) == c_ref[0], a_ref[...], b_ref[...])
        g_ref[...] = g
        d_ref[...], nm_ref[...], nv_ref[...] = _adamw_math(w_ref[...], g, m_ref[...], v_ref[...])

    full = pl.BlockSpec((None, tr, C), lambda h, i, c_ref: (h, i, 0))
    half = pl.BlockSpec((tr, C), lambda h, i, c_ref: (i, 0))
    return _pcall(
        body, name=name,
        grid_spec=pltpu.PrefetchScalarGridSpec(
            num_scalar_prefetch=1, grid=(2, R // tr),
            in_specs=[full, half, half, full, full], out_specs=[full] * 4),
        out_shape=[_sds((2, R, C), F32)] * 4,
        compiler_params=_params("parallel", "parallel"),
    )(c, w, mine, theirs, m, v)


def _adamw(w, g, m, v, *, name):
    R, C = w.shape
    tr = _tile(R, 512, SUBLANE)

    def body(w_ref, g_ref, m_ref, v_ref, d_ref, nm_ref, nv_ref):
        d_ref[...], nm_ref[...], nv_ref[...] = _adamw_math(w_ref[...], g_ref[...], m_ref[...], v_ref[...])

    blk = pl.BlockSpec((tr, C), lambda i: (i, 0))
    return _pcall(
        body, name=name, grid=(R // tr,),
        in_specs=[blk] * 4, out_specs=[blk] * 3,
        out_shape=[_sds((R, C), F32)] * 3,
        compiler_params=_params("parallel"),
    )(w, g, m, v)


ANY = pl.BlockSpec(memory_space=pl.ANY)


def _place():
    x, y, c = lax.axis_index("x"), lax.axis_index("y"), lax.axis_index("c")
    chips = [(1 - x, y), (x, 1 - y), (1 - x, 1 - y)]
    return x, y, c, chips


LOCAL_PIECES = 4


class _GatherChips:
    def __init__(self, vs):
        self.inputs = list(vs)
        n = self.n = len(vs)
        self.out_shape = [_sds((4,) + v.shape, v.dtype) for v in vs]
        self.scratch = [pltpu.SemaphoreType.DMA((6 * n,)), pltpu.SemaphoreType.DMA((6 * n,)),
                        pltpu.SemaphoreType.DMA((LOCAL_PIECES * n,))]

    def _copies(self, v_refs, o_refs, sems):
        send_sems, recv_sems, local_sems = sems
        x, y, c, chips = _place()
        me = 2 * x + y

        def copy(a, k, block, half, to, src=None):
            dst = o_refs[a].at[block, half]
            return pltpu.make_async_remote_copy(
                src_ref=dst if src is None else src, dst_ref=dst,
                send_sem=send_sems.at[6 * a + k], recv_sem=recv_sems.at[6 * a + k],
                device_id=to, device_id_type=MESH)

        ks = [(a, k, cx, cy) for a in range(self.n) for k, (cx, cy) in enumerate(chips)]

        def local():
            out = []
            for a in range(self.n):
                rows = self.inputs[a].shape[1] // (LOCAL_PIECES // 2)
                for p in range(LOCAL_PIECES):
                    h, r0 = p % 2, (p // 2) * rows
                    out.append(pltpu.make_async_copy(
                        v_refs[a].at[h, pl.ds(r0, rows)], o_refs[a].at[me, h, pl.ds(r0, rows)],
                        local_sems.at[LOCAL_PIECES * a + p]))
            return out

        return dict(
            first=lambda: [copy(a, k, me, c, (cx, cy, c), src=v_refs[a].at[c]) for a, k, cx, cy in ks],
            landed=lambda: [copy(a, k, 2 * cx + cy, c, (x, y, c)) for a, k, cx, cy in ks],
            passed=lambda: [copy(a, 3 + k, 2 * cx + cy, c, (x, y, 1 - c)) for a, k, cx, cy in ks],
            final=lambda: [copy(a, 3 + k, 2 * cx + cy, 1 - c, (x, y, c)) for a, k, cx, cy in ks],
            local=local)

    def start(self, v_refs, o_refs, sems):
        cps = self._copies(v_refs, o_refs, sems)
        for cp in cps["first"]() + cps["local"]():
            cp.start()

    def mid(self, v_refs, o_refs, sems):
        cps = self._copies(v_refs, o_refs, sems)
        for got, fwd in zip(cps["landed"](), cps["passed"]()):
            got.wait_recv()
            fwd.start()

    def finish(self, v_refs, o_refs, sems):
        cps = self._copies(v_refs, o_refs, sems)
        for cp in cps["final"]():
            cp.wait_recv()
        for cp in cps["first"]() + cps["passed"]():
            cp.wait_send()
        for cp in cps["local"]():
            cp.wait()


class _ExchangeBlocks:
    def __init__(self, ps):
        self.inputs = list(ps)
        n = self.n = len(ps)
        self.out_shape = [_sds((8,) + p.shape[2:], p.dtype) for p in ps]
        self.scratch = [pltpu.SemaphoreType.DMA((7 * n,)), pltpu.SemaphoreType.DMA((7 * n,))]

    def _copies(self, p_refs, o_refs, sems, incoming):
        send_sems, recv_sems = sems
        x, y, c, _ = _place()
        me = 4 * x + 2 * y + c
        out = []
        for a in range(self.n):
            for k in range(1, 8):
                px, py, pc = x ^ (k >> 2), y ^ ((k >> 1) & 1), c ^ (k & 1)
                out.append(pltpu.make_async_remote_copy(
                    src_ref=p_refs[a].at[2 * px + py, pc],
                    dst_ref=o_refs[a].at[4 * px + 2 * py + pc if incoming else me],
                    send_sem=send_sems.at[7 * a + k - 1], recv_sem=recv_sems.at[7 * a + k - 1],
                    device_id=(x, y, c) if incoming else (px, py, pc), device_id_type=MESH))
        return out

    def start(self, p_refs, o_refs, sems):
        for cp in self._copies(p_refs, o_refs, sems, False):
            cp.start()

    def mid(self, p_refs, o_refs, sems):
        pass

    def finish(self, p_refs, o_refs, sems):
        for cp in self._copies(p_refs, o_refs, sems, True):
            cp.wait_recv()
        for cp in self._copies(p_refs, o_refs, sems, False):
            cp.wait_send()


def _run_exchange(plan, *, name):
    n = plan.n

    def body(*refs):
        args = (refs[:n], refs[n:2 * n], refs[2 * n:])
        plan.start(*args)
        plan.mid(*args)
        plan.finish(*args)

    return _pcall(
        body, name=name, in_specs=[ANY] * n, out_specs=[ANY] * n,
        out_shape=plan.out_shape, scratch_shapes=plan.scratch,
    )(*plan.inputs)


def _send_sibling(rs, *, name):
    n = len(rs)

    def body(*refs):
        r_refs, o_refs = refs[:n], refs[n:2 * n]
        send_sems, recv_sems = refs[2 * n:]
        x, y, c, _ = _place()
        cps = [pltpu.make_async_remote_copy(
            src_ref=r_refs[a], dst_ref=o_refs[a], send_sem=send_sems.at[a], recv_sem=recv_sems.at[a],
            device_id=(x, y, 1 - c), device_id_type=MESH) for a in range(n)]
        for cp in cps:
            cp.start()
        for cp in cps:
            cp.wait()

    return _pcall(
        body, name=name, in_specs=[ANY] * n, out_specs=[ANY] * n,
        out_shape=[_sds(r.shape, r.dtype) for r in rs],
        scratch_shapes=[pltpu.SemaphoreType.DMA((n,)), pltpu.SemaphoreType.DMA((n,))],
    )(*rs)


def _add_devices(p, got, place, *, name):
    _, _, R, C = p.shape
    tr = _tile(R, 256, 16)

    def body(place_ref, p_ref, o_ref, out_ref):
        me = place_ref[2]
        own = p_ref[...].astype(F32)
        acc = jnp.where(me == 0, own, o_ref[0].astype(F32))
        for d in range(1, 8):
            acc = acc + jnp.where(me == d, own, o_ref[d].astype(F32))
        out_ref[...] = acc

    return _pcall(
        body, name=name,
        grid_spec=pltpu.PrefetchScalarGridSpec(
            num_scalar_prefetch=1, grid=(R // tr,),
            in_specs=[pl.BlockSpec((None, None, tr, C), lambda i, pr: (pr[0], pr[1], i, 0)),
                      pl.BlockSpec((8, tr, C), lambda i, pr: (0, i, 0))],
            out_specs=pl.BlockSpec((tr, C), lambda i, pr: (i, 0))),
        out_shape=_sds((R, C), F32),
        compiler_params=_params("parallel"),
    )(place, p, got)


def _round_up(n, m):
    return (n + m - 1) // m * m


def _f32_as_bf16(a):
    return lax.bitcast_convert_type(a.astype(F32), BF16).reshape(-1)


def _bf16_as_f32(a):
    return lax.bitcast_convert_type(a.reshape(-1, 2), F32)


def _by_chip_cols(a, cols):
    lead = a.shape[:-1]
    a = a.reshape(lead + (4, cols))
    return jnp.moveaxis(a, -2, 0).reshape(4, -1)


def kernel(x, meta_tokens, norm_mix, norm_mlp, sb_w_qkv, sb_w_o, lru_w_in, lru_conv_w, lru_conv_b, lru_w_rg, lru_b_rg, lru_w_ig, lru_b_ig, lru_lambda, lru_w_out, mlp_w_up, mlp_w_down, norm_final, loss_target, m_meta_tokens, m_norm_mix, m_norm_mlp, m_sb_w_qkv, m_sb_w_o, m_lru_w_in, m_lru_conv_w, m_lru_conv_b, m_lru_w_rg, m_lru_b_rg, m_lru_w_ig, m_lru_b_ig, m_lru_lambda, m_lru_w_out, m_mlp_w_up, m_mlp_w_down, m_norm_final, v_meta_tokens, v_norm_mix, v_norm_mlp, v_sb_w_qkv, v_sb_w_o, v_lru_w_in, v_lru_conv_w, v_lru_conv_b, v_lru_w_rg, v_lru_b_rg, v_lru_w_ig, v_lru_b_ig, v_lru_lambda, v_lru_w_out, v_mlp_w_up, v_mlp_w_down, v_norm_final):
    weights = dict(meta_tokens=meta_tokens, norm_mix=norm_mix, norm_mlp=norm_mlp, sb_w_qkv=sb_w_qkv,
                   sb_w_o=sb_w_o, lru_w_in=lru_w_in, lru_conv_w=lru_conv_w, lru_conv_b=lru_conv_b,
                   lru_w_rg=lru_w_rg, lru_b_rg=lru_b_rg, lru_w_ig=lru_w_ig, lru_b_ig=lru_b_ig,
                   lru_lambda=lru_lambda, lru_w_out=lru_w_out, mlp_w_up=mlp_w_up, mlp_w_down=mlp_w_down,
                   norm_final=norm_final)
    m_in = dict(meta_tokens=m_meta_tokens, norm_mix=m_norm_mix, norm_mlp=m_norm_mlp, sb_w_qkv=m_sb_w_qkv,
                sb_w_o=m_sb_w_o, lru_w_in=m_lru_w_in, lru_conv_w=m_lru_conv_w, lru_conv_b=m_lru_conv_b,
                lru_w_rg=m_lru_w_rg, lru_b_rg=m_lru_b_rg, lru_w_ig=m_lru_w_ig, lru_b_ig=m_lru_b_ig,
                lru_lambda=m_lru_lambda, lru_w_out=m_lru_w_out, mlp_w_up=m_mlp_w_up,
                mlp_w_down=m_mlp_w_down, norm_final=m_norm_final)
    v_in = dict(meta_tokens=v_meta_tokens, norm_mix=v_norm_mix, norm_mlp=v_norm_mlp, sb_w_qkv=v_sb_w_qkv,
                sb_w_o=v_sb_w_o, lru_w_in=v_lru_w_in, lru_conv_w=v_lru_conv_w, lru_conv_b=v_lru_conv_b,
                lru_w_rg=v_lru_w_rg, lru_b_rg=v_lru_b_rg, lru_w_ig=v_lru_w_ig, lru_b_ig=v_lru_b_ig,
                lru_lambda=v_lru_lambda, lru_w_out=v_lru_w_out, mlp_w_up=v_mlp_w_up,
                mlp_w_down=v_mlp_w_down, norm_final=v_norm_final)
    names = list(weights)

    seq, D = x.shape[1], x.shape[2]
    n_meta = meta_tokens.shape[0]
    Dq = D // 4
    T = _round_up(n_meta + seq, ATT_BLOCK)
    nb = lru_w_rg.shape[1]
    F = mlp_w_up.shape[2]
    depth = mlp_w_up.shape[0]
    my_x, my_y, my_c = lax.axis_index("x"), lax.axis_index("y"), lax.axis_index("c")
    c_arr = jnp.reshape(my_c, (1,)).astype(jnp.int32)

    assert depth == 2

    def halves(a):
        return a.astype(BF16).reshape(2, a.shape[0] // 2, a.shape[1])

    small = [meta_tokens, lru_conv_w[0], lru_conv_b, lru_b_rg, lru_b_ig, lru_lambda]
    sparts = [_f32_as_bf16(s) for s in small]
    sizes = [p.shape[0] for p in sparts]
    total = _round_up(sum(sizes), 2 * 32 * LANE)
    sflat =jnp.concatenate(sparts + [jnp.zeros((total - sum(sizes),), BF16)]).reshape(2, -1, LANE)
    gq, gsm = _run_exchange(_GatherChips([halves(sb_w_qkv[0]), sflat]), name="gather_first")
    gather_rest = _GatherChips([halves(sb_w_o[0]), halves(lru_w_in[0]), halves(lru_w_out[0]),
                                mlp_w_up.astype(BF16), mlp_w_down.astype(BF16)])
    w_qkv = gq.reshape(4, D, 3 * Dq)
    gsm = gsm.reshape(4, total)
    offs = [sum(sizes[:k]) for k in range(len(sizes))]
    sm = [_bf16_as_f32(gsm[:, o:o + s]) for o, s in zip(offs, sizes)]
    meta_full = jnp.moveaxis(sm[0].reshape(4, n_meta, Dq), 0, 1).reshape(n_meta, D)
    conv_w = jnp.moveaxis(sm[1].reshape(4, 4, Dq), 0, 1).reshape(4, D)
    conv_b, b_rg, b_ig, lam = [s.reshape(1, D) for s in sm[2:6]]
    w_rg = lru_w_rg[0].astype(BF16)
    w_ig = lru_w_ig[0].astype(BF16)
    g_mix = [norm_mix[l].reshape(1, D) for l in range(depth)]
    g_mlp = [norm_mlp[l].reshape(1, D) for l in range(depth)]
    g_fin = norm_final.reshape(1, D)

    pad_rows = T - n_meta - seq
    h0 = jnp.concatenate([meta_full, x[0], jnp.zeros((pad_rows, D), F32)], axis=0)
    target = jnp.concatenate([jnp.zeros((n_meta, D), F32), loss_target[0], jnp.zeros((pad_rows, D), F32)], axis=0)

    hn0, qkv = _norm_mm(h0, g_mix[0], w_qkv, out_dtype=BF16, name="qkv_proj")
    att, go, gi, gout, w_up, w_down = _attn_fwd(qkv, name="attn_fwd", plan=gather_rest)
    w_o = go.reshape(D, D)
    w_in = gi.reshape(4, D, 2 * Dq)
    w_out = gout.reshape(D, D)
    h1 = _mm_res(att, w_o, h0, name="attn_out")
    h2, hnm0, up0 = _mlp_fwd(h1, g_mlp[0], w_up, w_down, layer=0, name="mlp0_fwd")
    hn1, gr = _norm_mm(h2, g_mix[1], w_in, out_dtype=F32, name="lru_in")
    a_t, b_t = _lru_pre(gr, conv_w, conv_b, w_rg, b_rg, w_ig, b_ig, lam, name="lru_pre")
    hs = _lru_scan(a_t, b_t, reverse=False, name="lru_scan")
    y = _lru_out(gr, hs, name="lru_gate_out")
    h3 = _mm_res(y, w_out, h2, name="lru_out")
    h4, hnm1, up1 = _mlp_fwd(h3, g_mlp[1], w_up, w_down, layer=1, name="mlp1_fwd")
    loss, dh4, dh4b, dg_fin = _loss_head(h4, g_fin, target, row_lo=n_meta, row_hi=n_meta + seq, name="loss_head")

    dup1, dh3, dh3b, dg_mlp1 = _mlp_bwd(dh4, h3, g_mlp[1], up1, w_up, w_down, layer=1, name="mlp1_bwd")
    dw_up = _mm_tn(hnm1, dup1, shards=4, relu2=False, slot=1, name="mlp1_dwup")
    dw_down = _mm_tn(up1, dh4b, shards=1, relu2=True, slot=1, row_shards=4, name="mlp1_dwdown")
    dy = _mm_nt(dh3b, w_out, out_dtype=F32, name="lru_out_bwd")
    dw_out = _mm_tn(y, dh3b, shards=1, relu2=False, name="lru_dwout")
    dgate, dhy = _lru_out_bwd(gr, hs, dy, name="lru_gate_out_bwd")
    lmb = _lru_scan(a_t, dhy, reverse=True, name="lru_scan_bwd")
    du, dw_rg, db_rg, dw_ig, db_ig, dlam = _lru_gate_bwd(
        gr, hs, lmb, conv_w, conv_b, w_rg, b_rg, w_ig, b_ig, lam, name="lru_gate_bwd")
    drec, dconv_w, dconv_b = _lru_conv_bwd(gr, du, conv_w, name="lru_conv_bwd")
    dgr = jnp.concatenate([dgate, drec], axis=1)
    dh2, dh2b, dg_mix1 = _mm_nt_normbwd(dgr, w_in, h2, g_mix[1], dh3, name="lru_in_bwd")
    dw_in = _mm_tn(hn1, dgr, shards=4, relu2=False, name="lru_dwin")
    dup0, dh1, dh1b, dg_mlp0 = _mlp_bwd(dh2, h1, g_mlp[0], up0, w_up, w_down, layer=0, name="mlp0_bwd")
    dw_up = _mm_tn(hnm0, dup0, shards=4, relu2=False, slot=0, into=dw_up, name="mlp0_dwup")
    dw_down = _mm_tn(up0, dh2b, shards=1, relu2=True, slot=0, into=dw_down, row_shards=4, name="mlp0_dwdown")
    datt = _mm_nt(dh1b, w_o, out_dtype=BF16, name="attn_out_bwd")
    dw_o = _mm_tn(att, dh1b, shards=1, relu2=False, name="attn_dwo")
    def halves_of(d, rows):
        return d.reshape(4, 2, rows // 2, d.shape[-1])

    early = [halves_of(dw_o, Dq), halves_of(dw_in, D), halves_of(dw_out, Dq), dw_up, dw_down]
    dq, dk, dv, *got_early = _attn_bwd(qkv, datt, name="attn_bwd", plan=_ExchangeBlocks(early))
    dqkv = jnp.concatenate([dq, dk, dv], axis=1)
    dh0, _, dg_mix0 = _mm_nt_normbwd(dqkv, w_qkv, h0, g_mix[0], dh1, name="qkv_bwd")
    dw_qkv = _mm_tn(hn0, dqkv, shards=4, relu2=False, name="attn_dwqkv")
    grad_x = dh0[n_meta:n_meta + seq][None]
    dmeta = dh0[:n_meta]

    large = ["sb_w_o", "lru_w_in", "lru_w_out", "mlp_w_up", "mlp_w_down", "sb_w_qkv"]
    sharded =[_by_chip_cols(dmeta, Dq), _by_chip_cols(dconv_w, Dq), dconv_b.reshape(4, Dq),
               db_rg.reshape(4, Dq), db_ig.reshape(4, Dq), dlam.reshape(4, Dq)]
    repl = [jnp.concatenate([dg_mix0, dg_mix1], axis=0).reshape(-1),
            jnp.concatenate([dg_mlp0, dg_mlp1], axis=0).reshape(-1),
            dg_fin.reshape(-1), dw_rg.reshape(-1), dw_ig.reshape(-1)]
    rsizes = [r.shape[0] for r in repl]
    rtotal = _round_up(sum(rsizes), 4 * 2 * 16 * LANE)
    rflat = jnp.concatenate(repl + [jnp.zeros((rtotal - sum(rsizes),), F32)]).reshape(4, rtotal // 4)
    gsizes = [s.shape[1] for s in sharded] + [rtotal // 4]
    gtotal = _round_up(sum(gsizes), 2 * 16 * LANE)
    tail = jnp.concatenate(sharded + [rflat, jnp.zeros((4, gtotal - sum(gsizes)), F32)], axis=1)
    late = [halves_of(dw_qkv, D), tail.reshape(4, 2, -1, LANE)]
    got_late = _run_exchange(_ExchangeBlocks(late), name="reduce_late")
    place = jnp.stack([2 * my_x + my_y, my_c, 4 * my_x + 2 * my_y + my_c]).astype(jnp.int32)
    mine = [_add_devices(p, o, place, name="reduce_add_" + t)
            for p, o, t in zip(early + late, list(got_early) + list(got_late), large + ["tail"])]
    theirs = _send_sibling(mine, name="reduce_join")

    grads, delta, new_m, new_v = {}, {}, {}, {}
    for n, a, b in zip(large, mine, theirs):
        shp = weights[n].shape
        view = (2,) + a.shape
        g, d, nm, nv = _adamw_halves(weights[n].reshape(view), a, b, m_in[n].reshape(view),
                                     v_in[n].reshape(view), c_arr, name="adamw_" + n)
        grads[n], delta[n], new_m[n], new_v[n] = g.reshape(shp), d.reshape(shp), nm.reshape(shp), nv.reshape(shp)

    lo = jnp.where(my_c == 0, mine[-1], theirs[-1])
    hi = jnp.where(my_c == 0, theirs[-1], mine[-1])
    gshard = jnp.concatenate([lo, hi], axis=0).reshape(gtotal)
    goffs = [sum(gsizes[:k]) for k in range(len(gsizes))]
    gp = [gshard[o:o + s] for o, s in zip(goffs, gsizes)]
    rfull = _run_exchange(_GatherChips([gp[-1].reshape(2, -1, LANE)]), name="gather_replicated")[0].reshape(rtotal)
    roffs = [sum(rsizes[:k]) for k in range(len(rsizes))]
    rp = [rfull[o:o + s] for o, s in zip(roffs, rsizes)]
    grads.update(meta_tokens=gp[0], lru_conv_w=gp[1], lru_conv_b=gp[2], lru_b_rg=gp[3], lru_b_ig=gp[4],
                 lru_lambda=gp[5], norm_mix=rp[0], norm_mlp=rp[1], norm_final=rp[2], lru_w_rg=rp[3],
                 lru_w_ig=rp[4])
    grads = {n: grads[n].reshape(weights[n].shape) for n in names}
    rest = [n for n in names if n not in large]
    ssz = [weights[n].size for n in rest]
    small_cols = 8 * LANE
    stotal = _round_up(sum(ssz), SUBLANE * small_cols)

    def pack(src):
        return jnp.concatenate([src[n].reshape(-1) for n in rest]
                               + [jnp.ones((stotal - sum(ssz),), F32)]).reshape(-1, small_cols)

    d, nm, nv = _adamw(pack(weights), pack(grads), pack(m_in), pack(v_in), name="adamw_small")
    soffs = [sum(ssz[:k]) for k in range(len(ssz))]
    for n, o, s in zip(rest, soffs, ssz):
        shp = weights[n].shape
        delta[n] = d.reshape(-1)[o:o + s].reshape(shp)
        new_m[n] = nm.reshape(-1)[o:o + s].reshape(shp)
        new_v[n] = nv.reshape(-1)[o:o + s].reshape(shp)

    loss = lax.psum(loss[0, 0], ("x", "y", "c"))
    return (loss, grad_x, *[grads[n] for n in names], *[delta[n] for n in names],
            *[new_m[n] for n in names], *[new_v[n] for n in names])
```

```python
import functools

import jax
import jax.numpy as jnp
from jax import lax
from jax.experimental import pallas as pl
from jax.experimental.pallas import tpu as pltpu

F32 = jnp.float32
BF16 = jnp.bfloat16
MESH = pl.DeviceIdType.MESH

EPS = 1e-6
HEAD_DIM = 64
LANE = 128
SUBLANE = 8
LRU_C = 8.0
VMEM_LIMIT = 56 * 1024 * 1024

ADAM_LR = 0.001
ADAM_B1 = 0.9
ADAM_B2 = 0.999
ADAM_EPS = 1e-08
ADAM_WD = 0.01
ADAM_STEP = 10


def _pcall(body, **kw):
    return pl.pallas_call(body, **kw)


def _params(*sem):
    return pltpu.CompilerParams(dimension_semantics=sem, vmem_limit_bytes=VMEM_LIMIT)


def _tile(n, pref, align):
    best = None
    for t in range(align, min(n, pref) + 1, align):
        if n % t == 0:
            best = t
    return n if best is None else best


def _sds(shape, dtype):
    return jax.ShapeDtypeStruct(shape, dtype)


def _rstd(x):
    return lax.rsqrt(jnp.mean(x * x, axis=-1, keepdims=True) + EPS)


def _norm_bwd(x, g, dy):
    rstd = _rstd(x)
    n = x * rstd
    dn = dy * g
    dx = rstd * (dn - n * jnp.mean(dn * n, axis=-1, keepdims=True))
    dg = jnp.sum(dy * n, axis=0, keepdims=True)
    return dx, dg


def _softplus_parts(z):
    l1p = jnp.log(1.0 + jnp.exp(-jnp.abs(z)))
    return jnp.maximum(z, 0.0) + l1p, jnp.minimum(z, 0.0) - l1p


def _sigmoid(x):
    return 1.0 / (1.0 + jnp.exp(-x))


def _gelu_parts(x):
    k = 0.7978845608028654
    inner = k * (x + 0.044715 * (x * x * x))
    t = jnp.tanh(inner)
    gelu = 0.5 * x * (1.0 + t)
    dgelu = 0.5 * (1.0 + t) + 0.5 * x * (1.0 - t * t) * (k * (1.0 + 3.0 * 0.044715 * (x * x)))
    return gelu, dgelu


def _norm_mm(h, g, w, *, out_dtype, name):
    T, D = h.shape
    S, _, n = w.shape
    tm = _tile(T, 1056, 16)
    tn = _tile(n, 768, LANE)
    nj = n // tn

    def body(h_ref, g_ref, w_ref, hn_ref, o_ref):
        @pl.when(pl.program_id(1) == 0)
        def _():
            x = h_ref[...]
            hn_ref[...] = (x * _rstd(x) * g_ref[...]).astype(BF16)

        o_ref[...] = jnp.dot(hn_ref[...], w_ref[...], preferred_element_type=F32).astype(out_dtype)

    return _pcall(
        body, name=name, grid=(T // tm, S * nj),
        in_specs=[pl.BlockSpec((tm, D), lambda i, j: (i, 0)),
                  pl.BlockSpec((1, D), lambda i, j: (0, 0)),
                  pl.BlockSpec((None, D, tn), lambda i, j: (j // nj, 0, j % nj))],
        out_specs=[pl.BlockSpec((tm, D), lambda i, j: (i, 0)),
                   pl.BlockSpec((tm, tn), lambda i, j: (i, j))],
        out_shape=[_sds((T, D), BF16), _sds((T, S * n), out_dtype)],
        compiler_params=_params("parallel", "arbitrary"),
    )(h, g, w)


def _mm_res(a, w, res, *, name):
    T, K = a.shape
    N = w.shape[1]
    tm = _tile(T, 1056, 16)

    def body(a_ref, w_ref, r_ref, o_ref):
        o_ref[...] = r_ref[...] + jnp.dot(a_ref[...], w_ref[...], preferred_element_type=F32)

    return _pcall(
        body, name=name, grid=(T // tm,),
        in_specs=[pl.BlockSpec((tm, K), lambda i: (i, 0)),
                  pl.BlockSpec((K, N), lambda i: (0, 0)),
                  pl.BlockSpec((tm, N), lambda i: (i, 0))],
        out_specs=pl.BlockSpec((tm, N), lambda i: (i, 0)),
        out_shape=_sds((T, N), F32),
        compiler_params=_params("parallel"),
    )(a, w, res)


def _mm_nt(a, w, *, out_dtype, name):
    T, N = a.shape
    K = w.shape[0]
    tm = _tile(T, 1056, 16)

    def body(a_ref, w_ref, o_ref):
        o_ref[...] = lax.dot_general(a_ref[...], w_ref[...], (((1,), (1,)), ((), ())),
                                     preferred_element_type=F32).astype(out_dtype)

    return _pcall(
        body, name=name, grid=(T // tm,),
        in_specs=[pl.BlockSpec((tm, N), lambda i: (i, 0)),
                  pl.BlockSpec((K, N), lambda i: (0, 0))],
        out_specs=pl.BlockSpec((tm, K), lambda i: (i, 0)),
        out_shape=_sds((T, K), out_dtype),
        compiler_params=_params("parallel"),
    )(a, w)


def _mm_tn(a, b, *, shards, relu2, name, slot=None, into=None, row_shards=0, out_dtype=BF16):
    T, Ka = a.shape
    Nb = b.shape[1]
    n = Nb // shards
    tka = _tile(Ka, 512, LANE)
    tnb = _tile(n, 512, LANE)
    nj = n // tnb

    def body(a_ref, b_ref, *rest):
        o_ref = rest[-1]
        av = a_ref[...]
        if relu2:
            r = jnp.maximum(av, 0)
            av = r * r
        o_ref[...] = lax.dot_general(av, b_ref[...], (((0,), (0,)), ((), ())),
                                     preferred_element_type=F32).astype(out_dtype)

    in_specs = [pl.BlockSpec((T, tka), lambda i, j: (0, i)),
                pl.BlockSpec((T, tnb), lambda i, j: (0, j))]
    args = [a, b]
    aliases = {}
    if slot is None:
        out_spec = pl.BlockSpec((None, tka, tnb), lambda i, j: (j // nj, i, j % nj))
        out_shape = _sds((shards, Ka, n), out_dtype)
    else:
        if row_shards:
            ni = Ka // row_shards // tka
            out_spec = pl.BlockSpec((None, None, tka, tnb), lambda i, j: (i // ni, slot, i % ni, j))
            out_shape = _sds((row_shards, 2, Ka // row_shards, n), out_dtype)
        else:
            out_spec = pl.BlockSpec((None, None, tka, tnb), lambda i, j: (j // nj, slot, i, j % nj))
            out_shape = _sds((shards, 2, Ka, n), out_dtype)
        if into is not None:
            in_specs.append(pl.BlockSpec(memory_space=pl.ANY))
            args.append(into)
            aliases = {2: 0}
    return _pcall(
        body, name=name, grid=(Ka // tka, shards * nj),
        in_specs=in_specs, out_specs=out_spec, out_shape=out_shape,
        input_output_aliases=aliases,
        compiler_params=_params("parallel", "parallel"),
    )(*args)


def _mm_nt_normbwd(dy, w, h, g, dres, *, name):
    T, D = h.shape
    S, _, n = w.shape
    tm = _tile(T, 528, 16)

    def body(dy_ref, w_ref, h_ref, g_ref, dr_ref, dh_ref, dhb_ref, dg_ref, acc_ref):
        i, s = pl.program_id(0), pl.program_id(1)
        part = lax.dot_general(dy_ref[...], w_ref[...], (((1,), (1,)), ((), ())),
                               preferred_element_type=F32)

        @pl.when(s == 0)
        def _():
            acc_ref[...] = part

        @pl.when(s > 0)
        def _():
            acc_ref[...] += part

        @pl.when(s == S - 1)
        def _():
            dx, dg = _norm_bwd(h_ref[...], g_ref[...], acc_ref[...])
            dh = dr_ref[...] + dx
            dh_ref[...] = dh
            dhb_ref[...] = dh.astype(BF16)

            @pl.when(i == 0)
            def _():
                dg_ref[...] = dg

            @pl.when(i > 0)
            def _():
                dg_ref[...] += dg

    return _pcall(
        body, name=name, grid=(T // tm, S),
        in_specs=[pl.BlockSpec((tm, n), lambda i, s: (i, s)),
                  pl.BlockSpec((None, D, n), lambda i, s: (s, 0, 0)),
                  pl.BlockSpec((tm, D), lambda i, s: (i, 0)),
                  pl.BlockSpec((1, D), lambda i, s: (0, 0)),
                  pl.BlockSpec((tm, D), lambda i, s: (i, 0))],
        out_specs=[pl.BlockSpec((tm, D), lambda i, s: (i, 0)),
                   pl.BlockSpec((tm, D), lambda i, s: (i, 0)),
                   pl.BlockSpec((1, D), lambda i, s: (0, 0))],
        out_shape=[_sds((T, D), F32), _sds((T, D), BF16), _sds((1, D), F32)],
        scratch_shapes=[pltpu.VMEM((tm, D), F32)],
        compiler_params=_params("arbitrary", "arbitrary"),
    )(dy, w, h, g, dres)


def _row_chains(tm):
    first = (tm // 2 + 15) // 16 * 16
    return [slice(0, first), slice(first, tm)] if 0 < first < tm else [slice(0, tm)]


def _mlp_fwd(h, g, w_up, w_down, *, layer, name):
    T, D = h.shape
    S, _, _, n = w_up.shape
    tm = _tile(T, 528, 16)
    tf = _tile(n, 1024, LANE)
    nj = n // tf
    nf = S * nj
    chains = _row_chains(tm)

    def body(h_ref, g_ref, wu_ref, wd_ref, o_ref, hn_ref, up_ref, acc_ref):
        f = pl.program_id(1)

        @pl.when(f == 0)
        def _():
            x = h_ref[...]
            hn_ref[...] = (x * _rstd(x) * g_ref[...]).astype(BF16)

        parts = []
        for rows in chains:
            up = jnp.dot(hn_ref[rows, :], wu_ref[...], preferred_element_type=F32)
            up_ref[rows, :] = up.astype(BF16)
            r = jnp.maximum(up, 0.0)
            parts.append(jnp.dot((r * r).astype(BF16), wd_ref[...], preferred_element_type=F32))
        part = jnp.concatenate(parts, axis=0)

        @pl.when(f == 0)
        def _():
            acc_ref[...] = part

        @pl.when(f > 0)
        def _():
            acc_ref[...] += part

        @pl.when(f == nf - 1)
        def _():
            o_ref[...] = h_ref[...] + acc_ref[...]

    return _pcall(
        body, name=name, grid=(T // tm, nf),
        in_specs=[pl.BlockSpec((tm, D), lambda i, f: (i, 0)),
                  pl.BlockSpec((1, D), lambda i, f: (0, 0)),
                  pl.BlockSpec((None, None, D, tf), lambda i, f: (f // nj, layer, 0, f % nj)),
                  pl.BlockSpec((None, None, tf, D), lambda i, f: (f // nj, layer, f % nj, 0))],
        out_specs=[pl.BlockSpec((tm, D), lambda i, f: (i, 0)),
                   pl.BlockSpec((tm, D), lambda i, f: (i, 0)),
                   pl.BlockSpec((tm, tf), lambda i, f: (i, f))],
        out_shape=[_sds((T, D), F32), _sds((T, D), BF16), _sds((T, S * n), BF16)],
        scratch_shapes=[pltpu.VMEM((tm, D), F32)],
        compiler_params=_params("parallel", "arbitrary"),
    )(h, g, w_up, w_down)


def _mlp_bwd(dy, h, g, up, w_up, w_down, *, layer, name):
    T, D = h.shape
    S, _, _, n = w_up.shape
    tm = _tile(T, 528, 16)
    tf = _tile(n, 1024, LANE)
    nj = n // tf
    nf = S * nj
    chains = _row_chains(tm)

    def body(dy_ref, h_ref, g_ref, up_ref, wu_ref, wd_ref, dup_ref, dh_ref, dhb_ref, dg_ref,
             dyb_ref, acc_ref):
        i, f = pl.program_id(0), pl.program_id(1)

        @pl.when(f == 0)
        def _():
            dyb_ref[...] = dy_ref[...].astype(BF16)

        parts = []
        for rows in chains:
            dact = lax.dot_general(dyb_ref[rows, :], wd_ref[...], (((1,), (1,)), ((), ())),
                                   preferred_element_type=F32)
            r = jnp.maximum(up_ref[rows, :].astype(F32), 0.0)
            dup = (dact * (2.0 * r)).astype(BF16)
            dup_ref[rows, :] = dup
            parts.append(lax.dot_general(dup, wu_ref[...], (((1,), (1,)), ((), ())),
                                         preferred_element_type=F32))
        part = jnp.concatenate(parts, axis=0)

        @pl.when(f == 0)
        def _():
            acc_ref[...] = part

        @pl.when(f > 0)
        def _():
            acc_ref[...] += part

        @pl.when(f == nf - 1)
        def _():
            dx, dg = _norm_bwd(h_ref[...], g_ref[...], acc_ref[...])
            dh = dy_ref[...] + dx
            dh_ref[...] = dh
            dhb_ref[...] = dh.astype(BF16)

            @pl.when(i == 0)
            def _():
                dg_ref[...] = dg

            @pl.when(i > 0)
            def _():
                dg_ref[...] += dg

    return _pcall(
        body, name=name, grid=(T // tm, nf),
        in_specs=[pl.BlockSpec((tm, D), lambda i, f: (i, 0)),
                  pl.BlockSpec((tm, D), lambda i, f: (i, 0)),
                  pl.BlockSpec((1, D), lambda i, f: (0, 0)),
                  pl.BlockSpec((tm, tf), lambda i, f: (i, f)),
                  pl.BlockSpec((None, None, D, tf), lambda i, f: (f // nj, layer, 0, f % nj)),
                  pl.BlockSpec((None, None, tf, D), lambda i, f: (f // nj, layer, f % nj, 0))],
        out_specs=[pl.BlockSpec((tm, tf), lambda i, f: (i, f)),
                   pl.BlockSpec((tm, D), lambda i, f: (i, 0)),
                   pl.BlockSpec((tm, D), lambda i, f: (i, 0)),
                   pl.BlockSpec((1, D), lambda i, f: (0, 0))],
        out_shape=[_sds((T, S * n), BF16), _sds((T, D), F32), _sds((T, D), BF16), _sds((1, D), F32)],
        scratch_shapes=[pltpu.VMEM((tm, D), BF16), pltpu.VMEM((tm, D), F32)],
        compiler_params=_params("arbitrary", "arbitrary"),
    )(dy, h, g, up, w_up, w_down)


ATT_BLOCK = 128
ATT_HEADS = 4


def _attn_tile(T):
    for w in (3 * ATT_BLOCK, 2 * ATT_BLOCK):
        if T % w == 0:
            return w
    return ATT_BLOCK


def _tri(strict_lower, value):
    B = ATT_BLOCK
    r = lax.broadcasted_iota(jnp.int32, (2 * B, B), 0)
    r = jnp.where(r >= B, r - B, r)
    c = lax.broadcasted_iota(jnp.int32, (2 * B, B), 1)
    m = (r > c) if strict_lower else (r < c)
    return jnp.where(m, value, 0.0).astype(BF16)


def _split_dot(x, tri):
    hi = lax.bitcast_convert_type(lax.bitcast_convert_type(x, jnp.uint32) & jnp.uint32(0xFFFF0000), F32)
    lo = x - hi
    return jnp.dot(jnp.concatenate([hi.astype(BF16), lo.astype(BF16)], axis=1), tri,
                   preferred_element_type=F32)


def _causal_mask(W):
    r = lax.broadcasted_iota(jnp.int32, (W, W), 0)
    c = lax.broadcasted_iota(jnp.int32, (W, W), 1)
    return c < r


def _attn_scores(qs, ks, carry, tri_neg, masked, want_sig):
    W = qs.shape[0]
    B = ATT_BLOCK
    z = lax.dot_general(qs, ks, (((1,), (1,)), ((), ())), preferred_element_type=F32)
    minus_abs = lax.bitcast_convert_type(
        lax.bitcast_convert_type(z, jnp.uint32) | jnp.uint32(0x80000000), F32)
    sp = jnp.maximum(z, 0.0) + jnp.log(1.0 + jnp.exp(minus_abs))
    logsig = z - sp
    if masked:
        causal = _causal_mask(W)
        sp = jnp.where(causal, sp, 0.0)
    afters = []
    for b in reversed(range(W // B)):
        blk = sp[:, b * B:(b + 1) * B]
        within = _split_dot(blk, tri_neg)
        afters.append(within + carry)
        carry = carry + (within[:, 0:1] - blk[:, 0:1])
    after = jnp.concatenate(afters[::-1], axis=1)
    w = jnp.exp(logsig + after)
    if masked:
        w = jnp.where(causal, w, 0.0)
    return w, (jnp.exp(logsig) if want_sig else None), carry


def _ride_along(plan, npairs, nq):
    if plan is None:
        return 0, (lambda refs: None), (lambda refs: None)
    nx = plan.n

    def before(refs):
        p, i = pl.program_id(0), pl.program_id(1)

        @pl.when(jnp.logical_and(p == 0, i == 0))
        def _():
            plan.start(*refs)

        @pl.when(jnp.logical_and(p == npairs // 2, i == 0))
        def _():
            plan.mid(*refs)

    def after(refs):
        p, i = pl.program_id(0), pl.program_id(1)

        @pl.when(jnp.logical_and(p == npairs - 1, i == nq - 1))
        def _():
            plan.finish(*refs)

    return nx, before, after


def _saved_tile(i, j):
    return i * (i + 1) // 2 + j


def _attn_fwd(qkv, *, name, plan=None):
    T = qkv.shape[0]
    D = qkv.shape[1] // 3
    W = _attn_tile(T)
    H = ATT_HEADS
    lanes = H * HEAD_DIM
    ngroups = D // lanes
    nq = T // W
    ntri = nq * (nq + 1) // 2
    scale = HEAD_DIM ** -0.5
    nx, before, after = _ride_along(plan, ngroups, nq)

    def body(*refs):
        q_ref, k_ref, v_ref = refs[:3]
        o_ref, wsv_ref, ssv_ref = refs[3 + nx:6 + nx]
        stage_w, stage_s, stage_sems = refs[6 + 2 * nx:9 + 2 * nx]
        ride = (refs[3:3 + nx], refs[6 + nx:6 + 2 * nx], refs[9 + 2 * nx:])
        before(ride)
        p, i = pl.program_id(0), pl.program_id(1)
        tri = _tri(True, -1.0)
        heads = [slice(hh * HEAD_DIM, (hh + 1) * HEAD_DIM) for hh in range(H)]
        qs = [q_ref[:, cols] * scale for cols in heads]

        def save(slot, j):
            dst = _saved_tile(i, j)
            return [pltpu.make_async_copy(stage.at[slot], sv.at[pl.ds(p * H, H), dst], stage_sems.at[slot, a])
                    for a, (sv, stage) in enumerate(((wsv_ref, stage_w), (ssv_ref, stage_s)))]

        def tile(t, state, masked):
            j = i - t
            rows = pl.ds(pl.multiple_of(j * W, W), W)
            slot = t % 2
            if not masked:
                @pl.when(t >= 2)
                def _():
                    for cp in save(slot, j):
                        cp.wait()
            out = []
            for hh, (cols, q, (carry, acc)) in enumerate(zip(heads, qs, state)):
                w, sig, carry = _attn_scores(q, k_ref[rows, cols], carry, tri, masked, True)
                wb = w.astype(BF16)
                stage_w[slot, hh] = wb
                stage_s[slot, hh] = sig.astype(BF16)
                acc = acc + jnp.dot(wb, v_ref[rows, cols], preferred_element_type=F32)
                out.append((carry, acc))
            for cp in save(slot, j):
                cp.start()
            return tuple(out)

        zero = (jnp.zeros((W, 1), F32), jnp.zeros((W, HEAD_DIM), F32))
        state = tile(0, (zero,) * H, True)
        state = lax.fori_loop(1, i + 1, lambda t, st: tile(t, st, False), state)
        for cols, (_, acc) in zip(heads, state):
            o_ref[:, cols] = acc.astype(BF16)
        for cp in save(i % 2, 0):
            cp.wait()

        @pl.when(i >= 1)
        def _():
            for cp in save((i + 1) % 2, 0):
                cp.wait()

        after(ride)

    extra = plan.inputs if plan else []
    saved = _sds((D // HEAD_DIM, ntri, W, W), BF16)
    return _pcall(
        body, name=name, grid=(ngroups, nq),
        in_specs=[pl.BlockSpec((W, lanes), lambda p, i: (i, p)),
                  pl.BlockSpec((T, lanes), lambda p, i: (0, ngroups + p)),
                  pl.BlockSpec((T, lanes), lambda p, i: (0, 2 * ngroups + p))] + [ANY] * nx,
        out_specs=[pl.BlockSpec((W, lanes), lambda p, i: (i, p)), ANY, ANY] + [ANY] * nx,
        out_shape=[_sds((T, D), BF16), saved, saved] + (plan.out_shape if plan else []),
        scratch_shapes=[pltpu.VMEM((2, H, W, W), BF16), pltpu.VMEM((2, H, W, W), BF16),
                        pltpu.SemaphoreType.DMA((2, 2))] + (plan.scratch if plan else []),
        compiler_params=_params("arbitrary", "arbitrary"),
    )(qkv, qkv, qkv, *extra)


def _attn_bwd(qkv, do, w_sv, s_sv, *, name, plan=None):
    T = qkv.shape[0]
    D = qkv.shape[1] // 3
    B = ATT_BLOCK
    W = _attn_tile(T)
    H = ATT_HEADS
    lanes = H * HEAD_DIM
    ngroups = D // lanes
    nq = T // W
    scale = HEAD_DIM ** -0.5
    nx, before, after = _ride_along(plan, ngroups, nq)

    def body(*refs):
        q_ref, k_ref, v_ref, do_ref, wsv_ref, ssv_ref = refs[:6]
        dq_ref, dk_ref, dv_ref = refs[6 + nx:9 + nx]
        dk_acc, dv_acc, stage_w, stage_s, stage_sems = refs[9 + 2 * nx:14 + 2 * nx]
        ride = (refs[6:6 + nx], refs[9 + nx:9 + 2 * nx], refs[14 + 2 * nx:])
        before(ride)
        p, i = pl.program_id(0), pl.program_id(1)

        @pl.when(i == 0)
        def _():
            dk_acc[...] = jnp.zeros_like(dk_acc)
            dv_acc[...] = jnp.zeros_like(dv_acc)

        def fetch(slot, j):
            src = _saved_tile(i, j)
            return [pltpu.make_async_copy(sv.at[pl.ds(p * H, H), src], stage.at[slot], stage_sems.at[slot, a])
                    for a, (sv, stage) in enumerate(((wsv_ref, stage_w), (ssv_ref, stage_s)))]

        tri_before = _tri(False, 1.0)
        heads = [slice(hh * HEAD_DIM, (hh + 1) * HEAD_DIM) for hh in range(H)]
        qs = [q_ref[:, cols] * scale for cols in heads]
        douts = [do_ref[:, cols] for cols in heads]

        def grad(j, state, masked):
            rows = pl.ds(pl.multiple_of(j * W, W), W)
            slot = j % 2
            for cp in fetch(slot, j):
                cp.wait()
            if not masked:
                for cp in fetch(1 - slot, j + 1):
                    cp.start()
            out = []
            for hh, (cols, (gsum, dq)) in enumerate(zip(heads, state)):
                wb = stage_w[slot, hh]
                sig = stage_s[slot, hh].astype(F32)
                dw = lax.dot_general(douts[hh], v_ref[rows, cols], (((1,), (1,)), ((), ())),
                                     preferred_element_type=F32)
                g = dw * wb.astype(F32)
                befores = []
                for b in range(W // B):
                    blk = g[:, b * B:(b + 1) * B]
                    within = _split_dot(blk, tri_before)
                    befores.append(within + gsum)
                    gsum = gsum + (within[:, B - 1:B] + blk[:, B - 1:B])
                dz = g - sig * (g + jnp.concatenate(befores, axis=1))
                if masked:
                    dz = jnp.where(_causal_mask(W), dz, 0.0)
                dzb = dz.astype(BF16)
                dq = dq + jnp.dot(dzb, k_ref[rows, cols] * scale, preferred_element_type=F32)
                dk_acc[rows, cols] += lax.dot_general(dzb, qs[hh], (((0,), (0,)), ((), ())),
                                                      preferred_element_type=F32)
                dv_acc[rows, cols] += lax.dot_general(wb, douts[hh], (((0,), (0,)), ((), ())),
                                                      preferred_element_type=F32)
                out.append((gsum, dq))
            return tuple(out)

        for cp in fetch(0, 0):
            cp.start()
        zero = (jnp.zeros((W, 1), F32), jnp.zeros((W, HEAD_DIM), F32))
        state = lax.fori_loop(0, i, lambda j, st: grad(j, st, False), (zero,) * H)
        state = grad(i, state, True)
        for cols, (_, dq) in zip(heads, state):
            dq_ref[:, cols] = dq.astype(BF16)

        @pl.when(i == nq - 1)
        def _():
            dk_ref[...] = dk_acc[...].astype(BF16)
            dv_ref[...] = dv_acc[...].astype(BF16)

        after(ride)

    extra = plan.inputs if plan else []
    return _pcall(
        body, name=name, grid=(ngroups, nq),
        in_specs=[pl.BlockSpec((W, lanes), lambda p, i: (i, p)),
                  pl.BlockSpec((T, lanes), lambda p, i: (0, ngroups + p)),
                  pl.BlockSpec((T, lanes), lambda p, i: (0, 2 * ngroups + p)),
                  pl.BlockSpec((W, lanes), lambda p, i: (i, p)), ANY, ANY] + [ANY] * nx,
        out_specs=[pl.BlockSpec((W, lanes), lambda p, i: (i, p)),
                   pl.BlockSpec((T, lanes), lambda p, i: (0, p)),
                   pl.BlockSpec((T, lanes), lambda p, i: (0, p))] + [ANY] * nx,
        out_shape=[_sds((T, D), BF16)] * 3 + (plan.out_shape if plan else []),
        scratch_shapes=[pltpu.VMEM((T, lanes), F32), pltpu.VMEM((T, lanes), F32),
                        pltpu.VMEM((2, H, W, W), BF16), pltpu.VMEM((2, H, W, W), BF16),
                        pltpu.SemaphoreType.DMA((2, 2))]
        + (plan.scratch if plan else []),
        compiler_params=_params("arbitrary", "arbitrary"),
    )(qkv, qkv, qkv, do, w_sv, s_sv, *extra)


HALO = SUBLANE


def _lru_gates(u, w_rg, b_rg, w_ig, b_ig, lam):
    nb = w_rg.shape[0]
    pre_r, pre_i = [], []
    for n in range(nb):
        ub = u[:, n * LANE:(n + 1) * LANE].astype(BF16)
        pre_r.append(jnp.dot(ub, w_rg[n], preferred_element_type=F32))
        pre_i.append(jnp.dot(ub, w_ig[n], preferred_element_type=F32))
    r = _sigmoid(jnp.concatenate(pre_r, axis=1) + b_rg)
    i = _sigmoid(jnp.concatenate(pre_i, axis=1) + b_ig)
    c = -LRU_C * _softplus_parts(-lam)[0]
    log_a = c * r
    a = jnp.exp(log_a)
    x2 = 2.0 * log_a
    em1 = jnp.where(jnp.abs(x2) < 1e-2, x2 * (1.0 + x2 * (0.5 + x2 * (1.0 / 6.0))), jnp.exp(x2) - 1.0)
    mult = jnp.sqrt(-em1)
    return r, i, a, mult, c


def _conv_rows(buf_ref, tt, conv_w, conv_b):
    u = conv_b
    for j in range(4):
        u = u + buf_ref[pl.ds(HALO - 3 + j, tt), :] * conv_w[j:j + 1, :]
    return u


def _fill_with_halo(buf_ref, prev_ref, cur_ref, first):
    tt = cur_ref.shape[0]
    buf_ref[pl.ds(0, HALO), :] = jnp.where(first, 0.0, prev_ref[...])
    buf_ref[pl.ds(HALO, tt), :] = cur_ref[...]


def _lru_time_tile(T):
    return _tile(T, 256, SUBLANE)


def _lru_pre(gr, conv_w, conv_b, w_rg, b_rg, w_ig, b_ig, lam, *, name):
    T = gr.shape[0]
    D = gr.shape[1] // 2
    tt = _lru_time_tile(T)
    hb = tt // HALO

    def body(x_ref, xp_ref, cw_ref, cb_ref, wr_ref, br_ref, wi_ref, bi_ref, lam_ref, a_ref, b_ref, buf):
        _fill_with_halo(buf, xp_ref, x_ref, pl.program_id(0) == 0)
        u = _conv_rows(buf, tt, cw_ref[...], cb_ref[...])
        _, i, a, mult, _ = _lru_gates(u, wr_ref, br_ref[...], wi_ref, bi_ref[...], lam_ref[...])
        a_ref[...] = a
        b_ref[...] = mult * (i * u)

    vec = pl.BlockSpec((1, D), lambda t: (0, 0))
    mat = pl.BlockSpec(w_rg.shape, lambda t: (0, 0, 0))
    return _pcall(
        body, name=name, grid=(T // tt,),
        in_specs=[pl.BlockSpec((tt, D), lambda t: (t, 1)),
                  pl.BlockSpec((HALO, D), lambda t: (jnp.maximum(t * hb - 1, 0), 1)),
                  pl.BlockSpec((4, D), lambda t: (0, 0)), vec, mat, vec, mat, vec, vec],
        out_specs=[pl.BlockSpec((tt, D), lambda t: (t, 0))] * 2,
        out_shape=[_sds((T, D), F32)] * 2,
        scratch_shapes=[pltpu.VMEM((tt + HALO, D), F32)],
        compiler_params=_params("parallel"),
    )(gr, gr, conv_w, conv_b, w_rg, b_rg, w_ig, b_ig, lam)


def _lru_scan(a, b, *, reverse, name):
    T, D = a.shape
    nb = D // LANE
    ts = _tile(T, 1056, SUBLANE)
    nt = T // ts
    a3 = a.reshape(T, nb, LANE)
    b3 = b.reshape(T, nb, LANE)

    def body(a_ref, b_ref, o_ref, carry):
        @pl.when(pl.program_id(0) == 0)
        def _():
            carry[...] = jnp.zeros_like(carry)

        if reverse:
            def step(k, c):
                t = ts - 1 - k
                l = b_ref[t] + c
                o_ref[t] = l
                return a_ref[t] * l
        else:
            def step(k, h):
                h = a_ref[k] * h + b_ref[k]
                o_ref[k] = h
                return h

        carry[...] = lax.fori_loop(0, ts, step, carry[...], unroll=8)

    if reverse:
        spec = pl.BlockSpec((ts, nb, LANE), lambda t: (nt - 1 - t, 0, 0))
    else:
        spec = pl.BlockSpec((ts, nb, LANE), lambda t: (t, 0, 0))
    out = _pcall(
        body, name=name, grid=(nt,),
        in_specs=[spec, spec], out_specs=spec,
        out_shape=_sds((T, nb, LANE), F32),
        scratch_shapes=[pltpu.VMEM((nb, LANE), F32)],
        compiler_params=_params("arbitrary"),
    )(a3, b3)
    return out.reshape(T, D)


def _lru_out(gr, hs, *, name):
    T, D = hs.shape
    tt = _tile(T, 1056, 16)

    def body(g_ref, h_ref, y_ref):
        y_ref[...] = (h_ref[...] * _gelu_parts(g_ref[...])[0]).astype(BF16)

    return _pcall(
        body, name=name, grid=(T // tt,),
        in_specs=[pl.BlockSpec((tt, D), lambda t: (t, 0)), pl.BlockSpec((tt, D), lambda t: (t, 0))],
        out_specs=pl.BlockSpec((tt, D), lambda t: (t, 0)),
        out_shape=_sds((T, D), BF16),
        compiler_params=_params("parallel"),
    )(gr, hs)


def _lru_out_bwd(gr, hs, dy, *, name):
    T, D = hs.shape
    tt = _tile(T, 1056, 16)

    def body(g_ref, h_ref, dy_ref, dg_ref, dh_ref):
        gelu, dgelu = _gelu_parts(g_ref[...])
        dy = dy_ref[...]
        dg_ref[...] = (dy * h_ref[...] * dgelu).astype(BF16)
        dh_ref[...] = dy * gelu

    spec = pl.BlockSpec((tt, D), lambda t: (t, 0))
    return _pcall(
        body, name=name, grid=(T // tt,),
        in_specs=[spec, spec, spec], out_specs=[spec, spec],
        out_shape=[_sds((T, D), BF16), _sds((T, D), F32)],
        compiler_params=_params("parallel"),
    )(gr, hs, dy)


def _lru_gate_bwd(gr, hs, lmb, conv_w, conv_b, w_rg, b_rg, w_ig, b_ig, lam, *, name):
    T, D = hs.shape
    nb = w_rg.shape[0]
    tt = _lru_time_tile(T)
    hb = tt // HALO
    nt = T // tt

    def body(x_ref, xp_ref, h_ref, hp_ref, l_ref, cw_ref, cb_ref, wr_ref, br_ref, wi_ref, bi_ref, lam_ref,
             du_ref, dwr_ref, dbr_ref, dwi_ref, dbi_ref, dlam_ref, xbuf, hbuf):
        t = pl.program_id(0)
        first = t == 0
        _fill_with_halo(xbuf, xp_ref, x_ref, first)
        _fill_with_halo(hbuf, hp_ref, h_ref, first)
        u = _conv_rows(xbuf, tt, cw_ref[...], cb_ref[...])
        lam_v = lam_ref[...]
        r, i, a, mult, c = _lru_gates(u, wr_ref, br_ref[...], wi_ref, bi_ref[...], lam_v)
        l = l_ref[...]
        h_prev = hbuf[pl.ds(HALO - 1, tt), :]
        dlog_a = l * h_prev * a - l * (i * u) * (a * a) / mult
        d_iu = l * mult
        dpre_r = (dlog_a * c) * (r * (1.0 - r))
        dpre_i = (d_iu * u) * (i * (1.0 - i))
        dpr_b = dpre_r.astype(BF16)
        dpi_b = dpre_i.astype(BF16)
        du_parts, dwr, dwi = [], [], []
        for n in range(nb):
            cs = slice(n * LANE, (n + 1) * LANE)
            ub = u[:, cs].astype(BF16)
            du_parts.append(
                lax.dot_general(dpr_b[:, cs], wr_ref[n], (((1,), (1,)), ((), ())), preferred_element_type=F32)
                + lax.dot_general(dpi_b[:, cs], wi_ref[n], (((1,), (1,)), ((), ())), preferred_element_type=F32))
            dwr.append(lax.dot_general(ub, dpr_b[:, cs], (((0,), (0,)), ((), ())), preferred_element_type=F32))
            dwi.append(lax.dot_general(ub, dpi_b[:, cs], (((0,), (0,)), ((), ())), preferred_element_type=F32))
        du_ref[...] = d_iu * i + jnp.concatenate(du_parts, axis=1)
        dbr = jnp.sum(dpre_r, axis=0, keepdims=True)
        dbi = jnp.sum(dpre_i, axis=0, keepdims=True)
        dc = jnp.sum(dlog_a * r, axis=0, keepdims=True)

        @pl.when(first)
        def _():
            for n in range(nb):
                dwr_ref[n] = dwr[n]
                dwi_ref[n] = dwi[n]
            dbr_ref[...] = dbr
            dbi_ref[...] = dbi
            dlam_ref[...] = dc

        @pl.when(t > 0)
        def _():
            for n in range(nb):
                dwr_ref[n] += dwr[n]
                dwi_ref[n] += dwi[n]
            dbr_ref[...] += dbr
            dbi_ref[...] += dbi
            dlam_ref[...] += dc

        @pl.when(t == nt - 1)
        def _():
            dlam_ref[...] = dlam_ref[...] * (LRU_C * _sigmoid(-lam_v))

    vec = pl.BlockSpec((1, D), lambda t: (0, 0))
    mat = pl.BlockSpec(w_rg.shape, lambda t: (0, 0, 0))
    blk = pl.BlockSpec((tt, D), lambda t: (t, 0))
    prev = pl.BlockSpec((HALO, D), lambda t: (jnp.maximum(t * hb - 1, 0), 0))
    return _pcall(
        body, name=name, grid=(nt,),
        in_specs=[pl.BlockSpec((tt, D), lambda t: (t, 1)),
                  pl.BlockSpec((HALO, D), lambda t: (jnp.maximum(t * hb - 1, 0), 1)),
                  blk, prev, blk,
                  pl.BlockSpec((4, D), lambda t: (0, 0)), vec, mat, vec, mat, vec, vec],
        out_specs=[blk, mat, vec, mat, vec, vec],
        out_shape=[_sds((T, D), F32), _sds(w_rg.shape, F32), _sds((1, D), F32),
                   _sds(w_rg.shape, F32), _sds((1, D), F32), _sds((1, D), F32)],
        scratch_shapes=[pltpu.VMEM((tt + HALO, D), F32), pltpu.VMEM((tt + HALO, D), F32)],
        compiler_params=_params("arbitrary"),
    )(gr, gr, hs, hs, lmb, conv_w, conv_b, w_rg, b_rg, w_ig, b_ig, lam)


def _lru_conv_bwd(gr, du, conv_w, *, name):
    T, D = du.shape
    tt = _lru_time_tile(T)
    hb = tt // HALO
    nt = T // tt

    def body(x_ref, xp_ref, du_ref, dun_ref, cw_ref, dx_ref, dcw_ref, dcb_ref, xbuf, dbuf):
        t = pl.program_id(0)
        _fill_with_halo(xbuf, xp_ref, x_ref, t == 0)
        du = du_ref[...]
        dbuf[pl.ds(0, tt), :] = du
        dbuf[pl.ds(tt, HALO), :] = jnp.where(t == nt - 1, 0.0, dun_ref[...])
        cw = cw_ref[...]
        dx = jnp.zeros((tt, D), F32)
        dcw = []
        for j in range(4):
            dx = dx + dbuf[pl.ds(3 - j, tt), :] * cw[j:j + 1, :]
            dcw.append(jnp.sum(du * xbuf[pl.ds(HALO - 3 + j, tt), :], axis=0, keepdims=True))
        dx_ref[...] = dx.astype(BF16)
        dcw = jnp.concatenate(dcw, axis=0)
        dcb = jnp.sum(du, axis=0, keepdims=True)

        @pl.when(t == 0)
        def _():
            dcw_ref[...] = dcw
            dcb_ref[...] = dcb

        @pl.when(t > 0)
        def _():
            dcw_ref[...] += dcw
            dcb_ref[...] += dcb

    blk = pl.BlockSpec((tt, D), lambda t: (t, 0))
    return _pcall(
        body, name=name, grid=(nt,),
        in_specs=[pl.BlockSpec((tt, D), lambda t: (t, 1)),
                  pl.BlockSpec((HALO, D), lambda t: (jnp.maximum(t * hb - 1, 0), 1)),
                  blk,
                  pl.BlockSpec((HALO, D), lambda t: (jnp.minimum((t + 1) * hb, T // HALO - 1), 0)),
                  pl.BlockSpec((4, D), lambda t: (0, 0))],
        out_specs=[blk, pl.BlockSpec((4, D), lambda t: (0, 0)), pl.BlockSpec((1, D), lambda t: (0, 0))],
        out_shape=[_sds((T, D), BF16), _sds((4, D), F32), _sds((1, D), F32)],
        scratch_shapes=[pltpu.VMEM((tt + HALO, D), F32), pltpu.VMEM((tt + HALO, D), F32)],
        compiler_params=_params("arbitrary"),
    )(gr, gr, du, du, conv_w)


def _loss_head(h, g, target, *, row_lo, row_hi, name):
    T, D = h.shape
    tm = _tile(T, 1056, 16)

    def body(h_ref, g_ref, t_ref, loss_ref, dh_ref, dhb_ref, dg_ref):
        i = pl.program_id(0)
        x = h_ref[...]
        g = g_ref[...]
        row = i * tm + lax.broadcasted_iota(jnp.int32, (tm, 1), 0)
        valid = jnp.logical_and(row >= row_lo, row < row_hi)
        rstd = _rstd(x)
        n = x * rstd
        err = jnp.where(valid, n * g - t_ref[...], 0.0)
        part = (0.5 / D) * jnp.sum(jnp.sum(err * err, axis=1, keepdims=True), axis=0, keepdims=True)
        dy = err * (1.0 / D)
        dn = dy * g
        dh = rstd * (dn - n * jnp.mean(dn * n, axis=-1, keepdims=True))
        dh_ref[...] = dh
        dhb_ref[...] = dh.astype(BF16)
        dg = jnp.sum(dy * n, axis=0, keepdims=True)

        @pl.when(i == 0)
        def _():
            loss_ref[...] = part
            dg_ref[...] = dg

        @pl.when(i > 0)
        def _():
            loss_ref[...] += part
            dg_ref[...] += dg

    blk = pl.BlockSpec((tm, D), lambda i: (i, 0))
    vec = pl.BlockSpec((1, D), lambda i: (0, 0))
    return _pcall(
        body, name=name, grid=(T // tm,),
        in_specs=[blk, vec, blk],
        out_specs=[pl.BlockSpec((1, 1), lambda i: (0, 0)), blk, blk, vec],
        out_shape=[_sds((1, 1), F32), _sds((T, D), F32), _sds((T, D), BF16), _sds((1, D), F32)],
        compiler_params=_params("arbitrary"),
    )(h, g, target)


def _adamw_math(w, g, m, v):
    c1 = 1.0 / (1.0 - ADAM_B1 ** ADAM_STEP)
    c2 = 1.0 / (1.0 - ADAM_B2 ** ADAM_STEP)
    m = ADAM_B1 * m + (1.0 - ADAM_B1) * g
    v = ADAM_B2 * v + (1.0 - ADAM_B2) * (g * g)
    delta = -ADAM_LR * ((m * c1) / (jnp.sqrt(v * c2) + ADAM_EPS) + ADAM_WD * w)
    return delta, m, v


def _adamw_halves(w, mine, theirs, m, v, c, *, name):
    _, R, C = w.shape
    tr = _tile(R, 256, SUBLANE)

    def body(c_ref, w_ref, a_ref, b_ref, m_ref, v_ref, g_ref, d_ref, nm_ref, nv_ref):
        g = jnp.where(pl.program_id(0) == c_ref[0], a_ref[...], b_ref[...])
        g_ref[...] = g
        d_ref[...], nm_ref[...], nv_ref[...] = _adamw_math(w_ref[...], g, m_ref[...], v_ref[...])

    full = pl.BlockSpec((None, tr, C), lambda h, i, c_ref: (h, i, 0))
    half = pl.BlockSpec((tr, C), lambda h, i, c_ref: (i, 0))
    return _pcall(
        body, name=name,
        grid_spec=pltpu.PrefetchScalarGridSpec(
            num_scalar_prefetch=1, grid=(2, R // tr),
            in_specs=[full, half, half, full, full], out_specs=[full] * 4),
        out_shape=[_sds((2, R, C), F32)] * 4,
        compiler_params=_params("parallel", "parallel"),
    )(c, w, mine, theirs, m, v)


def _adamw(w, g, m, v, *, name):
    R, C = w.shape
    tr = _tile(R, 512, SUBLANE)

    def body(w_ref, g_ref, m_ref, v_ref, d_ref, nm_ref, nv_ref):
        d_ref[...], nm_ref[...], nv_ref[...] = _adamw_math(w_ref[...], g_ref[...], m_ref[...], v_ref[...])

    blk = pl.BlockSpec((tr, C), lambda i: (i, 0))
    return _pcall(
        body, name=name, grid=(R // tr,),
        in_specs=[blk] * 4, out_specs=[blk] * 3,
        out_shape=[_sds((R, C), F32)] * 3,
        compiler_params=_params("parallel"),
    )(w, g, m, v)


ANY = pl.BlockSpec(memory_space=pl.ANY)


def _place():
    x, y, c = lax.axis_index("x"), lax.axis_index("y"), lax.axis_index("c")
    chips = [(1 - x, y), (x, 1 - y), (1 - x, 1 - y)]
    return x, y, c, chips


LOCAL_PIECES = 4


class _GatherChips:
    def __init__(self, vs):
        self.inputs = list(vs)
        n = self.n = len(vs)
        self.out_shape = [_sds((4,) + v.shape, v.dtype) for v in vs]
        self.scratch = [pltpu.SemaphoreType.DMA((6 * n,)), pltpu.SemaphoreType.DMA((6 * n,)),
                        pltpu.SemaphoreType.DMA((LOCAL_PIECES * n,))]

    def _copies(self, v_refs, o_refs, sems):
        send_sems, recv_sems, local_sems = sems
        x, y, c, chips = _place()
        me = 2 * x + y

        def copy(a, k, block, half, to, src=None):
            dst = o_refs[a].at[block, half]
            return pltpu.make_async_remote_copy(
                src_ref=dst if src is None else src, dst_ref=dst,
                send_sem=send_sems.at[6 * a + k], recv_sem=recv_sems.at[6 * a + k],
                device_id=to, device_id_type=MESH)

        ks = [(a, k, cx, cy) for a in range(self.n) for k, (cx, cy) in enumerate(chips)]

        def local():
            out = []
            for a in range(self.n):
                rows = self.inputs[a].shape[1] // (LOCAL_PIECES // 2)
                for p in range(LOCAL_PIECES):
                    h, r0 = p % 2, (p // 2) * rows
                    out.append(pltpu.make_async_copy(
                        v_refs[a].at[h, pl.ds(r0, rows)], o_refs[a].at[me, h, pl.ds(r0, rows)],
                        local_sems.at[LOCAL_PIECES * a + p]))
            return out

        return dict(
            first=lambda: [copy(a, k, me, c, (cx, cy, c), src=v_refs[a].at[c]) for a, k, cx, cy in ks],
            landed=lambda: [copy(a, k, 2 * cx + cy, c, (x, y, c)) for a, k, cx, cy in ks],
            passed=lambda: [copy(a, 3 + k, 2 * cx + cy, c, (x, y, 1 - c)) for a, k, cx, cy in ks],
            final=lambda: [copy(a, 3 + k, 2 * cx + cy, 1 - c, (x, y, c)) for a, k, cx, cy in ks],
            local=local)

    def start(self, v_refs, o_refs, sems):
        cps = self._copies(v_refs, o_refs, sems)
        for cp in cps["first"]() + cps["local"]():
            cp.start()

    def mid(self, v_refs, o_refs, sems):
        cps = self._copies(v_refs, o_refs, sems)
        for got, fwd in zip(cps["landed"](), cps["passed"]()):
            got.wait_recv()
            fwd.start()

    def finish(self, v_refs, o_refs, sems):
        cps = self._copies(v_refs, o_refs, sems)
        for cp in cps["final"]():
            cp.wait_recv()
        for cp in cps["first"]() + cps["passed"]():
            cp.wait_send()
        for cp in cps["local"]():
            cp.wait()


class _ExchangeBlocks:
    def __init__(self, ps):
        self.inputs = list(ps)
        n = self.n = len(ps)
        self.out_shape = [_sds((8,) + p.shape[2:], p.dtype) for p in ps]
        self.scratch = [pltpu.SemaphoreType.DMA((7 * n,)), pltpu.SemaphoreType.DMA((7 * n,))]

    def _copies(self, p_refs, o_refs, sems, incoming):
        send_sems, recv_sems = sems
        x, y, c, _ = _place()
        me = 4 * x + 2 * y + c
        out = []
        for a in range(self.n):
            for k in range(1, 8):
                px, py, pc = x ^ (k >> 2), y ^ ((k >> 1) & 1), c ^ (k & 1)
                out.append(pltpu.make_async_remote_copy(
                    src_ref=p_refs[a].at[2 * px + py, pc],
                    dst_ref=o_refs[a].at[4 * px + 2 * py + pc if incoming else me],
                    send_sem=send_sems.at[7 * a + k - 1], recv_sem=recv_sems.at[7 * a + k - 1],
                    device_id=(x, y, c) if incoming else (px, py, pc), device_id_type=MESH))
        return out

    def start(self, p_refs, o_refs, sems):
        for cp in self._copies(p_refs, o_refs, sems, False):
            cp.start()

    def mid(self, p_refs, o_refs, sems):
        pass

    def finish(self, p_refs, o_refs, sems):
        for cp in self._copies(p_refs, o_refs, sems, True):
            cp.wait_recv()
        for cp in self._copies(p_refs, o_refs, sems, False):
            cp.wait_send()


def _run_exchange(plan, *, name):
    n = plan.n

    def body(*refs):
        args = (refs[:n], refs[n:2 * n], refs[2 * n:])
        plan.start(*args)
        plan.mid(*args)
        plan.finish(*args)

    return _pcall(
        body, name=name, in_specs=[ANY] * n, out_specs=[ANY] * n,
        out_shape=plan.out_shape, scratch_shapes=plan.scratch,
    )(*plan.inputs)


def _send_sibling(rs, *, name):
    n = len(rs)

    def body(*refs):
        r_refs, o_refs = refs[:n], refs[n:2 * n]
        send_sems, recv_sems = refs[2 * n:]
        x, y, c, _ = _place()
        cps = [pltpu.make_async_remote_copy(
            src_ref=r_refs[a], dst_ref=o_refs[a], send_sem=send_sems.at[a], recv_sem=recv_sems.at[a],
            device_id=(x, y, 1 - c), device_id_type=MESH) for a in range(n)]
        for cp in cps:
            cp.start()
        for cp in cps:
            cp.wait()

    return _pcall(
        body, name=name, in_specs=[ANY] * n, out_specs=[ANY] * n,
        out_shape=[_sds(r.shape, r.dtype) for r in rs],
        scratch_shapes=[pltpu.SemaphoreType.DMA((n,)), pltpu.SemaphoreType.DMA((n,))],
    )(*rs)


def _add_devices(p, got, place, *, name):
    _, _, R, C = p.shape
    tr = _tile(R, 256, 16)

    def body(place_ref, p_ref, o_ref, out_ref):
        me = place_ref[2]
        own = p_ref[...].astype(F32)
        acc = jnp.where(me == 0, own, o_ref[0].astype(F32))
        for d in range(1, 8):
            acc = acc + jnp.where(me == d, own, o_ref[d].astype(F32))
        out_ref[...] = acc

    return _pcall(
        body, name=name,
        grid_spec=pltpu.PrefetchScalarGridSpec(
            num_scalar_prefetch=1, grid=(R // tr,),
            in_specs=[pl.BlockSpec((None, None, tr, C), lambda i, pr: (pr[0], pr[1], i, 0)),
                      pl.BlockSpec((8, tr, C), lambda i, pr: (0, i, 0))],
            out_specs=pl.BlockSpec((tr, C), lambda i, pr: (i, 0))),
        out_shape=_sds((R, C), F32),
        compiler_params=_params("parallel"),
    )(place, p, got)


def _round_up(n, m):
    return (n + m - 1) // m * m


def _f32_as_bf16(a):
    return lax.bitcast_convert_type(a.astype(F32), BF16).reshape(-1)


def _bf16_as_f32(a):
    return lax.bitcast_convert_type(a.reshape(-1, 2), F32)


def _by_chip_cols(a, cols):
    lead = a.shape[:-1]
    a = a.reshape(lead + (4, cols))
    return jnp.moveaxis(a, -2, 0).reshape(4, -1)


def kernel(x, meta_tokens, norm_mix, norm_mlp, sb_w_qkv, sb_w_o, lru_w_in, lru_conv_w, lru_conv_b, lru_w_rg, lru_b_rg, lru_w_ig, lru_b_ig, lru_lambda, lru_w_out, mlp_w_up, mlp_w_down, norm_final, loss_target, m_meta_tokens, m_norm_mix, m_norm_mlp, m_sb_w_qkv, m_sb_w_o, m_lru_w_in, m_lru_conv_w, m_lru_conv_b, m_lru_w_rg, m_lru_b_rg, m_lru_w_ig, m_lru_b_ig, m_lru_lambda, m_lru_w_out, m_mlp_w_up, m_mlp_w_down, m_norm_final, v_meta_tokens, v_norm_mix, v_norm_mlp, v_sb_w_qkv, v_sb_w_o, v_lru_w_in, v_lru_conv_w, v_lru_conv_b, v_lru_w_rg, v_lru_b_rg, v_lru_w_ig, v_lru_b_ig, v_lru_lambda, v_lru_w_out, v_mlp_w_up, v_mlp_w_down, v_norm_final):
    weights = dict(meta_tokens=meta_tokens, norm_mix=norm_mix, norm_mlp=norm_mlp, sb_w_qkv=sb_w_qkv,
                   sb_w_o=sb_w_o, lru_w_in=lru_w_in, lru_conv_w=lru_conv_w, lru_conv_b=lru_conv_b,
                   lru_w_rg=lru_w_rg, lru_b_rg=lru_b_rg, lru_w_ig=lru_w_ig, lru_b_ig=lru_b_ig,
                   lru_lambda=lru_lambda, lru_w_out=lru_w_out, mlp_w_up=mlp_w_up, mlp_w_down=mlp_w_down,
                   norm_final=norm_final)
    m_in = dict(meta_tokens=m_meta_tokens, norm_mix=m_norm_mix, norm_mlp=m_norm_mlp, sb_w_qkv=m_sb_w_qkv,
                sb_w_o=m_sb_w_o, lru_w_in=m_lru_w_in, lru_conv_w=m_lru_conv_w, lru_conv_b=m_lru_conv_b,
                lru_w_rg=m_lru_w_rg, lru_b_rg=m_lru_b_rg, lru_w_ig=m_lru_w_ig, lru_b_ig=m_lru_b_ig,
                lru_lambda=m_lru_lambda, lru_w_out=m_lru_w_out, mlp_w_up=m_mlp_w_up,
                mlp_w_down=m_mlp_w_down, norm_final=m_norm_final)
    v_in = dict(meta_tokens=v_meta_tokens, norm_mix=v_norm_mix, norm_mlp=v_norm_mlp, sb_w_qkv=v_sb_w_qkv,
                sb_w_o=v_sb_w_o, lru_w_in=v_lru_w_in, lru_conv_w=v_lru_conv_w, lru_conv_b=v_lru_conv_b,
                lru_w_rg=v_lru_w_rg, lru_b_rg=v_lru_b_rg, lru_w_ig=v_lru_w_ig, lru_b_ig=v_lru_b_ig,
                lru_lambda=v_lru_lambda, lru_w_out=v_lru_w_out, mlp_w_up=v_mlp_w_up,
                mlp_w_down=v_mlp_w_down, norm_final=v_norm_final)
    names = list(weights)

    seq, D = x.shape[1], x.shape[2]
    n_meta = meta_tokens.shape[0]
    Dq = D // 4
    T = _round_up(n_meta + seq, ATT_BLOCK)
    nb = lru_w_rg.shape[1]
    F = mlp_w_up.shape[2]
    depth = mlp_w_up.shape[0]
    my_x, my_y, my_c = lax.axis_index("x"), lax.axis_index("y"), lax.axis_index("c")
    c_arr = jnp.reshape(my_c, (1,)).astype(jnp.int32)

    assert depth == 2

    def halves(a):
        return a.astype(BF16).reshape(2, a.shape[0] // 2, a.shape[1])

    small = [meta_tokens, lru_conv_w[0], lru_conv_b, lru_b_rg, lru_b_ig, lru_lambda]
    sparts = [_f32_as_bf16(s) for s in small]
    sizes = [p.shape[0] for p in sparts]
    total = _round_up(sum(sizes), 2 * 32 * LANE)
    sflat =jnp.concatenate(sparts + [jnp.zeros((total - sum(sizes),), BF16)]).reshape(2, -1, LANE)
    gq, gsm = _run_exchange(_GatherChips([halves(sb_w_qkv[0]), sflat]), name="gather_first")
    gather_rest = _GatherChips([halves(sb_w_o[0]), halves(lru_w_in[0]), halves(lru_w_out[0]),
                                mlp_w_up.astype(BF16), mlp_w_down.astype(BF16)])
    w_qkv = gq.reshape(4, D, 3 * Dq)
    gsm = gsm.reshape(4, total)
    offs = [sum(sizes[:k]) for k in range(len(sizes))]
    sm = [_bf16_as_f32(gsm[:, o:o + s]) for o, s in zip(offs, sizes)]
    meta_full = jnp.moveaxis(sm[0].reshape(4, n_meta, Dq), 0, 1).reshape(n_meta, D)
    conv_w = jnp.moveaxis(sm[1].reshape(4, 4, Dq), 0, 1).reshape(4, D)
    conv_b, b_rg, b_ig, lam = [s.reshape(1, D) for s in sm[2:6]]
    w_rg = lru_w_rg[0].astype(BF16)
    w_ig = lru_w_ig[0].astype(BF16)
    g_mix = [norm_mix[l].reshape(1, D) for l in range(depth)]
    g_mlp = [norm_mlp[l].reshape(1, D) for l in range(depth)]
    g_fin = norm_final.reshape(1, D)

    pad_rows = T - n_meta - seq
    h0 = jnp.concatenate([meta_full, x[0], jnp.zeros((pad_rows, D), F32)], axis=0)
    target = jnp.concatenate([jnp.zeros((n_meta, D), F32), loss_target[0], jnp.zeros((pad_rows, D), F32)], axis=0)

    hn0, qkv = _norm_mm(h0, g_mix[0], w_qkv, out_dtype=BF16, name="qkv_proj")
    att, w_sv, s_sv, go, gi, gout, w_up, w_down = _attn_fwd(qkv, name="attn_fwd", plan=gather_rest)
    w_o = go.reshape(D, D)
    w_in = gi.reshape(4, D, 2 * Dq)
    w_out = gout.reshape(D, D)
    h1 = _mm_res(att, w_o, h0, name="attn_out")
    h2, hnm0, up0 = _mlp_fwd(h1, g_mlp[0], w_up, w_down, layer=0, name="mlp0_fwd")
    hn1, gr = _norm_mm(h2, g_mix[1], w_in, out_dtype=F32, name="lru_in")
    a_t, b_t = _lru_pre(gr, conv_w, conv_b, w_rg, b_rg, w_ig, b_ig, lam, name="lru_pre")
    hs = _lru_scan(a_t, b_t, reverse=False, name="lru_scan")
    y = _lru_out(gr, hs, name="lru_gate_out")
    h3 = _mm_res(y, w_out, h2, name="lru_out")
    h4, hnm1, up1 = _mlp_fwd(h3, g_mlp[1], w_up, w_down, layer=1, name="mlp1_fwd")
    loss, dh4, dh4b, dg_fin = _loss_head(h4, g_fin, target, row_lo=n_meta, row_hi=n_meta + seq, name="loss_head")

    dup1, dh3, dh3b, dg_mlp1 = _mlp_bwd(dh4, h3, g_mlp[1], up1, w_up, w_down, layer=1, name="mlp1_bwd")
    dw_up = _mm_tn(hnm1, dup1, shards=4, relu2=False, slot=1, name="mlp1_dwup")
    dw_down = _mm_tn(up1, dh4b, shards=1, relu2=True, slot=1, row_shards=4, name="mlp1_dwdown")
    dy = _mm_nt(dh3b, w_out, out_dtype=F32, name="lru_out_bwd")
    dw_out = _mm_tn(y, dh3b, shards=1, relu2=False, name="lru_dwout")
    dgate, dhy = _lru_out_bwd(gr, hs, dy, name="lru_gate_out_bwd")
    lmb = _lru_scan(a_t, dhy, reverse=True, name="lru_scan_bwd")
    du, dw_rg, db_rg, dw_ig, db_ig, dlam = _lru_gate_bwd(
        gr, hs, lmb, conv_w, conv_b, w_rg, b_rg, w_ig, b_ig, lam, name="lru_gate_bwd")
    drec, dconv_w, dconv_b = _lru_conv_bwd(gr, du, conv_w, name="lru_conv_bwd")
    dgr = jnp.concatenate([dgate, drec], axis=1)
    dh2, dh2b, dg_mix1 = _mm_nt_normbwd(dgr, w_in, h2, g_mix[1], dh3, name="lru_in_bwd")
    dw_in = _mm_tn(hn1, dgr, shards=4, relu2=False, name="lru_dwin")
    dup0, dh1, dh1b, dg_mlp0 = _mlp_bwd(dh2, h1, g_mlp[0], up0, w_up, w_down, layer=0, name="mlp0_bwd")
    dw_up = _mm_tn(hnm0, dup0, shards=4, relu2=False, slot=0, into=dw_up, name="mlp0_dwup")
    dw_down = _mm_tn(up0, dh2b, shards=1, relu2=True, slot=0, into=dw_down, row_shards=4, name="mlp0_dwdown")
    datt = _mm_nt(dh1b, w_o, out_dtype=BF16, name="attn_out_bwd")
    dw_o = _mm_tn(att, dh1b, shards=1, relu2=False, name="attn_dwo")
    def halves_of(d, rows):
        return d.reshape(4, 2, rows // 2, d.shape[-1])

    early = [halves_of(dw_o, Dq), halves_of(dw_in, D), halves_of(dw_out, Dq), dw_up, dw_down]
    dq, dk, dv, *got_early = _attn_bwd(qkv, datt, w_sv, s_sv, name="attn_bwd", plan=_ExchangeBlocks(early))
    dqkv = jnp.concatenate([dq, dk, dv], axis=1)
    dh0, _, dg_mix0 = _mm_nt_normbwd(dqkv, w_qkv, h0, g_mix[0], dh1, name="qkv_bwd")
    dw_qkv = _mm_tn(hn0, dqkv, shards=4, relu2=False, name="attn_dwqkv")
    grad_x = dh0[n_meta:n_meta + seq][None]
    dmeta = dh0[:n_meta]

    large = ["sb_w_o", "lru_w_in", "lru_w_out", "mlp_w_up", "mlp_w_down", "sb_w_qkv"]
    sharded =[_by_chip_cols(dmeta, Dq), _by_chip_cols(dconv_w, Dq), dconv_b.reshape(4, Dq),
               db_rg.reshape(4, Dq), db_ig.reshape(4, Dq), dlam.reshape(4, Dq)]
    repl = [jnp.concatenate([dg_mix0, dg_mix1], axis=0).reshape(-1),
            jnp.concatenate([dg_mlp0, dg_mlp1], axis=0).reshape(-1),
            dg_fin.reshape(-1), dw_rg.reshape(-1), dw_ig.reshape(-1)]
    rsizes = [r.shape[0] for r in repl]
    rtotal = _round_up(sum(rsizes), 4 * 2 * 16 * LANE)
    rflat = jnp.concatenate(repl + [jnp.zeros((rtotal - sum(rsizes),), F32)]).reshape(4, rtotal // 4)
    gsizes = [s.shape[1] for s in sharded] + [rtotal // 4]
    gtotal = _round_up(sum(gsizes), 2 * 16 * LANE)
    tail = jnp.concatenate(sharded + [rflat, jnp.zeros((4, gtotal - sum(gsizes)), F32)], axis=1)
    late = [halves_of(dw_qkv, D), tail.reshape(4, 2, -1, LANE)]
    got_late = _run_exchange(_ExchangeBlocks(late), name="reduce_late")
    place = jnp.stack([2 * my_x + my_y, my_c, 4 * my_x + 2 * my_y + my_c]).astype(jnp.int32)
    mine = [_add_devices(p, o, place, name="reduce_add_" + t)
            for p, o, t in zip(early + late, list(got_early) + list(got_late), large + ["tail"])]
    theirs = _send_sibling(mine, name="reduce_join")

    grads, delta, new_m, new_v = {}, {}, {}, {}
    for n, a, b in zip(large, mine, theirs):
        shp = weights[n].shape
        view = (2,) + a.shape
        g, d, nm, nv = _adamw_halves(weights[n].reshape(view), a, b, m_in[n].reshape(view),
                                     v_in[n].reshape(view), c_arr, name="adamw_" + n)
        grads[n], delta[n], new_m[n], new_v[n] = g.reshape(shp), d.reshape(shp), nm.reshape(shp), nv.reshape(shp)

    lo = jnp.where(my_c == 0, mine[-1], theirs[-1])
    hi = jnp.where(my_c == 0, theirs[-1], mine[-1])
    gshard = jnp.concatenate([lo, hi], axis=0).reshape(gtotal)
    goffs = [sum(gsizes[:k]) for k in range(len(gsizes))]
    gp = [gshard[o:o + s] for o, s in zip(goffs, gsizes)]
    rfull = _run_exchange(_GatherChips([gp[-1].reshape(2, -1, LANE)]), name="gather_replicated")[0].reshape(rtotal)
    roffs = [sum(rsizes[:k]) for k in range(len(rsizes))]
    rp = [rfull[o:o + s] for o, s in zip(roffs, rsizes)]
    grads.update(meta_tokens=gp[0], lru_conv_w=gp[1], lru_conv_b=gp[2], lru_b_rg=gp[3], lru_b_ig=gp[4],
                 lru_lambda=gp[5], norm_mix=rp[0], norm_mlp=rp[1], norm_final=rp[2], lru_w_rg=rp[3],
                 lru_w_ig=rp[4])
    grads = {n: grads[n].reshape(weights[n].shape) for n in names}
    rest = [n for n in names if n not in large]
    ssz = [weights[n].size for n in rest]
    small_cols = 8 * LANE
    stotal = _round_up(sum(ssz), SUBLANE * small_cols)

    def pack(src):
        return jnp.concatenate([src[n].reshape(-1) for n in rest]
                               + [jnp.ones((stotal - sum(ssz),), F32)]).reshape(-1, small_cols)

    d, nm, nv = _adamw(pack(weights), pack(grads), pack(m_in), pack(v_in), name="adamw_small")
    soffs = [sum(ssz[:k]) for k in range(len(ssz))]
    for n, o, s in zip(rest, soffs, ssz):
        shp = weights[n].shape
        delta[n] = d.reshape(-1)[o:o + s].reshape(shp)
        new_m[n] = nm.reshape(-1)[o:o + s].reshape(shp)
        new_v[n] = nv.reshape(-1)[o:o + s].reshape(shp)

    loss = lax.psum(loss[0, 0], ("x", "y", "c"))
    return (loss, grad_x, *[grads[n] for n in names], *[delta[n] for n in names],
            *[new_m[n] for n in names], *[new_v[n] for n in names])
```

```python
import functools

import jax
import jax.numpy as jnp
from jax import lax
from jax.experimental import pallas as pl
from jax.experimental.pallas import tpu as pltpu

F32 = jnp.float32
BF16 = jnp.bfloat16
MESH = pl.DeviceIdType.MESH

EPS = 1e-6
HEAD_DIM = 64
LANE = 128
SUBLANE = 8
LRU_C = 8.0
VMEM_LIMIT = 56 * 1024 * 1024

ADAM_LR = 0.001
ADAM_B1 = 0.9
ADAM_B2 = 0.999
ADAM_EPS = 1e-08
ADAM_WD = 0.01
ADAM_STEP = 10


def _pcall(body, **kw):
    return pl.pallas_call(body, **kw)


def _params(*sem):
    return pltpu.CompilerParams(dimension_semantics=sem, vmem_limit_bytes=VMEM_LIMIT)


def _tile(n, pref, align):
    best = None
    for t in range(align, min(n, pref) + 1, align):
        if n % t == 0:
            best = t
    return n if best is None else best


def _sds(shape, dtype):
    return jax.ShapeDtypeStruct(shape, dtype)


def _rstd(x):
    return lax.rsqrt(jnp.mean(x * x, axis=-1, keepdims=True) + EPS)


def _norm_bwd(x, g, dy):
    rstd = _rstd(x)
    n = x * rstd
    dn = dy * g
    dx = rstd * (dn - n * jnp.mean(dn * n, axis=-1, keepdims=True))
    dg = jnp.sum(dy * n, axis=0, keepdims=True)
    return dx, dg


def _softplus_parts(z):
    l1p = jnp.log(1.0 + jnp.exp(-jnp.abs(z)))
    return jnp.maximum(z, 0.0) + l1p, jnp.minimum(z, 0.0) - l1p


def _sigmoid(x):
    return 1.0 / (1.0 + jnp.exp(-x))


def _gelu_parts(x):
    k = 0.7978845608028654
    inner = k * (x + 0.044715 * (x * x * x))
    t = jnp.tanh(inner)
    gelu = 0.5 * x * (1.0 + t)
    dgelu = 0.5 * (1.0 + t) + 0.5 * x * (1.0 - t * t) * (k * (1.0 + 3.0 * 0.044715 * (x * x)))
    return gelu, dgelu


def _norm_mm(h, g, w, *, out_dtype, name):
    T, D = h.shape
    S, _, n = w.shape
    tm = _tile(T, 1056, 16)
    tn = _tile(n, 768, LANE)
    nj = n // tn

    def body(h_ref, g_ref, w_ref, hn_ref, o_ref):
        @pl.when(pl.program_id(1) == 0)
        def _():
            x = h_ref[...]
            hn_ref[...] = (x * _rstd(x) * g_ref[...]).astype(BF16)

        o_ref[...] = jnp.dot(hn_ref[...], w_ref[...], preferred_element_type=F32).astype(out_dtype)

    return _pcall(
        body, name=name, grid=(T // tm, S * nj),
        in_specs=[pl.BlockSpec((tm, D), lambda i, j: (i, 0)),
                  pl.BlockSpec((1, D), lambda i, j: (0, 0)),
                  pl.BlockSpec((None, D, tn), lambda i, j: (j // nj, 0, j % nj))],
        out_specs=[pl.BlockSpec((tm, D), lambda i, j: (i, 0)),
                   pl.BlockSpec((tm, tn), lambda i, j: (i, j))],
        out_shape=[_sds((T, D), BF16), _sds((T, S * n), out_dtype)],
        compiler_params=_params("parallel", "arbitrary"),
    )(h, g, w)


def _mm_res(a, w, res, *, name):
    T, K = a.shape
    N = w.shape[1]
    tm = _tile(T, 1056, 16)

    def body(a_ref, w_ref, r_ref, o_ref):
        o_ref[...] = r_ref[...] + jnp.dot(a_ref[...], w_ref[...], preferred_element_type=F32)

    return _pcall(
        body, name=name, grid=(T // tm,),
        in_specs=[pl.BlockSpec((tm, K), lambda i: (i, 0)),
                  pl.BlockSpec((K, N), lambda i: (0, 0)),
                  pl.BlockSpec((tm, N), lambda i: (i, 0))],
        out_specs=pl.BlockSpec((tm, N), lambda i: (i, 0)),
        out_shape=_sds((T, N), F32),
        compiler_params=_params("parallel"),
    )(a, w, res)


def _mm_nt(a, w, *, out_dtype, name):
    T, N = a.shape
    K = w.shape[0]
    tm = _tile(T, 1056, 16)

    def body(a_ref, w_ref, o_ref):
        o_ref[...] = lax.dot_general(a_ref[...], w_ref[...], (((1,), (1,)), ((), ())),
                                     preferred_element_type=F32).astype(out_dtype)

    return _pcall(
        body, name=name, grid=(T // tm,),
        in_specs=[pl.BlockSpec((tm, N), lambda i: (i, 0)),
                  pl.BlockSpec((K, N), lambda i: (0, 0))],
        out_specs=pl.BlockSpec((tm, K), lambda i: (i, 0)),
        out_shape=_sds((T, K), out_dtype),
        compiler_params=_params("parallel"),
    )(a, w)


def _mm_tn(a, b, *, shards, relu2, name, slot=None, into=None, row_shards=0, out_dtype=BF16):
    T, Ka = a.shape
    Nb = b.shape[1]
    n = Nb // shards
    tka = _tile(Ka, 512, LANE)
    tnb = _tile(n, 512, LANE)
    nj = n // tnb

    def body(a_ref, b_ref, *rest):
        o_ref = rest[-1]
        av = a_ref[...]
        if relu2:
            r = jnp.maximum(av, 0)
            av = r * r
        o_ref[...] = lax.dot_general(av, b_ref[...], (((0,), (0,)), ((), ())),
                                     preferred_element_type=F32).astype(out_dtype)

    in_specs = [pl.BlockSpec((T, tka), lambda i, j: (0, i)),
                pl.BlockSpec((T, tnb), lambda i, j: (0, j))]
    args = [a, b]
    aliases = {}
    if slot is None:
        out_spec = pl.BlockSpec((None, tka, tnb), lambda i, j: (j // nj, i, j % nj))
        out_shape = _sds((shards, Ka, n), out_dtype)
    else:
        if row_shards:
            ni = Ka // row_shards // tka
            out_spec = pl.BlockSpec((None, None, tka, tnb), lambda i, j: (i // ni, slot, i % ni, j))
            out_shape = _sds((row_shards, 2, Ka // row_shards, n), out_dtype)
        else:
            out_spec = pl.BlockSpec((None, None, tka, tnb), lambda i, j: (j // nj, slot, i, j % nj))
            out_shape = _sds((shards, 2, Ka, n), out_dtype)
        if into is not None:
            in_specs.append(pl.BlockSpec(memory_space=pl.ANY))
            args.append(into)
            aliases = {2: 0}
    return _pcall(
        body, name=name, grid=(Ka // tka, shards * nj),
        in_specs=in_specs, out_specs=out_spec, out_shape=out_shape,
        input_output_aliases=aliases,
        compiler_params=_params("parallel", "parallel"),
    )(*args)


def _mm_nt_normbwd(dy, w, h, g, dres, *, name):
    T, D = h.shape
    S, _, n = w.shape
    tm = _tile(T, 1056, 16)

    def body(dy_ref, w_ref, h_ref, g_ref, dr_ref, dh_ref, dhb_ref, dg_ref, acc_ref):
        i, s = pl.program_id(0), pl.program_id(1)
        part = lax.dot_general(dy_ref[...], w_ref[...], (((1,), (1,)), ((), ())),
                               preferred_element_type=F32)

        @pl.when(s == 0)
        def _():
            acc_ref[...] = part

        @pl.when(s > 0)
        def _():
            acc_ref[...] += part

        @pl.when(s == S - 1)
        def _():
            dx, dg = _norm_bwd(h_ref[...], g_ref[...], acc_ref[...])
            dh = dr_ref[...] + dx
            dh_ref[...] = dh
            dhb_ref[...] = dh.astype(BF16)

            @pl.when(i == 0)
            def _():
                dg_ref[...] = dg

            @pl.when(i > 0)
            def _():
                dg_ref[...] += dg

    return _pcall(
        body, name=name, grid=(T // tm, S),
        in_specs=[pl.BlockSpec((tm, n), lambda i, s: (i, s)),
                  pl.BlockSpec((None, D, n), lambda i, s: (s, 0, 0)),
                  pl.BlockSpec((tm, D), lambda i, s: (i, 0)),
                  pl.BlockSpec((1, D), lambda i, s: (0, 0)),
                  pl.BlockSpec((tm, D), lambda i, s: (i, 0))],
        out_specs=[pl.BlockSpec((tm, D), lambda i, s: (i, 0)),
                   pl.BlockSpec((tm, D), lambda i, s: (i, 0)),
                   pl.BlockSpec((1, D), lambda i, s: (0, 0))],
        out_shape=[_sds((T, D), F32), _sds((T, D), BF16), _sds((1, D), F32)],
        scratch_shapes=[pltpu.VMEM((tm, D), F32)],
        compiler_params=_params("arbitrary", "arbitrary"),
    )(dy, w, h, g, dres)


def _row_chains(tm):
    first = (tm // 2 + 15) // 16 * 16
    return [slice(0, first), slice(first, tm)] if 0 < first < tm else [slice(0, tm)]


def _mlp_fwd(h, g, w_up, w_down, *, layer, name):
    T, D = h.shape
    S, _, _, n = w_up.shape
    tm = _tile(T, 528, 16)
    tf = _tile(n, 1024, LANE)
    nj = n // tf
    nf = S * nj
    chains = _row_chains(tm)

    def body(h_ref, g_ref, wu_ref, wd_ref, o_ref, hn_ref, up_ref, acc_ref):
        f = pl.program_id(1)

        @pl.when(f == 0)
        def _():
            x = h_ref[...]
            hn_ref[...] = (x * _rstd(x) * g_ref[...]).astype(BF16)

        parts = []
        for rows in chains:
            up = jnp.dot(hn_ref[rows, :], wu_ref[...], preferred_element_type=F32)
            up_ref[rows, :] = up.astype(BF16)
            r = jnp.maximum(up, 0.0)
            parts.append(jnp.dot((r * r).astype(BF16), wd_ref[...], preferred_element_type=F32))
        part = jnp.concatenate(parts, axis=0)

        @pl.when(f == 0)
        def _():
            acc_ref[...] = part

        @pl.when(f > 0)
        def _():
            acc_ref[...] += part

        @pl.when(f == nf - 1)
        def _():
            o_ref[...] = h_ref[...] + acc_ref[...]

    return _pcall(
        body, name=name, grid=(T // tm, nf),
        in_specs=[pl.BlockSpec((tm, D), lambda i, f: (i, 0)),
                  pl.BlockSpec((1, D), lambda i, f: (0, 0)),
                  pl.BlockSpec((None, None, D, tf), lambda i, f: (f // nj, layer, 0, f % nj)),
                  pl.BlockSpec((None, None, tf, D), lambda i, f: (f // nj, layer, f % nj, 0))],
        out_specs=[pl.BlockSpec((tm, D), lambda i, f: (i, 0)),
                   pl.BlockSpec((tm, D), lambda i, f: (i, 0)),
                   pl.BlockSpec((tm, tf), lambda i, f: (i, f))],
        out_shape=[_sds((T, D), F32), _sds((T, D), BF16), _sds((T, S * n), BF16)],
        scratch_shapes=[pltpu.VMEM((tm, D), F32)],
        compiler_params=_params("parallel", "arbitrary"),
    )(h, g, w_up, w_down)


def _mlp_bwd(dy, h, g, up, w_up, w_down, *, layer, name):
    T, D = h.shape
    S, _, _, n = w_up.shape
    tm = _tile(T, 528, 16)
    tf = _tile(n, 1024, LANE)
    nj = n // tf
    nf = S * nj
    chains = _row_chains(tm)

    def body(dy_ref, h_ref, g_ref, up_ref, wu_ref, wd_ref, dup_ref, dh_ref, dhb_ref, dg_ref,
             dyb_ref, acc_ref):
        i, f = pl.program_id(0), pl.program_id(1)

        @pl.when(f == 0)
        def _():
            dyb_ref[...] = dy_ref[...].astype(BF16)

        parts = []
        for rows in chains:
            dact = lax.dot_general(dyb_ref[rows, :], wd_ref[...], (((1,), (1,)), ((), ())),
                                   preferred_element_type=F32)
            r = jnp.maximum(up_ref[rows, :].astype(F32), 0.0)
            dup = (dact * (2.0 * r)).astype(BF16)
            dup_ref[rows, :] = dup
            parts.append(lax.dot_general(dup, wu_ref[...], (((1,), (1,)), ((), ())),
                                         preferred_element_type=F32))
        part = jnp.concatenate(parts, axis=0)

        @pl.when(f == 0)
        def _():
            acc_ref[...] = part

        @pl.when(f > 0)
        def _():
            acc_ref[...] += part

        @pl.when(f == nf - 1)
        def _():
            dx, dg = _norm_bwd(h_ref[...], g_ref[...], acc_ref[...])
            dh = dy_ref[...] + dx
            dh_ref[...] = dh
            dhb_ref[...] = dh.astype(BF16)

            @pl.when(i == 0)
            def _():
                dg_ref[...] = dg

            @pl.when(i > 0)
            def _():
                dg_ref[...] += dg

    return _pcall(
        body, name=name, grid=(T // tm, nf),
        in_specs=[pl.BlockSpec((tm, D), lambda i, f: (i, 0)),
                  pl.BlockSpec((tm, D), lambda i, f: (i, 0)),
                  pl.BlockSpec((1, D), lambda i, f: (0, 0)),
                  pl.BlockSpec((tm, tf), lambda i, f: (i, f)),
                  pl.BlockSpec((None, None, D, tf), lambda i, f: (f // nj, layer, 0, f % nj)),
                  pl.BlockSpec((None, None, tf, D), lambda i, f: (f // nj, layer, f % nj, 0))],
        out_specs=[pl.BlockSpec((tm, tf), lambda i, f: (i, f)),
                   pl.BlockSpec((tm, D), lambda i, f: (i, 0)),
                   pl.BlockSpec((tm, D), lambda i, f: (i, 0)),
                   pl.BlockSpec((1, D), lambda i, f: (0, 0))],
        out_shape=[_sds((T, S * n), BF16), _sds((T, D), F32), _sds((T, D), BF16), _sds((1, D), F32)],
        scratch_shapes=[pltpu.VMEM((tm, D), BF16), pltpu.VMEM((tm, D), F32)],
        compiler_params=_params("arbitrary", "arbitrary"),
    )(dy, h, g, up, w_up, w_down)


ATT_BLOCK = 128
ATT_HEADS = 4


def _attn_tile(T):
    for w in (3 * ATT_BLOCK, 2 * ATT_BLOCK):
        if T % w == 0:
            return w
    return ATT_BLOCK


def _tri(strict_lower, value):
    B = ATT_BLOCK
    r = lax.broadcasted_iota(jnp.int32, (2 * B, B), 0)
    r = jnp.where(r >= B, r - B, r)
    c = lax.broadcasted_iota(jnp.int32, (2 * B, B), 1)
    m = (r > c) if strict_lower else (r < c)
    return jnp.where(m, value, 0.0).astype(BF16)


def _split_dot(x, tri):
    hi = lax.bitcast_convert_type(lax.bitcast_convert_type(x, jnp.uint32) & jnp.uint32(0xFFFF0000), F32)
    lo = x - hi
    return jnp.dot(jnp.concatenate([hi.astype(BF16), lo.astype(BF16)], axis=1), tri,
                   preferred_element_type=F32)


def _causal_mask(W):
    r = lax.broadcasted_iota(jnp.int32, (W, W), 0)
    c = lax.broadcasted_iota(jnp.int32, (W, W), 1)
    return c < r


def _attn_scores(qs, ks, carry, tri_neg, masked, want_sig):
    W = qs.shape[0]
    B = ATT_BLOCK
    z = lax.dot_general(qs, ks, (((1,), (1,)), ((), ())), preferred_element_type=F32)
    minus_abs = lax.bitcast_convert_type(
        lax.bitcast_convert_type(z, jnp.uint32) | jnp.uint32(0x80000000), F32)
    sp = jnp.maximum(z, 0.0) + jnp.log(1.0 + jnp.exp(minus_abs))
    logsig = z - sp
    if masked:
        causal = _causal_mask(W)
        sp = jnp.where(causal, sp, 0.0)
    afters = []
    for b in reversed(range(W // B)):
        blk = sp[:, b * B:(b + 1) * B]
        within = _split_dot(blk, tri_neg)
        afters.append(within + carry)
        carry = carry + (within[:, 0:1] - blk[:, 0:1])
    after = jnp.concatenate(afters[::-1], axis=1)
    w = jnp.exp(logsig + after)
    if masked:
        w = jnp.where(causal, w, 0.0)
    return w, (jnp.exp(logsig) if want_sig else None), carry


def _ride_along(plan, npairs, nq):
    if plan is None:
        return 0, (lambda refs: None), (lambda refs: None)
    nx = plan.n

    def before(refs):
        p, i = pl.program_id(0), pl.program_id(1)

        @pl.when(jnp.logical_and(p == 0, i == 0))
        def _():
            plan.start(*refs)

        @pl.when(jnp.logical_and(p == npairs // 2, i == 0))
        def _():
            plan.mid(*refs)

    def after(refs):
        p, i = pl.program_id(0), pl.program_id(1)

        @pl.when(jnp.logical_and(p == npairs - 1, i == nq - 1))
        def _():
            plan.finish(*refs)

    return nx, before, after


def _saved_tile(i, j):
    return i * (i + 1) // 2 + j


def _attn_fwd(qkv, *, name, plan=None):
    T = qkv.shape[0]
    D = qkv.shape[1] // 3
    W = _attn_tile(T)
    H = ATT_HEADS
    lanes = H * HEAD_DIM
    ngroups = D // lanes
    nq = T // W
    ntri = nq * (nq + 1) // 2
    scale = HEAD_DIM ** -0.5
    nx, before, after = _ride_along(plan, ngroups, nq)

    def body(*refs):
        q_ref, k_ref, v_ref = refs[:3]
        o_ref, wsv_ref, ssv_ref = refs[3 + nx:6 + nx]
        stage_w, stage_s, stage_sems = refs[6 + 2 * nx:9 + 2 * nx]
        ride = (refs[3:3 + nx], refs[6 + nx:6 + 2 * nx], refs[9 + 2 * nx:])
        before(ride)
        p, i = pl.program_id(0), pl.program_id(1)
        tri = _tri(True, -1.0)
        heads = [slice(hh * HEAD_DIM, (hh + 1) * HEAD_DIM) for hh in range(H)]
        qs = [q_ref[:, cols] * scale for cols in heads]

        def save(slot, j):
            dst = _saved_tile(i, j)
            return [pltpu.make_async_copy(stage.at[slot], sv.at[pl.ds(p * H, H), dst], stage_sems.at[slot, a])
                    for a, (sv, stage) in enumerate(((wsv_ref, stage_w), (ssv_ref, stage_s)))]

        def tile(t, state, masked):
            j = i - t
            rows = pl.ds(pl.multiple_of(j * W, W), W)
            slot = t % 2
            if not masked:
                @pl.when(t >= 2)
                def _():
                    for cp in save(slot, j):
                        cp.wait()
            out = []
            for hh, (cols, q, (carry, acc)) in enumerate(zip(heads, qs, state)):
                w, sig, carry = _attn_scores(q, k_ref[rows, cols], carry, tri, masked, True)
                wb = w.astype(BF16)
                stage_w[slot, hh] = wb
                stage_s[slot, hh] = sig.astype(BF16)
                acc = acc + jnp.dot(wb, v_ref[rows, cols], preferred_element_type=F32)
                out.append((carry, acc))
            for cp in save(slot, j):
                cp.start()
            return tuple(out)

        zero = (jnp.zeros((W, 1), F32), jnp.zeros((W, HEAD_DIM), F32))
        state = tile(0, (zero,) * H, True)
        state = lax.fori_loop(1, i + 1, lambda t, st: tile(t, st, False), state)
        for cols, (_, acc) in zip(heads, state):
            o_ref[:, cols] = acc.astype(BF16)
        for cp in save(i % 2, 0):
            cp.wait()

        @pl.when(i >= 1)
        def _():
            for cp in save((i + 1) % 2, 0):
                cp.wait()

        after(ride)

    extra = plan.inputs if plan else []
    saved = _sds((D // HEAD_DIM, ntri, W, W), BF16)
    return _pcall(
        body, name=name, grid=(ngroups, nq),
        in_specs=[pl.BlockSpec((W, lanes), lambda p, i: (i, p)),
                  pl.BlockSpec((T, lanes), lambda p, i: (0, ngroups + p)),
                  pl.BlockSpec((T, lanes), lambda p, i: (0, 2 * ngroups + p))] + [ANY] * nx,
        out_specs=[pl.BlockSpec((W, lanes), lambda p, i: (i, p)), ANY, ANY] + [ANY] * nx,
        out_shape=[_sds((T, D), BF16), saved, saved] + (plan.out_shape if plan else []),
        scratch_shapes=[pltpu.VMEM((2, H, W, W), BF16), pltpu.VMEM((2, H, W, W), BF16),
                        pltpu.SemaphoreType.DMA((2, 2))] + (plan.scratch if plan else []),
        compiler_params=_params("arbitrary", "arbitrary"),
    )(qkv, qkv, qkv, *extra)


def _attn_bwd(qkv, do, w_sv, s_sv, *, name, plan=None):
    T = qkv.shape[0]
    D = qkv.shape[1] // 3
    B = ATT_BLOCK
    W = _attn_tile(T)
    H = ATT_HEADS
    lanes = H * HEAD_DIM
    ngroups = D // lanes
    nq = T // W
    scale = HEAD_DIM ** -0.5
    nx, before, after = _ride_along(plan, ngroups, nq)

    def body(*refs):
        q_ref, k_ref, v_ref, do_ref, wsv_ref, ssv_ref = refs[:6]
        dq_ref, dk_ref, dv_ref = refs[6 + nx:9 + nx]
        dk_acc, dv_acc, stage_w, stage_s, stage_sems = refs[9 + 2 * nx:14 + 2 * nx]
        ride = (refs[6:6 + nx], refs[9 + nx:9 + 2 * nx], refs[14 + 2 * nx:])
        before(ride)
        p, i = pl.program_id(0), pl.program_id(1)

        @pl.when(i == 0)
        def _():
            dk_acc[...] = jnp.zeros_like(dk_acc)
            dv_acc[...] = jnp.zeros_like(dv_acc)

        def fetch(slot, j):
            src = _saved_tile(i, j)
            return [pltpu.make_async_copy(sv.at[pl.ds(p * H, H), src], stage.at[slot], stage_sems.at[slot, a])
                    for a, (sv, stage) in enumerate(((wsv_ref, stage_w), (ssv_ref, stage_s)))]

        tri_before = _tri(False, 1.0)
        heads = [slice(hh * HEAD_DIM, (hh + 1) * HEAD_DIM) for hh in range(H)]
        qs = [q_ref[:, cols] * scale for cols in heads]
        douts = [do_ref[:, cols] for cols in heads]

        def grad(j, state, masked):
            rows = pl.ds(pl.multiple_of(j * W, W), W)
            slot = j % 2
            for cp in fetch(slot, j):
                cp.wait()
            if not masked:
                for cp in fetch(1 - slot, j + 1):
                    cp.start()
            out = []
            for hh, (cols, (gsum, dq)) in enumerate(zip(heads, state)):
                wb = stage_w[slot, hh]
                sig = stage_s[slot, hh].astype(F32)
                dw = lax.dot_general(douts[hh], v_ref[rows, cols], (((1,), (1,)), ((), ())),
                                     preferred_element_type=F32)
                g = dw * wb.astype(F32)
                befores = []
                for b in range(W // B):
                    blk = g[:, b * B:(b + 1) * B]
                    within = _split_dot(blk, tri_before)
                    befores.append(within + gsum)
                    gsum = gsum + (within[:, B - 1:B] + blk[:, B - 1:B])
                dz = g - sig * (g + jnp.concatenate(befores, axis=1))
                if masked:
                    dz = jnp.where(_causal_mask(W), dz, 0.0)
                dzb = dz.astype(BF16)
                dq = dq + jnp.dot(dzb, k_ref[rows, cols] * scale, preferred_element_type=F32)
                dk_acc[rows, cols] += lax.dot_general(dzb, qs[hh], (((0,), (0,)), ((), ())),
                                                      preferred_element_type=F32)
                dv_acc[rows, cols] += lax.dot_general(wb, douts[hh], (((0,), (0,)), ((), ())),
                                                      preferred_element_type=F32)
                out.append((gsum, dq))
            return tuple(out)

        for cp in fetch(0, 0):
            cp.start()
        zero = (jnp.zeros((W, 1), F32), jnp.zeros((W, HEAD_DIM), F32))
        state = lax.fori_loop(0, i, lambda j, st: grad(j, st, False), (zero,) * H)
        state = grad(i, state, True)
        for cols, (_, dq) in zip(heads, state):
            dq_ref[:, cols] = dq.astype(BF16)

        @pl.when(i == nq - 1)
        def _():
            dk_ref[...] = dk_acc[...].astype(BF16)
            dv_ref[...] = dv_acc[...].astype(BF16)

        after(ride)

    extra = plan.inputs if plan else []
    return _pcall(
        body, name=name, grid=(ngroups, nq),
        in_specs=[pl.BlockSpec((W, lanes), lambda p, i: (i, p)),
                  pl.BlockSpec((T, lanes), lambda p, i: (0, ngroups + p)),
                  pl.BlockSpec((T, lanes), lambda p, i: (0, 2 * ngroups + p)),
                  pl.BlockSpec((W, lanes), lambda p, i: (i, p)), ANY, ANY] + [ANY] * nx,
        out_specs=[pl.BlockSpec((W, lanes), lambda p, i: (i, p)),
                   pl.BlockSpec((T, lanes), lambda p, i: (0, p)),
                   pl.BlockSpec((T, lanes), lambda p, i: (0, p))] + [ANY] * nx,
        out_shape=[_sds((T, D), BF16)] * 3 + (plan.out_shape if plan else []),
        scratch_shapes=[pltpu.VMEM((T, lanes), F32), pltpu.VMEM((T, lanes), F32),
                        pltpu.VMEM((2, H, W, W), BF16), pltpu.VMEM((2, H, W, W), BF16),
                        pltpu.SemaphoreType.DMA((2, 2))]
        + (plan.scratch if plan else []),
        compiler_params=_params("arbitrary", "arbitrary"),
    )(qkv, qkv, qkv, do, w_sv, s_sv, *extra)


HALO = SUBLANE


def _lru_gates(u, w_rg, b_rg, w_ig, b_ig, lam):
    nb = w_rg.shape[0]
    pre_r, pre_i = [], []
    for n in range(nb):
        ub = u[:, n * LANE:(n + 1) * LANE].astype(BF16)
        pre_r.append(jnp.dot(ub, w_rg[n], preferred_element_type=F32))
        pre_i.append(jnp.dot(ub, w_ig[n], preferred_element_type=F32))
    r = _sigmoid(jnp.concatenate(pre_r, axis=1) + b_rg)
    i = _sigmoid(jnp.concatenate(pre_i, axis=1) + b_ig)
    c = -LRU_C * _softplus_parts(-lam)[0]
    log_a = c * r
    a = jnp.exp(log_a)
    x2 = 2.0 * log_a
    em1 = jnp.where(jnp.abs(x2) < 1e-2, x2 * (1.0 + x2 * (0.5 + x2 * (1.0 / 6.0))), jnp.exp(x2) - 1.0)
    mult = jnp.sqrt(-em1)
    return r, i, a, mult, c


def _conv_rows(buf_ref, tt, conv_w, conv_b):
    u = conv_b
    for j in range(4):
        u = u + buf_ref[pl.ds(HALO - 3 + j, tt), :] * conv_w[j:j + 1, :]
    return u


def _fill_with_halo(buf_ref, prev_ref, cur_ref, first):
    tt = cur_ref.shape[0]
    buf_ref[pl.ds(0, HALO), :] = jnp.where(first, 0.0, prev_ref[...])
    buf_ref[pl.ds(HALO, tt), :] = cur_ref[...]


def _lru_time_tile(T):
    return _tile(T, 256, SUBLANE)


def _lru_pre(gr, conv_w, conv_b, w_rg, b_rg, w_ig, b_ig, lam, *, name):
    T = gr.shape[0]
    D = gr.shape[1] // 2
    tt = _lru_time_tile(T)
    hb = tt // HALO

    def body(x_ref, xp_ref, cw_ref, cb_ref, wr_ref, br_ref, wi_ref, bi_ref, lam_ref, a_ref, b_ref, buf):
        _fill_with_halo(buf, xp_ref, x_ref, pl.program_id(0) == 0)
        u = _conv_rows(buf, tt, cw_ref[...], cb_ref[...])
        _, i, a, mult, _ = _lru_gates(u, wr_ref, br_ref[...], wi_ref, bi_ref[...], lam_ref[...])
        a_ref[...] = a
        b_ref[...] = mult * (i * u)

    vec = pl.BlockSpec((1, D), lambda t: (0, 0))
    mat = pl.BlockSpec(w_rg.shape, lambda t: (0, 0, 0))
    return _pcall(
        body, name=name, grid=(T // tt,),
        in_specs=[pl.BlockSpec((tt, D), lambda t: (t, 1)),
                  pl.BlockSpec((HALO, D), lambda t: (jnp.maximum(t * hb - 1, 0), 1)),
                  pl.BlockSpec((4, D), lambda t: (0, 0)), vec, mat, vec, mat, vec, vec],
        out_specs=[pl.BlockSpec((tt, D), lambda t: (t, 0))] * 2,
        out_shape=[_sds((T, D), F32)] * 2,
        scratch_shapes=[pltpu.VMEM((tt + HALO, D), F32)],
        compiler_params=_params("parallel"),
    )(gr, gr, conv_w, conv_b, w_rg, b_rg, w_ig, b_ig, lam)


def _lru_scan(a, b, *, reverse, name):
    T, D = a.shape
    R = SUBLANE
    ts = _tile(T, 528, R)
    nt = T // ts

    def body(a_ref, b_ref, o_ref, carry):
        @pl.when(pl.program_id(0) == 0)
        def _():
            carry[...] = jnp.zeros_like(carry)

        rowid = lax.broadcasted_iota(jnp.int32, (R, D), 0)

        def chunk(k, run):
            if reverse:
                k = ts // R - 1 - k
            rows = pl.ds(pl.multiple_of(k * R, R), R)
            at, bt = a_ref[rows, :], b_ref[rows, :]
            out = jnp.zeros((R, D), F32)
            for r in (range(R - 1, -1, -1) if reverse else range(R)):
                if reverse:
                    cand = bt + run
                    nxt = at * cand
                else:
                    cand = at * run + bt
                    nxt = cand
                out = jnp.where(rowid == r, cand, out)
                run = jnp.broadcast_to(nxt[r:r + 1, :], (R, D))
            o_ref[rows, :] = out
            return run

        carry[...] = lax.fori_loop(0, ts // R, chunk, carry[...])

    if reverse:
        spec = pl.BlockSpec((ts, D), lambda t: (nt - 1 - t, 0))
    else:
        spec = pl.BlockSpec((ts, D), lambda t: (t, 0))
    return _pcall(
        body, name=name, grid=(nt,),
        in_specs=[spec, spec], out_specs=spec,
        out_shape=_sds((T, D), F32),
        scratch_shapes=[pltpu.VMEM((R, D), F32)],
        compiler_params=_params("arbitrary"),
    )(a, b)


def _lru_out(gr, hs, w, res, *, name):
    T, D = hs.shape
    tt = _tile(T, 528, 16)

    def body(g_ref, h_ref, w_ref, r_ref, y_ref, o_ref):
        y = (h_ref[...] * _gelu_parts(g_ref[...])[0]).astype(BF16)
        y_ref[...] = y
        o_ref[...] = r_ref[...] + jnp.dot(y, w_ref[...], preferred_element_type=F32)

    blk = pl.BlockSpec((tt, D), lambda t: (t, 0))
    return _pcall(
        body, name=name, grid=(T // tt,),
        in_specs=[blk, blk, pl.BlockSpec((D, D), lambda t: (0, 0)), blk],
        out_specs=[blk, blk],
        out_shape=[_sds((T, D), BF16), _sds((T, D), F32)],
        compiler_params=_params("parallel"),
    )(gr, hs, w, res)


def _lru_out_bwd(gr, hs, dout, w, *, name):
    T, D = hs.shape
    tt = _tile(T, 528, 16)

    def body(g_ref, h_ref, d_ref, w_ref, dg_ref, dh_ref):
        dy = lax.dot_general(d_ref[...], w_ref[...], (((1,), (1,)), ((), ())), preferred_element_type=F32)
        gelu, dgelu = _gelu_parts(g_ref[...])
        dg_ref[...] = (dy * h_ref[...] * dgelu).astype(BF16)
        dh_ref[...] = dy * gelu

    spec = pl.BlockSpec((tt, D), lambda t: (t, 0))
    return _pcall(
        body, name=name, grid=(T // tt,),
        in_specs=[spec, spec, spec, pl.BlockSpec((D, D), lambda t: (0, 0))], out_specs=[spec, spec],
        out_shape=[_sds((T, 2 * D), BF16), _sds((T, D), F32)],
        compiler_params=_params("parallel"),
    )(gr, hs, dout, w)


def _lru_gate_bwd(gr, hs, lmb, conv_w, conv_b, w_rg, b_rg, w_ig, b_ig, lam, *, name):
    T, D = hs.shape
    nb = w_rg.shape[0]
    tt = _lru_time_tile(T)
    hb = tt // HALO
    nt = T // tt

    def body(x_ref, xp_ref, h_ref, hp_ref, l_ref, cw_ref, cb_ref, wr_ref, br_ref, wi_ref, bi_ref, lam_ref,
             du_ref, dwr_ref, dbr_ref, dwi_ref, dbi_ref, dlam_ref, xbuf, hbuf):
        t = pl.program_id(0)
        first = t == 0
        _fill_with_halo(xbuf, xp_ref, x_ref, first)
        _fill_with_halo(hbuf, hp_ref, h_ref, first)
        u = _conv_rows(xbuf, tt, cw_ref[...], cb_ref[...])
        lam_v = lam_ref[...]
        r, i, a, mult, c = _lru_gates(u, wr_ref, br_ref[...], wi_ref, bi_ref[...], lam_v)
        l = l_ref[...]
        h_prev = hbuf[pl.ds(HALO - 1, tt), :]
        dlog_a = l * h_prev * a - l * (i * u) * (a * a) / mult
        d_iu = l * mult
        dpre_r = (dlog_a * c) * (r * (1.0 - r))
        dpre_i = (d_iu * u) * (i * (1.0 - i))
        dpr_b = dpre_r.astype(BF16)
        dpi_b = dpre_i.astype(BF16)
        du_parts, dwr, dwi = [], [], []
        for n in range(nb):
            cs = slice(n * LANE, (n + 1) * LANE)
            ub = u[:, cs].astype(BF16)
            du_parts.append(
                lax.dot_general(dpr_b[:, cs], wr_ref[n], (((1,), (1,)), ((), ())), preferred_element_type=F32)
                + lax.dot_general(dpi_b[:, cs], wi_ref[n], (((1,), (1,)), ((), ())), preferred_element_type=F32))
            dwr.append(lax.dot_general(ub, dpr_b[:, cs], (((0,), (0,)), ((), ())), preferred_element_type=F32))
            dwi.append(lax.dot_general(ub, dpi_b[:, cs], (((0,), (0,)), ((), ())), preferred_element_type=F32))
        du_ref[...] = d_iu * i + jnp.concatenate(du_parts, axis=1)
        dbr = jnp.sum(dpre_r, axis=0, keepdims=True)
        dbi = jnp.sum(dpre_i, axis=0, keepdims=True)
        dc = jnp.sum(dlog_a * r, axis=0, keepdims=True)

        @pl.when(first)
        def _():
            for n in range(nb):
                dwr_ref[n] = dwr[n]
                dwi_ref[n] = dwi[n]
            dbr_ref[...] = dbr
            dbi_ref[...] = dbi
            dlam_ref[...] = dc

        @pl.when(t > 0)
        def _():
            for n in range(nb):
                dwr_ref[n] += dwr[n]
                dwi_ref[n] += dwi[n]
            dbr_ref[...] += dbr
            dbi_ref[...] += dbi
            dlam_ref[...] += dc

        @pl.when(t == nt - 1)
        def _():
            dlam_ref[...] = dlam_ref[...] * (LRU_C * _sigmoid(-lam_v))

    vec = pl.BlockSpec((1, D), lambda t: (0, 0))
    mat = pl.BlockSpec(w_rg.shape, lambda t: (0, 0, 0))
    blk = pl.BlockSpec((tt, D), lambda t: (t, 0))
    prev = pl.BlockSpec((HALO, D), lambda t: (jnp.maximum(t * hb - 1, 0), 0))
    return _pcall(
        body, name=name, grid=(nt,),
        in_specs=[pl.BlockSpec((tt, D), lambda t: (t, 1)),
                  pl.BlockSpec((HALO, D), lambda t: (jnp.maximum(t * hb - 1, 0), 1)),
                  blk, prev, blk,
                  pl.BlockSpec((4, D), lambda t: (0, 0)), vec, mat, vec, mat, vec, vec],
        out_specs=[blk, mat, vec, mat, vec, vec],
        out_shape=[_sds((T, D), F32), _sds(w_rg.shape, F32), _sds((1, D), F32),
                   _sds(w_rg.shape, F32), _sds((1, D), F32), _sds((1, D), F32)],
        scratch_shapes=[pltpu.VMEM((tt + HALO, D), F32), pltpu.VMEM((tt + HALO, D), F32)],
        compiler_params=_params("arbitrary"),
    )(gr, gr, hs, hs, lmb, conv_w, conv_b, w_rg, b_rg, w_ig, b_ig, lam)


def _lru_conv_bwd(gr, du, conv_w, dgr, *, name):
    T, D = du.shape
    tt = _lru_time_tile(T)
    hb = tt // HALO
    nt = T // tt

    def body(x_ref, xp_ref, du_ref, dun_ref, cw_ref, _, dx_ref, dcw_ref, dcb_ref, xbuf, dbuf):
        t = pl.program_id(0)
        _fill_with_halo(xbuf, xp_ref, x_ref, t == 0)
        du = du_ref[...]
        dbuf[pl.ds(0, tt), :] = du
        dbuf[pl.ds(tt, HALO), :] = jnp.where(t == nt - 1, 0.0, dun_ref[...])
        cw = cw_ref[...]
        dx = jnp.zeros((tt, D), F32)
        dcw = []
        for j in range(4):
            dx = dx + dbuf[pl.ds(3 - j, tt), :] * cw[j:j + 1, :]
            dcw.append(jnp.sum(du * xbuf[pl.ds(HALO - 3 + j, tt), :], axis=0, keepdims=True))
        dx_ref[...] = dx.astype(BF16)
        dcw = jnp.concatenate(dcw, axis=0)
        dcb = jnp.sum(du, axis=0, keepdims=True)

        @pl.when(t == 0)
        def _():
            dcw_ref[...] = dcw
            dcb_ref[...] = dcb

        @pl.when(t > 0)
        def _():
            dcw_ref[...] += dcw
            dcb_ref[...] += dcb

    blk = pl.BlockSpec((tt, D), lambda t: (t, 0))
    return _pcall(
        body, name=name, grid=(nt,),
        in_specs=[pl.BlockSpec((tt, D), lambda t: (t, 1)),
                  pl.BlockSpec((HALO, D), lambda t: (jnp.maximum(t * hb - 1, 0), 1)),
                  blk,
                  pl.BlockSpec((HALO, D), lambda t: (jnp.minimum((t + 1) * hb, T // HALO - 1), 0)),
                  pl.BlockSpec((4, D), lambda t: (0, 0)), ANY],
        out_specs=[pl.BlockSpec((tt, D), lambda t: (t, 1)),
                   pl.BlockSpec((4, D), lambda t: (0, 0)), pl.BlockSpec((1, D), lambda t: (0, 0))],
        out_shape=[_sds((T, 2 * D), BF16), _sds((4, D), F32), _sds((1, D), F32)],
        input_output_aliases={5: 0},
        scratch_shapes=[pltpu.VMEM((tt + HALO, D), F32), pltpu.VMEM((tt + HALO, D), F32)],
        compiler_params=_params("arbitrary"),
    )(gr, gr, du, du, conv_w, dgr)


def _loss_head(h, g, target, *, row_lo, row_hi, name):
    T, D = h.shape
    tm = _tile(T, 1056, 16)

    def body(h_ref, g_ref, t_ref, loss_ref, dh_ref, dhb_ref, dg_ref):
        i = pl.program_id(0)
        x = h_ref[...]
        g = g_ref[...]
        row = i * tm + lax.broadcasted_iota(jnp.int32, (tm, 1), 0)
        valid = jnp.logical_and(row >= row_lo, row < row_hi)
        rstd = _rstd(x)
        n = x * rstd
        err = jnp.where(valid, n * g - t_ref[...], 0.0)
        part = (0.5 / D) * jnp.sum(jnp.sum(err * err, axis=1, keepdims=True), axis=0, keepdims=True)
        dy = err * (1.0 / D)
        dn = dy * g
        dh = rstd * (dn - n * jnp.mean(dn * n, axis=-1, keepdims=True))
        dh_ref[...] = dh
        dhb_ref[...] = dh.astype(BF16)
        dg = jnp.sum(dy * n, axis=0, keepdims=True)

        @pl.when(i == 0)
        def _():
            loss_ref[...] = part
            dg_ref[...] = dg

        @pl.when(i > 0)
        def _():
            loss_ref[...] += part
            dg_ref[...] += dg

    blk = pl.BlockSpec((tm, D), lambda i: (i, 0))
    vec = pl.BlockSpec((1, D), lambda i: (0, 0))
    return _pcall(
        body, name=name, grid=(T // tm,),
        in_specs=[blk, vec, blk],
        out_specs=[pl.BlockSpec((1, 1), lambda i: (0, 0)), blk, blk, vec],
        out_shape=[_sds((1, 1), F32), _sds((T, D), F32), _sds((T, D), BF16), _sds((1, D), F32)],
        compiler_params=_params("arbitrary"),
    )(h, g, target)


def _adamw_math(w, g, m, v):
    c1 = 1.0 / (1.0 - ADAM_B1 ** ADAM_STEP)
    c2 = 1.0 / (1.0 - ADAM_B2 ** ADAM_STEP)
    m = ADAM_B1 * m + (1.0 - ADAM_B1) * g
    v = ADAM_B2 * v + (1.0 - ADAM_B2) * (g * g)
    delta = -ADAM_LR * ((m * c1) / (jnp.sqrt(v * c2) + ADAM_EPS) + ADAM_WD * w)
    return delta, m, v


def _adamw_halves(w, mine, theirs, m, v, c, *, name):
    _, R, C = w.shape
    tr = _tile(R, 256, SUBLANE)

    def body(c_ref, w_ref, a_ref, b_ref, m_ref, v_ref, g_ref, d_ref, nm_ref, nv_ref):
        g = jnp.where(pl.program_id(0) == c_ref[0], a_ref[...], b_ref[...])
        g_ref[...] = g
        d_ref[...], nm_ref[...], nv_ref[...] = _adamw_math(w_ref[...], g, m_ref[...], v_ref[...])

    full = pl.BlockSpec((None, tr, C), lambda h, i, c_ref: (h, i, 0))
    half = pl.BlockSpec((tr, C), lambda h, i, c_ref: (i, 0))
    return _pcall(
        body, name=name,
        grid_spec=pltpu.PrefetchScalarGridSpec(
            num_scalar_prefetch=1, grid=(2, R // tr),
            in_specs=[full, half, half, full, full], out_specs=[full] * 4),
        out_shape=[_sds((2, R, C), F32)] * 4,
        compiler_params=_params("parallel", "parallel"),
    )(c, w, mine, theirs, m, v)


def _adamw(w, g, m, v, *, name):
    R, C = w.shape
    tr = _tile(R, 512, SUBLANE)

    def body(w_ref, g_ref, m_ref, v_ref, d_ref, nm_ref, nv_ref):
        d_ref[...], nm_ref[...], nv_ref[...] = _adamw_math(w_ref[...], g_ref[...], m_ref[...], v_ref[...])

    blk = pl.BlockSpec((tr, C), lambda i: (i, 0))
    return _pcall(
        body, name=name, grid=(R // tr,),
        in_specs=[blk] * 4, out_specs=[blk] * 3,
        out_shape=[_sds((R, C), F32)] * 3,
        compiler_params=_params("parallel"),
    )(w, g, m, v)


ANY = pl.BlockSpec(memory_space=pl.ANY)


def _place():
    x, y, c = lax.axis_index("x"), lax.axis_index("y"), lax.axis_index("c")
    chips = [(1 - x, y), (x, 1 - y), (1 - x, 1 - y)]
    return x, y, c, chips


LOCAL_PIECES = 4


class _GatherChips:
    def __init__(self, vs):
        self.inputs = list(vs)
        n = self.n = len(vs)
        self.out_shape = [_sds((4,) + v.shape, v.dtype) for v in vs]
        self.scratch = [pltpu.SemaphoreType.DMA((6 * n,)), pltpu.SemaphoreType.DMA((6 * n,)),
                        pltpu.SemaphoreType.DMA((LOCAL_PIECES * n,))]

    def _copies(self, v_refs, o_refs, sems):
        send_sems, recv_sems, local_sems = sems
        x, y, c, chips = _place()
        me = 2 * x + y

        def copy(a, k, block, half, to, src=None):
            dst = o_refs[a].at[block, half]
            return pltpu.make_async_remote_copy(
                src_ref=dst if src is None else src, dst_ref=dst,
                send_sem=send_sems.at[6 * a + k], recv_sem=recv_sems.at[6 * a + k],
                device_id=to, device_id_type=MESH)

        ks = [(a, k, cx, cy) for a in range(self.n) for k, (cx, cy) in enumerate(chips)]

        def local():
            out = []
            for a in range(self.n):
                rows = self.inputs[a].shape[1] // (LOCAL_PIECES // 2)
                for p in range(LOCAL_PIECES):
                    h, r0 = p % 2, (p // 2) * rows
                    out.append(pltpu.make_async_copy(
                        v_refs[a].at[h, pl.ds(r0, rows)], o_refs[a].at[me, h, pl.ds(r0, rows)],
                        local_sems.at[LOCAL_PIECES * a + p]))
            return out

        return dict(
            first=lambda: [copy(a, k, me, c, (cx, cy, c), src=v_refs[a].at[c]) for a, k, cx, cy in ks],
            landed=lambda: [copy(a, k, 2 * cx + cy, c, (x, y, c)) for a, k, cx, cy in ks],
            passed=lambda: [copy(a, 3 + k, 2 * cx + cy, c, (x, y, 1 - c)) for a, k, cx, cy in ks],
            final=lambda: [copy(a, 3 + k, 2 * cx + cy, 1 - c, (x, y, c)) for a, k, cx, cy in ks],
            local=local)

    def start(self, v_refs, o_refs, sems):
        cps = self._copies(v_refs, o_refs, sems)
        for cp in cps["first"]() + cps["local"]():
            cp.start()

    def mid(self, v_refs, o_refs, sems):
        cps = self._copies(v_refs, o_refs, sems)
        for got, fwd in zip(cps["landed"](), cps["passed"]()):
            got.wait_recv()
            fwd.start()

    def finish(self, v_refs, o_refs, sems):
        cps = self._copies(v_refs, o_refs, sems)
        for cp in cps["final"]():
            cp.wait_recv()
        for cp in cps["first"]() + cps["passed"]():
            cp.wait_send()
        for cp in cps["local"]():
            cp.wait()


class _ExchangeBlocks:
    def __init__(self, ps):
        self.inputs = list(ps)
        n = self.n = len(ps)
        self.out_shape = [_sds((8,) + p.shape[2:], p.dtype) for p in ps]
        self.scratch = [pltpu.SemaphoreType.DMA((7 * n,)), pltpu.SemaphoreType.DMA((7 * n,))]

    def _copies(self, p_refs, o_refs, sems, incoming):
        send_sems, recv_sems = sems
        x, y, c, _ = _place()
        me = 4 * x + 2 * y + c
        out = []
        for a in range(self.n):
            for k in range(1, 8):
                px, py, pc = x ^ (k >> 2), y ^ ((k >> 1) & 1), c ^ (k & 1)
                out.append(pltpu.make_async_remote_copy(
                    src_ref=p_refs[a].at[2 * px + py, pc],
                    dst_ref=o_refs[a].at[4 * px + 2 * py + pc if incoming else me],
                    send_sem=send_sems.at[7 * a + k - 1], recv_sem=recv_sems.at[7 * a + k - 1],
                    device_id=(x, y, c) if incoming else (px, py, pc), device_id_type=MESH))
        return out

    def start(self, p_refs, o_refs, sems):
        for cp in self._copies(p_refs, o_refs, sems, False):
            cp.start()

    def mid(self, p_refs, o_refs, sems):
        pass

    def finish(self, p_refs, o_refs, sems):
        for cp in self._copies(p_refs, o_refs, sems, True):
            cp.wait_recv()
        for cp in self._copies(p_refs, o_refs, sems, False):
            cp.wait_send()


def _run_exchange(plan, *, name):
    n = plan.n

    def body(*refs):
        args = (refs[:n], refs[n:2 * n], refs[2 * n:])
        plan.start(*args)
        plan.mid(*args)
        plan.finish(*args)

    return _pcall(
        body, name=name, in_specs=[ANY] * n, out_specs=[ANY] * n,
        out_shape=plan.out_shape, scratch_shapes=plan.scratch,
    )(*plan.inputs)


def _send_sibling(rs, *, name):
    n = len(rs)

    def body(*refs):
        r_refs, o_refs = refs[:n], refs[n:2 * n]
        send_sems, recv_sems = refs[2 * n:]
        x, y, c, _ = _place()
        cps = [pltpu.make_async_remote_copy(
            src_ref=r_refs[a], dst_ref=o_refs[a], send_sem=send_sems.at[a], recv_sem=recv_sems.at[a],
            device_id=(x, y, 1 - c), device_id_type=MESH) for a in range(n)]
        for cp in cps:
            cp.start()
        for cp in cps:
            cp.wait()

    return _pcall(
        body, name=name, in_specs=[ANY] * n, out_specs=[ANY] * n,
        out_shape=[_sds(r.shape, r.dtype) for r in rs],
        scratch_shapes=[pltpu.SemaphoreType.DMA((n,)), pltpu.SemaphoreType.DMA((n,))],
    )(*rs)


def _add_devices(p, got, place, *, name):
    _, _, R, C = p.shape
    tr = _tile(R, 256, 16)

    def body(place_ref, p_ref, o_ref, out_ref):
        me = place_ref[2]
        own = p_ref[...].astype(F32)
        acc = jnp.where(me == 0, own, o_ref[0].astype(F32))
        for d in range(1, 8):
            acc = acc + jnp.where(me == d, own, o_ref[d].astype(F32))
        out_ref[...] = acc

    return _pcall(
        body, name=name,
        grid_spec=pltpu.PrefetchScalarGridSpec(
            num_scalar_prefetch=1, grid=(R // tr,),
            in_specs=[pl.BlockSpec((None, None, tr, C), lambda i, pr: (pr[0], pr[1], i, 0)),
                      pl.BlockSpec((8, tr, C), lambda i, pr: (0, i, 0))],
            out_specs=pl.BlockSpec((tr, C), lambda i, pr: (i, 0))),
        out_shape=_sds((R, C), F32),
        compiler_params=_params("parallel"),
    )(place, p, got)


def _round_up(n, m):
    return (n + m - 1) // m * m


def _f32_as_bf16(a):
    return lax.bitcast_convert_type(a.astype(F32), BF16).reshape(-1)


def _bf16_as_f32(a):
    return lax.bitcast_convert_type(a.reshape(-1, 2), F32)


def _by_chip_cols(a, cols):
    lead = a.shape[:-1]
    a = a.reshape(lead + (4, cols))
    return jnp.moveaxis(a, -2, 0).reshape(4, -1)


def kernel(x, meta_tokens, norm_mix, norm_mlp, sb_w_qkv, sb_w_o, lru_w_in, lru_conv_w, lru_conv_b, lru_w_rg, lru_b_rg, lru_w_ig, lru_b_ig, lru_lambda, lru_w_out, mlp_w_up, mlp_w_down, norm_final, loss_target, m_meta_tokens, m_norm_mix, m_norm_mlp, m_sb_w_qkv, m_sb_w_o, m_lru_w_in, m_lru_conv_w, m_lru_conv_b, m_lru_w_rg, m_lru_b_rg, m_lru_w_ig, m_lru_b_ig, m_lru_lambda, m_lru_w_out, m_mlp_w_up, m_mlp_w_down, m_norm_final, v_meta_tokens, v_norm_mix, v_norm_mlp, v_sb_w_qkv, v_sb_w_o, v_lru_w_in, v_lru_conv_w, v_lru_conv_b, v_lru_w_rg, v_lru_b_rg, v_lru_w_ig, v_lru_b_ig, v_lru_lambda, v_lru_w_out, v_mlp_w_up, v_mlp_w_down, v_norm_final):
    weights = dict(meta_tokens=meta_tokens, norm_mix=norm_mix, norm_mlp=norm_mlp, sb_w_qkv=sb_w_qkv,
                   sb_w_o=sb_w_o, lru_w_in=lru_w_in, lru_conv_w=lru_conv_w, lru_conv_b=lru_conv_b,
                   lru_w_rg=lru_w_rg, lru_b_rg=lru_b_rg, lru_w_ig=lru_w_ig, lru_b_ig=lru_b_ig,
                   lru_lambda=lru_lambda, lru_w_out=lru_w_out, mlp_w_up=mlp_w_up, mlp_w_down=mlp_w_down,
                   norm_final=norm_final)
    m_in = dict(meta_tokens=m_meta_tokens, norm_mix=m_norm_mix, norm_mlp=m_norm_mlp, sb_w_qkv=m_sb_w_qkv,
                sb_w_o=m_sb_w_o, lru_w_in=m_lru_w_in, lru_conv_w=m_lru_conv_w, lru_conv_b=m_lru_conv_b,
                lru_w_rg=m_lru_w_rg, lru_b_rg=m_lru_b_rg, lru_w_ig=m_lru_w_ig, lru_b_ig=m_lru_b_ig,
                lru_lambda=m_lru_lambda, lru_w_out=m_lru_w_out, mlp_w_up=m_mlp_w_up,
                mlp_w_down=m_mlp_w_down, norm_final=m_norm_final)
    v_in = dict(meta_tokens=v_meta_tokens, norm_mix=v_norm_mix, norm_mlp=v_norm_mlp, sb_w_qkv=v_sb_w_qkv,
                sb_w_o=v_sb_w_o, lru_w_in=v_lru_w_in, lru_conv_w=v_lru_conv_w, lru_conv_b=v_lru_conv_b,
                lru_w_rg=v_lru_w_rg, lru_b_rg=v_lru_b_rg, lru_w_ig=v_lru_w_ig, lru_b_ig=v_lru_b_ig,
                lru_lambda=v_lru_lambda, lru_w_out=v_lru_w_out, mlp_w_up=v_mlp_w_up,
                mlp_w_down=v_mlp_w_down, norm_final=v_norm_final)
    names = list(weights)

    seq, D = x.shape[1], x.shape[2]
    n_meta = meta_tokens.shape[0]
    Dq = D // 4
    T = _round_up(n_meta + seq, ATT_BLOCK)
    nb = lru_w_rg.shape[1]
    F = mlp_w_up.shape[2]
    depth = mlp_w_up.shape[0]
    my_x, my_y, my_c = lax.axis_index("x"), lax.axis_index("y"), lax.axis_index("c")
    c_arr = jnp.reshape(my_c, (1,)).astype(jnp.int32)

    assert depth == 2

    def halves(a):
        return a.astype(BF16).reshape(2, a.shape[0] // 2, a.shape[1])

    small = [meta_tokens, lru_conv_w[0], lru_conv_b, lru_b_rg, lru_b_ig, lru_lambda]
    sparts = [_f32_as_bf16(s) for s in small]
    sizes = [p.shape[0] for p in sparts]
    total = _round_up(sum(sizes), 2 * 32 * LANE)
    sflat =jnp.concatenate(sparts + [jnp.zeros((total - sum(sizes),), BF16)]).reshape(2, -1, LANE)
    gq, gsm = _run_exchange(_GatherChips([halves(sb_w_qkv[0]), sflat]), name="gather_first")
    gather_rest = _GatherChips([halves(sb_w_o[0]), halves(lru_w_in[0]), halves(lru_w_out[0]),
                                mlp_w_up.astype(BF16), mlp_w_down.astype(BF16)])
    w_qkv = gq.reshape(4, D, 3 * Dq)
    gsm = gsm.reshape(4, total)
    offs = [sum(sizes[:k]) for k in range(len(sizes))]
    sm = [_bf16_as_f32(gsm[:, o:o + s]) for o, s in zip(offs, sizes)]
    meta_full = jnp.moveaxis(sm[0].reshape(4, n_meta, Dq), 0, 1).reshape(n_meta, D)
    conv_w = jnp.moveaxis(sm[1].reshape(4, 4, Dq), 0, 1).reshape(4, D)
    conv_b, b_rg, b_ig, lam = [s.reshape(1, D) for s in sm[2:6]]
    w_rg = lru_w_rg[0].astype(BF16)
    w_ig = lru_w_ig[0].astype(BF16)
    g_mix = [norm_mix[l].reshape(1, D) for l in range(depth)]
    g_mlp = [norm_mlp[l].reshape(1, D) for l in range(depth)]
    g_fin = norm_final.reshape(1, D)

    pad_rows = T - n_meta - seq
    h0 = jnp.concatenate([meta_full, x[0], jnp.zeros((pad_rows, D), F32)], axis=0)
    target = jnp.concatenate([jnp.zeros((n_meta, D), F32), loss_target[0], jnp.zeros((pad_rows, D), F32)], axis=0)

    hn0, qkv = _norm_mm(h0, g_mix[0], w_qkv, out_dtype=BF16, name="qkv_proj")
    att, w_sv, s_sv, go, gi, gout, w_up, w_down = _attn_fwd(qkv, name="attn_fwd", plan=gather_rest)
    w_o = go.reshape(D, D)
    w_in = gi.reshape(4, D, 2 * Dq)
    w_out = gout.reshape(D, D)
    h1 = _mm_res(att, w_o, h0, name="attn_out")
    h2, hnm0, up0 = _mlp_fwd(h1, g_mlp[0], w_up, w_down, layer=0, name="mlp0_fwd")
    hn1, gr = _norm_mm(h2, g_mix[1], w_in, out_dtype=F32, name="lru_in")
    a_t, b_t = _lru_pre(gr, conv_w, conv_b, w_rg, b_rg, w_ig, b_ig, lam, name="lru_pre")
    hs = _lru_scan(a_t, b_t, reverse=False, name="lru_scan")
    y, h3 = _lru_out(gr, hs, w_out, h2, name="lru_out")
    h4, hnm1, up1 = _mlp_fwd(h3, g_mlp[1], w_up, w_down, layer=1, name="mlp1_fwd")
    loss, dh4, dh4b, dg_fin = _loss_head(h4, g_fin, target, row_lo=n_meta, row_hi=n_meta + seq, name="loss_head")

    dup1, dh3, dh3b, dg_mlp1 = _mlp_bwd(dh4, h3, g_mlp[1], up1, w_up, w_down, layer=1, name="mlp1_bwd")
    dw_up = _mm_tn(hnm1, dup1, shards=4, relu2=False, slot=1, name="mlp1_dwup")
    dw_down = _mm_tn(up1, dh4b, shards=1, relu2=True, slot=1, row_shards=4, name="mlp1_dwdown")
    dw_out = _mm_tn(y, dh3b, shards=1, relu2=False, name="lru_dwout")
    dgr, dhy = _lru_out_bwd(gr, hs, dh3b, w_out, name="lru_out_bwd")
    lmb = _lru_scan(a_t, dhy, reverse=True, name="lru_scan_bwd")
    du, dw_rg, db_rg, dw_ig, db_ig, dlam = _lru_gate_bwd(
        gr, hs, lmb, conv_w, conv_b, w_rg, b_rg, w_ig, b_ig, lam, name="lru_gate_bwd")
    dgr, dconv_w, dconv_b = _lru_conv_bwd(gr, du, conv_w, dgr, name="lru_conv_bwd")
    dh2, dh2b, dg_mix1 = _mm_nt_normbwd(dgr, w_in, h2, g_mix[1], dh3, name="lru_in_bwd")
    dw_in = _mm_tn(hn1, dgr, shards=4, relu2=False, name="lru_dwin")
    dup0, dh1, dh1b, dg_mlp0 = _mlp_bwd(dh2, h1, g_mlp[0], up0, w_up, w_down, layer=0, name="mlp0_bwd")
    dw_up = _mm_tn(hnm0, dup0, shards=4, relu2=False, slot=0, into=dw_up, name="mlp0_dwup")
    dw_down = _mm_tn(up0, dh2b, shards=1, relu2=True, slot=0, into=dw_down, row_shards=4, name="mlp0_dwdown")
    datt = _mm_nt(dh1b, w_o, out_dtype=BF16, name="attn_out_bwd")
    dw_o = _mm_tn(att, dh1b, shards=1, relu2=False, name="attn_dwo")
    def halves_of(d, rows):
        return d.reshape(4, 2, rows // 2, d.shape[-1])

    early = [halves_of(dw_o, Dq), halves_of(dw_in, D), halves_of(dw_out, Dq), dw_up, dw_down]
    dq, dk, dv, *got_early = _attn_bwd(qkv, datt, w_sv, s_sv, name="attn_bwd", plan=_ExchangeBlocks(early))
    dqkv = jnp.concatenate([dq, dk, dv], axis=1)
    dh0, _, dg_mix0 = _mm_nt_normbwd(dqkv, w_qkv, h0, g_mix[0], dh1, name="qkv_bwd")
    dw_qkv = _mm_tn(hn0, dqkv, shards=4, relu2=False, name="attn_dwqkv")
    grad_x = dh0[n_meta:n_meta + seq][None]
    dmeta = dh0[:n_meta]

    large = ["sb_w_o", "lru_w_in", "lru_w_out", "mlp_w_up", "mlp_w_down", "sb_w_qkv"]
    sharded =[_by_chip_cols(dmeta, Dq), _by_chip_cols(dconv_w, Dq), dconv_b.reshape(4, Dq),
               db_rg.reshape(4, Dq), db_ig.reshape(4, Dq), dlam.reshape(4, Dq)]
    repl = [jnp.concatenate([dg_mix0, dg_mix1], axis=0).reshape(-1),
            jnp.concatenate([dg_mlp0, dg_mlp1], axis=0).reshape(-1),
            dg_fin.reshape(-1), dw_rg.reshape(-1), dw_ig.reshape(-1)]
    rsizes = [r.shape[0] for r in repl]
    rtotal = _round_up(sum(rsizes), 4 * 2 * 16 * LANE)
    rflat = jnp.concatenate(repl + [jnp.zeros((rtotal - sum(rsizes),), F32)]).reshape(4, rtotal // 4)
    gsizes = [s.shape[1] for s in sharded] + [rtotal // 4]
    gtotal = _round_up(sum(gsizes), 2 * 16 * LANE)
    tail = jnp.concatenate(sharded + [rflat, jnp.zeros((4, gtotal - sum(gsizes)), F32)], axis=1)
    late = [halves_of(dw_qkv, D), tail.reshape(4, 2, -1, LANE)]
    got_late = _run_exchange(_ExchangeBlocks(late), name="reduce_late")
    place = jnp.stack([2 * my_x + my_y, my_c, 4 * my_x + 2 * my_y + my_c]).astype(jnp.int32)
    mine = [_add_devices(p, o, place, name="reduce_add_" + t)
            for p, o, t in zip(early + late, list(got_early) + list(got_late), large + ["tail"])]
    theirs = _send_sibling(mine, name="reduce_join")

    grads, delta, new_m, new_v = {}, {}, {}, {}
    for n, a, b in zip(large, mine, theirs):
        shp = weights[n].shape
        view = (2,) + a.shape
        g, d, nm, nv = _adamw_halves(weights[n].reshape(view), a, b, m_in[n].reshape(view),
                                     v_in[n].reshape(view), c_arr, name="adamw_" + n)
        grads[n], delta[n], new_m[n], new_v[n] = g.reshape(shp), d.reshape(shp), nm.reshape(shp), nv.reshape(shp)

    lo = jnp.where(my_c == 0, mine[-1], theirs[-1])
    hi = jnp.where(my_c == 0, theirs[-1], mine[-1])
    gshard = jnp.concatenate([lo, hi], axis=0).reshape(gtotal)
    goffs = [sum(gsizes[:k]) for k in range(len(gsizes))]
    gp = [gshard[o:o + s] for o, s in zip(goffs, gsizes)]
    rfull = _run_exchange(_GatherChips([gp[-1].reshape(2, -1, LANE)]), name="gather_replicated")[0].reshape(rtotal)
    roffs = [sum(rsizes[:k]) for k in range(len(rsizes))]
    rp = [rfull[o:o + s] for o, s in zip(roffs, rsizes)]
    grads.update(meta_tokens=gp[0], lru_conv_w=gp[1], lru_conv_b=gp[2], lru_b_rg=gp[3], lru_b_ig=gp[4],
                 lru_lambda=gp[5], norm_mix=rp[0], norm_mlp=rp[1], norm_final=rp[2], lru_w_rg=rp[3],
                 lru_w_ig=rp[4])
    grads = {n: grads[n].reshape(weights[n].shape) for n in names}
    rest = [n for n in names if n not in large]
    ssz = [weights[n].size for n in rest]
    small_cols = 8 * LANE
    stotal = _round_up(sum(ssz), SUBLANE * small_cols)

    def pack(src):
        return jnp.concatenate([src[n].reshape(-1) for n in rest]
                               + [jnp.ones((stotal - sum(ssz),), F32)]).reshape(-1, small_cols)

    d, nm, nv = _adamw(pack(weights), pack(grads), pack(m_in), pack(v_in), name="adamw_small")
    soffs = [sum(ssz[:k]) for k in range(len(ssz))]
    for n, o, s in zip(rest, soffs, ssz):
        shp = weights[n].shape
        delta[n] = d.reshape(-1)[o:o + s].reshape(shp)
        new_m[n] = nm.reshape(-1)[o:o + s].reshape(shp)
        new_v[n] = nv.reshape(-1)[o:o + s].reshape(shp)

    loss = lax.psum(loss[0, 0], ("x", "y", "c"))
    return (loss, grad_x, *[grads[n] for n in names], *[delta[n] for n in names],
            *[new_m[n] for n in names], *[new_v[n] for n in names])
```

```python
import functools

import jax
import jax.numpy as jnp
from jax import lax
from jax.experimental import pallas as pl
from jax.experimental.pallas import tpu as pltpu

F32 = jnp.float32
BF16 = jnp.bfloat16
MESH = pl.DeviceIdType.MESH

EPS = 1e-6
HEAD_DIM = 64
LANE = 128
SUBLANE = 8
LRU_C = 8.0
VMEM_LIMIT = 56 * 1024 * 1024

ADAM_LR = 0.001
ADAM_B1 = 0.9
ADAM_B2 = 0.999
ADAM_EPS = 1e-08
ADAM_WD = 0.01
ADAM_STEP = 10


def _pcall(body, **kw):
    return pl.pallas_call(body, **kw)


def _params(*sem):
    return pltpu.CompilerParams(dimension_semantics=sem, vmem_limit_bytes=VMEM_LIMIT)


def _tile(n, pref, align):
    best = None
    for t in range(align, min(n, pref) + 1, align):
        if n % t == 0:
            best = t
    return n if best is None else best


def _sds(shape, dtype):
    return jax.ShapeDtypeStruct(shape, dtype)


def _rstd(x):
    return lax.rsqrt(jnp.mean(x * x, axis=-1, keepdims=True) + EPS)


def _norm_bwd(x, g, dy):
    rstd = _rstd(x)
    n = x * rstd
    dn = dy * g
    dx = rstd * (dn - n * jnp.mean(dn * n, axis=-1, keepdims=True))
    dg = jnp.sum(dy * n, axis=0, keepdims=True)
    return dx, dg


def _softplus_parts(z):
    l1p = jnp.log(1.0 + jnp.exp(-jnp.abs(z)))
    return jnp.maximum(z, 0.0) + l1p, jnp.minimum(z, 0.0) - l1p


def _sigmoid(x):
    return 1.0 / (1.0 + jnp.exp(-x))


def _gelu_parts(x):
    k = 0.7978845608028654
    inner = k * (x + 0.044715 * (x * x * x))
    t = jnp.tanh(inner)
    gelu = 0.5 * x * (1.0 + t)
    dgelu = 0.5 * (1.0 + t) + 0.5 * x * (1.0 - t * t) * (k * (1.0 + 3.0 * 0.044715 * (x * x)))
    return gelu, dgelu


def _norm_mm(h, g, w, *, out_dtype, name):
    T, D = h.shape
    S, _, n = w.shape
    tm = _tile(T, 1056, 16)
    tn = _tile(n, 768, LANE)
    nj = n // tn

    def body(h_ref, g_ref, w_ref, hn_ref, o_ref):
        @pl.when(pl.program_id(1) == 0)
        def _():
            x = h_ref[...]
            hn_ref[...] = (x * _rstd(x) * g_ref[...]).astype(BF16)

        o_ref[...] = jnp.dot(hn_ref[...], w_ref[...], preferred_element_type=F32).astype(out_dtype)

    return _pcall(
        body, name=name, grid=(T // tm, S * nj),
        in_specs=[pl.BlockSpec((tm, D), lambda i, j: (i, 0)),
                  pl.BlockSpec((1, D), lambda i, j: (0, 0)),
                  pl.BlockSpec((None, D, tn), lambda i, j: (j // nj, 0, j % nj))],
        out_specs=[pl.BlockSpec((tm, D), lambda i, j: (i, 0)),
                   pl.BlockSpec((tm, tn), lambda i, j: (i, j))],
        out_shape=[_sds((T, D), BF16), _sds((T, S * n), out_dtype)],
        compiler_params=_params("parallel", "arbitrary"),
    )(h, g, w)


def _mm_res(a, w, res, *, name):
    T, K = a.shape
    N = w.shape[1]
    tm = _tile(T, 1056, 16)

    def body(a_ref, w_ref, r_ref, o_ref):
        o_ref[...] = r_ref[...] + jnp.dot(a_ref[...], w_ref[...], preferred_element_type=F32)

    return _pcall(
        body, name=name, grid=(T // tm,),
        in_specs=[pl.BlockSpec((tm, K), lambda i: (i, 0)),
                  pl.BlockSpec((K, N), lambda i: (0, 0)),
                  pl.BlockSpec((tm, N), lambda i: (i, 0))],
        out_specs=pl.BlockSpec((tm, N), lambda i: (i, 0)),
        out_shape=_sds((T, N), F32),
        compiler_params=_params("parallel"),
    )(a, w, res)


def _mm_nt(a, w, *, out_dtype, name):
    T, N = a.shape
    K = w.shape[0]
    tm = _tile(T, 1056, 16)

    def body(a_ref, w_ref, o_ref):
        o_ref[...] = lax.dot_general(a_ref[...], w_ref[...], (((1,), (1,)), ((), ())),
                                     preferred_element_type=F32).astype(out_dtype)

    return _pcall(
        body, name=name, grid=(T // tm,),
        in_specs=[pl.BlockSpec((tm, N), lambda i: (i, 0)),
                  pl.BlockSpec((K, N), lambda i: (0, 0))],
        out_specs=pl.BlockSpec((tm, K), lambda i: (i, 0)),
        out_shape=_sds((T, K), out_dtype),
        compiler_params=_params("parallel"),
    )(a, w)


def _mm_tn(a, b, *, shards, relu2, name, slot=None, into=None, row_shards=0, out_dtype=BF16):
    T, Ka = a.shape
    Nb = b.shape[1]
    n = Nb // shards
    tka = _tile(Ka, 512, LANE)
    tnb = _tile(n, 512, LANE)
    nj = n // tnb

    def body(a_ref, b_ref, *rest):
        o_ref = rest[-1]
        av = a_ref[...]
        if relu2:
            r = jnp.maximum(av, 0)
            av = r * r
        o_ref[...] = lax.dot_general(av, b_ref[...], (((0,), (0,)), ((), ())),
                                     preferred_element_type=F32).astype(out_dtype)

    in_specs = [pl.BlockSpec((T, tka), lambda i, j: (0, i)),
                pl.BlockSpec((T, tnb), lambda i, j: (0, j))]
    args = [a, b]
    aliases = {}
    if slot is None:
        out_spec = pl.BlockSpec((None, tka, tnb), lambda i, j: (j // nj, i, j % nj))
        out_shape = _sds((shards, Ka, n), out_dtype)
    else:
        if row_shards:
            ni = Ka // row_shards // tka
            out_spec = pl.BlockSpec((None, None, tka, tnb), lambda i, j: (i // ni, slot, i % ni, j))
            out_shape = _sds((row_shards, 2, Ka // row_shards, n), out_dtype)
        else:
            out_spec = pl.BlockSpec((None, None, tka, tnb), lambda i, j: (j // nj, slot, i, j % nj))
            out_shape = _sds((shards, 2, Ka, n), out_dtype)
        if into is not None:
            in_specs.append(pl.BlockSpec(memory_space=pl.ANY))
            args.append(into)
            aliases = {2: 0}
    return _pcall(
        body, name=name, grid=(Ka // tka, shards * nj),
        in_specs=in_specs, out_specs=out_spec, out_shape=out_shape,
        input_output_aliases=aliases,
        compiler_params=_params("parallel", "parallel"),
    )(*args)


def _mm_nt_normbwd(dy, w, h, g, dres, *, name, plan=None):
    T, D = h.shape
    S, _, n = w.shape
    tm = _tile(T, 1056, 16)
    nx, before, after = _ride_along(plan, T // tm, S)

    def body(*refs):
        dy_ref, w_ref, h_ref, g_ref, dr_ref = refs[:5]
        dh_ref, dhb_ref, dg_ref = refs[5 + nx:8 + nx]
        acc_ref = refs[8 + 2 * nx]
        ride = (refs[5:5 + nx], refs[8 + nx:8 + 2 * nx], refs[9 + 2 * nx:])
        before(ride)
        i, s = pl.program_id(0), pl.program_id(1)
        part = lax.dot_general(dy_ref[...], w_ref[...], (((1,), (1,)), ((), ())),
                               preferred_element_type=F32)

        @pl.when(s == 0)
        def _():
            acc_ref[...] = part

        @pl.when(s > 0)
        def _():
            acc_ref[...] += part

        @pl.when(s == S - 1)
        def _():
            dx, dg = _norm_bwd(h_ref[...], g_ref[...], acc_ref[...])
            dh = dr_ref[...] + dx
            dh_ref[...] = dh
            dhb_ref[...] = dh.astype(BF16)

            @pl.when(i == 0)
            def _():
                dg_ref[...] = dg

            @pl.when(i > 0)
            def _():
                dg_ref[...] += dg

        after(ride)

    extra = plan.inputs if plan else []
    return _pcall(
        body, name=name, grid=(T // tm, S),
        in_specs=[pl.BlockSpec((tm, n), lambda i, s: (i, s)),
                  pl.BlockSpec((None, D, n), lambda i, s: (s, 0, 0)),
                  pl.BlockSpec((tm, D), lambda i, s: (i, 0)),
                  pl.BlockSpec((1, D), lambda i, s: (0, 0)),
                  pl.BlockSpec((tm, D), lambda i, s: (i, 0))] + [ANY] * nx,
        out_specs=[pl.BlockSpec((tm, D), lambda i, s: (i, 0)),
                   pl.BlockSpec((tm, D), lambda i, s: (i, 0)),
                   pl.BlockSpec((1, D), lambda i, s: (0, 0))] + [ANY] * nx,
        out_shape=[_sds((T, D), F32), _sds((T, D), BF16), _sds((1, D), F32)] + (plan.out_shape if plan else []),
        scratch_shapes=[pltpu.VMEM((tm, D), F32)] + (plan.scratch if plan else []),
        compiler_params=_params("arbitrary", "arbitrary"),
    )(dy, w, h, g, dres, *extra)


def _row_chains(tm):
    first = (tm // 2 + 15) // 16 * 16
    return [slice(0, first), slice(first, tm)] if 0 < first < tm else [slice(0, tm)]


def _mlp_fwd(h, g, w_up, w_down, *, layer, name):
    T, D = h.shape
    S, _, _, n = w_up.shape
    tm = _tile(T, 528, 16)
    tf = _tile(n, 1024, LANE)
    nj = n // tf
    nf = S * nj
    chains = _row_chains(tm)

    def body(h_ref, g_ref, wu_ref, wd_ref, o_ref, hn_ref, up_ref, acc_ref):
        f = pl.program_id(1)

        @pl.when(f == 0)
        def _():
            x = h_ref[...]
            hn_ref[...] = (x * _rstd(x) * g_ref[...]).astype(BF16)

        parts = []
        for rows in chains:
            up = jnp.dot(hn_ref[rows, :], wu_ref[...], preferred_element_type=F32)
            up_ref[rows, :] = up.astype(BF16)
            r = jnp.maximum(up, 0.0)
            parts.append(jnp.dot((r * r).astype(BF16), wd_ref[...], preferred_element_type=F32))
        part = jnp.concatenate(parts, axis=0)

        @pl.when(f == 0)
        def _():
            acc_ref[...] = part

        @pl.when(f > 0)
        def _():
            acc_ref[...] += part

        @pl.when(f == nf - 1)
        def _():
            o_ref[...] = h_ref[...] + acc_ref[...]

    return _pcall(
        body, name=name, grid=(T // tm, nf),
        in_specs=[pl.BlockSpec((tm, D), lambda i, f: (i, 0)),
                  pl.BlockSpec((1, D), lambda i, f: (0, 0)),
                  pl.BlockSpec((None, None, D, tf), lambda i, f: (f // nj, layer, 0, f % nj)),
                  pl.BlockSpec((None, None, tf, D), lambda i, f: (f // nj, layer, f % nj, 0))],
        out_specs=[pl.BlockSpec((tm, D), lambda i, f: (i, 0)),
                   pl.BlockSpec((tm, D), lambda i, f: (i, 0)),
                   pl.BlockSpec((tm, tf), lambda i, f: (i, f))],
        out_shape=[_sds((T, D), F32), _sds((T, D), BF16), _sds((T, S * n), BF16)],
        scratch_shapes=[pltpu.VMEM((tm, D), F32)],
        compiler_params=_params("parallel", "arbitrary"),
    )(h, g, w_up, w_down)


def _mlp_bwd(dy, h, g, up, w_up, w_down, *, layer, name):
    T, D = h.shape
    S, _, _, n = w_up.shape
    tm = _tile(T, 528, 16)
    tf = _tile(n, 1024, LANE)
    nj = n // tf
    nf = S * nj
    chains = _row_chains(tm)

    def body(dy_ref, h_ref, g_ref, up_ref, wu_ref, wd_ref, dup_ref, dh_ref, dhb_ref, dg_ref,
             dyb_ref, acc_ref):
        i, f = pl.program_id(0), pl.program_id(1)

        @pl.when(f == 0)
        def _():
            dyb_ref[...] = dy_ref[...].astype(BF16)

        parts = []
        for rows in chains:
            dact = lax.dot_general(dyb_ref[rows, :], wd_ref[...], (((1,), (1,)), ((), ())),
                                   preferred_element_type=F32)
            r = jnp.maximum(up_ref[rows, :].astype(F32), 0.0)
            dup = (dact * (2.0 * r)).astype(BF16)
            dup_ref[rows, :] = dup
            parts.append(lax.dot_general(dup, wu_ref[...], (((1,), (1,)), ((), ())),
                                         preferred_element_type=F32))
        part = jnp.concatenate(parts, axis=0)

        @pl.when(f == 0)
        def _():
            acc_ref[...] = part

        @pl.when(f > 0)
        def _():
            acc_ref[...] += part

        @pl.when(f == nf - 1)
        def _():
            dx, dg = _norm_bwd(h_ref[...], g_ref[...], acc_ref[...])
            dh = dy_ref[...] + dx
            dh_ref[...] = dh
            dhb_ref[...] = dh.astype(BF16)

            @pl.when(i == 0)
            def _():
                dg_ref[...] = dg

            @pl.when(i > 0)
            def _():
                dg_ref[...] += dg

    return _pcall(
        body, name=name, grid=(T // tm, nf),
        in_specs=[pl.BlockSpec((tm, D), lambda i, f: (i, 0)),
                  pl.BlockSpec((tm, D), lambda i, f: (i, 0)),
                  pl.BlockSpec((1, D), lambda i, f: (0, 0)),
                  pl.BlockSpec((tm, tf), lambda i, f: (i, f)),
                  pl.BlockSpec((None, None, D, tf), lambda i, f: (f // nj, layer, 0, f % nj)),
                  pl.BlockSpec((None, None, tf, D), lambda i, f: (f // nj, layer, f % nj, 0))],
        out_specs=[pl.BlockSpec((tm, tf), lambda i, f: (i, f)),
                   pl.BlockSpec((tm, D), lambda i, f: (i, 0)),
                   pl.BlockSpec((tm, D), lambda i, f: (i, 0)),
                   pl.BlockSpec((1, D), lambda i, f: (0, 0))],
        out_shape=[_sds((T, S * n), BF16), _sds((T, D), F32), _sds((T, D), BF16), _sds((1, D), F32)],
        scratch_shapes=[pltpu.VMEM((tm, D), BF16), pltpu.VMEM((tm, D), F32)],
        compiler_params=_params("arbitrary", "arbitrary"),
    )(dy, h, g, up, w_up, w_down)


ATT_BLOCK = 128
ATT_HEADS = 4


def _attn_tile(T):
    for w in (3 * ATT_BLOCK, 2 * ATT_BLOCK):
        if T % w == 0:
            return w
    return ATT_BLOCK


def _tri(strict_lower, value):
    B = ATT_BLOCK
    r = lax.broadcasted_iota(jnp.int32, (2 * B, B), 0)
    r = jnp.where(r >= B, r - B, r)
    c = lax.broadcasted_iota(jnp.int32, (2 * B, B), 1)
    m = (r > c) if strict_lower else (r < c)
    return jnp.where(m, value, 0.0).astype(BF16)


def _split_dot(x, tri):
    hi = lax.bitcast_convert_type(lax.bitcast_convert_type(x, jnp.uint32) & jnp.uint32(0xFFFF0000), F32)
    lo = x - hi
    return jnp.dot(jnp.concatenate([hi.astype(BF16), lo.astype(BF16)], axis=1), tri,
                   preferred_element_type=F32)


def _causal_mask(W):
    r = lax.broadcasted_iota(jnp.int32, (W, W), 0)
    c = lax.broadcasted_iota(jnp.int32, (W, W), 1)
    return c < r


def _attn_scores(qs, ks, carry, tri_neg, masked, want_sig):
    W = qs.shape[0]
    B = ATT_BLOCK
    z = lax.dot_general(qs, ks, (((1,), (1,)), ((), ())), preferred_element_type=F32)
    minus_abs = lax.bitcast_convert_type(
        lax.bitcast_convert_type(z, jnp.uint32) | jnp.uint32(0x80000000), F32)
    sp = jnp.maximum(z, 0.0) + jnp.log(1.0 + jnp.exp(minus_abs))
    logsig = z - sp
    if masked:
        causal = _causal_mask(W)
        sp = jnp.where(causal, sp, 0.0)
    afters = []
    for b in reversed(range(W // B)):
        blk = sp[:, b * B:(b + 1) * B]
        within = _split_dot(blk, tri_neg)
        afters.append(within + carry)
        carry = carry + (within[:, 0:1] - blk[:, 0:1])
    after = jnp.concatenate(afters[::-1], axis=1)
    w = jnp.exp(logsig + after)
    if masked:
        w = jnp.where(causal, w, 0.0)
    return w, (jnp.exp(logsig) if want_sig else None), carry


def _ride_along(plan, npairs, nq):
    if plan is None:
        return 0, (lambda refs: None), (lambda refs: None)
    nx = plan.n

    def before(refs):
        p, i = pl.program_id(0), pl.program_id(1)

        @pl.when(jnp.logical_and(p == 0, i == 0))
        def _():
            plan.start(*refs)

        @pl.when(jnp.logical_and(p == npairs // 2, i == 0))
        def _():
            plan.mid(*refs)

    def after(refs):
        p, i = pl.program_id(0), pl.program_id(1)

        @pl.when(jnp.logical_and(p == npairs - 1, i == nq - 1))
        def _():
            plan.finish(*refs)

    return nx, before, after


def _saved_tile(i, j):
    return i * (i + 1) // 2 + j


def _attn_fwd(qkv, *, name, plan=None):
    T = qkv.shape[0]
    D = qkv.shape[1] // 3
    W = _attn_tile(T)
    H = ATT_HEADS
    lanes = H * HEAD_DIM
    ngroups = D // lanes
    nq = T // W
    ntri = nq * (nq + 1) // 2
    scale = HEAD_DIM ** -0.5
    nx, before, after = _ride_along(plan, ngroups, nq)

    def body(*refs):
        q_ref, k_ref, v_ref = refs[:3]
        o_ref, wsv_ref, ssv_ref = refs[3 + nx:6 + nx]
        stage_w, stage_s, stage_sems = refs[6 + 2 * nx:9 + 2 * nx]
        ride = (refs[3:3 + nx], refs[6 + nx:6 + 2 * nx], refs[9 + 2 * nx:])
        before(ride)
        p, i = pl.program_id(0), pl.program_id(1)
        tri = _tri(True, -1.0)
        heads = [slice(hh * HEAD_DIM, (hh + 1) * HEAD_DIM) for hh in range(H)]
        qs = [q_ref[:, cols] * scale for cols in heads]

        def save(slot, j):
            dst = _saved_tile(i, j)
            return [pltpu.make_async_copy(stage.at[slot], sv.at[pl.ds(p * H, H), dst], stage_sems.at[slot, a])
                    for a, (sv, stage) in enumerate(((wsv_ref, stage_w), (ssv_ref, stage_s)))]

        def tile(t, state, masked):
            j = i - t
            rows = pl.ds(pl.multiple_of(j * W, W), W)
            slot = t % 2
            if not masked:
                @pl.when(t >= 2)
                def _():
                    for cp in save(slot, j):
                        cp.wait()
            out = []
            for hh, (cols, q, (carry, acc)) in enumerate(zip(heads, qs, state)):
                w, sig, carry = _attn_scores(q, k_ref[rows, cols], carry, tri, masked, True)
                wb = w.astype(BF16)
                stage_w[slot, hh] = wb
                stage_s[slot, hh] = sig.astype(BF16)
                acc = acc + jnp.dot(wb, v_ref[rows, cols], preferred_element_type=F32)
                out.append((carry, acc))
            for cp in save(slot, j):
                cp.start()
            return tuple(out)

        zero = (jnp.zeros((W, 1), F32), jnp.zeros((W, HEAD_DIM), F32))
        state = tile(0, (zero,) * H, True)
        state = lax.fori_loop(1, i + 1, lambda t, st: tile(t, st, False), state)
        for cols, (_, acc) in zip(heads, state):
            o_ref[:, cols] = acc.astype(BF16)
        for cp in save(i % 2, 0):
            cp.wait()

        @pl.when(i >= 1)
        def _():
            for cp in save((i + 1) % 2, 0):
                cp.wait()

        after(ride)

    extra = plan.inputs if plan else []
    saved = _sds((D // HEAD_DIM, ntri, W, W), BF16)
    return _pcall(
        body, name=name, grid=(ngroups, nq),
        in_specs=[pl.BlockSpec((W, lanes), lambda p, i: (i, p)),
                  pl.BlockSpec((T, lanes), lambda p, i: (0, ngroups + p)),
                  pl.BlockSpec((T, lanes), lambda p, i: (0, 2 * ngroups + p))] + [ANY] * nx,
        out_specs=[pl.BlockSpec((W, lanes), lambda p, i: (i, p)), ANY, ANY] + [ANY] * nx,
        out_shape=[_sds((T, D), BF16), saved, saved] + (plan.out_shape if plan else []),
        scratch_shapes=[pltpu.VMEM((2, H, W, W), BF16), pltpu.VMEM((2, H, W, W), BF16),
                        pltpu.SemaphoreType.DMA((2, 2))] + (plan.scratch if plan else []),
        compiler_params=_params("arbitrary", "arbitrary"),
    )(qkv, qkv, qkv, *extra)


def _attn_bwd(qkv, do, w_sv, s_sv, *, name, plan=None):
    T = qkv.shape[0]
    D = qkv.shape[1] // 3
    B = ATT_BLOCK
    W = _attn_tile(T)
    H = ATT_HEADS
    lanes = H * HEAD_DIM
    ngroups = D // lanes
    nq = T // W
    scale = HEAD_DIM ** -0.5
    nx, before, after = _ride_along(plan, ngroups, nq)

    def body(*refs):
        q_ref, k_ref, v_ref, do_ref, wsv_ref, ssv_ref = refs[:6]
        dq_ref, dk_ref, dv_ref = refs[6 + nx:9 + nx]
        dk_acc, dv_acc, stage_w, stage_s, stage_sems = refs[9 + 2 * nx:14 + 2 * nx]
        ride = (refs[6:6 + nx], refs[9 + nx:9 + 2 * nx], refs[14 + 2 * nx:])
        before(ride)
        p, i = pl.program_id(0), pl.program_id(1)

        @pl.when(i == 0)
        def _():
            dk_acc[...] = jnp.zeros_like(dk_acc)
            dv_acc[...] = jnp.zeros_like(dv_acc)

        def fetch(slot, j):
            src = _saved_tile(i, j)
            return [pltpu.make_async_copy(sv.at[pl.ds(p * H, H), src], stage.at[slot], stage_sems.at[slot, a])
                    for a, (sv, stage) in enumerate(((wsv_ref, stage_w), (ssv_ref, stage_s)))]

        tri_before = _tri(False, 1.0)
        heads = [slice(hh * HEAD_DIM, (hh + 1) * HEAD_DIM) for hh in range(H)]
        qs = [q_ref[:, cols] * scale for cols in heads]
        douts = [do_ref[:, cols] for cols in heads]

        def grad(j, state, masked):
            rows = pl.ds(pl.multiple_of(j * W, W), W)
            slot = j % 2
            for cp in fetch(slot, j):
                cp.wait()
            if not masked:
                for cp in fetch(1 - slot, j + 1):
                    cp.start()
            out = []
            for hh, (cols, (gsum, dq)) in enumerate(zip(heads, state)):
                wb = stage_w[slot, hh]
                sig = stage_s[slot, hh].astype(F32)
                dw = lax.dot_general(douts[hh], v_ref[rows, cols], (((1,), (1,)), ((), ())),
                                     preferred_element_type=F32)
                g = dw * wb.astype(F32)
                befores = []
                for b in range(W // B):
                    blk = g[:, b * B:(b + 1) * B]
                    within = _split_dot(blk, tri_before)
                    befores.append(within + gsum)
                    gsum = gsum + (within[:, B - 1:B] + blk[:, B - 1:B])
                dz = g - sig * (g + jnp.concatenate(befores, axis=1))
                if masked:
                    dz = jnp.where(_causal_mask(W), dz, 0.0)
                dzb = dz.astype(BF16)
                dq = dq + jnp.dot(dzb, k_ref[rows, cols] * scale, preferred_element_type=F32)
                dk_acc[rows, cols] += lax.dot_general(dzb, qs[hh], (((0,), (0,)), ((), ())),
                                                      preferred_element_type=F32)
                dv_acc[rows, cols] += lax.dot_general(wb, douts[hh], (((0,), (0,)), ((), ())),
                                                      preferred_element_type=F32)
                out.append((gsum, dq))
            return tuple(out)

        for cp in fetch(0, 0):
            cp.start()
        zero = (jnp.zeros((W, 1), F32), jnp.zeros((W, HEAD_DIM), F32))
        state = lax.fori_loop(0, i, lambda j, st: grad(j, st, False), (zero,) * H)
        state = grad(i, state, True)
        for cols, (_, dq) in zip(heads, state):
            dq_ref[:, cols] = dq.astype(BF16)

        @pl.when(i == nq - 1)
        def _():
            dk_ref[...] = dk_acc[...].astype(BF16)
            dv_ref[...] = dv_acc[...].astype(BF16)

        after(ride)

    extra = plan.inputs if plan else []
    return _pcall(
        body, name=name, grid=(ngroups, nq),
        in_specs=[pl.BlockSpec((W, lanes), lambda p, i: (i, p)),
                  pl.BlockSpec((T, lanes), lambda p, i: (0, ngroups + p)),
                  pl.BlockSpec((T, lanes), lambda p, i: (0, 2 * ngroups + p)),
                  pl.BlockSpec((W, lanes), lambda p, i: (i, p)), ANY, ANY] + [ANY] * nx,
        out_specs=[pl.BlockSpec((W, lanes), lambda p, i: (i, p)),
                   pl.BlockSpec((T, lanes), lambda p, i: (0, p)),
                   pl.BlockSpec((T, lanes), lambda p, i: (0, p))] + [ANY] * nx,
        out_shape=[_sds((T, D), BF16)] * 3 + (plan.out_shape if plan else []),
        scratch_shapes=[pltpu.VMEM((T, lanes), F32), pltpu.VMEM((T, lanes), F32),
                        pltpu.VMEM((2, H, W, W), BF16), pltpu.VMEM((2, H, W, W), BF16),
                        pltpu.SemaphoreType.DMA((2, 2))]
        + (plan.scratch if plan else []),
        compiler_params=_params("arbitrary", "arbitrary"),
    )(qkv, qkv, qkv, do, w_sv, s_sv, *extra)


HALO = SUBLANE


def _lru_gates(u, w_rg, b_rg, w_ig, b_ig, lam):
    nb = w_rg.shape[0]
    pre_r, pre_i = [], []
    for n in range(nb):
        ub = u[:, n * LANE:(n + 1) * LANE].astype(BF16)
        pre_r.append(jnp.dot(ub, w_rg[n], preferred_element_type=F32))
        pre_i.append(jnp.dot(ub, w_ig[n], preferred_element_type=F32))
    r = _sigmoid(jnp.concatenate(pre_r, axis=1) + b_rg)
    i = _sigmoid(jnp.concatenate(pre_i, axis=1) + b_ig)
    c = -LRU_C * _softplus_parts(-lam)[0]
    log_a = c * r
    a = jnp.exp(log_a)
    x2 = 2.0 * log_a
    em1 = jnp.where(jnp.abs(x2) < 1e-2, x2 * (1.0 + x2 * (0.5 + x2 * (1.0 / 6.0))), jnp.exp(x2) - 1.0)
    mult = jnp.sqrt(-em1)
    return r, i, a, mult, c


def _conv_rows(buf_ref, tt, conv_w, conv_b):
    u = conv_b
    for j in range(4):
        u = u + buf_ref[pl.ds(HALO - 3 + j, tt), :] * conv_w[j:j + 1, :]
    return u


def _fill_with_halo(buf_ref, prev_ref, cur_ref, first):
    tt = cur_ref.shape[0]
    buf_ref[pl.ds(0, HALO), :] = jnp.where(first, 0.0, prev_ref[...])
    buf_ref[pl.ds(HALO, tt), :] = cur_ref[...]


def _lru_time_tile(T):
    return _tile(T, 256, SUBLANE)


def _lru_pre(gr, conv_w, conv_b, w_rg, b_rg, w_ig, b_ig, lam, *, name):
    T = gr.shape[0]
    D = gr.shape[1] // 2
    tt = _lru_time_tile(T)
    hb = tt // HALO

    def body(x_ref, xp_ref, cw_ref, cb_ref, wr_ref, br_ref, wi_ref, bi_ref, lam_ref, a_ref, b_ref, buf):
        _fill_with_halo(buf, xp_ref, x_ref, pl.program_id(0) == 0)
        u = _conv_rows(buf, tt, cw_ref[...], cb_ref[...])
        _, i, a, mult, _ = _lru_gates(u, wr_ref, br_ref[...], wi_ref, bi_ref[...], lam_ref[...])
        a_ref[...] = a
        b_ref[...] = mult * (i * u)

    vec = pl.BlockSpec((1, D), lambda t: (0, 0))
    mat = pl.BlockSpec(w_rg.shape, lambda t: (0, 0, 0))
    return _pcall(
        body, name=name, grid=(T // tt,),
        in_specs=[pl.BlockSpec((tt, D), lambda t: (t, 1)),
                  pl.BlockSpec((HALO, D), lambda t: (jnp.maximum(t * hb - 1, 0), 1)),
                  pl.BlockSpec((4, D), lambda t: (0, 0)), vec, mat, vec, mat, vec, vec],
        out_specs=[pl.BlockSpec((tt, D), lambda t: (t, 0))] * 2,
        out_shape=[_sds((T, D), F32)] * 2,
        scratch_shapes=[pltpu.VMEM((tt + HALO, D), F32)],
        compiler_params=_params("parallel"),
    )(gr, gr, conv_w, conv_b, w_rg, b_rg, w_ig, b_ig, lam)


def _lru_scan(a, b, *, reverse, name):
    T, D = a.shape
    R = SUBLANE
    ts = _tile(T, 528, R)
    nt = T // ts

    def body(a_ref, b_ref, o_ref, carry):
        @pl.when(pl.program_id(0) == 0)
        def _():
            carry[...] = jnp.zeros_like(carry)

        rowid = lax.broadcasted_iota(jnp.int32, (R, D), 0)

        def chunk(k, run):
            if reverse:
                k = ts // R - 1 - k
            rows = pl.ds(pl.multiple_of(k * R, R), R)
            at, bt = a_ref[rows, :], b_ref[rows, :]
            out = jnp.zeros((R, D), F32)
            for r in (range(R - 1, -1, -1) if reverse else range(R)):
                if reverse:
                    cand = bt + run
                    nxt = at * cand
                else:
                    cand = at * run + bt
                    nxt = cand
                out = jnp.where(rowid == r, cand, out)
                run = jnp.broadcast_to(nxt[r:r + 1, :], (R, D))
            o_ref[rows, :] = out
            return run

        carry[...] = lax.fori_loop(0, ts // R, chunk, carry[...])

    if reverse:
        spec = pl.BlockSpec((ts, D), lambda t: (nt - 1 - t, 0))
    else:
        spec = pl.BlockSpec((ts, D), lambda t: (t, 0))
    return _pcall(
        body, name=name, grid=(nt,),
        in_specs=[spec, spec], out_specs=spec,
        out_shape=_sds((T, D), F32),
        scratch_shapes=[pltpu.VMEM((R, D), F32)],
        compiler_params=_params("arbitrary"),
    )(a, b)


def _lru_out(gr, hs, w, res, *, name):
    T, D = hs.shape
    tt = _tile(T, 528, 16)

    def body(g_ref, h_ref, w_ref, r_ref, y_ref, o_ref):
        y = (h_ref[...] * _gelu_parts(g_ref[...])[0]).astype(BF16)
        y_ref[...] = y
        o_ref[...] = r_ref[...] + jnp.dot(y, w_ref[...], preferred_element_type=F32)

    blk = pl.BlockSpec((tt, D), lambda t: (t, 0))
    return _pcall(
        body, name=name, grid=(T // tt,),
        in_specs=[blk, blk, pl.BlockSpec((D, D), lambda t: (0, 0)), blk],
        out_specs=[blk, blk],
        out_shape=[_sds((T, D), BF16), _sds((T, D), F32)],
        compiler_params=_params("parallel"),
    )(gr, hs, w, res)


def _lru_out_bwd(gr, hs, dout, w, *, name):
    T, D = hs.shape
    tt = _tile(T, 528, 16)

    def body(g_ref, h_ref, d_ref, w_ref, dg_ref, dh_ref):
        dy = lax.dot_general(d_ref[...], w_ref[...], (((1,), (1,)), ((), ())), preferred_element_type=F32)
        gelu, dgelu = _gelu_parts(g_ref[...])
        dg_ref[...] = (dy * h_ref[...] * dgelu).astype(BF16)
        dh_ref[...] = dy * gelu

    spec = pl.BlockSpec((tt, D), lambda t: (t, 0))
    return _pcall(
        body, name=name, grid=(T // tt,),
        in_specs=[spec, spec, spec, pl.BlockSpec((D, D), lambda t: (0, 0))], out_specs=[spec, spec],
        out_shape=[_sds((T, 2 * D), BF16), _sds((T, D), F32)],
        compiler_params=_params("parallel"),
    )(gr, hs, dout, w)


def _lru_gate_bwd(gr, hs, lmb, conv_w, conv_b, w_rg, b_rg, w_ig, b_ig, lam, *, name):
    T, D = hs.shape
    nb = w_rg.shape[0]
    tt = _lru_time_tile(T)
    hb = tt // HALO
    nt = T // tt

    def body(x_ref, xp_ref, h_ref, hp_ref, l_ref, cw_ref, cb_ref, wr_ref, br_ref, wi_ref, bi_ref, lam_ref,
             du_ref, dwr_ref, dbr_ref, dwi_ref, dbi_ref, dlam_ref, xbuf, hbuf):
        t = pl.program_id(0)
        first = t == 0
        _fill_with_halo(xbuf, xp_ref, x_ref, first)
        _fill_with_halo(hbuf, hp_ref, h_ref, first)
        u = _conv_rows(xbuf, tt, cw_ref[...], cb_ref[...])
        lam_v = lam_ref[...]
        r, i, a, mult, c = _lru_gates(u, wr_ref, br_ref[...], wi_ref, bi_ref[...], lam_v)
        l = l_ref[...]
        h_prev = hbuf[pl.ds(HALO - 1, tt), :]
        dlog_a = l * h_prev * a - l * (i * u) * (a * a) / mult
        d_iu = l * mult
        dpre_r = (dlog_a * c) * (r * (1.0 - r))
        dpre_i = (d_iu * u) * (i * (1.0 - i))
        dpr_b = dpre_r.astype(BF16)
        dpi_b = dpre_i.astype(BF16)
        du_parts, dwr, dwi = [], [], []
        for n in range(nb):
            cs = slice(n * LANE, (n + 1) * LANE)
            ub = u[:, cs].astype(BF16)
            du_parts.append(
                lax.dot_general(dpr_b[:, cs], wr_ref[n], (((1,), (1,)), ((), ())), preferred_element_type=F32)
                + lax.dot_general(dpi_b[:, cs], wi_ref[n], (((1,), (1,)), ((), ())), preferred_element_type=F32))
            dwr.append(lax.dot_general(ub, dpr_b[:, cs], (((0,), (0,)), ((), ())), preferred_element_type=F32))
            dwi.append(lax.dot_general(ub, dpi_b[:, cs], (((0,), (0,)), ((), ())), preferred_element_type=F32))
        du_ref[...] = d_iu * i + jnp.concatenate(du_parts, axis=1)
        dbr = jnp.sum(dpre_r, axis=0, keepdims=True)
        dbi = jnp.sum(dpre_i, axis=0, keepdims=True)
        dc = jnp.sum(dlog_a * r, axis=0, keepdims=True)

        @pl.when(first)
        def _():
            for n in range(nb):
                dwr_ref[n] = dwr[n]
                dwi_ref[n] = dwi[n]
            dbr_ref[...] = dbr
            dbi_ref[...] = dbi
            dlam_ref[...] = dc

        @pl.when(t > 0)
        def _():
            for n in range(nb):
                dwr_ref[n] += dwr[n]
                dwi_ref[n] += dwi[n]
            dbr_ref[...] += dbr
            dbi_ref[...] += dbi
            dlam_ref[...] += dc

        @pl.when(t == nt - 1)
        def _():
            dlam_ref[...] = dlam_ref[...] * (LRU_C * _sigmoid(-lam_v))

    vec = pl.BlockSpec((1, D), lambda t: (0, 0))
    mat = pl.BlockSpec(w_rg.shape, lambda t: (0, 0, 0))
    blk = pl.BlockSpec((tt, D), lambda t: (t, 0))
    prev = pl.BlockSpec((HALO, D), lambda t: (jnp.maximum(t * hb - 1, 0), 0))
    return _pcall(
        body, name=name, grid=(nt,),
        in_specs=[pl.BlockSpec((tt, D), lambda t: (t, 1)),
                  pl.BlockSpec((HALO, D), lambda t: (jnp.maximum(t * hb - 1, 0), 1)),
                  blk, prev, blk,
                  pl.BlockSpec((4, D), lambda t: (0, 0)), vec, mat, vec, mat, vec, vec],
        out_specs=[blk, mat, vec, mat, vec, vec],
        out_shape=[_sds((T, D), F32), _sds(w_rg.shape, F32), _sds((1, D), F32),
                   _sds(w_rg.shape, F32), _sds((1, D), F32), _sds((1, D), F32)],
        scratch_shapes=[pltpu.VMEM((tt + HALO, D), F32), pltpu.VMEM((tt + HALO, D), F32)],
        compiler_params=_params("arbitrary"),
    )(gr, gr, hs, hs, lmb, conv_w, conv_b, w_rg, b_rg, w_ig, b_ig, lam)


def _lru_conv_bwd(gr, du, conv_w, dgr, *, name):
    T, D = du.shape
    tt = _lru_time_tile(T)
    hb = tt // HALO
    nt = T // tt

    def body(x_ref, xp_ref, du_ref, dun_ref, cw_ref, _, dx_ref, dcw_ref, dcb_ref, xbuf, dbuf):
        t = pl.program_id(0)
        _fill_with_halo(xbuf, xp_ref, x_ref, t == 0)
        du = du_ref[...]
        dbuf[pl.ds(0, tt), :] = du
        dbuf[pl.ds(tt, HALO), :] = jnp.where(t == nt - 1, 0.0, dun_ref[...])
        cw = cw_ref[...]
        dx = jnp.zeros((tt, D), F32)
        dcw = []
        for j in range(4):
            dx = dx + dbuf[pl.ds(3 - j, tt), :] * cw[j:j + 1, :]
            dcw.append(jnp.sum(du * xbuf[pl.ds(HALO - 3 + j, tt), :], axis=0, keepdims=True))
        dx_ref[...] = dx.astype(BF16)
        dcw = jnp.concatenate(dcw, axis=0)
        dcb = jnp.sum(du, axis=0, keepdims=True)

        @pl.when(t == 0)
        def _():
            dcw_ref[...] = dcw
            dcb_ref[...] = dcb

        @pl.when(t > 0)
        def _():
            dcw_ref[...] += dcw
            dcb_ref[...] += dcb

    blk = pl.BlockSpec((tt, D), lambda t: (t, 0))
    return _pcall(
        body, name=name, grid=(nt,),
        in_specs=[pl.BlockSpec((tt, D), lambda t: (t, 1)),
                  pl.BlockSpec((HALO, D), lambda t: (jnp.maximum(t * hb - 1, 0), 1)),
                  blk,
                  pl.BlockSpec((HALO, D), lambda t: (jnp.minimum((t + 1) * hb, T // HALO - 1), 0)),
                  pl.BlockSpec((4, D), lambda t: (0, 0)), ANY],
        out_specs=[pl.BlockSpec((tt, D), lambda t: (t, 1)),
                   pl.BlockSpec((4, D), lambda t: (0, 0)), pl.BlockSpec((1, D), lambda t: (0, 0))],
        out_shape=[_sds((T, 2 * D), BF16), _sds((4, D), F32), _sds((1, D), F32)],
        input_output_aliases={5: 0},
        scratch_shapes=[pltpu.VMEM((tt + HALO, D), F32), pltpu.VMEM((tt + HALO, D), F32)],
        compiler_params=_params("arbitrary"),
    )(gr, gr, du, du, conv_w, dgr)


def _loss_head(h, g, target, *, row_lo, row_hi, name):
    T, D = h.shape
    tm = _tile(T, 1056, 16)

    def body(h_ref, g_ref, t_ref, loss_ref, dh_ref, dhb_ref, dg_ref):
        i = pl.program_id(0)
        x = h_ref[...]
        g = g_ref[...]
        row = i * tm + lax.broadcasted_iota(jnp.int32, (tm, 1), 0)
        valid = jnp.logical_and(row >= row_lo, row < row_hi)
        rstd = _rstd(x)
        n = x * rstd
        err = jnp.where(valid, n * g - t_ref[...], 0.0)
        part = (0.5 / D) * jnp.sum(jnp.sum(err * err, axis=1, keepdims=True), axis=0, keepdims=True)
        dy = err * (1.0 / D)
        dn = dy * g
        dh = rstd * (dn - n * jnp.mean(dn * n, axis=-1, keepdims=True))
        dh_ref[...] = dh
        dhb_ref[...] = dh.astype(BF16)
        dg = jnp.sum(dy * n, axis=0, keepdims=True)

        @pl.when(i == 0)
        def _():
            loss_ref[...] = part
            dg_ref[...] = dg

        @pl.when(i > 0)
        def _():
            loss_ref[...] += part
            dg_ref[...] += dg

    blk = pl.BlockSpec((tm, D), lambda i: (i, 0))
    vec = pl.BlockSpec((1, D), lambda i: (0, 0))
    return _pcall(
        body, name=name, grid=(T // tm,),
        in_specs=[blk, vec, blk],
        out_specs=[pl.BlockSpec((1, 1), lambda i: (0, 0)), blk, blk, vec],
        out_shape=[_sds((1, 1), F32), _sds((T, D), F32), _sds((T, D), BF16), _sds((1, D), F32)],
        compiler_params=_params("arbitrary"),
    )(h, g, target)


def _adamw_math(w, g, m, v):
    c1 = 1.0 / (1.0 - ADAM_B1 ** ADAM_STEP)
    c2 = 1.0 / (1.0 - ADAM_B2 ** ADAM_STEP)
    m = ADAM_B1 * m + (1.0 - ADAM_B1) * g
    v = ADAM_B2 * v + (1.0 - ADAM_B2) * (g * g)
    delta = -ADAM_LR * ((m * c1) / (jnp.sqrt(v * c2) + ADAM_EPS) + ADAM_WD * w)
    return delta, m, v


def _adamw_halves(w, mine, theirs, m, v, c, *, name):
    _, R, C = w.shape
    tr = _tile(R, 256, SUBLANE)

    def body(c_ref, w_ref, a_ref, b_ref, m_ref, v_ref, g_ref, d_ref, nm_ref, nv_ref):
        g = jnp.where(pl.program_id(0) == c_ref[0], a_ref[...], b_ref[...])
        g_ref[...] = g
        d_ref[...], nm_ref[...], nv_ref[...] = _adamw_math(w_ref[...], g, m_ref[...], v_ref[...])

    full = pl.BlockSpec((None, tr, C), lambda h, i, c_ref: (h, i, 0))
    half = pl.BlockSpec((tr, C), lambda h, i, c_ref: (i, 0))
    return _pcall(
        body, name=name,
        grid_spec=pltpu.PrefetchScalarGridSpec(
            num_scalar_prefetch=1, grid=(2, R // tr),
            in_specs=[full, half, half, full, full], out_specs=[full] * 4),
        out_shape=[_sds((2, R, C), F32)] * 4,
        compiler_params=_params("parallel", "parallel"),
    )(c, w, mine, theirs, m, v)


def _adamw(w, g, m, v, *, name):
    R, C = w.shape
    tr = _tile(R, 512, SUBLANE)

    def body(w_ref, g_ref, m_ref, v_ref, d_ref, nm_ref, nv_ref):
        d_ref[...], nm_ref[...], nv_ref[...] = _adamw_math(w_ref[...], g_ref[...], m_ref[...], v_ref[...])

    blk = pl.BlockSpec((tr, C), lambda i: (i, 0))
    return _pcall(
        body, name=name, grid=(R // tr,),
        in_specs=[blk] * 4, out_specs=[blk] * 3,
        out_shape=[_sds((R, C), F32)] * 3,
        compiler_params=_params("parallel"),
    )(w, g, m, v)


ANY = pl.BlockSpec(memory_space=pl.ANY)


def _place():
    x, y, c = lax.axis_index("x"), lax.axis_index("y"), lax.axis_index("c")
    chips = [(1 - x, y), (x, 1 - y), (1 - x, 1 - y)]
    return x, y, c, chips


LOCAL_PIECES = 4


class _GatherChips:
    def __init__(self, vs):
        self.inputs = list(vs)
        n = self.n = len(vs)
        self.out_shape = [_sds((4,) + v.shape, v.dtype) for v in vs]
        self.scratch = [pltpu.SemaphoreType.DMA((6 * n,)), pltpu.SemaphoreType.DMA((6 * n,)),
                        pltpu.SemaphoreType.DMA((LOCAL_PIECES * n,))]

    def _copies(self, v_refs, o_refs, sems):
        send_sems, recv_sems, local_sems = sems
        x, y, c, chips = _place()
        me = 2 * x + y

        def copy(a, k, block, half, to, src=None):
            dst = o_refs[a].at[block, half]
            return pltpu.make_async_remote_copy(
                src_ref=dst if src is None else src, dst_ref=dst,
                send_sem=send_sems.at[6 * a + k], recv_sem=recv_sems.at[6 * a + k],
                device_id=to, device_id_type=MESH)

        ks = [(a, k, cx, cy) for a in range(self.n) for k, (cx, cy) in enumerate(chips)]

        def local():
            out = []
            for a in range(self.n):
                rows = self.inputs[a].shape[1] // (LOCAL_PIECES // 2)
                for p in range(LOCAL_PIECES):
                    h, r0 = p % 2, (p // 2) * rows
                    out.append(pltpu.make_async_copy(
                        v_refs[a].at[h, pl.ds(r0, rows)], o_refs[a].at[me, h, pl.ds(r0, rows)],
                        local_sems.at[LOCAL_PIECES * a + p]))
            return out

        return dict(
            first=lambda: [copy(a, k, me, c, (cx, cy, c), src=v_refs[a].at[c]) for a, k, cx, cy in ks],
            landed=lambda: [copy(a, k, 2 * cx + cy, c, (x, y, c)) for a, k, cx, cy in ks],
            passed=lambda: [copy(a, 3 + k, 2 * cx + cy, c, (x, y, 1 - c)) for a, k, cx, cy in ks],
            final=lambda: [copy(a, 3 + k, 2 * cx + cy, 1 - c, (x, y, c)) for a, k, cx, cy in ks],
            local=local)

    def start(self, v_refs, o_refs, sems):
        cps = self._copies(v_refs, o_refs, sems)
        for cp in cps["first"]() + cps["local"]():
            cp.start()

    def mid(self, v_refs, o_refs, sems):
        cps = self._copies(v_refs, o_refs, sems)
        for got, fwd in zip(cps["landed"](), cps["passed"]()):
            got.wait_recv()
            fwd.start()

    def finish(self, v_refs, o_refs, sems):
        cps = self._copies(v_refs, o_refs, sems)
        for cp in cps["final"]():
            cp.wait_recv()
        for cp in cps["first"]() + cps["passed"]():
            cp.wait_send()
        for cp in cps["local"]():
            cp.wait()


class _ExchangeBlocks:
    def __init__(self, ps):
        self.inputs = list(ps)
        n = self.n = len(ps)
        self.out_shape = [_sds((8,) + p.shape[2:], p.dtype) for p in ps]
        self.scratch = [pltpu.SemaphoreType.DMA((7 * n,)), pltpu.SemaphoreType.DMA((7 * n,))]

    def _copies(self, p_refs, o_refs, sems, incoming):
        send_sems, recv_sems = sems
        x, y, c, _ = _place()
        me = 4 * x + 2 * y + c
        out = []
        for a in range(self.n):
            for k in range(1, 8):
                px, py, pc = x ^ (k >> 2), y ^ ((k >> 1) & 1), c ^ (k & 1)
                out.append(pltpu.make_async_remote_copy(
                    src_ref=p_refs[a].at[2 * px + py, pc],
                    dst_ref=o_refs[a].at[4 * px + 2 * py + pc if incoming else me],
                    send_sem=send_sems.at[7 * a + k - 1], recv_sem=recv_sems.at[7 * a + k - 1],
                    device_id=(x, y, c) if incoming else (px, py, pc), device_id_type=MESH))
        return out

    def start(self, p_refs, o_refs, sems):
        for cp in self._copies(p_refs, o_refs, sems, False):
            cp.start()

    def mid(self, p_refs, o_refs, sems):
        pass

    def finish(self, p_refs, o_refs, sems):
        for cp in self._copies(p_refs, o_refs, sems, True):
            cp.wait_recv()
        for cp in self._copies(p_refs, o_refs, sems, False):
            cp.wait_send()


def _run_exchange(plan, *, name):
    n = plan.n

    def body(*refs):
        args = (refs[:n], refs[n:2 * n], refs[2 * n:])
        plan.start(*args)
        plan.mid(*args)
        plan.finish(*args)

    return _pcall(
        body, name=name, in_specs=[ANY] * n, out_specs=[ANY] * n,
        out_shape=plan.out_shape, scratch_shapes=plan.scratch,
    )(*plan.inputs)


def _send_sibling(rs, *, name):
    n = len(rs)

    def body(*refs):
        r_refs, o_refs = refs[:n], refs[n:2 * n]
        send_sems, recv_sems = refs[2 * n:]
        x, y, c, _ = _place()
        cps = [pltpu.make_async_remote_copy(
            src_ref=r_refs[a], dst_ref=o_refs[a], send_sem=send_sems.at[a], recv_sem=recv_sems.at[a],
            device_id=(x, y, 1 - c), device_id_type=MESH) for a in range(n)]
        for cp in cps:
            cp.start()
        for cp in cps:
            cp.wait()

    return _pcall(
        body, name=name, in_specs=[ANY] * n, out_specs=[ANY] * n,
        out_shape=[_sds(r.shape, r.dtype) for r in rs],
        scratch_shapes=[pltpu.SemaphoreType.DMA((n,)), pltpu.SemaphoreType.DMA((n,))],
    )(*rs)


def _add_devices(p, got, place, *, name):
    _, _, R, C = p.shape
    tr = _tile(R, 256, 16)

    def body(place_ref, p_ref, o_ref, out_ref):
        me = place_ref[2]
        own = p_ref[...].astype(F32)
        acc = jnp.where(me == 0, own, o_ref[0].astype(F32))
        for d in range(1, 8):
            acc = acc + jnp.where(me == d, own, o_ref[d].astype(F32))
        out_ref[...] = acc

    return _pcall(
        body, name=name,
        grid_spec=pltpu.PrefetchScalarGridSpec(
            num_scalar_prefetch=1, grid=(R // tr,),
            in_specs=[pl.BlockSpec((None, None, tr, C), lambda i, pr: (pr[0], pr[1], i, 0)),
                      pl.BlockSpec((8, tr, C), lambda i, pr: (0, i, 0))],
            out_specs=pl.BlockSpec((tr, C), lambda i, pr: (i, 0))),
        out_shape=_sds((R, C), F32),
        compiler_params=_params("parallel"),
    )(place, p, got)


def _round_up(n, m):
    return (n + m - 1) // m * m


def _f32_as_bf16(a):
    return lax.bitcast_convert_type(a.astype(F32), BF16).reshape(-1)


def _bf16_as_f32(a):
    return lax.bitcast_convert_type(a.reshape(-1, 2), F32)


def _by_chip_cols(a, cols):
    lead = a.shape[:-1]
    a = a.reshape(lead + (4, cols))
    return jnp.moveaxis(a, -2, 0).reshape(4, -1)


def kernel(x, meta_tokens, norm_mix, norm_mlp, sb_w_qkv, sb_w_o, lru_w_in, lru_conv_w, lru_conv_b, lru_w_rg, lru_b_rg, lru_w_ig, lru_b_ig, lru_lambda, lru_w_out, mlp_w_up, mlp_w_down, norm_final, loss_target, m_meta_tokens, m_norm_mix, m_norm_mlp, m_sb_w_qkv, m_sb_w_o, m_lru_w_in, m_lru_conv_w, m_lru_conv_b, m_lru_w_rg, m_lru_b_rg, m_lru_w_ig, m_lru_b_ig, m_lru_lambda, m_lru_w_out, m_mlp_w_up, m_mlp_w_down, m_norm_final, v_meta_tokens, v_norm_mix, v_norm_mlp, v_sb_w_qkv, v_sb_w_o, v_lru_w_in, v_lru_conv_w, v_lru_conv_b, v_lru_w_rg, v_lru_b_rg, v_lru_w_ig, v_lru_b_ig, v_lru_lambda, v_lru_w_out, v_mlp_w_up, v_mlp_w_down, v_norm_final):
    weights = dict(meta_tokens=meta_tokens, norm_mix=norm_mix, norm_mlp=norm_mlp, sb_w_qkv=sb_w_qkv,
                   sb_w_o=sb_w_o, lru_w_in=lru_w_in, lru_conv_w=lru_conv_w, lru_conv_b=lru_conv_b,
                   lru_w_rg=lru_w_rg, lru_b_rg=lru_b_rg, lru_w_ig=lru_w_ig, lru_b_ig=lru_b_ig,
                   lru_lambda=lru_lambda, lru_w_out=lru_w_out, mlp_w_up=mlp_w_up, mlp_w_down=mlp_w_down,
                   norm_final=norm_final)
    m_in = dict(meta_tokens=m_meta_tokens, norm_mix=m_norm_mix, norm_mlp=m_norm_mlp, sb_w_qkv=m_sb_w_qkv,
                sb_w_o=m_sb_w_o, lru_w_in=m_lru_w_in, lru_conv_w=m_lru_conv_w, lru_conv_b=m_lru_conv_b,
                lru_w_rg=m_lru_w_rg, lru_b_rg=m_lru_b_rg, lru_w_ig=m_lru_w_ig, lru_b_ig=m_lru_b_ig,
                lru_lambda=m_lru_lambda, lru_w_out=m_lru_w_out, mlp_w_up=m_mlp_w_up,
                mlp_w_down=m_mlp_w_down, norm_final=m_norm_final)
    v_in = dict(meta_tokens=v_meta_tokens, norm_mix=v_norm_mix, norm_mlp=v_norm_mlp, sb_w_qkv=v_sb_w_qkv,
                sb_w_o=v_sb_w_o, lru_w_in=v_lru_w_in, lru_conv_w=v_lru_conv_w, lru_conv_b=v_lru_conv_b,
                lru_w_rg=v_lru_w_rg, lru_b_rg=v_lru_b_rg, lru_w_ig=v_lru_w_ig, lru_b_ig=v_lru_b_ig,
                lru_lambda=v_lru_lambda, lru_w_out=v_lru_w_out, mlp_w_up=v_mlp_w_up,
                mlp_w_down=v_mlp_w_down, norm_final=v_norm_final)
    names = list(weights)

    seq, D = x.shape[1], x.shape[2]
    n_meta = meta_tokens.shape[0]
    Dq = D // 4
    T = _round_up(n_meta + seq, ATT_BLOCK)
    nb = lru_w_rg.shape[1]
    F = mlp_w_up.shape[2]
    depth = mlp_w_up.shape[0]
    my_x, my_y, my_c = lax.axis_index("x"), lax.axis_index("y"), lax.axis_index("c")
    c_arr = jnp.reshape(my_c, (1,)).astype(jnp.int32)

    assert depth == 2

    def halves(a):
        return a.astype(BF16).reshape(2, a.shape[0] // 2, a.shape[1])

    small = [meta_tokens, lru_conv_w[0], lru_conv_b, lru_b_rg, lru_b_ig, lru_lambda]
    sparts = [_f32_as_bf16(s) for s in small]
    sizes = [p.shape[0] for p in sparts]
    total = _round_up(sum(sizes), 2 * 32 * LANE)
    sflat =jnp.concatenate(sparts + [jnp.zeros((total - sum(sizes),), BF16)]).reshape(2, -1, LANE)
    gq, gsm = _run_exchange(_GatherChips([halves(sb_w_qkv[0]), sflat]), name="gather_first")
    gather_rest = _GatherChips([halves(sb_w_o[0]), halves(lru_w_in[0]), halves(lru_w_out[0]),
                                mlp_w_up.astype(BF16), mlp_w_down.astype(BF16)])
    w_qkv = gq.reshape(4, D, 3 * Dq)
    gsm = gsm.reshape(4, total)
    offs = [sum(sizes[:k]) for k in range(len(sizes))]
    sm = [_bf16_as_f32(gsm[:, o:o + s]) for o, s in zip(offs, sizes)]
    meta_full = jnp.moveaxis(sm[0].reshape(4, n_meta, Dq), 0, 1).reshape(n_meta, D)
    conv_w = jnp.moveaxis(sm[1].reshape(4, 4, Dq), 0, 1).reshape(4, D)
    conv_b, b_rg, b_ig, lam = [s.reshape(1, D) for s in sm[2:6]]
    w_rg = lru_w_rg[0].astype(BF16)
    w_ig = lru_w_ig[0].astype(BF16)
    g_mix = [norm_mix[l].reshape(1, D) for l in range(depth)]
    g_mlp = [norm_mlp[l].reshape(1, D) for l in range(depth)]
    g_fin = norm_final.reshape(1, D)

    pad_rows = T - n_meta - seq
    h0 = jnp.concatenate([meta_full, x[0], jnp.zeros((pad_rows, D), F32)], axis=0)
    target = jnp.concatenate([jnp.zeros((n_meta, D), F32), loss_target[0], jnp.zeros((pad_rows, D), F32)], axis=0)

    hn0, qkv = _norm_mm(h0, g_mix[0], w_qkv, out_dtype=BF16, name="qkv_proj")
    att, w_sv, s_sv, go, gi, gout, w_up, w_down = _attn_fwd(qkv, name="attn_fwd", plan=gather_rest)
    w_o = go.reshape(D, D)
    w_in = gi.reshape(4, D, 2 * Dq)
    w_out = gout.reshape(D, D)
    h1 = _mm_res(att, w_o, h0, name="attn_out")
    h2, hnm0, up0 = _mlp_fwd(h1, g_mlp[0], w_up, w_down, layer=0, name="mlp0_fwd")
    hn1, gr = _norm_mm(h2, g_mix[1], w_in, out_dtype=F32, name="lru_in")
    a_t, b_t = _lru_pre(gr, conv_w, conv_b, w_rg, b_rg, w_ig, b_ig, lam, name="lru_pre")
    hs = _lru_scan(a_t, b_t, reverse=False, name="lru_scan")
    y, h3 = _lru_out(gr, hs, w_out, h2, name="lru_out")
    h4, hnm1, up1 = _mlp_fwd(h3, g_mlp[1], w_up, w_down, layer=1, name="mlp1_fwd")
    loss, dh4, dh4b, dg_fin = _loss_head(h4, g_fin, target, row_lo=n_meta, row_hi=n_meta + seq, name="loss_head")

    dup1, dh3, dh3b, dg_mlp1 = _mlp_bwd(dh4, h3, g_mlp[1], up1, w_up, w_down, layer=1, name="mlp1_bwd")
    dw_up = _mm_tn(hnm1, dup1, shards=4, relu2=False, slot=1, name="mlp1_dwup")
    dw_down = _mm_tn(up1, dh4b, shards=1, relu2=True, slot=1, row_shards=4, name="mlp1_dwdown")
    dw_out = _mm_tn(y, dh3b, shards=1, relu2=False, name="lru_dwout")
    dgr, dhy = _lru_out_bwd(gr, hs, dh3b, w_out, name="lru_out_bwd")
    lmb = _lru_scan(a_t, dhy, reverse=True, name="lru_scan_bwd")
    du, dw_rg, db_rg, dw_ig, db_ig, dlam = _lru_gate_bwd(
        gr, hs, lmb, conv_w, conv_b, w_rg, b_rg, w_ig, b_ig, lam, name="lru_gate_bwd")
    dgr, dconv_w, dconv_b = _lru_conv_bwd(gr, du, conv_w, dgr, name="lru_conv_bwd")
    dh2, dh2b, dg_mix1 = _mm_nt_normbwd(dgr, w_in, h2, g_mix[1], dh3, name="lru_in_bwd")
    dw_in = _mm_tn(hn1, dgr, shards=4, relu2=False, name="lru_dwin")
    dup0, dh1, dh1b, dg_mlp0 = _mlp_bwd(dh2, h1, g_mlp[0], up0, w_up, w_down, layer=0, name="mlp0_bwd")
    dw_up = _mm_tn(hnm0, dup0, shards=4, relu2=False, slot=0, into=dw_up, name="mlp0_dwup")
    dw_down = _mm_tn(up0, dh2b, shards=1, relu2=True, slot=0, into=dw_down, row_shards=4, name="mlp0_dwdown")
    datt = _mm_nt(dh1b, w_o, out_dtype=BF16, name="attn_out_bwd")
    dw_o = _mm_tn(att, dh1b, shards=1, relu2=False, name="attn_dwo")
    def halves_of(d, rows):
        return d.reshape(4, 2, rows // 2, d.shape[-1])

    early = [halves_of(dw_o, Dq), halves_of(dw_in, D), halves_of(dw_out, Dq), dw_up, dw_down]
    dq, dk, dv, *got_early = _attn_bwd(qkv, datt, w_sv, s_sv, name="attn_bwd", plan=_ExchangeBlocks(early))
    dqkv = jnp.concatenate([dq, dk, dv], axis=1)
    dw_qkv = halves_of(_mm_tn(hn0, dqkv, shards=4, relu2=False, name="attn_dwqkv"), D)
    dh0, _, dg_mix0, got_qkv = _mm_nt_normbwd(dqkv, w_qkv, h0, g_mix[0], dh1, name="qkv_bwd",
                                               plan=_ExchangeBlocks([dw_qkv]))
    grad_x = dh0[n_meta:n_meta + seq][None]
    dmeta = dh0[:n_meta]

    large = ["sb_w_o", "lru_w_in", "lru_w_out", "mlp_w_up", "mlp_w_down", "sb_w_qkv"]
    sharded =[_by_chip_cols(dmeta, Dq), _by_chip_cols(dconv_w, Dq), dconv_b.reshape(4, Dq),
               db_rg.reshape(4, Dq), db_ig.reshape(4, Dq), dlam.reshape(4, Dq)]
    repl = [jnp.concatenate([dg_mix0, dg_mix1], axis=0).reshape(-1),
            jnp.concatenate([dg_mlp0, dg_mlp1], axis=0).reshape(-1),
            dg_fin.reshape(-1), dw_rg.reshape(-1), dw_ig.reshape(-1), loss.reshape(-1)]
    rsizes = [r.shape[0] for r in repl]
    rtotal = _round_up(sum(rsizes), 4 * 2 * 16 * LANE)
    rflat = jnp.concatenate(repl + [jnp.zeros((rtotal - sum(rsizes),), F32)]).reshape(4, rtotal // 4)
    gsizes = [s.shape[1] for s in sharded] + [rtotal // 4]
    gtotal = _round_up(sum(gsizes), 2 * 16 * LANE)
    tail = jnp.concatenate(sharded + [rflat, jnp.zeros((4, gtotal - sum(gsizes)), F32)], axis=1)
    late = [tail.reshape(4, 2, -1, LANE)]
    got_late = _run_exchange(_ExchangeBlocks(late), name="reduce_late")
    place = jnp.stack([2 * my_x + my_y, my_c, 4 * my_x + 2 * my_y + my_c]).astype(jnp.int32)
    mine = [_add_devices(p, o, place, name="reduce_add_" + t)
            for p, o, t in zip(early + [dw_qkv] + late, list(got_early) + [got_qkv] + list(got_late),
                               large + ["tail"])]
    theirs = _send_sibling(mine, name="reduce_join")

    grads, delta, new_m, new_v = {}, {}, {}, {}
    for n, a, b in zip(large, mine, theirs):
        shp = weights[n].shape
        view = (2,) + a.shape
        g, d, nm, nv = _adamw_halves(weights[n].reshape(view), a, b, m_in[n].reshape(view),
                                     v_in[n].reshape(view), c_arr, name="adamw_" + n)
        grads[n], delta[n], new_m[n], new_v[n] = g.reshape(shp), d.reshape(shp), nm.reshape(shp), nv.reshape(shp)

    lo = jnp.where(my_c == 0, mine[-1], theirs[-1])
    hi = jnp.where(my_c == 0, theirs[-1], mine[-1])
    gshard = jnp.concatenate([lo, hi], axis=0).reshape(gtotal)
    goffs = [sum(gsizes[:k]) for k in range(len(gsizes))]
    gp = [gshard[o:o + s] for o, s in zip(goffs, gsizes)]
    rfull = _run_exchange(_GatherChips([gp[-1].reshape(2, -1, LANE)]), name="gather_replicated")[0].reshape(rtotal)
    roffs = [sum(rsizes[:k]) for k in range(len(rsizes))]
    rp = [rfull[o:o + s] for o, s in zip(roffs, rsizes)]
    grads.update(meta_tokens=gp[0], lru_conv_w=gp[1], lru_conv_b=gp[2], lru_b_rg=gp[3], lru_b_ig=gp[4],
                 lru_lambda=gp[5], norm_mix=rp[0], norm_mlp=rp[1], norm_final=rp[2], lru_w_rg=rp[3],
                 lru_w_ig=rp[4])
    grads = {n: grads[n].reshape(weights[n].shape) for n in names}
    rest = [n for n in names if n not in large]
    ssz = [weights[n].size for n in rest]
    small_cols = 8 * LANE
    stotal = _round_up(sum(ssz), SUBLANE * small_cols)

    def pack(src):
        return jnp.concatenate([src[n].reshape(-1) for n in rest]
                               + [jnp.ones((stotal - sum(ssz),), F32)]).reshape(-1, small_cols)

    d, nm, nv = _adamw(pack(weights), pack(grads), pack(m_in), pack(v_in), name="adamw_small")
    soffs = [sum(ssz[:k]) for k in range(len(ssz))]
    for n, o, s in zip(rest, soffs, ssz):
        shp = weights[n].shape
        delta[n] = d.reshape(-1)[o:o + s].reshape(shp)
        new_m[n] = nm.reshape(-1)[o:o + s].reshape(shp)
        new_v[n] = nv.reshape(-1)[o:o + s].reshape(shp)

    loss = rp[5][0]
    return (loss, grad_x, *[grads[n] for n in names], *[delta[n] for n in names],
            *[new_m[n] for n in names], *[new_v[n] for n in names])
```

```python
import jax
import jax.numpy as jnp
from jax import lax
from jax.experimental import pallas as pl
from jax.experimental.pallas import tpu as pltpu

F32 = jnp.float32
BF16 = jnp.bfloat16
MESH = pl.DeviceIdType.MESH

EPS = 1e-6
HEAD_DIM = 64
LANE = 128
SUBLANE = 8
LRU_C = 8.0
BF16_ROWS = 16
VMEM_LIMIT = 56 * 1024 * 1024
ROWS_WIDE = 1056
ROWS_NARROW = 528

ADAM_LR = 0.001
ADAM_B1 = 0.9
ADAM_B2 = 0.999
ADAM_EPS = 1e-08
ADAM_WD = 0.01
ADAM_STEP = 10


def _pcall(body, **kw):
    return pl.pallas_call(body, **kw)


def _params(*sem):
    return pltpu.CompilerParams(dimension_semantics=sem, vmem_limit_bytes=VMEM_LIMIT)


def _tile(n, pref, align):
    best = None
    for t in range(align, min(n, pref) + 1, align):
        if n % t == 0:
            best = t
    return n if best is None else best


def _sds(shape, dtype):
    return jax.ShapeDtypeStruct(shape, dtype)


def _rstd(x):
    return lax.rsqrt(jnp.mean(x * x, axis=-1, keepdims=True) + EPS)


def _norm_bwd(x, g, dy):
    rstd = _rstd(x)
    n = x * rstd
    dn = dy * g
    dx = rstd * (dn - n * jnp.mean(dn * n, axis=-1, keepdims=True))
    dg = jnp.sum(dy * n, axis=0, keepdims=True)
    return dx, dg


def _softplus_parts(z):
    l1p = jnp.log(1.0 + jnp.exp(-jnp.abs(z)))
    return jnp.maximum(z, 0.0) + l1p, jnp.minimum(z, 0.0) - l1p


def _sigmoid(x):
    return 1.0 / (1.0 + jnp.exp(-x))


def _gelu_parts(x):
    k = 0.7978845608028654
    inner = k * (x + 0.044715 * (x * x * x))
    t = jnp.tanh(inner)
    gelu = 0.5 * x * (1.0 + t)
    dgelu = 0.5 * (1.0 + t) + 0.5 * x * (1.0 - t * t) * (k * (1.0 + 3.0 * 0.044715 * (x * x)))
    return gelu, dgelu


def _norm_mm(h, g, w, *, out_dtype, name):
    T, D = h.shape
    S, _, n = w.shape
    tm = _tile(T, ROWS_WIDE, BF16_ROWS)
    tn = _tile(n, 768, LANE)
    nj = n // tn

    def body(h_ref, g_ref, w_ref, hn_ref, o_ref):
        @pl.when(pl.program_id(1) == 0)
        def _():
            x = h_ref[...]
            hn_ref[...] = (x * _rstd(x) * g_ref[...]).astype(BF16)

        o_ref[...] = jnp.dot(hn_ref[...], w_ref[...], preferred_element_type=F32).astype(out_dtype)

    return _pcall(
        body, name=name, grid=(T // tm, S * nj),
        in_specs=[pl.BlockSpec((tm, D), lambda i, j: (i, 0)),
                  pl.BlockSpec((1, D), lambda i, j: (0, 0)),
                  pl.BlockSpec((None, D, tn), lambda i, j: (j // nj, 0, j % nj))],
        out_specs=[pl.BlockSpec((tm, D), lambda i, j: (i, 0)),
                   pl.BlockSpec((tm, tn), lambda i, j: (i, j))],
        out_shape=[_sds((T, D), BF16), _sds((T, S * n), out_dtype)],
        compiler_params=_params("parallel", "arbitrary"),
    )(h, g, w)


def _mm_res(a, w, res, *, name):
    T, K = a.shape
    N = w.shape[1]
    tm = _tile(T, ROWS_WIDE, BF16_ROWS)

    def body(a_ref, w_ref, r_ref, o_ref):
        o_ref[...] = r_ref[...] + jnp.dot(a_ref[...], w_ref[...], preferred_element_type=F32)

    return _pcall(
        body, name=name, grid=(T // tm,),
        in_specs=[pl.BlockSpec((tm, K), lambda i: (i, 0)),
                  pl.BlockSpec((K, N), lambda i: (0, 0)),
                  pl.BlockSpec((tm, N), lambda i: (i, 0))],
        out_specs=pl.BlockSpec((tm, N), lambda i: (i, 0)),
        out_shape=_sds((T, N), F32),
        compiler_params=_params("parallel"),
    )(a, w, res)


def _mm_nt(a, w, *, out_dtype, name):
    T, N = a.shape
    K = w.shape[0]
    tm = _tile(T, ROWS_WIDE, BF16_ROWS)

    def body(a_ref, w_ref, o_ref):
        o_ref[...] = lax.dot_general(a_ref[...], w_ref[...], (((1,), (1,)), ((), ())),
                                     preferred_element_type=F32).astype(out_dtype)

    return _pcall(
        body, name=name, grid=(T // tm,),
        in_specs=[pl.BlockSpec((tm, N), lambda i: (i, 0)),
                  pl.BlockSpec((K, N), lambda i: (0, 0))],
        out_specs=pl.BlockSpec((tm, K), lambda i: (i, 0)),
        out_shape=_sds((T, K), out_dtype),
        compiler_params=_params("parallel"),
    )(a, w)


def _mm_tn(a, b, *, shards, relu2, name, slot=None, into=None, row_shards=0, out_dtype=BF16):
    T, Ka = a.shape
    Nb = b.shape[1]
    n = Nb // shards
    tka = _tile(Ka, 512, LANE)
    tnb = _tile(n, 512, LANE)
    nj = n // tnb

    def body(a_ref, b_ref, *rest):
        o_ref = rest[-1]
        av = a_ref[...]
        if relu2:
            r = jnp.maximum(av, 0)
            av = r * r
        o_ref[...] = lax.dot_general(av, b_ref[...], (((0,), (0,)), ((), ())),
                                     preferred_element_type=F32).astype(out_dtype)

    in_specs = [pl.BlockSpec((T, tka), lambda i, j: (0, i)),
                pl.BlockSpec((T, tnb), lambda i, j: (0, j))]
    args = [a, b]
    aliases = {}
    if slot is None:
        out_spec = pl.BlockSpec((None, tka, tnb), lambda i, j: (j // nj, i, j % nj))
        out_shape = _sds((shards, Ka, n), out_dtype)
    else:
        if row_shards:
            ni = Ka // row_shards // tka
            out_spec = pl.BlockSpec((None, None, tka, tnb), lambda i, j: (i // ni, slot, i % ni, j))
            out_shape = _sds((row_shards, 2, Ka // row_shards, n), out_dtype)
        else:
            out_spec = pl.BlockSpec((None, None, tka, tnb), lambda i, j: (j // nj, slot, i, j % nj))
            out_shape = _sds((shards, 2, Ka, n), out_dtype)
        if into is not None:
            in_specs.append(pl.BlockSpec(memory_space=pl.ANY))
            args.append(into)
            aliases = {2: 0}
    return _pcall(
        body, name=name, grid=(Ka // tka, shards * nj),
        in_specs=in_specs, out_specs=out_spec, out_shape=out_shape,
        input_output_aliases=aliases,
        compiler_params=_params("parallel", "parallel"),
    )(*args)


def _mm_nt_normbwd(dy, w, h, g, dres, *, name, plan=None):
    T, D = h.shape
    S, _, n = w.shape
    tm = _tile(T, ROWS_WIDE, BF16_ROWS)
    nx, before, after = _ride_along(plan, T // tm, S)

    def body(*refs):
        dy_ref, w_ref, h_ref, g_ref, dr_ref = refs[:5]
        dh_ref, dhb_ref, dg_ref = refs[5 + nx:8 + nx]
        acc_ref = refs[8 + 2 * nx]
        ride = (refs[5:5 + nx], refs[8 + nx:8 + 2 * nx], refs[9 + 2 * nx:])
        before(ride)
        i, s = pl.program_id(0), pl.program_id(1)
        part = lax.dot_general(dy_ref[...], w_ref[...], (((1,), (1,)), ((), ())),
                               preferred_element_type=F32)

        @pl.when(s == 0)
        def _():
            acc_ref[...] = part

        @pl.when(s > 0)
        def _():
            acc_ref[...] += part

        @pl.when(s == S - 1)
        def _():
            dx, dg = _norm_bwd(h_ref[...], g_ref[...], acc_ref[...])
            dh = dr_ref[...] + dx
            dh_ref[...] = dh
            dhb_ref[...] = dh.astype(BF16)

            @pl.when(i == 0)
            def _():
                dg_ref[...] = dg

            @pl.when(i > 0)
            def _():
                dg_ref[...] += dg

        after(ride)

    extra = plan.inputs if plan else []
    return _pcall(
        body, name=name, grid=(T // tm, S),
        in_specs=[pl.BlockSpec((tm, n), lambda i, s: (i, s)),
                  pl.BlockSpec((None, D, n), lambda i, s: (s, 0, 0)),
                  pl.BlockSpec((tm, D), lambda i, s: (i, 0)),
                  pl.BlockSpec((1, D), lambda i, s: (0, 0)),
                  pl.BlockSpec((tm, D), lambda i, s: (i, 0))] + [ANY] * nx,
        out_specs=[pl.BlockSpec((tm, D), lambda i, s: (i, 0)),
                   pl.BlockSpec((tm, D), lambda i, s: (i, 0)),
                   pl.BlockSpec((1, D), lambda i, s: (0, 0))] + [ANY] * nx,
        out_shape=[_sds((T, D), F32), _sds((T, D), BF16), _sds((1, D), F32)] + (plan.out_shape if plan else []),
        scratch_shapes=[pltpu.VMEM((tm, D), F32)] + (plan.scratch if plan else []),
        compiler_params=_params("arbitrary", "arbitrary"),
    )(dy, w, h, g, dres, *extra)


def _row_chains(tm):
    first = (tm // 2 + 15) // 16 * 16
    return [slice(0, first), slice(first, tm)] if 0 < first < tm else [slice(0, tm)]


def _mlp_fwd(h, g, w_up, w_down, *, layer, name):
    T, D = h.shape
    S, _, _, n = w_up.shape
    tm = _tile(T, ROWS_NARROW, BF16_ROWS)
    tf = _tile(n, 1024, LANE)
    nj = n // tf
    nf = S * nj
    chains = _row_chains(tm)

    def body(h_ref, g_ref, wu_ref, wd_ref, o_ref, hn_ref, up_ref, acc_ref):
        f = pl.program_id(1)

        @pl.when(f == 0)
        def _():
            x = h_ref[...]
            hn_ref[...] = (x * _rstd(x) * g_ref[...]).astype(BF16)

        parts = []
        for rows in chains:
            up = jnp.dot(hn_ref[rows, :], wu_ref[...], preferred_element_type=F32)
            up_ref[rows, :] = up.astype(BF16)
            r = jnp.maximum(up, 0.0)
            parts.append(jnp.dot((r * r).astype(BF16), wd_ref[...], preferred_element_type=F32))
        part = jnp.concatenate(parts, axis=0)

        @pl.when(f == 0)
        def _():
            acc_ref[...] = part

        @pl.when(f > 0)
        def _():
            acc_ref[...] += part

        @pl.when(f == nf - 1)
        def _():
            o_ref[...] = h_ref[...] + acc_ref[...]

    return _pcall(
        body, name=name, grid=(T // tm, nf),
        in_specs=[pl.BlockSpec((tm, D), lambda i, f: (i, 0)),
                  pl.BlockSpec((1, D), lambda i, f: (0, 0)),
                  pl.BlockSpec((None, None, D, tf), lambda i, f: (f // nj, layer, 0, f % nj)),
                  pl.BlockSpec((None, None, tf, D), lambda i, f: (f // nj, layer, f % nj, 0))],
        out_specs=[pl.BlockSpec((tm, D), lambda i, f: (i, 0)),
                   pl.BlockSpec((tm, D), lambda i, f: (i, 0)),
                   pl.BlockSpec((tm, tf), lambda i, f: (i, f))],
        out_shape=[_sds((T, D), F32), _sds((T, D), BF16), _sds((T, S * n), BF16)],
        scratch_shapes=[pltpu.VMEM((tm, D), F32)],
        compiler_params=_params("parallel", "arbitrary"),
    )(h, g, w_up, w_down)


def _mlp_bwd(dy, h, g, up, w_up, w_down, *, layer, name):
    T, D = h.shape
    S, _, _, n = w_up.shape
    tm = _tile(T, ROWS_NARROW, BF16_ROWS)
    tf = _tile(n, 1024, LANE)
    nj = n // tf
    nf = S * nj
    chains = _row_chains(tm)

    def body(dy_ref, h_ref, g_ref, up_ref, wu_ref, wd_ref, dup_ref, dh_ref, dhb_ref, dg_ref,
             dyb_ref, acc_ref):
        i, f = pl.program_id(0), pl.program_id(1)

        @pl.when(f == 0)
        def _():
            dyb_ref[...] = dy_ref[...].astype(BF16)

        parts = []
        for rows in chains:
            dact = lax.dot_general(dyb_ref[rows, :], wd_ref[...], (((1,), (1,)), ((), ())),
                                   preferred_element_type=F32)
            r = jnp.maximum(up_ref[rows, :].astype(F32), 0.0)
            dup = (dact * (2.0 * r)).astype(BF16)
            dup_ref[rows, :] = dup
            parts.append(lax.dot_general(dup, wu_ref[...], (((1,), (1,)), ((), ())),
                                         preferred_element_type=F32))
        part = jnp.concatenate(parts, axis=0)

        @pl.when(f == 0)
        def _():
            acc_ref[...] = part

        @pl.when(f > 0)
        def _():
            acc_ref[...] += part

        @pl.when(f == nf - 1)
        def _():
            dx, dg = _norm_bwd(h_ref[...], g_ref[...], acc_ref[...])
            dh = dy_ref[...] + dx
            dh_ref[...] = dh
            dhb_ref[...] = dh.astype(BF16)

            @pl.when(i == 0)
            def _():
                dg_ref[...] = dg

            @pl.when(i > 0)
            def _():
                dg_ref[...] += dg

    return _pcall(
        body, name=name, grid=(T // tm, nf),
        in_specs=[pl.BlockSpec((tm, D), lambda i, f: (i, 0)),
                  pl.BlockSpec((tm, D), lambda i, f: (i, 0)),
                  pl.BlockSpec((1, D), lambda i, f: (0, 0)),
                  pl.BlockSpec((tm, tf), lambda i, f: (i, f)),
                  pl.BlockSpec((None, None, D, tf), lambda i, f: (f // nj, layer, 0, f % nj)),
                  pl.BlockSpec((None, None, tf, D), lambda i, f: (f // nj, layer, f % nj, 0))],
        out_specs=[pl.BlockSpec((tm, tf), lambda i, f: (i, f)),
                   pl.BlockSpec((tm, D), lambda i, f: (i, 0)),
                   pl.BlockSpec((tm, D), lambda i, f: (i, 0)),
                   pl.BlockSpec((1, D), lambda i, f: (0, 0))],
        out_shape=[_sds((T, S * n), BF16), _sds((T, D), F32), _sds((T, D), BF16), _sds((1, D), F32)],
        scratch_shapes=[pltpu.VMEM((tm, D), BF16), pltpu.VMEM((tm, D), F32)],
        compiler_params=_params("arbitrary", "arbitrary"),
    )(dy, h, g, up, w_up, w_down)


ATT_BLOCK = 128
ATT_HEADS = 4


def _attn_tile(T):
    for w in (3 * ATT_BLOCK, 2 * ATT_BLOCK):
        if T % w == 0:
            return w
    return ATT_BLOCK


def _tri(strict_lower, value):
    B = ATT_BLOCK
    r = lax.broadcasted_iota(jnp.int32, (2 * B, B), 0)
    r = jnp.where(r >= B, r - B, r)
    c = lax.broadcasted_iota(jnp.int32, (2 * B, B), 1)
    m = (r > c) if strict_lower else (r < c)
    return jnp.where(m, value, 0.0).astype(BF16)


def _split_dot(x, tri):
    hi = lax.bitcast_convert_type(lax.bitcast_convert_type(x, jnp.uint32) & jnp.uint32(0xFFFF0000), F32)
    lo = x - hi
    return jnp.dot(jnp.concatenate([hi.astype(BF16), lo.astype(BF16)], axis=1), tri,
                   preferred_element_type=F32)


def _causal_mask(W):
    r = lax.broadcasted_iota(jnp.int32, (W, W), 0)
    c = lax.broadcasted_iota(jnp.int32, (W, W), 1)
    return c < r


def _two_heads(x):
    left = lax.broadcasted_iota(jnp.int32, x.shape, 1) < HEAD_DIM
    zero = jnp.zeros_like(x)
    return jnp.concatenate([jnp.where(left, x, zero), jnp.where(left, zero, x)], axis=0)


def _attn_scores(z, carry, tri_neg, masked):
    W = z.shape[0]
    B = ATT_BLOCK
    minus_abs = lax.bitcast_convert_type(
        lax.bitcast_convert_type(z, jnp.uint32) | jnp.uint32(0x80000000), F32)
    sp = jnp.maximum(z, 0.0) + jnp.log(1.0 + jnp.exp(minus_abs))
    logsig = z - sp
    if masked:
        causal = _causal_mask(W)
        sp = jnp.where(causal, sp, 0.0)
    afters = []
    for b in reversed(range(W // B)):
        blk = sp[:, b * B:(b + 1) * B]
        within = _split_dot(blk, tri_neg)
        afters.append(within + carry)
        carry = carry + (within[:, 0:1] - blk[:, 0:1])
    after = jnp.concatenate(afters[::-1], axis=1)
    w = jnp.exp(logsig + after)
    if masked:
        w = jnp.where(causal, w, 0.0)
    return w, jnp.exp(logsig), carry


def _ride_along(plan, npairs, nq):
    if plan is None:
        return 0, (lambda refs: None), (lambda refs: None)
    nx = plan.n

    def before(refs):
        p, i = pl.program_id(0), pl.program_id(1)

        @pl.when(jnp.logical_and(p == 0, i == 0))
        def _():
            plan.start(*refs)

        @pl.when(jnp.logical_and(p == npairs // 2, i == 0))
        def _():
            plan.mid(*refs)

    def after(refs):
        p, i = pl.program_id(0), pl.program_id(1)

        @pl.when(jnp.logical_and(p == npairs - 1, i == nq - 1))
        def _():
            plan.finish(*refs)

    return nx, before, after


def _saved_tile(i, j):
    return i * (i + 1) // 2 + j


def _attn_fwd(qkv, *, name, plan=None):
    T = qkv.shape[0]
    D = qkv.shape[1] // 3
    W = _attn_tile(T)
    H = ATT_HEADS
    lanes = H * HEAD_DIM
    ngroups = D // lanes
    nq = T // W
    ntri = nq * (nq + 1) // 2
    scale = HEAD_DIM ** -0.5
    nx, before, after = _ride_along(plan, ngroups, nq)

    def body(*refs):
        q_ref, k_ref, v_ref = refs[:3]
        o_ref, wsv_ref, ssv_ref = refs[3 + nx:6 + nx]
        stage_w, stage_s, stage_sems = refs[6 + 2 * nx:9 + 2 * nx]
        ride = (refs[3:3 + nx], refs[6 + nx:6 + 2 * nx], refs[9 + 2 * nx:])
        before(ride)
        p, i = pl.program_id(0), pl.program_id(1)
        tri = _tri(True, -1.0)
        pairs = [slice(pp * LANE, (pp + 1) * LANE) for pp in range(H // 2)]
        qs = [q_ref[:, cols] * scale for cols in pairs]

        def save(slot, j):
            dst = _saved_tile(i, j)
            return [pltpu.make_async_copy(stage.at[slot], sv.at[pl.ds(p * H, H), dst], stage_sems.at[slot, a])
                    for a, (sv, stage) in enumerate(((wsv_ref, stage_w), (ssv_ref, stage_s)))]

        def tile(t, state, masked):
            j = i - t
            rows = pl.ds(pl.multiple_of(j * W, W), W)
            slot = t % 2
            if not masked:
                @pl.when(t >= 2)
                def _():
                    for cp in save(slot, j):
                        cp.wait()
            out = []
            for pp, (cols, q, (carries, acc)) in enumerate(zip(pairs, qs, state)):
                z = lax.dot_general(q, _two_heads(k_ref[rows, cols]), (((1,), (1,)), ((), ())),
                                    preferred_element_type=F32)
                wbs, new_carries = [], []
                for e, carry in enumerate(carries):
                    w, sig, carry = _attn_scores(z[:, e * W:(e + 1) * W], carry, tri, masked)
                    wb = w.astype(BF16)
                    stage_w[slot, 2 * pp + e] = wb
                    stage_s[slot, 2 * pp + e] = sig.astype(BF16)
                    wbs.append(wb)
                    new_carries.append(carry)
                acc = acc + jnp.dot(jnp.concatenate(wbs, axis=1), _two_heads(v_ref[rows, cols]),
                                    preferred_element_type=F32)
                out.append((tuple(new_carries), acc))
            for cp in save(slot, j):
                cp.start()
            return tuple(out)

        zero = ((jnp.zeros((W, 1), F32),) * 2, jnp.zeros((W, LANE), F32))
        state = tile(0, (zero,) * (H // 2), True)
        state = lax.fori_loop(1, i + 1, lambda t, st: tile(t, st, False), state)
        for cols, (_, acc) in zip(pairs, state):
            o_ref[:, cols] = acc.astype(BF16)
        for cp in save(i % 2, 0):
            cp.wait()

        @pl.when(i >= 1)
        def _():
            for cp in save((i + 1) % 2, 0):
                cp.wait()

        after(ride)

    extra = plan.inputs if plan else []
    saved = _sds((D // HEAD_DIM, ntri, W, W), BF16)
    return _pcall(
        body, name=name, grid=(ngroups, nq),
        in_specs=[pl.BlockSpec((W, lanes), lambda p, i: (i, p)),
                  pl.BlockSpec((T, lanes), lambda p, i: (0, ngroups + p)),
                  pl.BlockSpec((T, lanes), lambda p, i: (0, 2 * ngroups + p))] + [ANY] * nx,
        out_specs=[pl.BlockSpec((W, lanes), lambda p, i: (i, p)), ANY, ANY] + [ANY] * nx,
        out_shape=[_sds((T, D), BF16), saved, saved] + (plan.out_shape if plan else []),
        scratch_shapes=[pltpu.VMEM((2, H, W, W), BF16), pltpu.VMEM((2, H, W, W), BF16),
                        pltpu.SemaphoreType.DMA((2, 2))] + (plan.scratch if plan else []),
        compiler_params=_params("arbitrary", "arbitrary"),
    )(qkv, qkv, qkv, *extra)


def _attn_bwd(qkv, do, w_sv, s_sv, *, name, plan=None):
    T = qkv.shape[0]
    D = qkv.shape[1] // 3
    B = ATT_BLOCK
    W = _attn_tile(T)
    H = ATT_HEADS
    lanes = H * HEAD_DIM
    ngroups = D // lanes
    nq = T // W
    scale = HEAD_DIM ** -0.5
    nx, before, after = _ride_along(plan, ngroups, nq)

    def body(*refs):
        q_ref, k_ref, v_ref, do_ref, wsv_ref, ssv_ref = refs[:6]
        dq_ref, dk_ref, dv_ref = refs[6 + nx:9 + nx]
        dk_acc, dv_acc, stage_w, stage_s, stage_sems = refs[9 + 2 * nx:14 + 2 * nx]
        ride = (refs[6:6 + nx], refs[9 + nx:9 + 2 * nx], refs[14 + 2 * nx:])
        before(ride)
        p, i = pl.program_id(0), pl.program_id(1)

        @pl.when(i == 0)
        def _():
            dk_acc[...] = jnp.zeros_like(dk_acc)
            dv_acc[...] = jnp.zeros_like(dv_acc)

        def fetch(slot, j):
            src = _saved_tile(i, j)
            return [pltpu.make_async_copy(sv.at[pl.ds(p * H, H), src], stage.at[slot], stage_sems.at[slot, a])
                    for a, (sv, stage) in enumerate(((wsv_ref, stage_w), (ssv_ref, stage_s)))]

        tri_before = _tri(False, 1.0)
        pairs = [slice(pp * LANE, (pp + 1) * LANE) for pp in range(H // 2)]
        q_two = [_two_heads(q_ref[:, cols] * scale) for cols in pairs]
        do_pair = [do_ref[:, cols] for cols in pairs]
        do_two = [_two_heads(d) for d in do_pair]

        def grad(j, state, masked):
            rows = pl.ds(pl.multiple_of(j * W, W), W)
            slot = j % 2
            for cp in fetch(slot, j):
                cp.wait()
            if not masked:
                for cp in fetch(1 - slot, j + 1):
                    cp.start()
            out = []
            for pp, (cols, (gsums, dq)) in enumerate(zip(pairs, state)):
                dw = lax.dot_general(do_pair[pp], _two_heads(v_ref[rows, cols]), (((1,), (1,)), ((), ())),
                                     preferred_element_type=F32)
                dzs, wbs, new_gsums = [], [], []
                for e, gsum in enumerate(gsums):
                    wb = stage_w[slot, 2 * pp + e]
                    sig = stage_s[slot, 2 * pp + e].astype(F32)
                    g = dw[:, e * W:(e + 1) * W] * wb.astype(F32)
                    befores = []
                    for b in range(W // B):
                        blk = g[:, b * B:(b + 1) * B]
                        within = _split_dot(blk, tri_before)
                        befores.append(within + gsum)
                        gsum = gsum + (within[:, B - 1:B] + blk[:, B - 1:B])
                    dz = g - sig * (g + jnp.concatenate(befores, axis=1))
                    if masked:
                        dz = jnp.where(_causal_mask(W), dz, 0.0)
                    dzs.append(dz.astype(BF16))
                    wbs.append(wb)
                    new_gsums.append(gsum)
                dq = dq + jnp.dot(jnp.concatenate(dzs, axis=1), _two_heads(k_ref[rows, cols] * scale),
                                  preferred_element_type=F32)
                dk_acc[rows, cols] += lax.dot_general(jnp.concatenate(dzs, axis=0), q_two[pp],
                                                      (((0,), (0,)), ((), ())), preferred_element_type=F32)
                dv_acc[rows, cols] += lax.dot_general(jnp.concatenate(wbs, axis=0), do_two[pp],
                                                      (((0,), (0,)), ((), ())), preferred_element_type=F32)
                out.append((tuple(new_gsums), dq))
            return tuple(out)

        for cp in fetch(0, 0):
            cp.start()
        zero = ((jnp.zeros((W, 1), F32),) * 2, jnp.zeros((W, LANE), F32))
        state = lax.fori_loop(0, i, lambda j, st: grad(j, st, False), (zero,) * (H // 2))
        state = grad(i, state, True)
        for cols, (_, dq) in zip(pairs, state):
            dq_ref[:, cols] = dq.astype(BF16)

        @pl.when(i == nq - 1)
        def _():
            dk_ref[...] = dk_acc[...].astype(BF16)
            dv_ref[...] = dv_acc[...].astype(BF16)

        after(ride)

    extra = plan.inputs if plan else []
    return _pcall(
        body, name=name, grid=(ngroups, nq),
        in_specs=[pl.BlockSpec((W, lanes), lambda p, i: (i, p)),
                  pl.BlockSpec((T, lanes), lambda p, i: (0, ngroups + p)),
                  pl.BlockSpec((T, lanes), lambda p, i: (0, 2 * ngroups + p)),
                  pl.BlockSpec((W, lanes), lambda p, i: (i, p)), ANY, ANY] + [ANY] * nx,
        out_specs=[pl.BlockSpec((W, lanes), lambda p, i: (i, p)),
                   pl.BlockSpec((T, lanes), lambda p, i: (0, p)),
                   pl.BlockSpec((T, lanes), lambda p, i: (0, p))] + [ANY] * nx,
        out_shape=[_sds((T, D), BF16)] * 3 + (plan.out_shape if plan else []),
        scratch_shapes=[pltpu.VMEM((T, lanes), F32), pltpu.VMEM((T, lanes), F32),
                        pltpu.VMEM((2, H, W, W), BF16), pltpu.VMEM((2, H, W, W), BF16),
                        pltpu.SemaphoreType.DMA((2, 2))]
        + (plan.scratch if plan else []),
        compiler_params=_params("arbitrary", "arbitrary"),
    )(qkv, qkv, qkv, do, w_sv, s_sv, *extra)


HALO = SUBLANE


def _lru_gates(u, w_rg, b_rg, w_ig, b_ig, lam):
    nb = w_rg.shape[0]
    pre_r, pre_i = [], []
    for n in range(nb):
        ub = u[:, n * LANE:(n + 1) * LANE].astype(BF16)
        pre_r.append(jnp.dot(ub, w_rg[n], preferred_element_type=F32))
        pre_i.append(jnp.dot(ub, w_ig[n], preferred_element_type=F32))
    r = _sigmoid(jnp.concatenate(pre_r, axis=1) + b_rg)
    i = _sigmoid(jnp.concatenate(pre_i, axis=1) + b_ig)
    c = -LRU_C * _softplus_parts(-lam)[0]
    log_a = c * r
    a = jnp.exp(log_a)
    x2 = 2.0 * log_a
    em1 = jnp.where(jnp.abs(x2) < 1e-2, x2 * (1.0 + x2 * (0.5 + x2 * (1.0 / 6.0))), jnp.exp(x2) - 1.0)
    mult = jnp.sqrt(-em1)
    return r, i, a, mult, c


def _conv_rows(buf_ref, tt, conv_w, conv_b):
    u = conv_b
    for j in range(4):
        u = u + buf_ref[pl.ds(HALO - 3 + j, tt), :] * conv_w[j:j + 1, :]
    return u


def _fill_with_halo(buf_ref, prev_ref, cur_ref, first):
    tt = cur_ref.shape[0]
    buf_ref[pl.ds(0, HALO), :] = jnp.where(first, 0.0, prev_ref[...])
    buf_ref[pl.ds(HALO, tt), :] = cur_ref[...]


def _lru_time_tile(T):
    return _tile(T, 256, SUBLANE)


def _lru_pre(gr, conv_w, conv_b, w_rg, b_rg, w_ig, b_ig, lam, *, name):
    T = gr.shape[0]
    D = gr.shape[1] // 2
    tt = _lru_time_tile(T)
    hb = tt // HALO

    def body(x_ref, xp_ref, cw_ref, cb_ref, wr_ref, br_ref, wi_ref, bi_ref, lam_ref, a_ref, b_ref, buf):
        _fill_with_halo(buf, xp_ref, x_ref, pl.program_id(0) == 0)
        u = _conv_rows(buf, tt, cw_ref[...], cb_ref[...])
        _, i, a, mult, _ = _lru_gates(u, wr_ref, br_ref[...], wi_ref, bi_ref[...], lam_ref[...])
        a_ref[...] = a
        b_ref[...] = mult * (i * u)

    vec = pl.BlockSpec((1, D), lambda t: (0, 0))
    mat = pl.BlockSpec(w_rg.shape, lambda t: (0, 0, 0))
    return _pcall(
        body, name=name, grid=(T // tt,),
        in_specs=[pl.BlockSpec((tt, D), lambda t: (t, 1)),
                  pl.BlockSpec((HALO, D), lambda t: (jnp.maximum(t * hb - 1, 0), 1)),
                  pl.BlockSpec((4, D), lambda t: (0, 0)), vec, mat, vec, mat, vec, vec],
        out_specs=[pl.BlockSpec((tt, D), lambda t: (t, 0))] * 2,
        out_shape=[_sds((T, D), F32)] * 2,
        scratch_shapes=[pltpu.VMEM((tt + HALO, D), F32)],
        compiler_params=_params("parallel"),
    )(gr, gr, conv_w, conv_b, w_rg, b_rg, w_ig, b_ig, lam)


def _lru_scan(a, b, *, reverse, name):
    T, D = a.shape
    R = SUBLANE
    ts = _tile(T, 528, R)
    nt = T // ts

    def body(a_ref, b_ref, o_ref, carry):
        @pl.when(pl.program_id(0) == 0)
        def _():
            carry[...] = jnp.zeros_like(carry)

        rowid = lax.broadcasted_iota(jnp.int32, (R, D), 0)

        def chunk(k, run):
            if reverse:
                k = ts // R - 1 - k
            rows = pl.ds(pl.multiple_of(k * R, R), R)
            at, bt = a_ref[rows, :], b_ref[rows, :]
            out = jnp.zeros((R, D), F32)
            for r in (range(R - 1, -1, -1) if reverse else range(R)):
                if reverse:
                    cand = bt + run
                    nxt = at * cand
                else:
                    cand = at * run + bt
                    nxt = cand
                out = jnp.where(rowid == r, cand, out)
                run = jnp.broadcast_to(nxt[r:r + 1, :], (R, D))
            o_ref[rows, :] = out
            return run

        carry[...] = lax.fori_loop(0, ts // R, chunk, carry[...])

    if reverse:
        spec = pl.BlockSpec((ts, D), lambda t: (nt - 1 - t, 0))
    else:
        spec = pl.BlockSpec((ts, D), lambda t: (t, 0))
    return _pcall(
        body, name=name, grid=(nt,),
        in_specs=[spec, spec], out_specs=spec,
        out_shape=_sds((T, D), F32),
        scratch_shapes=[pltpu.VMEM((R, D), F32)],
        compiler_params=_params("arbitrary"),
    )(a, b)


def _lru_out(gr, hs, w, res, *, name):
    T, D = hs.shape
    tt = _tile(T, ROWS_NARROW, BF16_ROWS)

    def body(g_ref, h_ref, w_ref, r_ref, y_ref, o_ref):
        y = (h_ref[...] * _gelu_parts(g_ref[...])[0]).astype(BF16)
        y_ref[...] = y
        o_ref[...] = r_ref[...] + jnp.dot(y, w_ref[...], preferred_element_type=F32)

    blk = pl.BlockSpec((tt, D), lambda t: (t, 0))
    return _pcall(
        body, name=name, grid=(T // tt,),
        in_specs=[blk, blk, pl.BlockSpec((D, D), lambda t: (0, 0)), blk],
        out_specs=[blk, blk],
        out_shape=[_sds((T, D), BF16), _sds((T, D), F32)],
        compiler_params=_params("parallel"),
    )(gr, hs, w, res)


def _lru_out_bwd(gr, hs, dout, w, *, name):
    T, D = hs.shape
    tt = _tile(T, ROWS_NARROW, BF16_ROWS)

    def body(g_ref, h_ref, d_ref, w_ref, dg_ref, dh_ref):
        dy = lax.dot_general(d_ref[...], w_ref[...], (((1,), (1,)), ((), ())), preferred_element_type=F32)
        gelu, dgelu = _gelu_parts(g_ref[...])
        dg_ref[...] = (dy * h_ref[...] * dgelu).astype(BF16)
        dh_ref[...] = dy * gelu

    spec = pl.BlockSpec((tt, D), lambda t: (t, 0))
    return _pcall(
        body, name=name, grid=(T // tt,),
        in_specs=[spec, spec, spec, pl.BlockSpec((D, D), lambda t: (0, 0))], out_specs=[spec, spec],
        out_shape=[_sds((T, 2 * D), BF16), _sds((T, D), F32)],
        compiler_params=_params("parallel"),
    )(gr, hs, dout, w)


def _lru_gate_bwd(gr, hs, lmb, conv_w, conv_b, w_rg, b_rg, w_ig, b_ig, lam, *, name):
    T, D = hs.shape
    nb = w_rg.shape[0]
    tt = _lru_time_tile(T)
    hb = tt // HALO
    nt = T // tt

    def body(x_ref, xp_ref, h_ref, hp_ref, l_ref, cw_ref, cb_ref, wr_ref, br_ref, wi_ref, bi_ref, lam_ref,
             du_ref, dwr_ref, dbr_ref, dwi_ref, dbi_ref, dlam_ref, xbuf, hbuf):
        t = pl.program_id(0)
        first = t == 0
        _fill_with_halo(xbuf, xp_ref, x_ref, first)
        _fill_with_halo(hbuf, hp_ref, h_ref, first)
        u = _conv_rows(xbuf, tt, cw_ref[...], cb_ref[...])
        lam_v = lam_ref[...]
        r, i, a, mult, c = _lru_gates(u, wr_ref, br_ref[...], wi_ref, bi_ref[...], lam_v)
        l = l_ref[...]
        h_prev = hbuf[pl.ds(HALO - 1, tt), :]
        dlog_a = l * h_prev * a - l * (i * u) * (a * a) / mult
        d_iu = l * mult
        dpre_r = (dlog_a * c) * (r * (1.0 - r))
        dpre_i = (d_iu * u) * (i * (1.0 - i))
        dpr_b = dpre_r.astype(BF16)
        dpi_b = dpre_i.astype(BF16)
        du_parts, dwr, dwi = [], [], []
        for n in range(nb):
            cs = slice(n * LANE, (n + 1) * LANE)
            ub = u[:, cs].astype(BF16)
            du_parts.append(
                lax.dot_general(dpr_b[:, cs], wr_ref[n], (((1,), (1,)), ((), ())), preferred_element_type=F32)
                + lax.dot_general(dpi_b[:, cs], wi_ref[n], (((1,), (1,)), ((), ())), preferred_element_type=F32))
            dwr.append(lax.dot_general(ub, dpr_b[:, cs], (((0,), (0,)), ((), ())), preferred_element_type=F32))
            dwi.append(lax.dot_general(ub, dpi_b[:, cs], (((0,), (0,)), ((), ())), preferred_element_type=F32))
        du_ref[...] = d_iu * i + jnp.concatenate(du_parts, axis=1)
        dbr = jnp.sum(dpre_r, axis=0, keepdims=True)
        dbi = jnp.sum(dpre_i, axis=0, keepdims=True)
        dc = jnp.sum(dlog_a * r, axis=0, keepdims=True)

        @pl.when(first)
        def _():
            for n in range(nb):
                dwr_ref[n] = dwr[n]
                dwi_ref[n] = dwi[n]
            dbr_ref[...] = dbr
            dbi_ref[...] = dbi
            dlam_ref[...] = dc

        @pl.when(t > 0)
        def _():
            for n in range(nb):
                dwr_ref[n] += dwr[n]
                dwi_ref[n] += dwi[n]
            dbr_ref[...] += dbr
            dbi_ref[...] += dbi
            dlam_ref[...] += dc

        @pl.when(t == nt - 1)
        def _():
            dlam_ref[...] = dlam_ref[...] * (LRU_C * _sigmoid(-lam_v))

    vec = pl.BlockSpec((1, D), lambda t: (0, 0))
    mat = pl.BlockSpec(w_rg.shape, lambda t: (0, 0, 0))
    blk = pl.BlockSpec((tt, D), lambda t: (t, 0))
    prev = pl.BlockSpec((HALO, D), lambda t: (jnp.maximum(t * hb - 1, 0), 0))
    return _pcall(
        body, name=name, grid=(nt,),
        in_specs=[pl.BlockSpec((tt, D), lambda t: (t, 1)),
                  pl.BlockSpec((HALO, D), lambda t: (jnp.maximum(t * hb - 1, 0), 1)),
                  blk, prev, blk,
                  pl.BlockSpec((4, D), lambda t: (0, 0)), vec, mat, vec, mat, vec, vec],
        out_specs=[blk, mat, vec, mat, vec, vec],
        out_shape=[_sds((T, D), F32), _sds(w_rg.shape, F32), _sds((1, D), F32),
                   _sds(w_rg.shape, F32), _sds((1, D), F32), _sds((1, D), F32)],
        scratch_shapes=[pltpu.VMEM((tt + HALO, D), F32), pltpu.VMEM((tt + HALO, D), F32)],
        compiler_params=_params("arbitrary"),
    )(gr, gr, hs, hs, lmb, conv_w, conv_b, w_rg, b_rg, w_ig, b_ig, lam)


def _lru_conv_bwd(gr, du, conv_w, dgr, *, name):
    T, D = du.shape
    tt = _lru_time_tile(T)
    hb = tt // HALO
    nt = T // tt

    def body(x_ref, xp_ref, du_ref, dun_ref, cw_ref, _, dx_ref, dcw_ref, dcb_ref, xbuf, dbuf):
        t = pl.program_id(0)
        _fill_with_halo(xbuf, xp_ref, x_ref, t == 0)
        du = du_ref[...]
        dbuf[pl.ds(0, tt), :] = du
        dbuf[pl.ds(tt, HALO), :] = jnp.where(t == nt - 1, 0.0, dun_ref[...])
        cw = cw_ref[...]
        dx = jnp.zeros((tt, D), F32)
        dcw = []
        for j in range(4):
            dx = dx + dbuf[pl.ds(3 - j, tt), :] * cw[j:j + 1, :]
            dcw.append(jnp.sum(du * xbuf[pl.ds(HALO - 3 + j, tt), :], axis=0, keepdims=True))
        dx_ref[...] = dx.astype(BF16)
        dcw = jnp.concatenate(dcw, axis=0)
        dcb = jnp.sum(du, axis=0, keepdims=True)

        @pl.when(t == 0)
        def _():
            dcw_ref[...] = dcw
            dcb_ref[...] = dcb

        @pl.when(t > 0)
        def _():
            dcw_ref[...] += dcw
            dcb_ref[...] += dcb

    blk = pl.BlockSpec((tt, D), lambda t: (t, 0))
    return _pcall(
        body, name=name, grid=(nt,),
        in_specs=[pl.BlockSpec((tt, D), lambda t: (t, 1)),
                  pl.BlockSpec((HALO, D), lambda t: (jnp.maximum(t * hb - 1, 0), 1)),
                  blk,
                  pl.BlockSpec((HALO, D), lambda t: (jnp.minimum((t + 1) * hb, T // HALO - 1), 0)),
                  pl.BlockSpec((4, D), lambda t: (0, 0)), ANY],
        out_specs=[pl.BlockSpec((tt, D), lambda t: (t, 1)),
                   pl.BlockSpec((4, D), lambda t: (0, 0)), pl.BlockSpec((1, D), lambda t: (0, 0))],
        out_shape=[_sds((T, 2 * D), BF16), _sds((4, D), F32), _sds((1, D), F32)],
        input_output_aliases={5: 0},
        scratch_shapes=[pltpu.VMEM((tt + HALO, D), F32), pltpu.VMEM((tt + HALO, D), F32)],
        compiler_params=_params("arbitrary"),
    )(gr, gr, du, du, conv_w, dgr)


def _loss_head(h, g, target, *, row_lo, row_hi, name):
    T, D = h.shape
    tm = _tile(T, ROWS_WIDE, BF16_ROWS)

    def body(h_ref, g_ref, t_ref, loss_ref, dh_ref, dhb_ref, dg_ref):
        i = pl.program_id(0)
        x = h_ref[...]
        g = g_ref[...]
        row = i * tm + lax.broadcasted_iota(jnp.int32, (tm, 1), 0)
        valid = jnp.logical_and(row >= row_lo, row < row_hi)
        rstd = _rstd(x)
        n = x * rstd
        err = jnp.where(valid, n * g - t_ref[...], 0.0)
        part = (0.5 / D) * jnp.sum(jnp.sum(err * err, axis=1, keepdims=True), axis=0, keepdims=True)
        dy = err * (1.0 / D)
        dn = dy * g
        dh = rstd * (dn - n * jnp.mean(dn * n, axis=-1, keepdims=True))
        dh_ref[...] = dh
        dhb_ref[...] = dh.astype(BF16)
        dg = jnp.sum(dy * n, axis=0, keepdims=True)

        @pl.when(i == 0)
        def _():
            loss_ref[...] = part
            dg_ref[...] = dg

        @pl.when(i > 0)
        def _():
            loss_ref[...] += part
            dg_ref[...] += dg

    blk = pl.BlockSpec((tm, D), lambda i: (i, 0))
    vec = pl.BlockSpec((1, D), lambda i: (0, 0))
    return _pcall(
        body, name=name, grid=(T // tm,),
        in_specs=[blk, vec, blk],
        out_specs=[pl.BlockSpec((1, 1), lambda i: (0, 0)), blk, blk, vec],
        out_shape=[_sds((1, 1), F32), _sds((T, D), F32), _sds((T, D), BF16), _sds((1, D), F32)],
        compiler_params=_params("arbitrary"),
    )(h, g, target)


def _adamw_math(w, g, m, v):
    c1 = 1.0 / (1.0 - ADAM_B1 ** ADAM_STEP)
    c2 = 1.0 / (1.0 - ADAM_B2 ** ADAM_STEP)
    m = ADAM_B1 * m + (1.0 - ADAM_B1) * g
    v = ADAM_B2 * v + (1.0 - ADAM_B2) * (g * g)
    delta = -ADAM_LR * ((m * c1) / (jnp.sqrt(v * c2) + ADAM_EPS) + ADAM_WD * w)
    return delta, m, v


def _adamw_halves(w, mine, theirs, m, v, c, *, name):
    _, R, C = w.shape
    tr = _tile(R, 256, SUBLANE)

    def body(c_ref, w_ref, a_ref, b_ref, m_ref, v_ref, g_ref, d_ref, nm_ref, nv_ref):
        g = jnp.where(pl.program_id(0) == c_ref[0], a_ref[...], b_ref[...])
        g_ref[...] = g
        d_ref[...], nm_ref[...], nv_ref[...] = _adamw_math(w_ref[...], g, m_ref[...], v_ref[...])

    full = pl.BlockSpec((None, tr, C), lambda h, i, c_ref: (h, i, 0))
    half = pl.BlockSpec((tr, C), lambda h, i, c_ref: (i, 0))
    return _pcall(
        body, name=name,
        grid_spec=pltpu.PrefetchScalarGridSpec(
            num_scalar_prefetch=1, grid=(2, R // tr),
            in_specs=[full, half, half, full, full], out_specs=[full] * 4),
        out_shape=[_sds((2, R, C), F32)] * 4,
        compiler_params=_params("parallel", "parallel"),
    )(c, w, mine, theirs, m, v)


def _adamw(w, g, m, v, *, name):
    R, C = w.shape
    tr = _tile(R, 512, SUBLANE)

    def body(w_ref, g_ref, m_ref, v_ref, d_ref, nm_ref, nv_ref):
        d_ref[...], nm_ref[...], nv_ref[...] = _adamw_math(w_ref[...], g_ref[...], m_ref[...], v_ref[...])

    blk = pl.BlockSpec((tr, C), lambda i: (i, 0))
    return _pcall(
        body, name=name, grid=(R // tr,),
        in_specs=[blk] * 4, out_specs=[blk] * 3,
        out_shape=[_sds((R, C), F32)] * 3,
        compiler_params=_params("parallel"),
    )(w, g, m, v)


ANY = pl.BlockSpec(memory_space=pl.ANY)


def _place():
    x, y, c = lax.axis_index("x"), lax.axis_index("y"), lax.axis_index("c")
    chips = [(1 - x, y), (x, 1 - y), (1 - x, 1 - y)]
    return x, y, c, chips


LOCAL_PIECES = 4


class _GatherChips:
    def __init__(self, vs):
        self.inputs = list(vs)
        n = self.n = len(vs)
        self.out_shape = [_sds((4,) + v.shape, v.dtype) for v in vs]
        self.scratch = [pltpu.SemaphoreType.DMA((6 * n,)), pltpu.SemaphoreType.DMA((6 * n,)),
                        pltpu.SemaphoreType.DMA((LOCAL_PIECES * n,))]

    def _copies(self, v_refs, o_refs, sems):
        send_sems, recv_sems, local_sems = sems
        x, y, c, chips = _place()
        me = 2 * x + y

        def copy(a, k, block, half, to, src=None):
            dst = o_refs[a].at[block, half]
            return pltpu.make_async_remote_copy(
                src_ref=dst if src is None else src, dst_ref=dst,
                send_sem=send_sems.at[6 * a + k], recv_sem=recv_sems.at[6 * a + k],
                device_id=to, device_id_type=MESH)

        ks = [(a, k, cx, cy) for a in range(self.n) for k, (cx, cy) in enumerate(chips)]

        def local():
            out = []
            for a in range(self.n):
                rows = self.inputs[a].shape[1] // (LOCAL_PIECES // 2)
                for p in range(LOCAL_PIECES):
                    h, r0 = p % 2, (p // 2) * rows
                    out.append(pltpu.make_async_copy(
                        v_refs[a].at[h, pl.ds(r0, rows)], o_refs[a].at[me, h, pl.ds(r0, rows)],
                        local_sems.at[LOCAL_PIECES * a + p]))
            return out

        return dict(
            first=lambda: [copy(a, k, me, c, (cx, cy, c), src=v_refs[a].at[c]) for a, k, cx, cy in ks],
            landed=lambda: [copy(a, k, 2 * cx + cy, c, (x, y, c)) for a, k, cx, cy in ks],
            passed=lambda: [copy(a, 3 + k, 2 * cx + cy, c, (x, y, 1 - c)) for a, k, cx, cy in ks],
            final=lambda: [copy(a, 3 + k, 2 * cx + cy, 1 - c, (x, y, c)) for a, k, cx, cy in ks],
            local=local)

    def start(self, v_refs, o_refs, sems):
        cps = self._copies(v_refs, o_refs, sems)
        for cp in cps["first"]() + cps["local"]():
            cp.start()

    def mid(self, v_refs, o_refs, sems):
        cps = self._copies(v_refs, o_refs, sems)
        for got, fwd in zip(cps["landed"](), cps["passed"]()):
            got.wait_recv()
            fwd.start()

    def finish(self, v_refs, o_refs, sems):
        cps = self._copies(v_refs, o_refs, sems)
        for cp in cps["final"]():
            cp.wait_recv()
        for cp in cps["first"]() + cps["passed"]():
            cp.wait_send()
        for cp in cps["local"]():
            cp.wait()


class _ExchangeBlocks:
    def __init__(self, ps):
        self.inputs = list(ps)
        n = self.n = len(ps)
        self.out_shape = [_sds((8,) + p.shape[2:], p.dtype) for p in ps]
        self.scratch = [pltpu.SemaphoreType.DMA((7 * n,)), pltpu.SemaphoreType.DMA((7 * n,))]

    def _copies(self, p_refs, o_refs, sems, incoming):
        send_sems, recv_sems = sems
        x, y, c, _ = _place()
        me = 4 * x + 2 * y + c
        out = []
        for a in range(self.n):
            for k in range(1, 8):
                px, py, pc = x ^ (k >> 2), y ^ ((k >> 1) & 1), c ^ (k & 1)
                out.append(pltpu.make_async_remote_copy(
                    src_ref=p_refs[a].at[2 * px + py, pc],
                    dst_ref=o_refs[a].at[4 * px + 2 * py + pc if incoming else me],
                    send_sem=send_sems.at[7 * a + k - 1], recv_sem=recv_sems.at[7 * a + k - 1],
                    device_id=(x, y, c) if incoming else (px, py, pc), device_id_type=MESH))
        return out

    def start(self, p_refs, o_refs, sems):
        for cp in self._copies(p_refs, o_refs, sems, False):
            cp.start()

    def mid(self, p_refs, o_refs, sems):
        pass

    def finish(self, p_refs, o_refs, sems):
        for cp in self._copies(p_refs, o_refs, sems, True):
            cp.wait_recv()
        for cp in self._copies(p_refs, o_refs, sems, False):
            cp.wait_send()


def _run_exchange(plan, *, name):
    n = plan.n

    def body(*refs):
        args = (refs[:n], refs[n:2 * n], refs[2 * n:])
        plan.start(*args)
        plan.mid(*args)
        plan.finish(*args)

    return _pcall(
        body, name=name, in_specs=[ANY] * n, out_specs=[ANY] * n,
        out_shape=plan.out_shape, scratch_shapes=plan.scratch,
    )(*plan.inputs)


def _send_sibling(rs, *, name):
    n = len(rs)

    def body(*refs):
        r_refs, o_refs = refs[:n], refs[n:2 * n]
        send_sems, recv_sems = refs[2 * n:]
        x, y, c, _ = _place()
        cps = [pltpu.make_async_remote_copy(
            src_ref=r_refs[a], dst_ref=o_refs[a], send_sem=send_sems.at[a], recv_sem=recv_sems.at[a],
            device_id=(x, y, 1 - c), device_id_type=MESH) for a in range(n)]
        for cp in cps:
            cp.start()
        for cp in cps:
            cp.wait()

    return _pcall(
        body, name=name, in_specs=[ANY] * n, out_specs=[ANY] * n,
        out_shape=[_sds(r.shape, r.dtype) for r in rs],
        scratch_shapes=[pltpu.SemaphoreType.DMA((n,)), pltpu.SemaphoreType.DMA((n,))],
    )(*rs)


def _add_devices(p, got, place, *, name):
    _, _, R, C = p.shape
    tr = _tile(R, 256, 16)

    def body(place_ref, p_ref, o_ref, out_ref):
        me = place_ref[2]
        own = p_ref[...].astype(F32)
        acc = jnp.where(me == 0, own, o_ref[0].astype(F32))
        for d in range(1, 8):
            acc = acc + jnp.where(me == d, own, o_ref[d].astype(F32))
        out_ref[...] = acc

    return _pcall(
        body, name=name,
        grid_spec=pltpu.PrefetchScalarGridSpec(
            num_scalar_prefetch=1, grid=(R // tr,),
            in_specs=[pl.BlockSpec((None, None, tr, C), lambda i, pr: (pr[0], pr[1], i, 0)),
                      pl.BlockSpec((8, tr, C), lambda i, pr: (0, i, 0))],
            out_specs=pl.BlockSpec((tr, C), lambda i, pr: (i, 0))),
        out_shape=_sds((R, C), F32),
        compiler_params=_params("parallel"),
    )(place, p, got)


def _round_up(n, m):
    return (n + m - 1) // m * m


def _f32_as_bf16(a):
    return lax.bitcast_convert_type(a.astype(F32), BF16).reshape(-1)


def _bf16_as_f32(a):
    return lax.bitcast_convert_type(a.reshape(-1, 2), F32)


def _by_chip_cols(a, cols):
    lead = a.shape[:-1]
    a = a.reshape(lead + (4, cols))
    return jnp.moveaxis(a, -2, 0).reshape(4, -1)


def kernel(x, meta_tokens, norm_mix, norm_mlp, sb_w_qkv, sb_w_o, lru_w_in, lru_conv_w, lru_conv_b, lru_w_rg, lru_b_rg, lru_w_ig, lru_b_ig, lru_lambda, lru_w_out, mlp_w_up, mlp_w_down, norm_final, loss_target, m_meta_tokens, m_norm_mix, m_norm_mlp, m_sb_w_qkv, m_sb_w_o, m_lru_w_in, m_lru_conv_w, m_lru_conv_b, m_lru_w_rg, m_lru_b_rg, m_lru_w_ig, m_lru_b_ig, m_lru_lambda, m_lru_w_out, m_mlp_w_up, m_mlp_w_down, m_norm_final, v_meta_tokens, v_norm_mix, v_norm_mlp, v_sb_w_qkv, v_sb_w_o, v_lru_w_in, v_lru_conv_w, v_lru_conv_b, v_lru_w_rg, v_lru_b_rg, v_lru_w_ig, v_lru_b_ig, v_lru_lambda, v_lru_w_out, v_mlp_w_up, v_mlp_w_down, v_norm_final):
    weights = dict(meta_tokens=meta_tokens, norm_mix=norm_mix, norm_mlp=norm_mlp, sb_w_qkv=sb_w_qkv,
                   sb_w_o=sb_w_o, lru_w_in=lru_w_in, lru_conv_w=lru_conv_w, lru_conv_b=lru_conv_b,
                   lru_w_rg=lru_w_rg, lru_b_rg=lru_b_rg, lru_w_ig=lru_w_ig, lru_b_ig=lru_b_ig,
                   lru_lambda=lru_lambda, lru_w_out=lru_w_out, mlp_w_up=mlp_w_up, mlp_w_down=mlp_w_down,
                   norm_final=norm_final)
    m_in = dict(meta_tokens=m_meta_tokens, norm_mix=m_norm_mix, norm_mlp=m_norm_mlp, sb_w_qkv=m_sb_w_qkv,
                sb_w_o=m_sb_w_o, lru_w_in=m_lru_w_in, lru_conv_w=m_lru_conv_w, lru_conv_b=m_lru_conv_b,
                lru_w_rg=m_lru_w_rg, lru_b_rg=m_lru_b_rg, lru_w_ig=m_lru_w_ig, lru_b_ig=m_lru_b_ig,
                lru_lambda=m_lru_lambda, lru_w_out=m_lru_w_out, mlp_w_up=m_mlp_w_up,
                mlp_w_down=m_mlp_w_down, norm_final=m_norm_final)
    v_in = dict(meta_tokens=v_meta_tokens, norm_mix=v_norm_mix, norm_mlp=v_norm_mlp, sb_w_qkv=v_sb_w_qkv,
                sb_w_o=v_sb_w_o, lru_w_in=v_lru_w_in, lru_conv_w=v_lru_conv_w, lru_conv_b=v_lru_conv_b,
                lru_w_rg=v_lru_w_rg, lru_b_rg=v_lru_b_rg, lru_w_ig=v_lru_w_ig, lru_b_ig=v_lru_b_ig,
                lru_lambda=v_lru_lambda, lru_w_out=v_lru_w_out, mlp_w_up=v_mlp_w_up,
                mlp_w_down=v_mlp_w_down, norm_final=v_norm_final)
    names = list(weights)

    seq, D = x.shape[1], x.shape[2]
    n_meta = meta_tokens.shape[0]
    Dq = D // 4
    T = _round_up(n_meta + seq, ATT_BLOCK)
    depth = mlp_w_up.shape[0]
    my_x, my_y, my_c = lax.axis_index("x"), lax.axis_index("y"), lax.axis_index("c")
    c_arr = jnp.reshape(my_c, (1,)).astype(jnp.int32)

    assert depth == 2

    def halves(a):
        return a.astype(BF16).reshape(2, a.shape[0] // 2, a.shape[1])

    small = [meta_tokens, lru_conv_w[0], lru_conv_b, lru_b_rg, lru_b_ig, lru_lambda]
    sparts = [_f32_as_bf16(s) for s in small]
    sizes = [p.shape[0] for p in sparts]
    total = _round_up(sum(sizes), 2 * 32 * LANE)
    sflat =jnp.concatenate(sparts + [jnp.zeros((total - sum(sizes),), BF16)]).reshape(2, -1, LANE)
    gq, gsm = _run_exchange(_GatherChips([halves(sb_w_qkv[0]), sflat]), name="gather_first")
    gather_rest = _GatherChips([halves(sb_w_o[0]), halves(lru_w_in[0]), halves(lru_w_out[0]),
                                mlp_w_up.astype(BF16), mlp_w_down.astype(BF16)])
    w_qkv = gq.reshape(4, D, 3 * Dq)
    gsm = gsm.reshape(4, total)
    offs = [sum(sizes[:k]) for k in range(len(sizes))]
    sm = [_bf16_as_f32(gsm[:, o:o + s]) for o, s in zip(offs, sizes)]
    meta_full = jnp.moveaxis(sm[0].reshape(4, n_meta, Dq), 0, 1).reshape(n_meta, D)
    conv_w = jnp.moveaxis(sm[1].reshape(4, 4, Dq), 0, 1).reshape(4, D)
    conv_b, b_rg, b_ig, lam = [s.reshape(1, D) for s in sm[2:6]]
    w_rg = lru_w_rg[0].astype(BF16)
    w_ig = lru_w_ig[0].astype(BF16)
    g_mix = [norm_mix[l].reshape(1, D) for l in range(depth)]
    g_mlp = [norm_mlp[l].reshape(1, D) for l in range(depth)]
    g_fin = norm_final.reshape(1, D)

    pad_rows = T - n_meta - seq
    h0 = jnp.concatenate([meta_full, x[0], jnp.zeros((pad_rows, D), F32)], axis=0)
    target = jnp.concatenate([jnp.zeros((n_meta, D), F32), loss_target[0], jnp.zeros((pad_rows, D), F32)], axis=0)

    hn0, qkv = _norm_mm(h0, g_mix[0], w_qkv, out_dtype=BF16, name="qkv_proj")
    att, w_sv, s_sv, go, gi, gout, w_up, w_down = _attn_fwd(qkv, name="attn_fwd", plan=gather_rest)
    w_o = go.reshape(D, D)
    w_in = gi.reshape(4, D, 2 * Dq)
    w_out = gout.reshape(D, D)
    h1 = _mm_res(att, w_o, h0, name="attn_out")
    h2, hnm0, up0 = _mlp_fwd(h1, g_mlp[0], w_up, w_down, layer=0, name="mlp0_fwd")
    hn1, gr = _norm_mm(h2, g_mix[1], w_in, out_dtype=F32, name="lru_in")
    a_t, b_t = _lru_pre(gr, conv_w, conv_b, w_rg, b_rg, w_ig, b_ig, lam, name="lru_pre")
    hs = _lru_scan(a_t, b_t, reverse=False, name="lru_scan")
    y, h3 = _lru_out(gr, hs, w_out, h2, name="lru_out")
    h4, hnm1, up1 = _mlp_fwd(h3, g_mlp[1], w_up, w_down, layer=1, name="mlp1_fwd")
    loss, dh4, dh4b, dg_fin = _loss_head(h4, g_fin, target, row_lo=n_meta, row_hi=n_meta + seq, name="loss_head")

    dup1, dh3, dh3b, dg_mlp1 = _mlp_bwd(dh4, h3, g_mlp[1], up1, w_up, w_down, layer=1, name="mlp1_bwd")
    dw_up = _mm_tn(hnm1, dup1, shards=4, relu2=False, slot=1, name="mlp1_dwup")
    dw_down = _mm_tn(up1, dh4b, shards=1, relu2=True, slot=1, row_shards=4, name="mlp1_dwdown")
    dw_out = _mm_tn(y, dh3b, shards=1, relu2=False, name="lru_dwout")
    dgr, dhy = _lru_out_bwd(gr, hs, dh3b, w_out, name="lru_out_bwd")
    lmb = _lru_scan(a_t, dhy, reverse=True, name="lru_scan_bwd")
    du, dw_rg, db_rg, dw_ig, db_ig, dlam = _lru_gate_bwd(
        gr, hs, lmb, conv_w, conv_b, w_rg, b_rg, w_ig, b_ig, lam, name="lru_gate_bwd")
    dgr, dconv_w, dconv_b = _lru_conv_bwd(gr, du, conv_w, dgr, name="lru_conv_bwd")
    dh2, dh2b, dg_mix1 = _mm_nt_normbwd(dgr, w_in, h2, g_mix[1], dh3, name="lru_in_bwd")
    dw_in = _mm_tn(hn1, dgr, shards=4, relu2=False, name="lru_dwin")
    dup0, dh1, dh1b, dg_mlp0 = _mlp_bwd(dh2, h1, g_mlp[0], up0, w_up, w_down, layer=0, name="mlp0_bwd")
    dw_up = _mm_tn(hnm0, dup0, shards=4, relu2=False, slot=0, into=dw_up, name="mlp0_dwup")
    dw_down = _mm_tn(up0, dh2b, shards=1, relu2=True, slot=0, into=dw_down, row_shards=4, name="mlp0_dwdown")
    datt = _mm_nt(dh1b, w_o, out_dtype=BF16, name="attn_out_bwd")
    dw_o = _mm_tn(att, dh1b, shards=1, relu2=False, name="attn_dwo")
    def halves_of(d, rows):
        return d.reshape(4, 2, rows // 2, d.shape[-1])

    early = [halves_of(dw_o, Dq), halves_of(dw_in, D), halves_of(dw_out, Dq), dw_up, dw_down]
    dq, dk, dv, *got_early = _attn_bwd(qkv, datt, w_sv, s_sv, name="attn_bwd", plan=_ExchangeBlocks(early))
    dqkv = jnp.concatenate([dq, dk, dv], axis=1)
    dw_qkv = halves_of(_mm_tn(hn0, dqkv, shards=4, relu2=False, name="attn_dwqkv"), D)
    dh0, _, dg_mix0, got_qkv = _mm_nt_normbwd(dqkv, w_qkv, h0, g_mix[0], dh1, name="qkv_bwd",
                                               plan=_ExchangeBlocks([dw_qkv]))
    grad_x = dh0[n_meta:n_meta + seq][None]
    dmeta = dh0[:n_meta]

    large = ["sb_w_o", "lru_w_in", "lru_w_out", "mlp_w_up", "mlp_w_down", "sb_w_qkv"]
    sharded =[_by_chip_cols(dmeta, Dq), _by_chip_cols(dconv_w, Dq), dconv_b.reshape(4, Dq),
               db_rg.reshape(4, Dq), db_ig.reshape(4, Dq), dlam.reshape(4, Dq)]
    repl = [jnp.concatenate([dg_mix0, dg_mix1], axis=0).reshape(-1),
            jnp.concatenate([dg_mlp0, dg_mlp1], axis=0).reshape(-1),
            dg_fin.reshape(-1), dw_rg.reshape(-1), dw_ig.reshape(-1), loss.reshape(-1)]
    rsizes = [r.shape[0] for r in repl]
    rtotal = _round_up(sum(rsizes), 4 * 2 * 16 * LANE)
    rflat = jnp.concatenate(repl + [jnp.zeros((rtotal - sum(rsizes),), F32)]).reshape(4, rtotal // 4)
    gsizes = [s.shape[1] for s in sharded] + [rtotal // 4]
    gtotal = _round_up(sum(gsizes), 2 * 16 * LANE)
    tail = jnp.concatenate(sharded + [rflat, jnp.zeros((4, gtotal - sum(gsizes)), F32)], axis=1)
    late = [tail.reshape(4, 2, -1, LANE)]
    got_late = _run_exchange(_ExchangeBlocks(late), name="reduce_late")
    place = jnp.stack([2 * my_x + my_y, my_c, 4 * my_x + 2 * my_y + my_c]).astype(jnp.int32)
    mine = [_add_devices(p, o, place, name="reduce_add_" + t)
            for p, o, t in zip(early + [dw_qkv] + late, list(got_early) + [got_qkv] + list(got_late),
                               large + ["tail"])]
    theirs = _send_sibling(mine, name="reduce_join")

    grads, delta, new_m, new_v = {}, {}, {}, {}
    for n, a, b in zip(large, mine, theirs):
        shp = weights[n].shape
        view = (2,) + a.shape
        g, d, nm, nv = _adamw_halves(weights[n].reshape(view), a, b, m_in[n].reshape(view),
                                     v_in[n].reshape(view), c_arr, name="adamw_" + n)
        grads[n], delta[n], new_m[n], new_v[n] = g.reshape(shp), d.reshape(shp), nm.reshape(shp), nv.reshape(shp)

    lo = jnp.where(my_c == 0, mine[-1], theirs[-1])
    hi = jnp.where(my_c == 0, theirs[-1], mine[-1])
    gshard = jnp.concatenate([lo, hi], axis=0).reshape(gtotal)
    goffs = [sum(gsizes[:k]) for k in range(len(gsizes))]
    gp = [gshard[o:o + s] for o, s in zip(goffs, gsizes)]
    rfull = _run_exchange(_GatherChips([gp[-1].reshape(2, -1, LANE)]), name="gather_replicated")[0].reshape(rtotal)
    roffs = [sum(rsizes[:k]) for k in range(len(rsizes))]
    rp = [rfull[o:o + s] for o, s in zip(roffs, rsizes)]
    grads.update(meta_tokens=gp[0], lru_conv_w=gp[1], lru_conv_b=gp[2], lru_b_rg=gp[3], lru_b_ig=gp[4],
                 lru_lambda=gp[5], norm_mix=rp[0], norm_mlp=rp[1], norm_final=rp[2], lru_w_rg=rp[3],
                 lru_w_ig=rp[4])
    grads = {n: grads[n].reshape(weights[n].shape) for n in names}
    rest = [n for n in names if n not in large]
    ssz = [weights[n].size for n in rest]
    small_cols = 8 * LANE
    stotal = _round_up(sum(ssz), SUBLANE * small_cols)

    def pack(src):
        return jnp.concatenate([src[n].reshape(-1) for n in rest]
                               + [jnp.ones((stotal - sum(ssz),), F32)]).reshape(-1, small_cols)

    d, nm, nv = _adamw(pack(weights), pack(grads), pack(m_in), pack(v_in), name="adamw_small")
    soffs = [sum(ssz[:k]) for k in range(len(ssz))]
    for n, o, s in zip(rest, soffs, ssz):
        shp = weights[n].shape
        delta[n] = d.reshape(-1)[o:o + s].reshape(shp)
        new_m[n] = nm.reshape(-1)[o:o + s].reshape(shp)
        new_v[n] = nv.reshape(-1)[o:o + s].reshape(shp)

    loss = rp[5][0]
    return (loss, grad_x, *[grads[n] for n in names], *[delta[n] for n in names],
            *[new_m[n] for n in names], *[new_v[n] for n in names])
```

```python
import jax
import jax.numpy as jnp
from jax import lax
from jax.experimental import pallas as pl
from jax.experimental.pallas import tpu as pltpu

F32 = jnp.float32
BF16 = jnp.bfloat16
MESH = pl.DeviceIdType.MESH

EPS = 1e-6
HEAD_DIM = 64
LANE = 128
SUBLANE = 8
LRU_C = 8.0
BF16_ROWS = 16
VMEM_LIMIT = 56 * 1024 * 1024
ROWS_WIDE = 1056
ROWS_NARROW = 528

ADAM_LR = 0.001
ADAM_B1 = 0.9
ADAM_B2 = 0.999
ADAM_EPS = 1e-08
ADAM_WD = 0.01
ADAM_STEP = 10


def _pcall(body, **kw):
    return pl.pallas_call(body, **kw)


def _params(*sem):
    return pltpu.CompilerParams(dimension_semantics=sem, vmem_limit_bytes=VMEM_LIMIT)


def _tile(n, pref, align):
    best = None
    for t in range(align, min(n, pref) + 1, align):
        if n % t == 0:
            best = t
    return n if best is None else best


def _sds(shape, dtype):
    return jax.ShapeDtypeStruct(shape, dtype)


def _rstd(x):
    return lax.rsqrt(jnp.mean(x * x, axis=-1, keepdims=True) + EPS)


def _norm_bwd(x, g, dy):
    rstd = _rstd(x)
    n = x * rstd
    dn = dy * g
    dx = rstd * (dn - n * jnp.mean(dn * n, axis=-1, keepdims=True))
    dg = jnp.sum(dy * n, axis=0, keepdims=True)
    return dx, dg


def _softplus_parts(z):
    l1p = jnp.log(1.0 + jnp.exp(-jnp.abs(z)))
    return jnp.maximum(z, 0.0) + l1p, jnp.minimum(z, 0.0) - l1p


def _sigmoid(x):
    return 1.0 / (1.0 + jnp.exp(-x))


def _gelu_parts(x):
    k = 0.7978845608028654
    inner = k * (x + 0.044715 * (x * x * x))
    t = jnp.tanh(inner)
    gelu = 0.5 * x * (1.0 + t)
    dgelu = 0.5 * (1.0 + t) + 0.5 * x * (1.0 - t * t) * (k * (1.0 + 3.0 * 0.044715 * (x * x)))
    return gelu, dgelu


def _norm_mm(h, g, w, *, out_dtype, name):
    T, D = h.shape
    S, _, n = w.shape
    tm = _tile(T, ROWS_WIDE, BF16_ROWS)
    tn = _tile(n, 768, LANE)
    nj = n // tn

    def body(h_ref, g_ref, w_ref, hn_ref, o_ref):
        @pl.when(pl.program_id(1) == 0)
        def _():
            x = h_ref[...]
            hn_ref[...] = (x * _rstd(x) * g_ref[...]).astype(BF16)

        o_ref[...] = jnp.dot(hn_ref[...], w_ref[...], preferred_element_type=F32).astype(out_dtype)

    return _pcall(
        body, name=name, grid=(T // tm, S * nj),
        in_specs=[pl.BlockSpec((tm, D), lambda i, j: (i, 0)),
                  pl.BlockSpec((1, D), lambda i, j: (0, 0)),
                  pl.BlockSpec((None, D, tn), lambda i, j: (j // nj, 0, j % nj))],
        out_specs=[pl.BlockSpec((tm, D), lambda i, j: (i, 0)),
                   pl.BlockSpec((tm, tn), lambda i, j: (i, j))],
        out_shape=[_sds((T, D), BF16), _sds((T, S * n), out_dtype)],
        compiler_params=_params("parallel", "arbitrary"),
    )(h, g, w)


def _mm_res(a, w, res, *, name):
    T, K = a.shape
    N = w.shape[1]
    tm = _tile(T, ROWS_WIDE, BF16_ROWS)

    def body(a_ref, w_ref, r_ref, o_ref):
        o_ref[...] = r_ref[...] + jnp.dot(a_ref[...], w_ref[...], preferred_element_type=F32)

    return _pcall(
        body, name=name, grid=(T // tm,),
        in_specs=[pl.BlockSpec((tm, K), lambda i: (i, 0)),
                  pl.BlockSpec((K, N), lambda i: (0, 0)),
                  pl.BlockSpec((tm, N), lambda i: (i, 0))],
        out_specs=pl.BlockSpec((tm, N), lambda i: (i, 0)),
        out_shape=_sds((T, N), F32),
        compiler_params=_params("parallel"),
    )(a, w, res)


def _mm_nt(a, w, *, out_dtype, name):
    T, N = a.shape
    K = w.shape[0]
    tm = _tile(T, ROWS_WIDE, BF16_ROWS)

    def body(a_ref, w_ref, o_ref):
        o_ref[...] = lax.dot_general(a_ref[...], w_ref[...], (((1,), (1,)), ((), ())),
                                     preferred_element_type=F32).astype(out_dtype)

    return _pcall(
        body, name=name, grid=(T // tm,),
        in_specs=[pl.BlockSpec((tm, N), lambda i: (i, 0)),
                  pl.BlockSpec((K, N), lambda i: (0, 0))],
        out_specs=pl.BlockSpec((tm, K), lambda i: (i, 0)),
        out_shape=_sds((T, K), out_dtype),
        compiler_params=_params("parallel"),
    )(a, w)


def _mm_tn(a, b, *, shards, relu2, name, slot=None, into=None, row_shards=0, out_dtype=BF16, plan=None):
    T, Ka = a.shape
    Nb = b.shape[1]
    n = Nb // shards
    tka = _tile(Ka, 512, LANE)
    tnb = _tile(n, 512, LANE)
    nj = n // tnb

    nx, before, after = _ride_along(plan, Ka // tka, shards * nj)
    n_in = 2 + (into is not None)

    def body(*refs):
        a_ref, b_ref = refs[:2]
        o_ref = refs[n_in + nx]
        ride = (refs[n_in:n_in + nx], refs[n_in + nx + 1:n_in + 2 * nx + 1], refs[n_in + 2 * nx + 1:])
        before(ride)
        av = a_ref[...]
        if relu2:
            r = jnp.maximum(av, 0)
            av = r * r
        o_ref[...] = lax.dot_general(av, b_ref[...], (((0,), (0,)), ((), ())),
                                     preferred_element_type=F32).astype(out_dtype)
        after(ride)

    in_specs = [pl.BlockSpec((T, tka), lambda i, j: (0, i)),
                pl.BlockSpec((T, tnb), lambda i, j: (0, j))]
    args = [a, b]
    aliases = {}
    if slot is None:
        out_spec = pl.BlockSpec((None, tka, tnb), lambda i, j: (j // nj, i, j % nj))
        out_shape = _sds((shards, Ka, n), out_dtype)
    else:
        if row_shards:
            ni = Ka // row_shards // tka
            out_spec = pl.BlockSpec((None, None, tka, tnb), lambda i, j: (i // ni, slot, i % ni, j))
            out_shape = _sds((row_shards, 2, Ka // row_shards, n), out_dtype)
        else:
            out_spec = pl.BlockSpec((None, None, tka, tnb), lambda i, j: (j // nj, slot, i, j % nj))
            out_shape = _sds((shards, 2, Ka, n), out_dtype)
        if into is not None:
            in_specs.append(pl.BlockSpec(memory_space=pl.ANY))
            args.append(into)
            aliases = {2: 0}
    extra = plan.inputs if plan else []
    res = _pcall(
        body, name=name, grid=(Ka // tka, shards * nj),
        in_specs=in_specs + [ANY] * nx, out_specs=[out_spec] + [ANY] * nx,
        out_shape=[out_shape] + (plan.out_shape if plan else []),
        input_output_aliases=aliases,
        scratch_shapes=plan.scratch if plan else [],
        compiler_params=_params("arbitrary", "arbitrary"),
    )(*args, *extra)
    return res if plan else res[0]


def _mm_nt_normbwd(dy, w, h, g, dres, *, name, plan=None):
    T, D = h.shape
    S, _, n = w.shape
    tm = _tile(T, ROWS_WIDE, BF16_ROWS)
    nx, before, after = _ride_along(plan, T // tm, S)

    def body(*refs):
        dy_ref, w_ref, h_ref, g_ref, dr_ref = refs[:5]
        dh_ref, dhb_ref, dg_ref = refs[5 + nx:8 + nx]
        acc_ref = refs[8 + 2 * nx]
        ride = (refs[5:5 + nx], refs[8 + nx:8 + 2 * nx], refs[9 + 2 * nx:])
        before(ride)
        i, s = pl.program_id(0), pl.program_id(1)
        part = lax.dot_general(dy_ref[...], w_ref[...], (((1,), (1,)), ((), ())),
                               preferred_element_type=F32)

        @pl.when(s == 0)
        def _():
            acc_ref[...] = part

        @pl.when(s > 0)
        def _():
            acc_ref[...] += part

        @pl.when(s == S - 1)
        def _():
            dx, dg = _norm_bwd(h_ref[...], g_ref[...], acc_ref[...])
            dh = dr_ref[...] + dx
            dh_ref[...] = dh
            dhb_ref[...] = dh.astype(BF16)

            @pl.when(i == 0)
            def _():
                dg_ref[...] = dg

            @pl.when(i > 0)
            def _():
                dg_ref[...] += dg

        after(ride)

    extra = plan.inputs if plan else []
    return _pcall(
        body, name=name, grid=(T // tm, S),
        in_specs=[pl.BlockSpec((tm, n), lambda i, s: (i, s)),
                  pl.BlockSpec((None, D, n), lambda i, s: (s, 0, 0)),
                  pl.BlockSpec((tm, D), lambda i, s: (i, 0)),
                  pl.BlockSpec((1, D), lambda i, s: (0, 0)),
                  pl.BlockSpec((tm, D), lambda i, s: (i, 0))] + [ANY] * nx,
        out_specs=[pl.BlockSpec((tm, D), lambda i, s: (i, 0)),
                   pl.BlockSpec((tm, D), lambda i, s: (i, 0)),
                   pl.BlockSpec((1, D), lambda i, s: (0, 0))] + [ANY] * nx,
        out_shape=[_sds((T, D), F32), _sds((T, D), BF16), _sds((1, D), F32)] + (plan.out_shape if plan else []),
        scratch_shapes=[pltpu.VMEM((tm, D), F32)] + (plan.scratch if plan else []),
        compiler_params=_params("arbitrary", "arbitrary"),
    )(dy, w, h, g, dres, *extra)


def _row_chains(tm):
    first = (tm // 2 + 15) // 16 * 16
    return [slice(0, first), slice(first, tm)] if 0 < first < tm else [slice(0, tm)]


def _mlp_fwd(h, g, w_up, w_down, *, layer, name):
    T, D = h.shape
    S, _, _, n = w_up.shape
    tm = _tile(T, ROWS_NARROW, BF16_ROWS)
    tf = _tile(n, 1024, LANE)
    nj = n // tf
    nf = S * nj
    chains = _row_chains(tm)

    def body(h_ref, g_ref, wu_ref, wd_ref, o_ref, hn_ref, up_ref, acc_ref):
        f = pl.program_id(1)

        @pl.when(f == 0)
        def _():
            x = h_ref[...]
            hn_ref[...] = (x * _rstd(x) * g_ref[...]).astype(BF16)

        parts = []
        for rows in chains:
            up = jnp.dot(hn_ref[rows, :], wu_ref[...], preferred_element_type=F32)
            up_ref[rows, :] = up.astype(BF16)
            r = jnp.maximum(up, 0.0)
            parts.append(jnp.dot((r * r).astype(BF16), wd_ref[...], preferred_element_type=F32))
        part = jnp.concatenate(parts, axis=0)

        @pl.when(f == 0)
        def _():
            acc_ref[...] = part

        @pl.when(f > 0)
        def _():
            acc_ref[...] += part

        @pl.when(f == nf - 1)
        def _():
            o_ref[...] = h_ref[...] + acc_ref[...]

    return _pcall(
        body, name=name, grid=(T // tm, nf),
        in_specs=[pl.BlockSpec((tm, D), lambda i, f: (i, 0)),
                  pl.BlockSpec((1, D), lambda i, f: (0, 0)),
                  pl.BlockSpec((None, None, D, tf), lambda i, f: (f // nj, layer, 0, f % nj)),
                  pl.BlockSpec((None, None, tf, D), lambda i, f: (f // nj, layer, f % nj, 0))],
        out_specs=[pl.BlockSpec((tm, D), lambda i, f: (i, 0)),
                   pl.BlockSpec((tm, D), lambda i, f: (i, 0)),
                   pl.BlockSpec((tm, tf), lambda i, f: (i, f))],
        out_shape=[_sds((T, D), F32), _sds((T, D), BF16), _sds((T, S * n), BF16)],
        scratch_shapes=[pltpu.VMEM((tm, D), F32)],
        compiler_params=_params("parallel", "arbitrary"),
    )(h, g, w_up, w_down)


def _mlp_bwd(dy, h, g, up, w_up, w_down, *, layer, name):
    T, D = h.shape
    S, _, _, n = w_up.shape
    tm = _tile(T, ROWS_NARROW, BF16_ROWS)
    tf = _tile(n, 1024, LANE)
    nj = n // tf
    nf = S * nj
    chains = _row_chains(tm)

    def body(dy_ref, h_ref, g_ref, up_ref, wu_ref, wd_ref, dup_ref, dh_ref, dhb_ref, dg_ref,
             dyb_ref, acc_ref):
        i, f = pl.program_id(0), pl.program_id(1)

        @pl.when(f == 0)
        def _():
            dyb_ref[...] = dy_ref[...].astype(BF16)

        parts = []
        for rows in chains:
            dact = lax.dot_general(dyb_ref[rows, :], wd_ref[...], (((1,), (1,)), ((), ())),
                                   preferred_element_type=F32)
            r = jnp.maximum(up_ref[rows, :].astype(F32), 0.0)
            dup = (dact * (2.0 * r)).astype(BF16)
            dup_ref[rows, :] = dup
            parts.append(lax.dot_general(dup, wu_ref[...], (((1,), (1,)), ((), ())),
                                         preferred_element_type=F32))
        part = jnp.concatenate(parts, axis=0)

        @pl.when(f == 0)
        def _():
            acc_ref[...] = part

        @pl.when(f > 0)
        def _():
            acc_ref[...] += part

        @pl.when(f == nf - 1)
        def _():
            dx, dg = _norm_bwd(h_ref[...], g_ref[...], acc_ref[...])
            dh = dy_ref[...] + dx
            dh_ref[...] = dh
            dhb_ref[...] = dh.astype(BF16)

            @pl.when(i == 0)
            def _():
                dg_ref[...] = dg

            @pl.when(i > 0)
            def _():
                dg_ref[...] += dg

    return _pcall(
        body, name=name, grid=(T // tm, nf),
        in_specs=[pl.BlockSpec((tm, D), lambda i, f: (i, 0)),
                  pl.BlockSpec((tm, D), lambda i, f: (i, 0)),
                  pl.BlockSpec((1, D), lambda i, f: (0, 0)),
                  pl.BlockSpec((tm, tf), lambda i, f: (i, f)),
                  pl.BlockSpec((None, None, D, tf), lambda i, f: (f // nj, layer, 0, f % nj)),
                  pl.BlockSpec((None, None, tf, D), lambda i, f: (f // nj, layer, f % nj, 0))],
        out_specs=[pl.BlockSpec((tm, tf), lambda i, f: (i, f)),
                   pl.BlockSpec((tm, D), lambda i, f: (i, 0)),
                   pl.BlockSpec((tm, D), lambda i, f: (i, 0)),
                   pl.BlockSpec((1, D), lambda i, f: (0, 0))],
        out_shape=[_sds((T, S * n), BF16), _sds((T, D), F32), _sds((T, D), BF16), _sds((1, D), F32)],
        scratch_shapes=[pltpu.VMEM((tm, D), BF16), pltpu.VMEM((tm, D), F32)],
        compiler_params=_params("arbitrary", "arbitrary"),
    )(dy, h, g, up, w_up, w_down)


ATT_BLOCK = 128
ATT_HEADS = 4


def _attn_tile(T):
    for w in (3 * ATT_BLOCK, 2 * ATT_BLOCK):
        if T % w == 0:
            return w
    return ATT_BLOCK


def _tri(strict_lower, value):
    B = ATT_BLOCK
    r = lax.broadcasted_iota(jnp.int32, (2 * B, B), 0)
    r = jnp.where(r >= B, r - B, r)
    c = lax.broadcasted_iota(jnp.int32, (2 * B, B), 1)
    m = (r > c) if strict_lower else (r < c)
    return jnp.where(m, value, 0.0).astype(BF16)


def _split_dot(x, tri):
    hi = lax.bitcast_convert_type(lax.bitcast_convert_type(x, jnp.uint32) & jnp.uint32(0xFFFF0000), F32)
    lo = x - hi
    return jnp.dot(jnp.concatenate([hi.astype(BF16), lo.astype(BF16)], axis=1), tri,
                   preferred_element_type=F32)


def _causal_mask(W):
    r = lax.broadcasted_iota(jnp.int32, (W, W), 0)
    c = lax.broadcasted_iota(jnp.int32, (W, W), 1)
    return c < r


def _two_heads(x):
    left = lax.broadcasted_iota(jnp.int32, x.shape, 1) < HEAD_DIM
    zero = jnp.zeros_like(x)
    return jnp.concatenate([jnp.where(left, x, zero), jnp.where(left, zero, x)], axis=0)


def _attn_scores(z, carry, tri_neg, masked):
    W = z.shape[0]
    B = ATT_BLOCK
    minus_abs = lax.bitcast_convert_type(
        lax.bitcast_convert_type(z, jnp.uint32) | jnp.uint32(0x80000000), F32)
    sp = jnp.maximum(z, 0.0) + jnp.log(1.0 + jnp.exp(minus_abs))
    logsig = z - sp
    if masked:
        causal = _causal_mask(W)
        sp = jnp.where(causal, sp, 0.0)
    afters = []
    for b in reversed(range(W // B)):
        blk = sp[:, b * B:(b + 1) * B]
        within = _split_dot(blk, tri_neg)
        afters.append(within + carry)
        carry = carry + (within[:, 0:1] - blk[:, 0:1])
    after = jnp.concatenate(afters[::-1], axis=1)
    w = jnp.exp(logsig + after)
    if masked:
        w = jnp.where(causal, w, 0.0)
    return w, jnp.exp(logsig), carry


def _ride_along(plan, npairs, nq):
    if plan is None:
        return 0, (lambda refs: None), (lambda refs: None)
    nx = plan.n

    def before(refs):
        p, i = pl.program_id(0), pl.program_id(1)

        @pl.when(jnp.logical_and(p == 0, i == 0))
        def _():
            plan.start(*refs)

        @pl.when(jnp.logical_and(p == npairs // 2, i == 0))
        def _():
            plan.mid(*refs)

    def after(refs):
        p, i = pl.program_id(0), pl.program_id(1)

        @pl.when(jnp.logical_and(p == npairs - 1, i == nq - 1))
        def _():
            plan.finish(*refs)

    return nx, before, after


def _saved_tile(i, j):
    return i * (i + 1) // 2 + j


def _attn_fwd(qkv, *, name, plan=None):
    T = qkv.shape[0]
    D = qkv.shape[1] // 3
    W = _attn_tile(T)
    H = 2 * ATT_HEADS if D % (2 * ATT_HEADS * HEAD_DIM) == 0 else ATT_HEADS
    lanes = H * HEAD_DIM
    ngroups = D // lanes
    nq = T // W
    ntri = nq * (nq + 1) // 2
    scale = HEAD_DIM ** -0.5
    nx, before, after = _ride_along(plan, ngroups, nq)

    def body(*refs):
        q_ref, k_ref, v_ref = refs[:3]
        o_ref, wsv_ref, ssv_ref = refs[3 + nx:6 + nx]
        stage_w, stage_s, stage_sems = refs[6 + 2 * nx:9 + 2 * nx]
        ride = (refs[3:3 + nx], refs[6 + nx:6 + 2 * nx], refs[9 + 2 * nx:])
        before(ride)
        p, i = pl.program_id(0), pl.program_id(1)
        tri = _tri(True, -1.0)
        pairs = [slice(pp * LANE, (pp + 1) * LANE) for pp in range(H // 2)]
        qs = [q_ref[:, cols] * scale for cols in pairs]

        def save(slot, j):
            dst = _saved_tile(i, j)
            return [pltpu.make_async_copy(stage.at[slot], sv.at[pl.ds(p * H, H), dst], stage_sems.at[slot, a])
                    for a, (sv, stage) in enumerate(((wsv_ref, stage_w), (ssv_ref, stage_s)))]

        def tile(t, state, masked):
            j = i - t
            rows = pl.ds(pl.multiple_of(j * W, W), W)
            slot = t % 2
            if not masked:
                @pl.when(t >= 2)
                def _():
                    for cp in save(slot, j):
                        cp.wait()
            out = []
            for pp, (cols, q, (carries, acc)) in enumerate(zip(pairs, qs, state)):
                z = lax.dot_general(q, _two_heads(k_ref[rows, cols]), (((1,), (1,)), ((), ())),
                                    preferred_element_type=F32)
                wbs, new_carries = [], []
                for e, carry in enumerate(carries):
                    w, sig, carry = _attn_scores(z[:, e * W:(e + 1) * W], carry, tri, masked)
                    wb = w.astype(BF16)
                    stage_w[slot, 2 * pp + e] = wb
                    stage_s[slot, 2 * pp + e] = sig.astype(BF16)
                    wbs.append(wb)
                    new_carries.append(carry)
                acc = acc + jnp.dot(jnp.concatenate(wbs, axis=1), _two_heads(v_ref[rows, cols]),
                                    preferred_element_type=F32)
                out.append((tuple(new_carries), acc))
            for cp in save(slot, j):
                cp.start()
            return tuple(out)

        zero = ((jnp.zeros((W, 1), F32),) * 2, jnp.zeros((W, LANE), F32))
        state = tile(0, (zero,) * (H // 2), True)
        state = lax.fori_loop(1, i + 1, lambda t, st: tile(t, st, False), state)
        for cols, (_, acc) in zip(pairs, state):
            o_ref[:, cols] = acc.astype(BF16)
        for cp in save(i % 2, 0):
            cp.wait()

        @pl.when(i >= 1)
        def _():
            for cp in save((i + 1) % 2, 0):
                cp.wait()

        after(ride)

    extra = plan.inputs if plan else []
    saved = _sds((D // HEAD_DIM, ntri, W, W), BF16)
    return _pcall(
        body, name=name, grid=(ngroups, nq),
        in_specs=[pl.BlockSpec((W, lanes), lambda p, i: (i, p)),
                  pl.BlockSpec((T, lanes), lambda p, i: (0, ngroups + p)),
                  pl.BlockSpec((T, lanes), lambda p, i: (0, 2 * ngroups + p))] + [ANY] * nx,
        out_specs=[pl.BlockSpec((W, lanes), lambda p, i: (i, p)), ANY, ANY] + [ANY] * nx,
        out_shape=[_sds((T, D), BF16), saved, saved] + (plan.out_shape if plan else []),
        scratch_shapes=[pltpu.VMEM((2, H, W, W), BF16), pltpu.VMEM((2, H, W, W), BF16),
                        pltpu.SemaphoreType.DMA((2, 2))] + (plan.scratch if plan else []),
        compiler_params=_params("arbitrary", "arbitrary"),
    )(qkv, qkv, qkv, *extra)


def _attn_bwd(qkv, do, w_sv, s_sv, *, name, plan=None):
    T = qkv.shape[0]
    D = qkv.shape[1] // 3
    B = ATT_BLOCK
    W = _attn_tile(T)
    H = ATT_HEADS
    lanes = H * HEAD_DIM
    ngroups = D // lanes
    nq = T // W
    scale = HEAD_DIM ** -0.5
    nx, before, after = _ride_along(plan, ngroups, nq)

    def body(*refs):
        q_ref, k_ref, v_ref, do_ref, wsv_ref, ssv_ref = refs[:6]
        dq_ref, dk_ref, dv_ref = refs[6 + nx:9 + nx]
        dk_acc, dv_acc, stage_w, stage_s, stage_sems = refs[9 + 2 * nx:14 + 2 * nx]
        ride = (refs[6:6 + nx], refs[9 + nx:9 + 2 * nx], refs[14 + 2 * nx:])
        before(ride)
        p, i = pl.program_id(0), pl.program_id(1)

        @pl.when(i == 0)
        def _():
            dk_acc[...] = jnp.zeros_like(dk_acc)
            dv_acc[...] = jnp.zeros_like(dv_acc)

        def fetch(slot, j):
            src = _saved_tile(i, j)
            return [pltpu.make_async_copy(sv.at[pl.ds(p * H, H), src], stage.at[slot], stage_sems.at[slot, a])
                    for a, (sv, stage) in enumerate(((wsv_ref, stage_w), (ssv_ref, stage_s)))]

        tri_before = _tri(False, 1.0)
        pairs = [slice(pp * LANE, (pp + 1) * LANE) for pp in range(H // 2)]
        q_two = [_two_heads(q_ref[:, cols] * scale) for cols in pairs]
        do_pair = [do_ref[:, cols] for cols in pairs]
        do_two = [_two_heads(d) for d in do_pair]

        def grad(j, state, masked):
            rows = pl.ds(pl.multiple_of(j * W, W), W)
            slot = j % 2
            for cp in fetch(slot, j):
                cp.wait()
            if not masked:
                for cp in fetch(1 - slot, j + 1):
                    cp.start()
            out = []
            for pp, (cols, (gsums, dq)) in enumerate(zip(pairs, state)):
                dw = lax.dot_general(do_pair[pp], _two_heads(v_ref[rows, cols]), (((1,), (1,)), ((), ())),
                                     preferred_element_type=F32)
                dzs, wbs, new_gsums = [], [], []
                for e, gsum in enumerate(gsums):
                    wb = stage_w[slot, 2 * pp + e]
                    sig = stage_s[slot, 2 * pp + e].astype(F32)
                    g = dw[:, e * W:(e + 1) * W] * wb.astype(F32)
                    befores = []
                    for b in range(W // B):
                        blk = g[:, b * B:(b + 1) * B]
                        within = _split_dot(blk, tri_before)
                        befores.append(within + gsum)
                        gsum = gsum + (within[:, B - 1:B] + blk[:, B - 1:B])
                    dz = g - sig * (g + jnp.concatenate(befores, axis=1))
                    if masked:
                        dz = jnp.where(_causal_mask(W), dz, 0.0)
                    dzs.append(dz.astype(BF16))
                    wbs.append(wb)
                    new_gsums.append(gsum)
                dq = dq + jnp.dot(jnp.concatenate(dzs, axis=1), _two_heads(k_ref[rows, cols] * scale),
                                  preferred_element_type=F32)
                dk_acc[rows, cols] += lax.dot_general(jnp.concatenate(dzs, axis=0), q_two[pp],
                                                      (((0,), (0,)), ((), ())), preferred_element_type=F32)
                dv_acc[rows, cols] += lax.dot_general(jnp.concatenate(wbs, axis=0), do_two[pp],
                                                      (((0,), (0,)), ((), ())), preferred_element_type=F32)
                out.append((tuple(new_gsums), dq))
            return tuple(out)

        for cp in fetch(0, 0):
            cp.start()
        zero = ((jnp.zeros((W, 1), F32),) * 2, jnp.zeros((W, LANE), F32))
        state = lax.fori_loop(0, i, lambda j, st: grad(j, st, False), (zero,) * (H // 2))
        state = grad(i, state, True)
        for cols, (_, dq) in zip(pairs, state):
            dq_ref[:, cols] = dq.astype(BF16)

        @pl.when(i == nq - 1)
        def _():
            dk_ref[...] = dk_acc[...].astype(BF16)
            dv_ref[...] = dv_acc[...].astype(BF16)

        after(ride)

    extra = plan.inputs if plan else []
    return _pcall(
        body, name=name, grid=(ngroups, nq),
        in_specs=[pl.BlockSpec((W, lanes), lambda p, i: (i, p)),
                  pl.BlockSpec((T, lanes), lambda p, i: (0, ngroups + p)),
                  pl.BlockSpec((T, lanes), lambda p, i: (0, 2 * ngroups + p)),
                  pl.BlockSpec((W, lanes), lambda p, i: (i, p)), ANY, ANY] + [ANY] * nx,
        out_specs=[pl.BlockSpec((W, lanes), lambda p, i: (i, p)),
                   pl.BlockSpec((T, lanes), lambda p, i: (0, p)),
                   pl.BlockSpec((T, lanes), lambda p, i: (0, p))] + [ANY] * nx,
        out_shape=[_sds((T, D), BF16)] * 3 + (plan.out_shape if plan else []),
        scratch_shapes=[pltpu.VMEM((T, lanes), F32), pltpu.VMEM((T, lanes), F32),
                        pltpu.VMEM((2, H, W, W), BF16), pltpu.VMEM((2, H, W, W), BF16),
                        pltpu.SemaphoreType.DMA((2, 2))]
        + (plan.scratch if plan else []),
        compiler_params=_params("arbitrary", "arbitrary"),
    )(qkv, qkv, qkv, do, w_sv, s_sv, *extra)


HALO = SUBLANE


def _lru_gates(u, w_rg, b_rg, w_ig, b_ig, lam):
    nb = w_rg.shape[0]
    pre_r, pre_i = [], []
    for n in range(nb):
        ub = u[:, n * LANE:(n + 1) * LANE].astype(BF16)
        pre_r.append(jnp.dot(ub, w_rg[n], preferred_element_type=F32))
        pre_i.append(jnp.dot(ub, w_ig[n], preferred_element_type=F32))
    r = _sigmoid(jnp.concatenate(pre_r, axis=1) + b_rg)
    i = _sigmoid(jnp.concatenate(pre_i, axis=1) + b_ig)
    c = -LRU_C * _softplus_parts(-lam)[0]
    log_a = c * r
    a = jnp.exp(log_a)
    x2 = 2.0 * log_a
    em1 = jnp.where(jnp.abs(x2) < 1e-2, x2 * (1.0 + x2 * (0.5 + x2 * (1.0 / 6.0))), jnp.exp(x2) - 1.0)
    mult = jnp.sqrt(-em1)
    return r, i, a, mult, c


def _conv_rows(buf_ref, tt, conv_w, conv_b):
    u = conv_b
    for j in range(4):
        u = u + buf_ref[pl.ds(HALO - 3 + j, tt), :] * conv_w[j:j + 1, :]
    return u


def _fill_with_halo(buf_ref, prev_ref, cur_ref, first):
    tt = cur_ref.shape[0]
    buf_ref[pl.ds(0, HALO), :] = jnp.where(first, 0.0, prev_ref[...])
    buf_ref[pl.ds(HALO, tt), :] = cur_ref[...]


def _lru_time_tile(T):
    return _tile(T, 256, SUBLANE)


def _lru_pre(gr, conv_w, conv_b, w_rg, b_rg, w_ig, b_ig, lam, *, name):
    T = gr.shape[0]
    D = gr.shape[1] // 2
    tt = _lru_time_tile(T)
    hb = tt // HALO

    def body(x_ref, xp_ref, cw_ref, cb_ref, wr_ref, br_ref, wi_ref, bi_ref, lam_ref, a_ref, b_ref, buf):
        _fill_with_halo(buf, xp_ref, x_ref, pl.program_id(0) == 0)
        u = _conv_rows(buf, tt, cw_ref[...], cb_ref[...])
        _, i, a, mult, _ = _lru_gates(u, wr_ref, br_ref[...], wi_ref, bi_ref[...], lam_ref[...])
        a_ref[...] = a
        b_ref[...] = mult * (i * u)

    vec = pl.BlockSpec((1, D), lambda t: (0, 0))
    mat = pl.BlockSpec(w_rg.shape, lambda t: (0, 0, 0))
    return _pcall(
        body, name=name, grid=(T // tt,),
        in_specs=[pl.BlockSpec((tt, D), lambda t: (t, 1)),
                  pl.BlockSpec((HALO, D), lambda t: (jnp.maximum(t * hb - 1, 0), 1)),
                  pl.BlockSpec((4, D), lambda t: (0, 0)), vec, mat, vec, mat, vec, vec],
        out_specs=[pl.BlockSpec((tt, D), lambda t: (t, 0))] * 2,
        out_shape=[_sds((T, D), F32)] * 2,
        scratch_shapes=[pltpu.VMEM((tt + HALO, D), F32)],
        compiler_params=_params("parallel"),
    )(gr, gr, conv_w, conv_b, w_rg, b_rg, w_ig, b_ig, lam)


def _lru_scan(a, b, *, reverse, name):
    T, D = a.shape
    R = SUBLANE
    ts = _tile(T, 528, R)
    nt = T // ts

    def body(a_ref, b_ref, o_ref, carry):
        @pl.when(pl.program_id(0) == 0)
        def _():
            carry[...] = jnp.zeros_like(carry)

        rowid = lax.broadcasted_iota(jnp.int32, (R, D), 0)

        def chunk(k, run):
            if reverse:
                k = ts // R - 1 - k
            rows = pl.ds(pl.multiple_of(k * R, R), R)
            at, bt = a_ref[rows, :], b_ref[rows, :]
            out = jnp.zeros((R, D), F32)
            for r in (range(R - 1, -1, -1) if reverse else range(R)):
                if reverse:
                    cand = bt + run
                    nxt = at * cand
                else:
                    cand = at * run + bt
                    nxt = cand
                out = jnp.where(rowid == r, cand, out)
                run = jnp.broadcast_to(nxt[r:r + 1, :], (R, D))
            o_ref[rows, :] = out
            return run

        carry[...] = lax.fori_loop(0, ts // R, chunk, carry[...])

    if reverse:
        spec = pl.BlockSpec((ts, D), lambda t: (nt - 1 - t, 0))
    else:
        spec = pl.BlockSpec((ts, D), lambda t: (t, 0))
    return _pcall(
        body, name=name, grid=(nt,),
        in_specs=[spec, spec], out_specs=spec,
        out_shape=_sds((T, D), F32),
        scratch_shapes=[pltpu.VMEM((R, D), F32)],
        compiler_params=_params("arbitrary"),
    )(a, b)


def _lru_out(gr, hs, w, res, *, name):
    T, D = hs.shape
    tt = _tile(T, ROWS_NARROW, BF16_ROWS)

    def body(g_ref, h_ref, w_ref, r_ref, y_ref, o_ref):
        y = (h_ref[...] * _gelu_parts(g_ref[...])[0]).astype(BF16)
        y_ref[...] = y
        o_ref[...] = r_ref[...] + jnp.dot(y, w_ref[...], preferred_element_type=F32)

    blk = pl.BlockSpec((tt, D), lambda t: (t, 0))
    return _pcall(
        body, name=name, grid=(T // tt,),
        in_specs=[blk, blk, pl.BlockSpec((D, D), lambda t: (0, 0)), blk],
        out_specs=[blk, blk],
        out_shape=[_sds((T, D), BF16), _sds((T, D), F32)],
        compiler_params=_params("parallel"),
    )(gr, hs, w, res)


def _lru_out_bwd(gr, hs, dout, w, *, name):
    T, D = hs.shape
    tt = _tile(T, ROWS_NARROW, BF16_ROWS)

    def body(g_ref, h_ref, d_ref, w_ref, dg_ref, dh_ref):
        dy = lax.dot_general(d_ref[...], w_ref[...], (((1,), (1,)), ((), ())), preferred_element_type=F32)
        gelu, dgelu = _gelu_parts(g_ref[...])
        dg_ref[...] = (dy * h_ref[...] * dgelu).astype(BF16)
        dh_ref[...] = dy * gelu

    spec = pl.BlockSpec((tt, D), lambda t: (t, 0))
    return _pcall(
        body, name=name, grid=(T // tt,),
        in_specs=[spec, spec, spec, pl.BlockSpec((D, D), lambda t: (0, 0))], out_specs=[spec, spec],
        out_shape=[_sds((T, 2 * D), BF16), _sds((T, D), F32)],
        compiler_params=_params("parallel"),
    )(gr, hs, dout, w)


def _lru_gate_bwd(gr, hs, lmb, conv_w, conv_b, w_rg, b_rg, w_ig, b_ig, lam, *, name):
    T, D = hs.shape
    nb = w_rg.shape[0]
    tt = _lru_time_tile(T)
    hb = tt // HALO
    nt = T // tt

    def body(x_ref, xp_ref, h_ref, hp_ref, l_ref, cw_ref, cb_ref, wr_ref, br_ref, wi_ref, bi_ref, lam_ref,
             du_ref, dwr_ref, dbr_ref, dwi_ref, dbi_ref, dlam_ref, xbuf, hbuf):
        t = pl.program_id(0)
        first = t == 0
        _fill_with_halo(xbuf, xp_ref, x_ref, first)
        _fill_with_halo(hbuf, hp_ref, h_ref, first)
        u = _conv_rows(xbuf, tt, cw_ref[...], cb_ref[...])
        lam_v = lam_ref[...]
        r, i, a, mult, c = _lru_gates(u, wr_ref, br_ref[...], wi_ref, bi_ref[...], lam_v)
        l = l_ref[...]
        h_prev = hbuf[pl.ds(HALO - 1, tt), :]
        dlog_a = l * h_prev * a - l * (i * u) * (a * a) / mult
        d_iu = l * mult
        dpre_r = (dlog_a * c) * (r * (1.0 - r))
        dpre_i = (d_iu * u) * (i * (1.0 - i))
        dpr_b = dpre_r.astype(BF16)
        dpi_b = dpre_i.astype(BF16)
        du_parts, dwr, dwi = [], [], []
        for n in range(nb):
            cs = slice(n * LANE, (n + 1) * LANE)
            ub = u[:, cs].astype(BF16)
            du_parts.append(
                lax.dot_general(dpr_b[:, cs], wr_ref[n], (((1,), (1,)), ((), ())), preferred_element_type=F32)
                + lax.dot_general(dpi_b[:, cs], wi_ref[n], (((1,), (1,)), ((), ())), preferred_element_type=F32))
            dwr.append(lax.dot_general(ub, dpr_b[:, cs], (((0,), (0,)), ((), ())), preferred_element_type=F32))
            dwi.append(lax.dot_general(ub, dpi_b[:, cs], (((0,), (0,)), ((), ())), preferred_element_type=F32))
        du_ref[...] = d_iu * i + jnp.concatenate(du_parts, axis=1)
        dbr = jnp.sum(dpre_r, axis=0, keepdims=True)
        dbi = jnp.sum(dpre_i, axis=0, keepdims=True)
        dc = jnp.sum(dlog_a * r, axis=0, keepdims=True)

        @pl.when(first)
        def _():
            for n in range(nb):
                dwr_ref[n] = dwr[n]
                dwi_ref[n] = dwi[n]
            dbr_ref[...] = dbr
            dbi_ref[...] = dbi
            dlam_ref[...] = dc

        @pl.when(t > 0)
        def _():
            for n in range(nb):
                dwr_ref[n] += dwr[n]
                dwi_ref[n] += dwi[n]
            dbr_ref[...] += dbr
            dbi_ref[...] += dbi
            dlam_ref[...] += dc

        @pl.when(t == nt - 1)
        def _():
            dlam_ref[...] = dlam_ref[...] * (LRU_C * _sigmoid(-lam_v))

    vec = pl.BlockSpec((1, D), lambda t: (0, 0))
    mat = pl.BlockSpec(w_rg.shape, lambda t: (0, 0, 0))
    blk = pl.BlockSpec((tt, D), lambda t: (t, 0))
    prev = pl.BlockSpec((HALO, D), lambda t: (jnp.maximum(t * hb - 1, 0), 0))
    return _pcall(
        body, name=name, grid=(nt,),
        in_specs=[pl.BlockSpec((tt, D), lambda t: (t, 1)),
                  pl.BlockSpec((HALO, D), lambda t: (jnp.maximum(t * hb - 1, 0), 1)),
                  blk, prev, blk,
                  pl.BlockSpec((4, D), lambda t: (0, 0)), vec, mat, vec, mat, vec, vec],
        out_specs=[blk, mat, vec, mat, vec, vec],
        out_shape=[_sds((T, D), F32), _sds(w_rg.shape, F32), _sds((1, D), F32),
                   _sds(w_rg.shape, F32), _sds((1, D), F32), _sds((1, D), F32)],
        scratch_shapes=[pltpu.VMEM((tt + HALO, D), F32), pltpu.VMEM((tt + HALO, D), F32)],
        compiler_params=_params("arbitrary"),
    )(gr, gr, hs, hs, lmb, conv_w, conv_b, w_rg, b_rg, w_ig, b_ig, lam)


def _lru_conv_bwd(gr, du, conv_w, dgr, *, name):
    T, D = du.shape
    tt = _lru_time_tile(T)
    hb = tt // HALO
    nt = T // tt

    def body(x_ref, xp_ref, du_ref, dun_ref, cw_ref, _, dx_ref, dcw_ref, dcb_ref, xbuf, dbuf):
        t = pl.program_id(0)
        _fill_with_halo(xbuf, xp_ref, x_ref, t == 0)
        du = du_ref[...]
        dbuf[pl.ds(0, tt), :] = du
        dbuf[pl.ds(tt, HALO), :] = jnp.where(t == nt - 1, 0.0, dun_ref[...])
        cw = cw_ref[...]
        dx = jnp.zeros((tt, D), F32)
        dcw = []
        for j in range(4):
            dx = dx + dbuf[pl.ds(3 - j, tt), :] * cw[j:j + 1, :]
            dcw.append(jnp.sum(du * xbuf[pl.ds(HALO - 3 + j, tt), :], axis=0, keepdims=True))
        dx_ref[...] = dx.astype(BF16)
        dcw = jnp.concatenate(dcw, axis=0)
        dcb = jnp.sum(du, axis=0, keepdims=True)

        @pl.when(t == 0)
        def _():
            dcw_ref[...] = dcw
            dcb_ref[...] = dcb

        @pl.when(t > 0)
        def _():
            dcw_ref[...] += dcw
            dcb_ref[...] += dcb

    blk = pl.BlockSpec((tt, D), lambda t: (t, 0))
    return _pcall(
        body, name=name, grid=(nt,),
        in_specs=[pl.BlockSpec((tt, D), lambda t: (t, 1)),
                  pl.BlockSpec((HALO, D), lambda t: (jnp.maximum(t * hb - 1, 0), 1)),
                  blk,
                  pl.BlockSpec((HALO, D), lambda t: (jnp.minimum((t + 1) * hb, T // HALO - 1), 0)),
                  pl.BlockSpec((4, D), lambda t: (0, 0)), ANY],
        out_specs=[pl.BlockSpec((tt, D), lambda t: (t, 1)),
                   pl.BlockSpec((4, D), lambda t: (0, 0)), pl.BlockSpec((1, D), lambda t: (0, 0))],
        out_shape=[_sds((T, 2 * D), BF16), _sds((4, D), F32), _sds((1, D), F32)],
        input_output_aliases={5: 0},
        scratch_shapes=[pltpu.VMEM((tt + HALO, D), F32), pltpu.VMEM((tt + HALO, D), F32)],
        compiler_params=_params("arbitrary"),
    )(gr, gr, du, du, conv_w, dgr)


def _loss_head(h, g, target, *, row_lo, row_hi, name):
    T, D = h.shape
    tm = _tile(T, ROWS_WIDE, BF16_ROWS)

    def body(h_ref, g_ref, t_ref, loss_ref, dh_ref, dhb_ref, dg_ref):
        i = pl.program_id(0)
        x = h_ref[...]
        g = g_ref[...]
        row = i * tm + lax.broadcasted_iota(jnp.int32, (tm, 1), 0)
        valid = jnp.logical_and(row >= row_lo, row < row_hi)
        rstd = _rstd(x)
        n = x * rstd
        err = jnp.where(valid, n * g - t_ref[...], 0.0)
        part = (0.5 / D) * jnp.sum(jnp.sum(err * err, axis=1, keepdims=True), axis=0, keepdims=True)
        dy = err * (1.0 / D)
        dn = dy * g
        dh = rstd * (dn - n * jnp.mean(dn * n, axis=-1, keepdims=True))
        dh_ref[...] = dh
        dhb_ref[...] = dh.astype(BF16)
        dg = jnp.sum(dy * n, axis=0, keepdims=True)

        @pl.when(i == 0)
        def _():
            loss_ref[...] = part
            dg_ref[...] = dg

        @pl.when(i > 0)
        def _():
            loss_ref[...] += part
            dg_ref[...] += dg

    blk = pl.BlockSpec((tm, D), lambda i: (i, 0))
    vec = pl.BlockSpec((1, D), lambda i: (0, 0))
    return _pcall(
        body, name=name, grid=(T // tm,),
        in_specs=[blk, vec, blk],
        out_specs=[pl.BlockSpec((1, 1), lambda i: (0, 0)), blk, blk, vec],
        out_shape=[_sds((1, 1), F32), _sds((T, D), F32), _sds((T, D), BF16), _sds((1, D), F32)],
        compiler_params=_params("arbitrary"),
    )(h, g, target)


def _adamw_math(w, g, m, v):
    c1 = 1.0 / (1.0 - ADAM_B1 ** ADAM_STEP)
    c2 = 1.0 / (1.0 - ADAM_B2 ** ADAM_STEP)
    m = ADAM_B1 * m + (1.0 - ADAM_B1) * g
    v = ADAM_B2 * v + (1.0 - ADAM_B2) * (g * g)
    delta = -ADAM_LR * ((m * c1) / (jnp.sqrt(v * c2) + ADAM_EPS) + ADAM_WD * w)
    return delta, m, v


def _adamw_halves(w, mine, theirs, m, v, c, *, name):
    _, R, C = w.shape
    tr = _tile(R, 256, SUBLANE)

    def body(c_ref, w_ref, a_ref, b_ref, m_ref, v_ref, g_ref, d_ref, nm_ref, nv_ref):
        g = jnp.where(pl.program_id(0) == c_ref[0], a_ref[...], b_ref[...])
        g_ref[...] = g
        d_ref[...], nm_ref[...], nv_ref[...] = _adamw_math(w_ref[...], g, m_ref[...], v_ref[...])

    full = pl.BlockSpec((None, tr, C), lambda h, i, c_ref: (h, i, 0))
    half = pl.BlockSpec((tr, C), lambda h, i, c_ref: (i, 0))
    return _pcall(
        body, name=name,
        grid_spec=pltpu.PrefetchScalarGridSpec(
            num_scalar_prefetch=1, grid=(2, R // tr),
            in_specs=[full, half, half, full, full], out_specs=[full] * 4),
        out_shape=[_sds((2, R, C), F32)] * 4,
        compiler_params=_params("parallel", "parallel"),
    )(c, w, mine, theirs, m, v)


def _adamw(w, g, m, v, *, name):
    R, C = w.shape
    tr = _tile(R, 512, SUBLANE)

    def body(w_ref, g_ref, m_ref, v_ref, d_ref, nm_ref, nv_ref):
        d_ref[...], nm_ref[...], nv_ref[...] = _adamw_math(w_ref[...], g_ref[...], m_ref[...], v_ref[...])

    blk = pl.BlockSpec((tr, C), lambda i: (i, 0))
    return _pcall(
        body, name=name, grid=(R // tr,),
        in_specs=[blk] * 4, out_specs=[blk] * 3,
        out_shape=[_sds((R, C), F32)] * 3,
        compiler_params=_params("parallel"),
    )(w, g, m, v)


ANY = pl.BlockSpec(memory_space=pl.ANY)


def _place():
    x, y, c = lax.axis_index("x"), lax.axis_index("y"), lax.axis_index("c")
    chips = [(1 - x, y), (x, 1 - y), (1 - x, 1 - y)]
    return x, y, c, chips


LOCAL_PIECES = 4


class _GatherChips:
    def __init__(self, vs):
        self.inputs = list(vs)
        n = self.n = len(vs)
        self.out_shape = [_sds((4,) + v.shape, v.dtype) for v in vs]
        self.scratch = [pltpu.SemaphoreType.DMA((6 * n,)), pltpu.SemaphoreType.DMA((6 * n,)),
                        pltpu.SemaphoreType.DMA((LOCAL_PIECES * n,))]

    def _copies(self, v_refs, o_refs, sems):
        send_sems, recv_sems, local_sems = sems
        x, y, c, chips = _place()
        me = 2 * x + y

        def copy(a, k, block, half, to, src=None):
            dst = o_refs[a].at[block, half]
            return pltpu.make_async_remote_copy(
                src_ref=dst if src is None else src, dst_ref=dst,
                send_sem=send_sems.at[6 * a + k], recv_sem=recv_sems.at[6 * a + k],
                device_id=to, device_id_type=MESH)

        ks = [(a, k, cx, cy) for a in range(self.n) for k, (cx, cy) in enumerate(chips)]

        def local():
            out = []
            for a in range(self.n):
                rows = self.inputs[a].shape[1] // (LOCAL_PIECES // 2)
                for p in range(LOCAL_PIECES):
                    h, r0 = p % 2, (p // 2) * rows
                    out.append(pltpu.make_async_copy(
                        v_refs[a].at[h, pl.ds(r0, rows)], o_refs[a].at[me, h, pl.ds(r0, rows)],
                        local_sems.at[LOCAL_PIECES * a + p]))
            return out

        return dict(
            first=lambda: [copy(a, k, me, c, (cx, cy, c), src=v_refs[a].at[c]) for a, k, cx, cy in ks],
            landed=lambda: [copy(a, k, 2 * cx + cy, c, (x, y, c)) for a, k, cx, cy in ks],
            passed=lambda: [copy(a, 3 + k, 2 * cx + cy, c, (x, y, 1 - c)) for a, k, cx, cy in ks],
            final=lambda: [copy(a, 3 + k, 2 * cx + cy, 1 - c, (x, y, c)) for a, k, cx, cy in ks],
            local=local)

    def start(self, v_refs, o_refs, sems):
        cps = self._copies(v_refs, o_refs, sems)
        for cp in cps["first"]() + cps["local"]():
            cp.start()

    def mid(self, v_refs, o_refs, sems):
        cps = self._copies(v_refs, o_refs, sems)
        for got, fwd in zip(cps["landed"](), cps["passed"]()):
            got.wait_recv()
            fwd.start()

    def finish(self, v_refs, o_refs, sems):
        cps = self._copies(v_refs, o_refs, sems)
        for cp in cps["final"]():
            cp.wait_recv()
        for cp in cps["first"]() + cps["passed"]():
            cp.wait_send()
        for cp in cps["local"]():
            cp.wait()


class _ExchangeBlocks:
    def __init__(self, ps):
        self.inputs = list(ps)
        n = self.n = len(ps)
        self.out_shape = [_sds((8,) + p.shape[2:], p.dtype) for p in ps]
        self.scratch = [pltpu.SemaphoreType.DMA((7 * n,)), pltpu.SemaphoreType.DMA((7 * n,))]

    def _copies(self, p_refs, o_refs, sems, incoming):
        send_sems, recv_sems = sems
        x, y, c, _ = _place()
        me = 4 * x + 2 * y + c
        out = []
        for a in range(self.n):
            for k in range(1, 8):
                px, py, pc = x ^ (k >> 2), y ^ ((k >> 1) & 1), c ^ (k & 1)
                out.append(pltpu.make_async_remote_copy(
                    src_ref=p_refs[a].at[2 * px + py, pc],
                    dst_ref=o_refs[a].at[4 * px + 2 * py + pc if incoming else me],
                    send_sem=send_sems.at[7 * a + k - 1], recv_sem=recv_sems.at[7 * a + k - 1],
                    device_id=(x, y, c) if incoming else (px, py, pc), device_id_type=MESH))
        return out

    def start(self, p_refs, o_refs, sems):
        for cp in self._copies(p_refs, o_refs, sems, False):
            cp.start()

    def mid(self, p_refs, o_refs, sems):
        pass

    def finish(self, p_refs, o_refs, sems):
        for cp in self._copies(p_refs, o_refs, sems, True):
            cp.wait_recv()
        for cp in self._copies(p_refs, o_refs, sems, False):
            cp.wait_send()


def _run_exchange(plan, *, name):
    n = plan.n

    def body(*refs):
        args = (refs[:n], refs[n:2 * n], refs[2 * n:])
        plan.start(*args)
        plan.mid(*args)
        plan.finish(*args)

    return _pcall(
        body, name=name, in_specs=[ANY] * n, out_specs=[ANY] * n,
        out_shape=plan.out_shape, scratch_shapes=plan.scratch,
    )(*plan.inputs)


class _SendSibling:
    def __init__(self, rs):
        self.inputs = list(rs)
        n = self.n = len(rs)
        self.out_shape = [_sds(r.shape, r.dtype) for r in rs]
        self.scratch = [pltpu.SemaphoreType.DMA((n,)), pltpu.SemaphoreType.DMA((n,))]

    def _copies(self, r_refs, o_refs, sems):
        send_sems, recv_sems = sems
        x, y, c, _ = _place()
        return [pltpu.make_async_remote_copy(
            src_ref=r_refs[a], dst_ref=o_refs[a], send_sem=send_sems.at[a], recv_sem=recv_sems.at[a],
            device_id=(x, y, 1 - c), device_id_type=MESH) for a in range(self.n)]

    def start(self, r_refs, o_refs, sems):
        for cp in self._copies(r_refs, o_refs, sems):
            cp.start()

    def mid(self, r_refs, o_refs, sems):
        pass

    def finish(self, r_refs, o_refs, sems):
        for cp in self._copies(r_refs, o_refs, sems):
            cp.wait()


def _add_devices(p, got, place, *, name):
    _, _, R, C = p.shape
    tr = _tile(R, 256, 16)

    def body(place_ref, p_ref, o_ref, out_ref):
        me = place_ref[2]
        own = p_ref[...].astype(F32)
        acc = jnp.where(me == 0, own, o_ref[0].astype(F32))
        for d in range(1, 8):
            acc = acc + jnp.where(me == d, own, o_ref[d].astype(F32))
        out_ref[...] = acc

    return _pcall(
        body, name=name,
        grid_spec=pltpu.PrefetchScalarGridSpec(
            num_scalar_prefetch=1, grid=(R // tr,),
            in_specs=[pl.BlockSpec((None, None, tr, C), lambda i, pr: (pr[0], pr[1], i, 0)),
                      pl.BlockSpec((8, tr, C), lambda i, pr: (0, i, 0))],
            out_specs=pl.BlockSpec((tr, C), lambda i, pr: (i, 0))),
        out_shape=_sds((R, C), F32),
        compiler_params=_params("parallel"),
    )(place, p, got)


def _round_up(n, m):
    return (n + m - 1) // m * m


def _f32_as_bf16(a):
    return lax.bitcast_convert_type(a.astype(F32), BF16).reshape(-1)


def _bf16_as_f32(a):
    return lax.bitcast_convert_type(a.reshape(-1, 2), F32)


def _by_chip_cols(a, cols):
    lead = a.shape[:-1]
    a = a.reshape(lead + (4, cols))
    return jnp.moveaxis(a, -2, 0).reshape(4, -1)


def kernel(x, meta_tokens, norm_mix, norm_mlp, sb_w_qkv, sb_w_o, lru_w_in, lru_conv_w, lru_conv_b, lru_w_rg, lru_b_rg, lru_w_ig, lru_b_ig, lru_lambda, lru_w_out, mlp_w_up, mlp_w_down, norm_final, loss_target, m_meta_tokens, m_norm_mix, m_norm_mlp, m_sb_w_qkv, m_sb_w_o, m_lru_w_in, m_lru_conv_w, m_lru_conv_b, m_lru_w_rg, m_lru_b_rg, m_lru_w_ig, m_lru_b_ig, m_lru_lambda, m_lru_w_out, m_mlp_w_up, m_mlp_w_down, m_norm_final, v_meta_tokens, v_norm_mix, v_norm_mlp, v_sb_w_qkv, v_sb_w_o, v_lru_w_in, v_lru_conv_w, v_lru_conv_b, v_lru_w_rg, v_lru_b_rg, v_lru_w_ig, v_lru_b_ig, v_lru_lambda, v_lru_w_out, v_mlp_w_up, v_mlp_w_down, v_norm_final):
    weights = dict(meta_tokens=meta_tokens, norm_mix=norm_mix, norm_mlp=norm_mlp, sb_w_qkv=sb_w_qkv,
                   sb_w_o=sb_w_o, lru_w_in=lru_w_in, lru_conv_w=lru_conv_w, lru_conv_b=lru_conv_b,
                   lru_w_rg=lru_w_rg, lru_b_rg=lru_b_rg, lru_w_ig=lru_w_ig, lru_b_ig=lru_b_ig,
                   lru_lambda=lru_lambda, lru_w_out=lru_w_out, mlp_w_up=mlp_w_up, mlp_w_down=mlp_w_down,
                   norm_final=norm_final)
    m_in = dict(meta_tokens=m_meta_tokens, norm_mix=m_norm_mix, norm_mlp=m_norm_mlp, sb_w_qkv=m_sb_w_qkv,
                sb_w_o=m_sb_w_o, lru_w_in=m_lru_w_in, lru_conv_w=m_lru_conv_w, lru_conv_b=m_lru_conv_b,
                lru_w_rg=m_lru_w_rg, lru_b_rg=m_lru_b_rg, lru_w_ig=m_lru_w_ig, lru_b_ig=m_lru_b_ig,
                lru_lambda=m_lru_lambda, lru_w_out=m_lru_w_out, mlp_w_up=m_mlp_w_up,
                mlp_w_down=m_mlp_w_down, norm_final=m_norm_final)
    v_in = dict(meta_tokens=v_meta_tokens, norm_mix=v_norm_mix, norm_mlp=v_norm_mlp, sb_w_qkv=v_sb_w_qkv,
                sb_w_o=v_sb_w_o, lru_w_in=v_lru_w_in, lru_conv_w=v_lru_conv_w, lru_conv_b=v_lru_conv_b,
                lru_w_rg=v_lru_w_rg, lru_b_rg=v_lru_b_rg, lru_w_ig=v_lru_w_ig, lru_b_ig=v_lru_b_ig,
                lru_lambda=v_lru_lambda, lru_w_out=v_lru_w_out, mlp_w_up=v_mlp_w_up,
                mlp_w_down=v_mlp_w_down, norm_final=v_norm_final)
    names = list(weights)

    seq, D = x.shape[1], x.shape[2]
    n_meta = meta_tokens.shape[0]
    Dq = D // 4
    T = _round_up(n_meta + seq, ATT_BLOCK)
    depth = mlp_w_up.shape[0]
    my_x, my_y, my_c = lax.axis_index("x"), lax.axis_index("y"), lax.axis_index("c")
    c_arr = jnp.reshape(my_c, (1,)).astype(jnp.int32)

    assert depth == 2

    def halves(a):
        return a.astype(BF16).reshape(2, a.shape[0] // 2, a.shape[1])

    small = [meta_tokens, lru_conv_w[0], lru_conv_b, lru_b_rg, lru_b_ig, lru_lambda]
    sparts = [_f32_as_bf16(s) for s in small]
    sizes = [p.shape[0] for p in sparts]
    total = _round_up(sum(sizes), 2 * 32 * LANE)
    sflat =jnp.concatenate(sparts + [jnp.zeros((total - sum(sizes),), BF16)]).reshape(2, -1, LANE)
    gq, gsm = _run_exchange(_GatherChips([halves(sb_w_qkv[0]), sflat]), name="gather_first")
    gather_rest = _GatherChips([halves(sb_w_o[0]), halves(lru_w_in[0]), halves(lru_w_out[0]),
                                mlp_w_up.astype(BF16), mlp_w_down.astype(BF16)])
    w_qkv = gq.reshape(4, D, 3 * Dq)
    gsm = gsm.reshape(4, total)
    offs = [sum(sizes[:k]) for k in range(len(sizes))]
    sm = [_bf16_as_f32(gsm[:, o:o + s]) for o, s in zip(offs, sizes)]
    meta_full = jnp.moveaxis(sm[0].reshape(4, n_meta, Dq), 0, 1).reshape(n_meta, D)
    conv_w = jnp.moveaxis(sm[1].reshape(4, 4, Dq), 0, 1).reshape(4, D)
    conv_b, b_rg, b_ig, lam = [s.reshape(1, D) for s in sm[2:6]]
    w_rg = lru_w_rg[0].astype(BF16)
    w_ig = lru_w_ig[0].astype(BF16)
    g_mix = [norm_mix[l].reshape(1, D) for l in range(depth)]
    g_mlp = [norm_mlp[l].reshape(1, D) for l in range(depth)]
    g_fin = norm_final.reshape(1, D)

    pad_rows = T - n_meta - seq
    h0 = jnp.concatenate([meta_full, x[0], jnp.zeros((pad_rows, D), F32)], axis=0)
    target = jnp.concatenate([jnp.zeros((n_meta, D), F32), loss_target[0], jnp.zeros((pad_rows, D), F32)], axis=0)

    hn0, qkv = _norm_mm(h0, g_mix[0], w_qkv, out_dtype=BF16, name="qkv_proj")
    att, w_sv, s_sv, go, gi, gout, w_up, w_down = _attn_fwd(qkv, name="attn_fwd", plan=gather_rest)
    w_o = go.reshape(D, D)
    w_in = gi.reshape(4, D, 2 * Dq)
    w_out = gout.reshape(D, D)
    h1 = _mm_res(att, w_o, h0, name="attn_out")
    h2, hnm0, up0 = _mlp_fwd(h1, g_mlp[0], w_up, w_down, layer=0, name="mlp0_fwd")
    hn1, gr = _norm_mm(h2, g_mix[1], w_in, out_dtype=F32, name="lru_in")
    a_t, b_t = _lru_pre(gr, conv_w, conv_b, w_rg, b_rg, w_ig, b_ig, lam, name="lru_pre")
    hs = _lru_scan(a_t, b_t, reverse=False, name="lru_scan")
    y, h3 = _lru_out(gr, hs, w_out, h2, name="lru_out")
    h4, hnm1, up1 = _mlp_fwd(h3, g_mlp[1], w_up, w_down, layer=1, name="mlp1_fwd")
    loss, dh4, dh4b, dg_fin = _loss_head(h4, g_fin, target, row_lo=n_meta, row_hi=n_meta + seq, name="loss_head")

    dup1, dh3, dh3b, dg_mlp1 = _mlp_bwd(dh4, h3, g_mlp[1], up1, w_up, w_down, layer=1, name="mlp1_bwd")
    dw_up = _mm_tn(hnm1, dup1, shards=4, relu2=False, slot=1, name="mlp1_dwup")
    dw_down = _mm_tn(up1, dh4b, shards=1, relu2=True, slot=1, row_shards=4, name="mlp1_dwdown")
    dw_out = _mm_tn(y, dh3b, shards=1, relu2=False, name="lru_dwout")
    dgr, dhy = _lru_out_bwd(gr, hs, dh3b, w_out, name="lru_out_bwd")
    lmb = _lru_scan(a_t, dhy, reverse=True, name="lru_scan_bwd")
    du, dw_rg, db_rg, dw_ig, db_ig, dlam = _lru_gate_bwd(
        gr, hs, lmb, conv_w, conv_b, w_rg, b_rg, w_ig, b_ig, lam, name="lru_gate_bwd")
    dgr, dconv_w, dconv_b = _lru_conv_bwd(gr, du, conv_w, dgr, name="lru_conv_bwd")
    dh2, dh2b, dg_mix1 = _mm_nt_normbwd(dgr, w_in, h2, g_mix[1], dh3, name="lru_in_bwd")
    dw_in = _mm_tn(hn1, dgr, shards=4, relu2=False, name="lru_dwin")
    dup0, dh1, dh1b, dg_mlp0 = _mlp_bwd(dh2, h1, g_mlp[0], up0, w_up, w_down, layer=0, name="mlp0_bwd")
    dw_up = _mm_tn(hnm0, dup0, shards=4, relu2=False, slot=0, into=dw_up, name="mlp0_dwup")
    dw_down = _mm_tn(up0, dh2b, shards=1, relu2=True, slot=0, into=dw_down, row_shards=4, name="mlp0_dwdown")
    datt = _mm_nt(dh1b, w_o, out_dtype=BF16, name="attn_out_bwd")
    dw_o = _mm_tn(att, dh1b, shards=1, relu2=False, name="attn_dwo")
    def halves_of(d, rows):
        return d.reshape(4, 2, rows // 2, d.shape[-1])

    early = [halves_of(dw_o, Dq), halves_of(dw_in, D), halves_of(dw_out, Dq), dw_up, dw_down]
    dq, dk, dv, *got_early = _attn_bwd(qkv, datt, w_sv, s_sv, name="attn_bwd", plan=_ExchangeBlocks(early))
    dqkv = jnp.concatenate([dq, dk, dv], axis=1)
    large = ["sb_w_o", "lru_w_in", "lru_w_out", "mlp_w_up", "mlp_w_down", "sb_w_qkv"]
    place = jnp.stack([2 * my_x + my_y, my_c, 4 * my_x + 2 * my_y + my_c]).astype(jnp.int32)
    mine = [_add_devices(p, o, place, name="reduce_add_" + t) for p, o, t in zip(early, got_early, large)]
    dw_qkv, *theirs = _mm_tn(hn0, dqkv, shards=4, relu2=False, name="attn_dwqkv", plan=_SendSibling(mine))
    dw_qkv = halves_of(dw_qkv, D)
    dh0, _, dg_mix0, got_qkv = _mm_nt_normbwd(dqkv, w_qkv, h0, g_mix[0], dh1, name="qkv_bwd",
                                               plan=_ExchangeBlocks([dw_qkv]))
    grad_x = dh0[n_meta:n_meta + seq][None]
    dmeta = dh0[:n_meta]

    sharded = [_by_chip_cols(dmeta, Dq), _by_chip_cols(dconv_w, Dq), dconv_b.reshape(4, Dq),
               db_rg.reshape(4, Dq), db_ig.reshape(4, Dq), dlam.reshape(4, Dq)]
    repl = [jnp.concatenate([dg_mix0, dg_mix1], axis=0).reshape(-1),
            jnp.concatenate([dg_mlp0, dg_mlp1], axis=0).reshape(-1),
            dg_fin.reshape(-1), dw_rg.reshape(-1), dw_ig.reshape(-1), loss.reshape(-1)]
    rsizes = [r.shape[0] for r in repl]
    rtotal = _round_up(sum(rsizes), 4 * 2 * 16 * LANE)
    rflat = jnp.concatenate(repl + [jnp.zeros((rtotal - sum(rsizes),), F32)]).reshape(4, rtotal // 4)
    gsizes = [s.shape[1] for s in sharded] + [rtotal // 4]
    gtotal = _round_up(sum(gsizes), 2 * 16 * LANE)
    tail = jnp.concatenate(sharded + [rflat, jnp.zeros((4, gtotal - sum(gsizes)), F32)], axis=1)
    late = [tail.reshape(4, 2, -1, LANE)]
    got_late = _run_exchange(_ExchangeBlocks(late), name="reduce_late")
    mine_late = [_add_devices(p, o, place, name="reduce_add_" + t)
                 for p, o, t in zip([dw_qkv] + late, [got_qkv] + list(got_late), ["sb_w_qkv", "tail"])]
    mine += mine_late
    theirs += _run_exchange(_SendSibling(mine_late), name="reduce_join")

    grads, delta, new_m, new_v = {}, {}, {}, {}
    for n, a, b in zip(large, mine, theirs):
        shp = weights[n].shape
        view = (2,) + a.shape
        g, d, nm, nv = _adamw_halves(weights[n].reshape(view), a, b, m_in[n].reshape(view),
                                     v_in[n].reshape(view), c_arr, name="adamw_" + n)
        grads[n], delta[n], new_m[n], new_v[n] = g.reshape(shp), d.reshape(shp), nm.reshape(shp), nv.reshape(shp)

    lo = jnp.where(my_c == 0, mine[-1], theirs[-1])
    hi = jnp.where(my_c == 0, theirs[-1], mine[-1])
    gshard = jnp.concatenate([lo, hi], axis=0).reshape(gtotal)
    goffs = [sum(gsizes[:k]) for k in range(len(gsizes))]
    gp = [gshard[o:o + s] for o, s in zip(goffs, gsizes)]
    rfull = _run_exchange(_GatherChips([gp[-1].reshape(2, -1, LANE)]), name="gather_replicated")[0].reshape(rtotal)
    roffs = [sum(rsizes[:k]) for k in range(len(rsizes))]
    rp = [rfull[o:o + s] for o, s in zip(roffs, rsizes)]
    grads.update(meta_tokens=gp[0], lru_conv_w=gp[1], lru_conv_b=gp[2], lru_b_rg=gp[3], lru_b_ig=gp[4],
                 lru_lambda=gp[5], norm_mix=rp[0], norm_mlp=rp[1], norm_final=rp[2], lru_w_rg=rp[3],
                 lru_w_ig=rp[4])
    grads = {n: grads[n].reshape(weights[n].shape) for n in names}
    rest = [n for n in names if n not in large]
    ssz = [weights[n].size for n in rest]
    small_cols = 8 * LANE
    stotal = _round_up(sum(ssz), SUBLANE * small_cols)

    def pack(src):
        return jnp.concatenate([src[n].reshape(-1) for n in rest]
                               + [jnp.ones((stotal - sum(ssz),), F32)]).reshape(-1, small_cols)

    d, nm, nv = _adamw(pack(weights), pack(grads), pack(m_in), pack(v_in), name="adamw_small")
    soffs = [sum(ssz[:k]) for k in range(len(ssz))]
    for n, o, s in zip(rest, soffs, ssz):
        shp = weights[n].shape
        delta[n] = d.reshape(-1)[o:o + s].reshape(shp)
        new_m[n] = nm.reshape(-1)[o:o + s].reshape(shp)
        new_v[n] = nv.reshape(-1)[o:o + s].reshape(shp)

    loss = rp[5][0]
    return (loss, grad_x, *[grads[n] for n in names], *[delta[n] for n in names],
            *[new_m[n] for n in names], *[new_v[n] for n in names])
```

```python
import jax
import jax.numpy as jnp
from jax import lax
from jax.experimental import pallas as pl
from jax.experimental.pallas import tpu as pltpu

F32 = jnp.float32
BF16 = jnp.bfloat16
MESH = pl.DeviceIdType.MESH

EPS = 1e-6
HEAD_DIM = 64
LANE = 128
SUBLANE = 8
LRU_C = 8.0
BF16_ROWS = 16
VMEM_LIMIT = 56 * 1024 * 1024
ROWS_WIDE = 1056
ROWS_NARROW = 528

ADAM_LR = 0.001
ADAM_B1 = 0.9
ADAM_B2 = 0.999
ADAM_EPS = 1e-08
ADAM_WD = 0.01
ADAM_STEP = 10


def _pcall(body, **kw):
    return pl.pallas_call(body, **kw)


def _params(*sem):
    return pltpu.CompilerParams(dimension_semantics=sem, vmem_limit_bytes=VMEM_LIMIT)


def _tile(n, pref, align):
    best = None
    for t in range(align, min(n, pref) + 1, align):
        if n % t == 0:
            best = t
    return n if best is None else best


def _sds(shape, dtype):
    return jax.ShapeDtypeStruct(shape, dtype)


def _rstd(x):
    return lax.rsqrt(jnp.mean(x * x, axis=-1, keepdims=True) + EPS)


def _norm_bwd(x, g, dy):
    rstd = _rstd(x)
    n = x * rstd
    dn = dy * g
    dx = rstd * (dn - n * jnp.mean(dn * n, axis=-1, keepdims=True))
    dg = jnp.sum(dy * n, axis=0, keepdims=True)
    return dx, dg


def _softplus_parts(z):
    l1p = jnp.log(1.0 + jnp.exp(-jnp.abs(z)))
    return jnp.maximum(z, 0.0) + l1p, jnp.minimum(z, 0.0) - l1p


def _sigmoid(x):
    return 1.0 / (1.0 + jnp.exp(-x))


def _gelu_parts(x):
    k = 0.7978845608028654
    inner = k * (x + 0.044715 * (x * x * x))
    t = jnp.tanh(inner)
    gelu = 0.5 * x * (1.0 + t)
    dgelu = 0.5 * (1.0 + t) + 0.5 * x * (1.0 - t * t) * (k * (1.0 + 3.0 * 0.044715 * (x * x)))
    return gelu, dgelu


def _norm_mm(h, g, w, *, out_dtype, name):
    T, D = h.shape
    S, _, n = w.shape
    tm = _tile(T, ROWS_WIDE, BF16_ROWS)
    tn = _tile(n, 768, LANE)
    nj = n // tn

    def body(h_ref, g_ref, w_ref, hn_ref, o_ref):
        @pl.when(pl.program_id(1) == 0)
        def _():
            x = h_ref[...]
            hn_ref[...] = (x * _rstd(x) * g_ref[...]).astype(BF16)

        o_ref[...] = jnp.dot(hn_ref[...], w_ref[...], preferred_element_type=F32).astype(out_dtype)

    return _pcall(
        body, name=name, grid=(T // tm, S * nj),
        in_specs=[pl.BlockSpec((tm, D), lambda i, j: (i, 0)),
                  pl.BlockSpec((1, D), lambda i, j: (0, 0)),
                  pl.BlockSpec((None, D, tn), lambda i, j: (j // nj, 0, j % nj))],
        out_specs=[pl.BlockSpec((tm, D), lambda i, j: (i, 0)),
                   pl.BlockSpec((tm, tn), lambda i, j: (i, j))],
        out_shape=[_sds((T, D), BF16), _sds((T, S * n), out_dtype)],
        compiler_params=_params("parallel", "arbitrary"),
    )(h, g, w)


def _mm_res(a, w, res, *, name):
    T, K = a.shape
    N = w.shape[1]
    tm = _tile(T, ROWS_WIDE, BF16_ROWS)

    def body(a_ref, w_ref, r_ref, o_ref):
        o_ref[...] = r_ref[...] + jnp.dot(a_ref[...], w_ref[...], preferred_element_type=F32)

    return _pcall(
        body, name=name, grid=(T // tm,),
        in_specs=[pl.BlockSpec((tm, K), lambda i: (i, 0)),
                  pl.BlockSpec((K, N), lambda i: (0, 0)),
                  pl.BlockSpec((tm, N), lambda i: (i, 0))],
        out_specs=pl.BlockSpec((tm, N), lambda i: (i, 0)),
        out_shape=_sds((T, N), F32),
        compiler_params=_params("parallel"),
    )(a, w, res)


def _mm_nt(a, w, *, out_dtype, name):
    T, N = a.shape
    K = w.shape[0]
    tm = _tile(T, ROWS_WIDE, BF16_ROWS)

    def body(a_ref, w_ref, o_ref):
        o_ref[...] = lax.dot_general(a_ref[...], w_ref[...], (((1,), (1,)), ((), ())),
                                     preferred_element_type=F32).astype(out_dtype)

    return _pcall(
        body, name=name, grid=(T // tm,),
        in_specs=[pl.BlockSpec((tm, N), lambda i: (i, 0)),
                  pl.BlockSpec((K, N), lambda i: (0, 0))],
        out_specs=pl.BlockSpec((tm, K), lambda i: (i, 0)),
        out_shape=_sds((T, K), out_dtype),
        compiler_params=_params("parallel"),
    )(a, w)


def _mm_tn(a, b, *, shards, relu2, name, slot=None, into=None, row_shards=0, out_dtype=BF16, plan=None):
    T, Ka = a.shape
    Nb = b.shape[1]
    n = Nb // shards
    tka = _tile(Ka, 512, LANE)
    tnb = _tile(n, 512, LANE)
    nj = n // tnb

    nx, before, after = _ride_along(plan, Ka // tka, shards * nj)
    n_in = 2 + (into is not None)

    def body(*refs):
        a_ref, b_ref = refs[:2]
        o_ref = refs[n_in + nx]
        ride = (refs[n_in:n_in + nx], refs[n_in + nx + 1:n_in + 2 * nx + 1], refs[n_in + 2 * nx + 1:])
        before(ride)
        av = a_ref[...]
        if relu2:
            r = jnp.maximum(av, 0)
            av = r * r
        o_ref[...] = lax.dot_general(av, b_ref[...], (((0,), (0,)), ((), ())),
                                     preferred_element_type=F32).astype(out_dtype)
        after(ride)

    in_specs = [pl.BlockSpec((T, tka), lambda i, j: (0, i)),
                pl.BlockSpec((T, tnb), lambda i, j: (0, j))]
    args = [a, b]
    aliases = {}
    if slot is None:
        out_spec = pl.BlockSpec((None, tka, tnb), lambda i, j: (j // nj, i, j % nj))
        out_shape = _sds((shards, Ka, n), out_dtype)
    else:
        if row_shards:
            ni = Ka // row_shards // tka
            out_spec = pl.BlockSpec((None, None, tka, tnb), lambda i, j: (i // ni, slot, i % ni, j))
            out_shape = _sds((row_shards, 2, Ka // row_shards, n), out_dtype)
        else:
            out_spec = pl.BlockSpec((None, None, tka, tnb), lambda i, j: (j // nj, slot, i, j % nj))
            out_shape = _sds((shards, 2, Ka, n), out_dtype)
        if into is not None:
            in_specs.append(pl.BlockSpec(memory_space=pl.ANY))
            args.append(into)
            aliases = {2: 0}
    extra = plan.inputs if plan else []
    res = _pcall(
        body, name=name, grid=(Ka // tka, shards * nj),
        in_specs=in_specs + [ANY] * nx, out_specs=[out_spec] + [ANY] * nx,
        out_shape=[out_shape] + (plan.out_shape if plan else []),
        input_output_aliases=aliases,
        scratch_shapes=plan.scratch if plan else [],
        compiler_params=_params("arbitrary", "arbitrary"),
    )(*args, *extra)
    return res if plan else res[0]


def _mm_nt_normbwd(dy, w, h, g, dres, *, name, plan=None):
    T, D = h.shape
    S, _, n = w.shape
    tm = _tile(T, ROWS_WIDE, BF16_ROWS)
    nx, before, after = _ride_along(plan, T // tm, S)

    def body(*refs):
        dy_ref, w_ref, h_ref, g_ref, dr_ref = refs[:5]
        dh_ref, dhb_ref, dg_ref = refs[5 + nx:8 + nx]
        acc_ref = refs[8 + 2 * nx]
        ride = (refs[5:5 + nx], refs[8 + nx:8 + 2 * nx], refs[9 + 2 * nx:])
        before(ride)
        i, s = pl.program_id(0), pl.program_id(1)
        part = lax.dot_general(dy_ref[...], w_ref[...], (((1,), (1,)), ((), ())),
                               preferred_element_type=F32)

        @pl.when(s == 0)
        def _():
            acc_ref[...] = part

        @pl.when(s > 0)
        def _():
            acc_ref[...] += part

        @pl.when(s == S - 1)
        def _():
            dx, dg = _norm_bwd(h_ref[...], g_ref[...], acc_ref[...])
            dh = dr_ref[...] + dx
            dh_ref[...] = dh
            dhb_ref[...] = dh.astype(BF16)

            @pl.when(i == 0)
            def _():
                dg_ref[...] = dg

            @pl.when(i > 0)
            def _():
                dg_ref[...] += dg

        after(ride)

    extra = plan.inputs if plan else []
    return _pcall(
        body, name=name, grid=(T // tm, S),
        in_specs=[pl.BlockSpec((tm, n), lambda i, s: (i, s)),
                  pl.BlockSpec((None, D, n), lambda i, s: (s, 0, 0)),
                  pl.BlockSpec((tm, D), lambda i, s: (i, 0)),
                  pl.BlockSpec((1, D), lambda i, s: (0, 0)),
                  pl.BlockSpec((tm, D), lambda i, s: (i, 0))] + [ANY] * nx,
        out_specs=[pl.BlockSpec((tm, D), lambda i, s: (i, 0)),
                   pl.BlockSpec((tm, D), lambda i, s: (i, 0)),
                   pl.BlockSpec((1, D), lambda i, s: (0, 0))] + [ANY] * nx,
        out_shape=[_sds((T, D), F32), _sds((T, D), BF16), _sds((1, D), F32)] + (plan.out_shape if plan else []),
        scratch_shapes=[pltpu.VMEM((tm, D), F32)] + (plan.scratch if plan else []),
        compiler_params=_params("arbitrary", "arbitrary"),
    )(dy, w, h, g, dres, *extra)


def _row_chains(tm):
    first = (tm // 2 + 15) // 16 * 16
    return [slice(0, first), slice(first, tm)] if 0 < first < tm else [slice(0, tm)]


def _mlp_fwd(h, g, w_up, w_down, *, layer, name):
    T, D = h.shape
    S, _, _, n = w_up.shape
    tm = _tile(T, ROWS_NARROW, BF16_ROWS)
    tf = _tile(n, 1024, LANE)
    nj = n // tf
    nf = S * nj
    chains = _row_chains(tm)

    def body(h_ref, g_ref, wu_ref, wd_ref, o_ref, hn_ref, up_ref, acc_ref):
        f = pl.program_id(1)

        @pl.when(f == 0)
        def _():
            x = h_ref[...]
            hn_ref[...] = (x * _rstd(x) * g_ref[...]).astype(BF16)

        parts = []
        for rows in chains:
            up = jnp.dot(hn_ref[rows, :], wu_ref[...], preferred_element_type=F32)
            up_ref[rows, :] = up.astype(BF16)
            r = jnp.maximum(up, 0.0)
            parts.append(jnp.dot((r * r).astype(BF16), wd_ref[...], preferred_element_type=F32))
        part = jnp.concatenate(parts, axis=0)

        @pl.when(f == 0)
        def _():
            acc_ref[...] = part

        @pl.when(f > 0)
        def _():
            acc_ref[...] += part

        @pl.when(f == nf - 1)
        def _():
            o_ref[...] = h_ref[...] + acc_ref[...]

    return _pcall(
        body, name=name, grid=(T // tm, nf),
        in_specs=[pl.BlockSpec((tm, D), lambda i, f: (i, 0)),
                  pl.BlockSpec((1, D), lambda i, f: (0, 0)),
                  pl.BlockSpec((None, None, D, tf), lambda i, f: (f // nj, layer, 0, f % nj)),
                  pl.BlockSpec((None, None, tf, D), lambda i, f: (f // nj, layer, f % nj, 0))],
        out_specs=[pl.BlockSpec((tm, D), lambda i, f: (i, 0)),
                   pl.BlockSpec((tm, D), lambda i, f: (i, 0)),
                   pl.BlockSpec((tm, tf), lambda i, f: (i, f))],
        out_shape=[_sds((T, D), F32), _sds((T, D), BF16), _sds((T, S * n), BF16)],
        scratch_shapes=[pltpu.VMEM((tm, D), F32)],
        compiler_params=_params("parallel", "arbitrary"),
    )(h, g, w_up, w_down)


def _mlp_bwd(dy, h, g, up, w_up, w_down, *, layer, name):
    T, D = h.shape
    S, _, _, n = w_up.shape
    tm = _tile(T, ROWS_NARROW, BF16_ROWS)
    tf = _tile(n, 1024, LANE)
    nj = n // tf
    nf = S * nj
    chains = _row_chains(tm)

    def body(dy_ref, h_ref, g_ref, up_ref, wu_ref, wd_ref, dup_ref, dh_ref, dhb_ref, dg_ref,
             dyb_ref, acc_ref):
        i, f = pl.program_id(0), pl.program_id(1)

        @pl.when(f == 0)
        def _():
            dyb_ref[...] = dy_ref[...].astype(BF16)

        parts = []
        for rows in chains:
            dact = lax.dot_general(dyb_ref[rows, :], wd_ref[...], (((1,), (1,)), ((), ())),
                                   preferred_element_type=F32)
            r = jnp.maximum(up_ref[rows, :].astype(F32), 0.0)
            dup = (dact * (2.0 * r)).astype(BF16)
            dup_ref[rows, :] = dup
            parts.append(lax.dot_general(dup, wu_ref[...], (((1,), (1,)), ((), ())),
                                         preferred_element_type=F32))
        part = jnp.concatenate(parts, axis=0)

        @pl.when(f == 0)
        def _():
            acc_ref[...] = part

        @pl.when(f > 0)
        def _():
            acc_ref[...] += part

        @pl.when(f == nf - 1)
        def _():
            dx, dg = _norm_bwd(h_ref[...], g_ref[...], acc_ref[...])
            dh = dy_ref[...] + dx
            dh_ref[...] = dh
            dhb_ref[...] = dh.astype(BF16)

            @pl.when(i == 0)
            def _():
                dg_ref[...] = dg

            @pl.when(i > 0)
            def _():
                dg_ref[...] += dg

    return _pcall(
        body, name=name, grid=(T // tm, nf),
        in_specs=[pl.BlockSpec((tm, D), lambda i, f: (i, 0)),
                  pl.BlockSpec((tm, D), lambda i, f: (i, 0)),
                  pl.BlockSpec((1, D), lambda i, f: (0, 0)),
                  pl.BlockSpec((tm, tf), lambda i, f: (i, f)),
                  pl.BlockSpec((None, None, D, tf), lambda i, f: (f // nj, layer, 0, f % nj)),
                  pl.BlockSpec((None, None, tf, D), lambda i, f: (f // nj, layer, f % nj, 0))],
        out_specs=[pl.BlockSpec((tm, tf), lambda i, f: (i, f)),
                   pl.BlockSpec((tm, D), lambda i, f: (i, 0)),
                   pl.BlockSpec((tm, D), lambda i, f: (i, 0)),
                   pl.BlockSpec((1, D), lambda i, f: (0, 0))],
        out_shape=[_sds((T, S * n), BF16), _sds((T, D), F32), _sds((T, D), BF16), _sds((1, D), F32)],
        scratch_shapes=[pltpu.VMEM((tm, D), BF16), pltpu.VMEM((tm, D), F32)],
        compiler_params=_params("arbitrary", "arbitrary"),
    )(dy, h, g, up, w_up, w_down)


ATT_BLOCK = 128
ATT_HEADS = 4


def _attn_tile(T):
    for w in (3 * ATT_BLOCK, 2 * ATT_BLOCK):
        if T % w == 0:
            return w
    return ATT_BLOCK


def _tri(strict_lower, value):
    B = ATT_BLOCK
    r = lax.broadcasted_iota(jnp.int32, (2 * B, B), 0)
    r = jnp.where(r >= B, r - B, r)
    c = lax.broadcasted_iota(jnp.int32, (2 * B, B), 1)
    m = (r > c) if strict_lower else (r < c)
    return jnp.where(m, value, 0.0).astype(BF16)


def _split_dot(x, tri):
    hi = lax.bitcast_convert_type(lax.bitcast_convert_type(x, jnp.uint32) & jnp.uint32(0xFFFF0000), F32)
    lo = x - hi
    return jnp.dot(jnp.concatenate([hi.astype(BF16), lo.astype(BF16)], axis=1), tri,
                   preferred_element_type=F32)


def _causal_mask(W):
    r = lax.broadcasted_iota(jnp.int32, (W, W), 0)
    c = lax.broadcasted_iota(jnp.int32, (W, W), 1)
    return c < r


def _two_heads(x):
    left = lax.broadcasted_iota(jnp.int32, x.shape, 1) < HEAD_DIM
    zero = jnp.zeros_like(x)
    return jnp.concatenate([jnp.where(left, x, zero), jnp.where(left, zero, x)], axis=0)


def _attn_scores(z, carry, tri_neg, masked):
    W = z.shape[0]
    B = ATT_BLOCK
    minus_abs = lax.bitcast_convert_type(
        lax.bitcast_convert_type(z, jnp.uint32) | jnp.uint32(0x80000000), F32)
    sp = jnp.maximum(z, 0.0) + jnp.log(1.0 + jnp.exp(minus_abs))
    logsig = z - sp
    if masked:
        causal = _causal_mask(W)
        sp = jnp.where(causal, sp, 0.0)
    afters = []
    for b in reversed(range(W // B)):
        blk = sp[:, b * B:(b + 1) * B]
        within = _split_dot(blk, tri_neg)
        afters.append(within + carry)
        carry = carry + (within[:, 0:1] - blk[:, 0:1])
    after = jnp.concatenate(afters[::-1], axis=1)
    w = jnp.exp(logsig + after)
    if masked:
        w = jnp.where(causal, w, 0.0)
    return w, jnp.exp(logsig), carry


def _ride_along(plan, npairs, nq):
    if plan is None:
        return 0, (lambda refs: None), (lambda refs: None)
    nx = plan.n

    def before(refs):
        p, i = pl.program_id(0), pl.program_id(1)

        @pl.when(jnp.logical_and(p == 0, i == 0))
        def _():
            plan.start(*refs)

        @pl.when(jnp.logical_and(p == npairs // 2, i == 0))
        def _():
            plan.mid(*refs)

    def after(refs):
        p, i = pl.program_id(0), pl.program_id(1)

        @pl.when(jnp.logical_and(p == npairs - 1, i == nq - 1))
        def _():
            plan.finish(*refs)

    return nx, before, after


def _saved_tile(i, j):
    return i * (i + 1) // 2 + j


def _attn_fwd(qkv, *, name, plan=None):
    T = qkv.shape[0]
    D = qkv.shape[1] // 3
    W = _attn_tile(T)
    H = 2 * ATT_HEADS if D % (2 * ATT_HEADS * HEAD_DIM) == 0 else ATT_HEADS
    lanes = H * HEAD_DIM
    ngroups = D // lanes
    nq = T // W
    ntri = nq * (nq + 1) // 2
    scale = HEAD_DIM ** -0.5
    nx, before, after = _ride_along(plan, ngroups, nq)

    def body(*refs):
        q_ref, k_ref, v_ref = refs[:3]
        o_ref, wsv_ref, ssv_ref = refs[3 + nx:6 + nx]
        stage_w, stage_s, stage_sems = refs[6 + 2 * nx:9 + 2 * nx]
        ride = (refs[3:3 + nx], refs[6 + nx:6 + 2 * nx], refs[9 + 2 * nx:])
        before(ride)
        p, i = pl.program_id(0), pl.program_id(1)
        tri = _tri(True, -1.0)
        pairs = [slice(pp * LANE, (pp + 1) * LANE) for pp in range(H // 2)]
        qs = [q_ref[:, cols] * scale for cols in pairs]

        def save(slot, j):
            dst = _saved_tile(i, j)
            return [pltpu.make_async_copy(stage.at[slot], sv.at[pl.ds(p * H, H), dst], stage_sems.at[slot, a])
                    for a, (sv, stage) in enumerate(((wsv_ref, stage_w), (ssv_ref, stage_s)))]

        def tile(t, state, masked):
            j = i - t
            rows = pl.ds(pl.multiple_of(j * W, W), W)
            slot = t % 2
            if not masked:
                @pl.when(t >= 2)
                def _():
                    for cp in save(slot, j):
                        cp.wait()
            out = []
            for pp, (cols, q, (carries, acc)) in enumerate(zip(pairs, qs, state)):
                z = lax.dot_general(q, _two_heads(k_ref[rows, cols]), (((1,), (1,)), ((), ())),
                                    preferred_element_type=F32)
                wbs, new_carries = [], []
                for e, carry in enumerate(carries):
                    w, sig, carry = _attn_scores(z[:, e * W:(e + 1) * W], carry, tri, masked)
                    wb = w.astype(BF16)
                    stage_w[slot, 2 * pp + e] = wb
                    stage_s[slot, 2 * pp + e] = sig.astype(BF16)
                    wbs.append(wb)
                    new_carries.append(carry)
                acc = acc + jnp.dot(jnp.concatenate(wbs, axis=1), _two_heads(v_ref[rows, cols]),
                                    preferred_element_type=F32)
                out.append((tuple(new_carries), acc))
            for cp in save(slot, j):
                cp.start()
            return tuple(out)

        zero = ((jnp.zeros((W, 1), F32),) * 2, jnp.zeros((W, LANE), F32))
        state = tile(0, (zero,) * (H // 2), True)
        state = lax.fori_loop(1, i + 1, lambda t, st: tile(t, st, False), state)
        for cols, (_, acc) in zip(pairs, state):
            o_ref[:, cols] = acc.astype(BF16)
        for cp in save(i % 2, 0):
            cp.wait()

        @pl.when(i >= 1)
        def _():
            for cp in save((i + 1) % 2, 0):
                cp.wait()

        after(ride)

    extra = plan.inputs if plan else []
    saved = _sds((D // HEAD_DIM, ntri, W, W), BF16)
    return _pcall(
        body, name=name, grid=(ngroups, nq),
        in_specs=[pl.BlockSpec((W, lanes), lambda p, i: (i, p)),
                  pl.BlockSpec((T, lanes), lambda p, i: (0, ngroups + p)),
                  pl.BlockSpec((T, lanes), lambda p, i: (0, 2 * ngroups + p))] + [ANY] * nx,
        out_specs=[pl.BlockSpec((W, lanes), lambda p, i: (i, p)), ANY, ANY] + [ANY] * nx,
        out_shape=[_sds((T, D), BF16), saved, saved] + (plan.out_shape if plan else []),
        scratch_shapes=[pltpu.VMEM((2, H, W, W), BF16), pltpu.VMEM((2, H, W, W), BF16),
                        pltpu.SemaphoreType.DMA((2, 2))] + (plan.scratch if plan else []),
        compiler_params=_params("arbitrary", "arbitrary"),
    )(qkv, qkv, qkv, *extra)


def _attn_bwd(qkv, do, w_sv, s_sv, *, name, plan=None):
    T = qkv.shape[0]
    D = qkv.shape[1] // 3
    B = ATT_BLOCK
    W = _attn_tile(T)
    H = ATT_HEADS
    lanes = H * HEAD_DIM
    ngroups = D // lanes
    nq = T // W
    scale = HEAD_DIM ** -0.5
    nx, before, after = _ride_along(plan, ngroups, nq)

    def body(*refs):
        q_ref, k_ref, v_ref, do_ref, wsv_ref, ssv_ref = refs[:6]
        dq_ref, dk_ref, dv_ref = refs[6 + nx:9 + nx]
        dk_acc, dv_acc, stage_w, stage_s, stage_sems = refs[9 + 2 * nx:14 + 2 * nx]
        ride = (refs[6:6 + nx], refs[9 + nx:9 + 2 * nx], refs[14 + 2 * nx:])
        before(ride)
        p, i = pl.program_id(0), pl.program_id(1)

        @pl.when(i == 0)
        def _():
            dk_acc[...] = jnp.zeros_like(dk_acc)
            dv_acc[...] = jnp.zeros_like(dv_acc)

        def fetch(slot, j):
            src = _saved_tile(i, j)
            return [pltpu.make_async_copy(sv.at[pl.ds(p * H, H), src], stage.at[slot], stage_sems.at[slot, a])
                    for a, (sv, stage) in enumerate(((wsv_ref, stage_w), (ssv_ref, stage_s)))]

        tri_before = _tri(False, 1.0)
        pairs = [slice(pp * LANE, (pp + 1) * LANE) for pp in range(H // 2)]
        q_two = [_two_heads(q_ref[:, cols] * scale) for cols in pairs]
        do_pair = [do_ref[:, cols] for cols in pairs]
        do_two = [_two_heads(d) for d in do_pair]

        def grad(j, state, masked):
            rows = pl.ds(pl.multiple_of(j * W, W), W)
            slot = j % 2
            for cp in fetch(slot, j):
                cp.wait()
            if not masked:
                for cp in fetch(1 - slot, j + 1):
                    cp.start()
            out = []
            for pp, (cols, (gsums, dq)) in enumerate(zip(pairs, state)):
                dw = lax.dot_general(do_pair[pp], _two_heads(v_ref[rows, cols]), (((1,), (1,)), ((), ())),
                                     preferred_element_type=F32)
                dzs, wbs, new_gsums = [], [], []
                for e, gsum in enumerate(gsums):
                    wb = stage_w[slot, 2 * pp + e]
                    sig = stage_s[slot, 2 * pp + e].astype(F32)
                    g = dw[:, e * W:(e + 1) * W] * wb.astype(F32)
                    befores = []
                    for b in range(W // B):
                        blk = g[:, b * B:(b + 1) * B]
                        within = _split_dot(blk, tri_before)
                        befores.append(within + gsum)
                        gsum = gsum + (within[:, B - 1:B] + blk[:, B - 1:B])
                    dz = g - sig * (g + jnp.concatenate(befores, axis=1))
                    if masked:
                        dz = jnp.where(_causal_mask(W), dz, 0.0)
                    dzs.append(dz.astype(BF16))
                    wbs.append(wb)
                    new_gsums.append(gsum)
                dq = dq + jnp.dot(jnp.concatenate(dzs, axis=1), _two_heads(k_ref[rows, cols] * scale),
                                  preferred_element_type=F32)
                dk_acc[rows, cols] += lax.dot_general(jnp.concatenate(dzs, axis=0), q_two[pp],
                                                      (((0,), (0,)), ((), ())), preferred_element_type=F32)
                dv_acc[rows, cols] += lax.dot_general(jnp.concatenate(wbs, axis=0), do_two[pp],
                                                      (((0,), (0,)), ((), ())), preferred_element_type=F32)
                out.append((tuple(new_gsums), dq))
            return tuple(out)

        for cp in fetch(0, 0):
            cp.start()
        zero = ((jnp.zeros((W, 1), F32),) * 2, jnp.zeros((W, LANE), F32))
        state = lax.fori_loop(0, i, lambda j, st: grad(j, st, False), (zero,) * (H // 2))
        state = grad(i, state, True)
        for cols, (_, dq) in zip(pairs, state):
            dq_ref[:, cols] = dq.astype(BF16)

        @pl.when(i == nq - 1)
        def _():
            dk_ref[...] = dk_acc[...].astype(BF16)
            dv_ref[...] = dv_acc[...].astype(BF16)

        after(ride)

    extra = plan.inputs if plan else []
    return _pcall(
        body, name=name, grid=(ngroups, nq),
        in_specs=[pl.BlockSpec((W, lanes), lambda p, i: (i, p)),
                  pl.BlockSpec((T, lanes), lambda p, i: (0, ngroups + p)),
                  pl.BlockSpec((T, lanes), lambda p, i: (0, 2 * ngroups + p)),
                  pl.BlockSpec((W, lanes), lambda p, i: (i, p)), ANY, ANY] + [ANY] * nx,
        out_specs=[pl.BlockSpec((W, lanes), lambda p, i: (i, p)),
                   pl.BlockSpec((T, lanes), lambda p, i: (0, p)),
                   pl.BlockSpec((T, lanes), lambda p, i: (0, p))] + [ANY] * nx,
        out_shape=[_sds((T, D), BF16)] * 3 + (plan.out_shape if plan else []),
        scratch_shapes=[pltpu.VMEM((T, lanes), F32), pltpu.VMEM((T, lanes), F32),
                        pltpu.VMEM((2, H, W, W), BF16), pltpu.VMEM((2, H, W, W), BF16),
                        pltpu.SemaphoreType.DMA((2, 2))]
        + (plan.scratch if plan else []),
        compiler_params=_params("arbitrary", "arbitrary"),
    )(qkv, qkv, qkv, do, w_sv, s_sv, *extra)


HALO = SUBLANE


def _lru_gates(u, w_rg, b_rg, w_ig, b_ig, lam):
    nb = w_rg.shape[0]
    pre_r, pre_i = [], []
    for n in range(nb):
        ub = u[:, n * LANE:(n + 1) * LANE].astype(BF16)
        pre_r.append(jnp.dot(ub, w_rg[n], preferred_element_type=F32))
        pre_i.append(jnp.dot(ub, w_ig[n], preferred_element_type=F32))
    r = _sigmoid(jnp.concatenate(pre_r, axis=1) + b_rg)
    i = _sigmoid(jnp.concatenate(pre_i, axis=1) + b_ig)
    c = -LRU_C * _softplus_parts(-lam)[0]
    log_a = c * r
    a = jnp.exp(log_a)
    x2 = 2.0 * log_a
    em1 = jnp.where(jnp.abs(x2) < 1e-2, x2 * (1.0 + x2 * (0.5 + x2 * (1.0 / 6.0))), jnp.exp(x2) - 1.0)
    mult = jnp.sqrt(-em1)
    return r, i, a, mult, c


def _conv_rows(buf_ref, tt, conv_w, conv_b):
    u = conv_b
    for j in range(4):
        u = u + buf_ref[pl.ds(HALO - 3 + j, tt), :] * conv_w[j:j + 1, :]
    return u


def _fill_with_halo(buf_ref, prev_ref, cur_ref, first):
    tt = cur_ref.shape[0]
    buf_ref[pl.ds(0, HALO), :] = jnp.where(first, 0.0, prev_ref[...])
    buf_ref[pl.ds(HALO, tt), :] = cur_ref[...]


def _lru_time_tile(T):
    return _tile(T, 256, SUBLANE)


def _lru_pre(gr, conv_w, conv_b, w_rg, b_rg, w_ig, b_ig, lam, *, name):
    T = gr.shape[0]
    D = gr.shape[1] // 2
    tt = _lru_time_tile(T)
    hb = tt // HALO

    def body(x_ref, xp_ref, cw_ref, cb_ref, wr_ref, br_ref, wi_ref, bi_ref, lam_ref, a_ref, b_ref, buf):
        _fill_with_halo(buf, xp_ref, x_ref, pl.program_id(0) == 0)
        u = _conv_rows(buf, tt, cw_ref[...], cb_ref[...])
        _, i, a, mult, _ = _lru_gates(u, wr_ref, br_ref[...], wi_ref, bi_ref[...], lam_ref[...])
        a_ref[...] = a
        b_ref[...] = mult * (i * u)

    vec = pl.BlockSpec((1, D), lambda t: (0, 0))
    mat = pl.BlockSpec(w_rg.shape, lambda t: (0, 0, 0))
    return _pcall(
        body, name=name, grid=(T // tt,),
        in_specs=[pl.BlockSpec((tt, D), lambda t: (t, 1)),
                  pl.BlockSpec((HALO, D), lambda t: (jnp.maximum(t * hb - 1, 0), 1)),
                  pl.BlockSpec((4, D), lambda t: (0, 0)), vec, mat, vec, mat, vec, vec],
        out_specs=[pl.BlockSpec((tt, D), lambda t: (t, 0))] * 2,
        out_shape=[_sds((T, D), F32)] * 2,
        scratch_shapes=[pltpu.VMEM((tt + HALO, D), F32)],
        compiler_params=_params("parallel"),
    )(gr, gr, conv_w, conv_b, w_rg, b_rg, w_ig, b_ig, lam)


def _lru_scan(a, b, *, reverse, name):
    T, D = a.shape
    R = SUBLANE
    ts = _tile(T, 528, R)
    nt = T // ts

    def body(a_ref, b_ref, o_ref, carry):
        @pl.when(pl.program_id(0) == 0)
        def _():
            carry[...] = jnp.zeros_like(carry)

        rowid = lax.broadcasted_iota(jnp.int32, (R, D), 0)

        def chunk(k, run):
            if reverse:
                k = ts // R - 1 - k
            rows = pl.ds(pl.multiple_of(k * R, R), R)
            at, bt = a_ref[rows, :], b_ref[rows, :]
            out = jnp.zeros((R, D), F32)
            for r in (range(R - 1, -1, -1) if reverse else range(R)):
                if reverse:
                    cand = bt + run
                    nxt = at * cand
                else:
                    cand = at * run + bt
                    nxt = cand
                out = jnp.where(rowid == r, cand, out)
                run = jnp.broadcast_to(nxt[r:r + 1, :], (R, D))
            o_ref[rows, :] = out
            return run

        carry[...] = lax.fori_loop(0, ts // R, chunk, carry[...])

    if reverse:
        spec = pl.BlockSpec((ts, D), lambda t: (nt - 1 - t, 0))
    else:
        spec = pl.BlockSpec((ts, D), lambda t: (t, 0))
    return _pcall(
        body, name=name, grid=(nt,),
        in_specs=[spec, spec], out_specs=spec,
        out_shape=_sds((T, D), F32),
        scratch_shapes=[pltpu.VMEM((R, D), F32)],
        compiler_params=_params("arbitrary"),
    )(a, b)


def _lru_out(gr, hs, w, res, *, name):
    T, D = hs.shape
    tt = _tile(T, ROWS_NARROW, BF16_ROWS)

    def body(g_ref, h_ref, w_ref, r_ref, y_ref, o_ref):
        y = (h_ref[...] * _gelu_parts(g_ref[...])[0]).astype(BF16)
        y_ref[...] = y
        o_ref[...] = r_ref[...] + jnp.dot(y, w_ref[...], preferred_element_type=F32)

    blk = pl.BlockSpec((tt, D), lambda t: (t, 0))
    return _pcall(
        body, name=name, grid=(T // tt,),
        in_specs=[blk, blk, pl.BlockSpec((D, D), lambda t: (0, 0)), blk],
        out_specs=[blk, blk],
        out_shape=[_sds((T, D), BF16), _sds((T, D), F32)],
        compiler_params=_params("parallel"),
    )(gr, hs, w, res)


def _lru_out_bwd(gr, hs, dout, w, *, name):
    T, D = hs.shape
    tt = _tile(T, ROWS_NARROW, BF16_ROWS)

    def body(g_ref, h_ref, d_ref, w_ref, dg_ref, dh_ref):
        dy = lax.dot_general(d_ref[...], w_ref[...], (((1,), (1,)), ((), ())), preferred_element_type=F32)
        gelu, dgelu = _gelu_parts(g_ref[...])
        dg_ref[...] = (dy * h_ref[...] * dgelu).astype(BF16)
        dh_ref[...] = dy * gelu

    spec = pl.BlockSpec((tt, D), lambda t: (t, 0))
    return _pcall(
        body, name=name, grid=(T // tt,),
        in_specs=[spec, spec, spec, pl.BlockSpec((D, D), lambda t: (0, 0))], out_specs=[spec, spec],
        out_shape=[_sds((T, 2 * D), BF16), _sds((T, D), F32)],
        compiler_params=_params("parallel"),
    )(gr, hs, dout, w)


def _lru_gate_bwd(gr, hs, lmb, conv_w, conv_b, w_rg, b_rg, w_ig, b_ig, lam, *, name):
    T, D = hs.shape
    nb = w_rg.shape[0]
    tt = _lru_time_tile(T)
    hb = tt // HALO
    nt = T // tt

    def body(x_ref, xp_ref, h_ref, hp_ref, l_ref, cw_ref, cb_ref, wr_ref, br_ref, wi_ref, bi_ref, lam_ref,
             du_ref, dwr_ref, dbr_ref, dwi_ref, dbi_ref, dlam_ref, xbuf, hbuf):
        t = pl.program_id(0)
        first = t == 0
        _fill_with_halo(xbuf, xp_ref, x_ref, first)
        _fill_with_halo(hbuf, hp_ref, h_ref, first)
        u = _conv_rows(xbuf, tt, cw_ref[...], cb_ref[...])
        lam_v = lam_ref[...]
        r, i, a, mult, c = _lru_gates(u, wr_ref, br_ref[...], wi_ref, bi_ref[...], lam_v)
        l = l_ref[...]
        h_prev = hbuf[pl.ds(HALO - 1, tt), :]
        dlog_a = l * h_prev * a - l * (i * u) * (a * a) / mult
        d_iu = l * mult
        dpre_r = (dlog_a * c) * (r * (1.0 - r))
        dpre_i = (d_iu * u) * (i * (1.0 - i))
        dpr_b = dpre_r.astype(BF16)
        dpi_b = dpre_i.astype(BF16)
        du_parts, dwr, dwi = [], [], []
        for n in range(nb):
            cs = slice(n * LANE, (n + 1) * LANE)
            ub = u[:, cs].astype(BF16)
            du_parts.append(
                lax.dot_general(dpr_b[:, cs], wr_ref[n], (((1,), (1,)), ((), ())), preferred_element_type=F32)
                + lax.dot_general(dpi_b[:, cs], wi_ref[n], (((1,), (1,)), ((), ())), preferred_element_type=F32))
            dwr.append(lax.dot_general(ub, dpr_b[:, cs], (((0,), (0,)), ((), ())), preferred_element_type=F32))
            dwi.append(lax.dot_general(ub, dpi_b[:, cs], (((0,), (0,)), ((), ())), preferred_element_type=F32))
        du_ref[...] = d_iu * i + jnp.concatenate(du_parts, axis=1)
        dbr = jnp.sum(dpre_r, axis=0, keepdims=True)
        dbi = jnp.sum(dpre_i, axis=0, keepdims=True)
        dc = jnp.sum(dlog_a * r, axis=0, keepdims=True)

        @pl.when(first)
        def _():
            for n in range(nb):
                dwr_ref[n] = dwr[n]
                dwi_ref[n] = dwi[n]
            dbr_ref[...] = dbr
            dbi_ref[...] = dbi
            dlam_ref[...] = dc

        @pl.when(t > 0)
        def _():
            for n in range(nb):
                dwr_ref[n] += dwr[n]
                dwi_ref[n] += dwi[n]
            dbr_ref[...] += dbr
            dbi_ref[...] += dbi
            dlam_ref[...] += dc

        @pl.when(t == nt - 1)
        def _():
            dlam_ref[...] = dlam_ref[...] * (LRU_C * _sigmoid(-lam_v))

    vec = pl.BlockSpec((1, D), lambda t: (0, 0))
    mat = pl.BlockSpec(w_rg.shape, lambda t: (0, 0, 0))
    blk = pl.BlockSpec((tt, D), lambda t: (t, 0))
    prev = pl.BlockSpec((HALO, D), lambda t: (jnp.maximum(t * hb - 1, 0), 0))
    return _pcall(
        body, name=name, grid=(nt,),
        in_specs=[pl.BlockSpec((tt, D), lambda t: (t, 1)),
                  pl.BlockSpec((HALO, D), lambda t: (jnp.maximum(t * hb - 1, 0), 1)),
                  blk, prev, blk,
                  pl.BlockSpec((4, D), lambda t: (0, 0)), vec, mat, vec, mat, vec, vec],
        out_specs=[blk, mat, vec, mat, vec, vec],
        out_shape=[_sds((T, D), F32), _sds(w_rg.shape, F32), _sds((1, D), F32),
                   _sds(w_rg.shape, F32), _sds((1, D), F32), _sds((1, D), F32)],
        scratch_shapes=[pltpu.VMEM((tt + HALO, D), F32), pltpu.VMEM((tt + HALO, D), F32)],
        compiler_params=_params("arbitrary"),
    )(gr, gr, hs, hs, lmb, conv_w, conv_b, w_rg, b_rg, w_ig, b_ig, lam)


def _lru_conv_bwd(gr, du, conv_w, dgr, *, name):
    T, D = du.shape
    tt = _lru_time_tile(T)
    hb = tt // HALO
    nt = T // tt

    def body(x_ref, xp_ref, du_ref, dun_ref, cw_ref, _, dx_ref, dcw_ref, dcb_ref, xbuf, dbuf):
        t = pl.program_id(0)
        _fill_with_halo(xbuf, xp_ref, x_ref, t == 0)
        du = du_ref[...]
        dbuf[pl.ds(0, tt), :] = du
        dbuf[pl.ds(tt, HALO), :] = jnp.where(t == nt - 1, 0.0, dun_ref[...])
        cw = cw_ref[...]
        dx = jnp.zeros((tt, D), F32)
        dcw = []
        for j in range(4):
            dx = dx + dbuf[pl.ds(3 - j, tt), :] * cw[j:j + 1, :]
            dcw.append(jnp.sum(du * xbuf[pl.ds(HALO - 3 + j, tt), :], axis=0, keepdims=True))
        dx_ref[...] = dx.astype(BF16)
        dcw = jnp.concatenate(dcw, axis=0)
        dcb = jnp.sum(du, axis=0, keepdims=True)

        @pl.when(t == 0)
        def _():
            dcw_ref[...] = dcw
            dcb_ref[...] = dcb

        @pl.when(t > 0)
        def _():
            dcw_ref[...] += dcw
            dcb_ref[...] += dcb

    blk = pl.BlockSpec((tt, D), lambda t: (t, 0))
    return _pcall(
        body, name=name, grid=(nt,),
        in_specs=[pl.BlockSpec((tt, D), lambda t: (t, 1)),
                  pl.BlockSpec((HALO, D), lambda t: (jnp.maximum(t * hb - 1, 0), 1)),
                  blk,
                  pl.BlockSpec((HALO, D), lambda t: (jnp.minimum((t + 1) * hb, T // HALO - 1), 0)),
                  pl.BlockSpec((4, D), lambda t: (0, 0)), ANY],
        out_specs=[pl.BlockSpec((tt, D), lambda t: (t, 1)),
                   pl.BlockSpec((4, D), lambda t: (0, 0)), pl.BlockSpec((1, D), lambda t: (0, 0))],
        out_shape=[_sds((T, 2 * D), BF16), _sds((4, D), F32), _sds((1, D), F32)],
        input_output_aliases={5: 0},
        scratch_shapes=[pltpu.VMEM((tt + HALO, D), F32), pltpu.VMEM((tt + HALO, D), F32)],
        compiler_params=_params("arbitrary"),
    )(gr, gr, du, du, conv_w, dgr)


def _loss_head(h, g, target, *, row_lo, row_hi, name):
    T, D = h.shape
    tm = _tile(T, ROWS_WIDE, BF16_ROWS)

    def body(h_ref, g_ref, t_ref, loss_ref, dh_ref, dhb_ref, dg_ref):
        i = pl.program_id(0)
        x = h_ref[...]
        g = g_ref[...]
        row = i * tm + lax.broadcasted_iota(jnp.int32, (tm, 1), 0)
        valid = jnp.logical_and(row >= row_lo, row < row_hi)
        rstd = _rstd(x)
        n = x * rstd
        err = jnp.where(valid, n * g - t_ref[...], 0.0)
        part = (0.5 / D) * jnp.sum(jnp.sum(err * err, axis=1, keepdims=True), axis=0, keepdims=True)
        dy = err * (1.0 / D)
        dn = dy * g
        dh = rstd * (dn - n * jnp.mean(dn * n, axis=-1, keepdims=True))
        dh_ref[...] = dh
        dhb_ref[...] = dh.astype(BF16)
        dg = jnp.sum(dy * n, axis=0, keepdims=True)

        @pl.when(i == 0)
        def _():
            loss_ref[...] = part
            dg_ref[...] = dg

        @pl.when(i > 0)
        def _():
            loss_ref[...] += part
            dg_ref[...] += dg

    blk = pl.BlockSpec((tm, D), lambda i: (i, 0))
    vec = pl.BlockSpec((1, D), lambda i: (0, 0))
    return _pcall(
        body, name=name, grid=(T // tm,),
        in_specs=[blk, vec, blk],
        out_specs=[pl.BlockSpec((1, 1), lambda i: (0, 0)), blk, blk, vec],
        out_shape=[_sds((1, 1), F32), _sds((T, D), F32), _sds((T, D), BF16), _sds((1, D), F32)],
        compiler_params=_params("arbitrary"),
    )(h, g, target)


def _adamw_math(w, g, m, v):
    c1 = 1.0 / (1.0 - ADAM_B1 ** ADAM_STEP)
    c2 = 1.0 / (1.0 - ADAM_B2 ** ADAM_STEP)
    m = ADAM_B1 * m + (1.0 - ADAM_B1) * g
    v = ADAM_B2 * v + (1.0 - ADAM_B2) * (g * g)
    delta = -ADAM_LR * ((m * c1) / (jnp.sqrt(v * c2) + ADAM_EPS) + ADAM_WD * w)
    return delta, m, v


def _adamw_halves(w, mine, theirs, m, v, c, *, name, plan=None):
    _, R, C = w.shape
    tr = _tile(R, 256, SUBLANE)
    nx, before, after = _ride_along(plan, 2, R // tr)

    def body(c_ref, *refs):
        w_ref, a_ref, b_ref, m_ref, v_ref = refs[:5]
        g_ref, d_ref, nm_ref, nv_ref = refs[5 + nx:9 + nx]
        ride = (refs[5:5 + nx], refs[9 + nx:9 + 2 * nx], refs[9 + 2 * nx:])
        before(ride)
        g = jnp.where(pl.program_id(0) == c_ref[0], a_ref[...], b_ref[...])
        g_ref[...] = g
        d_ref[...], nm_ref[...], nv_ref[...] = _adamw_math(w_ref[...], g, m_ref[...], v_ref[...])
        after(ride)

    full = pl.BlockSpec((None, tr, C), lambda h, i, c_ref: (h, i, 0))
    half = pl.BlockSpec((tr, C), lambda h, i, c_ref: (i, 0))
    extra = plan.inputs if plan else []
    return _pcall(
        body, name=name,
        grid_spec=pltpu.PrefetchScalarGridSpec(
            num_scalar_prefetch=1, grid=(2, R // tr),
            in_specs=[full, half, half, full, full] + [ANY] * nx, out_specs=[full] * 4 + [ANY] * nx,
            scratch_shapes=plan.scratch if plan else []),
        out_shape=[_sds((2, R, C), F32)] * 4 + (plan.out_shape if plan else []),
        compiler_params=_params("arbitrary", "arbitrary"),
    )(c, w, mine, theirs, m, v, *extra)


def _adamw(w, g, m, v, *, name):
    R, C = w.shape
    tr = _tile(R, 512, SUBLANE)

    def body(w_ref, g_ref, m_ref, v_ref, d_ref, nm_ref, nv_ref):
        d_ref[...], nm_ref[...], nv_ref[...] = _adamw_math(w_ref[...], g_ref[...], m_ref[...], v_ref[...])

    blk = pl.BlockSpec((tr, C), lambda i: (i, 0))
    return _pcall(
        body, name=name, grid=(R // tr,),
        in_specs=[blk] * 4, out_specs=[blk] * 3,
        out_shape=[_sds((R, C), F32)] * 3,
        compiler_params=_params("parallel"),
    )(w, g, m, v)


ANY = pl.BlockSpec(memory_space=pl.ANY)


def _place():
    x, y, c = lax.axis_index("x"), lax.axis_index("y"), lax.axis_index("c")
    chips = [(1 - x, y), (x, 1 - y), (1 - x, 1 - y)]
    return x, y, c, chips


LOCAL_PIECES = 4


class _GatherChips:
    def __init__(self, vs):
        self.inputs = list(vs)
        n = self.n = len(vs)
        self.out_shape = [_sds((4,) + v.shape, v.dtype) for v in vs]
        self.scratch = [pltpu.SemaphoreType.DMA((6 * n,)), pltpu.SemaphoreType.DMA((6 * n,)),
                        pltpu.SemaphoreType.DMA((LOCAL_PIECES * n,))]

    def _copies(self, v_refs, o_refs, sems):
        send_sems, recv_sems, local_sems = sems
        x, y, c, chips = _place()
        me = 2 * x + y

        def copy(a, k, block, half, to, src=None):
            dst = o_refs[a].at[block, half]
            return pltpu.make_async_remote_copy(
                src_ref=dst if src is None else src, dst_ref=dst,
                send_sem=send_sems.at[6 * a + k], recv_sem=recv_sems.at[6 * a + k],
                device_id=to, device_id_type=MESH)

        ks = [(a, k, cx, cy) for a in range(self.n) for k, (cx, cy) in enumerate(chips)]

        def local():
            out = []
            for a in range(self.n):
                rows = self.inputs[a].shape[1] // (LOCAL_PIECES // 2)
                for p in range(LOCAL_PIECES):
                    h, r0 = p % 2, (p // 2) * rows
                    out.append(pltpu.make_async_copy(
                        v_refs[a].at[h, pl.ds(r0, rows)], o_refs[a].at[me, h, pl.ds(r0, rows)],
                        local_sems.at[LOCAL_PIECES * a + p]))
            return out

        return dict(
            first=lambda: [copy(a, k, me, c, (cx, cy, c), src=v_refs[a].at[c]) for a, k, cx, cy in ks],
            landed=lambda: [copy(a, k, 2 * cx + cy, c, (x, y, c)) for a, k, cx, cy in ks],
            passed=lambda: [copy(a, 3 + k, 2 * cx + cy, c, (x, y, 1 - c)) for a, k, cx, cy in ks],
            final=lambda: [copy(a, 3 + k, 2 * cx + cy, 1 - c, (x, y, c)) for a, k, cx, cy in ks],
            local=local)

    def start(self, v_refs, o_refs, sems):
        cps = self._copies(v_refs, o_refs, sems)
        for cp in cps["first"]() + cps["local"]():
            cp.start()

    def mid(self, v_refs, o_refs, sems):
        cps = self._copies(v_refs, o_refs, sems)
        for got, fwd in zip(cps["landed"](), cps["passed"]()):
            got.wait_recv()
            fwd.start()

    def finish(self, v_refs, o_refs, sems):
        cps = self._copies(v_refs, o_refs, sems)
        for cp in cps["final"]():
            cp.wait_recv()
        for cp in cps["first"]() + cps["passed"]():
            cp.wait_send()
        for cp in cps["local"]():
            cp.wait()


class _ExchangeBlocks:
    def __init__(self, ps):
        self.inputs = list(ps)
        n = self.n = len(ps)
        self.out_shape = [_sds((8,) + p.shape[2:], p.dtype) for p in ps]
        self.scratch = [pltpu.SemaphoreType.DMA((7 * n,)), pltpu.SemaphoreType.DMA((7 * n,))]

    def _copies(self, p_refs, o_refs, sems, incoming):
        send_sems, recv_sems = sems
        x, y, c, _ = _place()
        me = 4 * x + 2 * y + c
        out = []
        for a in range(self.n):
            for k in range(1, 8):
                px, py, pc = x ^ (k >> 2), y ^ ((k >> 1) & 1), c ^ (k & 1)
                out.append(pltpu.make_async_remote_copy(
                    src_ref=p_refs[a].at[2 * px + py, pc],
                    dst_ref=o_refs[a].at[4 * px + 2 * py + pc if incoming else me],
                    send_sem=send_sems.at[7 * a + k - 1], recv_sem=recv_sems.at[7 * a + k - 1],
                    device_id=(x, y, c) if incoming else (px, py, pc), device_id_type=MESH))
        return out

    def start(self, p_refs, o_refs, sems):
        for cp in self._copies(p_refs, o_refs, sems, False):
            cp.start()

    def mid(self, p_refs, o_refs, sems):
        pass

    def finish(self, p_refs, o_refs, sems):
        for cp in self._copies(p_refs, o_refs, sems, True):
            cp.wait_recv()
        for cp in self._copies(p_refs, o_refs, sems, False):
            cp.wait_send()


def _run_exchange(plan, *, name):
    n = plan.n

    def body(*refs):
        args = (refs[:n], refs[n:2 * n], refs[2 * n:])
        plan.start(*args)
        plan.mid(*args)
        plan.finish(*args)

    return _pcall(
        body, name=name, in_specs=[ANY] * n, out_specs=[ANY] * n,
        out_shape=plan.out_shape, scratch_shapes=plan.scratch,
    )(*plan.inputs)


class _SendSibling:
    def __init__(self, rs):
        self.inputs = list(rs)
        n = self.n = len(rs)
        self.out_shape = [_sds(r.shape, r.dtype) for r in rs]
        self.scratch = [pltpu.SemaphoreType.DMA((n,)), pltpu.SemaphoreType.DMA((n,))]

    def _copies(self, r_refs, o_refs, sems):
        send_sems, recv_sems = sems
        x, y, c, _ = _place()
        return [pltpu.make_async_remote_copy(
            src_ref=r_refs[a], dst_ref=o_refs[a], send_sem=send_sems.at[a], recv_sem=recv_sems.at[a],
            device_id=(x, y, 1 - c), device_id_type=MESH) for a in range(self.n)]

    def start(self, r_refs, o_refs, sems):
        for cp in self._copies(r_refs, o_refs, sems):
            cp.start()

    def mid(self, r_refs, o_refs, sems):
        pass

    def finish(self, r_refs, o_refs, sems):
        for cp in self._copies(r_refs, o_refs, sems):
            cp.wait()


def _add_devices(p, got, place, *, name):
    _, _, R, C = p.shape
    tr = _tile(R, 256, 16)

    def body(place_ref, p_ref, o_ref, out_ref):
        me = place_ref[2]
        own = p_ref[...].astype(F32)
        acc = jnp.where(me == 0, own, o_ref[0].astype(F32))
        for d in range(1, 8):
            acc = acc + jnp.where(me == d, own, o_ref[d].astype(F32))
        out_ref[...] = acc

    return _pcall(
        body, name=name,
        grid_spec=pltpu.PrefetchScalarGridSpec(
            num_scalar_prefetch=1, grid=(R // tr,),
            in_specs=[pl.BlockSpec((None, None, tr, C), lambda i, pr: (pr[0], pr[1], i, 0)),
                      pl.BlockSpec((8, tr, C), lambda i, pr: (0, i, 0))],
            out_specs=pl.BlockSpec((tr, C), lambda i, pr: (i, 0))),
        out_shape=_sds((R, C), F32),
        compiler_params=_params("parallel"),
    )(place, p, got)


def _round_up(n, m):
    return (n + m - 1) // m * m


def _f32_as_bf16(a):
    return lax.bitcast_convert_type(a.astype(F32), BF16).reshape(-1)


def _bf16_as_f32(a):
    return lax.bitcast_convert_type(a.reshape(-1, 2), F32)


def _by_chip_cols(a, cols):
    lead = a.shape[:-1]
    a = a.reshape(lead + (4, cols))
    return jnp.moveaxis(a, -2, 0).reshape(4, -1)


def kernel(x, meta_tokens, norm_mix, norm_mlp, sb_w_qkv, sb_w_o, lru_w_in, lru_conv_w, lru_conv_b, lru_w_rg, lru_b_rg, lru_w_ig, lru_b_ig, lru_lambda, lru_w_out, mlp_w_up, mlp_w_down, norm_final, loss_target, m_meta_tokens, m_norm_mix, m_norm_mlp, m_sb_w_qkv, m_sb_w_o, m_lru_w_in, m_lru_conv_w, m_lru_conv_b, m_lru_w_rg, m_lru_b_rg, m_lru_w_ig, m_lru_b_ig, m_lru_lambda, m_lru_w_out, m_mlp_w_up, m_mlp_w_down, m_norm_final, v_meta_tokens, v_norm_mix, v_norm_mlp, v_sb_w_qkv, v_sb_w_o, v_lru_w_in, v_lru_conv_w, v_lru_conv_b, v_lru_w_rg, v_lru_b_rg, v_lru_w_ig, v_lru_b_ig, v_lru_lambda, v_lru_w_out, v_mlp_w_up, v_mlp_w_down, v_norm_final):
    weights = dict(meta_tokens=meta_tokens, norm_mix=norm_mix, norm_mlp=norm_mlp, sb_w_qkv=sb_w_qkv,
                   sb_w_o=sb_w_o, lru_w_in=lru_w_in, lru_conv_w=lru_conv_w, lru_conv_b=lru_conv_b,
                   lru_w_rg=lru_w_rg, lru_b_rg=lru_b_rg, lru_w_ig=lru_w_ig, lru_b_ig=lru_b_ig,
                   lru_lambda=lru_lambda, lru_w_out=lru_w_out, mlp_w_up=mlp_w_up, mlp_w_down=mlp_w_down,
                   norm_final=norm_final)
    m_in = dict(meta_tokens=m_meta_tokens, norm_mix=m_norm_mix, norm_mlp=m_norm_mlp, sb_w_qkv=m_sb_w_qkv,
                sb_w_o=m_sb_w_o, lru_w_in=m_lru_w_in, lru_conv_w=m_lru_conv_w, lru_conv_b=m_lru_conv_b,
                lru_w_rg=m_lru_w_rg, lru_b_rg=m_lru_b_rg, lru_w_ig=m_lru_w_ig, lru_b_ig=m_lru_b_ig,
                lru_lambda=m_lru_lambda, lru_w_out=m_lru_w_out, mlp_w_up=m_mlp_w_up,
                mlp_w_down=m_mlp_w_down, norm_final=m_norm_final)
    v_in = dict(meta_tokens=v_meta_tokens, norm_mix=v_norm_mix, norm_mlp=v_norm_mlp, sb_w_qkv=v_sb_w_qkv,
                sb_w_o=v_sb_w_o, lru_w_in=v_lru_w_in, lru_conv_w=v_lru_conv_w, lru_conv_b=v_lru_conv_b,
                lru_w_rg=v_lru_w_rg, lru_b_rg=v_lru_b_rg, lru_w_ig=v_lru_w_ig, lru_b_ig=v_lru_b_ig,
                lru_lambda=v_lru_lambda, lru_w_out=v_lru_w_out, mlp_w_up=v_mlp_w_up,
                mlp_w_down=v_mlp_w_down, norm_final=v_norm_final)
    names = list(weights)

    seq, D = x.shape[1], x.shape[2]
    n_meta = meta_tokens.shape[0]
    Dq = D // 4
    T = _round_up(n_meta + seq, ATT_BLOCK)
    depth = mlp_w_up.shape[0]
    my_x, my_y, my_c = lax.axis_index("x"), lax.axis_index("y"), lax.axis_index("c")
    c_arr = jnp.reshape(my_c, (1,)).astype(jnp.int32)

    assert depth == 2

    def halves(a):
        return a.astype(BF16).reshape(2, a.shape[0] // 2, a.shape[1])

    small = [meta_tokens, lru_conv_w[0], lru_conv_b, lru_b_rg, lru_b_ig, lru_lambda]
    sparts = [_f32_as_bf16(s) for s in small]
    sizes = [p.shape[0] for p in sparts]
    total = _round_up(sum(sizes), 2 * 32 * LANE)
    sflat =jnp.concatenate(sparts + [jnp.zeros((total - sum(sizes),), BF16)]).reshape(2, -1, LANE)
    gq, gsm = _run_exchange(_GatherChips([halves(sb_w_qkv[0]), sflat]), name="gather_first")
    gather_rest = _GatherChips([halves(sb_w_o[0]), halves(lru_w_in[0]), halves(lru_w_out[0]),
                                mlp_w_up.astype(BF16), mlp_w_down.astype(BF16)])
    w_qkv = gq.reshape(4, D, 3 * Dq)
    gsm = gsm.reshape(4, total)
    offs = [sum(sizes[:k]) for k in range(len(sizes))]
    sm = [_bf16_as_f32(gsm[:, o:o + s]) for o, s in zip(offs, sizes)]
    meta_full = jnp.moveaxis(sm[0].reshape(4, n_meta, Dq), 0, 1).reshape(n_meta, D)
    conv_w = jnp.moveaxis(sm[1].reshape(4, 4, Dq), 0, 1).reshape(4, D)
    conv_b, b_rg, b_ig, lam = [s.reshape(1, D) for s in sm[2:6]]
    w_rg = lru_w_rg[0].astype(BF16)
    w_ig = lru_w_ig[0].astype(BF16)
    g_mix = [norm_mix[l].reshape(1, D) for l in range(depth)]
    g_mlp = [norm_mlp[l].reshape(1, D) for l in range(depth)]
    g_fin = norm_final.reshape(1, D)

    pad_rows = T - n_meta - seq
    h0 = jnp.concatenate([meta_full, x[0], jnp.zeros((pad_rows, D), F32)], axis=0)
    target = jnp.concatenate([jnp.zeros((n_meta, D), F32), loss_target[0], jnp.zeros((pad_rows, D), F32)], axis=0)

    hn0, qkv = _norm_mm(h0, g_mix[0], w_qkv, out_dtype=BF16, name="qkv_proj")
    att, w_sv, s_sv, go, gi, gout, w_up, w_down = _attn_fwd(qkv, name="attn_fwd", plan=gather_rest)
    w_o = go.reshape(D, D)
    w_in = gi.reshape(4, D, 2 * Dq)
    w_out = gout.reshape(D, D)
    h1 = _mm_res(att, w_o, h0, name="attn_out")
    h2, hnm0, up0 = _mlp_fwd(h1, g_mlp[0], w_up, w_down, layer=0, name="mlp0_fwd")
    hn1, gr = _norm_mm(h2, g_mix[1], w_in, out_dtype=F32, name="lru_in")
    a_t, b_t = _lru_pre(gr, conv_w, conv_b, w_rg, b_rg, w_ig, b_ig, lam, name="lru_pre")
    hs = _lru_scan(a_t, b_t, reverse=False, name="lru_scan")
    y, h3 = _lru_out(gr, hs, w_out, h2, name="lru_out")
    h4, hnm1, up1 = _mlp_fwd(h3, g_mlp[1], w_up, w_down, layer=1, name="mlp1_fwd")
    loss, dh4, dh4b, dg_fin = _loss_head(h4, g_fin, target, row_lo=n_meta, row_hi=n_meta + seq, name="loss_head")

    dup1, dh3, dh3b, dg_mlp1 = _mlp_bwd(dh4, h3, g_mlp[1], up1, w_up, w_down, layer=1, name="mlp1_bwd")
    dw_up = _mm_tn(hnm1, dup1, shards=4, relu2=False, slot=1, name="mlp1_dwup")
    dw_down = _mm_tn(up1, dh4b, shards=1, relu2=True, slot=1, row_shards=4, name="mlp1_dwdown")
    dw_out = _mm_tn(y, dh3b, shards=1, relu2=False, name="lru_dwout")
    dgr, dhy = _lru_out_bwd(gr, hs, dh3b, w_out, name="lru_out_bwd")
    lmb = _lru_scan(a_t, dhy, reverse=True, name="lru_scan_bwd")
    du, dw_rg, db_rg, dw_ig, db_ig, dlam = _lru_gate_bwd(
        gr, hs, lmb, conv_w, conv_b, w_rg, b_rg, w_ig, b_ig, lam, name="lru_gate_bwd")
    dgr, dconv_w, dconv_b = _lru_conv_bwd(gr, du, conv_w, dgr, name="lru_conv_bwd")
    dh2, dh2b, dg_mix1 = _mm_nt_normbwd(dgr, w_in, h2, g_mix[1], dh3, name="lru_in_bwd")
    dw_in = _mm_tn(hn1, dgr, shards=4, relu2=False, name="lru_dwin")
    dup0, dh1, dh1b, dg_mlp0 = _mlp_bwd(dh2, h1, g_mlp[0], up0, w_up, w_down, layer=0, name="mlp0_bwd")
    dw_up = _mm_tn(hnm0, dup0, shards=4, relu2=False, slot=0, into=dw_up, name="mlp0_dwup")
    dw_down = _mm_tn(up0, dh2b, shards=1, relu2=True, slot=0, into=dw_down, row_shards=4, name="mlp0_dwdown")
    datt = _mm_nt(dh1b, w_o, out_dtype=BF16, name="attn_out_bwd")
    dw_o = _mm_tn(att, dh1b, shards=1, relu2=False, name="attn_dwo")
    def halves_of(d, rows):
        return d.reshape(4, 2, rows // 2, d.shape[-1])

    early = [halves_of(dw_o, Dq), halves_of(dw_in, D), halves_of(dw_out, Dq), dw_up, dw_down]
    dq, dk, dv, *got_early = _attn_bwd(qkv, datt, w_sv, s_sv, name="attn_bwd", plan=_ExchangeBlocks(early))
    dqkv = jnp.concatenate([dq, dk, dv], axis=1)
    large = ["sb_w_o", "lru_w_in", "lru_w_out", "mlp_w_up", "mlp_w_down", "sb_w_qkv"]
    place = jnp.stack([2 * my_x + my_y, my_c, 4 * my_x + 2 * my_y + my_c]).astype(jnp.int32)
    mine = [_add_devices(p, o, place, name="reduce_add_" + t) for p, o, t in zip(early, got_early, large)]
    dw_qkv, *theirs = _mm_tn(hn0, dqkv, shards=4, relu2=False, name="attn_dwqkv", plan=_SendSibling(mine))
    dw_qkv = halves_of(dw_qkv, D)
    dh0, _, dg_mix0, got_qkv = _mm_nt_normbwd(dqkv, w_qkv, h0, g_mix[0], dh1, name="qkv_bwd",
                                               plan=_ExchangeBlocks([dw_qkv]))
    grad_x = dh0[n_meta:n_meta + seq][None]
    dmeta = dh0[:n_meta]

    sharded = [_by_chip_cols(dmeta, Dq), _by_chip_cols(dconv_w, Dq), dconv_b.reshape(4, Dq),
               db_rg.reshape(4, Dq), db_ig.reshape(4, Dq), dlam.reshape(4, Dq)]
    repl = [jnp.concatenate([dg_mix0, dg_mix1], axis=0).reshape(-1),
            jnp.concatenate([dg_mlp0, dg_mlp1], axis=0).reshape(-1),
            dg_fin.reshape(-1), dw_rg.reshape(-1), dw_ig.reshape(-1), loss.reshape(-1)]
    rsizes = [r.shape[0] for r in repl]
    rtotal = _round_up(sum(rsizes), 4 * 2 * 16 * LANE)
    rflat = jnp.concatenate(repl + [jnp.zeros((rtotal - sum(rsizes),), F32)]).reshape(4, rtotal // 4)
    gsizes = [s.shape[1] for s in sharded] + [rtotal // 4]
    gtotal = _round_up(sum(gsizes), 2 * 16 * LANE)
    tail = jnp.concatenate(sharded + [rflat, jnp.zeros((4, gtotal - sum(gsizes)), F32)], axis=1)
    tail = tail.reshape(4, 2, -1, LANE)

    grads, delta, new_m, new_v = {}, {}, {}, {}
    halves = dict(zip(large, zip(mine, theirs)))

    def adam(n, plan=None):
        a, b = halves[n]
        shp = weights[n].shape
        view = (2,) + a.shape
        g, d, nm, nv, *rode = _adamw_halves(weights[n].reshape(view), a, b, m_in[n].reshape(view),
                                            v_in[n].reshape(view), c_arr, name="adamw_" + n, plan=plan)
        grads[n], delta[n], new_m[n], new_v[n] = g.reshape(shp), d.reshape(shp), nm.reshape(shp), nv.reshape(shp)
        return rode

    mine_qkv = _add_devices(dw_qkv, got_qkv, place, name="reduce_add_sb_w_qkv")
    (got_tail,) = adam("mlp_w_up", _ExchangeBlocks([tail]))
    mine_tail = _add_devices(tail, got_tail, place, name="reduce_add_tail")
    their_qkv, their_tail = _run_exchange(_SendSibling([mine_qkv, mine_tail]), name="reduce_join")
    halves["sb_w_qkv"] = (mine_qkv, their_qkv)
    lo = jnp.where(my_c == 0, mine_tail, their_tail)
    hi = jnp.where(my_c == 0, their_tail, mine_tail)
    gshard = jnp.concatenate([lo, hi], axis=0).reshape(gtotal)
    goffs = [sum(gsizes[:k]) for k in range(len(gsizes))]
    gp = [gshard[o:o + s] for o, s in zip(goffs, gsizes)]
    (rfull,) = adam("mlp_w_down", _GatherChips([gp[-1].reshape(2, -1, LANE)]))
    rfull = rfull.reshape(rtotal)
    for n in ("sb_w_o", "lru_w_in", "lru_w_out", "sb_w_qkv"):
        adam(n)
    roffs = [sum(rsizes[:k]) for k in range(len(rsizes))]
    rp = [rfull[o:o + s] for o, s in zip(roffs, rsizes)]
    grads.update(meta_tokens=gp[0], lru_conv_w=gp[1], lru_conv_b=gp[2], lru_b_rg=gp[3], lru_b_ig=gp[4],
                 lru_lambda=gp[5], norm_mix=rp[0], norm_mlp=rp[1], norm_final=rp[2], lru_w_rg=rp[3],
                 lru_w_ig=rp[4])
    grads = {n: grads[n].reshape(weights[n].shape) for n in names}
    rest = [n for n in names if n not in large]
    ssz = [weights[n].size for n in rest]
    small_cols = 8 * LANE
    stotal = _round_up(sum(ssz), SUBLANE * small_cols)

    def pack(src):
        return jnp.concatenate([src[n].reshape(-1) for n in rest]
                               + [jnp.ones((stotal - sum(ssz),), F32)]).reshape(-1, small_cols)

    d, nm, nv = _adamw(pack(weights), pack(grads), pack(m_in), pack(v_in), name="adamw_small")
    soffs = [sum(ssz[:k]) for k in range(len(ssz))]
    for n, o, s in zip(rest, soffs, ssz):
        shp = weights[n].shape
        delta[n] = d.reshape(-1)[o:o + s].reshape(shp)
        new_m[n] = nm.reshape(-1)[o:o + s].reshape(shp)
        new_v[n] = nv.reshape(-1)[o:o + s].reshape(shp)

    loss = rp[5][0]
    return (loss, grad_x, *[grads[n] for n in names], *[delta[n] for n in names],
            *[new_m[n] for n in names], *[new_v[n] for n in names])
```

```python
import jax
import jax.numpy as jnp
from jax import lax
from jax.experimental import pallas as pl
from jax.experimental.pallas import tpu as pltpu

F32 = jnp.float32
BF16 = jnp.bfloat16
MESH = pl.DeviceIdType.MESH

EPS = 1e-6
HEAD_DIM = 64
LANE = 128
SUBLANE = 8
LRU_C = 8.0
BF16_ROWS = 16
VMEM_LIMIT = 56 * 1024 * 1024
ROWS_WIDE = 1056
ROWS_NARROW = 528

ADAM_LR = 0.001
ADAM_B1 = 0.9
ADAM_B2 = 0.999
ADAM_EPS = 1e-08
ADAM_WD = 0.01
ADAM_STEP = 10


def _pcall(body, **kw):
    return pl.pallas_call(body, **kw)


def _params(*sem):
    return pltpu.CompilerParams(dimension_semantics=sem, vmem_limit_bytes=VMEM_LIMIT)


def _tile(n, pref, align):
    best = None
    for t in range(align, min(n, pref) + 1, align):
        if n % t == 0:
            best = t
    return n if best is None else best


def _sds(shape, dtype):
    return jax.ShapeDtypeStruct(shape, dtype)


def _rstd(x):
    return lax.rsqrt(jnp.mean(x * x, axis=-1, keepdims=True) + EPS)


def _norm_bwd(x, g, dy):
    rstd = _rstd(x)
    n = x * rstd
    dn = dy * g
    dx = rstd * (dn - n * jnp.mean(dn * n, axis=-1, keepdims=True))
    dg = jnp.sum(dy * n, axis=0, keepdims=True)
    return dx, dg


def _softplus_parts(z):
    l1p = jnp.log(1.0 + jnp.exp(-jnp.abs(z)))
    return jnp.maximum(z, 0.0) + l1p, jnp.minimum(z, 0.0) - l1p


def _sigmoid(x):
    return 1.0 / (1.0 + jnp.exp(-x))


def _gelu_parts(x):
    k = 0.7978845608028654
    inner = k * (x + 0.044715 * (x * x * x))
    t = jnp.tanh(inner)
    gelu = 0.5 * x * (1.0 + t)
    dgelu = 0.5 * (1.0 + t) + 0.5 * x * (1.0 - t * t) * (k * (1.0 + 3.0 * 0.044715 * (x * x)))
    return gelu, dgelu


def _norm_mm(h, g, w, *, out_dtype, name):
    T, D = h.shape
    S, _, n = w.shape
    tm = _tile(T, ROWS_WIDE, BF16_ROWS)
    tn = _tile(n, 768, LANE)
    nj = n // tn

    def body(h_ref, g_ref, w_ref, hn_ref, o_ref):
        @pl.when(pl.program_id(1) == 0)
        def _():
            x = h_ref[...]
            hn_ref[...] = (x * _rstd(x) * g_ref[...]).astype(BF16)

        o_ref[...] = jnp.dot(hn_ref[...], w_ref[...], preferred_element_type=F32).astype(out_dtype)

    return _pcall(
        body, name=name, grid=(T // tm, S * nj),
        in_specs=[pl.BlockSpec((tm, D), lambda i, j: (i, 0)),
                  pl.BlockSpec((1, D), lambda i, j: (0, 0)),
                  pl.BlockSpec((None, D, tn), lambda i, j: (j // nj, 0, j % nj))],
        out_specs=[pl.BlockSpec((tm, D), lambda i, j: (i, 0)),
                   pl.BlockSpec((tm, tn), lambda i, j: (i, j))],
        out_shape=[_sds((T, D), BF16), _sds((T, S * n), out_dtype)],
        compiler_params=_params("parallel", "arbitrary"),
    )(h, g, w)


def _mm_res(a, w, res, *, name):
    T, K = a.shape
    N = w.shape[1]
    tm = _tile(T, ROWS_WIDE, BF16_ROWS)

    def body(a_ref, w_ref, r_ref, o_ref):
        o_ref[...] = r_ref[...] + jnp.dot(a_ref[...], w_ref[...], preferred_element_type=F32)

    return _pcall(
        body, name=name, grid=(T // tm,),
        in_specs=[pl.BlockSpec((tm, K), lambda i: (i, 0)),
                  pl.BlockSpec((K, N), lambda i: (0, 0)),
                  pl.BlockSpec((tm, N), lambda i: (i, 0))],
        out_specs=pl.BlockSpec((tm, N), lambda i: (i, 0)),
        out_shape=_sds((T, N), F32),
        compiler_params=_params("parallel"),
    )(a, w, res)


def _mm_nt(a, w, *, out_dtype, name):
    T, N = a.shape
    K = w.shape[0]
    tm = _tile(T, ROWS_WIDE, BF16_ROWS)

    def body(a_ref, w_ref, o_ref):
        o_ref[...] = lax.dot_general(a_ref[...], w_ref[...], (((1,), (1,)), ((), ())),
                                     preferred_element_type=F32).astype(out_dtype)

    return _pcall(
        body, name=name, grid=(T // tm,),
        in_specs=[pl.BlockSpec((tm, N), lambda i: (i, 0)),
                  pl.BlockSpec((K, N), lambda i: (0, 0))],
        out_specs=pl.BlockSpec((tm, K), lambda i: (i, 0)),
        out_shape=_sds((T, K), out_dtype),
        compiler_params=_params("parallel"),
    )(a, w)


def _mm_tn(a, b, *, shards, relu2, name, slot=None, into=None, row_shards=0, out_dtype=BF16, plan=None):
    T, Ka = a.shape
    Nb = b.shape[1]
    n = Nb // shards
    tka = _tile(Ka, 512, LANE)
    tnb = _tile(n, 512, LANE)
    nj = n // tnb

    nx, before, after = _ride_along(plan, Ka // tka, shards * nj)
    n_in = 2 + (into is not None)

    def body(*refs):
        a_ref, b_ref = refs[:2]
        o_ref = refs[n_in + nx]
        ride = (refs[n_in:n_in + nx], refs[n_in + nx + 1:n_in + 2 * nx + 1], refs[n_in + 2 * nx + 1:])
        before(ride)
        av = a_ref[...]
        if relu2:
            r = jnp.maximum(av, 0)
            av = r * r
        o_ref[...] = lax.dot_general(av, b_ref[...], (((0,), (0,)), ((), ())),
                                     preferred_element_type=F32).astype(out_dtype)
        after(ride)

    in_specs = [pl.BlockSpec((T, tka), lambda i, j: (0, i)),
                pl.BlockSpec((T, tnb), lambda i, j: (0, j))]
    args = [a, b]
    aliases = {}
    if slot is None:
        out_spec = pl.BlockSpec((None, tka, tnb), lambda i, j: (j // nj, i, j % nj))
        out_shape = _sds((shards, Ka, n), out_dtype)
    else:
        if row_shards:
            ni = Ka // row_shards // tka
            out_spec = pl.BlockSpec((None, None, tka, tnb), lambda i, j: (i // ni, slot, i % ni, j))
            out_shape = _sds((row_shards, 2, Ka // row_shards, n), out_dtype)
        else:
            out_spec = pl.BlockSpec((None, None, tka, tnb), lambda i, j: (j // nj, slot, i, j % nj))
            out_shape = _sds((shards, 2, Ka, n), out_dtype)
        if into is not None:
            in_specs.append(pl.BlockSpec(memory_space=pl.ANY))
            args.append(into)
            aliases = {2: 0}
    extra = plan.inputs if plan else []
    res = _pcall(
        body, name=name, grid=(Ka // tka, shards * nj),
        in_specs=in_specs + [ANY] * nx, out_specs=[out_spec] + [ANY] * nx,
        out_shape=[out_shape] + (plan.out_shape if plan else []),
        input_output_aliases=aliases,
        scratch_shapes=plan.scratch if plan else [],
        compiler_params=_params("arbitrary", "arbitrary"),
    )(*args, *extra)
    return res if plan else res[0]


def _mm_nt_normbwd(dy, w, h, g, dres, *, name, plan=None):
    T, D = h.shape
    S, _, n = w.shape
    tm = _tile(T, ROWS_WIDE, BF16_ROWS)
    nx, before, after = _ride_along(plan, T // tm, S)

    def body(*refs):
        dy_ref, w_ref, h_ref, g_ref, dr_ref = refs[:5]
        dh_ref, dhb_ref, dg_ref = refs[5 + nx:8 + nx]
        acc_ref = refs[8 + 2 * nx]
        ride = (refs[5:5 + nx], refs[8 + nx:8 + 2 * nx], refs[9 + 2 * nx:])
        before(ride)
        i, s = pl.program_id(0), pl.program_id(1)
        part = lax.dot_general(dy_ref[...], w_ref[...], (((1,), (1,)), ((), ())),
                               preferred_element_type=F32)

        @pl.when(s == 0)
        def _():
            acc_ref[...] = part

        @pl.when(s > 0)
        def _():
            acc_ref[...] += part

        @pl.when(s == S - 1)
        def _():
            dx, dg = _norm_bwd(h_ref[...], g_ref[...], acc_ref[...])
            dh = dr_ref[...] + dx
            dh_ref[...] = dh
            dhb_ref[...] = dh.astype(BF16)

            @pl.when(i == 0)
            def _():
                dg_ref[...] = dg

            @pl.when(i > 0)
            def _():
                dg_ref[...] += dg

        after(ride)

    extra = plan.inputs if plan else []
    return _pcall(
        body, name=name, grid=(T // tm, S),
        in_specs=[pl.BlockSpec((tm, n), lambda i, s: (i, s)),
                  pl.BlockSpec((None, D, n), lambda i, s: (s, 0, 0)),
                  pl.BlockSpec((tm, D), lambda i, s: (i, 0)),
                  pl.BlockSpec((1, D), lambda i, s: (0, 0)),
                  pl.BlockSpec((tm, D), lambda i, s: (i, 0))] + [ANY] * nx,
        out_specs=[pl.BlockSpec((tm, D), lambda i, s: (i, 0)),
                   pl.BlockSpec((tm, D), lambda i, s: (i, 0)),
                   pl.BlockSpec((1, D), lambda i, s: (0, 0))] + [ANY] * nx,
        out_shape=[_sds((T, D), F32), _sds((T, D), BF16), _sds((1, D), F32)] + (plan.out_shape if plan else []),
        scratch_shapes=[pltpu.VMEM((tm, D), F32)] + (plan.scratch if plan else []),
        compiler_params=_params("arbitrary", "arbitrary"),
    )(dy, w, h, g, dres, *extra)


def _row_chains(tm):
    first = (tm // 2 + 15) // 16 * 16
    return [slice(0, first), slice(first, tm)] if 0 < first < tm else [slice(0, tm)]


def _mlp_fwd(h, g, w_up, w_down, *, layer, name):
    T, D = h.shape
    S, _, _, n = w_up.shape
    tm = _tile(T, ROWS_NARROW, BF16_ROWS)
    tf = _tile(n, 1024, LANE)
    nj = n // tf
    nf = S * nj
    chains = _row_chains(tm)

    def body(h_ref, g_ref, wu_ref, wd_ref, o_ref, hn_ref, up_ref, acc_ref):
        f = pl.program_id(1)

        @pl.when(f == 0)
        def _():
            x = h_ref[...]
            hn_ref[...] = (x * _rstd(x) * g_ref[...]).astype(BF16)

        parts = []
        for rows in chains:
            up = jnp.dot(hn_ref[rows, :], wu_ref[...], preferred_element_type=F32)
            up_ref[rows, :] = up.astype(BF16)
            r = jnp.maximum(up, 0.0)
            parts.append(jnp.dot((r * r).astype(BF16), wd_ref[...], preferred_element_type=F32))
        part = jnp.concatenate(parts, axis=0)

        @pl.when(f == 0)
        def _():
            acc_ref[...] = part

        @pl.when(f > 0)
        def _():
            acc_ref[...] += part

        @pl.when(f == nf - 1)
        def _():
            o_ref[...] = h_ref[...] + acc_ref[...]

    return _pcall(
        body, name=name, grid=(T // tm, nf),
        in_specs=[pl.BlockSpec((tm, D), lambda i, f: (i, 0)),
                  pl.BlockSpec((1, D), lambda i, f: (0, 0)),
                  pl.BlockSpec((None, None, D, tf), lambda i, f: (f // nj, layer, 0, f % nj)),
                  pl.BlockSpec((None, None, tf, D), lambda i, f: (f // nj, layer, f % nj, 0))],
        out_specs=[pl.BlockSpec((tm, D), lambda i, f: (i, 0)),
                   pl.BlockSpec((tm, D), lambda i, f: (i, 0)),
                   pl.BlockSpec((tm, tf), lambda i, f: (i, f))],
        out_shape=[_sds((T, D), F32), _sds((T, D), BF16), _sds((T, S * n), BF16)],
        scratch_shapes=[pltpu.VMEM((tm, D), F32)],
        compiler_params=_params("parallel", "arbitrary"),
    )(h, g, w_up, w_down)


def _mlp_bwd(dy, h, g, up, w_up, w_down, *, layer, name):
    T, D = h.shape
    S, _, _, n = w_up.shape
    tm = _tile(T, ROWS_NARROW, BF16_ROWS)
    tf = _tile(n, 1024, LANE)
    nj = n // tf
    nf = S * nj
    chains = _row_chains(tm)

    def body(dy_ref, h_ref, g_ref, up_ref, wu_ref, wd_ref, dup_ref, dh_ref, dhb_ref, dg_ref,
             dyb_ref, acc_ref):
        i, f = pl.program_id(0), pl.program_id(1)

        @pl.when(f == 0)
        def _():
            dyb_ref[...] = dy_ref[...].astype(BF16)

        parts = []
        for rows in chains:
            dact = lax.dot_general(dyb_ref[rows, :], wd_ref[...], (((1,), (1,)), ((), ())),
                                   preferred_element_type=F32)
            r = jnp.maximum(up_ref[rows, :].astype(F32), 0.0)
            dup = (dact * (2.0 * r)).astype(BF16)
            dup_ref[rows, :] = dup
            parts.append(lax.dot_general(dup, wu_ref[...], (((1,), (1,)), ((), ())),
                                         preferred_element_type=F32))
        part = jnp.concatenate(parts, axis=0)

        @pl.when(f == 0)
        def _():
            acc_ref[...] = part

        @pl.when(f > 0)
        def _():
            acc_ref[...] += part

        @pl.when(f == nf - 1)
        def _():
            dx, dg = _norm_bwd(h_ref[...], g_ref[...], acc_ref[...])
            dh = dy_ref[...] + dx
            dh_ref[...] = dh
            dhb_ref[...] = dh.astype(BF16)

            @pl.when(i == 0)
            def _():
                dg_ref[...] = dg

            @pl.when(i > 0)
            def _():
                dg_ref[...] += dg

    return _pcall(
        body, name=name, grid=(T // tm, nf),
        in_specs=[pl.BlockSpec((tm, D), lambda i, f: (i, 0)),
                  pl.BlockSpec((tm, D), lambda i, f: (i, 0)),
                  pl.BlockSpec((1, D), lambda i, f: (0, 0)),
                  pl.BlockSpec((tm, tf), lambda i, f: (i, f)),
                  pl.BlockSpec((None, None, D, tf), lambda i, f: (f // nj, layer, 0, f % nj)),
                  pl.BlockSpec((None, None, tf, D), lambda i, f: (f // nj, layer, f % nj, 0))],
        out_specs=[pl.BlockSpec((tm, tf), lambda i, f: (i, f)),
                   pl.BlockSpec((tm, D), lambda i, f: (i, 0)),
                   pl.BlockSpec((tm, D), lambda i, f: (i, 0)),
                   pl.BlockSpec((1, D), lambda i, f: (0, 0))],
        out_shape=[_sds((T, S * n), BF16), _sds((T, D), F32), _sds((T, D), BF16), _sds((1, D), F32)],
        scratch_shapes=[pltpu.VMEM((tm, D), BF16), pltpu.VMEM((tm, D), F32)],
        compiler_params=_params("arbitrary", "arbitrary"),
    )(dy, h, g, up, w_up, w_down)


ATT_BLOCK = 128
ATT_HEADS = 4


def _attn_tile(T):
    for w in (3 * ATT_BLOCK, 2 * ATT_BLOCK):
        if T % w == 0:
            return w
    return ATT_BLOCK


def _tri(strict_lower, value):
    B = ATT_BLOCK
    r = lax.broadcasted_iota(jnp.int32, (2 * B, B), 0)
    r = jnp.where(r >= B, r - B, r)
    c = lax.broadcasted_iota(jnp.int32, (2 * B, B), 1)
    m = (r > c) if strict_lower else (r < c)
    return jnp.where(m, value, 0.0).astype(BF16)


def _split_dot(x, tri):
    hi = lax.bitcast_convert_type(lax.bitcast_convert_type(x, jnp.uint32) & jnp.uint32(0xFFFF0000), F32)
    lo = x - hi
    return jnp.dot(jnp.concatenate([hi.astype(BF16), lo.astype(BF16)], axis=1), tri,
                   preferred_element_type=F32)


def _causal_mask(W):
    r = lax.broadcasted_iota(jnp.int32, (W, W), 0)
    c = lax.broadcasted_iota(jnp.int32, (W, W), 1)
    return c < r


def _two_heads(x):
    left = lax.broadcasted_iota(jnp.int32, x.shape, 1) < HEAD_DIM
    zero = jnp.zeros_like(x)
    return jnp.concatenate([jnp.where(left, x, zero), jnp.where(left, zero, x)], axis=0)


def _attn_scores(z, carry, tri_neg, masked):
    W = z.shape[0]
    B = ATT_BLOCK
    minus_abs = lax.bitcast_convert_type(
        lax.bitcast_convert_type(z, jnp.uint32) | jnp.uint32(0x80000000), F32)
    sp = jnp.maximum(z, 0.0) + jnp.log(1.0 + jnp.exp(minus_abs))
    logsig = z - sp
    if masked:
        causal = _causal_mask(W)
        sp = jnp.where(causal, sp, 0.0)
    afters = []
    for b in reversed(range(W // B)):
        blk = sp[:, b * B:(b + 1) * B]
        within = _split_dot(blk, tri_neg)
        afters.append(within + carry)
        carry = carry + (within[:, 0:1] - blk[:, 0:1])
    after = jnp.concatenate(afters[::-1], axis=1)
    w = jnp.exp(logsig + after)
    if masked:
        w = jnp.where(causal, w, 0.0)
    return w, jnp.exp(logsig), carry


def _ride_along(plan, npairs, nq):
    if plan is None:
        return 0, (lambda refs: None), (lambda refs: None)
    nx = plan.n

    def before(refs):
        p, i = pl.program_id(0), pl.program_id(1)

        @pl.when(jnp.logical_and(p == 0, i == 0))
        def _():
            plan.start(*refs)

        @pl.when(jnp.logical_and(p == npairs // 2, i == 0))
        def _():
            plan.mid(*refs)

    def after(refs):
        p, i = pl.program_id(0), pl.program_id(1)

        @pl.when(jnp.logical_and(p == npairs - 1, i == nq - 1))
        def _():
            plan.finish(*refs)

    return nx, before, after


def _saved_tile(i, j):
    return i * (i + 1) // 2 + j


def _attn_fwd(qkv, *, name, plan=None):
    T = qkv.shape[0]
    D = qkv.shape[1] // 3
    W = _attn_tile(T)
    H = 2 * ATT_HEADS if D % (2 * ATT_HEADS * HEAD_DIM) == 0 else ATT_HEADS
    lanes = H * HEAD_DIM
    ngroups = D // lanes
    nq = T // W
    ntri = nq * (nq + 1) // 2
    scale = HEAD_DIM ** -0.5
    nx, before, after = _ride_along(plan, ngroups, nq)

    def body(*refs):
        q_ref, k_ref, v_ref = refs[:3]
        o_ref, wsv_ref, ssv_ref = refs[3 + nx:6 + nx]
        stage_w, stage_s, stage_sems = refs[6 + 2 * nx:9 + 2 * nx]
        ride = (refs[3:3 + nx], refs[6 + nx:6 + 2 * nx], refs[9 + 2 * nx:])
        before(ride)
        p, i = pl.program_id(0), pl.program_id(1)
        tri = _tri(True, -1.0)
        pairs = [slice(pp * LANE, (pp + 1) * LANE) for pp in range(H // 2)]
        qs = [q_ref[:, cols] * scale for cols in pairs]

        def save(slot, j):
            dst = _saved_tile(i, j)
            return [pltpu.make_async_copy(stage.at[slot], sv.at[pl.ds(p * H, H), dst], stage_sems.at[slot, a])
                    for a, (sv, stage) in enumerate(((wsv_ref, stage_w), (ssv_ref, stage_s)))]

        def tile(t, state, masked):
            j = i - t
            rows = pl.ds(pl.multiple_of(j * W, W), W)
            slot = t % 2
            if not masked:
                @pl.when(t >= 2)
                def _():
                    for cp in save(slot, j):
                        cp.wait()
            out = []
            for pp, (cols, q, (carries, acc)) in enumerate(zip(pairs, qs, state)):
                z = lax.dot_general(q, _two_heads(k_ref[rows, cols]), (((1,), (1,)), ((), ())),
                                    preferred_element_type=F32)
                wbs, new_carries = [], []
                for e, carry in enumerate(carries):
                    w, sig, carry = _attn_scores(z[:, e * W:(e + 1) * W], carry, tri, masked)
                    wb = w.astype(BF16)
                    stage_w[slot, 2 * pp + e] = wb
                    stage_s[slot, 2 * pp + e] = sig.astype(BF16)
                    wbs.append(wb)
                    new_carries.append(carry)
                acc = acc + jnp.dot(jnp.concatenate(wbs, axis=1), _two_heads(v_ref[rows, cols]),
                                    preferred_element_type=F32)
                out.append((tuple(new_carries), acc))
            for cp in save(slot, j):
                cp.start()
            return tuple(out)

        zero = ((jnp.zeros((W, 1), F32),) * 2, jnp.zeros((W, LANE), F32))
        state = tile(0, (zero,) * (H // 2), True)
        state = lax.fori_loop(1, i + 1, lambda t, st: tile(t, st, False), state)
        for cols, (_, acc) in zip(pairs, state):
            o_ref[:, cols] = acc.astype(BF16)
        for cp in save(i % 2, 0):
            cp.wait()

        @pl.when(i >= 1)
        def _():
            for cp in save((i + 1) % 2, 0):
                cp.wait()

        after(ride)

    extra = plan.inputs if plan else []
    saved = _sds((D // HEAD_DIM, ntri, W, W), BF16)
    return _pcall(
        body, name=name, grid=(ngroups, nq),
        in_specs=[pl.BlockSpec((W, lanes), lambda p, i: (i, p)),
                  pl.BlockSpec((T, lanes), lambda p, i: (0, ngroups + p)),
                  pl.BlockSpec((T, lanes), lambda p, i: (0, 2 * ngroups + p))] + [ANY] * nx,
        out_specs=[pl.BlockSpec((W, lanes), lambda p, i: (i, p)), ANY, ANY] + [ANY] * nx,
        out_shape=[_sds((T, D), BF16), saved, saved] + (plan.out_shape if plan else []),
        scratch_shapes=[pltpu.VMEM((2, H, W, W), BF16), pltpu.VMEM((2, H, W, W), BF16),
                        pltpu.SemaphoreType.DMA((2, 2))] + (plan.scratch if plan else []),
        compiler_params=_params("arbitrary", "arbitrary"),
    )(qkv, qkv, qkv, *extra)


def _attn_bwd(qkv, do, w_sv, s_sv, *, name, plan=None):
    T = qkv.shape[0]
    D = qkv.shape[1] // 3
    B = ATT_BLOCK
    W = _attn_tile(T)
    H = 2 * ATT_HEADS if D % (2 * ATT_HEADS * HEAD_DIM) == 0 else ATT_HEADS
    lanes = H * HEAD_DIM
    ngroups = D // lanes
    nq = T // W
    scale = HEAD_DIM ** -0.5
    nx, before, after = _ride_along(plan, ngroups, nq)

    def body(*refs):
        q_ref, k_ref, v_ref, do_ref, wsv_ref, ssv_ref = refs[:6]
        dq_ref, dk_ref, dv_ref = refs[6 + nx:9 + nx]
        dk_acc, dv_acc, stage_w, stage_s, stage_sems = refs[9 + 2 * nx:14 + 2 * nx]
        ride = (refs[6:6 + nx], refs[9 + nx:9 + 2 * nx], refs[14 + 2 * nx:])
        before(ride)
        p, i = pl.program_id(0), pl.program_id(1)

        @pl.when(i == 0)
        def _():
            dk_acc[...] = jnp.zeros_like(dk_acc)
            dv_acc[...] = jnp.zeros_like(dv_acc)

        def fetch(slot, j):
            src = _saved_tile(i, j)
            return [pltpu.make_async_copy(sv.at[pl.ds(p * H, H), src], stage.at[slot], stage_sems.at[slot, a])
                    for a, (sv, stage) in enumerate(((wsv_ref, stage_w), (ssv_ref, stage_s)))]

        tri_before = _tri(False, 1.0)
        pairs = [slice(pp * LANE, (pp + 1) * LANE) for pp in range(H // 2)]
        q_two = [_two_heads(q_ref[:, cols] * scale) for cols in pairs]
        do_pair = [do_ref[:, cols] for cols in pairs]
        do_two = [_two_heads(d) for d in do_pair]

        def grad(j, state, masked):
            rows = pl.ds(pl.multiple_of(j * W, W), W)
            slot = j % 2
            for cp in fetch(slot, j):
                cp.wait()
            if not masked:
                for cp in fetch(1 - slot, j + 1):
                    cp.start()
            out = []
            for pp, (cols, (gsums, dq)) in enumerate(zip(pairs, state)):
                dw = lax.dot_general(do_pair[pp], _two_heads(v_ref[rows, cols]), (((1,), (1,)), ((), ())),
                                     preferred_element_type=F32)
                dzs, wbs, new_gsums = [], [], []
                for e, gsum in enumerate(gsums):
                    wb = stage_w[slot, 2 * pp + e]
                    sig = stage_s[slot, 2 * pp + e].astype(F32)
                    g = dw[:, e * W:(e + 1) * W] * wb.astype(F32)
                    befores = []
                    for b in range(W // B):
                        blk = g[:, b * B:(b + 1) * B]
                        within = _split_dot(blk, tri_before)
                        befores.append(within + gsum)
                        gsum = gsum + (within[:, B - 1:B] + blk[:, B - 1:B])
                    dz = g - sig * (g + jnp.concatenate(befores, axis=1))
                    if masked:
                        dz = jnp.where(_causal_mask(W), dz, 0.0)
                    dzs.append(dz.astype(BF16))
                    wbs.append(wb)
                    new_gsums.append(gsum)
                dq = dq + jnp.dot(jnp.concatenate(dzs, axis=1), _two_heads(k_ref[rows, cols] * scale),
                                  preferred_element_type=F32)
                dk_acc[rows, cols] += lax.dot_general(jnp.concatenate(dzs, axis=0), q_two[pp],
                                                      (((0,), (0,)), ((), ())), preferred_element_type=F32)
                dv_acc[rows, cols] += lax.dot_general(jnp.concatenate(wbs, axis=0), do_two[pp],
                                                      (((0,), (0,)), ((), ())), preferred_element_type=F32)
                out.append((tuple(new_gsums), dq))
            return tuple(out)

        for cp in fetch(0, 0):
            cp.start()
        zero = ((jnp.zeros((W, 1), F32),) * 2, jnp.zeros((W, LANE), F32))
        state = lax.fori_loop(0, i, lambda j, st: grad(j, st, False), (zero,) * (H // 2))
        state = grad(i, state, True)
        for cols, (_, dq) in zip(pairs, state):
            dq_ref[:, cols] = dq.astype(BF16)

        @pl.when(i == nq - 1)
        def _():
            dk_ref[...] = dk_acc[...].astype(BF16)
            dv_ref[...] = dv_acc[...].astype(BF16)

        after(ride)

    extra = plan.inputs if plan else []
    return _pcall(
        body, name=name, grid=(ngroups, nq),
        in_specs=[pl.BlockSpec((W, lanes), lambda p, i: (i, p)),
                  pl.BlockSpec((T, lanes), lambda p, i: (0, ngroups + p), pipeline_mode=pl.Buffered(1)),
                  pl.BlockSpec((T, lanes), lambda p, i: (0, 2 * ngroups + p), pipeline_mode=pl.Buffered(1)),
                  pl.BlockSpec((W, lanes), lambda p, i: (i, p)), ANY, ANY] + [ANY] * nx,
        out_specs=[pl.BlockSpec((W, lanes), lambda p, i: (i, p)),
                   pl.BlockSpec((T, lanes), lambda p, i: (0, p), pipeline_mode=pl.Buffered(1)),
                   pl.BlockSpec((T, lanes), lambda p, i: (0, p), pipeline_mode=pl.Buffered(1))] + [ANY] * nx,
        out_shape=[_sds((T, D), BF16)] * 3 + (plan.out_shape if plan else []),
        scratch_shapes=[pltpu.VMEM((T, lanes), F32), pltpu.VMEM((T, lanes), F32),
                        pltpu.VMEM((2, H, W, W), BF16), pltpu.VMEM((2, H, W, W), BF16),
                        pltpu.SemaphoreType.DMA((2, 2))]
        + (plan.scratch if plan else []),
        compiler_params=_params("arbitrary", "arbitrary"),
    )(qkv, qkv, qkv, do, w_sv, s_sv, *extra)


HALO = SUBLANE


def _lru_gates(u, w_rg, b_rg, w_ig, b_ig, lam):
    nb = w_rg.shape[0]
    pre_r, pre_i = [], []
    for n in range(nb):
        ub = u[:, n * LANE:(n + 1) * LANE].astype(BF16)
        pre_r.append(jnp.dot(ub, w_rg[n], preferred_element_type=F32))
        pre_i.append(jnp.dot(ub, w_ig[n], preferred_element_type=F32))
    r = _sigmoid(jnp.concatenate(pre_r, axis=1) + b_rg)
    i = _sigmoid(jnp.concatenate(pre_i, axis=1) + b_ig)
    c = -LRU_C * _softplus_parts(-lam)[0]
    log_a = c * r
    a = jnp.exp(log_a)
    x2 = 2.0 * log_a
    em1 = jnp.where(jnp.abs(x2) < 1e-2, x2 * (1.0 + x2 * (0.5 + x2 * (1.0 / 6.0))), jnp.exp(x2) - 1.0)
    mult = jnp.sqrt(-em1)
    return r, i, a, mult, c


def _conv_rows(buf_ref, tt, conv_w, conv_b):
    u = conv_b
    for j in range(4):
        u = u + buf_ref[pl.ds(HALO - 3 + j, tt), :] * conv_w[j:j + 1, :]
    return u


def _fill_with_halo(buf_ref, prev_ref, cur_ref, first):
    tt = cur_ref.shape[0]
    buf_ref[pl.ds(0, HALO), :] = jnp.where(first, 0.0, prev_ref[...])
    buf_ref[pl.ds(HALO, tt), :] = cur_ref[...]


def _lru_time_tile(T):
    return _tile(T, 256, SUBLANE)


def _lru_pre(gr, conv_w, conv_b, w_rg, b_rg, w_ig, b_ig, lam, *, name):
    T = gr.shape[0]
    D = gr.shape[1] // 2
    tt = _lru_time_tile(T)
    hb = tt // HALO

    def body(x_ref, xp_ref, cw_ref, cb_ref, wr_ref, br_ref, wi_ref, bi_ref, lam_ref, a_ref, b_ref, buf):
        _fill_with_halo(buf, xp_ref, x_ref, pl.program_id(0) == 0)
        u = _conv_rows(buf, tt, cw_ref[...], cb_ref[...])
        _, i, a, mult, _ = _lru_gates(u, wr_ref, br_ref[...], wi_ref, bi_ref[...], lam_ref[...])
        a_ref[...] = a
        b_ref[...] = mult * (i * u)

    vec = pl.BlockSpec((1, D), lambda t: (0, 0))
    mat = pl.BlockSpec(w_rg.shape, lambda t: (0, 0, 0))
    return _pcall(
        body, name=name, grid=(T // tt,),
        in_specs=[pl.BlockSpec((tt, D), lambda t: (t, 1)),
                  pl.BlockSpec((HALO, D), lambda t: (jnp.maximum(t * hb - 1, 0), 1)),
                  pl.BlockSpec((4, D), lambda t: (0, 0)), vec, mat, vec, mat, vec, vec],
        out_specs=[pl.BlockSpec((tt, D), lambda t: (t, 0))] * 2,
        out_shape=[_sds((T, D), F32)] * 2,
        scratch_shapes=[pltpu.VMEM((tt + HALO, D), F32)],
        compiler_params=_params("parallel"),
    )(gr, gr, conv_w, conv_b, w_rg, b_rg, w_ig, b_ig, lam)


def _lru_scan(a, b, *, reverse, name):
    T, D = a.shape
    R = SUBLANE
    ts = _tile(T, 528, R)
    nt = T // ts

    def body(a_ref, b_ref, o_ref, carry):
        @pl.when(pl.program_id(0) == 0)
        def _():
            carry[...] = jnp.zeros_like(carry)

        rowid = lax.broadcasted_iota(jnp.int32, (R, D), 0)

        def chunk(k, run):
            if reverse:
                k = ts // R - 1 - k
            rows = pl.ds(pl.multiple_of(k * R, R), R)
            at, bt = a_ref[rows, :], b_ref[rows, :]
            out = jnp.zeros((R, D), F32)
            for r in (range(R - 1, -1, -1) if reverse else range(R)):
                if reverse:
                    cand = bt + run
                    nxt = at * cand
                else:
                    cand = at * run + bt
                    nxt = cand
                out = jnp.where(rowid == r, cand, out)
                run = jnp.broadcast_to(nxt[r:r + 1, :], (R, D))
            o_ref[rows, :] = out
            return run

        carry[...] = lax.fori_loop(0, ts // R, chunk, carry[...])

    if reverse:
        spec = pl.BlockSpec((ts, D), lambda t: (nt - 1 - t, 0))
    else:
        spec = pl.BlockSpec((ts, D), lambda t: (t, 0))
    return _pcall(
        body, name=name, grid=(nt,),
        in_specs=[spec, spec], out_specs=spec,
        out_shape=_sds((T, D), F32),
        scratch_shapes=[pltpu.VMEM((R, D), F32)],
        compiler_params=_params("arbitrary"),
    )(a, b)


def _lru_out(gr, hs, w, res, *, name):
    T, D = hs.shape
    tt = _tile(T, ROWS_NARROW, BF16_ROWS)

    def body(g_ref, h_ref, w_ref, r_ref, y_ref, o_ref):
        y = (h_ref[...] * _gelu_parts(g_ref[...])[0]).astype(BF16)
        y_ref[...] = y
        o_ref[...] = r_ref[...] + jnp.dot(y, w_ref[...], preferred_element_type=F32)

    blk = pl.BlockSpec((tt, D), lambda t: (t, 0))
    return _pcall(
        body, name=name, grid=(T // tt,),
        in_specs=[blk, blk, pl.BlockSpec((D, D), lambda t: (0, 0)), blk],
        out_specs=[blk, blk],
        out_shape=[_sds((T, D), BF16), _sds((T, D), F32)],
        compiler_params=_params("parallel"),
    )(gr, hs, w, res)


def _lru_out_bwd(gr, hs, dout, w, *, name):
    T, D = hs.shape
    tt = _tile(T, ROWS_NARROW, BF16_ROWS)

    def body(g_ref, h_ref, d_ref, w_ref, dg_ref, dh_ref):
        dy = lax.dot_general(d_ref[...], w_ref[...], (((1,), (1,)), ((), ())), preferred_element_type=F32)
        gelu, dgelu = _gelu_parts(g_ref[...])
        dg_ref[...] = (dy * h_ref[...] * dgelu).astype(BF16)
        dh_ref[...] = dy * gelu

    spec = pl.BlockSpec((tt, D), lambda t: (t, 0))
    return _pcall(
        body, name=name, grid=(T // tt,),
        in_specs=[spec, spec, spec, pl.BlockSpec((D, D), lambda t: (0, 0))], out_specs=[spec, spec],
        out_shape=[_sds((T, 2 * D), BF16), _sds((T, D), F32)],
        compiler_params=_params("parallel"),
    )(gr, hs, dout, w)


def _lru_gate_bwd(gr, hs, lmb, conv_w, conv_b, w_rg, b_rg, w_ig, b_ig, lam, *, name):
    T, D = hs.shape
    nb = w_rg.shape[0]
    tt = _lru_time_tile(T)
    hb = tt // HALO
    nt = T // tt

    def body(x_ref, xp_ref, h_ref, hp_ref, l_ref, cw_ref, cb_ref, wr_ref, br_ref, wi_ref, bi_ref, lam_ref,
             du_ref, dwr_ref, dbr_ref, dwi_ref, dbi_ref, dlam_ref, xbuf, hbuf):
        t = pl.program_id(0)
        first = t == 0
        _fill_with_halo(xbuf, xp_ref, x_ref, first)
        _fill_with_halo(hbuf, hp_ref, h_ref, first)
        u = _conv_rows(xbuf, tt, cw_ref[...], cb_ref[...])
        lam_v = lam_ref[...]
        r, i, a, mult, c = _lru_gates(u, wr_ref, br_ref[...], wi_ref, bi_ref[...], lam_v)
        l = l_ref[...]
        h_prev = hbuf[pl.ds(HALO - 1, tt), :]
        dlog_a = l * h_prev * a - l * (i * u) * (a * a) / mult
        d_iu = l * mult
        dpre_r = (dlog_a * c) * (r * (1.0 - r))
        dpre_i = (d_iu * u) * (i * (1.0 - i))
        dpr_b = dpre_r.astype(BF16)
        dpi_b = dpre_i.astype(BF16)
        du_parts, dwr, dwi = [], [], []
        for n in range(nb):
            cs = slice(n * LANE, (n + 1) * LANE)
            ub = u[:, cs].astype(BF16)
            du_parts.append(
                lax.dot_general(dpr_b[:, cs], wr_ref[n], (((1,), (1,)), ((), ())), preferred_element_type=F32)
                + lax.dot_general(dpi_b[:, cs], wi_ref[n], (((1,), (1,)), ((), ())), preferred_element_type=F32))
            dwr.append(lax.dot_general(ub, dpr_b[:, cs], (((0,), (0,)), ((), ())), preferred_element_type=F32))
            dwi.append(lax.dot_general(ub, dpi_b[:, cs], (((0,), (0,)), ((), ())), preferred_element_type=F32))
        du_ref[...] = d_iu * i + jnp.concatenate(du_parts, axis=1)
        dbr = jnp.sum(dpre_r, axis=0, keepdims=True)
        dbi = jnp.sum(dpre_i, axis=0, keepdims=True)
        dc = jnp.sum(dlog_a * r, axis=0, keepdims=True)

        @pl.when(first)
        def _():
            for n in range(nb):
                dwr_ref[n] = dwr[n]
                dwi_ref[n] = dwi[n]
            dbr_ref[...] = dbr
            dbi_ref[...] = dbi
            dlam_ref[...] = dc

        @pl.when(t > 0)
        def _():
            for n in range(nb):
                dwr_ref[n] += dwr[n]
                dwi_ref[n] += dwi[n]
            dbr_ref[...] += dbr
            dbi_ref[...] += dbi
            dlam_ref[...] += dc

        @pl.when(t == nt - 1)
        def _():
            dlam_ref[...] = dlam_ref[...] * (LRU_C * _sigmoid(-lam_v))

    vec = pl.BlockSpec((1, D), lambda t: (0, 0))
    mat = pl.BlockSpec(w_rg.shape, lambda t: (0, 0, 0))
    blk = pl.BlockSpec((tt, D), lambda t: (t, 0))
    prev = pl.BlockSpec((HALO, D), lambda t: (jnp.maximum(t * hb - 1, 0), 0))
    return _pcall(
        body, name=name, grid=(nt,),
        in_specs=[pl.BlockSpec((tt, D), lambda t: (t, 1)),
                  pl.BlockSpec((HALO, D), lambda t: (jnp.maximum(t * hb - 1, 0), 1)),
                  blk, prev, blk,
                  pl.BlockSpec((4, D), lambda t: (0, 0)), vec, mat, vec, mat, vec, vec],
        out_specs=[blk, mat, vec, mat, vec, vec],
        out_shape=[_sds((T, D), F32), _sds(w_rg.shape, F32), _sds((1, D), F32),
                   _sds(w_rg.shape, F32), _sds((1, D), F32), _sds((1, D), F32)],
        scratch_shapes=[pltpu.VMEM((tt + HALO, D), F32), pltpu.VMEM((tt + HALO, D), F32)],
        compiler_params=_params("arbitrary"),
    )(gr, gr, hs, hs, lmb, conv_w, conv_b, w_rg, b_rg, w_ig, b_ig, lam)


def _lru_conv_bwd(gr, du, conv_w, dgr, *, name):
    T, D = du.shape
    tt = _lru_time_tile(T)
    hb = tt // HALO
    nt = T // tt

    def body(x_ref, xp_ref, du_ref, dun_ref, cw_ref, _, dx_ref, dcw_ref, dcb_ref, xbuf, dbuf):
        t = pl.program_id(0)
        _fill_with_halo(xbuf, xp_ref, x_ref, t == 0)
        du = du_ref[...]
        dbuf[pl.ds(0, tt), :] = du
        dbuf[pl.ds(tt, HALO), :] = jnp.where(t == nt - 1, 0.0, dun_ref[...])
        cw = cw_ref[...]
        dx = jnp.zeros((tt, D), F32)
        dcw = []
        for j in range(4):
            dx = dx + dbuf[pl.ds(3 - j, tt), :] * cw[j:j + 1, :]
            dcw.append(jnp.sum(du * xbuf[pl.ds(HALO - 3 + j, tt), :], axis=0, keepdims=True))
        dx_ref[...] = dx.astype(BF16)
        dcw = jnp.concatenate(dcw, axis=0)
        dcb = jnp.sum(du, axis=0, keepdims=True)

        @pl.when(t == 0)
        def _():
            dcw_ref[...] = dcw
            dcb_ref[...] = dcb

        @pl.when(t > 0)
        def _():
            dcw_ref[...] += dcw
            dcb_ref[...] += dcb

    blk = pl.BlockSpec((tt, D), lambda t: (t, 0))
    return _pcall(
        body, name=name, grid=(nt,),
        in_specs=[pl.BlockSpec((tt, D), lambda t: (t, 1)),
                  pl.BlockSpec((HALO, D), lambda t: (jnp.maximum(t * hb - 1, 0), 1)),
                  blk,
                  pl.BlockSpec((HALO, D), lambda t: (jnp.minimum((t + 1) * hb, T // HALO - 1), 0)),
                  pl.BlockSpec((4, D), lambda t: (0, 0)), ANY],
        out_specs=[pl.BlockSpec((tt, D), lambda t: (t, 1)),
                   pl.BlockSpec((4, D), lambda t: (0, 0)), pl.BlockSpec((1, D), lambda t: (0, 0))],
        out_shape=[_sds((T, 2 * D), BF16), _sds((4, D), F32), _sds((1, D), F32)],
        input_output_aliases={5: 0},
        scratch_shapes=[pltpu.VMEM((tt + HALO, D), F32), pltpu.VMEM((tt + HALO, D), F32)],
        compiler_params=_params("arbitrary"),
    )(gr, gr, du, du, conv_w, dgr)


def _loss_head(h, g, target, *, row_lo, row_hi, name):
    T, D = h.shape
    tm = _tile(T, ROWS_WIDE, BF16_ROWS)

    def body(h_ref, g_ref, t_ref, loss_ref, dh_ref, dhb_ref, dg_ref):
        i = pl.program_id(0)
        x = h_ref[...]
        g = g_ref[...]
        row = i * tm + lax.broadcasted_iota(jnp.int32, (tm, 1), 0)
        valid = jnp.logical_and(row >= row_lo, row < row_hi)
        rstd = _rstd(x)
        n = x * rstd
        err = jnp.where(valid, n * g - t_ref[...], 0.0)
        part = (0.5 / D) * jnp.sum(jnp.sum(err * err, axis=1, keepdims=True), axis=0, keepdims=True)
        dy = err * (1.0 / D)
        dn = dy * g
        dh = rstd * (dn - n * jnp.mean(dn * n, axis=-1, keepdims=True))
        dh_ref[...] = dh
        dhb_ref[...] = dh.astype(BF16)
        dg = jnp.sum(dy * n, axis=0, keepdims=True)

        @pl.when(i == 0)
        def _():
            loss_ref[...] = part
            dg_ref[...] = dg

        @pl.when(i > 0)
        def _():
            loss_ref[...] += part
            dg_ref[...] += dg

    blk = pl.BlockSpec((tm, D), lambda i: (i, 0))
    vec = pl.BlockSpec((1, D), lambda i: (0, 0))
    return _pcall(
        body, name=name, grid=(T // tm,),
        in_specs=[blk, vec, blk],
        out_specs=[pl.BlockSpec((1, 1), lambda i: (0, 0)), blk, blk, vec],
        out_shape=[_sds((1, 1), F32), _sds((T, D), F32), _sds((T, D), BF16), _sds((1, D), F32)],
        compiler_params=_params("arbitrary"),
    )(h, g, target)


def _adamw_math(w, g, m, v):
    c1 = 1.0 / (1.0 - ADAM_B1 ** ADAM_STEP)
    c2 = 1.0 / (1.0 - ADAM_B2 ** ADAM_STEP)
    m = ADAM_B1 * m + (1.0 - ADAM_B1) * g
    v = ADAM_B2 * v + (1.0 - ADAM_B2) * (g * g)
    delta = -ADAM_LR * ((m * c1) / (jnp.sqrt(v * c2) + ADAM_EPS) + ADAM_WD * w)
    return delta, m, v


def _adamw_halves(w, mine, theirs, m, v, c, *, name):
    _, R, C = w.shape
    tr = _tile(R, 256, SUBLANE)

    def body(c_ref, w_ref, a_ref, b_ref, m_ref, v_ref, g_ref, d_ref, nm_ref, nv_ref):
        g = jnp.where(pl.program_id(0) == c_ref[0], a_ref[...], b_ref[...])
        g_ref[...] = g
        d_ref[...], nm_ref[...], nv_ref[...] = _adamw_math(w_ref[...], g, m_ref[...], v_ref[...])

    full = pl.BlockSpec((None, tr, C), lambda h, i, c_ref: (h, i, 0))
    half = pl.BlockSpec((tr, C), lambda h, i, c_ref: (i, 0))
    return _pcall(
        body, name=name,
        grid_spec=pltpu.PrefetchScalarGridSpec(
            num_scalar_prefetch=1, grid=(2, R // tr),
            in_specs=[full, half, half, full, full], out_specs=[full] * 4),
        out_shape=[_sds((2, R, C), F32)] * 4,
        compiler_params=_params("parallel", "parallel"),
    )(c, w, mine, theirs, m, v)


def _adamw(w, g, m, v, *, name):
    R, C = w.shape
    tr = _tile(R, 512, SUBLANE)

    def body(w_ref, g_ref, m_ref, v_ref, d_ref, nm_ref, nv_ref):
        d_ref[...], nm_ref[...], nv_ref[...] = _adamw_math(w_ref[...], g_ref[...], m_ref[...], v_ref[...])

    blk = pl.BlockSpec((tr, C), lambda i: (i, 0))
    return _pcall(
        body, name=name, grid=(R // tr,),
        in_specs=[blk] * 4, out_specs=[blk] * 3,
        out_shape=[_sds((R, C), F32)] * 3,
        compiler_params=_params("parallel"),
    )(w, g, m, v)


ANY = pl.BlockSpec(memory_space=pl.ANY)


def _place():
    x, y, c = lax.axis_index("x"), lax.axis_index("y"), lax.axis_index("c")
    chips = [(1 - x, y), (x, 1 - y), (1 - x, 1 - y)]
    return x, y, c, chips


LOCAL_PIECES = 4


class _GatherChips:
    def __init__(self, vs):
        self.inputs = list(vs)
        n = self.n = len(vs)
        self.out_shape = [_sds((4,) + v.shape, v.dtype) for v in vs]
        self.scratch = [pltpu.SemaphoreType.DMA((6 * n,)), pltpu.SemaphoreType.DMA((6 * n,)),
                        pltpu.SemaphoreType.DMA((LOCAL_PIECES * n,))]

    def _copies(self, v_refs, o_refs, sems):
        send_sems, recv_sems, local_sems = sems
        x, y, c, chips = _place()
        me = 2 * x + y

        def copy(a, k, block, half, to, src=None):
            dst = o_refs[a].at[block, half]
            return pltpu.make_async_remote_copy(
                src_ref=dst if src is None else src, dst_ref=dst,
                send_sem=send_sems.at[6 * a + k], recv_sem=recv_sems.at[6 * a + k],
                device_id=to, device_id_type=MESH)

        ks = [(a, k, cx, cy) for a in range(self.n) for k, (cx, cy) in enumerate(chips)]

        def local():
            out = []
            for a in range(self.n):
                rows = self.inputs[a].shape[1] // (LOCAL_PIECES // 2)
                for p in range(LOCAL_PIECES):
                    h, r0 = p % 2, (p // 2) * rows
                    out.append(pltpu.make_async_copy(
                        v_refs[a].at[h, pl.ds(r0, rows)], o_refs[a].at[me, h, pl.ds(r0, rows)],
                        local_sems.at[LOCAL_PIECES * a + p]))
            return out

        return dict(
            first=lambda: [copy(a, k, me, c, (cx, cy, c), src=v_refs[a].at[c]) for a, k, cx, cy in ks],
            landed=lambda: [copy(a, k, 2 * cx + cy, c, (x, y, c)) for a, k, cx, cy in ks],
            passed=lambda: [copy(a, 3 + k, 2 * cx + cy, c, (x, y, 1 - c)) for a, k, cx, cy in ks],
            final=lambda: [copy(a, 3 + k, 2 * cx + cy, 1 - c, (x, y, c)) for a, k, cx, cy in ks],
            local=local)

    def start(self, v_refs, o_refs, sems):
        cps = self._copies(v_refs, o_refs, sems)
        for cp in cps["first"]() + cps["local"]():
            cp.start()

    def mid(self, v_refs, o_refs, sems):
        cps = self._copies(v_refs, o_refs, sems)
        for got, fwd in zip(cps["landed"](), cps["passed"]()):
            got.wait_recv()
            fwd.start()

    def finish(self, v_refs, o_refs, sems):
        cps = self._copies(v_refs, o_refs, sems)
        for cp in cps["final"]():
            cp.wait_recv()
        for cp in cps["first"]() + cps["passed"]():
            cp.wait_send()
        for cp in cps["local"]():
            cp.wait()


class _ExchangeBlocks:
    def __init__(self, ps):
        self.inputs = list(ps)
        n = self.n = len(ps)
        self.out_shape = [_sds((8,) + p.shape[2:], p.dtype) for p in ps]
        self.scratch = [pltpu.SemaphoreType.DMA((7 * n,)), pltpu.SemaphoreType.DMA((7 * n,))]

    def _copies(self, p_refs, o_refs, sems, incoming):
        send_sems, recv_sems = sems
        x, y, c, _ = _place()
        me = 4 * x + 2 * y + c
        out = []
        for a in range(self.n):
            for k in range(1, 8):
                px, py, pc = x ^ (k >> 2), y ^ ((k >> 1) & 1), c ^ (k & 1)
                out.append(pltpu.make_async_remote_copy(
                    src_ref=p_refs[a].at[2 * px + py, pc],
                    dst_ref=o_refs[a].at[4 * px + 2 * py + pc if incoming else me],
                    send_sem=send_sems.at[7 * a + k - 1], recv_sem=recv_sems.at[7 * a + k - 1],
                    device_id=(x, y, c) if incoming else (px, py, pc), device_id_type=MESH))
        return out

    def start(self, p_refs, o_refs, sems):
        for cp in self._copies(p_refs, o_refs, sems, False):
            cp.start()

    def mid(self, p_refs, o_refs, sems):
        pass

    def finish(self, p_refs, o_refs, sems):
        for cp in self._copies(p_refs, o_refs, sems, True):
            cp.wait_recv()
        for cp in self._copies(p_refs, o_refs, sems, False):
            cp.wait_send()


def _run_exchange(plan, *, name):
    n = plan.n

    def body(*refs):
        args = (refs[:n], refs[n:2 * n], refs[2 * n:])
        plan.start(*args)
        plan.mid(*args)
        plan.finish(*args)

    return _pcall(
        body, name=name, in_specs=[ANY] * n, out_specs=[ANY] * n,
        out_shape=plan.out_shape, scratch_shapes=plan.scratch,
    )(*plan.inputs)


class _SendSibling:
    def __init__(self, rs):
        self.inputs = list(rs)
        n = self.n = len(rs)
        self.out_shape = [_sds(r.shape, r.dtype) for r in rs]
        self.scratch = [pltpu.SemaphoreType.DMA((n,)), pltpu.SemaphoreType.DMA((n,))]

    def _copies(self, r_refs, o_refs, sems):
        send_sems, recv_sems = sems
        x, y, c, _ = _place()
        return [pltpu.make_async_remote_copy(
            src_ref=r_refs[a], dst_ref=o_refs[a], send_sem=send_sems.at[a], recv_sem=recv_sems.at[a],
            device_id=(x, y, 1 - c), device_id_type=MESH) for a in range(self.n)]

    def start(self, r_refs, o_refs, sems):
        for cp in self._copies(r_refs, o_refs, sems):
            cp.start()

    def mid(self, r_refs, o_refs, sems):
        pass

    def finish(self, r_refs, o_refs, sems):
        for cp in self._copies(r_refs, o_refs, sems):
            cp.wait()


def _add_devices(p, got, place, *, name):
    _, _, R, C = p.shape
    tr = _tile(R, 256, 16)

    def body(place_ref, p_ref, o_ref, out_ref):
        me = place_ref[2]
        own = p_ref[...].astype(F32)
        acc = jnp.where(me == 0, own, o_ref[0].astype(F32))
        for d in range(1, 8):
            acc = acc + jnp.where(me == d, own, o_ref[d].astype(F32))
        out_ref[...] = acc

    return _pcall(
        body, name=name,
        grid_spec=pltpu.PrefetchScalarGridSpec(
            num_scalar_prefetch=1, grid=(R // tr,),
            in_specs=[pl.BlockSpec((None, None, tr, C), lambda i, pr: (pr[0], pr[1], i, 0)),
                      pl.BlockSpec((8, tr, C), lambda i, pr: (0, i, 0))],
            out_specs=pl.BlockSpec((tr, C), lambda i, pr: (i, 0))),
        out_shape=_sds((R, C), F32),
        compiler_params=_params("parallel"),
    )(place, p, got)


def _round_up(n, m):
    return (n + m - 1) // m * m


def _f32_as_bf16(a):
    return lax.bitcast_convert_type(a.astype(F32), BF16).reshape(-1)


def _bf16_as_f32(a):
    return lax.bitcast_convert_type(a.reshape(-1, 2), F32)


def _by_chip_cols(a, cols):
    lead = a.shape[:-1]
    a = a.reshape(lead + (4, cols))
    return jnp.moveaxis(a, -2, 0).reshape(4, -1)


def kernel(x, meta_tokens, norm_mix, norm_mlp, sb_w_qkv, sb_w_o, lru_w_in, lru_conv_w, lru_conv_b, lru_w_rg, lru_b_rg, lru_w_ig, lru_b_ig, lru_lambda, lru_w_out, mlp_w_up, mlp_w_down, norm_final, loss_target, m_meta_tokens, m_norm_mix, m_norm_mlp, m_sb_w_qkv, m_sb_w_o, m_lru_w_in, m_lru_conv_w, m_lru_conv_b, m_lru_w_rg, m_lru_b_rg, m_lru_w_ig, m_lru_b_ig, m_lru_lambda, m_lru_w_out, m_mlp_w_up, m_mlp_w_down, m_norm_final, v_meta_tokens, v_norm_mix, v_norm_mlp, v_sb_w_qkv, v_sb_w_o, v_lru_w_in, v_lru_conv_w, v_lru_conv_b, v_lru_w_rg, v_lru_b_rg, v_lru_w_ig, v_lru_b_ig, v_lru_lambda, v_lru_w_out, v_mlp_w_up, v_mlp_w_down, v_norm_final):
    weights = dict(meta_tokens=meta_tokens, norm_mix=norm_mix, norm_mlp=norm_mlp, sb_w_qkv=sb_w_qkv,
                   sb_w_o=sb_w_o, lru_w_in=lru_w_in, lru_conv_w=lru_conv_w, lru_conv_b=lru_conv_b,
                   lru_w_rg=lru_w_rg, lru_b_rg=lru_b_rg, lru_w_ig=lru_w_ig, lru_b_ig=lru_b_ig,
                   lru_lambda=lru_lambda, lru_w_out=lru_w_out, mlp_w_up=mlp_w_up, mlp_w_down=mlp_w_down,
                   norm_final=norm_final)
    m_in = dict(meta_tokens=m_meta_tokens, norm_mix=m_norm_mix, norm_mlp=m_norm_mlp, sb_w_qkv=m_sb_w_qkv,
                sb_w_o=m_sb_w_o, lru_w_in=m_lru_w_in, lru_conv_w=m_lru_conv_w, lru_conv_b=m_lru_conv_b,
                lru_w_rg=m_lru_w_rg, lru_b_rg=m_lru_b_rg, lru_w_ig=m_lru_w_ig, lru_b_ig=m_lru_b_ig,
                lru_lambda=m_lru_lambda, lru_w_out=m_lru_w_out, mlp_w_up=m_mlp_w_up,
                mlp_w_down=m_mlp_w_down, norm_final=m_norm_final)
    v_in = dict(meta_tokens=v_meta_tokens, norm_mix=v_norm_mix, norm_mlp=v_norm_mlp, sb_w_qkv=v_sb_w_qkv,
                sb_w_o=v_sb_w_o, lru_w_in=v_lru_w_in, lru_conv_w=v_lru_conv_w, lru_conv_b=v_lru_conv_b,
                lru_w_rg=v_lru_w_rg, lru_b_rg=v_lru_b_rg, lru_w_ig=v_lru_w_ig, lru_b_ig=v_lru_b_ig,
                lru_lambda=v_lru_lambda, lru_w_out=v_lru_w_out, mlp_w_up=v_mlp_w_up,
                mlp_w_down=v_mlp_w_down, norm_final=v_norm_final)
    names = list(weights)

    seq, D = x.shape[1], x.shape[2]
    n_meta = meta_tokens.shape[0]
    Dq = D // 4
    T = _round_up(n_meta + seq, ATT_BLOCK)
    depth = mlp_w_up.shape[0]
    my_x, my_y, my_c = lax.axis_index("x"), lax.axis_index("y"), lax.axis_index("c")
    c_arr = jnp.reshape(my_c, (1,)).astype(jnp.int32)

    assert depth == 2

    def halves(a):
        return a.astype(BF16).reshape(2, a.shape[0] // 2, a.shape[1])

    small = [meta_tokens, lru_conv_w[0], lru_conv_b, lru_b_rg, lru_b_ig, lru_lambda]
    sparts = [_f32_as_bf16(s) for s in small]
    sizes = [p.shape[0] for p in sparts]
    total = _round_up(sum(sizes), 2 * 32 * LANE)
    sflat =jnp.concatenate(sparts + [jnp.zeros((total - sum(sizes),), BF16)]).reshape(2, -1, LANE)
    gq, gsm = _run_exchange(_GatherChips([halves(sb_w_qkv[0]), sflat]), name="gather_first")
    gather_rest = _GatherChips([halves(sb_w_o[0]), halves(lru_w_in[0]), halves(lru_w_out[0]),
                                mlp_w_up.astype(BF16), mlp_w_down.astype(BF16)])
    w_qkv = gq.reshape(4, D, 3 * Dq)
    gsm = gsm.reshape(4, total)
    offs = [sum(sizes[:k]) for k in range(len(sizes))]
    sm = [_bf16_as_f32(gsm[:, o:o + s]) for o, s in zip(offs, sizes)]
    meta_full = jnp.moveaxis(sm[0].reshape(4, n_meta, Dq), 0, 1).reshape(n_meta, D)
    conv_w = jnp.moveaxis(sm[1].reshape(4, 4, Dq), 0, 1).reshape(4, D)
    conv_b, b_rg, b_ig, lam = [s.reshape(1, D) for s in sm[2:6]]
    w_rg = lru_w_rg[0].astype(BF16)
    w_ig = lru_w_ig[0].astype(BF16)
    g_mix = [norm_mix[l].reshape(1, D) for l in range(depth)]
    g_mlp = [norm_mlp[l].reshape(1, D) for l in range(depth)]
    g_fin = norm_final.reshape(1, D)

    pad_rows = T - n_meta - seq
    h0 = jnp.concatenate([meta_full, x[0], jnp.zeros((pad_rows, D), F32)], axis=0)
    target = jnp.concatenate([jnp.zeros((n_meta, D), F32), loss_target[0], jnp.zeros((pad_rows, D), F32)], axis=0)

    hn0, qkv = _norm_mm(h0, g_mix[0], w_qkv, out_dtype=BF16, name="qkv_proj")
    att, w_sv, s_sv, go, gi, gout, w_up, w_down = _attn_fwd(qkv, name="attn_fwd", plan=gather_rest)
    w_o = go.reshape(D, D)
    w_in = gi.reshape(4, D, 2 * Dq)
    w_out = gout.reshape(D, D)
    h1 = _mm_res(att, w_o, h0, name="attn_out")
    h2, hnm0, up0 = _mlp_fwd(h1, g_mlp[0], w_up, w_down, layer=0, name="mlp0_fwd")
    hn1, gr = _norm_mm(h2, g_mix[1], w_in, out_dtype=F32, name="lru_in")
    a_t, b_t = _lru_pre(gr, conv_w, conv_b, w_rg, b_rg, w_ig, b_ig, lam, name="lru_pre")
    hs = _lru_scan(a_t, b_t, reverse=False, name="lru_scan")
    y, h3 = _lru_out(gr, hs, w_out, h2, name="lru_out")
    h4, hnm1, up1 = _mlp_fwd(h3, g_mlp[1], w_up, w_down, layer=1, name="mlp1_fwd")
    loss, dh4, dh4b, dg_fin = _loss_head(h4, g_fin, target, row_lo=n_meta, row_hi=n_meta + seq, name="loss_head")

    dup1, dh3, dh3b, dg_mlp1 = _mlp_bwd(dh4, h3, g_mlp[1], up1, w_up, w_down, layer=1, name="mlp1_bwd")
    dw_up = _mm_tn(hnm1, dup1, shards=4, relu2=False, slot=1, name="mlp1_dwup")
    dw_down = _mm_tn(up1, dh4b, shards=1, relu2=True, slot=1, row_shards=4, name="mlp1_dwdown")
    dw_out = _mm_tn(y, dh3b, shards=1, relu2=False, name="lru_dwout")
    dgr, dhy = _lru_out_bwd(gr, hs, dh3b, w_out, name="lru_out_bwd")
    lmb = _lru_scan(a_t, dhy, reverse=True, name="lru_scan_bwd")
    du, dw_rg, db_rg, dw_ig, db_ig, dlam = _lru_gate_bwd(
        gr, hs, lmb, conv_w, conv_b, w_rg, b_rg, w_ig, b_ig, lam, name="lru_gate_bwd")
    dgr, dconv_w, dconv_b = _lru_conv_bwd(gr, du, conv_w, dgr, name="lru_conv_bwd")
    dh2, dh2b, dg_mix1 = _mm_nt_normbwd(dgr, w_in, h2, g_mix[1], dh3, name="lru_in_bwd")
    dw_in = _mm_tn(hn1, dgr, shards=4, relu2=False, name="lru_dwin")
    dup0, dh1, dh1b, dg_mlp0 = _mlp_bwd(dh2, h1, g_mlp[0], up0, w_up, w_down, layer=0, name="mlp0_bwd")
    dw_up = _mm_tn(hnm0, dup0, shards=4, relu2=False, slot=0, into=dw_up, name="mlp0_dwup")
    dw_down = _mm_tn(up0, dh2b, shards=1, relu2=True, slot=0, into=dw_down, row_shards=4, name="mlp0_dwdown")
    datt = _mm_nt(dh1b, w_o, out_dtype=BF16, name="attn_out_bwd")
    dw_o = _mm_tn(att, dh1b, shards=1, relu2=False, name="attn_dwo")
    def halves_of(d, rows):
        return d.reshape(4, 2, rows // 2, d.shape[-1])

    early = [halves_of(dw_o, Dq), halves_of(dw_in, D), halves_of(dw_out, Dq), dw_up, dw_down]
    dq, dk, dv, *got_early = _attn_bwd(qkv, datt, w_sv, s_sv, name="attn_bwd", plan=_ExchangeBlocks(early))
    dqkv = jnp.concatenate([dq, dk, dv], axis=1)
    large = ["sb_w_o", "lru_w_in", "lru_w_out", "mlp_w_up", "mlp_w_down", "sb_w_qkv"]
    place = jnp.stack([2 * my_x + my_y, my_c, 4 * my_x + 2 * my_y + my_c]).astype(jnp.int32)
    mine = [_add_devices(p, o, place, name="reduce_add_" + t) for p, o, t in zip(early, got_early, large)]
    dw_qkv, *theirs = _mm_tn(hn0, dqkv, shards=4, relu2=False, name="attn_dwqkv", plan=_SendSibling(mine))
    dw_qkv = halves_of(dw_qkv, D)
    dh0, _, dg_mix0, got_qkv = _mm_nt_normbwd(dqkv, w_qkv, h0, g_mix[0], dh1, name="qkv_bwd",
                                               plan=_ExchangeBlocks([dw_qkv]))
    grad_x = dh0[n_meta:n_meta + seq][None]
    dmeta = dh0[:n_meta]

    sharded = [_by_chip_cols(dmeta, Dq), _by_chip_cols(dconv_w, Dq), dconv_b.reshape(4, Dq),
               db_rg.reshape(4, Dq), db_ig.reshape(4, Dq), dlam.reshape(4, Dq)]
    repl = [jnp.concatenate([dg_mix0, dg_mix1], axis=0).reshape(-1),
            jnp.concatenate([dg_mlp0, dg_mlp1], axis=0).reshape(-1),
            dg_fin.reshape(-1), dw_rg.reshape(-1), dw_ig.reshape(-1), loss.reshape(-1)]
    rsizes = [r.shape[0] for r in repl]
    rtotal = _round_up(sum(rsizes), 4 * 2 * 16 * LANE)
    rflat = jnp.concatenate(repl + [jnp.zeros((rtotal - sum(rsizes),), F32)]).reshape(4, rtotal // 4)
    gsizes = [s.shape[1] for s in sharded] + [rtotal // 4]
    gtotal = _round_up(sum(gsizes), 2 * 16 * LANE)
    tail = jnp.concatenate(sharded + [rflat, jnp.zeros((4, gtotal - sum(gsizes)), F32)], axis=1)
    late = [tail.reshape(4, 2, -1, LANE)]
    got_late = _run_exchange(_ExchangeBlocks(late), name="reduce_late")
    mine_late = [_add_devices(p, o, place, name="reduce_add_" + t)
                 for p, o, t in zip([dw_qkv] + late, [got_qkv] + list(got_late), ["sb_w_qkv", "tail"])]
    mine += mine_late
    theirs += _run_exchange(_SendSibling(mine_late), name="reduce_join")

    grads, delta, new_m, new_v = {}, {}, {}, {}
    for n, a, b in zip(large, mine, theirs):
        shp = weights[n].shape
        view = (2,) + a.shape
        g, d, nm, nv = _adamw_halves(weights[n].reshape(view), a, b, m_in[n].reshape(view),
                                     v_in[n].reshape(view), c_arr, name="adamw_" + n)
        grads[n], delta[n], new_m[n], new_v[n] = g.reshape(shp), d.reshape(shp), nm.reshape(shp), nv.reshape(shp)

    lo = jnp.where(my_c == 0, mine[-1], theirs[-1])
    hi = jnp.where(my_c == 0, theirs[-1], mine[-1])
    gshard = jnp.concatenate([lo, hi], axis=0).reshape(gtotal)
    goffs = [sum(gsizes[:k]) for k in range(len(gsizes))]
    gp = [gshard[o:o + s] for o, s in zip(goffs, gsizes)]
    rfull = _run_exchange(_GatherChips([gp[-1].reshape(2, -1, LANE)]), name="gather_replicated")[0].reshape(rtotal)
    roffs = [sum(rsizes[:k]) for k in range(len(rsizes))]
    rp = [rfull[o:o + s] for o, s in zip(roffs, rsizes)]
    grads.update(meta_tokens=gp[0], lru_conv_w=gp[1], lru_conv_b=gp[2], lru_b_rg=gp[3], lru_b_ig=gp[4],
                 lru_lambda=gp[5], norm_mix=rp[0], norm_mlp=rp[1], norm_final=rp[2], lru_w_rg=rp[3],
                 lru_w_ig=rp[4])
    grads = {n: grads[n].reshape(weights[n].shape) for n in names}
    rest = [n for n in names if n not in large]
    ssz = [weights[n].size for n in rest]
    small_cols = 8 * LANE
    stotal = _round_up(sum(ssz), SUBLANE * small_cols)

    def pack(src):
        return jnp.concatenate([src[n].reshape(-1) for n in rest]
                               + [jnp.ones((stotal - sum(ssz),), F32)]).reshape(-1, small_cols)

    d, nm, nv = _adamw(pack(weights), pack(grads), pack(m_in), pack(v_in), name="adamw_small")
    soffs = [sum(ssz[:k]) for k in range(len(ssz))]
    for n, o, s in zip(rest, soffs, ssz):
        shp = weights[n].shape
        delta[n] = d.reshape(-1)[o:o + s].reshape(shp)
        new_m[n] = nm.reshape(-1)[o:o + s].reshape(shp)
        new_v[n] = nv.reshape(-1)[o:o + s].reshape(shp)

    loss = rp[5][0]
    return (loss, grad_x, *[grads[n] for n in names], *[delta[n] for n in names],
            *[new_m[n] for n in names], *[new_v[n] for n in names])
```

```python
import jax
import jax.numpy as jnp
from jax import lax
from jax.experimental import pallas as pl
from jax.experimental.pallas import tpu as pltpu

F32 = jnp.float32
BF16 = jnp.bfloat16
MESH = pl.DeviceIdType.MESH

EPS = 1e-6
HEAD_DIM = 64
LANE = 128
SUBLANE = 8
LRU_C = 8.0
BF16_ROWS = 16
VMEM_LIMIT = 56 * 1024 * 1024
ROWS_WIDE = 1056
ROWS_NARROW = 528

ADAM_LR = 0.001
ADAM_B1 = 0.9
ADAM_B2 = 0.999
ADAM_EPS = 1e-08
ADAM_WD = 0.01
ADAM_STEP = 10


def _pcall(body, **kw):
    return pl.pallas_call(body, **kw)


def _params(*sem):
    return pltpu.CompilerParams(dimension_semantics=sem, vmem_limit_bytes=VMEM_LIMIT)


def _tile(n, pref, align):
    best = None
    for t in range(align, min(n, pref) + 1, align):
        if n % t == 0:
            best = t
    return n if best is None else best


def _sds(shape, dtype):
    return jax.ShapeDtypeStruct(shape, dtype)


def _rstd(x):
    return lax.rsqrt(jnp.mean(x * x, axis=-1, keepdims=True) + EPS)


def _norm_bwd(x, g, dy):
    rstd = _rstd(x)
    n = x * rstd
    dn = dy * g
    dx = rstd * (dn - n * jnp.mean(dn * n, axis=-1, keepdims=True))
    dg = jnp.sum(dy * n, axis=0, keepdims=True)
    return dx, dg


def _softplus_parts(z):
    l1p = jnp.log(1.0 + jnp.exp(-jnp.abs(z)))
    return jnp.maximum(z, 0.0) + l1p, jnp.minimum(z, 0.0) - l1p


def _sigmoid(x):
    return 1.0 / (1.0 + jnp.exp(-x))


def _gelu_parts(x):
    k = 0.7978845608028654
    inner = k * (x + 0.044715 * (x * x * x))
    t = jnp.tanh(inner)
    gelu = 0.5 * x * (1.0 + t)
    dgelu = 0.5 * (1.0 + t) + 0.5 * x * (1.0 - t * t) * (k * (1.0 + 3.0 * 0.044715 * (x * x)))
    return gelu, dgelu


def _norm_mm(h, g, w, *, out_dtype, name):
    T, D = h.shape
    S, _, n = w.shape
    tm = _tile(T, ROWS_WIDE, BF16_ROWS)
    tn = _tile(n, 768, LANE)
    nj = n // tn

    def body(h_ref, g_ref, w_ref, hn_ref, o_ref):
        @pl.when(pl.program_id(1) == 0)
        def _():
            x = h_ref[...]
            hn_ref[...] = (x * _rstd(x) * g_ref[...]).astype(BF16)

        o_ref[...] = jnp.dot(hn_ref[...], w_ref[...], preferred_element_type=F32).astype(out_dtype)

    return _pcall(
        body, name=name, grid=(T // tm, S * nj),
        in_specs=[pl.BlockSpec((tm, D), lambda i, j: (i, 0)),
                  pl.BlockSpec((1, D), lambda i, j: (0, 0)),
                  pl.BlockSpec((None, D, tn), lambda i, j: (j // nj, 0, j % nj))],
        out_specs=[pl.BlockSpec((tm, D), lambda i, j: (i, 0)),
                   pl.BlockSpec((tm, tn), lambda i, j: (i, j))],
        out_shape=[_sds((T, D), BF16), _sds((T, S * n), out_dtype)],
        compiler_params=_params("parallel", "arbitrary"),
    )(h, g, w)


def _mm_res(a, w, res, *, name):
    T, K = a.shape
    N = w.shape[1]
    tm = _tile(T, ROWS_WIDE, BF16_ROWS)

    def body(a_ref, w_ref, r_ref, o_ref):
        o_ref[...] = r_ref[...] + jnp.dot(a_ref[...], w_ref[...], preferred_element_type=F32)

    return _pcall(
        body, name=name, grid=(T // tm,),
        in_specs=[pl.BlockSpec((tm, K), lambda i: (i, 0)),
                  pl.BlockSpec((K, N), lambda i: (0, 0)),
                  pl.BlockSpec((tm, N), lambda i: (i, 0))],
        out_specs=pl.BlockSpec((tm, N), lambda i: (i, 0)),
        out_shape=_sds((T, N), F32),
        compiler_params=_params("parallel"),
    )(a, w, res)


def _mm_nt(a, w, *, out_dtype, name):
    T, N = a.shape
    K = w.shape[0]
    tm = _tile(T, ROWS_WIDE, BF16_ROWS)

    def body(a_ref, w_ref, o_ref):
        o_ref[...] = lax.dot_general(a_ref[...], w_ref[...], (((1,), (1,)), ((), ())),
                                     preferred_element_type=F32).astype(out_dtype)

    return _pcall(
        body, name=name, grid=(T // tm,),
        in_specs=[pl.BlockSpec((tm, N), lambda i: (i, 0)),
                  pl.BlockSpec((K, N), lambda i: (0, 0))],
        out_specs=pl.BlockSpec((tm, K), lambda i: (i, 0)),
        out_shape=_sds((T, K), out_dtype),
        compiler_params=_params("parallel"),
    )(a, w)


def _mm_tn(a, b, *, shards, relu2, name, slot=None, into=None, row_shards=0, out_dtype=BF16, plan=None):
    T, Ka = a.shape
    Nb = b.shape[1]
    n = Nb // shards
    tka = _tile(Ka, 512, LANE)
    tnb = _tile(n, 512, LANE)
    nj = n // tnb

    nx, before, after = _ride_along(plan, Ka // tka, shards * nj)
    n_in = 2 + (into is not None)

    def body(*refs):
        a_ref, b_ref = refs[:2]
        o_ref = refs[n_in + nx]
        ride = (refs[n_in:n_in + nx], refs[n_in + nx + 1:n_in + 2 * nx + 1], refs[n_in + 2 * nx + 1:])
        before(ride)
        av = a_ref[...]
        if relu2:
            r = jnp.maximum(av, 0)
            av = r * r
        o_ref[...] = lax.dot_general(av, b_ref[...], (((0,), (0,)), ((), ())),
                                     preferred_element_type=F32).astype(out_dtype)
        after(ride)

    in_specs = [pl.BlockSpec((T, tka), lambda i, j: (0, i)),
                pl.BlockSpec((T, tnb), lambda i, j: (0, j))]
    args = [a, b]
    aliases = {}
    if slot is None:
        out_spec = pl.BlockSpec((None, tka, tnb), lambda i, j: (j // nj, i, j % nj))
        out_shape = _sds((shards, Ka, n), out_dtype)
    else:
        if row_shards:
            ni = Ka // row_shards // tka
            out_spec = pl.BlockSpec((None, None, tka, tnb), lambda i, j: (i // ni, slot, i % ni, j))
            out_shape = _sds((row_shards, 2, Ka // row_shards, n), out_dtype)
        else:
            out_spec = pl.BlockSpec((None, None, tka, tnb), lambda i, j: (j // nj, slot, i, j % nj))
            out_shape = _sds((shards, 2, Ka, n), out_dtype)
        if into is not None:
            in_specs.append(pl.BlockSpec(memory_space=pl.ANY))
            args.append(into)
            aliases = {2: 0}
    extra = plan.inputs if plan else []
    res = _pcall(
        body, name=name, grid=(Ka // tka, shards * nj),
        in_specs=in_specs + [ANY] * nx, out_specs=[out_spec] + [ANY] * nx,
        out_shape=[out_shape] + (plan.out_shape if plan else []),
        input_output_aliases=aliases,
        scratch_shapes=plan.scratch if plan else [],
        compiler_params=_params("arbitrary", "arbitrary"),
    )(*args, *extra)
    return res if plan else res[0]


def _mm_nt_normbwd(dy, w, h, g, dres, *, name, plan=None):
    T, D = h.shape
    S, _, n = w.shape
    tm = _tile(T, ROWS_WIDE, BF16_ROWS)
    nx, before, after = _ride_along(plan, T // tm, S)

    def body(*refs):
        dy_ref, w_ref, h_ref, g_ref, dr_ref = refs[:5]
        dh_ref, dhb_ref, dg_ref = refs[5 + nx:8 + nx]
        acc_ref = refs[8 + 2 * nx]
        ride = (refs[5:5 + nx], refs[8 + nx:8 + 2 * nx], refs[9 + 2 * nx:])
        before(ride)
        i, s = pl.program_id(0), pl.program_id(1)
        part = lax.dot_general(dy_ref[...], w_ref[...], (((1,), (1,)), ((), ())),
                               preferred_element_type=F32)

        @pl.when(s == 0)
        def _():
            acc_ref[...] = part

        @pl.when(s > 0)
        def _():
            acc_ref[...] += part

        @pl.when(s == S - 1)
        def _():
            dx, dg = _norm_bwd(h_ref[...], g_ref[...], acc_ref[...])
            dh = dr_ref[...] + dx
            dh_ref[...] = dh
            dhb_ref[...] = dh.astype(BF16)

            @pl.when(i == 0)
            def _():
                dg_ref[...] = dg

            @pl.when(i > 0)
            def _():
                dg_ref[...] += dg

        after(ride)

    extra = plan.inputs if plan else []
    return _pcall(
        body, name=name, grid=(T // tm, S),
        in_specs=[pl.BlockSpec((tm, n), lambda i, s: (i, s)),
                  pl.BlockSpec((None, D, n), lambda i, s: (s, 0, 0)),
                  pl.BlockSpec((tm, D), lambda i, s: (i, 0)),
                  pl.BlockSpec((1, D), lambda i, s: (0, 0)),
                  pl.BlockSpec((tm, D), lambda i, s: (i, 0))] + [ANY] * nx,
        out_specs=[pl.BlockSpec((tm, D), lambda i, s: (i, 0)),
                   pl.BlockSpec((tm, D), lambda i, s: (i, 0)),
                   pl.BlockSpec((1, D), lambda i, s: (0, 0))] + [ANY] * nx,
        out_shape=[_sds((T, D), F32), _sds((T, D), BF16), _sds((1, D), F32)] + (plan.out_shape if plan else []),
        scratch_shapes=[pltpu.VMEM((tm, D), F32)] + (plan.scratch if plan else []),
        compiler_params=_params("arbitrary", "arbitrary"),
    )(dy, w, h, g, dres, *extra)


def _row_chains(tm):
    first = (tm // 2 + 15) // 16 * 16
    return [slice(0, first), slice(first, tm)] if 0 < first < tm else [slice(0, tm)]


def _mlp_fwd(h, g, w_up, w_down, *, layer, name):
    T, D = h.shape
    S, _, _, n = w_up.shape
    tm = _tile(T, ROWS_NARROW, BF16_ROWS)
    tf = _tile(n, 1024, LANE)
    nj = n // tf
    nf = S * nj
    chains = _row_chains(tm)

    def body(h_ref, g_ref, wu_ref, wd_ref, o_ref, hn_ref, up_ref, acc_ref):
        f = pl.program_id(1)

        @pl.when(f == 0)
        def _():
            x = h_ref[...]
            hn_ref[...] = (x * _rstd(x) * g_ref[...]).astype(BF16)

        parts = []
        for rows in chains:
            up = jnp.dot(hn_ref[rows, :], wu_ref[...], preferred_element_type=F32)
            up_ref[rows, :] = up.astype(BF16)
            r = jnp.maximum(up, 0.0)
            parts.append(jnp.dot((r * r).astype(BF16), wd_ref[...], preferred_element_type=F32))
        part = jnp.concatenate(parts, axis=0)

        @pl.when(f == 0)
        def _():
            acc_ref[...] = part

        @pl.when(f > 0)
        def _():
            acc_ref[...] += part

        @pl.when(f == nf - 1)
        def _():
            o_ref[...] = h_ref[...] + acc_ref[...]

    return _pcall(
        body, name=name, grid=(T // tm, nf),
        in_specs=[pl.BlockSpec((tm, D), lambda i, f: (i, 0)),
                  pl.BlockSpec((1, D), lambda i, f: (0, 0)),
                  pl.BlockSpec((None, None, D, tf), lambda i, f: (f // nj, layer, 0, f % nj)),
                  pl.BlockSpec((None, None, tf, D), lambda i, f: (f // nj, layer, f % nj, 0))],
        out_specs=[pl.BlockSpec((tm, D), lambda i, f: (i, 0)),
                   pl.BlockSpec((tm, D), lambda i, f: (i, 0)),
                   pl.BlockSpec((tm, tf), lambda i, f: (i, f))],
        out_shape=[_sds((T, D), F32), _sds((T, D), BF16), _sds((T, S * n), BF16)],
        scratch_shapes=[pltpu.VMEM((tm, D), F32)],
        compiler_params=_params("parallel", "arbitrary"),
    )(h, g, w_up, w_down)


def _mlp_bwd(dy, h, g, up, w_up, w_down, *, layer, name):
    T, D = h.shape
    S, _, _, n = w_up.shape
    tm = _tile(T, ROWS_NARROW, BF16_ROWS)
    tf = _tile(n, 1024, LANE)
    nj = n // tf
    nf = S * nj
    chains = _row_chains(tm)

    def body(dy_ref, h_ref, g_ref, up_ref, wu_ref, wd_ref, dup_ref, dh_ref, dhb_ref, dg_ref,
             dyb_ref, acc_ref):
        i, f = pl.program_id(0), pl.program_id(1)

        @pl.when(f == 0)
        def _():
            dyb_ref[...] = dy_ref[...].astype(BF16)

        parts = []
        for rows in chains:
            dact = lax.dot_general(dyb_ref[rows, :], wd_ref[...], (((1,), (1,)), ((), ())),
                                   preferred_element_type=F32)
            r = jnp.maximum(up_ref[rows, :].astype(F32), 0.0)
            dup = (dact * (2.0 * r)).astype(BF16)
            dup_ref[rows, :] = dup
            parts.append(lax.dot_general(dup, wu_ref[...], (((1,), (1,)), ((), ())),
                                         preferred_element_type=F32))
        part = jnp.concatenate(parts, axis=0)

        @pl.when(f == 0)
        def _():
            acc_ref[...] = part

        @pl.when(f > 0)
        def _():
            acc_ref[...] += part

        @pl.when(f == nf - 1)
        def _():
            dx, dg = _norm_bwd(h_ref[...], g_ref[...], acc_ref[...])
            dh = dy_ref[...] + dx
            dh_ref[...] = dh
            dhb_ref[...] = dh.astype(BF16)

            @pl.when(i == 0)
            def _():
                dg_ref[...] = dg

            @pl.when(i > 0)
            def _():
                dg_ref[...] += dg

    return _pcall(
        body, name=name, grid=(T // tm, nf),
        in_specs=[pl.BlockSpec((tm, D), lambda i, f: (i, 0)),
                  pl.BlockSpec((tm, D), lambda i, f: (i, 0)),
                  pl.BlockSpec((1, D), lambda i, f: (0, 0)),
                  pl.BlockSpec((tm, tf), lambda i, f: (i, f)),
                  pl.BlockSpec((None, None, D, tf), lambda i, f: (f // nj, layer, 0, f % nj)),
                  pl.BlockSpec((None, None, tf, D), lambda i, f: (f // nj, layer, f % nj, 0))],
        out_specs=[pl.BlockSpec((tm, tf), lambda i, f: (i, f)),
                   pl.BlockSpec((tm, D), lambda i, f: (i, 0)),
                   pl.BlockSpec((tm, D), lambda i, f: (i, 0)),
                   pl.BlockSpec((1, D), lambda i, f: (0, 0))],
        out_shape=[_sds((T, S * n), BF16), _sds((T, D), F32), _sds((T, D), BF16), _sds((1, D), F32)],
        scratch_shapes=[pltpu.VMEM((tm, D), BF16), pltpu.VMEM((tm, D), F32)],
        compiler_params=_params("arbitrary", "arbitrary"),
    )(dy, h, g, up, w_up, w_down)


ATT_BLOCK = 128
ATT_HEADS = 4


def _attn_tile(T):
    for w in (3 * ATT_BLOCK, 2 * ATT_BLOCK):
        if T % w == 0:
            return w
    return ATT_BLOCK


def _tri(strict_lower, value):
    B = ATT_BLOCK
    r = lax.broadcasted_iota(jnp.int32, (2 * B, B), 0)
    r = jnp.where(r >= B, r - B, r)
    c = lax.broadcasted_iota(jnp.int32, (2 * B, B), 1)
    m = (r > c) if strict_lower else (r < c)
    return jnp.where(m, value, 0.0).astype(BF16)


def _split_dot(x, tri):
    hi = lax.bitcast_convert_type(lax.bitcast_convert_type(x, jnp.uint32) & jnp.uint32(0xFFFF0000), F32)
    lo = x - hi
    return jnp.dot(jnp.concatenate([hi.astype(BF16), lo.astype(BF16)], axis=1), tri,
                   preferred_element_type=F32)


def _causal_mask(W):
    r = lax.broadcasted_iota(jnp.int32, (W, W), 0)
    c = lax.broadcasted_iota(jnp.int32, (W, W), 1)
    return c < r


def _two_heads(x):
    left = lax.broadcasted_iota(jnp.int32, x.shape, 1) < HEAD_DIM
    zero = jnp.zeros_like(x)
    return jnp.concatenate([jnp.where(left, x, zero), jnp.where(left, zero, x)], axis=0)


def _attn_scores(z, carry, tri_neg, masked):
    W = z.shape[0]
    B = ATT_BLOCK
    minus_abs = lax.bitcast_convert_type(
        lax.bitcast_convert_type(z, jnp.uint32) | jnp.uint32(0x80000000), F32)
    sp = jnp.maximum(z, 0.0) + jnp.log(1.0 + jnp.exp(minus_abs))
    logsig = z - sp
    if masked:
        causal = _causal_mask(W)
        sp = jnp.where(causal, sp, 0.0)
    afters = []
    for b in reversed(range(W // B)):
        blk = sp[:, b * B:(b + 1) * B]
        within = _split_dot(blk, tri_neg)
        afters.append(within + carry)
        carry = carry + (within[:, 0:1] - blk[:, 0:1])
    after = jnp.concatenate(afters[::-1], axis=1)
    w = jnp.exp(logsig + after)
    if masked:
        w = jnp.where(causal, w, 0.0)
    return w, jnp.exp(logsig), carry


def _ride_along(plan, npairs, nq):
    if plan is None:
        return 0, (lambda refs: None), (lambda refs: None)
    nx = plan.n

    def before(refs):
        p, i = pl.program_id(0), pl.program_id(1)

        @pl.when(jnp.logical_and(p == 0, i == 0))
        def _():
            plan.start(*refs)

        mid_p, mid_i = (npairs // 2, 0) if npairs > 1 else (0, 3 * nq // 4)

        @pl.when(jnp.logical_and(p == mid_p, i == mid_i))
        def _():
            plan.mid(*refs)

    def after(refs):
        p, i = pl.program_id(0), pl.program_id(1)

        @pl.when(jnp.logical_and(p == npairs - 1, i == nq - 1))
        def _():
            plan.finish(*refs)

    return nx, before, after


def _saved_tile(i, j):
    return i * (i + 1) // 2 + j


def _attn_fwd(qkv, *, name, plan=None):
    T = qkv.shape[0]
    D = qkv.shape[1] // 3
    W = _attn_tile(T)
    H = max(h for h in (ATT_HEADS, 2 * ATT_HEADS, 4 * ATT_HEADS) if D % (h * HEAD_DIM) == 0)
    lanes = H * HEAD_DIM
    ngroups = D // lanes
    nq = T // W
    ntri = nq * (nq + 1) // 2
    scale = HEAD_DIM ** -0.5
    nx, before, after = _ride_along(plan, ngroups, nq)

    def body(*refs):
        q_ref, k_ref, v_ref = refs[:3]
        o_ref, wsv_ref, ssv_ref = refs[3 + nx:6 + nx]
        stage_w, stage_s, stage_sems = refs[6 + 2 * nx:9 + 2 * nx]
        ride = (refs[3:3 + nx], refs[6 + nx:6 + 2 * nx], refs[9 + 2 * nx:])
        before(ride)
        p, i = pl.program_id(0), pl.program_id(1)
        tri = _tri(True, -1.0)
        pairs = [slice(pp * LANE, (pp + 1) * LANE) for pp in range(H // 2)]
        qs = [q_ref[:, cols] * scale for cols in pairs]

        def save(slot, j):
            dst = _saved_tile(i, j)
            return [pltpu.make_async_copy(stage.at[slot], sv.at[pl.ds(p * H, H), dst], stage_sems.at[slot, a])
                    for a, (sv, stage) in enumerate(((wsv_ref, stage_w), (ssv_ref, stage_s)))]

        def tile(t, state, masked):
            j = i - t
            rows = pl.ds(pl.multiple_of(j * W, W), W)
            slot = t % 2
            if not masked:
                @pl.when(t >= 2)
                def _():
                    for cp in save(slot, j):
                        cp.wait()
            out = []
            for pp, (cols, q, (carries, acc)) in enumerate(zip(pairs, qs, state)):
                z = lax.dot_general(q, _two_heads(k_ref[rows, cols]), (((1,), (1,)), ((), ())),
                                    preferred_element_type=F32)
                wbs, new_carries = [], []
                for e, carry in enumerate(carries):
                    w, sig, carry = _attn_scores(z[:, e * W:(e + 1) * W], carry, tri, masked)
                    wb = w.astype(BF16)
                    stage_w[slot, 2 * pp + e] = wb
                    stage_s[slot, 2 * pp + e] = sig.astype(BF16)
                    wbs.append(wb)
                    new_carries.append(carry)
                acc = acc + jnp.dot(jnp.concatenate(wbs, axis=1), _two_heads(v_ref[rows, cols]),
                                    preferred_element_type=F32)
                out.append((tuple(new_carries), acc))
            for cp in save(slot, j):
                cp.start()
            return tuple(out)

        zero = ((jnp.zeros((W, 1), F32),) * 2, jnp.zeros((W, LANE), F32))
        state = tile(0, (zero,) * (H // 2), True)
        state = lax.fori_loop(1, i + 1, lambda t, st: tile(t, st, False), state)
        for cols, (_, acc) in zip(pairs, state):
            o_ref[:, cols] = acc.astype(BF16)
        for cp in save(i % 2, 0):
            cp.wait()

        @pl.when(i >= 1)
        def _():
            for cp in save((i + 1) % 2, 0):
                cp.wait()

        after(ride)

    extra = plan.inputs if plan else []
    saved = _sds((D // HEAD_DIM, ntri, W, W), BF16)
    return _pcall(
        body, name=name, grid=(ngroups, nq),
        in_specs=[pl.BlockSpec((W, lanes), lambda p, i: (i, p)),
                  pl.BlockSpec((T, lanes), lambda p, i: (0, ngroups + p), pipeline_mode=pl.Buffered(1)),
                  pl.BlockSpec((T, lanes), lambda p, i: (0, 2 * ngroups + p), pipeline_mode=pl.Buffered(1))]
        + [ANY] * nx,
        out_specs=[pl.BlockSpec((W, lanes), lambda p, i: (i, p)), ANY, ANY] + [ANY] * nx,
        out_shape=[_sds((T, D), BF16), saved, saved] + (plan.out_shape if plan else []),
        scratch_shapes=[pltpu.VMEM((2, H, W, W), BF16), pltpu.VMEM((2, H, W, W), BF16),
                        pltpu.SemaphoreType.DMA((2, 2))] + (plan.scratch if plan else []),
        compiler_params=_params("arbitrary", "arbitrary"),
    )(qkv, qkv, qkv, *extra)


def _attn_bwd(qkv, do, w_sv, s_sv, *, name, plan=None):
    T = qkv.shape[0]
    D = qkv.shape[1] // 3
    B = ATT_BLOCK
    W = _attn_tile(T)
    H = 2 * ATT_HEADS if D % (2 * ATT_HEADS * HEAD_DIM) == 0 else ATT_HEADS
    lanes = H * HEAD_DIM
    ngroups = D // lanes
    nq = T // W
    scale = HEAD_DIM ** -0.5
    nx, before, after = _ride_along(plan, ngroups, nq)

    def body(*refs):
        q_ref, k_ref, v_ref, do_ref, wsv_ref, ssv_ref = refs[:6]
        dq_ref, dk_ref, dv_ref = refs[6 + nx:9 + nx]
        dk_acc, dv_acc, stage_w, stage_s, stage_sems = refs[9 + 2 * nx:14 + 2 * nx]
        ride = (refs[6:6 + nx], refs[9 + nx:9 + 2 * nx], refs[14 + 2 * nx:])
        before(ride)
        p, i = pl.program_id(0), pl.program_id(1)

        @pl.when(i == 0)
        def _():
            dk_acc[...] = jnp.zeros_like(dk_acc)
            dv_acc[...] = jnp.zeros_like(dv_acc)

        def fetch(slot, j):
            src = _saved_tile(i, j)
            return [pltpu.make_async_copy(sv.at[pl.ds(p * H, H), src], stage.at[slot], stage_sems.at[slot, a])
                    for a, (sv, stage) in enumerate(((wsv_ref, stage_w), (ssv_ref, stage_s)))]

        tri_before = _tri(False, 1.0)
        pairs = [slice(pp * LANE, (pp + 1) * LANE) for pp in range(H // 2)]
        q_two = [_two_heads(q_ref[:, cols] * scale) for cols in pairs]
        do_pair = [do_ref[:, cols] for cols in pairs]
        do_two = [_two_heads(d) for d in do_pair]

        def grad(j, state, masked):
            rows = pl.ds(pl.multiple_of(j * W, W), W)
            slot = j % 2
            for cp in fetch(slot, j):
                cp.wait()
            if not masked:
                for cp in fetch(1 - slot, j + 1):
                    cp.start()
            out = []
            for pp, (cols, (gsums, dq)) in enumerate(zip(pairs, state)):
                dw = lax.dot_general(do_pair[pp], _two_heads(v_ref[rows, cols]), (((1,), (1,)), ((), ())),
                                     preferred_element_type=F32)
                dzs, wbs, new_gsums = [], [], []
                for e, gsum in enumerate(gsums):
                    wb = stage_w[slot, 2 * pp + e]
                    sig = stage_s[slot, 2 * pp + e].astype(F32)
                    g = dw[:, e * W:(e + 1) * W] * wb.astype(F32)
                    befores = []
                    for b in range(W // B):
                        blk = g[:, b * B:(b + 1) * B]
                        within = _split_dot(blk, tri_before)
                        befores.append(within + gsum)
                        gsum = gsum + (within[:, B - 1:B] + blk[:, B - 1:B])
                    dz = g - sig * (g + jnp.concatenate(befores, axis=1))
                    if masked:
                        dz = jnp.where(_causal_mask(W), dz, 0.0)
                    dzs.append(dz.astype(BF16))
                    wbs.append(wb)
                    new_gsums.append(gsum)
                dq = dq + jnp.dot(jnp.concatenate(dzs, axis=1), _two_heads(k_ref[rows, cols] * scale),
                                  preferred_element_type=F32)
                dk_acc[rows, cols] += lax.dot_general(jnp.concatenate(dzs, axis=0), q_two[pp],
                                                      (((0,), (0,)), ((), ())), preferred_element_type=F32)
                dv_acc[rows, cols] += lax.dot_general(jnp.concatenate(wbs, axis=0), do_two[pp],
                                                      (((0,), (0,)), ((), ())), preferred_element_type=F32)
                out.append((tuple(new_gsums), dq))
            return tuple(out)

        for cp in fetch(0, 0):
            cp.start()
        zero = ((jnp.zeros((W, 1), F32),) * 2, jnp.zeros((W, LANE), F32))
        state = lax.fori_loop(0, i, lambda j, st: grad(j, st, False), (zero,) * (H // 2))
        state = grad(i, state, True)
        for cols, (_, dq) in zip(pairs, state):
            dq_ref[:, cols] = dq.astype(BF16)

        @pl.when(i == nq - 1)
        def _():
            dk_ref[...] = dk_acc[...].astype(BF16)
            dv_ref[...] = dv_acc[...].astype(BF16)

        after(ride)

    extra = plan.inputs if plan else []
    return _pcall(
        body, name=name, grid=(ngroups, nq),
        in_specs=[pl.BlockSpec((W, lanes), lambda p, i: (i, p)),
                  pl.BlockSpec((T, lanes), lambda p, i: (0, ngroups + p), pipeline_mode=pl.Buffered(1)),
                  pl.BlockSpec((T, lanes), lambda p, i: (0, 2 * ngroups + p), pipeline_mode=pl.Buffered(1)),
                  pl.BlockSpec((W, lanes), lambda p, i: (i, p)), ANY, ANY] + [ANY] * nx,
        out_specs=[pl.BlockSpec((W, lanes), lambda p, i: (i, p)),
                   pl.BlockSpec((T, lanes), lambda p, i: (0, p), pipeline_mode=pl.Buffered(1)),
                   pl.BlockSpec((T, lanes), lambda p, i: (0, p), pipeline_mode=pl.Buffered(1))] + [ANY] * nx,
        out_shape=[_sds((T, D), BF16)] * 3 + (plan.out_shape if plan else []),
        scratch_shapes=[pltpu.VMEM((T, lanes), F32), pltpu.VMEM((T, lanes), F32),
                        pltpu.VMEM((2, H, W, W), BF16), pltpu.VMEM((2, H, W, W), BF16),
                        pltpu.SemaphoreType.DMA((2, 2))]
        + (plan.scratch if plan else []),
        compiler_params=_params("arbitrary", "arbitrary"),
    )(qkv, qkv, qkv, do, w_sv, s_sv, *extra)


HALO = SUBLANE


def _lru_gates(u, w_rg, b_rg, w_ig, b_ig, lam):
    nb = w_rg.shape[0]
    pre_r, pre_i = [], []
    for n in range(nb):
        ub = u[:, n * LANE:(n + 1) * LANE].astype(BF16)
        pre_r.append(jnp.dot(ub, w_rg[n], preferred_element_type=F32))
        pre_i.append(jnp.dot(ub, w_ig[n], preferred_element_type=F32))
    r = _sigmoid(jnp.concatenate(pre_r, axis=1) + b_rg)
    i = _sigmoid(jnp.concatenate(pre_i, axis=1) + b_ig)
    c = -LRU_C * _softplus_parts(-lam)[0]
    log_a = c * r
    a = jnp.exp(log_a)
    x2 = 2.0 * log_a
    em1 = jnp.where(jnp.abs(x2) < 1e-2, x2 * (1.0 + x2 * (0.5 + x2 * (1.0 / 6.0))), jnp.exp(x2) - 1.0)
    mult = jnp.sqrt(-em1)
    return r, i, a, mult, c


def _conv_rows(buf_ref, tt, conv_w, conv_b):
    u = conv_b
    for j in range(4):
        u = u + buf_ref[pl.ds(HALO - 3 + j, tt), :] * conv_w[j:j + 1, :]
    return u


def _fill_with_halo(buf_ref, prev_ref, cur_ref, first):
    tt = cur_ref.shape[0]
    buf_ref[pl.ds(0, HALO), :] = jnp.where(first, 0.0, prev_ref[...])
    buf_ref[pl.ds(HALO, tt), :] = cur_ref[...]


def _lru_time_tile(T):
    return _tile(T, 256, SUBLANE)


def _lru_pre(gr, conv_w, conv_b, w_rg, b_rg, w_ig, b_ig, lam, *, name):
    T = gr.shape[0]
    D = gr.shape[1] // 2
    tt = _lru_time_tile(T)
    hb = tt // HALO

    def body(x_ref, xp_ref, cw_ref, cb_ref, wr_ref, br_ref, wi_ref, bi_ref, lam_ref, a_ref, b_ref, buf):
        _fill_with_halo(buf, xp_ref, x_ref, pl.program_id(0) == 0)
        u = _conv_rows(buf, tt, cw_ref[...], cb_ref[...])
        _, i, a, mult, _ = _lru_gates(u, wr_ref, br_ref[...], wi_ref, bi_ref[...], lam_ref[...])
        a_ref[...] = a
        b_ref[...] = mult * (i * u)

    vec = pl.BlockSpec((1, D), lambda t: (0, 0))
    mat = pl.BlockSpec(w_rg.shape, lambda t: (0, 0, 0))
    return _pcall(
        body, name=name, grid=(T // tt,),
        in_specs=[pl.BlockSpec((tt, D), lambda t: (t, 1)),
                  pl.BlockSpec((HALO, D), lambda t: (jnp.maximum(t * hb - 1, 0), 1)),
                  pl.BlockSpec((4, D), lambda t: (0, 0)), vec, mat, vec, mat, vec, vec],
        out_specs=[pl.BlockSpec((tt, D), lambda t: (t, 0))] * 2,
        out_shape=[_sds((T, D), F32)] * 2,
        scratch_shapes=[pltpu.VMEM((tt + HALO, D), F32)],
        compiler_params=_params("parallel"),
    )(gr, gr, conv_w, conv_b, w_rg, b_rg, w_ig, b_ig, lam)


def _lru_scan(a, b, *, reverse, name):
    T, D = a.shape
    R = SUBLANE
    ts = _tile(T, 528, R)
    nt = T // ts

    def body(a_ref, b_ref, o_ref, carry):
        @pl.when(pl.program_id(0) == 0)
        def _():
            carry[...] = jnp.zeros_like(carry)

        rowid = lax.broadcasted_iota(jnp.int32, (R, D), 0)

        def chunk(k, run):
            if reverse:
                k = ts // R - 1 - k
            rows = pl.ds(pl.multiple_of(k * R, R), R)
            at, bt = a_ref[rows, :], b_ref[rows, :]
            out = jnp.zeros((R, D), F32)
            for r in (range(R - 1, -1, -1) if reverse else range(R)):
                if reverse:
                    cand = bt + run
                    nxt = at * cand
                else:
                    cand = at * run + bt
                    nxt = cand
                out = jnp.where(rowid == r, cand, out)
                run = jnp.broadcast_to(nxt[r:r + 1, :], (R, D))
            o_ref[rows, :] = out
            return run

        carry[...] = lax.fori_loop(0, ts // R, chunk, carry[...])

    if reverse:
        spec = pl.BlockSpec((ts, D), lambda t: (nt - 1 - t, 0))
    else:
        spec = pl.BlockSpec((ts, D), lambda t: (t, 0))
    return _pcall(
        body, name=name, grid=(nt,),
        in_specs=[spec, spec], out_specs=spec,
        out_shape=_sds((T, D), F32),
        scratch_shapes=[pltpu.VMEM((R, D), F32)],
        compiler_params=_params("arbitrary"),
    )(a, b)


def _lru_out(gr, hs, w, res, *, name):
    T, D = hs.shape
    tt = _tile(T, ROWS_NARROW, BF16_ROWS)

    def body(g_ref, h_ref, w_ref, r_ref, y_ref, o_ref):
        y = (h_ref[...] * _gelu_parts(g_ref[...])[0]).astype(BF16)
        y_ref[...] = y
        o_ref[...] = r_ref[...] + jnp.dot(y, w_ref[...], preferred_element_type=F32)

    blk = pl.BlockSpec((tt, D), lambda t: (t, 0))
    return _pcall(
        body, name=name, grid=(T // tt,),
        in_specs=[blk, blk, pl.BlockSpec((D, D), lambda t: (0, 0)), blk],
        out_specs=[blk, blk],
        out_shape=[_sds((T, D), BF16), _sds((T, D), F32)],
        compiler_params=_params("parallel"),
    )(gr, hs, w, res)


def _lru_out_bwd(gr, hs, dout, w, *, name):
    T, D = hs.shape
    tt = _tile(T, ROWS_NARROW, BF16_ROWS)

    def body(g_ref, h_ref, d_ref, w_ref, dg_ref, dh_ref):
        dy = lax.dot_general(d_ref[...], w_ref[...], (((1,), (1,)), ((), ())), preferred_element_type=F32)
        gelu, dgelu = _gelu_parts(g_ref[...])
        dg_ref[...] = (dy * h_ref[...] * dgelu).astype(BF16)
        dh_ref[...] = dy * gelu

    spec = pl.BlockSpec((tt, D), lambda t: (t, 0))
    return _pcall(
        body, name=name, grid=(T // tt,),
        in_specs=[spec, spec, spec, pl.BlockSpec((D, D), lambda t: (0, 0))], out_specs=[spec, spec],
        out_shape=[_sds((T, 2 * D), BF16), _sds((T, D), F32)],
        compiler_params=_params("parallel"),
    )(gr, hs, dout, w)


def _lru_gate_bwd(gr, hs, lmb, conv_w, conv_b, w_rg, b_rg, w_ig, b_ig, lam, *, name):
    T, D = hs.shape
    nb = w_rg.shape[0]
    tt = _lru_time_tile(T)
    hb = tt // HALO
    nt = T // tt

    def body(x_ref, xp_ref, h_ref, hp_ref, l_ref, cw_ref, cb_ref, wr_ref, br_ref, wi_ref, bi_ref, lam_ref,
             du_ref, dwr_ref, dbr_ref, dwi_ref, dbi_ref, dlam_ref, xbuf, hbuf):
        t = pl.program_id(0)
        first = t == 0
        _fill_with_halo(xbuf, xp_ref, x_ref, first)
        _fill_with_halo(hbuf, hp_ref, h_ref, first)
        u = _conv_rows(xbuf, tt, cw_ref[...], cb_ref[...])
        lam_v = lam_ref[...]
        r, i, a, mult, c = _lru_gates(u, wr_ref, br_ref[...], wi_ref, bi_ref[...], lam_v)
        l = l_ref[...]
        h_prev = hbuf[pl.ds(HALO - 1, tt), :]
        dlog_a = l * h_prev * a - l * (i * u) * (a * a) / mult
        d_iu = l * mult
        dpre_r = (dlog_a * c) * (r * (1.0 - r))
        dpre_i = (d_iu * u) * (i * (1.0 - i))
        dpr_b = dpre_r.astype(BF16)
        dpi_b = dpre_i.astype(BF16)
        du_parts, dwr, dwi = [], [], []
        for n in range(nb):
            cs = slice(n * LANE, (n + 1) * LANE)
            ub = u[:, cs].astype(BF16)
            du_parts.append(
                lax.dot_general(dpr_b[:, cs], wr_ref[n], (((1,), (1,)), ((), ())), preferred_element_type=F32)
                + lax.dot_general(dpi_b[:, cs], wi_ref[n], (((1,), (1,)), ((), ())), preferred_element_type=F32))
            dwr.append(lax.dot_general(ub, dpr_b[:, cs], (((0,), (0,)), ((), ())), preferred_element_type=F32))
            dwi.append(lax.dot_general(ub, dpi_b[:, cs], (((0,), (0,)), ((), ())), preferred_element_type=F32))
        du_ref[...] = d_iu * i + jnp.concatenate(du_parts, axis=1)
        dbr = jnp.sum(dpre_r, axis=0, keepdims=True)
        dbi = jnp.sum(dpre_i, axis=0, keepdims=True)
        dc = jnp.sum(dlog_a * r, axis=0, keepdims=True)

        @pl.when(first)
        def _():
            for n in range(nb):
                dwr_ref[n] = dwr[n]
                dwi_ref[n] = dwi[n]
            dbr_ref[...] = dbr
            dbi_ref[...] = dbi
            dlam_ref[...] = dc

        @pl.when(t > 0)
        def _():
            for n in range(nb):
                dwr_ref[n] += dwr[n]
                dwi_ref[n] += dwi[n]
            dbr_ref[...] += dbr
            dbi_ref[...] += dbi
            dlam_ref[...] += dc

        @pl.when(t == nt - 1)
        def _():
            dlam_ref[...] = dlam_ref[...] * (LRU_C * _sigmoid(-lam_v))

    vec = pl.BlockSpec((1, D), lambda t: (0, 0))
    mat = pl.BlockSpec(w_rg.shape, lambda t: (0, 0, 0))
    blk = pl.BlockSpec((tt, D), lambda t: (t, 0))
    prev = pl.BlockSpec((HALO, D), lambda t: (jnp.maximum(t * hb - 1, 0), 0))
    return _pcall(
        body, name=name, grid=(nt,),
        in_specs=[pl.BlockSpec((tt, D), lambda t: (t, 1)),
                  pl.BlockSpec((HALO, D), lambda t: (jnp.maximum(t * hb - 1, 0), 1)),
                  blk, prev, blk,
                  pl.BlockSpec((4, D), lambda t: (0, 0)), vec, mat, vec, mat, vec, vec],
        out_specs=[blk, mat, vec, mat, vec, vec],
        out_shape=[_sds((T, D), F32), _sds(w_rg.shape, F32), _sds((1, D), F32),
                   _sds(w_rg.shape, F32), _sds((1, D), F32), _sds((1, D), F32)],
        scratch_shapes=[pltpu.VMEM((tt + HALO, D), F32), pltpu.VMEM((tt + HALO, D), F32)],
        compiler_params=_params("arbitrary"),
    )(gr, gr, hs, hs, lmb, conv_w, conv_b, w_rg, b_rg, w_ig, b_ig, lam)


def _lru_conv_bwd(gr, du, conv_w, dgr, *, name):
    T, D = du.shape
    tt = _lru_time_tile(T)
    hb = tt // HALO
    nt = T // tt

    def body(x_ref, xp_ref, du_ref, dun_ref, cw_ref, _, dx_ref, dcw_ref, dcb_ref, xbuf, dbuf):
        t = pl.program_id(0)
        _fill_with_halo(xbuf, xp_ref, x_ref, t == 0)
        du = du_ref[...]
        dbuf[pl.ds(0, tt), :] = du
        dbuf[pl.ds(tt, HALO), :] = jnp.where(t == nt - 1, 0.0, dun_ref[...])
        cw = cw_ref[...]
        dx = jnp.zeros((tt, D), F32)
        dcw = []
        for j in range(4):
            dx = dx + dbuf[pl.ds(3 - j, tt), :] * cw[j:j + 1, :]
            dcw.append(jnp.sum(du * xbuf[pl.ds(HALO - 3 + j, tt), :], axis=0, keepdims=True))
        dx_ref[...] = dx.astype(BF16)
        dcw = jnp.concatenate(dcw, axis=0)
        dcb = jnp.sum(du, axis=0, keepdims=True)

        @pl.when(t == 0)
        def _():
            dcw_ref[...] = dcw
            dcb_ref[...] = dcb

        @pl.when(t > 0)
        def _():
            dcw_ref[...] += dcw
            dcb_ref[...] += dcb

    blk = pl.BlockSpec((tt, D), lambda t: (t, 0))
    return _pcall(
        body, name=name, grid=(nt,),
        in_specs=[pl.BlockSpec((tt, D), lambda t: (t, 1)),
                  pl.BlockSpec((HALO, D), lambda t: (jnp.maximum(t * hb - 1, 0), 1)),
                  blk,
                  pl.BlockSpec((HALO, D), lambda t: (jnp.minimum((t + 1) * hb, T // HALO - 1), 0)),
                  pl.BlockSpec((4, D), lambda t: (0, 0)), ANY],
        out_specs=[pl.BlockSpec((tt, D), lambda t: (t, 1)),
                   pl.BlockSpec((4, D), lambda t: (0, 0)), pl.BlockSpec((1, D), lambda t: (0, 0))],
        out_shape=[_sds((T, 2 * D), BF16), _sds((4, D), F32), _sds((1, D), F32)],
        input_output_aliases={5: 0},
        scratch_shapes=[pltpu.VMEM((tt + HALO, D), F32), pltpu.VMEM((tt + HALO, D), F32)],
        compiler_params=_params("arbitrary"),
    )(gr, gr, du, du, conv_w, dgr)


def _loss_head(h, g, target, *, row_lo, row_hi, name):
    T, D = h.shape
    tm = _tile(T, ROWS_WIDE, BF16_ROWS)

    def body(h_ref, g_ref, t_ref, loss_ref, dh_ref, dhb_ref, dg_ref):
        i = pl.program_id(0)
        x = h_ref[...]
        g = g_ref[...]
        row = i * tm + lax.broadcasted_iota(jnp.int32, (tm, 1), 0)
        valid = jnp.logical_and(row >= row_lo, row < row_hi)
        rstd = _rstd(x)
        n = x * rstd
        err = jnp.where(valid, n * g - t_ref[...], 0.0)
        part = (0.5 / D) * jnp.sum(jnp.sum(err * err, axis=1, keepdims=True), axis=0, keepdims=True)
        dy = err * (1.0 / D)
        dn = dy * g
        dh = rstd * (dn - n * jnp.mean(dn * n, axis=-1, keepdims=True))
        dh_ref[...] = dh
        dhb_ref[...] = dh.astype(BF16)
        dg = jnp.sum(dy * n, axis=0, keepdims=True)

        @pl.when(i == 0)
        def _():
            loss_ref[...] = part
            dg_ref[...] = dg

        @pl.when(i > 0)
        def _():
            loss_ref[...] += part
            dg_ref[...] += dg

    blk = pl.BlockSpec((tm, D), lambda i: (i, 0))
    vec = pl.BlockSpec((1, D), lambda i: (0, 0))
    return _pcall(
        body, name=name, grid=(T // tm,),
        in_specs=[blk, vec, blk],
        out_specs=[pl.BlockSpec((1, 1), lambda i: (0, 0)), blk, blk, vec],
        out_shape=[_sds((1, 1), F32), _sds((T, D), F32), _sds((T, D), BF16), _sds((1, D), F32)],
        compiler_params=_params("arbitrary"),
    )(h, g, target)


def _adamw_math(w, g, m, v):
    c1 = 1.0 / (1.0 - ADAM_B1 ** ADAM_STEP)
    c2 = 1.0 / (1.0 - ADAM_B2 ** ADAM_STEP)
    m = ADAM_B1 * m + (1.0 - ADAM_B1) * g
    v = ADAM_B2 * v + (1.0 - ADAM_B2) * (g * g)
    delta = -ADAM_LR * ((m * c1) / (jnp.sqrt(v * c2) + ADAM_EPS) + ADAM_WD * w)
    return delta, m, v


def _adamw_halves(w, mine, theirs, m, v, c, *, name):
    _, R, C = w.shape
    tr = _tile(R, 256, SUBLANE)

    def body(c_ref, w_ref, a_ref, b_ref, m_ref, v_ref, g_ref, d_ref, nm_ref, nv_ref):
        g = jnp.where(pl.program_id(0) == c_ref[0], a_ref[...], b_ref[...])
        g_ref[...] = g
        d_ref[...], nm_ref[...], nv_ref[...] = _adamw_math(w_ref[...], g, m_ref[...], v_ref[...])

    full = pl.BlockSpec((None, tr, C), lambda h, i, c_ref: (h, i, 0))
    half = pl.BlockSpec((tr, C), lambda h, i, c_ref: (i, 0))
    return _pcall(
        body, name=name,
        grid_spec=pltpu.PrefetchScalarGridSpec(
            num_scalar_prefetch=1, grid=(2, R // tr),
            in_specs=[full, half, half, full, full], out_specs=[full] * 4),
        out_shape=[_sds((2, R, C), F32)] * 4,
        compiler_params=_params("parallel", "parallel"),
    )(c, w, mine, theirs, m, v)


def _adamw(w, g, m, v, *, name):
    R, C = w.shape
    tr = _tile(R, 512, SUBLANE)

    def body(w_ref, g_ref, m_ref, v_ref, d_ref, nm_ref, nv_ref):
        d_ref[...], nm_ref[...], nv_ref[...] = _adamw_math(w_ref[...], g_ref[...], m_ref[...], v_ref[...])

    blk = pl.BlockSpec((tr, C), lambda i: (i, 0))
    return _pcall(
        body, name=name, grid=(R // tr,),
        in_specs=[blk] * 4, out_specs=[blk] * 3,
        out_shape=[_sds((R, C), F32)] * 3,
        compiler_params=_params("parallel"),
    )(w, g, m, v)


ANY = pl.BlockSpec(memory_space=pl.ANY)


def _place():
    x, y, c = lax.axis_index("x"), lax.axis_index("y"), lax.axis_index("c")
    chips = [(1 - x, y), (x, 1 - y), (1 - x, 1 - y)]
    return x, y, c, chips


LOCAL_PIECES = 4


class _GatherChips:
    def __init__(self, vs):
        self.inputs = list(vs)
        n = self.n = len(vs)
        self.out_shape = [_sds((4,) + v.shape, v.dtype) for v in vs]
        self.scratch = [pltpu.SemaphoreType.DMA((6 * n,)), pltpu.SemaphoreType.DMA((6 * n,)),
                        pltpu.SemaphoreType.DMA((LOCAL_PIECES * n,))]

    def _copies(self, v_refs, o_refs, sems):
        send_sems, recv_sems, local_sems = sems
        x, y, c, chips = _place()
        me = 2 * x + y

        def copy(a, k, block, half, to, src=None):
            dst = o_refs[a].at[block, half]
            return pltpu.make_async_remote_copy(
                src_ref=dst if src is None else src, dst_ref=dst,
                send_sem=send_sems.at[6 * a + k], recv_sem=recv_sems.at[6 * a + k],
                device_id=to, device_id_type=MESH)

        ks = [(a, k, cx, cy) for a in range(self.n) for k, (cx, cy) in enumerate(chips)]

        def local():
            out = []
            for a in range(self.n):
                rows = self.inputs[a].shape[1] // (LOCAL_PIECES // 2)
                for p in range(LOCAL_PIECES):
                    h, r0 = p % 2, (p // 2) * rows
                    out.append(pltpu.make_async_copy(
                        v_refs[a].at[h, pl.ds(r0, rows)], o_refs[a].at[me, h, pl.ds(r0, rows)],
                        local_sems.at[LOCAL_PIECES * a + p]))
            return out

        return dict(
            first=lambda: [copy(a, k, me, c, (cx, cy, c), src=v_refs[a].at[c]) for a, k, cx, cy in ks],
            landed=lambda: [copy(a, k, 2 * cx + cy, c, (x, y, c)) for a, k, cx, cy in ks],
            passed=lambda: [copy(a, 3 + k, 2 * cx + cy, c, (x, y, 1 - c)) for a, k, cx, cy in ks],
            final=lambda: [copy(a, 3 + k, 2 * cx + cy, 1 - c, (x, y, c)) for a, k, cx, cy in ks],
            local=local)

    def start(self, v_refs, o_refs, sems):
        cps = self._copies(v_refs, o_refs, sems)
        for cp in cps["first"]() + cps["local"]():
            cp.start()

    def mid(self, v_refs, o_refs, sems):
        cps = self._copies(v_refs, o_refs, sems)
        for got, fwd in zip(cps["landed"](), cps["passed"]()):
            got.wait_recv()
            fwd.start()

    def finish(self, v_refs, o_refs, sems):
        cps = self._copies(v_refs, o_refs, sems)
        for cp in cps["final"]():
            cp.wait_recv()
        for cp in cps["first"]() + cps["passed"]():
            cp.wait_send()
        for cp in cps["local"]():
            cp.wait()


class _ExchangeBlocks:
    def __init__(self, ps):
        self.inputs = list(ps)
        n = self.n = len(ps)
        self.out_shape = [_sds((8,) + p.shape[2:], p.dtype) for p in ps]
        self.scratch = [pltpu.SemaphoreType.DMA((7 * n,)), pltpu.SemaphoreType.DMA((7 * n,))]

    def _copies(self, p_refs, o_refs, sems, incoming):
        send_sems, recv_sems = sems
        x, y, c, _ = _place()
        me = 4 * x + 2 * y + c
        out = []
        for a in range(self.n):
            for k in range(1, 8):
                px, py, pc = x ^ (k >> 2), y ^ ((k >> 1) & 1), c ^ (k & 1)
                out.append(pltpu.make_async_remote_copy(
                    src_ref=p_refs[a].at[2 * px + py, pc],
                    dst_ref=o_refs[a].at[4 * px + 2 * py + pc if incoming else me],
                    send_sem=send_sems.at[7 * a + k - 1], recv_sem=recv_sems.at[7 * a + k - 1],
                    device_id=(x, y, c) if incoming else (px, py, pc), device_id_type=MESH))
        return out

    def start(self, p_refs, o_refs, sems):
        for cp in self._copies(p_refs, o_refs, sems, False):
            cp.start()

    def mid(self, p_refs, o_refs, sems):
        pass

    def finish(self, p_refs, o_refs, sems):
        for cp in self._copies(p_refs, o_refs, sems, True):
            cp.wait_recv()
        for cp in self._copies(p_refs, o_refs, sems, False):
            cp.wait_send()


def _run_exchange(plan, *, name):
    n = plan.n

    def body(*refs):
        args = (refs[:n], refs[n:2 * n], refs[2 * n:])
        plan.start(*args)
        plan.mid(*args)
        plan.finish(*args)

    return _pcall(
        body, name=name, in_specs=[ANY] * n, out_specs=[ANY] * n,
        out_shape=plan.out_shape, scratch_shapes=plan.scratch,
    )(*plan.inputs)


class _SendSibling:
    def __init__(self, rs):
        self.inputs = list(rs)
        n = self.n = len(rs)
        self.out_shape = [_sds(r.shape, r.dtype) for r in rs]
        self.scratch = [pltpu.SemaphoreType.DMA((n,)), pltpu.SemaphoreType.DMA((n,))]

    def _copies(self, r_refs, o_refs, sems):
        send_sems, recv_sems = sems
        x, y, c, _ = _place()
        return [pltpu.make_async_remote_copy(
            src_ref=r_refs[a], dst_ref=o_refs[a], send_sem=send_sems.at[a], recv_sem=recv_sems.at[a],
            device_id=(x, y, 1 - c), device_id_type=MESH) for a in range(self.n)]

    def start(self, r_refs, o_refs, sems):
        for cp in self._copies(r_refs, o_refs, sems):
            cp.start()

    def mid(self, r_refs, o_refs, sems):
        pass

    def finish(self, r_refs, o_refs, sems):
        for cp in self._copies(r_refs, o_refs, sems):
            cp.wait()


def _add_devices(p, got, place, *, name):
    _, _, R, C = p.shape
    tr = _tile(R, 256, 16)

    def body(place_ref, p_ref, o_ref, out_ref):
        me = place_ref[2]
        own = p_ref[...].astype(F32)
        acc = jnp.where(me == 0, own, o_ref[0].astype(F32))
        for d in range(1, 8):
            acc = acc + jnp.where(me == d, own, o_ref[d].astype(F32))
        out_ref[...] = acc

    return _pcall(
        body, name=name,
        grid_spec=pltpu.PrefetchScalarGridSpec(
            num_scalar_prefetch=1, grid=(R // tr,),
            in_specs=[pl.BlockSpec((None, None, tr, C), lambda i, pr: (pr[0], pr[1], i, 0)),
                      pl.BlockSpec((8, tr, C), lambda i, pr: (0, i, 0))],
            out_specs=pl.BlockSpec((tr, C), lambda i, pr: (i, 0))),
        out_shape=_sds((R, C), F32),
        compiler_params=_params("parallel"),
    )(place, p, got)


def _round_up(n, m):
    return (n + m - 1) // m * m


def _f32_as_bf16(a):
    return lax.bitcast_convert_type(a.astype(F32), BF16).reshape(-1)


def _bf16_as_f32(a):
    return lax.bitcast_convert_type(a.reshape(-1, 2), F32)


def _by_chip_cols(a, cols):
    lead = a.shape[:-1]
    a = a.reshape(lead + (4, cols))
    return jnp.moveaxis(a, -2, 0).reshape(4, -1)


def kernel(x, meta_tokens, norm_mix, norm_mlp, sb_w_qkv, sb_w_o, lru_w_in, lru_conv_w, lru_conv_b, lru_w_rg, lru_b_rg, lru_w_ig, lru_b_ig, lru_lambda, lru_w_out, mlp_w_up, mlp_w_down, norm_final, loss_target, m_meta_tokens, m_norm_mix, m_norm_mlp, m_sb_w_qkv, m_sb_w_o, m_lru_w_in, m_lru_conv_w, m_lru_conv_b, m_lru_w_rg, m_lru_b_rg, m_lru_w_ig, m_lru_b_ig, m_lru_lambda, m_lru_w_out, m_mlp_w_up, m_mlp_w_down, m_norm_final, v_meta_tokens, v_norm_mix, v_norm_mlp, v_sb_w_qkv, v_sb_w_o, v_lru_w_in, v_lru_conv_w, v_lru_conv_b, v_lru_w_rg, v_lru_b_rg, v_lru_w_ig, v_lru_b_ig, v_lru_lambda, v_lru_w_out, v_mlp_w_up, v_mlp_w_down, v_norm_final):
    weights = dict(meta_tokens=meta_tokens, norm_mix=norm_mix, norm_mlp=norm_mlp, sb_w_qkv=sb_w_qkv,
                   sb_w_o=sb_w_o, lru_w_in=lru_w_in, lru_conv_w=lru_conv_w, lru_conv_b=lru_conv_b,
                   lru_w_rg=lru_w_rg, lru_b_rg=lru_b_rg, lru_w_ig=lru_w_ig, lru_b_ig=lru_b_ig,
                   lru_lambda=lru_lambda, lru_w_out=lru_w_out, mlp_w_up=mlp_w_up, mlp_w_down=mlp_w_down,
                   norm_final=norm_final)
    m_in = dict(meta_tokens=m_meta_tokens, norm_mix=m_norm_mix, norm_mlp=m_norm_mlp, sb_w_qkv=m_sb_w_qkv,
                sb_w_o=m_sb_w_o, lru_w_in=m_lru_w_in, lru_conv_w=m_lru_conv_w, lru_conv_b=m_lru_conv_b,
                lru_w_rg=m_lru_w_rg, lru_b_rg=m_lru_b_rg, lru_w_ig=m_lru_w_ig, lru_b_ig=m_lru_b_ig,
                lru_lambda=m_lru_lambda, lru_w_out=m_lru_w_out, mlp_w_up=m_mlp_w_up,
                mlp_w_down=m_mlp_w_down, norm_final=m_norm_final)
    v_in = dict(meta_tokens=v_meta_tokens, norm_mix=v_norm_mix, norm_mlp=v_norm_mlp, sb_w_qkv=v_sb_w_qkv,
                sb_w_o=v_sb_w_o, lru_w_in=v_lru_w_in, lru_conv_w=v_lru_conv_w, lru_conv_b=v_lru_conv_b,
                lru_w_rg=v_lru_w_rg, lru_b_rg=v_lru_b_rg, lru_w_ig=v_lru_w_ig, lru_b_ig=v_lru_b_ig,
                lru_lambda=v_lru_lambda, lru_w_out=v_lru_w_out, mlp_w_up=v_mlp_w_up,
                mlp_w_down=v_mlp_w_down, norm_final=v_norm_final)
    names = list(weights)

    seq, D = x.shape[1], x.shape[2]
    n_meta = meta_tokens.shape[0]
    Dq = D // 4
    T = _round_up(n_meta + seq, ATT_BLOCK)
    depth = mlp_w_up.shape[0]
    my_x, my_y, my_c = lax.axis_index("x"), lax.axis_index("y"), lax.axis_index("c")
    c_arr = jnp.reshape(my_c, (1,)).astype(jnp.int32)

    assert depth == 2

    def halves(a):
        return a.astype(BF16).reshape(2, a.shape[0] // 2, a.shape[1])

    small = [meta_tokens, lru_conv_w[0], lru_conv_b, lru_b_rg, lru_b_ig, lru_lambda]
    sparts = [_f32_as_bf16(s) for s in small]
    sizes = [p.shape[0] for p in sparts]
    total = _round_up(sum(sizes), 2 * 32 * LANE)
    sflat =jnp.concatenate(sparts + [jnp.zeros((total - sum(sizes),), BF16)]).reshape(2, -1, LANE)
    gq, gsm = _run_exchange(_GatherChips([halves(sb_w_qkv[0]), sflat]), name="gather_first")
    gather_rest = _GatherChips([halves(sb_w_o[0]), halves(lru_w_in[0]), halves(lru_w_out[0]),
                                mlp_w_up.astype(BF16), mlp_w_down.astype(BF16)])
    w_qkv = gq.reshape(4, D, 3 * Dq)
    gsm = gsm.reshape(4, total)
    offs = [sum(sizes[:k]) for k in range(len(sizes))]
    sm = [_bf16_as_f32(gsm[:, o:o + s]) for o, s in zip(offs, sizes)]
    meta_full = jnp.moveaxis(sm[0].reshape(4, n_meta, Dq), 0, 1).reshape(n_meta, D)
    conv_w = jnp.moveaxis(sm[1].reshape(4, 4, Dq), 0, 1).reshape(4, D)
    conv_b, b_rg, b_ig, lam = [s.reshape(1, D) for s in sm[2:6]]
    w_rg = lru_w_rg[0].astype(BF16)
    w_ig = lru_w_ig[0].astype(BF16)
    g_mix = [norm_mix[l].reshape(1, D) for l in range(depth)]
    g_mlp = [norm_mlp[l].reshape(1, D) for l in range(depth)]
    g_fin = norm_final.reshape(1, D)

    pad_rows = T - n_meta - seq
    h0 = jnp.concatenate([meta_full, x[0], jnp.zeros((pad_rows, D), F32)], axis=0)
    target = jnp.concatenate([jnp.zeros((n_meta, D), F32), loss_target[0], jnp.zeros((pad_rows, D), F32)], axis=0)

    hn0, qkv = _norm_mm(h0, g_mix[0], w_qkv, out_dtype=BF16, name="qkv_proj")
    att, w_sv, s_sv, go, gi, gout, w_up, w_down = _attn_fwd(qkv, name="attn_fwd", plan=gather_rest)
    w_o = go.reshape(D, D)
    w_in = gi.reshape(4, D, 2 * Dq)
    w_out = gout.reshape(D, D)
    h1 = _mm_res(att, w_o, h0, name="attn_out")
    h2, hnm0, up0 = _mlp_fwd(h1, g_mlp[0], w_up, w_down, layer=0, name="mlp0_fwd")
    hn1, gr = _norm_mm(h2, g_mix[1], w_in, out_dtype=F32, name="lru_in")
    a_t, b_t = _lru_pre(gr, conv_w, conv_b, w_rg, b_rg, w_ig, b_ig, lam, name="lru_pre")
    hs = _lru_scan(a_t, b_t, reverse=False, name="lru_scan")
    y, h3 = _lru_out(gr, hs, w_out, h2, name="lru_out")
    h4, hnm1, up1 = _mlp_fwd(h3, g_mlp[1], w_up, w_down, layer=1, name="mlp1_fwd")
    loss, dh4, dh4b, dg_fin = _loss_head(h4, g_fin, target, row_lo=n_meta, row_hi=n_meta + seq, name="loss_head")

    dup1, dh3, dh3b, dg_mlp1 = _mlp_bwd(dh4, h3, g_mlp[1], up1, w_up, w_down, layer=1, name="mlp1_bwd")
    dw_up = _mm_tn(hnm1, dup1, shards=4, relu2=False, slot=1, name="mlp1_dwup")
    dw_down = _mm_tn(up1, dh4b, shards=1, relu2=True, slot=1, row_shards=4, name="mlp1_dwdown")
    dw_out = _mm_tn(y, dh3b, shards=1, relu2=False, name="lru_dwout")
    dgr, dhy = _lru_out_bwd(gr, hs, dh3b, w_out, name="lru_out_bwd")
    lmb = _lru_scan(a_t, dhy, reverse=True, name="lru_scan_bwd")
    du, dw_rg, db_rg, dw_ig, db_ig, dlam = _lru_gate_bwd(
        gr, hs, lmb, conv_w, conv_b, w_rg, b_rg, w_ig, b_ig, lam, name="lru_gate_bwd")
    dgr, dconv_w, dconv_b = _lru_conv_bwd(gr, du, conv_w, dgr, name="lru_conv_bwd")
    dh2, dh2b, dg_mix1 = _mm_nt_normbwd(dgr, w_in, h2, g_mix[1], dh3, name="lru_in_bwd")
    dw_in = _mm_tn(hn1, dgr, shards=4, relu2=False, name="lru_dwin")
    dup0, dh1, dh1b, dg_mlp0 = _mlp_bwd(dh2, h1, g_mlp[0], up0, w_up, w_down, layer=0, name="mlp0_bwd")
    dw_up = _mm_tn(hnm0, dup0, shards=4, relu2=False, slot=0, into=dw_up, name="mlp0_dwup")
    dw_down = _mm_tn(up0, dh2b, shards=1, relu2=True, slot=0, into=dw_down, row_shards=4, name="mlp0_dwdown")
    datt = _mm_nt(dh1b, w_o, out_dtype=BF16, name="attn_out_bwd")
    dw_o = _mm_tn(att, dh1b, shards=1, relu2=False, name="attn_dwo")
    def halves_of(d, rows):
        return d.reshape(4, 2, rows // 2, d.shape[-1])

    early = [halves_of(dw_o, Dq), halves_of(dw_in, D), halves_of(dw_out, Dq), dw_up, dw_down]
    dq, dk, dv, *got_early = _attn_bwd(qkv, datt, w_sv, s_sv, name="attn_bwd", plan=_ExchangeBlocks(early))
    dqkv = jnp.concatenate([dq, dk, dv], axis=1)
    large = ["sb_w_o", "lru_w_in", "lru_w_out", "mlp_w_up", "mlp_w_down", "sb_w_qkv"]
    place = jnp.stack([2 * my_x + my_y, my_c, 4 * my_x + 2 * my_y + my_c]).astype(jnp.int32)
    mine = [_add_devices(p, o, place, name="reduce_add_" + t) for p, o, t in zip(early, got_early, large)]
    dw_qkv, *theirs = _mm_tn(hn0, dqkv, shards=4, relu2=False, name="attn_dwqkv", plan=_SendSibling(mine))
    dw_qkv = halves_of(dw_qkv, D)
    dh0, _, dg_mix0, got_qkv = _mm_nt_normbwd(dqkv, w_qkv, h0, g_mix[0], dh1, name="qkv_bwd",
                                               plan=_ExchangeBlocks([dw_qkv]))
    grad_x = dh0[n_meta:n_meta + seq][None]
    dmeta = dh0[:n_meta]

    sharded = [_by_chip_cols(dmeta, Dq), _by_chip_cols(dconv_w, Dq), dconv_b.reshape(4, Dq),
               db_rg.reshape(4, Dq), db_ig.reshape(4, Dq), dlam.reshape(4, Dq)]
    repl = [jnp.concatenate([dg_mix0, dg_mix1], axis=0).reshape(-1),
            jnp.concatenate([dg_mlp0, dg_mlp1], axis=0).reshape(-1),
            dg_fin.reshape(-1), dw_rg.reshape(-1), dw_ig.reshape(-1), loss.reshape(-1)]
    rsizes = [r.shape[0] for r in repl]
    rtotal = _round_up(sum(rsizes), 4 * 2 * 16 * LANE)
    rflat = jnp.concatenate(repl + [jnp.zeros((rtotal - sum(rsizes),), F32)]).reshape(4, rtotal // 4)
    gsizes = [s.shape[1] for s in sharded] + [rtotal // 4]
    gtotal = _round_up(sum(gsizes), 2 * 16 * LANE)
    tail = jnp.concatenate(sharded + [rflat, jnp.zeros((4, gtotal - sum(gsizes)), F32)], axis=1)
    late = [tail.reshape(4, 2, -1, LANE)]
    got_late = _run_exchange(_ExchangeBlocks(late), name="reduce_late")
    mine_late = [_add_devices(p, o, place, name="reduce_add_" + t)
                 for p, o, t in zip([dw_qkv] + late, [got_qkv] + list(got_late), ["sb_w_qkv", "tail"])]
    mine += mine_late
    theirs += _run_exchange(_SendSibling(mine_late), name="reduce_join")

    grads, delta, new_m, new_v = {}, {}, {}, {}
    for n, a, b in zip(large, mine, theirs):
        shp = weights[n].shape
        view = (2,) + a.shape
        g, d, nm, nv = _adamw_halves(weights[n].reshape(view), a, b, m_in[n].reshape(view),
                                     v_in[n].reshape(view), c_arr, name="adamw_" + n)
        grads[n], delta[n], new_m[n], new_v[n] = g.reshape(shp), d.reshape(shp), nm.reshape(shp), nv.reshape(shp)

    lo = jnp.where(my_c == 0, mine[-1], theirs[-1])
    hi = jnp.where(my_c == 0, theirs[-1], mine[-1])
    gshard = jnp.concatenate([lo, hi], axis=0).reshape(gtotal)
    goffs = [sum(gsizes[:k]) for k in range(len(gsizes))]
    gp = [gshard[o:o + s] for o, s in zip(goffs, gsizes)]
    rfull = _run_exchange(_GatherChips([gp[-1].reshape(2, -1, LANE)]), name="gather_replicated")[0].reshape(rtotal)
    roffs = [sum(rsizes[:k]) for k in range(len(rsizes))]
    rp = [rfull[o:o + s] for o, s in zip(roffs, rsizes)]
    grads.update(meta_tokens=gp[0], lru_conv_w=gp[1], lru_conv_b=gp[2], lru_b_rg=gp[3], lru_b_ig=gp[4],
                 lru_lambda=gp[5], norm_mix=rp[0], norm_mlp=rp[1], norm_final=rp[2], lru_w_rg=rp[3],
                 lru_w_ig=rp[4])
    grads = {n: grads[n].reshape(weights[n].shape) for n in names}
    rest = [n for n in names if n not in large]
    ssz = [weights[n].size for n in rest]
    small_cols = 8 * LANE
    stotal = _round_up(sum(ssz), SUBLANE * small_cols)

    def pack(src):
        return jnp.concatenate([src[n].reshape(-1) for n in rest]
                               + [jnp.ones((stotal - sum(ssz),), F32)]).reshape(-1, small_cols)

    d, nm, nv = _adamw(pack(weights), pack(grads), pack(m_in), pack(v_in), name="adamw_small")
    soffs = [sum(ssz[:k]) for k in range(len(ssz))]
    for n, o, s in zip(rest, soffs, ssz):
        shp = weights[n].shape
        delta[n] = d.reshape(-1)[o:o + s].reshape(shp)
        new_m[n] = nm.reshape(-1)[o:o + s].reshape(shp)
        new_v[n] = nv.reshape(-1)[o:o + s].reshape(shp)

    loss = rp[5][0]
    return (loss, grad_x, *[grads[n] for n in names], *[delta[n] for n in names],
            *[new_m[n] for n in names], *[new_v[n] for n in names])
```
